```python
import math
import jax, jax.numpy as jnp
from jax import lax
import numpy as np

D_MODEL = 1024
BATCH = 8
SEQ = 2048
DEPTH = 2

HEAD_DIM = 64
N_A_LAYERS = DEPTH // 2
N_B_LAYERS = DEPTH - N_A_LAYERS
FOX_HEADS = D_MODEL // HEAD_DIM
SWA_Q_HEADS = D_MODEL // HEAD_DIM
SWA_KV_HEADS = SWA_Q_HEADS // 8
SWA_GROUP = SWA_Q_HEADS // SWA_KV_HEADS
WINDOW = 128
Q_BLOCK = 128
D_FF = 4 * D_MODEL
N_BUCKETS = 32
REL_MAX_DIST = 128
NORM_EPS = 1e-6

kernel_name = "yoco_fox_swa_sink_hybrid"


def _rmsnorm(x, g):
    xf = x.astype(jnp.float32)
    y = xf * lax.rsqrt(jnp.mean(xf * xf, axis=-1, keepdims=True) + NORM_EPS)
    return (y * g.astype(jnp.float32)).astype(x.dtype)


def _t5_causal_bucket(dist):
    n = np.maximum(dist, 0)
    max_exact = N_BUCKETS // 2
    large = max_exact + (np.log(np.maximum(n, 1) / max_exact)
                         / np.log(REL_MAX_DIST / max_exact)
                         * (N_BUCKETS - max_exact)).astype(np.int32)
    large = np.minimum(large, N_BUCKETS - 1)
    return np.where(n < max_exact, n, large).astype(np.int32)


def _sq_relu_mlp(h, w_up, w_down):
    u = h @ w_up
    return (jnp.square(jax.nn.relu(u))) @ w_down


def _fox_attention(h, w_in, b_f, g_q, g_k, w_out):
    bsz, seq, _ = h.shape
    hw = FOX_HEADS * HEAD_DIM
    proj = h @ w_in
    q = _rmsnorm(proj[..., :hw].reshape(bsz, seq, FOX_HEADS, HEAD_DIM), g_q)
    k = _rmsnorm(proj[..., hw:2 * hw].reshape(bsz, seq, FOX_HEADS, HEAD_DIM), g_k)
    v = proj[..., 2 * hw:3 * hw].reshape(bsz, seq, FOX_HEADS, HEAD_DIM)
    log_f = jax.nn.log_sigmoid(proj[..., 3 * hw:].astype(jnp.float32)
                               + b_f.astype(jnp.float32))
    c = jnp.cumsum(log_f, axis=1).transpose(0, 2, 1)
    scale = HEAD_DIM ** -0.5
    outs = []
    for blk in range(seq // Q_BLOCK):
        t0, t1 = blk * Q_BLOCK, (blk + 1) * Q_BLOCK
        s = jnp.einsum('bqhd,bkhd->bhqk', q[:, t0:t1], k[:, :t1]).astype(jnp.float32) * scale
        s = s + c[:, :, t0:t1, None] - c[:, :, None, :t1]
        mask = np.arange(t1)[None, :] <= np.arange(t0, t1)[:, None]
        s = jnp.where(mask, s, -jnp.inf)
        p = jax.nn.softmax(s, axis=-1).astype(v.dtype)
        outs.append(jnp.einsum('bhqk,bkhd->bqhd', p, v[:, :t1]))
    o = jnp.concatenate(outs, axis=1).reshape(bsz, seq, hw)
    return o @ w_out


def _shared_kv(h, g_kv, w_kv, g_k):
    bsz, seq, _ = h.shape
    kvw = SWA_KV_HEADS * HEAD_DIM
    kv = _rmsnorm(h, g_kv) @ w_kv
    k = _rmsnorm(kv[..., :kvw].reshape(bsz, seq, SWA_KV_HEADS, HEAD_DIM), g_k)
    v = kv[..., kvw:].reshape(bsz, seq, SWA_KV_HEADS, HEAD_DIM)
    return k, v


def _band(x):
    bsz, seq = x.shape[:2]
    xb = x.reshape(bsz, seq // WINDOW, WINDOW, *x.shape[2:])
    prev = jnp.pad(xb[:, :-1], ((0, 0), (1, 0), (0, 0), (0, 0), (0, 0)))
    return jnp.concatenate([prev, xb], axis=2)


def _swa_sink_attention(h, w_q, g_q, k_band, v_band, sinks, rel_bias, w_out):
    bsz, seq, _ = h.shape
    nblk = seq // WINDOW
    q = _rmsnorm((h @ w_q).reshape(bsz, seq, SWA_Q_HEADS, HEAD_DIM), g_q)
    qb = q.reshape(bsz, nblk, WINDOW, SWA_KV_HEADS, SWA_GROUP, HEAD_DIM)
    s = jnp.einsum('bnqkgd,bnjkd->bnkgqj', qb, k_band).astype(jnp.float32) * (HEAD_DIM ** -0.5)
    dist = np.arange(WINDOW)[:, None] + WINDOW - np.arange(2 * WINDOW)[None, :]
    bias = rel_bias.astype(jnp.float32)[_t5_causal_bucket(dist)]
    bias = bias.transpose(2, 0, 1).reshape(SWA_KV_HEADS, SWA_GROUP, WINDOW, 2 * WINDOW)
    s = s + bias[None, None]
    valid = (dist >= 0) & (dist < WINDOW)
    blk_ok = (np.arange(nblk)[:, None] > 0) | (np.arange(2 * WINDOW)[None, :] >= WINDOW)
    mask = valid[None, :, :] & blk_ok[:, None, :]
    s = jnp.where(mask[None, :, None, None], s, -jnp.inf)
    sink = jnp.broadcast_to(
        sinks.astype(jnp.float32).reshape(1, 1, SWA_KV_HEADS, SWA_GROUP, 1, 1),
        s.shape[:-1] + (1,))
    p = jax.nn.softmax(jnp.concatenate([s, sink], axis=-1), axis=-1)[..., :-1]
    o = jnp.einsum('bnkgqj,bnjkd->bnqkgd', p.astype(v_band.dtype), v_band)
    return o.reshape(bsz, seq, SWA_Q_HEADS * HEAD_DIM) @ w_out


def _fwd_setup_inputs(seed: int = 0) -> dict:
    key = jax.random.key(seed)
    ks = jax.random.split(key, 20)
    f32 = jnp.float32
    hw = FOX_HEADS * HEAD_DIM
    qw = SWA_Q_HEADS * HEAD_DIM
    kvw = SWA_KV_HEADS * HEAD_DIM

    def nrm(k, shape, fan_in):
        return jax.random.normal(k, shape, f32) * (fan_in ** -0.5)

    def gain(k, shape):
        return 1.0 + 0.05 * jax.random.normal(k, shape, f32)

    return {
        "x": jax.random.normal(ks[0], (BATCH, SEQ, D_MODEL), f32),
        "g_attn": gain(ks[1], (DEPTH, D_MODEL)),
        "g_mlp": gain(ks[2], (DEPTH, D_MODEL)),
        "w_in_a": nrm(ks[3], (N_A_LAYERS, D_MODEL, 3 * hw + FOX_HEADS), D_MODEL),
        "b_f": 3.0 + 0.5 * jax.random.normal(ks[4], (N_A_LAYERS, FOX_HEADS), f32),
        "gq_a": gain(ks[5], (N_A_LAYERS, HEAD_DIM)),
        "gk_a": gain(ks[6], (N_A_LAYERS, HEAD_DIM)),
        "w_out_a": nrm(ks[7], (N_A_LAYERS, hw, D_MODEL), hw),
        "g_kv": gain(ks[8], (D_MODEL,)),
        "w_kv": nrm(ks[9], (D_MODEL, 2 * kvw), D_MODEL),
        "gk_b": gain(ks[10], (HEAD_DIM,)),
        "w_q_b": nrm(ks[11], (N_B_LAYERS, D_MODEL, qw), D_MODEL),
        "gq_b": gain(ks[12], (N_B_LAYERS, HEAD_DIM)),
        "sinks": 0.5 * jax.random.normal(ks[13], (N_B_LAYERS, SWA_Q_HEADS), f32),
        "rel_bias": 0.2 * jax.random.normal(ks[14], (N_BUCKETS, SWA_Q_HEADS), f32),
        "w_out_b": nrm(ks[15], (N_B_LAYERS, qw, D_MODEL), qw),
        "w_up": nrm(ks[16], (DEPTH, D_MODEL, D_FF), D_MODEL),
        "w_down": nrm(ks[17], (DEPTH, D_FF, D_MODEL), D_FF),
    }


def _fwd_reference(x, g_attn, g_mlp, w_in_a, b_f, gq_a, gk_a, w_out_a, g_kv, w_kv, gk_b,
              w_q_b, gq_b, sinks, rel_bias, w_out_b, w_up, w_down):
    h = x
    k_band = None
    v_band = None
    for layer in range(DEPTH):
        if layer < N_A_LAYERS:
            a = layer
            h = h + _fox_attention(_rmsnorm(h, g_attn[layer]), w_in_a[a], b_f[a],
                                   gq_a[a], gk_a[a], w_out_a[a])
        else:
            b = layer - N_A_LAYERS
            if b == 0:
                k_sh, v_sh = _shared_kv(h, g_kv, w_kv, gk_b)
                k_band, v_band = _band(k_sh), _band(v_sh)
            h = h + _swa_sink_attention(_rmsnorm(h, g_attn[layer]), w_q_b[b], gq_b[b],
                                        k_band, v_band, sinks[b], rel_bias, w_out_b[b])
        h = h + _sq_relu_mlp(_rmsnorm(h, g_mlp[layer]), w_up[layer], w_down[layer])
    return h


import jax as _jax
import jax.numpy as _jnp

TWIN_FORMAT = 'train_step'
FWD_PARAMS = ['x', 'g_attn', 'g_mlp', 'w_in_a', 'b_f', 'gq_a', 'gk_a', 'w_out_a', 'g_kv', 'w_kv', 'gk_b', 'w_q_b', 'gq_b', 'sinks', 'rel_bias', 'w_out_b', 'w_up', 'w_down']
TWIN_WEIGHTS = ['g_attn', 'g_mlp', 'w_in_a', 'b_f', 'gq_a', 'gk_a', 'w_out_a', 'g_kv', 'w_kv', 'gk_b', 'w_q_b', 'gq_b', 'sinks', 'rel_bias', 'w_out_b', 'w_up', 'w_down']
TWIN_DIFF_INPUT = 'x'
TWIN_INPUTS = ['x', 'g_attn', 'g_mlp', 'w_in_a', 'b_f', 'gq_a', 'gk_a', 'w_out_a', 'g_kv', 'w_kv', 'gk_b', 'w_q_b', 'gq_b', 'sinks', 'rel_bias', 'w_out_b', 'w_up', 'w_down', 'loss_target', 'm_g_attn', 'm_g_mlp', 'm_w_in_a', 'm_b_f', 'm_gq_a', 'm_gk_a', 'm_w_out_a', 'm_g_kv', 'm_w_kv', 'm_gk_b', 'm_w_q_b', 'm_gq_b', 'm_sinks', 'm_rel_bias', 'm_w_out_b', 'm_w_up', 'm_w_down', 'v_g_attn', 'v_g_mlp', 'v_w_in_a', 'v_b_f', 'v_gq_a', 'v_gk_a', 'v_w_out_a', 'v_g_kv', 'v_w_kv', 'v_gk_b', 'v_w_q_b', 'v_gq_b', 'v_sinks', 'v_rel_bias', 'v_w_out_b', 'v_w_up', 'v_w_down']
TWIN_OUTPUTS = ['loss', 'grad_x', 'grad_g_attn', 'grad_g_mlp', 'grad_w_in_a', 'grad_b_f', 'grad_gq_a', 'grad_gk_a', 'grad_w_out_a', 'grad_g_kv', 'grad_w_kv', 'grad_gk_b', 'grad_w_q_b', 'grad_gq_b', 'grad_sinks', 'grad_rel_bias', 'grad_w_out_b', 'grad_w_up', 'grad_w_down', 'delta_g_attn', 'delta_g_mlp', 'delta_w_in_a', 'delta_b_f', 'delta_gq_a', 'delta_gk_a', 'delta_w_out_a', 'delta_g_kv', 'delta_w_kv', 'delta_gk_b', 'delta_w_q_b', 'delta_gq_b', 'delta_sinks', 'delta_rel_bias', 'delta_w_out_b', 'delta_w_up', 'delta_w_down', 'new_m_g_attn', 'new_m_g_mlp', 'new_m_w_in_a', 'new_m_b_f', 'new_m_gq_a', 'new_m_gk_a', 'new_m_w_out_a', 'new_m_g_kv', 'new_m_w_kv', 'new_m_gk_b', 'new_m_w_q_b', 'new_m_gq_b', 'new_m_sinks', 'new_m_rel_bias', 'new_m_w_out_b', 'new_m_w_up', 'new_m_w_down', 'new_v_g_attn', 'new_v_g_mlp', 'new_v_w_in_a', 'new_v_b_f', 'new_v_gq_a', 'new_v_gk_a', 'new_v_w_out_a', 'new_v_g_kv', 'new_v_w_kv', 'new_v_gk_b', 'new_v_w_q_b', 'new_v_gq_b', 'new_v_sinks', 'new_v_rel_bias', 'new_v_w_out_b', 'new_v_w_up', 'new_v_w_down']
TWIN_LEAF_KINDS = {'loss': 'loss', 'grad_x': 'grad_x', 'grad_g_attn': 'grad_w', 'grad_g_mlp': 'grad_w', 'grad_w_in_a': 'grad_w', 'grad_b_f': 'grad_w', 'grad_gq_a': 'grad_w', 'grad_gk_a': 'grad_w', 'grad_w_out_a': 'grad_w', 'grad_g_kv': 'grad_w', 'grad_w_kv': 'grad_w', 'grad_gk_b': 'grad_w', 'grad_w_q_b': 'grad_w', 'grad_gq_b': 'grad_w', 'grad_sinks': 'grad_w', 'grad_rel_bias': 'grad_w', 'grad_w_out_b': 'grad_w', 'grad_w_up': 'grad_w', 'grad_w_down': 'grad_w', 'delta_g_attn': 'delta_w', 'delta_g_mlp': 'delta_w', 'delta_w_in_a': 'delta_w', 'delta_b_f': 'delta_w', 'delta_gq_a': 'delta_w', 'delta_gk_a': 'delta_w', 'delta_w_out_a': 'delta_w', 'delta_g_kv': 'delta_w', 'delta_w_kv': 'delta_w', 'delta_gk_b': 'delta_w', 'delta_w_q_b': 'delta_w', 'delta_gq_b': 'delta_w', 'delta_sinks': 'delta_w', 'delta_rel_bias': 'delta_w', 'delta_w_out_b': 'delta_w', 'delta_w_up': 'delta_w', 'delta_w_down': 'delta_w', 'new_m_g_attn': 'new_m', 'new_m_g_mlp': 'new_m', 'new_m_w_in_a': 'new_m', 'new_m_b_f': 'new_m', 'new_m_gq_a': 'new_m', 'new_m_gk_a': 'new_m', 'new_m_w_out_a': 'new_m', 'new_m_g_kv': 'new_m', 'new_m_w_kv': 'new_m', 'new_m_gk_b': 'new_m', 'new_m_w_q_b': 'new_m', 'new_m_gq_b': 'new_m', 'new_m_sinks': 'new_m', 'new_m_rel_bias': 'new_m', 'new_m_w_out_b': 'new_m', 'new_m_w_up': 'new_m', 'new_m_w_down': 'new_m', 'new_v_g_attn': 'new_v', 'new_v_g_mlp': 'new_v', 'new_v_w_in_a': 'new_v', 'new_v_b_f': 'new_v', 'new_v_gq_a': 'new_v', 'new_v_gk_a': 'new_v', 'new_v_w_out_a': 'new_v', 'new_v_g_kv': 'new_v', 'new_v_w_kv': 'new_v', 'new_v_gk_b': 'new_v', 'new_v_w_q_b': 'new_v', 'new_v_gq_b': 'new_v', 'new_v_sinks': 'new_v', 'new_v_rel_bias': 'new_v', 'new_v_w_out_b': 'new_v', 'new_v_w_up': 'new_v', 'new_v_w_down': 'new_v'}


def _forward(args):
    return _fwd_reference(*[args[k] for k in FWD_PARAMS])


def _output_shape():
    out = _jax.eval_shape(lambda: _forward(_fwd_setup_inputs(0)))
    return out.shape, out.dtype

N_MICROBATCH = 1
ADAM_LR = 0.001
ADAM_B1 = 0.9
ADAM_B2 = 0.999
ADAM_EPS = 1e-08
ADAM_WD = 0.01
ADAM_STEP = 10
PER_EXAMPLE_BATCH_AXIS = {'x': 0, 'loss_target': 0}
SHARED_INPUTS = []
_WEIGHT_DTYPES = {'g_attn': _jnp.float32, 'g_mlp': _jnp.float32, 'w_in_a': _jnp.float32, 'b_f': _jnp.float32, 'gq_a': _jnp.float32, 'gk_a': _jnp.float32, 'w_out_a': _jnp.float32, 'g_kv': _jnp.float32, 'w_kv': _jnp.float32, 'gk_b': _jnp.float32, 'w_q_b': _jnp.float32, 'gq_b': _jnp.float32, 'sinks': _jnp.float32, 'rel_bias': _jnp.float32, 'w_out_b': _jnp.float32, 'w_up': _jnp.float32, 'w_down': _jnp.float32}
MOMENT_SCALE = {'g_attn': 1.163177e+00, 'g_mlp': 5.028462e+01, 'w_in_a': 3.130715e-01, 'b_f': 3.168940e+01, 'gq_a': 1.165912e+01, 'gk_a': 1.180805e+01, 'w_out_a': 4.182951e-01, 'g_kv': 6.925380e+00, 'w_kv': 1.289870e+01, 'gk_b': 2.946502e+00, 'w_q_b': 1.407096e-01, 'gq_b': 2.980665e+00, 'sinks': 1.019267e+00, 'rel_bias': 1.177439e-01, 'w_out_b': 6.404418e+00, 'w_up': 2.530659e+00, 'w_down': 1.005833e+01}


def _to_microbatches(a, axis):
    t = _jnp.moveaxis(a, axis, 0)
    t = t.reshape((N_MICROBATCH, t.shape[0] // N_MICROBATCH) + t.shape[1:])
    return _jnp.moveaxis(t, 1, axis + 1)


def setup_inputs(seed: int = 0) -> dict:
    inp = _fwd_setup_inputs(seed)
    key = _jax.random.fold_in(_jax.random.key(seed), 7919)
    shape, _ = _output_shape()
    out = dict(inp)
    out["loss_target"] = _jax.random.normal(_jax.random.fold_in(key, 0), shape, _jnp.float32)
    for i, name in enumerate(TWIN_WEIGHTS):
        w = inp[name].astype(_jnp.float32)
        if MOMENT_SCALE is None:
            s = _jnp.sqrt(_jnp.mean(_jnp.square(w)) + 1e-30)
        else:
            s = MOMENT_SCALE[name]
        km, kv = _jax.random.split(_jax.random.fold_in(key, i + 1))
        out[name] = w
        out["m_" + name] = s * _jax.random.normal(km, w.shape, _jnp.float32)
        out["v_" + name] = (s * s) * _jax.random.uniform(kv, w.shape, _jnp.float32, 0.5, 1.5)
    if N_MICROBATCH > 1:
        for name, axis in PER_EXAMPLE_BATCH_AXIS.items():
            out[name] = _to_microbatches(out[name], axis)
    return {'x': out['x'], 'g_attn': out['g_attn'], 'g_mlp': out['g_mlp'], 'w_in_a': out['w_in_a'], 'b_f': out['b_f'], 'gq_a': out['gq_a'], 'gk_a': out['gk_a'], 'w_out_a': out['w_out_a'], 'g_kv': out['g_kv'], 'w_kv': out['w_kv'], 'gk_b': out['gk_b'], 'w_q_b': out['w_q_b'], 'gq_b': out['gq_b'], 'sinks': out['sinks'], 'rel_bias': out['rel_bias'], 'w_out_b': out['w_out_b'], 'w_up': out['w_up'], 'w_down': out['w_down'], 'loss_target': out['loss_target'], 'm_g_attn': out['m_g_attn'], 'm_g_mlp': out['m_g_mlp'], 'm_w_in_a': out['m_w_in_a'], 'm_b_f': out['m_b_f'], 'm_gq_a': out['m_gq_a'], 'm_gk_a': out['m_gk_a'], 'm_w_out_a': out['m_w_out_a'], 'm_g_kv': out['m_g_kv'], 'm_w_kv': out['m_w_kv'], 'm_gk_b': out['m_gk_b'], 'm_w_q_b': out['m_w_q_b'], 'm_gq_b': out['m_gq_b'], 'm_sinks': out['m_sinks'], 'm_rel_bias': out['m_rel_bias'], 'm_w_out_b': out['m_w_out_b'], 'm_w_up': out['m_w_up'], 'm_w_down': out['m_w_down'], 'v_g_attn': out['v_g_attn'], 'v_g_mlp': out['v_g_mlp'], 'v_w_in_a': out['v_w_in_a'], 'v_b_f': out['v_b_f'], 'v_gq_a': out['v_gq_a'], 'v_gk_a': out['v_gk_a'], 'v_w_out_a': out['v_w_out_a'], 'v_g_kv': out['v_g_kv'], 'v_w_kv': out['v_w_kv'], 'v_gk_b': out['v_gk_b'], 'v_w_q_b': out['v_w_q_b'], 'v_gq_b': out['v_gq_b'], 'v_sinks': out['v_sinks'], 'v_rel_bias': out['v_rel_bias'], 'v_w_out_b': out['v_w_out_b'], 'v_w_up': out['v_w_up'], 'v_w_down': out['v_w_down']}


def _loss(weights, diff, rest, loss_target):
    with _jax.named_scope("forward"):
        args = {**rest, TWIN_DIFF_INPUT: diff, **{k: w.astype(_WEIGHT_DTYPES[k]) for k, w in weights.items()}}
        y = _forward(args)
    with _jax.named_scope("loss_head"):
        err = _jnp.square(y.astype(_jnp.float32) - loss_target)
        return 0.5 * _jnp.sum(_jnp.mean(err, axis=-1)) if err.ndim else 0.5 * err


def _adamw(w, g, m, v):
    m = ADAM_B1 * m + (1.0 - ADAM_B1) * g
    v = ADAM_B2 * v + (1.0 - ADAM_B2) * _jnp.square(g)
    m_hat = m / (1.0 - ADAM_B1 ** ADAM_STEP)
    v_hat = v / (1.0 - ADAM_B2 ** ADAM_STEP)
    delta = -ADAM_LR * (m_hat / (_jnp.sqrt(v_hat) + ADAM_EPS) + ADAM_WD * w)
    return delta, m, v


def reference(x, g_attn, g_mlp, w_in_a, b_f, gq_a, gk_a, w_out_a, g_kv, w_kv, gk_b, w_q_b, gq_b, sinks, rel_bias, w_out_b, w_up, w_down, loss_target, m_g_attn, m_g_mlp, m_w_in_a, m_b_f, m_gq_a, m_gk_a, m_w_out_a, m_g_kv, m_w_kv, m_gk_b, m_w_q_b, m_gq_b, m_sinks, m_rel_bias, m_w_out_b, m_w_up, m_w_down, v_g_attn, v_g_mlp, v_w_in_a, v_b_f, v_gq_a, v_gk_a, v_w_out_a, v_g_kv, v_w_kv, v_gk_b, v_w_q_b, v_gq_b, v_sinks, v_rel_bias, v_w_out_b, v_w_up, v_w_down):
    given = dict(x=x, g_attn=g_attn, g_mlp=g_mlp, w_in_a=w_in_a, b_f=b_f, gq_a=gq_a, gk_a=gk_a, w_out_a=w_out_a, g_kv=g_kv, w_kv=w_kv, gk_b=gk_b, w_q_b=w_q_b, gq_b=gq_b, sinks=sinks, rel_bias=rel_bias, w_out_b=w_out_b, w_up=w_up, w_down=w_down, loss_target=loss_target, m_g_attn=m_g_attn, m_g_mlp=m_g_mlp, m_w_in_a=m_w_in_a, m_b_f=m_b_f, m_gq_a=m_gq_a, m_gk_a=m_gk_a, m_w_out_a=m_w_out_a, m_g_kv=m_g_kv, m_w_kv=m_w_kv, m_gk_b=m_gk_b, m_w_q_b=m_w_q_b, m_gq_b=m_gq_b, m_sinks=m_sinks, m_rel_bias=m_rel_bias, m_w_out_b=m_w_out_b, m_w_up=m_w_up, m_w_down=m_w_down, v_g_attn=v_g_attn, v_g_mlp=v_g_mlp, v_w_in_a=v_w_in_a, v_b_f=v_b_f, v_gq_a=v_gq_a, v_gk_a=v_gk_a, v_w_out_a=v_w_out_a, v_g_kv=v_g_kv, v_w_kv=v_w_kv, v_gk_b=v_gk_b, v_w_q_b=v_w_q_b, v_gq_b=v_gq_b, v_sinks=v_sinks, v_rel_bias=v_rel_bias, v_w_out_b=v_w_out_b, v_w_up=v_w_up, v_w_down=v_w_down)
    weights = {n: given[n] for n in TWIN_WEIGHTS}
    shared = {n: given[n] for n in SHARED_INPUTS}
    per_example = {n: given[n] for n in ['x']}
    grad_fn = _jax.value_and_grad(_loss, argnums=(0, 1))

    def one_microbatch(ex, loss_target):
        ex = dict(ex)
        diff = ex.pop(TWIN_DIFF_INPUT)
        return grad_fn(weights, diff, {**shared, **ex}, loss_target)

    if N_MICROBATCH == 1:
        loss, (grad_w, grad_x) = one_microbatch(per_example, given["loss_target"])
    else:
        def body(carry, xs):
            loss_sum, grad_sum = carry
            l_k, (gw_k, gx_k) = one_microbatch(xs[0], xs[1])
            with _jax.named_scope("update"):
                return (loss_sum + l_k, _jax.tree.map(_jnp.add, grad_sum, gw_k)), gx_k

        init = (_jnp.zeros((), _jnp.float32), _jax.tree.map(_jnp.zeros_like, weights))
        (loss, grad_w), grad_x = _jax.lax.scan(body, init, (per_example, given["loss_target"]))
    with _jax.named_scope("update"):
        delta_w, new_m, new_v = {}, {}, {}
        for n in TWIN_WEIGHTS:
            delta_w[n], new_m[n], new_v[n] = _adamw(weights[n], grad_w[n], given["m_" + n], given["v_" + n])
    return (loss, grad_x, *[grad_w[n] for n in TWIN_WEIGHTS], *[delta_w[n] for n in TWIN_WEIGHTS],
            *[new_m[n] for n in TWIN_WEIGHTS], *[new_v[n] for n in TWIN_WEIGHTS])
```

```python
import functools

import numpy as np
import jax
import jax.numpy as jnp
from jax import lax
from jax.experimental import pallas as pl
from jax.experimental.pallas import tpu as pltpu

F32 = jnp.float32
BF16 = jnp.bfloat16
MESH = pl.DeviceIdType.MESH

HEAD_DIM = 64
WINDOW = 128
N_BUCKETS = 32
REL_MAX_DIST = 128
NORM_EPS = 1e-6
ADAM_LR = 0.001
ADAM_B1 = 0.9
ADAM_B2 = 0.999
ADAM_EPS = 1e-08
ADAM_WD = 0.01
ADAM_STEP = 10
NEG = -1e30
N_CHIPS = 4
PACK_W = 1024
PACK_ROW_ALIGN = 256
VMEM_LIMIT = 56 * 1024 * 1024
HBM_SPEC = pl.BlockSpec(memory_space=pltpu.HBM)
VMEM_SPEC = pl.BlockSpec(memory_space=pltpu.VMEM)
SMEM_SPEC = pl.BlockSpec(memory_space=pltpu.SMEM)

BIG = (("w_in_a", 2), ("w_out_a", 1), ("w_kv", 0), ("w_q_b", 1), ("w_out_b", 1), ("w_up", 2), ("w_down", 1))
SMALL = ("g_attn", "g_mlp", "b_f", "gq_a", "gk_a", "g_kv", "gk_b", "gq_b", "sinks", "rel_bias")


def _pcall(body, **kw):
    return pl.pallas_call(body, **kw)


def _params(sem=None):
    return pltpu.CompilerParams(dimension_semantics=sem, vmem_limit_bytes=VMEM_LIMIT)


def _rinv(x):
    return lax.rsqrt(jnp.mean(x * x, axis=-1, keepdims=True) + NORM_EPS)


def _dot(a, b, dims, precision=None):
    return lax.dot_general(a, b, (dims, ((), ())), precision=precision, preferred_element_type=F32)


NN = ((1,), (0,))
NT = ((1,), (1,))
TN = ((0,), (0,))


def _matmul(a, b, mode, name, out_dtypes=(F32,), extras=(), epilogue=None, tm=512, tn=512, tk=None):
    if mode == "nn":
        (M, K), (K2, N) = a.shape, b.shape
    elif mode == "nt":
        (M, K), (N, K2) = a.shape, b.shape
    else:
        (K, M), (K2, N) = a.shape, b.shape
    assert K == K2, (a.shape, b.shape, mode)
    tm, tn = min(tm, M), min(tn, N)
    tk = K if tk is None else tk
    assert M % tm == 0 and N % tn == 0 and K % tk == 0, (M, N, K, tm, tn, tk)
    nk = K // tk
    dims = {"nn": NN, "nt": NT, "tn": TN}[mode]
    a_spec = pl.BlockSpec((tk, tm), lambda i, j, k: (k, i)) if mode == "tn" else pl.BlockSpec((tm, tk), lambda i, j, k: (i, k))
    b_spec = pl.BlockSpec((tn, tk), lambda i, j, k: (j, k)) if mode == "nt" else pl.BlockSpec((tk, tn), lambda i, j, k: (k, j))
    o_spec = pl.BlockSpec((tm, tn), lambda i, j, k: (i, j))
    n_ex, n_out = len(extras), len(out_dtypes)

    def body(*refs):
        a_ref, b_ref = refs[0], refs[1]
        ex_refs = refs[2:2 + n_ex]
        out_refs = refs[2 + n_ex:2 + n_ex + n_out]
        acc_ref = refs[2 + n_ex + n_out]
        k = pl.program_id(2)
        part = _dot(a_ref[...].astype(BF16), b_ref[...].astype(BF16), dims)

        @pl.when(k == 0)
        def _():
            acc_ref[...] = part

        @pl.when(k > 0)
        def _():
            acc_ref[...] += part

        @pl.when(k == nk - 1)
        def _():
            acc = acc_ref[...]
            outs = (acc,) if epilogue is None else epilogue(acc, *[r[...] for r in ex_refs])
            for r, o in zip(out_refs, outs):
                r[...] = o.astype(r.dtype)

    outs = _pcall(
        body, name=name, grid=(M // tm, N // tn, nk),
        in_specs=[a_spec, b_spec] + [o_spec] * n_ex,
        out_specs=[o_spec] * n_out,
        out_shape=[jax.ShapeDtypeStruct((M, N), dt) for dt in out_dtypes],
        scratch_shapes=[pltpu.VMEM((tm, tn), F32)],
        compiler_params=_params(("parallel", "parallel", "arbitrary")),
    )(a, b, *extras)
    return outs[0] if n_out == 1 else outs


def _rms_fwd(x, gains, name, ts=256):
    S, D = x.shape
    ts = min(ts, S)
    n = len(gains)

    def body(*refs):
        x_ref, g_refs, o_refs = refs[0], refs[1:1 + n], refs[1 + n:]
        xv = x_ref[...]
        xh = xv * _rinv(xv)
        for g_ref, o_ref in zip(g_refs, o_refs):
            o_ref[...] = (xh * g_ref[...]).astype(BF16)

    row = pl.BlockSpec((ts, D), lambda i: (i, 0))
    vec = pl.BlockSpec((1, D), lambda i: (0, 0))
    return _pcall(body, name=name, grid=(S // ts,), in_specs=[row] + [vec] * n, out_specs=[row] * n,
                  out_shape=[jax.ShapeDtypeStruct((S, D), BF16)] * n, compiler_params=_params(("parallel",)))(x, *gains)


def _rms_bwd(x, dres, gains, dns, name, ts=256):
    S, D = x.shape
    ts = min(ts, S)
    n = len(gains)

    def body(*refs):
        x_ref, dres_ref = refs[0], refs[1]
        g_refs, dn_refs = refs[2:2 + n], refs[2 + n:2 + 2 * n]
        dx_ref, dg_refs = refs[2 + 2 * n], refs[3 + 2 * n:]
        xv = x_ref[...]
        r = _rinv(xv)
        xh = xv * r
        dx = dres_ref[...]
        first = pl.program_id(0) == 0
        for g_ref, dn_ref, dg_ref in zip(g_refs, dn_refs, dg_refs):
            dn = dn_ref[...].astype(F32)
            part = jnp.sum(dn * xh, axis=0, keepdims=True)

            @pl.when(first)
            def _():
                dg_ref[...] = part

            @pl.when(jnp.logical_not(first))
            def _():
                dg_ref[...] += part

            dxh = dn * g_ref[...]
            dx = dx + r * (dxh - xh * jnp.mean(dxh * xh, axis=-1, keepdims=True))
        dx_ref[...] = dx

    row = pl.BlockSpec((ts, D), lambda i: (i, 0))
    vec = pl.BlockSpec((1, D), lambda i: (0, 0))
    outs = _pcall(body, name=name, grid=(S // ts,), in_specs=[row, row] + [vec] * n + [row] * n,
                  out_specs=[row] + [vec] * n,
                  out_shape=[jax.ShapeDtypeStruct((S, D), F32)] + [jax.ShapeDtypeStruct((1, D), F32)] * n,
                  compiler_params=_params(("arbitrary",)))(x, dres, *gains, *dns)
    return outs[0], outs[1:]


def _loss_head(h, tgt, name, ts=256):
    S, D = h.shape
    ts = min(ts, S)

    def body(h_ref, t_ref, dh_ref, loss_ref):
        err = h_ref[...] - t_ref[...]
        dh_ref[...] = err * (1.0 / D)
        part = 0.5 * jnp.sum(jnp.mean(err * err, axis=-1, keepdims=True), axis=0, keepdims=True)
        first = pl.program_id(0) == 0

        @pl.when(first)
        def _():
            loss_ref[...] = part

        @pl.when(jnp.logical_not(first))
        def _():
            loss_ref[...] += part

    row = pl.BlockSpec((ts, D), lambda i: (i, 0))
    return _pcall(body, name=name, grid=(S // ts,), in_specs=[row, row],
                  out_specs=[row, pl.BlockSpec((1, 1), lambda i: (0, 0))],
                  out_shape=[jax.ShapeDtypeStruct((S, D), F32), jax.ShapeDtypeStruct((1, 1), F32)],
                  compiler_params=_params(("arbitrary",)))(h, tgt)


def _gate_fwd(zt, bf, name):
    H, S = zt.shape
    nb = S // 128

    def body(z_ref, b_ref, c_ref):
        z = z_ref[...] + b_ref[...]
        lf = jnp.minimum(z, 0.0) - jnp.log(1.0 + jnp.exp(-jnp.abs(z)))
        upper = (lax.broadcasted_iota(jnp.int32, (128, 128), 0) <= lax.broadcasted_iota(jnp.int32, (128, 128), 1)).astype(F32)
        carry = jnp.zeros((H, 1), F32)
        for blk in range(nb):
            cs = _dot(lf[:, blk * 128:(blk + 1) * 128], upper, NN, precision=lax.Precision.HIGHEST) + carry
            c_ref[:, blk * 128:(blk + 1) * 128] = cs
            carry = cs[:, 127:128]

    return _pcall(body, name=name, in_specs=[VMEM_SPEC, VMEM_SPEC], out_specs=VMEM_SPEC,
                  out_shape=jax.ShapeDtypeStruct((H, S), F32))(zt, bf)


def _gate_bwd(dct, zt, bf, name):
    H, S = zt.shape
    nb = S // 128

    def body(dc_ref, z_ref, b_ref, dz_ref, db_ref):
        z = z_ref[...] + b_ref[...]
        e = jnp.exp(-jnp.abs(z))
        sig_neg = jnp.where(z >= 0, e, 1.0) / (1.0 + e)
        lower = (lax.broadcasted_iota(jnp.int32, (128, 128), 0) >= lax.broadcasted_iota(jnp.int32, (128, 128), 1)).astype(F32)
        dc = dc_ref[...]
        carry = jnp.zeros((H, 1), F32)
        db = jnp.zeros((H, 1), F32)
        for blk in reversed(range(nb)):
            sl = slice(blk * 128, (blk + 1) * 128)
            dlf = _dot(dc[:, sl], lower, NN, precision=lax.Precision.HIGHEST) + carry
            carry = dlf[:, 0:1]
            dz = dlf * sig_neg[:, sl]
            dz_ref[:, sl] = dz
            db = db + jnp.sum(dz, axis=1, keepdims=True)
        db_ref[...] = db

    return _pcall(body, name=name, in_specs=[VMEM_SPEC] * 3, out_specs=[VMEM_SPEC] * 2,
                  out_shape=[jax.ShapeDtypeStruct((H, S), F32), jax.ShapeDtypeStruct((H, 1), F32)])(dct, zt, bf)


def _tri_mask(t, keys_on_rows):
    r = lax.broadcasted_iota(jnp.int32, (t, t), 0)
    c = lax.broadcasted_iota(jnp.int32, (t, t), 1)
    return (r <= c) if keys_on_rows else (r >= c)


def _fox_fwd(qr, kr, v, c_col, c_row, gq, gk, name, t=256):
    H, S, hd = qr.shape
    t = min(t, S)
    nq = S // t

    def body(qr_ref, kr_ref, v_ref, cc_ref, cr_ref, gq_ref, gk_ref, o_ref, lse_ref, qs_s, kb_s, vb_s):
        q = qr_ref[...]
        qs_s[...] = (q * _rinv(q) * gq_ref[...] * 0.125).astype(BF16)
        k = kr_ref[...]
        kb_s[...] = (k * _rinv(k) * gk_ref[...]).astype(BF16)
        vb_s[...] = v_ref[...].astype(BF16)
        causal = _tri_mask(t, False)
        for i in range(nq):
            qi = qs_s[i * t:(i + 1) * t, :]
            ci = cc_ref[i * t:(i + 1) * t, :]

            def step(j, carry, qi=qi, ci=ci, masked=False):
                m, l, acc = carry
                j0 = j * t if isinstance(j, int) else pl.multiple_of(j * t, t)
                s = _dot(qi, kb_s[pl.ds(j0, t), :], NT) + ci - cr_ref[j]
                if masked:
                    s = jnp.where(causal, s, NEG)
                m_new = jnp.maximum(m, jnp.max(s, axis=-1, keepdims=True))
                a = jnp.exp(m - m_new)
                p = jnp.exp(s - m_new)
                l = a * l + jnp.sum(p, axis=-1, keepdims=True)
                acc = a * acc + _dot(p.astype(BF16), vb_s[pl.ds(j0, t), :], NN)
                return m_new, l, acc

            carry = (jnp.full((t, 1), NEG, F32), jnp.zeros((t, 1), F32), jnp.zeros((t, hd), F32))
            if i > 0:
                carry = lax.fori_loop(0, i, step, carry)
            m, l, acc = step(i, carry, masked=True)
            o_ref[i * t:(i + 1) * t, :] = acc / l
            lse_ref[i * t:(i + 1) * t, :] = m + jnp.log(l)

    head = pl.BlockSpec((None, S, hd), lambda h: (h, 0, 0))
    col = pl.BlockSpec((None, S, 1), lambda h: (h, 0, 0))
    rowv = pl.BlockSpec((None, nq, 1, t), lambda h: (h, 0, 0, 0))
    gain = pl.BlockSpec((1, hd), lambda h: (0, 0))
    c_row = c_row.reshape(H, nq, 1, t)
    return _pcall(body, name=name, grid=(H,), in_specs=[head, head, head, col, rowv, gain, gain],
                  out_specs=[head, col],
                  out_shape=[jax.ShapeDtypeStruct((H, S, hd), F32), jax.ShapeDtypeStruct((H, S, 1), F32)],
                  scratch_shapes=[pltpu.VMEM((S, hd), BF16)] * 3,
                  compiler_params=_params(("parallel",)))(qr, kr, v, c_col, c_row, gq, gk)


def _headnorm_bwd(raw, gain, dnormed):
    r = _rinv(raw)
    xh = raw * r
    dgain = jnp.sum(dnormed * xh, axis=0, keepdims=True)
    dxh = dnormed * gain
    return r * (dxh - xh * jnp.mean(dxh * xh, axis=-1, keepdims=True)), dgain


def _accumulate(ref, val, first):
    @pl.when(first)
    def _():
        ref[...] = val

    @pl.when(jnp.logical_not(first))
    def _():
        ref[...] += val


def _fox_bwd(qr, kr, v, c_col, c_row, gq, gk, lse_row, do, name, t=256):
    H, S, hd = qr.shape
    t = min(t, S)
    nq = S // t

    def body(qr_ref, kr_ref, v_ref, cc_ref, cr_ref, gq_ref, gk_ref, lr_ref, do_ref,
             dq_ref, dk_ref, dv_ref, dc_ref, dgq_ref, dgk_ref, qs_s, kb_s, vb_s, dob_s, dq_s, dk_s, p_s, dp_s):
        q = qr_ref[...]
        qs_s[...] = (q * _rinv(q) * gq_ref[...] * 0.125).astype(BF16)
        k = kr_ref[...]
        kb_s[...] = (k * _rinv(k) * gk_ref[...]).astype(BF16)
        vb_s[...] = v_ref[...].astype(BF16)
        dob_s[...] = do_ref[...].astype(BF16)
        dk_s[...] = jnp.zeros((S, hd), F32)
        dv_ref[...] = jnp.zeros((S, hd), F32)
        dc_ref[...] = jnp.zeros((S, 1), F32)
        causal = _tri_mask(t, True)
        for i in range(nq):
            qi = qs_s[i * t:(i + 1) * t, :]
            doi = dob_s[i * t:(i + 1) * t, :]
            cri, lri = cr_ref[i], lr_ref[i]

            def sweep1(j, delta, qi=qi, doi=doi, cri=cri, lri=lri, masked=False):
                j0 = j * t if isinstance(j, int) else pl.multiple_of(j * t, t)
                sT = _dot(kb_s[pl.ds(j0, t), :], qi, NT) + cri - cc_ref[pl.ds(j0, t), :]
                pT = jnp.exp(sT - lri)
                if masked:
                    pT = jnp.where(causal, pT, 0.0)
                dpT = _dot(vb_s[pl.ds(j0, t), :], doi, NT)
                p_s[j] = pT
                dp_s[j] = dpT
                return delta + jnp.sum(pT * dpT, axis=0, keepdims=True)

            delta = jnp.zeros((1, t), F32)
            if i > 0:
                delta = lax.fori_loop(0, i, sweep1, delta)
            delta = sweep1(i, delta, masked=True)

            def sweep2(j, dq, qi=qi, doi=doi, delta=delta):
                j0 = pl.multiple_of(j * t, t)
                pT = p_s[j]
                dsT = pT * (dp_s[j] - delta)
                dsb = dsT.astype(BF16)
                dv_ref[pl.ds(j0, t), :] += _dot(pT.astype(BF16), doi, NN)
                dk_s[pl.ds(j0, t), :] += _dot(dsb, qi, NN)
                dc_ref[pl.ds(j0, t), :] -= jnp.sum(dsT, axis=1, keepdims=True)
                return dq + _dot(dsb, kb_s[pl.ds(j0, t), :], TN)

            dq_s[i * t:(i + 1) * t, :] = lax.fori_loop(0, i + 1, sweep2, jnp.zeros((t, hd), F32))
        first = pl.program_id(0) == 0
        dq_raw, dgq = _headnorm_bwd(qr_ref[...], gq_ref[...], dq_s[...] * 0.125)
        dq_ref[...] = dq_raw
        _accumulate(dgq_ref, dgq, first)
        dk_raw, dgk = _headnorm_bwd(kr_ref[...], gk_ref[...], dk_s[...])
        dk_ref[...] = dk_raw
        _accumulate(dgk_ref, dgk, first)

    head = pl.BlockSpec((None, S, hd), lambda h: (h, 0, 0))
    col = pl.BlockSpec((None, S, 1), lambda h: (h, 0, 0))
    rowv = pl.BlockSpec((None, nq, 1, t), lambda h: (h, 0, 0, 0))
    gain = pl.BlockSpec((1, hd), lambda h: (0, 0))
    hs = jax.ShapeDtypeStruct((H, S, hd), F32)
    gs = jax.ShapeDtypeStruct((1, hd), F32)
    c_row, lse_row = c_row.reshape(H, nq, 1, t), lse_row.reshape(H, nq, 1, t)
    return _pcall(body, name=name, grid=(H,),
                  in_specs=[head, head, head, col, rowv, gain, gain, rowv, head],
                  out_specs=[head, head, head, col, gain, gain],
                  out_shape=[hs, hs, hs, jax.ShapeDtypeStruct((H, S, 1), F32), gs, gs],
                  scratch_shapes=[pltpu.VMEM((S, hd), BF16)] * 4 + [pltpu.VMEM((S, hd), F32)] * 2 + [pltpu.VMEM((nq, t, t), F32)] * 2,
                  compiler_params=_params(("arbitrary",)))(qr, kr, v, c_col, c_row, gq, gk, lse_row, do)


def _bucket_onehot():
    W = WINDOW
    dist = np.arange(W)[:, None] + W - np.arange(2 * W)[None, :]
    n = np.maximum(dist, 0)
    max_exact = N_BUCKETS // 2
    large = max_exact + (np.log(np.maximum(n, 1) / max_exact) / np.log(REL_MAX_DIST / max_exact)
                         * (N_BUCKETS - max_exact)).astype(np.int32)
    large = np.minimum(large, N_BUCKETS - 1)
    bucket = np.where(n < max_exact, n, large).astype(np.int32)
    valid = (dist >= 0) & (dist < W)
    onehot = (bucket[None] == np.arange(N_BUCKETS)[:, None, None]) & valid[None]
    return onehot.reshape(N_BUCKETS, W * 2 * W).astype(np.float32)


def _bias_expand(rel_bias_t, onehot, name, tn=4096):
    HQ, NB = rel_bias_t.shape
    L = onehot.shape[1]

    def body(r_ref, oh_ref, out_ref):
        out_ref[...] = _dot(r_ref[...], oh_ref[...].astype(F32), NN, precision=lax.Precision.HIGHEST)

    return _pcall(body, name=name, grid=(L // tn,),
                  in_specs=[pl.BlockSpec((HQ, NB), lambda i: (0, 0)), pl.BlockSpec((NB, tn), lambda i: (0, i))],
                  out_specs=pl.BlockSpec((HQ, tn), lambda i: (0, i)),
                  out_shape=jax.ShapeDtypeStruct((HQ, L), F32), compiler_params=_params(("parallel",)))(rel_bias_t, onehot)


def _bias_reduce(dbias, onehot, name, tk=4096):
    HQ, L = dbias.shape
    NB = onehot.shape[0]

    def body(d_ref, oh_ref, out_ref):
        part = _dot(d_ref[...], oh_ref[...].astype(F32), NT, precision=lax.Precision.HIGHEST)
        _accumulate(out_ref, part, pl.program_id(0) == 0)

    return _pcall(body, name=name, grid=(L // tk,),
                  in_specs=[pl.BlockSpec((HQ, tk), lambda i: (0, i)), pl.BlockSpec((NB, tk), lambda i: (0, i))],
                  out_specs=pl.BlockSpec((HQ, NB), lambda i: (0, 0)),
                  out_shape=jax.ShapeDtypeStruct((HQ, NB), F32), compiler_params=_params(("arbitrary",)))(dbias, onehot)


def _swa_fwd(qr, kr, v, bias, sinks, gq, gk, name):
    HQ, S, hd = qr.shape
    KVH = kr.shape[0]
    G = HQ // KVH
    W = WINDOW
    nb = S // W

    def body(qr_ref, kr_ref, v_ref, bias_ref, sink_ref, gq_ref, gk_ref, o_ref, lse_ref, qs_s, kb_s, vb_s):
        hq = pl.program_id(0) * G + pl.program_id(1)
        q = qr_ref[...]
        qs_s[...] = (q * _rinv(q) * gq_ref[...] * 0.125).astype(BF16)
        k = kr_ref[...]
        kb_s[...] = (k * _rinv(k) * gk_ref[...]).astype(BF16)
        vb_s[...] = v_ref[...].astype(BF16)
        sink = sink_ref[hq]

        def finish(n0, s, vcat):
            m = jnp.maximum(jnp.max(s, axis=-1, keepdims=True), sink)
            e = jnp.exp(s - m)
            l = jnp.sum(e, axis=-1, keepdims=True) + jnp.exp(sink - m)
            o_ref[pl.ds(n0, W), :] = _dot(e.astype(BF16), vcat, NN) / l
            lse_ref[pl.ds(n0, W), :] = m + jnp.log(l)

        r1 = lax.broadcasted_iota(jnp.int32, (W, W), 0)
        c1 = lax.broadcasted_iota(jnp.int32, (W, W), 1)
        s0 = _dot(qs_s[0:W, :], kb_s[0:W, :], NT) + bias_ref[:, W:2 * W]
        finish(0, jnp.where(c1 <= r1, s0, NEG), vb_s[0:W, :])
        r2 = lax.broadcasted_iota(jnp.int32, (W, 2 * W), 0)
        c2 = lax.broadcasted_iota(jnp.int32, (W, 2 * W), 1)
        band = (c2 > r2) & (c2 <= r2 + W)

        def step(n, _):
            n0 = pl.multiple_of(n * W, W)
            k0 = pl.multiple_of(n * W - W, W)
            s = _dot(qs_s[pl.ds(n0, W), :], kb_s[pl.ds(k0, 2 * W), :], NT) + bias_ref[...]
            finish(n0, jnp.where(band, s, NEG), vb_s[pl.ds(k0, 2 * W), :])
            return 0

        lax.fori_loop(1, nb, step, 0)

    qh = pl.BlockSpec((None, S, hd), lambda a, g: (a * G + g, 0, 0))
    kh = pl.BlockSpec((None, S, hd), lambda a, g: (a, 0, 0))
    gain = pl.BlockSpec((1, hd), lambda a, g: (0, 0))
    return _pcall(body, name=name, grid=(KVH, G),
                  in_specs=[qh, kh, kh, pl.BlockSpec((None, W, 2 * W), lambda a, g: (a * G + g, 0, 0)), SMEM_SPEC, gain, gain],
                  out_specs=[qh, pl.BlockSpec((None, S, 1), lambda a, g: (a * G + g, 0, 0))],
                  out_shape=[jax.ShapeDtypeStruct((HQ, S, hd), F32), jax.ShapeDtypeStruct((HQ, S, 1), F32)],
                  scratch_shapes=[pltpu.VMEM((S, hd), BF16)] * 3,
                  compiler_params=_params(("parallel", "parallel")))(qr, kr, v, bias, sinks, gq, gk)


def _swa_bwd(qr, kr, v, bias_t, sinks, gq, gk, lse_row, do, name):
    HQ, S, hd = qr.shape
    KVH = kr.shape[0]
    G = HQ // KVH
    W = WINDOW
    nb = S // W

    def body(qr_ref, kr_ref, v_ref, bias_ref, sink_ref, gq_ref, gk_ref, lr_ref, do_ref,
             dq_ref, dk_ref, dv_ref, db_ref, dsink_ref, dgq_ref, dgk_ref,
             qs_s, kb_s, vb_s, dob_s, dq_s, dk_s):
        a, g = pl.program_id(0), pl.program_id(1)
        hq = a * G + g
        q = qr_ref[...]
        qs_s[...] = (q * _rinv(q) * gq_ref[...] * 0.125).astype(BF16)
        k = kr_ref[...]
        kb_s[...] = (k * _rinv(k) * gk_ref[...]).astype(BF16)
        vb_s[...] = v_ref[...].astype(BF16)
        dob_s[...] = do_ref[...].astype(BF16)
        sink = sink_ref[hq]

        @pl.when(g == 0)
        def _():
            dk_s[...] = jnp.zeros((S, hd), F32)
            dv_ref[...] = jnp.zeros((S, hd), F32)

        def block(n, n0, k0, nkeys, valid, bias_rows, first_band):
            qi = qs_s[pl.ds(n0, W), :]
            doi = dob_s[pl.ds(n0, W), :]
            kc = kb_s[pl.ds(k0, nkeys), :]
            vc = vb_s[pl.ds(k0, nkeys), :]
            sT = _dot(kc, qi, NT) + bias_rows
            lse = lr_ref[n]
            pT = jnp.where(valid, jnp.exp(sT - lse), 0.0)
            dv_ref[pl.ds(k0, nkeys), :] += _dot(pT.astype(BF16), doi, NN)
            dpT = _dot(vc, doi, NT)
            delta = jnp.sum(pT * dpT, axis=0, keepdims=True)
            dsT = pT * (dpT - delta)
            dsb = dsT.astype(BF16)
            dk_s[pl.ds(k0, nkeys), :] += _dot(dsb, qi, NN)
            dq_s[pl.ds(n0, W), :] = _dot(dsb, kc, TN)
            if first_band:
                db_ref[0:W, :] = jnp.zeros((W, W), F32)
                db_ref[W:2 * W, :] = dsT
            else:
                db_ref[...] += dsT
            return -jnp.sum(jnp.exp(sink - lse) * delta, axis=1, keepdims=True)

        r1 = lax.broadcasted_iota(jnp.int32, (W, W), 0)
        c1 = lax.broadcasted_iota(jnp.int32, (W, W), 1)
        dsink = block(0, 0, 0, W, r1 <= c1, bias_ref[W:2 * W, :], True)
        r2 = lax.broadcasted_iota(jnp.int32, (2 * W, W), 0)
        c2 = lax.broadcasted_iota(jnp.int32, (2 * W, W), 1)
        band = (r2 > c2) & (r2 <= c2 + W)

        def step(n, acc):
            return acc + block(n, pl.multiple_of(n * W, W), pl.multiple_of(n * W - W, W), 2 * W, band, bias_ref[...], False)

        dsink = lax.fori_loop(1, nb, step, dsink)
        dsink_ref[...] = jnp.broadcast_to(dsink, (1, 128))
        dq_raw, dgq = _headnorm_bwd(qr_ref[...], gq_ref[...], dq_s[...] * 0.125)
        dq_ref[...] = dq_raw
        _accumulate(dgq_ref, dgq, jnp.logical_and(a == 0, g == 0))

        @pl.when(g == G - 1)
        def _():
            dk_raw, dgk = _headnorm_bwd(kr_ref[...], gk_ref[...], dk_s[...])
            dk_ref[...] = dk_raw
            _accumulate(dgk_ref, dgk, a == 0)

    qh = pl.BlockSpec((None, S, hd), lambda a, g: (a * G + g, 0, 0))
    kh = pl.BlockSpec((None, S, hd), lambda a, g: (a, 0, 0))
    qrow = pl.BlockSpec((None, nb, 1, W), lambda a, g: (a * G + g, 0, 0, 0))
    lse_row = lse_row.reshape(HQ, nb, 1, W)
    bt = pl.BlockSpec((None, 2 * W, W), lambda a, g: (a * G + g, 0, 0))
    gain = pl.BlockSpec((1, hd), lambda a, g: (0, 0))
    qs = jax.ShapeDtypeStruct((HQ, S, hd), F32)
    ks = jax.ShapeDtypeStruct((KVH, S, hd), F32)
    gs = jax.ShapeDtypeStruct((1, hd), F32)
    return _pcall(body, name=name, grid=(KVH, G),
                  in_specs=[qh, kh, kh, bt, SMEM_SPEC, gain, gain, qrow, qh],
                  out_specs=[qh, kh, kh, bt, pl.BlockSpec((None, 1, 128), lambda a, g: (a * G + g, 0, 0)), gain, gain],
                  out_shape=[qs, ks, ks, jax.ShapeDtypeStruct((HQ, 2 * W, W), F32),
                             jax.ShapeDtypeStruct((HQ, 1, 128), F32), gs, gs],
                  scratch_shapes=[pltpu.VMEM((S, hd), BF16)] * 4 + [pltpu.VMEM((S, hd), F32)] * 2,
                  compiler_params=_params(("arbitrary", "arbitrary")))(qr, kr, v, bias_t, sinks, gq, gk, lse_row, do)


def _adamw(w, g, m, v, name, tr=256):
    R, C = w.shape
    tr = min(tr, R)
    assert R % tr == 0

    def body(w_ref, g_ref, m_ref, v_ref, d_ref, m2_ref, v2_ref):
        gv = g_ref[...]
        m2 = ADAM_B1 * m_ref[...] + (1.0 - ADAM_B1) * gv
        v2 = ADAM_B2 * v_ref[...] + (1.0 - ADAM_B2) * jnp.square(gv)
        m_hat = m2 / (1.0 - ADAM_B1 ** ADAM_STEP)
        v_hat = v2 / (1.0 - ADAM_B2 ** ADAM_STEP)
        d_ref[...] = -ADAM_LR * (m_hat / (jnp.sqrt(v_hat) + ADAM_EPS) + ADAM_WD * w_ref[...])
        m2_ref[...] = m2
        v2_ref[...] = v2

    blk = pl.BlockSpec((tr, C), lambda i: (i, 0))
    return _pcall(body, name=name, grid=(R // tr,), in_specs=[blk] * 4, out_specs=[blk] * 3,
                  out_shape=[jax.ShapeDtypeStruct((R, C), F32)] * 3, compiler_params=_params(("parallel",)))(w, g, m, v)


def _sum_parts(parts, name, out_dtype, tr=128):
    P, R, C = parts.shape
    tr = min(tr, R)
    assert R % tr == 0, (R, tr)

    def body(p_ref, o_ref):
        acc = p_ref[0].astype(F32)
        for k in range(1, P):
            acc = acc + p_ref[k].astype(F32)
        o_ref[...] = acc.astype(o_ref.dtype)

    return _pcall(body, name=name, grid=(R // tr,), in_specs=[pl.BlockSpec((P, tr, C), lambda i: (0, i, 0))],
                  out_specs=pl.BlockSpec((tr, C), lambda i: (i, 0)),
                  out_shape=jax.ShapeDtypeStruct((R, C), out_dtype), compiler_params=_params(("parallel",)))(parts)


def _place():
    x, y, c = lax.axis_index("x"), lax.axis_index("y"), lax.axis_index("c")
    others = [(1 - x, y), (x, 1 - y), (1 - x, 1 - y)]
    return x, y, c, others


def _allgather_weights(pk, name):
    R, C = pk.shape
    Rh = R // 2

    def body(pk_ref, out_ref, send_sems, recv_sems, local_sem):
        x, y, c, others = _place()
        me = 2 * x + y
        sibling = (x, y, 1 - c)

        def half(chip, hh):
            return out_ref.at[chip, pl.ds(pl.multiple_of(hh * Rh, PACK_ROW_ALIGN // 2), Rh), :]

        def copy(k, src, dst, to):
            return pltpu.make_async_remote_copy(src_ref=src, dst_ref=dst, send_sem=send_sems.at[k], recv_sem=recv_sems.at[k],
                                                device_id=to, device_id_type=MESH)

        mine = pltpu.make_async_copy(pk_ref, out_ref.at[me], local_sem)
        mine.start()
        my_half = pk_ref.at[pl.ds(pl.multiple_of(c * Rh, PACK_ROW_ALIGN // 2), Rh), :]
        first = [copy(k, my_half, half(me, c), (cx, cy, c)) for k, (cx, cy) in enumerate(others)]
        for cp in first:
            cp.start()
        passed = [copy(3 + k, half(2 * cx + cy, c), half(2 * cx + cy, c), sibling) for k, (cx, cy) in enumerate(others)]
        for k, (cx, cy) in enumerate(others):
            copy(k, my_half, half(2 * cx + cy, c), sibling).wait_recv()
            passed[k].start()
        for k, (cx, cy) in enumerate(others):
            copy(3 + k, my_half, half(2 * cx + cy, 1 - c), sibling).wait_recv()
        for cp in first + passed:
            cp.wait_send()
        mine.wait()

    return _pcall(body, name=name, in_specs=[HBM_SPEC], out_specs=HBM_SPEC,
                  out_shape=jax.ShapeDtypeStruct((N_CHIPS, R, C), pk.dtype),
                  scratch_shapes=[pltpu.SemaphoreType.DMA((6,)), pltpu.SemaphoreType.DMA((6,)), pltpu.SemaphoreType.DMA])(pk)


def _swap_sibling(buf, name):
    def body(src_ref, dst_ref, send_sem, recv_sem):
        x, y, c, _ = _place()
        cp = pltpu.make_async_remote_copy(src_ref=src_ref, dst_ref=dst_ref, send_sem=send_sem, recv_sem=recv_sem,
                                          device_id=(x, y, 1 - c), device_id_type=MESH)
        cp.start()
        cp.wait()

    return _pcall(body, name=name, in_specs=[HBM_SPEC], out_specs=HBM_SPEC,
                  out_shape=jax.ShapeDtypeStruct(buf.shape, buf.dtype),
                  scratch_shapes=[pltpu.SemaphoreType.DMA, pltpu.SemaphoreType.DMA])(buf)


def _scatter_to_chips(parts, name):
    P, R, C = parts.shape

    def body(src_ref, out_ref, send_sems, recv_sems, local_sem):
        x, y, c, others = _place()
        me = 2 * x + y
        mine = pltpu.make_async_copy(src_ref.at[me], out_ref.at[me], local_sem)
        mine.start()

        def copy(k, src_chip, dst_chip, to):
            return pltpu.make_async_remote_copy(src_ref=src_ref.at[src_chip], dst_ref=out_ref.at[dst_chip],
                                                send_sem=send_sems.at[k], recv_sem=recv_sems.at[k], device_id=to, device_id_type=MESH)

        sends = [copy(k, 2 * cx + cy, me, (cx, cy, c)) for k, (cx, cy) in enumerate(others)]
        for cp in sends:
            cp.start()
        for k, (cx, cy) in enumerate(others):
            copy(k, me, 2 * cx + cy, (cx, cy, c)).wait_recv()
        for cp in sends:
            cp.wait_send()
        mine.wait()

    return _pcall(body, name=name, in_specs=[HBM_SPEC], out_specs=HBM_SPEC,
                  out_shape=jax.ShapeDtypeStruct(parts.shape, parts.dtype),
                  scratch_shapes=[pltpu.SemaphoreType.DMA((3,)), pltpu.SemaphoreType.DMA((3,)), pltpu.SemaphoreType.DMA])(parts)


def _allgather_small(blk, name):
    M, C = blk.shape

    def body(x_ref, out_ref, send_sems, recv_sems, local_sem):
        x, y, c, others = _place()
        me, sibling = (x, y, c), (x, y, 1 - c)

        def rows(px, py, pc):
            return out_ref.at[4 * px + 2 * py + pc]

        def copy(k, block, to, src=None):
            return pltpu.make_async_remote_copy(src_ref=rows(*block) if src is None else src, dst_ref=rows(*block),
                                                send_sem=send_sems.at[k], recv_sem=recv_sems.at[k], device_id=to, device_id_type=MESH)

        mine = pltpu.make_async_copy(x_ref, rows(*me), local_sem)
        mine.start()
        first = [copy(0, me, sibling, src=x_ref)]
        first += [copy(1 + j, me, (*chip, c), src=x_ref) for j, chip in enumerate(others)]
        for cp in first:
            cp.start()
        passed = [copy(4 + j, (*chip, c), sibling) for j, chip in enumerate(others)]
        for j, chip in enumerate(others):
            copy(1 + j, (*chip, c), me).wait_recv()
            passed[j].start()
        copy(0, sibling, me).wait_recv()
        for j, chip in enumerate(others):
            copy(4 + j, (*chip, 1 - c), me).wait_recv()
        for cp in first + passed:
            cp.wait_send()
        mine.wait()

    return _pcall(body, name=name, in_specs=[VMEM_SPEC], out_specs=VMEM_SPEC,
                  out_shape=jax.ShapeDtypeStruct((8, M, C), blk.dtype),
                  scratch_shapes=[pltpu.SemaphoreType.DMA((7,)), pltpu.SemaphoreType.DMA((7,)), pltpu.SemaphoreType.DMA])(blk)


def _pack_rows(n_elems, width=PACK_W, align=PACK_ROW_ALIGN):
    rows = -(-n_elems // width)
    return -(-rows // align) * align


def _pack(arrays, dtype, width=PACK_W, align=PACK_ROW_ALIGN):
    flat = jnp.concatenate([a.astype(dtype).reshape(-1) for a in arrays])
    rows = _pack_rows(flat.shape[0], width, align)
    flat = jnp.pad(flat, (0, rows * width - flat.shape[0]))
    return flat.reshape(rows, width)


def _pack_small(arrays):
    return _pack(arrays, F32, width=128, align=8)


def _unpack(flat, shapes):
    out, off = [], 0
    for shp in shapes:
        n = int(np.prod(shp))
        out.append(flat[..., off:off + n].reshape(flat.shape[:-1] + tuple(shp)))
        off += n
    return out


def _join(pieces, axis):
    moved = jnp.moveaxis(pieces, 0, axis)
    shp = list(moved.shape)
    return moved.reshape(shp[:axis] + [shp[axis] * shp[axis + 1]] + shp[axis + 2:])


def _split(full, axis):
    shp = list(full.shape)
    parts = full.reshape(shp[:axis] + [N_CHIPS, shp[axis] // N_CHIPS] + shp[axis + 1:])
    return jnp.moveaxis(parts, axis, 0)


def _heads(x2d, n_heads):
    S = x2d.shape[0]
    return x2d.reshape(S, n_heads, HEAD_DIM).transpose(1, 0, 2)


def _unheads(x3d):
    H, S, hd = x3d.shape
    return x3d.transpose(1, 0, 2).reshape(S, H * hd)


def _mlp_fwd(h, g, w_up, w_down, tag):
    (n,) = _rms_fwd(h, [g], f"rms_mlp{tag}")
    u, a = _matmul(n, w_up, "nn", f"up{tag}", out_dtypes=(F32, BF16),
                   epilogue=lambda acc: (acc, jnp.square(jnp.maximum(acc, 0.0))))
    h_out = _matmul(a, w_down, "nn", f"down{tag}", extras=(h,), epilogue=lambda acc, res: (res + acc,), tk=1024)
    return h_out, (n, u, a)


def _mlp_bwd(dh_out, h, g, w_up, w_down, saved, tag):
    n, u, a = saved
    dw_down = _matmul(a, dh_out, "tn", f"dw_down{tag}")
    du = _matmul(dh_out, w_down, "nt", f"du{tag}", out_dtypes=(BF16,), extras=(u,),
                 epilogue=lambda acc, uu: (acc * (2.0 * jnp.maximum(uu, 0.0)),))
    dw_up = _matmul(n, du, "tn", f"dw_up{tag}")
    dn = _matmul(du, w_up, "nt", f"dn_mlp{tag}", tk=1024)
    dh, (dg,) = _rms_bwd(h, dh_out, [g], [dn], f"rms_mlp_bwd{tag}")
    return dh, dg, dw_up, dw_down


def kernel(x, g_attn, g_mlp, w_in_a, b_f, gq_a, gk_a, w_out_a, g_kv, w_kv, gk_b, w_q_b, gq_b, sinks, rel_bias, w_out_b, w_up, w_down, loss_target, m_g_attn, m_g_mlp, m_w_in_a, m_b_f, m_gq_a, m_gk_a, m_w_out_a, m_g_kv, m_w_kv, m_gk_b, m_w_q_b, m_gq_b, m_sinks, m_rel_bias, m_w_out_b, m_w_up, m_w_down, v_g_attn, v_g_mlp, v_w_in_a, v_b_f, v_gq_a, v_gk_a, v_w_out_a, v_g_kv, v_w_kv, v_gk_b, v_w_q_b, v_gq_b, v_sinks, v_rel_bias, v_w_out_b, v_w_up, v_w_down):
    given = dict(locals())
    S, D = x.shape[1], x.shape[2]
    H = D // HEAD_DIM
    KVH = w_kv.shape[1] // (2 * HEAD_DIM)
    kvw = KVH * HEAD_DIM
    hw = H * HEAD_DIM
    F = w_up.shape[2] * N_CHIPS
    c_idx = lax.axis_index("c")
    xs, tgt = x[0], loss_target[0]

    big_shards = [given[n] for n, _ in BIG]
    gathered = _allgather_weights(_pack(big_shards, BF16), "allgather_weights")
    pieces = _unpack(gathered.reshape(N_CHIPS, -1), [s.shape for s in big_shards])
    full = {n: _join(p, ax) for (n, ax), p in zip(BIG, pieces)}
    win = jnp.pad(full["w_in_a"][0], ((0, 0), (0, (-full["w_in_a"].shape[2]) % 128)))
    wout_a, wq_b, wout_b, wkv = full["w_out_a"][0], full["w_q_b"][0], full["w_out_b"][0], full["w_kv"]
    wup, wdown = full["w_up"], full["w_down"]
    n_in = win.shape[1]

    vec = lambda a: a.reshape(1, -1)

    (n0,) = _rms_fwd(xs, [vec(g_attn[0])], "rms_attn0")
    proj = _matmul(n0, win, "nn", "proj_in", tn=640 if n_in % 640 == 0 else 128)
    q_raw, k_raw, v_a = _heads(proj[:, :hw], H), _heads(proj[:, hw:2 * hw], H), _heads(proj[:, 2 * hw:3 * hw], H)
    zt = proj[:, 3 * hw:3 * hw + H].T
    c_row = _gate_fwd(zt, b_f.reshape(H, 1), "gate_fwd")
    c_col3, c_row3 = c_row.reshape(H, S, 1), c_row.reshape(H, 1, S)
    o_a, lse_a = _fox_fwd(q_raw, k_raw, v_a, c_col3, c_row3, vec(gq_a[0]), vec(gk_a[0]), "fox_fwd")
    o_a2 = _unheads(o_a).astype(BF16)
    h1 = _matmul(o_a2, wout_a, "nn", "out_a", extras=(xs,), epilogue=lambda acc, res: (res + acc,))
    h2, mlp0 = _mlp_fwd(h1, vec(g_mlp[0]), wup[0], wdown[0], "0")

    nkv, n2 = _rms_fwd(h2, [vec(g_kv), vec(g_attn[1])], "rms_attn1")
    kv = _matmul(nkv, wkv, "nn", "proj_kv")
    k2_raw, v_b = _heads(kv[:, :kvw], KVH), _heads(kv[:, kvw:], KVH)
    q2_raw = _heads(_matmul(n2, wq_b, "nn", "proj_q"), H)
    onehot = jnp.asarray(_bucket_onehot(), dtype=BF16)
    bias = _bias_expand(rel_bias.T, onehot, "bias_expand").reshape(H, WINDOW, 2 * WINDOW)
    o_b, lse_b = _swa_fwd(q2_raw, k2_raw, v_b, bias, sinks[0], vec(gq_b[0]), vec(gk_b), "swa_fwd")
    o_b2 = _unheads(o_b).astype(BF16)
    h3 = _matmul(o_b2, wout_b, "nn", "out_b", extras=(h2,), epilogue=lambda acc, res: (res + acc,))
    h4, mlp1 = _mlp_fwd(h3, vec(g_mlp[1]), wup[1], wdown[1], "1")

    dh4, loss_part = _loss_head(h4, tgt, "loss_head")
    loss = lax.psum(loss_part[0, 0], ("x", "y", "c"))

    dh3, dg_mlp1, dw_up1, dw_down1 = _mlp_bwd(dh4, h3, vec(g_mlp[1]), wup[1], wdown[1], mlp1, "1")
    dw_out_b = _matmul(o_b2, dh3, "tn", "dw_out_b")
    do_b = _heads(_matmul(dh3, wout_b, "nt", "do_b"), H)
    dq2_raw, dk2_raw, dv_b, dbias_t, dsink, dgq_b, dgk_b = _swa_bwd(
        q2_raw, k2_raw, v_b, bias.transpose(0, 2, 1), sinks[0], vec(gq_b[0]), vec(gk_b), lse_b, do_b, "swa_bwd")
    d_rel_bias = _bias_reduce(dbias_t.transpose(0, 2, 1).reshape(H, -1), onehot, "bias_reduce").T
    dq2 = _unheads(dq2_raw).astype(BF16)
    dw_q_b = _matmul(n2, dq2, "tn", "dw_q_b")
    dn2 = _matmul(dq2, wq_b, "nt", "dn2")
    dkv = jnp.concatenate([_unheads(dk2_raw), _unheads(dv_b)], axis=1).astype(BF16)
    dw_kv = _matmul(nkv, dkv, "tn", "dw_kv")
    dnkv = _matmul(dkv, wkv, "nt", "dnkv")
    dh2, (dg_kv, dg_attn1) = _rms_bwd(h2, dh3, [vec(g_kv), vec(g_attn[1])], [dnkv, dn2], "rms_attn1_bwd")

    dh1, dg_mlp0, dw_up0, dw_down0 = _mlp_bwd(dh2, h1, vec(g_mlp[0]), wup[0], wdown[0], mlp0, "0")
    dw_out_a = _matmul(o_a2, dh1, "tn", "dw_out_a")
    do_a = _heads(_matmul(dh1, wout_a, "nt", "do_a"), H)
    dq_raw, dk_raw, dv_a, dc_col, dgq_a, dgk_a = _fox_bwd(
        q_raw, k_raw, v_a, c_col3, c_row3, vec(gq_a[0]), vec(gk_a[0]), lse_a, do_a, "fox_bwd")
    dzt, db_f = _gate_bwd(dc_col.reshape(H, S), zt, b_f.reshape(H, 1), "gate_bwd")
    dproj = jnp.concatenate([_unheads(dq_raw), _unheads(dk_raw), _unheads(dv_a), dzt.T,
                             jnp.zeros((S, n_in - 3 * hw - H), F32)], axis=1).astype(BF16)
    dw_in = _matmul(n0, dproj, "tn", "dw_in", tn=640 if n_in % 640 == 0 else 128)
    dn0 = _matmul(dproj, win, "nt", "dn0", tk=640 if n_in % 640 == 0 else 128)
    grad_x, (dg_attn0,) = _rms_bwd(xs, dh1, [vec(g_attn[0])], [dn0], "rms_attn0_bwd")

    grads_full = {
        "w_in_a": dw_in[None, :, :3 * hw + H], "w_out_a": dw_out_a[None], "w_kv": dw_kv, "w_q_b": dw_q_b[None],
        "w_out_b": dw_out_b[None], "w_up": jnp.stack([dw_up0, dw_up1]), "w_down": jnp.stack([dw_down0, dw_down1]),
    }
    split = [_split(grads_full[n], ax) for n, ax in BIG]
    gpack = jnp.concatenate([s.astype(BF16).reshape(N_CHIPS, -1) for s in split], axis=1)
    R = _pack_rows(gpack.shape[1])
    gpack = jnp.pad(gpack, ((0, 0), (0, R * PACK_W - gpack.shape[1]))).reshape(N_CHIPS, 2, R // 2, PACK_W)
    keep = lax.dynamic_index_in_dim(gpack, c_idx, axis=1, keepdims=False)
    give = lax.dynamic_index_in_dim(gpack, 1 - c_idx, axis=1, keepdims=False)
    got = _swap_sibling(give, "swap_grad_halves")
    pair_sum = _sum_parts(jnp.stack([keep, got]).reshape(2, N_CHIPS * (R // 2), PACK_W), "sum_core_pair", BF16)
    landed = _scatter_to_chips(pair_sum.reshape(N_CHIPS, R // 2, PACK_W), "scatter_grads")
    my_half = _sum_parts(landed, "sum_chips", F32)
    their_half = _swap_sibling(my_half, "swap_reduced_halves")
    lo = jnp.where(c_idx == 0, my_half, their_half)
    hi = jnp.where(c_idx == 0, their_half, my_half)
    gshard = jnp.concatenate([lo, hi], axis=0).reshape(-1)
    big_grads = _unpack(gshard, [s.shape for s in big_shards])

    small_grads = {
        "g_attn": jnp.concatenate([dg_attn0, dg_attn1], axis=0), "g_mlp": jnp.concatenate([dg_mlp0, dg_mlp1], axis=0),
        "b_f": db_f.reshape(1, H), "gq_a": dgq_a, "gk_a": dgk_a, "g_kv": dg_kv.reshape(-1), "gk_b": dgk_b.reshape(-1),
        "gq_b": dgq_b, "sinks": dsink[:, 0, 0].reshape(1, H), "rel_bias": d_rel_bias,
    }
    small_shapes = [given[n].shape for n in SMALL]
    spack = _pack_small([small_grads[n] for n in SMALL])
    small_sum = _sum_parts(_allgather_small(spack, "allgather_small"), "sum_small", F32, tr=spack.shape[0])
    small_red = _unpack(small_sum.reshape(-1), small_shapes)

    grads = dict(zip([n for n, _ in BIG], big_grads))
    grads.update(dict(zip(SMALL, small_red)))
    sw = _pack_small([given[n] for n in SMALL])
    sm = _pack_small([given["m_" + n] for n in SMALL])
    sv = _pack_small([given["v_" + n] for n in SMALL])
    sd, sm2, sv2 = _adamw(sw, small_sum, sm, sv, "adamw_small", tr=sw.shape[0])
    delta = dict(zip(SMALL, _unpack(sd.reshape(-1), small_shapes)))
    new_m = dict(zip(SMALL, _unpack(sm2.reshape(-1), small_shapes)))
    new_v = dict(zip(SMALL, _unpack(sv2.reshape(-1), small_shapes)))
    for n, _ in BIG:
        w = given[n]
        two_d = (-1, w.shape[-1])
        d, m2, v2 = _adamw(w.reshape(two_d), grads[n].reshape(two_d), given["m_" + n].reshape(two_d),
                           given["v_" + n].reshape(two_d), "adamw_" + n)
        delta[n], new_m[n], new_v[n] = d.reshape(w.shape), m2.reshape(w.shape), v2.reshape(w.shape)

    order = ["g_attn", "g_mlp", "w_in_a", "b_f", "gq_a", "gk_a", "w_out_a", "g_kv", "w_kv", "gk_b", "w_q_b", "gq_b",
             "sinks", "rel_bias", "w_out_b", "w_up", "w_down"]
    return (loss, grad_x[None], *[grads[n] for n in order], *[delta[n] for n in order],
            *[new_m[n] for n in order], *[new_v[n] for n in order])
```

```python
import numpy as np
import jax
import jax.numpy as jnp
from jax import lax
from jax.experimental import pallas as pl
from jax.experimental.pallas import tpu as pltpu

F32 = jnp.float32
BF16 = jnp.bfloat16
MESH = pl.DeviceIdType.MESH

HEAD_DIM = 64
LANES = 128
WINDOW = 128
N_BUCKETS = 32
REL_MAX_DIST = 128
NORM_EPS = 1e-6
ADAM_LR = 0.001
ADAM_B1 = 0.9
ADAM_B2 = 0.999
ADAM_EPS = 1e-08
ADAM_WD = 0.01
ADAM_STEP = 10
NEG = -1e30
N_CHIPS = 4
PACK_W = 1024
PACK_ROW_ALIGN = 256
VMEM_LIMIT = 56 * 1024 * 1024
HBM_SPEC = pl.BlockSpec(memory_space=pltpu.HBM)
VMEM_SPEC = pl.BlockSpec(memory_space=pltpu.VMEM)

BIG = (("w_in_a", 2), ("w_out_a", 1), ("w_kv", 0), ("w_q_b", 1), ("w_out_b", 1), ("w_up", 2), ("w_down", 1))
SMALL = ("g_attn", "g_mlp", "b_f", "gq_a", "gk_a", "g_kv", "gk_b", "gq_b", "sinks", "rel_bias")


def _pcall(body, **kw):
    return pl.pallas_call(body, **kw)


def _params(sem=None):
    return pltpu.CompilerParams(dimension_semantics=sem, vmem_limit_bytes=VMEM_LIMIT)


def _rinv(x):
    return lax.rsqrt(jnp.mean(x * x, axis=-1, keepdims=True) + NORM_EPS)


def _dot(a, b, dims, precision=None):
    return lax.dot_general(a, b, (dims, ((), ())), precision=precision, preferred_element_type=F32)


NN = ((1,), (0,))
NT = ((1,), (1,))
TN = ((0,), (0,))


def _accumulate(ref, val, first):
    @pl.when(first)
    def _():
        ref[...] = val

    @pl.when(jnp.logical_not(first))
    def _():
        ref[...] += val


def _matmul(a, b, mode, name, out_dtypes=(F32,), extras=(), epilogue=None, tm=512, tn=512, tk=None):
    if mode == "nn":
        (M, K), (K2, N) = a.shape, b.shape
    elif mode == "nt":
        (M, K), (N, K2) = a.shape, b.shape
    else:
        (K, M), (K2, N) = a.shape, b.shape
    assert K == K2, (a.shape, b.shape, mode)
    tm, tn = min(tm, M), min(tn, N)
    tk = K if tk is None else tk
    assert M % tm == 0 and N % tn == 0 and K % tk == 0, (M, N, K, tm, tn, tk)
    nk = K // tk
    dims = {"nn": NN, "nt": NT, "tn": TN}[mode]
    a_spec = pl.BlockSpec((tk, tm), lambda i, j, k: (k, i)) if mode == "tn" else pl.BlockSpec((tm, tk), lambda i, j, k: (i, k))
    b_spec = pl.BlockSpec((tn, tk), lambda i, j, k: (j, k)) if mode == "nt" else pl.BlockSpec((tk, tn), lambda i, j, k: (k, j))
    o_spec = pl.BlockSpec((tm, tn), lambda i, j, k: (i, j))
    n_ex, n_out = len(extras), len(out_dtypes)

    def body(*refs):
        a_ref, b_ref = refs[0], refs[1]
        ex_refs = refs[2:2 + n_ex]
        out_refs = refs[2 + n_ex:2 + n_ex + n_out]
        acc_ref = refs[2 + n_ex + n_out]
        k = pl.program_id(2)
        part = _dot(a_ref[...].astype(BF16), b_ref[...].astype(BF16), dims)

        @pl.when(k == 0)
        def _():
            acc_ref[...] = part

        @pl.when(k > 0)
        def _():
            acc_ref[...] += part

        @pl.when(k == nk - 1)
        def _():
            acc = acc_ref[...]
            outs = (acc,) if epilogue is None else epilogue(acc, *[r[...] for r in ex_refs])
            for r, o in zip(out_refs, outs):
                r[...] = o.astype(r.dtype)

    outs = _pcall(
        body, name=name, grid=(M // tm, N // tn, nk),
        in_specs=[a_spec, b_spec] + [o_spec] * n_ex,
        out_specs=[o_spec] * n_out,
        out_shape=[jax.ShapeDtypeStruct((M, N), dt) for dt in out_dtypes],
        scratch_shapes=[pltpu.VMEM((tm, tn), F32)],
        compiler_params=_params(("parallel", "parallel", "arbitrary")),
    )(a, b, *extras)
    return outs[0] if n_out == 1 else outs


def _rms_fwd(x, gains, name, ts=256):
    S, D = x.shape
    ts = min(ts, S)
    n = len(gains)

    def body(*refs):
        x_ref, g_refs, o_refs = refs[0], refs[1:1 + n], refs[1 + n:]
        xv = x_ref[...]
        xh = xv * _rinv(xv)
        for g_ref, o_ref in zip(g_refs, o_refs):
            o_ref[...] = (xh * g_ref[...]).astype(BF16)

    row = pl.BlockSpec((ts, D), lambda i: (i, 0))
    vec = pl.BlockSpec((1, D), lambda i: (0, 0))
    return _pcall(body, name=name, grid=(S // ts,), in_specs=[row] + [vec] * n, out_specs=[row] * n,
                  out_shape=[jax.ShapeDtypeStruct((S, D), BF16)] * n, compiler_params=_params(("parallel",)))(x, *gains)


def _rms_bwd(x, dres, gains, dns, name, ts=256):
    S, D = x.shape
    ts = min(ts, S)
    n = len(gains)

    def body(*refs):
        x_ref, dres_ref = refs[0], refs[1]
        g_refs, dn_refs = refs[2:2 + n], refs[2 + n:2 + 2 * n]
        dx_ref, dg_refs = refs[2 + 2 * n], refs[3 + 2 * n:]
        xv = x_ref[...]
        r = _rinv(xv)
        xh = xv * r
        dx = dres_ref[...]
        first = pl.program_id(0) == 0
        for g_ref, dn_ref, dg_ref in zip(g_refs, dn_refs, dg_refs):
            dn = dn_ref[...].astype(F32)
            _accumulate(dg_ref, jnp.sum(dn * xh, axis=0, keepdims=True), first)
            dxh = dn * g_ref[...]
            dx = dx + r * (dxh - xh * jnp.mean(dxh * xh, axis=-1, keepdims=True))
        dx_ref[...] = dx

    row = pl.BlockSpec((ts, D), lambda i: (i, 0))
    vec = pl.BlockSpec((1, D), lambda i: (0, 0))
    outs = _pcall(body, name=name, grid=(S // ts,), in_specs=[row, row] + [vec] * n + [row] * n,
                  out_specs=[row] + [vec] * n,
                  out_shape=[jax.ShapeDtypeStruct((S, D), F32)] + [jax.ShapeDtypeStruct((1, D), F32)] * n,
                  compiler_params=_params(("arbitrary",)))(x, dres, *gains, *dns)
    return outs[0], outs[1:]


def _loss_head(h, tgt, name, ts=256):
    S, D = h.shape
    ts = min(ts, S)

    def body(h_ref, t_ref, dh_ref, loss_ref):
        err = h_ref[...] - t_ref[...]
        dh_ref[...] = err * (1.0 / D)
        part = 0.5 * jnp.sum(jnp.mean(err * err, axis=-1, keepdims=True), axis=0, keepdims=True)
        _accumulate(loss_ref, part, pl.program_id(0) == 0)

    row = pl.BlockSpec((ts, D), lambda i: (i, 0))
    return _pcall(body, name=name, grid=(S // ts,), in_specs=[row, row],
                  out_specs=[row, pl.BlockSpec((1, 1), lambda i: (0, 0))],
                  out_shape=[jax.ShapeDtypeStruct((S, D), F32), jax.ShapeDtypeStruct((1, 1), F32)],
                  compiler_params=_params(("arbitrary",)))(h, tgt)


def _gate_fwd(zt, bf, name):
    H, S = zt.shape
    nb = S // 128

    def body(z_ref, b_ref, c_ref):
        z = z_ref[...] + b_ref[...]
        lf = jnp.minimum(z, 0.0) - jnp.log(1.0 + jnp.exp(-jnp.abs(z)))
        upper = (lax.broadcasted_iota(jnp.int32, (128, 128), 0) <= lax.broadcasted_iota(jnp.int32, (128, 128), 1)).astype(F32)
        carry = jnp.zeros((H, 1), F32)
        for blk in range(nb):
            cs = _dot(lf[:, blk * 128:(blk + 1) * 128], upper, NN, precision=lax.Precision.HIGHEST) + carry
            c_ref[:, blk * 128:(blk + 1) * 128] = cs
            carry = cs[:, 127:128]

    return _pcall(body, name=name, in_specs=[VMEM_SPEC, VMEM_SPEC], out_specs=VMEM_SPEC,
                  out_shape=jax.ShapeDtypeStruct((H, S), F32))(zt, bf)


def _gate_bwd(dct, zt, bf, name):
    H, S = zt.shape
    nb = S // 128

    def body(dc_ref, z_ref, b_ref, dz_ref, db_ref):
        z = z_ref[...] + b_ref[...]
        e = jnp.exp(-jnp.abs(z))
        sig_neg = jnp.where(z >= 0, e, 1.0) / (1.0 + e)
        lower = (lax.broadcasted_iota(jnp.int32, (128, 128), 0) >= lax.broadcasted_iota(jnp.int32, (128, 128), 1)).astype(F32)
        dc = dc_ref[...]
        carry = jnp.zeros((H, 1), F32)
        db = jnp.zeros((H, 1), F32)
        for blk in reversed(range(nb)):
            sl = slice(blk * 128, (blk + 1) * 128)
            dlf = _dot(dc[:, sl], lower, NN, precision=lax.Precision.HIGHEST) + carry
            carry = dlf[:, 0:1]
            dz = dlf * sig_neg[:, sl]
            dz_ref[:, sl] = dz
            db = db + jnp.sum(dz, axis=1, keepdims=True)
        db_ref[...] = db

    return _pcall(body, name=name, in_specs=[VMEM_SPEC] * 3, out_specs=[VMEM_SPEC] * 2,
                  out_shape=[jax.ShapeDtypeStruct((H, S), F32), jax.ShapeDtypeStruct((H, 1), F32)])(dct, zt, bf)


def _lane_is_a():
    return lax.broadcasted_iota(jnp.int32, (1, LANES), 1) < HEAD_DIM


def _per_head_mean(x, is_a):
    sa = jnp.sum(jnp.where(is_a, x, 0.0), axis=-1, keepdims=True)
    sb = jnp.sum(jnp.where(is_a, 0.0, x), axis=-1, keepdims=True)
    return jnp.where(is_a, sa, sb) / HEAD_DIM


def _pair_norm(raw, gain, is_a):
    return raw * lax.rsqrt(_per_head_mean(raw * raw, is_a) + NORM_EPS) * gain


def _pair_norm_bwd(raw, gain, dnormed, is_a):
    r = lax.rsqrt(_per_head_mean(raw * raw, is_a) + NORM_EPS)
    xh = raw * r
    dgain = jnp.sum(dnormed * xh, axis=0, keepdims=True)
    dxh = dnormed * gain
    return r * (dxh - xh * _per_head_mean(dxh * xh, is_a)), dgain


def _fold_heads(x):
    i = lax.broadcasted_iota(jnp.int32, (LANES, LANES), 0)
    j = lax.broadcasted_iota(jnp.int32, (LANES, LANES), 1)
    fold = ((i == j) | (i == j + HEAD_DIM) | (i + HEAD_DIM == j)).astype(F32)
    return _dot(x, fold, NN, precision=lax.Precision.HIGHEST)


def _fold_row(ref):
    ref[...] = _fold_heads(jnp.broadcast_to(ref[...], (8, LANES)))[0:1, :]


def _tri_mask(t, keys_on_rows):
    r = lax.broadcasted_iota(jnp.int32, (t, t), 0)
    c = lax.broadcasted_iota(jnp.int32, (t, t), 1)
    return (r <= c) if keys_on_rows else (r >= c)


def _fox_fwd(proj, c_col, c_row, gq2, gk2, n_heads, name, t=256):
    S = proj.shape[0]
    H = n_heads
    P = H // 2
    t = min(t, S)
    nq = S // t

    def body(q_ref, k_ref, v_ref, cc_ref, cr_ref, gq_ref, gk_ref, o_ref, lse_ref, qs_s, kb_s, vb_s):
        is_a = _lane_is_a()
        qn = _pair_norm(q_ref[...], gq_ref[...], is_a) * 0.125
        qs_s[0] = jnp.where(is_a, qn, 0.0).astype(BF16)
        qs_s[1] = jnp.where(is_a, 0.0, qn).astype(BF16)
        kb_s[...] = _pair_norm(k_ref[...], gk_ref[...], is_a).astype(BF16)
        vb_s[...] = v_ref[...].astype(BF16)
        causal = _tri_mask(t, False)
        for i in range(nq):
            t0 = i * t
            rows = slice(t0, t0 + t)
            o_pair = None
            for a in range(2):
                qi = qs_s[a, rows, :]
                ci = cc_ref[a, rows, :]
                s_d = jnp.where(causal, _dot(qi, kb_s[rows, :], NT) + ci - cr_ref[a, :, rows], NEG)
                m = jnp.max(s_d, axis=-1, keepdims=True)
                if i > 0:
                    s_l = _dot(qi, kb_s[0:t0, :], NT) + ci - cr_ref[a, :, 0:t0]
                    m = jnp.maximum(m, jnp.max(s_l, axis=-1, keepdims=True))
                p_d = jnp.exp(s_d - m)
                l = jnp.sum(p_d, axis=-1, keepdims=True)
                acc = _dot(p_d.astype(BF16), vb_s[rows, :], NN)
                if i > 0:
                    p_l = jnp.exp(s_l - m)
                    l = l + jnp.sum(p_l, axis=-1, keepdims=True)
                    acc = acc + _dot(p_l.astype(BF16), vb_s[0:t0, :], NN)
                o_a = acc / l
                lse_ref[a, rows, :] = m + jnp.log(l)
                o_pair = o_a if a == 0 else jnp.where(is_a, o_pair, o_a)
            o_ref[rows, :] = o_pair.astype(BF16)

    def cols(off):
        return pl.BlockSpec((S, LANES), lambda p: (0, off + p))

    col = pl.BlockSpec((2, S, 1), lambda p: (p, 0, 0))
    rowv = pl.BlockSpec((2, 1, S), lambda p: (p, 0, 0))
    gain = pl.BlockSpec((1, LANES), lambda p: (0, 0))
    return _pcall(body, name=name, grid=(P,), in_specs=[cols(0), cols(P), cols(2 * P), col, rowv, gain, gain],
                  out_specs=[cols(0), col],
                  out_shape=[jax.ShapeDtypeStruct((S, H * HEAD_DIM), BF16), jax.ShapeDtypeStruct((H, S, 1), F32)],
                  scratch_shapes=[pltpu.VMEM((2, S, LANES), BF16), pltpu.VMEM((S, LANES), BF16), pltpu.VMEM((S, LANES), BF16)],
                  compiler_params=_params(("parallel",)))(proj, proj, proj, c_col, c_row, gq2, gk2)


def _fox_bwd(proj, c_col, c_row, gq2, gk2, lse_row, do, n_heads, name, t=256):
    S = proj.shape[0]
    H = n_heads
    P = H // 2
    t = min(t, S)
    nq = S // t
    assert t % LANES == 0

    def body(q_ref, k_ref, v_ref, cc_ref, cr_ref, gq_ref, gk_ref, lr_ref, do_ref,
             dq_ref, dk_ref, dv_ref, dc_ref, dgq_ref, dgk_ref,
             qs_s, kb_s, kt_s, vb_s, dob_s, dq_s, dk_s, dv_s, dcs_s):
        is_a = _lane_is_a()
        qn = _pair_norm(q_ref[...], gq_ref[...], is_a) * 0.125
        qs_s[0] = jnp.where(is_a, qn, 0.0).astype(BF16)
        qs_s[1] = jnp.where(is_a, 0.0, qn).astype(BF16)
        kn = _pair_norm(k_ref[...], gk_ref[...], is_a)
        kb_s[...] = kn.astype(BF16)
        kt_s[0] = jnp.where(is_a, kn, 0.0).T.astype(BF16)
        kt_s[1] = jnp.where(is_a, 0.0, kn).T.astype(BF16)
        vb_s[...] = v_ref[...].astype(BF16)
        dov = do_ref[...]
        dob_s[0] = jnp.where(is_a, dov, 0.0).astype(BF16)
        dob_s[1] = jnp.where(is_a, 0.0, dov).astype(BF16)
        dk_s[...] = jnp.zeros((S, LANES), F32)
        dv_s[...] = jnp.zeros((S, LANES), F32)
        dcs_s[...] = jnp.zeros((2, S, LANES), F32)
        causal = _tri_mask(t, True)
        for i in range(nq):
            t0 = i * t
            rows = slice(t0, t0 + t)
            dq_t = jnp.zeros((LANES, t), F32)
            for a in range(2):
                qi = qs_s[a, rows, :]
                doi = dob_s[a, rows, :]
                cri = cr_ref[a, :, rows]
                lri = lr_ref[a, :, rows]

                def probs(keys, masked, a=a, qi=qi, doi=doi, cri=cri, lri=lri):
                    p_t = jnp.exp(_dot(kb_s[keys, :], qi, NT) + cri - cc_ref[a, keys, :] - lri)
                    if masked:
                        p_t = jnp.where(causal, p_t, 0.0)
                    return p_t, _dot(vb_s[keys, :], doi, NT)

                parts = [(rows,) + probs(rows, True)]
                if i > 0:
                    parts.append((slice(0, t0),) + probs(slice(0, t0), False))
                delta = sum(jnp.sum(p_t * dp_t, axis=0, keepdims=True) for _, p_t, dp_t in parts)
                for keys, p_t, dp_t in parts:
                    ds_t = p_t * (dp_t - delta)
                    dsb = ds_t.astype(BF16)
                    dv_s[keys, :] += _dot(p_t.astype(BF16), doi, NN)
                    dk_s[keys, :] += _dot(dsb, qi, NN)
                    dq_t = dq_t + _dot(kt_s[a, :, keys], dsb, NN)
                    dcs_s[a, keys, :] += sum(ds_t[:, b * LANES:(b + 1) * LANES] for b in range(t // LANES))
            dq_s[rows, :] = dq_t.T
        first = pl.program_id(0) == 0
        last = pl.program_id(0) == P - 1
        dq_raw, dgq = _pair_norm_bwd(q_ref[...], gq_ref[...], dq_s[...] * 0.125, is_a)
        dq_ref[...] = dq_raw.astype(BF16)
        _accumulate(dgq_ref, dgq, first)
        dk_raw, dgk = _pair_norm_bwd(k_ref[...], gk_ref[...], dk_s[...], is_a)
        dk_ref[...] = dk_raw.astype(BF16)
        _accumulate(dgk_ref, dgk, first)
        dv_ref[...] = dv_s[...].astype(BF16)
        for a in range(2):
            dc_ref[a] = -jnp.sum(dcs_s[a], axis=1, keepdims=True)

        @pl.when(last)
        def _():
            _fold_row(dgq_ref)
            _fold_row(dgk_ref)

    def cols(off):
        return pl.BlockSpec((S, LANES), lambda p: (0, off + p))

    col = pl.BlockSpec((2, S, 1), lambda p: (p, 0, 0))
    rowv = pl.BlockSpec((2, 1, S), lambda p: (p, 0, 0))
    gain = pl.BlockSpec((1, LANES), lambda p: (0, 0))
    wide = jax.ShapeDtypeStruct((S, H * HEAD_DIM), BF16)
    gs = jax.ShapeDtypeStruct((1, LANES), F32)
    return _pcall(body, name=name, grid=(P,),
                  in_specs=[cols(0), cols(P), cols(2 * P), col, rowv, gain, gain, rowv, cols(0)],
                  out_specs=[cols(0), cols(0), cols(0), col, gain, gain],
                  out_shape=[wide, wide, wide, jax.ShapeDtypeStruct((H, S, 1), F32), gs, gs],
                  scratch_shapes=[pltpu.VMEM((2, S, LANES), BF16), pltpu.VMEM((S, LANES), BF16), pltpu.VMEM((2, LANES, S), BF16),
                                  pltpu.VMEM((S, LANES), BF16), pltpu.VMEM((2, S, LANES), BF16)]
                  + [pltpu.VMEM((S, LANES), F32)] * 3 + [pltpu.VMEM((2, S, LANES), F32)],
                  compiler_params=_params(("arbitrary",)))(proj, proj, proj, c_col, c_row, gq2, gk2, lse_row, do)


def _bucket_onehot():
    W = WINDOW
    dist = np.arange(W)[:, None] + W - np.arange(2 * W)[None, :]
    n = np.maximum(dist, 0)
    max_exact = N_BUCKETS // 2
    large = max_exact + (np.log(np.maximum(n, 1) / max_exact) / np.log(REL_MAX_DIST / max_exact)
                         * (N_BUCKETS - max_exact)).astype(np.int32)
    large = np.minimum(large, N_BUCKETS - 1)
    bucket = np.where(n < max_exact, n, large).astype(np.int32)
    valid = (dist >= 0) & (dist < W)
    onehot = (bucket[None] == np.arange(N_BUCKETS)[:, None, None]) & valid[None]
    return onehot.reshape(N_BUCKETS, W * 2 * W).astype(np.float32)


def _bias_expand(rel_bias_t, onehot, name, tn=4096):
    HQ, NB = rel_bias_t.shape
    L = onehot.shape[1]

    def body(r_ref, oh_ref, out_ref):
        out_ref[...] = _dot(r_ref[...], oh_ref[...].astype(F32), NN, precision=lax.Precision.HIGHEST)

    return _pcall(body, name=name, grid=(L // tn,),
                  in_specs=[pl.BlockSpec((HQ, NB), lambda i: (0, 0)), pl.BlockSpec((NB, tn), lambda i: (0, i))],
                  out_specs=pl.BlockSpec((HQ, tn), lambda i: (0, i)),
                  out_shape=jax.ShapeDtypeStruct((HQ, L), F32), compiler_params=_params(("parallel",)))(rel_bias_t, onehot)


def _bias_reduce(dbias, onehot, name, tk=4096):
    HQ, L = dbias.shape
    NB = onehot.shape[0]

    def body(d_ref, oh_ref, out_ref):
        part = _dot(d_ref[...], oh_ref[...].astype(F32), NT, precision=lax.Precision.HIGHEST)
        _accumulate(out_ref, part, pl.program_id(0) == 0)

    return _pcall(body, name=name, grid=(L // tk,),
                  in_specs=[pl.BlockSpec((HQ, tk), lambda i: (0, i)), pl.BlockSpec((NB, tk), lambda i: (0, i))],
                  out_specs=pl.BlockSpec((HQ, NB), lambda i: (0, 0)),
                  out_shape=jax.ShapeDtypeStruct((HQ, NB), F32), compiler_params=_params(("arbitrary",)))(dbias, onehot)


def _stacked_query_index(n_rows_or_cols_axis, shape):
    idx = lax.broadcasted_iota(jnp.int32, shape, n_rows_or_cols_axis)
    return jnp.where(idx >= WINDOW, idx - WINDOW, idx)


def _swa_fwd(qproj, kk, vv, bias_ab, sink_col, gq2, gk2, name):
    S, HQD = qproj.shape
    KVH = kk.shape[0]
    PP = HQD // LANES
    NP = PP // KVH
    W = WINDOW
    nb = S // W

    def body(q_ref, k_ref, v_ref, bias_ref, sink_ref, gq_ref, gk_ref, o_ref, lse_ref, qs_s, kb_s, vb_s):
        is_a = _lane_is_a()
        qn = _pair_norm(q_ref[...], gq_ref[...], is_a) * 0.125
        qs_s[0] = jnp.where(is_a, qn, 0.0).astype(BF16)
        qs_s[1] = jnp.where(is_a, 0.0, qn).astype(BF16)
        kb_s[...] = _pair_norm(k_ref[...], gk_ref[...], is_a).astype(BF16)
        vb_s[...] = v_ref[...].astype(BF16)
        sink = sink_ref[...]
        qi1 = _stacked_query_index(0, (2 * W, W))
        first_valid = lax.broadcasted_iota(jnp.int32, (2 * W, W), 1) <= qi1
        qi2 = _stacked_query_index(0, (2 * W, 2 * W))
        key2 = lax.broadcasted_iota(jnp.int32, (2 * W, 2 * W), 1)
        band_valid = (key2 > qi2) & (key2 <= qi2 + W)
        for n in range(nb):
            rows = slice(n * W, (n + 1) * W)
            keys = slice(0, W) if n == 0 else slice((n - 1) * W, (n + 1) * W)
            lhs = jnp.concatenate([qs_s[0, rows, :], qs_s[1, rows, :]], axis=0)
            s = _dot(lhs, kb_s[keys, :], NT) + (bias_ref[:, W:2 * W] if n == 0 else bias_ref[...])
            s = jnp.where(first_valid if n == 0 else band_valid, s, NEG)
            m = jnp.maximum(jnp.max(s, axis=-1, keepdims=True), sink)
            e = jnp.exp(s - m)
            l = jnp.sum(e, axis=-1, keepdims=True) + jnp.exp(sink - m)
            o_ab = _dot(e.astype(BF16), vb_s[keys, :], NN) / l
            o_ref[rows, :] = jnp.where(is_a, o_ab[0:W, :], o_ab[W:2 * W, :]).astype(BF16)
            lse_ref[n] = m + jnp.log(l)

    qcols = pl.BlockSpec((S, LANES), lambda a, g: (0, a * NP + g))
    kvs = pl.BlockSpec((None, S, LANES), lambda a, g: (a, 0, 0))
    gain = pl.BlockSpec((1, LANES), lambda a, g: (0, 0))
    return _pcall(body, name=name, grid=(KVH, NP),
                  in_specs=[qcols, kvs, kvs, pl.BlockSpec((None, 2 * W, 2 * W), lambda a, g: (a * NP + g, 0, 0)),
                            pl.BlockSpec((None, 2 * W, 1), lambda a, g: (a * NP + g, 0, 0)), gain, gain],
                  out_specs=[qcols, pl.BlockSpec((None, nb, 2 * W, 1), lambda a, g: (a * NP + g, 0, 0, 0))],
                  out_shape=[jax.ShapeDtypeStruct((S, HQD), BF16), jax.ShapeDtypeStruct((PP, nb, 2 * W, 1), F32)],
                  scratch_shapes=[pltpu.VMEM((2, S, LANES), BF16), pltpu.VMEM((S, LANES), BF16), pltpu.VMEM((S, LANES), BF16)],
                  compiler_params=_params(("parallel", "parallel")))(qproj, kk, vv, bias_ab, sink_col, gq2, gk2)


def _swa_bwd(qproj, kk, vv, bias_t_ab, sink_row, gq2, gk2, lse_row, do, name):
    S, HQD = qproj.shape
    KVH = kk.shape[0]
    PP = HQD // LANES
    NP = PP // KVH
    W = WINDOW
    nb = S // W

    def body(q_ref, k_ref, v_ref, bias_ref, sink_ref, gq_ref, gk_ref, lr_ref, do_ref,
             dq_ref, dk_ref, dv_ref, db_ref, dsink_ref, dgq_ref, dgk_ref,
             qs_s, kb_s, kt_s, vb_s, dob_s, dq_s, dk_s, dv_s):
        a, g = pl.program_id(0), pl.program_id(1)
        is_a = _lane_is_a()
        qn = _pair_norm(q_ref[...], gq_ref[...], is_a) * 0.125
        qs_s[0] = jnp.where(is_a, qn, 0.0).astype(BF16)
        qs_s[1] = jnp.where(is_a, 0.0, qn).astype(BF16)
        kn = _pair_norm(k_ref[...], gk_ref[...], is_a)
        kb_s[...] = kn.astype(BF16)
        kt_s[...] = kn.T.astype(BF16)
        vb_s[...] = v_ref[...].astype(BF16)
        dov = do_ref[...]
        dob_s[0] = jnp.where(is_a, dov, 0.0).astype(BF16)
        dob_s[1] = jnp.where(is_a, 0.0, dov).astype(BF16)
        sink = sink_ref[...]

        @pl.when(g == 0)
        def _():
            dk_s[...] = jnp.zeros((S, LANES), F32)
            dv_s[...] = jnp.zeros((S, LANES), F32)

        qi1 = _stacked_query_index(1, (W, 2 * W))
        first_valid = lax.broadcasted_iota(jnp.int32, (W, 2 * W), 0) <= qi1
        qi2 = _stacked_query_index(1, (2 * W, 2 * W))
        key2 = lax.broadcasted_iota(jnp.int32, (2 * W, 2 * W), 0)
        band_valid = (key2 > qi2) & (key2 <= qi2 + W)
        head_rows = lax.broadcasted_iota(jnp.int32, (LANES, W), 0) < HEAD_DIM
        db = jnp.zeros((2 * W, 2 * W), F32)
        dsk = jnp.zeros((1, 2 * W), F32)
        pend_k = pend_v = None
        for n in range(nb):
            rows = slice(n * W, (n + 1) * W)
            keys = slice(0, W) if n == 0 else slice((n - 1) * W, (n + 1) * W)
            lhs_q = jnp.concatenate([qs_s[0, rows, :], qs_s[1, rows, :]], axis=0)
            lhs_do = jnp.concatenate([dob_s[0, rows, :], dob_s[1, rows, :]], axis=0)
            lse = lr_ref[n]
            s_t = _dot(kb_s[keys, :], lhs_q, NT) + (bias_ref[W:2 * W, :] if n == 0 else bias_ref[...])
            p_t = jnp.where(first_valid if n == 0 else band_valid, jnp.exp(s_t - lse), 0.0)
            dp_t = _dot(vb_s[keys, :], lhs_do, NT)
            delta = jnp.sum(p_t * dp_t, axis=0, keepdims=True)
            ds_t = p_t * (dp_t - delta)
            dsb = ds_t.astype(BF16)
            dsk = dsk - jnp.exp(sink - lse) * delta
            dv_band = _dot(p_t.astype(BF16), lhs_do, NN)
            dk_band = _dot(dsb, lhs_q, NN)
            dq_t = _dot(kt_s[:, keys], dsb, NN)
            dq_s[rows, :] = jnp.where(head_rows, dq_t[:, 0:W], dq_t[:, W:2 * W]).T
            if n == 0:
                db = jnp.concatenate([jnp.zeros((W, 2 * W), F32), ds_t], axis=0)
                pend_k, pend_v = dk_band, dv_band
            else:
                db = db + ds_t
                prev = slice((n - 1) * W, n * W)
                dk_s[prev, :] += pend_k + dk_band[0:W, :]
                dv_s[prev, :] += pend_v + dv_band[0:W, :]
                pend_k, pend_v = dk_band[W:2 * W, :], dv_band[W:2 * W, :]
        tail = slice((nb - 1) * W, nb * W)
        dk_s[tail, :] += pend_k
        dv_s[tail, :] += pend_v
        db_ref[...] = db
        dsink_ref[0] = jnp.broadcast_to(jnp.sum(dsk[:, 0:W], axis=1, keepdims=True), (1, LANES))
        dsink_ref[1] = jnp.broadcast_to(jnp.sum(dsk[:, W:2 * W], axis=1, keepdims=True), (1, LANES))
        dq_raw, dgq = _pair_norm_bwd(q_ref[...], gq_ref[...], dq_s[...] * 0.125, is_a)
        dq_ref[...] = dq_raw.astype(BF16)
        _accumulate(dgq_ref, dgq, jnp.logical_and(a == 0, g == 0))

        @pl.when(jnp.logical_and(a == KVH - 1, g == NP - 1))
        def _():
            _fold_row(dgq_ref)

        @pl.when(g == NP - 1)
        def _():
            dk_raw, dgk = _pair_norm_bwd(k_ref[...], gk_ref[...], _fold_heads(dk_s[...]), is_a)
            dk_ref[...] = dk_raw
            _accumulate(dgk_ref, dgk, a == 0)
            dv_ref[...] = _fold_heads(dv_s[...])

    qcols = pl.BlockSpec((S, LANES), lambda a, g: (0, a * NP + g))
    kvs = pl.BlockSpec((None, S, LANES), lambda a, g: (a, 0, 0))
    sq = pl.BlockSpec((None, 2 * W, 2 * W), lambda a, g: (a * NP + g, 0, 0))
    gain = pl.BlockSpec((1, LANES), lambda a, g: (0, 0))
    ks = jax.ShapeDtypeStruct((KVH, S, LANES), F32)
    gs = jax.ShapeDtypeStruct((1, LANES), F32)
    return _pcall(body, name=name, grid=(KVH, NP),
                  in_specs=[qcols, kvs, kvs, sq, pl.BlockSpec((None, 1, 2 * W), lambda a, g: (a * NP + g, 0, 0)), gain, gain,
                            pl.BlockSpec((None, nb, 1, 2 * W), lambda a, g: (a * NP + g, 0, 0, 0)), qcols],
                  out_specs=[qcols, kvs, kvs, sq, pl.BlockSpec((2, 1, LANES), lambda a, g: (a * NP + g, 0, 0)), gain, gain],
                  out_shape=[jax.ShapeDtypeStruct((S, HQD), BF16), ks, ks, jax.ShapeDtypeStruct((PP, 2 * W, 2 * W), F32),
                             jax.ShapeDtypeStruct((2 * PP, 1, LANES), F32), gs, gs],
                  scratch_shapes=[pltpu.VMEM((2, S, LANES), BF16), pltpu.VMEM((S, LANES), BF16), pltpu.VMEM((LANES, S), BF16),
                                  pltpu.VMEM((S, LANES), BF16), pltpu.VMEM((2, S, LANES), BF16)] + [pltpu.VMEM((S, LANES), F32)] * 3,
                  compiler_params=_params(("arbitrary", "arbitrary")))(qproj, kk, vv, bias_t_ab, sink_row, gq2, gk2, lse_row, do)


def _adamw(w, g, m, v, name, tr=256):
    R, C = w.shape
    tr = min(tr, R)
    assert R % tr == 0

    def body(w_ref, g_ref, m_ref, v_ref, d_ref, m2_ref, v2_ref):
        gv = g_ref[...]
        m2 = ADAM_B1 * m_ref[...] + (1.0 - ADAM_B1) * gv
        v2 = ADAM_B2 * v_ref[...] + (1.0 - ADAM_B2) * jnp.square(gv)
        m_hat = m2 / (1.0 - ADAM_B1 ** ADAM_STEP)
        v_hat = v2 / (1.0 - ADAM_B2 ** ADAM_STEP)
        d_ref[...] = -ADAM_LR * (m_hat / (jnp.sqrt(v_hat) + ADAM_EPS) + ADAM_WD * w_ref[...])
        m2_ref[...] = m2
        v2_ref[...] = v2

    blk = pl.BlockSpec((tr, C), lambda i: (i, 0))
    return _pcall(body, name=name, grid=(R // tr,), in_specs=[blk] * 4, out_specs=[blk] * 3,
                  out_shape=[jax.ShapeDtypeStruct((R, C), F32)] * 3, compiler_params=_params(("parallel",)))(w, g, m, v)


def _sum_parts(parts, name, out_dtype, tr=128):
    P, R, C = parts.shape
    tr = min(tr, R)
    assert R % tr == 0, (R, tr)

    def body(p_ref, o_ref):
        acc = p_ref[0].astype(F32)
        for k in range(1, P):
            acc = acc + p_ref[k].astype(F32)
        o_ref[...] = acc.astype(o_ref.dtype)

    return _pcall(body, name=name, grid=(R // tr,), in_specs=[pl.BlockSpec((P, tr, C), lambda i: (0, i, 0))],
                  out_specs=pl.BlockSpec((tr, C), lambda i: (i, 0)),
                  out_shape=jax.ShapeDtypeStruct((R, C), out_dtype), compiler_params=_params(("parallel",)))(parts)


def _place():
    x, y, c = lax.axis_index("x"), lax.axis_index("y"), lax.axis_index("c")
    others = [(1 - x, y), (x, 1 - y), (1 - x, 1 - y)]
    return x, y, c, others


def _allgather_weights(pk, name):
    R, C = pk.shape
    Rh = R // 2

    def body(pk_ref, out_ref, send_sems, recv_sems, local_sem):
        x, y, c, others = _place()
        me = 2 * x + y
        sibling = (x, y, 1 - c)

        def half(chip, hh):
            return out_ref.at[chip, pl.ds(pl.multiple_of(hh * Rh, PACK_ROW_ALIGN // 2), Rh), :]

        def copy(k, src, dst, to):
            return pltpu.make_async_remote_copy(src_ref=src, dst_ref=dst, send_sem=send_sems.at[k], recv_sem=recv_sems.at[k],
                                                device_id=to, device_id_type=MESH)

        mine = pltpu.make_async_copy(pk_ref, out_ref.at[me], local_sem)
        mine.start()
        my_half = pk_ref.at[pl.ds(pl.multiple_of(c * Rh, PACK_ROW_ALIGN // 2), Rh), :]
        first = [copy(k, my_half, half(me, c), (cx, cy, c)) for k, (cx, cy) in enumerate(others)]
        for cp in first:
            cp.start()
        passed = [copy(3 + k, half(2 * cx + cy, c), half(2 * cx + cy, c), sibling) for k, (cx, cy) in enumerate(others)]
        for k, (cx, cy) in enumerate(others):
            copy(k, my_half, half(2 * cx + cy, c), sibling).wait_recv()
            passed[k].start()
        for k, (cx, cy) in enumerate(others):
            copy(3 + k, my_half, half(2 * cx + cy, 1 - c), sibling).wait_recv()
        for cp in first + passed:
            cp.wait_send()
        mine.wait()

    return _pcall(body, name=name, in_specs=[HBM_SPEC], out_specs=HBM_SPEC,
                  out_shape=jax.ShapeDtypeStruct((N_CHIPS, R, C), pk.dtype),
                  scratch_shapes=[pltpu.SemaphoreType.DMA((6,)), pltpu.SemaphoreType.DMA((6,)), pltpu.SemaphoreType.DMA])(pk)


def _swap_sibling(buf, name):
    def body(src_ref, dst_ref, send_sem, recv_sem):
        x, y, c, _ = _place()
        cp = pltpu.make_async_remote_copy(src_ref=src_ref, dst_ref=dst_ref, send_sem=send_sem, recv_sem=recv_sem,
                                          device_id=(x, y, 1 - c), device_id_type=MESH)
        cp.start()
        cp.wait()

    return _pcall(body, name=name, in_specs=[HBM_SPEC], out_specs=HBM_SPEC,
                  out_shape=jax.ShapeDtypeStruct(buf.shape, buf.dtype),
                  scratch_shapes=[pltpu.SemaphoreType.DMA, pltpu.SemaphoreType.DMA])(buf)


def _scatter_to_chips(parts, name):
    P, R, C = parts.shape

    def body(src_ref, out_ref, send_sems, recv_sems, local_sem):
        x, y, c, others = _place()
        me = 2 * x + y
        mine = pltpu.make_async_copy(src_ref.at[me], out_ref.at[me], local_sem)
        mine.start()

        def copy(k, src_chip, dst_chip, to):
            return pltpu.make_async_remote_copy(src_ref=src_ref.at[src_chip], dst_ref=out_ref.at[dst_chip],
                                                send_sem=send_sems.at[k], recv_sem=recv_sems.at[k], device_id=to, device_id_type=MESH)

        sends = [copy(k, 2 * cx + cy, me, (cx, cy, c)) for k, (cx, cy) in enumerate(others)]
        for cp in sends:
            cp.start()
        for k, (cx, cy) in enumerate(others):
            copy(k, me, 2 * cx + cy, (cx, cy, c)).wait_recv()
        for cp in sends:
            cp.wait_send()
        mine.wait()

    return _pcall(body, name=name, in_specs=[HBM_SPEC], out_specs=HBM_SPEC,
                  out_shape=jax.ShapeDtypeStruct(parts.shape, parts.dtype),
                  scratch_shapes=[pltpu.SemaphoreType.DMA((3,)), pltpu.SemaphoreType.DMA((3,)), pltpu.SemaphoreType.DMA])(parts)


def _allgather_small(blk, name):
    M, C = blk.shape

    def body(x_ref, out_ref, send_sems, recv_sems, local_sem):
        x, y, c, others = _place()
        me, sibling = (x, y, c), (x, y, 1 - c)

        def rows(px, py, pc):
            return out_ref.at[4 * px + 2 * py + pc]

        def copy(k, block, to, src=None):
            return pltpu.make_async_remote_copy(src_ref=rows(*block) if src is None else src, dst_ref=rows(*block),
                                                send_sem=send_sems.at[k], recv_sem=recv_sems.at[k], device_id=to, device_id_type=MESH)

        mine = pltpu.make_async_copy(x_ref, rows(*me), local_sem)
        mine.start()
        first = [copy(0, me, sibling, src=x_ref)]
        first += [copy(1 + j, me, (*chip, c), src=x_ref) for j, chip in enumerate(others)]
        for cp in first:
            cp.start()
        passed = [copy(4 + j, (*chip, c), sibling) for j, chip in enumerate(others)]
        for j, chip in enumerate(others):
            copy(1 + j, (*chip, c), me).wait_recv()
            passed[j].start()
        copy(0, sibling, me).wait_recv()
        for j, chip in enumerate(others):
            copy(4 + j, (*chip, 1 - c), me).wait_recv()
        for cp in first + passed:
            cp.wait_send()
        mine.wait()

    return _pcall(body, name=name, in_specs=[VMEM_SPEC], out_specs=VMEM_SPEC,
                  out_shape=jax.ShapeDtypeStruct((8, M, C), blk.dtype),
                  scratch_shapes=[pltpu.SemaphoreType.DMA((7,)), pltpu.SemaphoreType.DMA((7,)), pltpu.SemaphoreType.DMA])(blk)


def _pack_rows(n_elems, width=PACK_W, align=PACK_ROW_ALIGN):
    rows = -(-n_elems // width)
    return -(-rows // align) * align


def _pack(arrays, dtype, width=PACK_W, align=PACK_ROW_ALIGN):
    flat = jnp.concatenate([a.astype(dtype).reshape(-1) for a in arrays])
    rows = _pack_rows(flat.shape[0], width, align)
    flat = jnp.pad(flat, (0, rows * width - flat.shape[0]))
    return flat.reshape(rows, width)


def _pack_small(arrays):
    return _pack(arrays, F32, width=128, align=8)


def _unpack(flat, shapes):
    out, off = [], 0
    for shp in shapes:
        n = int(np.prod(shp))
        out.append(flat[..., off:off + n].reshape(flat.shape[:-1] + tuple(shp)))
        off += n
    return out


def _join(pieces, axis):
    moved = jnp.moveaxis(pieces, 0, axis)
    shp = list(moved.shape)
    return moved.reshape(shp[:axis] + [shp[axis] * shp[axis + 1]] + shp[axis + 2:])


def _split(full, axis):
    shp = list(full.shape)
    parts = full.reshape(shp[:axis] + [N_CHIPS, shp[axis] // N_CHIPS] + shp[axis + 1:])
    return jnp.moveaxis(parts, axis, 0)


def _doubled_heads(x2d, n_heads):
    S = x2d.shape[0]
    h = x2d.reshape(S, n_heads, HEAD_DIM).transpose(1, 0, 2)
    return jnp.concatenate([h, h], axis=-1)


def _mlp_fwd(h, g, w_up, w_down, tag):
    (n,) = _rms_fwd(h, [g], f"rms_mlp{tag}")
    u, a = _matmul(n, w_up, "nn", f"up{tag}", out_dtypes=(F32, BF16),
                   epilogue=lambda acc: (acc, jnp.square(jnp.maximum(acc, 0.0))))
    h_out = _matmul(a, w_down, "nn", f"down{tag}", extras=(h,), epilogue=lambda acc, res: (res + acc,), tk=1024)
    return h_out, (n, u, a)


def _mlp_bwd(dh_out, h, g, w_up, w_down, saved, tag):
    n, u, a = saved
    dw_down = _matmul(a, dh_out, "tn", f"dw_down{tag}", out_dtypes=(BF16,))
    du = _matmul(dh_out, w_down, "nt", f"du{tag}", out_dtypes=(BF16,), extras=(u,),
                 epilogue=lambda acc, uu: (acc * (2.0 * jnp.maximum(uu, 0.0)),))
    dw_up = _matmul(n, du, "tn", f"dw_up{tag}", out_dtypes=(BF16,))
    dn = _matmul(du, w_up, "nt", f"dn_mlp{tag}", tk=1024)
    dh, (dg,) = _rms_bwd(h, dh_out, [g], [dn], f"rms_mlp_bwd{tag}")
    return dh, dg, dw_up, dw_down


def kernel(x, g_attn, g_mlp, w_in_a, b_f, gq_a, gk_a, w_out_a, g_kv, w_kv, gk_b, w_q_b, gq_b, sinks, rel_bias, w_out_b, w_up, w_down, loss_target, m_g_attn, m_g_mlp, m_w_in_a, m_b_f, m_gq_a, m_gk_a, m_w_out_a, m_g_kv, m_w_kv, m_gk_b, m_w_q_b, m_gq_b, m_sinks, m_rel_bias, m_w_out_b, m_w_up, m_w_down, v_g_attn, v_g_mlp, v_w_in_a, v_b_f, v_gq_a, v_gk_a, v_w_out_a, v_g_kv, v_w_kv, v_gk_b, v_w_q_b, v_gq_b, v_sinks, v_rel_bias, v_w_out_b, v_w_up, v_w_down):
    given = dict(locals())
    S, D = x.shape[1], x.shape[2]
    H = D // HEAD_DIM
    KVH = w_kv.shape[1] // (2 * HEAD_DIM)
    kvw = KVH * HEAD_DIM
    hw = H * HEAD_DIM
    W = WINDOW
    nb = S // W
    c_idx = lax.axis_index("c")
    xs, tgt = x[0], loss_target[0]

    big_shards = [given[n] for n, _ in BIG]
    gathered = _allgather_weights(_pack(big_shards, BF16), "allgather_weights")
    pieces = _unpack(gathered.reshape(N_CHIPS, -1), [s.shape for s in big_shards])
    full = {n: _join(p, ax) for (n, ax), p in zip(BIG, pieces)}
    win = jnp.pad(full["w_in_a"][0], ((0, 0), (0, (-full["w_in_a"].shape[2]) % 128)))
    wout_a, wq_b, wout_b, wkv = full["w_out_a"][0], full["w_q_b"][0], full["w_out_b"][0], full["w_kv"]
    wup, wdown = full["w_up"], full["w_down"]
    n_in = win.shape[1]
    tile_in = 640 if n_in % 640 == 0 else 128

    vec = lambda a: a.reshape(1, -1)
    twice = lambda a: jnp.tile(a.reshape(1, -1), (1, 2))

    (n0,) = _rms_fwd(xs, [vec(g_attn[0])], "rms_attn0")
    proj = _matmul(n0, win, "nn", "proj_in", tn=tile_in)
    zt = proj[:, 3 * hw:3 * hw + H].T
    c_row = _gate_fwd(zt, b_f.reshape(H, 1), "gate_fwd")
    c_col3, c_row3 = c_row.reshape(H, S, 1), c_row.reshape(H, 1, S)
    o_a, lse_a = _fox_fwd(proj, c_col3, c_row3, twice(gq_a[0]), twice(gk_a[0]), H, "fox_fwd")
    h1 = _matmul(o_a, wout_a, "nn", "out_a", extras=(xs,), epilogue=lambda acc, res: (res + acc,))
    h2, mlp0 = _mlp_fwd(h1, vec(g_mlp[0]), wup[0], wdown[0], "0")

    nkv, n2 = _rms_fwd(h2, [vec(g_kv), vec(g_attn[1])], "rms_attn1")
    kv = _matmul(nkv, wkv, "nn", "proj_kv")
    kk, vv = _doubled_heads(kv[:, :kvw], KVH), _doubled_heads(kv[:, kvw:], KVH)
    q2 = _matmul(n2, wq_b, "nn", "proj_q")
    onehot = jnp.asarray(_bucket_onehot(), dtype=BF16)
    bias = _bias_expand(rel_bias.T, onehot, "bias_expand").reshape(H, W, 2 * W)
    bias_ab = bias.reshape(H // 2, 2 * W, 2 * W)
    bias_t_ab = bias.reshape(H // 2, 2, W, 2 * W).transpose(0, 3, 1, 2).reshape(H // 2, 2 * W, 2 * W)
    sink_ab = jnp.repeat(sinks[0].reshape(H // 2, 2), W, axis=1)
    o_b, lse_b = _swa_fwd(q2, kk, vv, bias_ab, sink_ab.reshape(H // 2, 2 * W, 1), twice(gq_b[0]), twice(gk_b), "swa_fwd")
    h3 = _matmul(o_b, wout_b, "nn", "out_b", extras=(h2,), epilogue=lambda acc, res: (res + acc,))
    h4, mlp1 = _mlp_fwd(h3, vec(g_mlp[1]), wup[1], wdown[1], "1")

    dh4, loss_part = _loss_head(h4, tgt, "loss_head")
    loss = lax.psum(loss_part[0, 0], ("x", "y", "c"))

    dh3, dg_mlp1, dw_up1, dw_down1 = _mlp_bwd(dh4, h3, vec(g_mlp[1]), wup[1], wdown[1], mlp1, "1")
    dw_out_b = _matmul(o_b, dh3, "tn", "dw_out_b", out_dtypes=(BF16,))
    do_b = _matmul(dh3, wout_b, "nt", "do_b")
    dq2, dk2, dv2, dbias_t_ab, dsink, dgq_b, dgk_b = _swa_bwd(
        q2, kk, vv, bias_t_ab, sink_ab.reshape(H // 2, 1, 2 * W), twice(gq_b[0]), twice(gk_b),
        lse_b.reshape(H // 2, nb, 1, 2 * W), do_b, "swa_bwd")
    dbias = dbias_t_ab.reshape(H // 2, 2 * W, 2, W).transpose(0, 2, 3, 1).reshape(H, W * 2 * W)
    d_rel_bias = _bias_reduce(dbias, onehot, "bias_reduce").T
    dw_q_b = _matmul(n2, dq2, "tn", "dw_q_b", out_dtypes=(BF16,))
    dn2 = _matmul(dq2, wq_b, "nt", "dn2")
    dkv = jnp.concatenate([dk2[h, :, :HEAD_DIM] for h in range(KVH)] + [dv2[h, :, :HEAD_DIM] for h in range(KVH)],
                          axis=1).astype(BF16)
    dw_kv = _matmul(nkv, dkv, "tn", "dw_kv", out_dtypes=(BF16,))
    dnkv = _matmul(dkv, wkv, "nt", "dnkv")
    dh2, (dg_kv, dg_attn1) = _rms_bwd(h2, dh3, [vec(g_kv), vec(g_attn[1])], [dnkv, dn2], "rms_attn1_bwd")

    dh1, dg_mlp0, dw_up0, dw_down0 = _mlp_bwd(dh2, h1, vec(g_mlp[0]), wup[0], wdown[0], mlp0, "0")
    dw_out_a = _matmul(o_a, dh1, "tn", "dw_out_a", out_dtypes=(BF16,))
    do_a = _matmul(dh1, wout_a, "nt", "do_a")
    dq_a, dk_a, dv_a, dc_col, dgq_a, dgk_a = _fox_bwd(
        proj, c_col3, c_row3, twice(gq_a[0]), twice(gk_a[0]), lse_a.reshape(H, 1, S), do_a, H, "fox_bwd")
    dzt, db_f = _gate_bwd(dc_col.reshape(H, S), zt, b_f.reshape(H, 1), "gate_bwd")
    dproj = jnp.concatenate([dq_a, dk_a, dv_a, dzt.T.astype(BF16), jnp.zeros((S, n_in - 3 * hw - H), BF16)], axis=1)
    dw_in = _matmul(n0, dproj, "tn", "dw_in", out_dtypes=(BF16,), tn=tile_in)
    dn0 = _matmul(dproj, win, "nt", "dn0", tk=tile_in)
    grad_x, (dg_attn0,) = _rms_bwd(xs, dh1, [vec(g_attn[0])], [dn0], "rms_attn0_bwd")

    grads_full = {
        "w_in_a": dw_in[None, :, :3 * hw + H], "w_out_a": dw_out_a[None], "w_kv": dw_kv, "w_q_b": dw_q_b[None],
        "w_out_b": dw_out_b[None], "w_up": jnp.stack([dw_up0, dw_up1]), "w_down": jnp.stack([dw_down0, dw_down1]),
    }
    split = [_split(grads_full[n], ax) for n, ax in BIG]
    gpack = jnp.concatenate([s.reshape(N_CHIPS, -1) for s in split], axis=1)
    R = _pack_rows(gpack.shape[1])
    gpack = jnp.pad(gpack, ((0, 0), (0, R * PACK_W - gpack.shape[1]))).reshape(N_CHIPS, 2, R // 2, PACK_W)
    keep = lax.dynamic_index_in_dim(gpack, c_idx, axis=1, keepdims=False)
    give = lax.dynamic_index_in_dim(gpack, 1 - c_idx, axis=1, keepdims=False)
    got = _swap_sibling(give, "swap_grad_halves")
    pair_sum = _sum_parts(jnp.stack([keep, got]).reshape(2, N_CHIPS * (R // 2), PACK_W), "sum_core_pair", BF16)
    landed = _scatter_to_chips(pair_sum.reshape(N_CHIPS, R // 2, PACK_W), "scatter_grads")
    my_half = _sum_parts(landed, "sum_chips", F32)
    their_half = _swap_sibling(my_half, "swap_reduced_halves")
    lo = jnp.where(c_idx == 0, my_half, their_half)
    hi = jnp.where(c_idx == 0, their_half, my_half)
    gshard = jnp.concatenate([lo, hi], axis=0).reshape(-1)
    big_grads = _unpack(gshard, [s.shape for s in big_shards])

    small_grads = {
        "g_attn": jnp.concatenate([dg_attn0, dg_attn1], axis=0), "g_mlp": jnp.concatenate([dg_mlp0, dg_mlp1], axis=0),
        "b_f": db_f.reshape(1, H), "gq_a": dgq_a[:, :HEAD_DIM], "gk_a": dgk_a[:, :HEAD_DIM], "g_kv": dg_kv.reshape(-1),
        "gk_b": dgk_b[0, :HEAD_DIM], "gq_b": dgq_b[:, :HEAD_DIM], "sinks": dsink[:, 0, 0].reshape(1, H), "rel_bias": d_rel_bias,
    }
    small_shapes = [given[n].shape for n in SMALL]
    spack = _pack_small([small_grads[n] for n in SMALL])
    small_sum = _sum_parts(_allgather_small(spack, "allgather_small"), "sum_small", F32, tr=spack.shape[0])
    small_red = _unpack(small_sum.reshape(-1), small_shapes)

    grads = dict(zip([n for n, _ in BIG], big_grads))
    grads.update(dict(zip(SMALL, small_red)))
    sw = _pack_small([given[n] for n in SMALL])
    sm = _pack_small([given["m_" + n] for n in SMALL])
    sv = _pack_small([given["v_" + n] for n in SMALL])
    sd, sm2, sv2 = _adamw(sw, small_sum, sm, sv, "adamw_small", tr=sw.shape[0])
    delta = dict(zip(SMALL, _unpack(sd.reshape(-1), small_shapes)))
    new_m = dict(zip(SMALL, _unpack(sm2.reshape(-1), small_shapes)))
    new_v = dict(zip(SMALL, _unpack(sv2.reshape(-1), small_shapes)))
    for n, _ in BIG:
        w = given[n]
        two_d = (-1, w.shape[-1])
        d, m2, v2 = _adamw(w.reshape(two_d), grads[n].reshape(two_d), given["m_" + n].reshape(two_d),
                           given["v_" + n].reshape(two_d), "adamw_" + n)
        delta[n], new_m[n], new_v[n] = d.reshape(w.shape), m2.reshape(w.shape), v2.reshape(w.shape)

    order = ["g_attn", "g_mlp", "w_in_a", "b_f", "gq_a", "gk_a", "w_out_a", "g_kv", "w_kv", "gk_b", "w_q_b", "gq_b",
             "sinks", "rel_bias", "w_out_b", "w_up", "w_down"]
    return (loss, grad_x[None], *[grads[n] for n in order], *[delta[n] for n in order],
            *[new_m[n] for n in order], *[new_v[n] for n in order])
```

```python
import numpy as np
import jax
import jax.numpy as jnp
from jax import lax
from jax.experimental import pallas as pl
from jax.experimental.pallas import tpu as pltpu

F32 = jnp.float32
BF16 = jnp.bfloat16
MESH = pl.DeviceIdType.MESH

HEAD_DIM = 64
LANES = 128
WINDOW = 128
N_BUCKETS = 32
REL_MAX_DIST = 128
NORM_EPS = 1e-6
ADAM_LR = 0.001
ADAM_B1 = 0.9
ADAM_B2 = 0.999
ADAM_EPS = 1e-08
ADAM_WD = 0.01
ADAM_STEP = 10
NEG = -1e30
N_CHIPS = 4
PACK_W = 1024
PACK_ROW_ALIGN = 256
VMEM_LIMIT = 56 * 1024 * 1024
HBM_SPEC = pl.BlockSpec(memory_space=pltpu.HBM)
VMEM_SPEC = pl.BlockSpec(memory_space=pltpu.VMEM)

BIG = (("w_in_a", 2), ("w_out_a", 1), ("w_kv", 0), ("w_q_b", 1), ("w_out_b", 1), ("w_up", 2), ("w_down", 1))
SMALL = ("g_attn", "g_mlp", "b_f", "gq_a", "gk_a", "g_kv", "gk_b", "gq_b", "sinks", "rel_bias")


def _pcall(body, **kw):
    return pl.pallas_call(body, **kw)


def _params(sem=None):
    return pltpu.CompilerParams(dimension_semantics=sem, vmem_limit_bytes=VMEM_LIMIT)


def _rinv(x):
    return lax.rsqrt(jnp.mean(x * x, axis=-1, keepdims=True) + NORM_EPS)


def _dot(a, b, dims, precision=None):
    return lax.dot_general(a, b, (dims, ((), ())), precision=precision, preferred_element_type=F32)


NN = ((1,), (0,))
NT = ((1,), (1,))
TN = ((0,), (0,))


def _accumulate(ref, val, first):
    @pl.when(first)
    def _():
        ref[...] = val

    @pl.when(jnp.logical_not(first))
    def _():
        ref[...] += val


def _matmul(a, b, mode, name, out_dtypes=(F32,), extras=(), epilogue=None, tm=512, tn=512, tk=None, chipwise=None):
    if chipwise == "b":
        nc = b.shape[2]
        M, K = a.shape
        (K2, N) = (b.shape[1], N_CHIPS * nc) if mode == "nn" else (N_CHIPS * nc, b.shape[1])
    elif mode == "nn":
        (M, K), (K2, N) = a.shape, b.shape
    elif mode == "nt":
        (M, K), (N, K2) = a.shape, b.shape
    else:
        (K, M), (K2, N) = a.shape, b.shape
    assert K == K2, (a.shape, b.shape, mode)
    tm, tn = min(tm, M), min(tn, N)
    tk = K if tk is None else tk
    assert M % tm == 0 and N % tn == 0 and K % tk == 0, (M, N, K, tm, tn, tk)
    nk = K // tk
    dims = {"nn": NN, "nt": NT, "tn": TN}[mode]
    a_spec = pl.BlockSpec((tk, tm), lambda i, j, k: (k, i)) if mode == "tn" else pl.BlockSpec((tm, tk), lambda i, j, k: (i, k))
    b_spec = pl.BlockSpec((tn, tk), lambda i, j, k: (j, k)) if mode == "nt" else pl.BlockSpec((tk, tn), lambda i, j, k: (k, j))
    o_spec = pl.BlockSpec((tm, tn), lambda i, j, k: (i, j))
    out_shape = (M, N)
    if chipwise == "b" and mode == "nn":
        per = nc // tn
        assert tk == K and nc % tn == 0
        b_spec = pl.BlockSpec((None, tk, tn), lambda i, j, k: (j // per, 0, j % per))
    elif chipwise == "b":
        assert mode == "nt" and tk == nc
        b_spec = pl.BlockSpec((None, tn, tk), lambda i, j, k: (k, j, 0))
    elif chipwise == "out":
        per = (N // N_CHIPS) // tn
        assert (N // N_CHIPS) % tn == 0
        o_spec = pl.BlockSpec((None, tm, tn), lambda i, j, k: (j // per, i, j % per))
        out_shape = (N_CHIPS, M, N // N_CHIPS)
        assert not extras
    n_ex, n_out = len(extras), len(out_dtypes)

    def body(*refs):
        a_ref, b_ref = refs[0], refs[1]
        ex_refs = refs[2:2 + n_ex]
        out_refs = refs[2 + n_ex:2 + n_ex + n_out]
        acc_ref = refs[2 + n_ex + n_out]
        k = pl.program_id(2)
        part = _dot(a_ref[...].astype(BF16), b_ref[...].astype(BF16), dims)

        @pl.when(k == 0)
        def _():
            acc_ref[...] = part

        @pl.when(k > 0)
        def _():
            acc_ref[...] += part

        @pl.when(k == nk - 1)
        def _():
            acc = acc_ref[...]
            outs = (acc,) if epilogue is None else epilogue(acc, *[r[...] for r in ex_refs])
            for r, o in zip(out_refs, outs):
                r[...] = o.astype(r.dtype)

    outs = _pcall(
        body, name=name, grid=(M // tm, N // tn, nk),
        in_specs=[a_spec, b_spec] + [o_spec] * n_ex,
        out_specs=[o_spec] * n_out,
        out_shape=[jax.ShapeDtypeStruct(out_shape, dt) for dt in out_dtypes],
        scratch_shapes=[pltpu.VMEM((tm, tn), F32)],
        compiler_params=_params(("parallel", "parallel", "arbitrary")),
    )(a, b, *extras)
    return outs[0] if n_out == 1 else outs


def _rms_fwd(x, gains, name, ts=256):
    S, D = x.shape
    ts = min(ts, S)
    n = len(gains)

    def body(*refs):
        x_ref, g_refs, o_refs = refs[0], refs[1:1 + n], refs[1 + n:]
        xv = x_ref[...]
        xh = xv * _rinv(xv)
        for g_ref, o_ref in zip(g_refs, o_refs):
            o_ref[...] = (xh * g_ref[...]).astype(BF16)

    row = pl.BlockSpec((ts, D), lambda i: (i, 0))
    vec = pl.BlockSpec((1, D), lambda i: (0, 0))
    return _pcall(body, name=name, grid=(S // ts,), in_specs=[row] + [vec] * n, out_specs=[row] * n,
                  out_shape=[jax.ShapeDtypeStruct((S, D), BF16)] * n, compiler_params=_params(("parallel",)))(x, *gains)


def _rms_bwd(x, dres, gains, dns, name, ts=256):
    S, D = x.shape
    ts = min(ts, S)
    n = len(gains)

    def body(*refs):
        x_ref, dres_ref = refs[0], refs[1]
        g_refs, dn_refs = refs[2:2 + n], refs[2 + n:2 + 2 * n]
        dx_ref, dg_refs = refs[2 + 2 * n], refs[3 + 2 * n:]
        xv = x_ref[...]
        r = _rinv(xv)
        xh = xv * r
        dx = dres_ref[...]
        first = pl.program_id(0) == 0
        for g_ref, dn_ref, dg_ref in zip(g_refs, dn_refs, dg_refs):
            dn = dn_ref[...].astype(F32)
            _accumulate(dg_ref, jnp.sum(dn * xh, axis=0, keepdims=True), first)
            dxh = dn * g_ref[...]
            dx = dx + r * (dxh - xh * jnp.mean(dxh * xh, axis=-1, keepdims=True))
        dx_ref[...] = dx

    row = pl.BlockSpec((ts, D), lambda i: (i, 0))
    vec = pl.BlockSpec((1, D), lambda i: (0, 0))
    outs = _pcall(body, name=name, grid=(S // ts,), in_specs=[row, row] + [vec] * n + [row] * n,
                  out_specs=[row] + [vec] * n,
                  out_shape=[jax.ShapeDtypeStruct((S, D), F32)] + [jax.ShapeDtypeStruct((1, D), F32)] * n,
                  compiler_params=_params(("arbitrary",)))(x, dres, *gains, *dns)
    return outs[0], outs[1:]


def _loss_head(h, tgt, name, ts=256):
    S, D = h.shape
    ts = min(ts, S)

    def body(h_ref, t_ref, dh_ref, loss_ref):
        err = h_ref[...] - t_ref[...]
        dh_ref[...] = err * (1.0 / D)
        part = 0.5 * jnp.sum(jnp.mean(err * err, axis=-1, keepdims=True), axis=0, keepdims=True)
        _accumulate(loss_ref, part, pl.program_id(0) == 0)

    row = pl.BlockSpec((ts, D), lambda i: (i, 0))
    return _pcall(body, name=name, grid=(S // ts,), in_specs=[row, row],
                  out_specs=[row, pl.BlockSpec((1, 1), lambda i: (0, 0))],
                  out_shape=[jax.ShapeDtypeStruct((S, D), F32), jax.ShapeDtypeStruct((1, 1), F32)],
                  compiler_params=_params(("arbitrary",)))(h, tgt)


def _gate_fwd(zt, bf, name):
    H, S = zt.shape
    nb = S // 128

    def body(z_ref, b_ref, c_ref):
        z = z_ref[...] + b_ref[...]
        lf = jnp.minimum(z, 0.0) - jnp.log(1.0 + jnp.exp(-jnp.abs(z)))
        upper = (lax.broadcasted_iota(jnp.int32, (128, 128), 0) <= lax.broadcasted_iota(jnp.int32, (128, 128), 1)).astype(F32)
        carry = jnp.zeros((H, 1), F32)
        for blk in range(nb):
            cs = _dot(lf[:, blk * 128:(blk + 1) * 128], upper, NN, precision=lax.Precision.HIGHEST) + carry
            c_ref[:, blk * 128:(blk + 1) * 128] = cs
            carry = cs[:, 127:128]

    return _pcall(body, name=name, in_specs=[VMEM_SPEC, VMEM_SPEC], out_specs=VMEM_SPEC,
                  out_shape=jax.ShapeDtypeStruct((H, S), F32))(zt, bf)


def _gate_bwd(dct, zt, bf, name):
    H, S = zt.shape
    nb = S // 128

    def body(dc_ref, z_ref, b_ref, dz_ref, db_ref):
        z = z_ref[...] + b_ref[...]
        e = jnp.exp(-jnp.abs(z))
        sig_neg = jnp.where(z >= 0, e, 1.0) / (1.0 + e)
        lower = (lax.broadcasted_iota(jnp.int32, (128, 128), 0) >= lax.broadcasted_iota(jnp.int32, (128, 128), 1)).astype(F32)
        dc = dc_ref[...]
        carry = jnp.zeros((H, 1), F32)
        db = jnp.zeros((H, 1), F32)
        for blk in reversed(range(nb)):
            sl = slice(blk * 128, (blk + 1) * 128)
            dlf = _dot(dc[:, sl], lower, NN, precision=lax.Precision.HIGHEST) + carry
            carry = dlf[:, 0:1]
            dz = dlf * sig_neg[:, sl]
            dz_ref[:, sl] = dz
            db = db + jnp.sum(dz, axis=1, keepdims=True)
        db_ref[...] = db

    return _pcall(body, name=name, in_specs=[VMEM_SPEC] * 3, out_specs=[VMEM_SPEC] * 2,
                  out_shape=[jax.ShapeDtypeStruct((H, S), F32), jax.ShapeDtypeStruct((H, 1), F32)])(dct, zt, bf)


def _lane_is_a():
    return lax.broadcasted_iota(jnp.int32, (1, LANES), 1) < HEAD_DIM


def _per_head_mean(x, is_a):
    sa = jnp.sum(jnp.where(is_a, x, 0.0), axis=-1, keepdims=True)
    sb = jnp.sum(jnp.where(is_a, 0.0, x), axis=-1, keepdims=True)
    return jnp.where(is_a, sa, sb) / HEAD_DIM


def _pair_norm(raw, gain, is_a):
    return raw * lax.rsqrt(_per_head_mean(raw * raw, is_a) + NORM_EPS) * gain


def _pair_norm_bwd(raw, gain, dnormed, is_a):
    r = lax.rsqrt(_per_head_mean(raw * raw, is_a) + NORM_EPS)
    xh = raw * r
    dgain = jnp.sum(dnormed * xh, axis=0, keepdims=True)
    dxh = dnormed * gain
    return r * (dxh - xh * _per_head_mean(dxh * xh, is_a)), dgain


def _fold_heads(x):
    i = lax.broadcasted_iota(jnp.int32, (LANES, LANES), 0)
    j = lax.broadcasted_iota(jnp.int32, (LANES, LANES), 1)
    fold = ((i == j) | (i == j + HEAD_DIM) | (i + HEAD_DIM == j)).astype(F32)
    return _dot(x, fold, NN, precision=lax.Precision.HIGHEST)


def _fold_row(ref):
    ref[...] = _fold_heads(jnp.broadcast_to(ref[...], (8, LANES)))[0:1, :]


def _tri_mask(t, keys_on_rows):
    r = lax.broadcasted_iota(jnp.int32, (t, t), 0)
    c = lax.broadcasted_iota(jnp.int32, (t, t), 1)
    return (r <= c) if keys_on_rows else (r >= c)


def _fox_fwd(proj, c_col, c_row, gq2, gk2, n_heads, name, t=256):
    S = proj.shape[0]
    H = n_heads
    P = H // 2
    t = min(t, S)
    nq = S // t

    def body(q_ref, k_ref, v_ref, cc_ref, cr_ref, gq_ref, gk_ref, o_ref, lse_ref, qs_s, kb_s, vb_s):
        is_a = _lane_is_a()
        qn = _pair_norm(q_ref[...], gq_ref[...], is_a) * 0.125
        qs_s[0] = jnp.where(is_a, qn, 0.0).astype(BF16)
        qs_s[1] = jnp.where(is_a, 0.0, qn).astype(BF16)
        kb_s[...] = _pair_norm(k_ref[...], gk_ref[...], is_a).astype(BF16)
        vb_s[...] = v_ref[...].astype(BF16)
        causal = _tri_mask(t, False)
        for i in range(nq):
            t0 = i * t
            rows = slice(t0, t0 + t)
            o_pair = None
            for a in range(2):
                qi = qs_s[a, rows, :]
                ci = cc_ref[a, rows, :]
                s_d = jnp.where(causal, _dot(qi, kb_s[rows, :], NT) + ci - cr_ref[a, :, rows], NEG)
                m = jnp.max(s_d, axis=-1, keepdims=True)
                if i > 0:
                    s_l = _dot(qi, kb_s[0:t0, :], NT) + ci - cr_ref[a, :, 0:t0]
                    m = jnp.maximum(m, jnp.max(s_l, axis=-1, keepdims=True))
                p_d = jnp.exp(s_d - m)
                l = jnp.sum(p_d, axis=-1, keepdims=True)
                acc = _dot(p_d.astype(BF16), vb_s[rows, :], NN)
                if i > 0:
                    p_l = jnp.exp(s_l - m)
                    l = l + jnp.sum(p_l, axis=-1, keepdims=True)
                    acc = acc + _dot(p_l.astype(BF16), vb_s[0:t0, :], NN)
                o_a = acc / l
                lse_ref[a, rows, :] = m + jnp.log(l)
                o_pair = o_a if a == 0 else jnp.where(is_a, o_pair, o_a)
            o_ref[rows, :] = o_pair.astype(BF16)

    def cols(off):
        return pl.BlockSpec((S, LANES), lambda p: (0, off + p))

    col = pl.BlockSpec((2, S, 1), lambda p: (p, 0, 0))
    rowv = pl.BlockSpec((2, 1, S), lambda p: (p, 0, 0))
    gain = pl.BlockSpec((1, LANES), lambda p: (0, 0))
    return _pcall(body, name=name, grid=(P,), in_specs=[cols(0), cols(P), cols(2 * P), col, rowv, gain, gain],
                  out_specs=[cols(0), col],
                  out_shape=[jax.ShapeDtypeStruct((S, H * HEAD_DIM), BF16), jax.ShapeDtypeStruct((H, S, 1), F32)],
                  scratch_shapes=[pltpu.VMEM((2, S, LANES), BF16), pltpu.VMEM((S, LANES), BF16), pltpu.VMEM((S, LANES), BF16)],
                  compiler_params=_params(("parallel",)))(proj, proj, proj, c_col, c_row, gq2, gk2)


def _fox_bwd(proj, c_col, c_row, gq2, gk2, lse_row, do, n_heads, name, t=256):
    S = proj.shape[0]
    H = n_heads
    P = H // 2
    t = min(t, S)
    nq = S // t
    assert t % LANES == 0

    def body(q_ref, k_ref, v_ref, cc_ref, cr_ref, gq_ref, gk_ref, lr_ref, do_ref,
             dq_ref, dk_ref, dv_ref, dc_ref, dgq_ref, dgk_ref,
             qs_s, kb_s, kt_s, vb_s, dob_s, dq_s, dk_s, dv_s, dcs_s):
        is_a = _lane_is_a()
        qn = _pair_norm(q_ref[...], gq_ref[...], is_a) * 0.125
        qs_s[0] = jnp.where(is_a, qn, 0.0).astype(BF16)
        qs_s[1] = jnp.where(is_a, 0.0, qn).astype(BF16)
        kn = _pair_norm(k_ref[...], gk_ref[...], is_a)
        kb_s[...] = kn.astype(BF16)
        kt_s[0] = jnp.where(is_a, kn, 0.0).T.astype(BF16)
        kt_s[1] = jnp.where(is_a, 0.0, kn).T.astype(BF16)
        vb_s[...] = v_ref[...].astype(BF16)
        dov = do_ref[...]
        dob_s[0] = jnp.where(is_a, dov, 0.0).astype(BF16)
        dob_s[1] = jnp.where(is_a, 0.0, dov).astype(BF16)
        dk_s[...] = jnp.zeros((S, LANES), F32)
        dv_s[...] = jnp.zeros((S, LANES), F32)
        dcs_s[...] = jnp.zeros((2, S, LANES), F32)
        causal = _tri_mask(t, True)
        for i in range(nq):
            t0 = i * t
            rows = slice(t0, t0 + t)
            dq_t = jnp.zeros((LANES, t), F32)
            for a in range(2):
                qi = qs_s[a, rows, :]
                doi = dob_s[a, rows, :]
                cri = cr_ref[a, :, rows]
                lri = lr_ref[a, :, rows]

                def probs(keys, masked, a=a, qi=qi, doi=doi, cri=cri, lri=lri):
                    p_t = jnp.exp(_dot(kb_s[keys, :], qi, NT) + cri - cc_ref[a, keys, :] - lri)
                    if masked:
                        p_t = jnp.where(causal, p_t, 0.0)
                    return p_t, _dot(vb_s[keys, :], doi, NT)

                parts = [(rows,) + probs(rows, True)]
                if i > 0:
                    parts.append((slice(0, t0),) + probs(slice(0, t0), False))
                delta = sum(jnp.sum(p_t * dp_t, axis=0, keepdims=True) for _, p_t, dp_t in parts)
                for keys, p_t, dp_t in parts:
                    ds_t = p_t * (dp_t - delta)
                    dsb = ds_t.astype(BF16)
                    dv_s[keys, :] += _dot(p_t.astype(BF16), doi, NN)
                    dk_s[keys, :] += _dot(dsb, qi, NN)
                    dq_t = dq_t + _dot(kt_s[a, :, keys], dsb, NN)
                    dcs_s[a, keys, :] += sum(ds_t[:, b * LANES:(b + 1) * LANES] for b in range(t // LANES))
            dq_s[rows, :] = dq_t.T
        first = pl.program_id(0) == 0
        last = pl.program_id(0) == P - 1
        dq_raw, dgq = _pair_norm_bwd(q_ref[...], gq_ref[...], dq_s[...] * 0.125, is_a)
        dq_ref[...] = dq_raw.astype(BF16)
        _accumulate(dgq_ref, dgq, first)
        dk_raw, dgk = _pair_norm_bwd(k_ref[...], gk_ref[...], dk_s[...], is_a)
        dk_ref[...] = dk_raw.astype(BF16)
        _accumulate(dgk_ref, dgk, first)
        dv_ref[...] = dv_s[...].astype(BF16)
        for a in range(2):
            dc_ref[a] = -jnp.sum(dcs_s[a], axis=1, keepdims=True)

        @pl.when(last)
        def _():
            _fold_row(dgq_ref)
            _fold_row(dgk_ref)

    def cols(off):
        return pl.BlockSpec((S, LANES), lambda p: (0, off + p))

    col = pl.BlockSpec((2, S, 1), lambda p: (p, 0, 0))
    rowv = pl.BlockSpec((2, 1, S), lambda p: (p, 0, 0))
    gain = pl.BlockSpec((1, LANES), lambda p: (0, 0))
    wide = jax.ShapeDtypeStruct((S, H * HEAD_DIM), BF16)
    gs = jax.ShapeDtypeStruct((1, LANES), F32)
    return _pcall(body, name=name, grid=(P,),
                  in_specs=[cols(0), cols(P), cols(2 * P), col, rowv, gain, gain, rowv, cols(0)],
                  out_specs=[cols(0), cols(0), cols(0), col, gain, gain],
                  out_shape=[wide, wide, wide, jax.ShapeDtypeStruct((H, S, 1), F32), gs, gs],
                  scratch_shapes=[pltpu.VMEM((2, S, LANES), BF16), pltpu.VMEM((S, LANES), BF16), pltpu.VMEM((2, LANES, S), BF16),
                                  pltpu.VMEM((S, LANES), BF16), pltpu.VMEM((2, S, LANES), BF16)]
                  + [pltpu.VMEM((S, LANES), F32)] * 3 + [pltpu.VMEM((2, S, LANES), F32)],
                  compiler_params=_params(("arbitrary",)))(proj, proj, proj, c_col, c_row, gq2, gk2, lse_row, do)


def _bucket_onehot():
    W = WINDOW
    dist = np.arange(W)[:, None] + W - np.arange(2 * W)[None, :]
    n = np.maximum(dist, 0)
    max_exact = N_BUCKETS // 2
    large = max_exact + (np.log(np.maximum(n, 1) / max_exact) / np.log(REL_MAX_DIST / max_exact)
                         * (N_BUCKETS - max_exact)).astype(np.int32)
    large = np.minimum(large, N_BUCKETS - 1)
    bucket = np.where(n < max_exact, n, large).astype(np.int32)
    valid = (dist >= 0) & (dist < W)
    onehot = (bucket[None] == np.arange(N_BUCKETS)[:, None, None]) & valid[None]
    return onehot.reshape(N_BUCKETS, W * 2 * W).astype(np.float32)


def _bias_expand(rel_bias_t, onehot, name, tn=4096):
    HQ, NB = rel_bias_t.shape
    L = onehot.shape[1]

    def body(r_ref, oh_ref, out_ref):
        out_ref[...] = _dot(r_ref[...], oh_ref[...].astype(F32), NN, precision=lax.Precision.HIGHEST)

    return _pcall(body, name=name, grid=(L // tn,),
                  in_specs=[pl.BlockSpec((HQ, NB), lambda i: (0, 0)), pl.BlockSpec((NB, tn), lambda i: (0, i))],
                  out_specs=pl.BlockSpec((HQ, tn), lambda i: (0, i)),
                  out_shape=jax.ShapeDtypeStruct((HQ, L), F32), compiler_params=_params(("parallel",)))(rel_bias_t, onehot)


def _bias_reduce(dbias, onehot, name, tk=4096):
    HQ, L = dbias.shape
    NB = onehot.shape[0]

    def body(d_ref, oh_ref, out_ref):
        part = _dot(d_ref[...], oh_ref[...].astype(F32), NT, precision=lax.Precision.HIGHEST)
        _accumulate(out_ref, part, pl.program_id(0) == 0)

    return _pcall(body, name=name, grid=(L // tk,),
                  in_specs=[pl.BlockSpec((HQ, tk), lambda i: (0, i)), pl.BlockSpec((NB, tk), lambda i: (0, i))],
                  out_specs=pl.BlockSpec((HQ, NB), lambda i: (0, 0)),
                  out_shape=jax.ShapeDtypeStruct((HQ, NB), F32), compiler_params=_params(("arbitrary",)))(dbias, onehot)


def _stacked_query_index(n_rows_or_cols_axis, shape):
    idx = lax.broadcasted_iota(jnp.int32, shape, n_rows_or_cols_axis)
    return jnp.where(idx >= WINDOW, idx - WINDOW, idx)


def _swa_fwd(qproj, kk, vv, bias_ab, sink_col, gq2, gk2, name):
    S, HQD = qproj.shape
    KVH = kk.shape[0]
    PP = HQD // LANES
    NP = PP // KVH
    W = WINDOW
    nb = S // W

    def body(q_ref, k_ref, v_ref, bias_ref, sink_ref, gq_ref, gk_ref, o_ref, lse_ref, qs_s, kb_s, vb_s):
        is_a = _lane_is_a()
        qn = _pair_norm(q_ref[...], gq_ref[...], is_a) * 0.125
        qs_s[0] = jnp.where(is_a, qn, 0.0).astype(BF16)
        qs_s[1] = jnp.where(is_a, 0.0, qn).astype(BF16)
        kb_s[...] = _pair_norm(k_ref[...], gk_ref[...], is_a).astype(BF16)
        vb_s[...] = v_ref[...].astype(BF16)
        sink = sink_ref[...]
        qi1 = _stacked_query_index(0, (2 * W, W))
        first_valid = lax.broadcasted_iota(jnp.int32, (2 * W, W), 1) <= qi1
        qi2 = _stacked_query_index(0, (2 * W, 2 * W))
        key2 = lax.broadcasted_iota(jnp.int32, (2 * W, 2 * W), 1)
        band_valid = (key2 > qi2) & (key2 <= qi2 + W)
        for n in range(nb):
            rows = slice(n * W, (n + 1) * W)
            keys = slice(0, W) if n == 0 else slice((n - 1) * W, (n + 1) * W)
            lhs = jnp.concatenate([qs_s[0, rows, :], qs_s[1, rows, :]], axis=0)
            s = _dot(lhs, kb_s[keys, :], NT) + (bias_ref[:, W:2 * W] if n == 0 else bias_ref[...])
            s = jnp.where(first_valid if n == 0 else band_valid, s, NEG)
            m = jnp.maximum(jnp.max(s, axis=-1, keepdims=True), sink)
            e = jnp.exp(s - m)
            l = jnp.sum(e, axis=-1, keepdims=True) + jnp.exp(sink - m)
            o_ab = _dot(e.astype(BF16), vb_s[keys, :], NN) / l
            o_ref[rows, :] = jnp.where(is_a, o_ab[0:W, :], o_ab[W:2 * W, :]).astype(BF16)
            lse_ref[n] = m + jnp.log(l)

    qcols = pl.BlockSpec((S, LANES), lambda a, g: (0, a * NP + g))
    kvs = pl.BlockSpec((None, S, LANES), lambda a, g: (a, 0, 0))
    gain = pl.BlockSpec((1, LANES), lambda a, g: (0, 0))
    return _pcall(body, name=name, grid=(KVH, NP),
                  in_specs=[qcols, kvs, kvs, pl.BlockSpec((None, 2 * W, 2 * W), lambda a, g: (a * NP + g, 0, 0)),
                            pl.BlockSpec((None, 2 * W, 1), lambda a, g: (a * NP + g, 0, 0)), gain, gain],
                  out_specs=[qcols, pl.BlockSpec((None, nb, 2 * W, 1), lambda a, g: (a * NP + g, 0, 0, 0))],
                  out_shape=[jax.ShapeDtypeStruct((S, HQD), BF16), jax.ShapeDtypeStruct((PP, nb, 2 * W, 1), F32)],
                  scratch_shapes=[pltpu.VMEM((2, S, LANES), BF16), pltpu.VMEM((S, LANES), BF16), pltpu.VMEM((S, LANES), BF16)],
                  compiler_params=_params(("parallel", "parallel")))(qproj, kk, vv, bias_ab, sink_col, gq2, gk2)


def _swa_bwd(qproj, kk, vv, bias_t_ab, sink_row, gq2, gk2, lse_row, do, name):
    S, HQD = qproj.shape
    KVH = kk.shape[0]
    PP = HQD // LANES
    NP = PP // KVH
    W = WINDOW
    nb = S // W

    def body(q_ref, k_ref, v_ref, bias_ref, sink_ref, gq_ref, gk_ref, lr_ref, do_ref,
             dq_ref, dk_ref, dv_ref, db_ref, dsink_ref, dgq_ref, dgk_ref,
             qs_s, kb_s, kt_s, vb_s, dob_s, dq_s, dk_s, dv_s):
        a, g = pl.program_id(0), pl.program_id(1)
        is_a = _lane_is_a()
        qn = _pair_norm(q_ref[...], gq_ref[...], is_a) * 0.125
        qs_s[0] = jnp.where(is_a, qn, 0.0).astype(BF16)
        qs_s[1] = jnp.where(is_a, 0.0, qn).astype(BF16)
        kn = _pair_norm(k_ref[...], gk_ref[...], is_a)
        kb_s[...] = kn.astype(BF16)
        kt_s[...] = kn.T.astype(BF16)
        vb_s[...] = v_ref[...].astype(BF16)
        dov = do_ref[...]
        dob_s[0] = jnp.where(is_a, dov, 0.0).astype(BF16)
        dob_s[1] = jnp.where(is_a, 0.0, dov).astype(BF16)
        sink = sink_ref[...]

        @pl.when(g == 0)
        def _():
            dk_s[...] = jnp.zeros((S, LANES), F32)
            dv_s[...] = jnp.zeros((S, LANES), F32)

        qi1 = _stacked_query_index(1, (W, 2 * W))
        first_valid = lax.broadcasted_iota(jnp.int32, (W, 2 * W), 0) <= qi1
        qi2 = _stacked_query_index(1, (2 * W, 2 * W))
        key2 = lax.broadcasted_iota(jnp.int32, (2 * W, 2 * W), 0)
        band_valid = (key2 > qi2) & (key2 <= qi2 + W)
        head_rows = lax.broadcasted_iota(jnp.int32, (LANES, W), 0) < HEAD_DIM
        db = jnp.zeros((2 * W, 2 * W), F32)
        dsk = jnp.zeros((1, 2 * W), F32)
        pend_k = pend_v = None
        for n in range(nb):
            rows = slice(n * W, (n + 1) * W)
            keys = slice(0, W) if n == 0 else slice((n - 1) * W, (n + 1) * W)
            lhs_q = jnp.concatenate([qs_s[0, rows, :], qs_s[1, rows, :]], axis=0)
            lhs_do = jnp.concatenate([dob_s[0, rows, :], dob_s[1, rows, :]], axis=0)
            lse = lr_ref[n]
            s_t = _dot(kb_s[keys, :], lhs_q, NT) + (bias_ref[W:2 * W, :] if n == 0 else bias_ref[...])
            p_t = jnp.where(first_valid if n == 0 else band_valid, jnp.exp(s_t - lse), 0.0)
            dp_t = _dot(vb_s[keys, :], lhs_do, NT)
            delta = jnp.sum(p_t * dp_t, axis=0, keepdims=True)
            ds_t = p_t * (dp_t - delta)
            dsb = ds_t.astype(BF16)
            dsk = dsk - jnp.exp(sink - lse) * delta
            dv_band = _dot(p_t.astype(BF16), lhs_do, NN)
            dk_band = _dot(dsb, lhs_q, NN)
            dq_t = _dot(kt_s[:, keys], dsb, NN)
            dq_s[rows, :] = jnp.where(head_rows, dq_t[:, 0:W], dq_t[:, W:2 * W]).T
            if n == 0:
                db = jnp.concatenate([jnp.zeros((W, 2 * W), F32), ds_t], axis=0)
                pend_k, pend_v = dk_band, dv_band
            else:
                db = db + ds_t
                prev = slice((n - 1) * W, n * W)
                dk_s[prev, :] += pend_k + dk_band[0:W, :]
                dv_s[prev, :] += pend_v + dv_band[0:W, :]
                pend_k, pend_v = dk_band[W:2 * W, :], dv_band[W:2 * W, :]
        tail = slice((nb - 1) * W, nb * W)
        dk_s[tail, :] += pend_k
        dv_s[tail, :] += pend_v
        db_ref[...] = db
        dsink_ref[0] = jnp.broadcast_to(jnp.sum(dsk[:, 0:W], axis=1, keepdims=True), (1, LANES))
        dsink_ref[1] = jnp.broadcast_to(jnp.sum(dsk[:, W:2 * W], axis=1, keepdims=True), (1, LANES))
        dq_raw, dgq = _pair_norm_bwd(q_ref[...], gq_ref[...], dq_s[...] * 0.125, is_a)
        dq_ref[...] = dq_raw.astype(BF16)
        _accumulate(dgq_ref, dgq, jnp.logical_and(a == 0, g == 0))

        @pl.when(jnp.logical_and(a == KVH - 1, g == NP - 1))
        def _():
            _fold_row(dgq_ref)

        @pl.when(g == NP - 1)
        def _():
            dk_raw, dgk = _pair_norm_bwd(k_ref[...], gk_ref[...], _fold_heads(dk_s[...]), is_a)
            dk_ref[...] = dk_raw
            _accumulate(dgk_ref, dgk, a == 0)
            dv_ref[...] = _fold_heads(dv_s[...])

    qcols = pl.BlockSpec((S, LANES), lambda a, g: (0, a * NP + g))
    kvs = pl.BlockSpec((None, S, LANES), lambda a, g: (a, 0, 0))
    sq = pl.BlockSpec((None, 2 * W, 2 * W), lambda a, g: (a * NP + g, 0, 0))
    gain = pl.BlockSpec((1, LANES), lambda a, g: (0, 0))
    ks = jax.ShapeDtypeStruct((KVH, S, LANES), F32)
    gs = jax.ShapeDtypeStruct((1, LANES), F32)
    return _pcall(body, name=name, grid=(KVH, NP),
                  in_specs=[qcols, kvs, kvs, sq, pl.BlockSpec((None, 1, 2 * W), lambda a, g: (a * NP + g, 0, 0)), gain, gain,
                            pl.BlockSpec((None, nb, 1, 2 * W), lambda a, g: (a * NP + g, 0, 0, 0)), qcols],
                  out_specs=[qcols, kvs, kvs, sq, pl.BlockSpec((2, 1, LANES), lambda a, g: (a * NP + g, 0, 0)), gain, gain],
                  out_shape=[jax.ShapeDtypeStruct((S, HQD), BF16), ks, ks, jax.ShapeDtypeStruct((PP, 2 * W, 2 * W), F32),
                             jax.ShapeDtypeStruct((2 * PP, 1, LANES), F32), gs, gs],
                  scratch_shapes=[pltpu.VMEM((2, S, LANES), BF16), pltpu.VMEM((S, LANES), BF16), pltpu.VMEM((LANES, S), BF16),
                                  pltpu.VMEM((S, LANES), BF16), pltpu.VMEM((2, S, LANES), BF16)] + [pltpu.VMEM((S, LANES), F32)] * 3,
                  compiler_params=_params(("arbitrary", "arbitrary")))(qproj, kk, vv, bias_t_ab, sink_row, gq2, gk2, lse_row, do)


def _adamw(w, g, m, v, name, tr=256):
    R, C = w.shape
    tr = min(tr, R)
    assert R % tr == 0

    def body(w_ref, g_ref, m_ref, v_ref, d_ref, m2_ref, v2_ref):
        gv = g_ref[...]
        m2 = ADAM_B1 * m_ref[...] + (1.0 - ADAM_B1) * gv
        v2 = ADAM_B2 * v_ref[...] + (1.0 - ADAM_B2) * jnp.square(gv)
        m_hat = m2 / (1.0 - ADAM_B1 ** ADAM_STEP)
        v_hat = v2 / (1.0 - ADAM_B2 ** ADAM_STEP)
        d_ref[...] = -ADAM_LR * (m_hat / (jnp.sqrt(v_hat) + ADAM_EPS) + ADAM_WD * w_ref[...])
        m2_ref[...] = m2
        v2_ref[...] = v2

    blk = pl.BlockSpec((tr, C), lambda i: (i, 0))
    return _pcall(body, name=name, grid=(R // tr,), in_specs=[blk] * 4, out_specs=[blk] * 3,
                  out_shape=[jax.ShapeDtypeStruct((R, C), F32)] * 3, compiler_params=_params(("parallel",)))(w, g, m, v)


def _sum_list(arrays, name, out_dtype, tr=128):
    R, C = arrays[0].shape
    tr = min(tr, R)
    assert R % tr == 0, (R, tr)
    n = len(arrays)

    def body(*refs):
        acc = refs[0][...].astype(F32)
        for r in refs[1:n]:
            acc = acc + r[...].astype(F32)
        refs[n][...] = acc.astype(refs[n].dtype)

    blk = pl.BlockSpec((tr, C), lambda i: (i, 0))
    return _pcall(body, name=name, grid=(R // tr,), in_specs=[blk] * n, out_specs=blk,
                  out_shape=jax.ShapeDtypeStruct((R, C), out_dtype), compiler_params=_params(("parallel",)))(*arrays)


def _sum_parts(parts, name, out_dtype, tr=128):
    P, R, C = parts.shape
    tr = min(tr, R)
    assert R % tr == 0, (R, tr)

    def body(p_ref, o_ref):
        acc = p_ref[0].astype(F32)
        for k in range(1, P):
            acc = acc + p_ref[k].astype(F32)
        o_ref[...] = acc.astype(o_ref.dtype)

    return _pcall(body, name=name, grid=(R // tr,), in_specs=[pl.BlockSpec((P, tr, C), lambda i: (0, i, 0))],
                  out_specs=pl.BlockSpec((tr, C), lambda i: (i, 0)),
                  out_shape=jax.ShapeDtypeStruct((R, C), out_dtype), compiler_params=_params(("parallel",)))(parts)


def _place():
    x, y, c = lax.axis_index("x"), lax.axis_index("y"), lax.axis_index("c")
    others = [(1 - x, y), (x, 1 - y), (1 - x, 1 - y)]
    return x, y, c, others


def _half_rows(ref, hh, lead=()):
    hr = ref.shape[-2] // 2
    return ref.at[(*lead, pl.ds(pl.multiple_of(hh * hr, 16), hr), slice(None))]


def _sem_arrays(*counts):
    return [pltpu.SemaphoreType.DMA((k,)) for k in counts]


def _allgather_group(shards, name):
    n = len(shards)

    def body(*refs):
        ins, outs = refs[:n], refs[n:2 * n]
        send_sems, recv_sems, local_sems = refs[2 * n:]
        x, y, c, others = _place()
        me = 2 * x + y
        sibling = (x, y, 1 - c)

        def copy(w, k, src, dst, to):
            return pltpu.make_async_remote_copy(src_ref=src, dst_ref=dst, send_sem=send_sems.at[6 * w + k],
                                                recv_sem=recv_sems.at[6 * w + k], device_id=to, device_id_type=MESH)

        local = [pltpu.make_async_copy(ins[w], outs[w].at[me], local_sems.at[w]) for w in range(n)]
        first = [copy(w, k, _half_rows(ins[w], c), _half_rows(outs[w], c, (me,)), (cx, cy, c))
                 for w in range(n) for k, (cx, cy) in enumerate(others)]
        for cp in local + first:
            cp.start()
        passed = []
        for w in range(n):
            for k, (cx, cy) in enumerate(others):
                landed = _half_rows(outs[w], c, (2 * cx + cy,))
                copy(w, k, landed, landed, sibling).wait_recv()
                passed.append(copy(w, 3 + k, landed, landed, sibling))
                passed[-1].start()
        for w in range(n):
            for k, (cx, cy) in enumerate(others):
                theirs = _half_rows(outs[w], 1 - c, (2 * cx + cy,))
                copy(w, 3 + k, theirs, theirs, sibling).wait_recv()
        for cp in first + passed:
            cp.wait_send()
        for cp in local:
            cp.wait()

    return _pcall(body, name=name, in_specs=[HBM_SPEC] * n, out_specs=[HBM_SPEC] * n,
                  out_shape=[jax.ShapeDtypeStruct((N_CHIPS,) + s.shape, s.dtype) for s in shards],
                  scratch_shapes=_sem_arrays(6 * n, 6 * n, n))(*shards)


def _swap_halves_group(arrs, name):
    n = len(arrs)

    def body(*refs):
        ins, keeps, gots = refs[:n], refs[n:2 * n], refs[2 * n:3 * n]
        send_sems, recv_sems, local_sems = refs[3 * n:]
        x, y, c, _ = _place()
        local = [pltpu.make_async_copy(_half_rows(ins[w], c, (slice(None),)), keeps[w], local_sems.at[w]) for w in range(n)]
        swaps = [pltpu.make_async_remote_copy(src_ref=_half_rows(ins[w], 1 - c, (slice(None),)), dst_ref=gots[w],
                                              send_sem=send_sems.at[w], recv_sem=recv_sems.at[w],
                                              device_id=(x, y, 1 - c), device_id_type=MESH) for w in range(n)]
        for cp in local + swaps:
            cp.start()
        for cp in swaps + local:
            cp.wait()

    half_shapes = [jax.ShapeDtypeStruct((a.shape[0], a.shape[1] // 2, a.shape[2]), a.dtype) for a in arrs]
    outs = _pcall(body, name=name, in_specs=[HBM_SPEC] * n, out_specs=[HBM_SPEC] * (2 * n), out_shape=half_shapes * 2,
                  scratch_shapes=_sem_arrays(n, n, n))(*arrs)
    return outs[:n], outs[n:]


def _scatter_group(parts, name):
    n = len(parts)

    def body(*refs):
        ins, outs = refs[:n], refs[n:2 * n]
        send_sems, recv_sems, local_sems = refs[2 * n:]
        x, y, c, others = _place()
        me = 2 * x + y

        def copy(w, k, src_chip, dst_chip, to):
            return pltpu.make_async_remote_copy(src_ref=ins[w].at[src_chip], dst_ref=outs[w].at[dst_chip],
                                                send_sem=send_sems.at[3 * w + k], recv_sem=recv_sems.at[3 * w + k],
                                                device_id=to, device_id_type=MESH)

        local = [pltpu.make_async_copy(ins[w].at[me], outs[w].at[me], local_sems.at[w]) for w in range(n)]
        sends = [copy(w, k, 2 * cx + cy, me, (cx, cy, c)) for w in range(n) for k, (cx, cy) in enumerate(others)]
        for cp in local + sends:
            cp.start()
        for w in range(n):
            for k, (cx, cy) in enumerate(others):
                copy(w, k, me, 2 * cx + cy, (cx, cy, c)).wait_recv()
        for cp in sends:
            cp.wait_send()
        for cp in local:
            cp.wait()

    return _pcall(body, name=name, in_specs=[HBM_SPEC] * n, out_specs=[HBM_SPEC] * n,
                  out_shape=[jax.ShapeDtypeStruct(p.shape, p.dtype) for p in parts],
                  scratch_shapes=_sem_arrays(3 * n, 3 * n, n))(*parts)


def _share_halves_group(halves, name):
    n = len(halves)

    def body(*refs):
        ins, outs = refs[:n], refs[n:2 * n]
        send_sems, recv_sems, local_sems = refs[2 * n:]
        x, y, c, _ = _place()
        local = [pltpu.make_async_copy(ins[w], outs[w].at[c], local_sems.at[w]) for w in range(n)]
        swaps = [pltpu.make_async_remote_copy(src_ref=ins[w], dst_ref=outs[w].at[c], send_sem=send_sems.at[w],
                                              recv_sem=recv_sems.at[w], device_id=(x, y, 1 - c), device_id_type=MESH)
                 for w in range(n)]
        for cp in local + swaps:
            cp.start()
        for w in range(n):
            pltpu.make_async_remote_copy(src_ref=ins[w], dst_ref=outs[w].at[1 - c], send_sem=send_sems.at[w],
                                         recv_sem=recv_sems.at[w], device_id=(x, y, 1 - c), device_id_type=MESH).wait()
        for cp in local:
            cp.wait()

    return _pcall(body, name=name, in_specs=[HBM_SPEC] * n, out_specs=[HBM_SPEC] * n,
                  out_shape=[jax.ShapeDtypeStruct((2,) + h.shape, h.dtype) for h in halves],
                  scratch_shapes=_sem_arrays(n, n, n))(*halves)


def _allgather_small(blk, name):
    M, C = blk.shape

    def body(x_ref, out_ref, send_sems, recv_sems, local_sem):
        x, y, c, others = _place()
        me, sibling = (x, y, c), (x, y, 1 - c)

        def rows(px, py, pc):
            return out_ref.at[4 * px + 2 * py + pc]

        def copy(k, block, to, src=None):
            return pltpu.make_async_remote_copy(src_ref=rows(*block) if src is None else src, dst_ref=rows(*block),
                                                send_sem=send_sems.at[k], recv_sem=recv_sems.at[k], device_id=to, device_id_type=MESH)

        mine = pltpu.make_async_copy(x_ref, rows(*me), local_sem)
        mine.start()
        first = [copy(0, me, sibling, src=x_ref)]
        first += [copy(1 + j, me, (*chip, c), src=x_ref) for j, chip in enumerate(others)]
        for cp in first:
            cp.start()
        passed = [copy(4 + j, (*chip, c), sibling) for j, chip in enumerate(others)]
        for j, chip in enumerate(others):
            copy(1 + j, (*chip, c), me).wait_recv()
            passed[j].start()
        copy(0, sibling, me).wait_recv()
        for j, chip in enumerate(others):
            copy(4 + j, (*chip, 1 - c), me).wait_recv()
        for cp in first + passed:
            cp.wait_send()
        mine.wait()

    return _pcall(body, name=name, in_specs=[VMEM_SPEC], out_specs=VMEM_SPEC,
                  out_shape=jax.ShapeDtypeStruct((8, M, C), blk.dtype),
                  scratch_shapes=[pltpu.SemaphoreType.DMA((7,)), pltpu.SemaphoreType.DMA((7,)), pltpu.SemaphoreType.DMA])(blk)


def _pack_rows(n_elems, width=PACK_W, align=PACK_ROW_ALIGN):
    rows = -(-n_elems // width)
    return -(-rows // align) * align


def _pack(arrays, dtype, width=PACK_W, align=PACK_ROW_ALIGN):
    flat = jnp.concatenate([a.astype(dtype).reshape(-1) for a in arrays])
    rows = _pack_rows(flat.shape[0], width, align)
    flat = jnp.pad(flat, (0, rows * width - flat.shape[0]))
    return flat.reshape(rows, width)


def _pack_small(arrays):
    return _pack(arrays, F32, width=128, align=8)


def _unpack(flat, shapes):
    out, off = [], 0
    for shp in shapes:
        n = int(np.prod(shp))
        out.append(flat[..., off:off + n].reshape(flat.shape[:-1] + tuple(shp)))
        off += n
    return out


def _doubled_heads(x2d, n_heads):
    S = x2d.shape[0]
    h = x2d.reshape(S, n_heads, HEAD_DIM).transpose(1, 0, 2)
    return jnp.concatenate([h, h], axis=-1)


def _mlp_fwd(h, g, w_up4, w_down, tag):
    (n,) = _rms_fwd(h, [g], f"rms_mlp{tag}")
    u, a = _matmul(n, w_up4, "nn", f"up{tag}", out_dtypes=(F32, BF16), chipwise="b",
                   epilogue=lambda acc: (acc, jnp.square(jnp.maximum(acc, 0.0))))
    h_out = _matmul(a, w_down, "nn", f"down{tag}", extras=(h,), epilogue=lambda acc, res: (res + acc,), tk=1024)
    return h_out, (n, u, a)


def _mlp_bwd(dh_out, h, g, w_up4, w_down, saved, tag):
    n, u, a = saved
    dw_down = _matmul(a, dh_out, "tn", f"dw_down{tag}", out_dtypes=(BF16,))
    du = _matmul(dh_out, w_down, "nt", f"du{tag}", out_dtypes=(BF16,), extras=(u,),
                 epilogue=lambda acc, uu: (acc * (2.0 * jnp.maximum(uu, 0.0)),))
    dw_up = _matmul(n, du, "tn", f"dw_up{tag}", out_dtypes=(BF16,), chipwise="out")
    dn = _matmul(du, w_up4, "nt", f"dn_mlp{tag}", tk=w_up4.shape[2], chipwise="b")
    dh, (dg,) = _rms_bwd(h, dh_out, [g], [dn], f"rms_mlp_bwd{tag}")
    return dh, dg, dw_up, dw_down


def kernel(x, g_attn, g_mlp, w_in_a, b_f, gq_a, gk_a, w_out_a, g_kv, w_kv, gk_b, w_q_b, gq_b, sinks, rel_bias, w_out_b, w_up, w_down, loss_target, m_g_attn, m_g_mlp, m_w_in_a, m_b_f, m_gq_a, m_gk_a, m_w_out_a, m_g_kv, m_w_kv, m_gk_b, m_w_q_b, m_gq_b, m_sinks, m_rel_bias, m_w_out_b, m_w_up, m_w_down, v_g_attn, v_g_mlp, v_w_in_a, v_b_f, v_gq_a, v_gk_a, v_w_out_a, v_g_kv, v_w_kv, v_gk_b, v_w_q_b, v_gq_b, v_sinks, v_rel_bias, v_w_out_b, v_w_up, v_w_down):
    given = dict(locals())
    S, D = x.shape[1], x.shape[2]
    H = D // HEAD_DIM
    KVH = w_kv.shape[1] // (2 * HEAD_DIM)
    kvw = KVH * HEAD_DIM
    hw = H * HEAD_DIM
    W = WINDOW
    nb = S // W
    c_idx = lax.axis_index("c")
    xs, tgt = x[0], loss_target[0]

    shards = {"w_in_a": w_in_a[0], "w_out_a": w_out_a[0], "w_up0": w_up[0], "w_down0": w_down[0], "w_kv": w_kv,
              "w_q_b": w_q_b[0], "w_out_b": w_out_b[0], "w_up1": w_up[1], "w_down1": w_down[1]}
    parts = list(shards)
    gathered = dict(zip(parts, _allgather_group([shards[n].astype(BF16) for n in parts], "allgather_weights")))
    win = jnp.moveaxis(gathered["w_in_a"], 0, 1).reshape(D, -1)
    win = jnp.pad(win, ((0, 0), (0, (-win.shape[1]) % 128)))
    wout_a, wq_b, wout_b = (gathered[n].reshape(-1, D) for n in ("w_out_a", "w_q_b", "w_out_b"))
    wkv = gathered["w_kv"].reshape(D, -1)
    wup = [gathered["w_up0"], gathered["w_up1"]]
    wdown = [gathered["w_down0"].reshape(-1, D), gathered["w_down1"].reshape(-1, D)]
    n_in = win.shape[1]
    tile_in = 640 if n_in % 640 == 0 else 128

    vec = lambda a: a.reshape(1, -1)
    twice = lambda a: jnp.tile(a.reshape(1, -1), (1, 2))

    (n0,) = _rms_fwd(xs, [vec(g_attn[0])], "rms_attn0")
    proj = _matmul(n0, win, "nn", "proj_in", tn=tile_in)
    zt = proj[:, 3 * hw:3 * hw + H].T
    c_row = _gate_fwd(zt, b_f.reshape(H, 1), "gate_fwd")
    c_col3, c_row3 = c_row.reshape(H, S, 1), c_row.reshape(H, 1, S)
    o_a, lse_a = _fox_fwd(proj, c_col3, c_row3, twice(gq_a[0]), twice(gk_a[0]), H, "fox_fwd")
    h1 = _matmul(o_a, wout_a, "nn", "out_a", extras=(xs,), epilogue=lambda acc, res: (res + acc,))
    h2, mlp0 = _mlp_fwd(h1, vec(g_mlp[0]), wup[0], wdown[0], "0")

    nkv, n2 = _rms_fwd(h2, [vec(g_kv), vec(g_attn[1])], "rms_attn1")
    kv = _matmul(nkv, wkv, "nn", "proj_kv")
    kk, vv = _doubled_heads(kv[:, :kvw], KVH), _doubled_heads(kv[:, kvw:], KVH)
    q2 = _matmul(n2, wq_b, "nn", "proj_q")
    onehot = jnp.asarray(_bucket_onehot(), dtype=BF16)
    bias = _bias_expand(rel_bias.T, onehot, "bias_expand").reshape(H, W, 2 * W)
    bias_ab = bias.reshape(H // 2, 2 * W, 2 * W)
    bias_t_ab = bias.reshape(H // 2, 2, W, 2 * W).transpose(0, 3, 1, 2).reshape(H // 2, 2 * W, 2 * W)
    sink_ab = jnp.repeat(sinks[0].reshape(H // 2, 2), W, axis=1)
    o_b, lse_b = _swa_fwd(q2, kk, vv, bias_ab, sink_ab.reshape(H // 2, 2 * W, 1), twice(gq_b[0]), twice(gk_b), "swa_fwd")
    h3 = _matmul(o_b, wout_b, "nn", "out_b", extras=(h2,), epilogue=lambda acc, res: (res + acc,))
    h4, mlp1 = _mlp_fwd(h3, vec(g_mlp[1]), wup[1], wdown[1], "1")

    dh4, loss_part = _loss_head(h4, tgt, "loss_head")
    loss = lax.psum(loss_part[0, 0], ("x", "y", "c"))

    dh3, dg_mlp1, dw_up1, dw_down1 = _mlp_bwd(dh4, h3, vec(g_mlp[1]), wup[1], wdown[1], mlp1, "1")
    dw_out_b = _matmul(o_b, dh3, "tn", "dw_out_b", out_dtypes=(BF16,))
    do_b = _matmul(dh3, wout_b, "nt", "do_b")
    dq2, dk2, dv2, dbias_t_ab, dsink, dgq_b, dgk_b = _swa_bwd(
        q2, kk, vv, bias_t_ab, sink_ab.reshape(H // 2, 1, 2 * W), twice(gq_b[0]), twice(gk_b),
        lse_b.reshape(H // 2, nb, 1, 2 * W), do_b, "swa_bwd")
    dbias = dbias_t_ab.reshape(H // 2, 2 * W, 2, W).transpose(0, 2, 3, 1).reshape(H, W * 2 * W)
    d_rel_bias = _bias_reduce(dbias, onehot, "bias_reduce").T
    dw_q_b = _matmul(n2, dq2, "tn", "dw_q_b", out_dtypes=(BF16,))
    dn2 = _matmul(dq2, wq_b, "nt", "dn2")
    dkv = jnp.concatenate([dk2[h, :, :HEAD_DIM] for h in range(KVH)] + [dv2[h, :, :HEAD_DIM] for h in range(KVH)],
                          axis=1).astype(BF16)
    dw_kv = _matmul(nkv, dkv, "tn", "dw_kv", out_dtypes=(BF16,))
    dnkv = _matmul(dkv, wkv, "nt", "dnkv")
    dh2, (dg_kv, dg_attn1) = _rms_bwd(h2, dh3, [vec(g_kv), vec(g_attn[1])], [dnkv, dn2], "rms_attn1_bwd")

    dh1, dg_mlp0, dw_up0, dw_down0 = _mlp_bwd(dh2, h1, vec(g_mlp[0]), wup[0], wdown[0], mlp0, "0")
    dw_out_a = _matmul(o_a, dh1, "tn", "dw_out_a", out_dtypes=(BF16,))
    do_a = _matmul(dh1, wout_a, "nt", "do_a")
    dq_a, dk_a, dv_a, dc_col, dgq_a, dgk_a = _fox_bwd(
        proj, c_col3, c_row3, twice(gq_a[0]), twice(gk_a[0]), lse_a.reshape(H, 1, S), do_a, H, "fox_bwd")
    dzt, db_f = _gate_bwd(dc_col.reshape(H, S), zt, b_f.reshape(H, 1), "gate_bwd")
    dproj = jnp.concatenate([dq_a, dk_a, dv_a, dzt.T.astype(BF16), jnp.zeros((S, n_in - 3 * hw - H), BF16)], axis=1)
    dw_in = _matmul(n0, dproj, "tn", "dw_in", out_dtypes=(BF16,), tn=tile_in)
    dn0 = _matmul(dproj, win, "nt", "dn0", tk=tile_in)
    grad_x, (dg_attn0,) = _rms_bwd(xs, dh1, [vec(g_attn[0])], [dn0], "rms_attn0_bwd")

    dw_in4 = dw_in[:, :3 * hw + H].reshape(D, N_CHIPS, -1).transpose(1, 0, 2)
    chipwise = {"w_in_a": dw_in4, "w_out_a": dw_out_a.reshape(N_CHIPS, -1, D), "w_up0": dw_up0,
                "w_down0": dw_down0.reshape(N_CHIPS, -1, D), "w_kv": dw_kv.reshape(N_CHIPS, -1, 2 * kvw),
                "w_q_b": dw_q_b.reshape(N_CHIPS, -1, D), "w_out_b": dw_out_b.reshape(N_CHIPS, -1, D), "w_up1": dw_up1,
                "w_down1": dw_down1.reshape(N_CHIPS, -1, D)}
    keep, got = _swap_halves_group([chipwise[n] for n in parts], "swap_grad_halves")
    pair_sums = [_sum_list([k.reshape(-1, k.shape[2]), g.reshape(-1, g.shape[2])], "sum_core_pair_" + n, BF16).reshape(k.shape)
                 for n, k, g in zip(parts, keep, got)]
    landed = _scatter_group(pair_sums, "scatter_grads")
    halves = [_sum_parts(l, "sum_chips_" + n, F32) for n, l in zip(parts, landed)]
    reduced = {n: r.reshape(shards[n].shape) for n, r in zip(parts, _share_halves_group(halves, "share_reduced_halves"))}
    big_grads = [reduced["w_in_a"][None], reduced["w_out_a"][None], reduced["w_kv"], reduced["w_q_b"][None],
                 reduced["w_out_b"][None], jnp.stack([reduced["w_up0"], reduced["w_up1"]]),
                 jnp.stack([reduced["w_down0"], reduced["w_down1"]])]

    small_grads = {
        "g_attn": jnp.concatenate([dg_attn0, dg_attn1], axis=0), "g_mlp": jnp.concatenate([dg_mlp0, dg_mlp1], axis=0),
        "b_f": db_f.reshape(1, H), "gq_a": dgq_a[:, :HEAD_DIM], "gk_a": dgk_a[:, :HEAD_DIM], "g_kv": dg_kv.reshape(-1),
        "gk_b": dgk_b[0, :HEAD_DIM], "gq_b": dgq_b[:, :HEAD_DIM], "sinks": dsink[:, 0, 0].reshape(1, H), "rel_bias": d_rel_bias,
    }
    small_shapes = [given[n].shape for n in SMALL]
    spack = _pack_small([small_grads[n] for n in SMALL])
    small_sum = _sum_parts(_allgather_small(spack, "allgather_small"), "sum_small", F32, tr=spack.shape[0])
    small_red = _unpack(small_sum.reshape(-1), small_shapes)

    grads = dict(zip([n for n, _ in BIG], big_grads))
    grads.update(dict(zip(SMALL, small_red)))
    sw = _pack_small([given[n] for n in SMALL])
    sm = _pack_small([given["m_" + n] for n in SMALL])
    sv = _pack_small([given["v_" + n] for n in SMALL])
    sd, sm2, sv2 = _adamw(sw, small_sum, sm, sv, "adamw_small", tr=sw.shape[0])
    delta = dict(zip(SMALL, _unpack(sd.reshape(-1), small_shapes)))
    new_m = dict(zip(SMALL, _unpack(sm2.reshape(-1), small_shapes)))
    new_v = dict(zip(SMALL, _unpack(sv2.reshape(-1), small_shapes)))
    for n, _ in BIG:
        w = given[n]
        two_d = (-1, w.shape[-1])
        d, m2, v2 = _adamw(w.reshape(two_d), grads[n].reshape(two_d), given["m_" + n].reshape(two_d),
                           given["v_" + n].reshape(two_d), "adamw_" + n)
        delta[n], new_m[n], new_v[n] = d.reshape(w.shape), m2.reshape(w.shape), v2.reshape(w.shape)

    order = ["g_attn", "g_mlp", "w_in_a", "b_f", "gq_a", "gk_a", "w_out_a", "g_kv", "w_kv", "gk_b", "w_q_b", "gq_b",
             "sinks", "rel_bias", "w_out_b", "w_up", "w_down"]
    return (loss, grad_x[None], *[grads[n] for n in order], *[delta[n] for n in order],
            *[new_m[n] for n in order], *[new_v[n] for n in order])
```

```python
import numpy as np
import jax
import jax.numpy as jnp
from jax import lax
from jax.experimental import pallas as pl
from jax.experimental.pallas import tpu as pltpu

F32 = jnp.float32
BF16 = jnp.bfloat16
MESH = pl.DeviceIdType.MESH

HEAD_DIM = 64
LANES = 128
WINDOW = 128
N_BUCKETS = 32
REL_MAX_DIST = 128
NORM_EPS = 1e-6
ADAM_LR = 0.001
ADAM_B1 = 0.9
ADAM_B2 = 0.999
ADAM_EPS = 1e-08
ADAM_WD = 0.01
ADAM_STEP = 10
NEG = -1e30
N_CHIPS = 4
PACK_W = 1024
PACK_ROW_ALIGN = 256
VMEM_LIMIT = 56 * 1024 * 1024
HBM_SPEC = pl.BlockSpec(memory_space=pltpu.HBM)
VMEM_SPEC = pl.BlockSpec(memory_space=pltpu.VMEM)

BIG = (("w_in_a", 2), ("w_out_a", 1), ("w_kv", 0), ("w_q_b", 1), ("w_out_b", 1), ("w_up", 2), ("w_down", 1))
SMALL = ("g_attn", "g_mlp", "b_f", "gq_a", "gk_a", "g_kv", "gk_b", "gq_b", "sinks", "rel_bias")


def _pcall(body, **kw):
    return pl.pallas_call(body, **kw)


def _params(sem=None):
    return pltpu.CompilerParams(dimension_semantics=sem, vmem_limit_bytes=VMEM_LIMIT)


def _rinv(x):
    return lax.rsqrt(jnp.mean(x * x, axis=-1, keepdims=True) + NORM_EPS)


def _dot(a, b, dims, precision=None):
    return lax.dot_general(a, b, (dims, ((), ())), precision=precision, preferred_element_type=F32)


NN = ((1,), (0,))
NT = ((1,), (1,))
TN = ((0,), (0,))


def _accumulate(ref, val, first):
    @pl.when(first)
    def _():
        ref[...] = val

    @pl.when(jnp.logical_not(first))
    def _():
        ref[...] += val


def _matmul(a, b, mode, name, out_dtypes=(F32,), extras=(), epilogue=None, tm=512, tn=512, tk=None, chipwise=None):
    if chipwise == "b":
        nc = b.shape[2]
        M, K = a.shape
        (K2, N) = (b.shape[1], N_CHIPS * nc) if mode == "nn" else (N_CHIPS * nc, b.shape[1])
    elif mode == "nn":
        (M, K), (K2, N) = a.shape, b.shape
    elif mode == "nt":
        (M, K), (N, K2) = a.shape, b.shape
    else:
        (K, M), (K2, N) = a.shape, b.shape
    assert K == K2, (a.shape, b.shape, mode)
    tm, tn = min(tm, M), min(tn, N)
    tk = K if tk is None else tk
    assert M % tm == 0 and N % tn == 0 and K % tk == 0, (M, N, K, tm, tn, tk)
    nk = K // tk
    dims = {"nn": NN, "nt": NT, "tn": TN}[mode]
    a_spec = pl.BlockSpec((tk, tm), lambda i, j, k: (k, i)) if mode == "tn" else pl.BlockSpec((tm, tk), lambda i, j, k: (i, k))
    b_spec = pl.BlockSpec((tn, tk), lambda i, j, k: (j, k)) if mode == "nt" else pl.BlockSpec((tk, tn), lambda i, j, k: (k, j))
    o_spec = pl.BlockSpec((tm, tn), lambda i, j, k: (i, j))
    out_shape = (M, N)
    if chipwise == "b" and mode == "nn":
        per = nc // tn
        assert tk == K and nc % tn == 0
        b_spec = pl.BlockSpec((None, tk, tn), lambda i, j, k: (j // per, 0, j % per))
    elif chipwise == "b":
        assert mode == "nt" and tk == nc
        b_spec = pl.BlockSpec((None, tn, tk), lambda i, j, k: (k, j, 0))
    elif chipwise == "out":
        per = (N // N_CHIPS) // tn
        assert (N // N_CHIPS) % tn == 0
        o_spec = pl.BlockSpec((None, tm, tn), lambda i, j, k: (j // per, i, j % per))
        out_shape = (N_CHIPS, M, N // N_CHIPS)
        assert not extras
    n_ex, n_out = len(extras), len(out_dtypes)

    def body(*refs):
        a_ref, b_ref = refs[0], refs[1]
        ex_refs = refs[2:2 + n_ex]
        out_refs = refs[2 + n_ex:2 + n_ex + n_out]
        acc_ref = refs[2 + n_ex + n_out]
        k = pl.program_id(2)
        part = _dot(a_ref[...].astype(BF16), b_ref[...].astype(BF16), dims)

        @pl.when(k == 0)
        def _():
            acc_ref[...] = part

        @pl.when(k > 0)
        def _():
            acc_ref[...] += part

        @pl.when(k == nk - 1)
        def _():
            acc = acc_ref[...]
            outs = (acc,) if epilogue is None else epilogue(acc, *[r[...] for r in ex_refs])
            for r, o in zip(out_refs, outs):
                r[...] = o.astype(r.dtype)

    outs = _pcall(
        body, name=name, grid=(M // tm, N // tn, nk),
        in_specs=[a_spec, b_spec] + [o_spec] * n_ex,
        out_specs=[o_spec] * n_out,
        out_shape=[jax.ShapeDtypeStruct(out_shape, dt) for dt in out_dtypes],
        scratch_shapes=[pltpu.VMEM((tm, tn), F32)],
        compiler_params=_params(("parallel", "parallel", "arbitrary")),
    )(a, b, *extras)
    return outs[0] if n_out == 1 else outs


def _rms_fwd(x, gains, name, ts=256):
    S, D = x.shape
    ts = min(ts, S)
    n = len(gains)

    def body(*refs):
        x_ref, g_refs, o_refs = refs[0], refs[1:1 + n], refs[1 + n:]
        xv = x_ref[...]
        xh = xv * _rinv(xv)
        for g_ref, o_ref in zip(g_refs, o_refs):
            o_ref[...] = (xh * g_ref[...]).astype(BF16)

    row = pl.BlockSpec((ts, D), lambda i: (i, 0))
    vec = pl.BlockSpec((1, D), lambda i: (0, 0))
    return _pcall(body, name=name, grid=(S // ts,), in_specs=[row] + [vec] * n, out_specs=[row] * n,
                  out_shape=[jax.ShapeDtypeStruct((S, D), BF16)] * n, compiler_params=_params(("parallel",)))(x, *gains)


def _rms_bwd(x, dres, gains, dns, name, ts=256):
    S, D = x.shape
    ts = min(ts, S)
    n = len(gains)

    def body(*refs):
        x_ref, dres_ref = refs[0], refs[1]
        g_refs, dn_refs = refs[2:2 + n], refs[2 + n:2 + 2 * n]
        dx_ref, dg_refs = refs[2 + 2 * n], refs[3 + 2 * n:]
        xv = x_ref[...]
        r = _rinv(xv)
        xh = xv * r
        dx = dres_ref[...]
        first = pl.program_id(0) == 0
        for g_ref, dn_ref, dg_ref in zip(g_refs, dn_refs, dg_refs):
            dn = dn_ref[...].astype(F32)
            _accumulate(dg_ref, jnp.sum(dn * xh, axis=0, keepdims=True), first)
            dxh = dn * g_ref[...]
            dx = dx + r * (dxh - xh * jnp.mean(dxh * xh, axis=-1, keepdims=True))
        dx_ref[...] = dx

    row = pl.BlockSpec((ts, D), lambda i: (i, 0))
    vec = pl.BlockSpec((1, D), lambda i: (0, 0))
    outs = _pcall(body, name=name, grid=(S // ts,), in_specs=[row, row] + [vec] * n + [row] * n,
                  out_specs=[row] + [vec] * n,
                  out_shape=[jax.ShapeDtypeStruct((S, D), F32)] + [jax.ShapeDtypeStruct((1, D), F32)] * n,
                  compiler_params=_params(("arbitrary",)))(x, dres, *gains, *dns)
    return outs[0], outs[1:]


def _loss_head(h, tgt, name, ts=256):
    S, D = h.shape
    ts = min(ts, S)

    def body(h_ref, t_ref, dh_ref, loss_ref):
        err = h_ref[...] - t_ref[...]
        dh_ref[...] = err * (1.0 / D)
        part = 0.5 * jnp.sum(jnp.mean(err * err, axis=-1, keepdims=True), axis=0, keepdims=True)
        _accumulate(loss_ref, part, pl.program_id(0) == 0)

    row = pl.BlockSpec((ts, D), lambda i: (i, 0))
    return _pcall(body, name=name, grid=(S // ts,), in_specs=[row, row],
                  out_specs=[row, pl.BlockSpec((1, 1), lambda i: (0, 0))],
                  out_shape=[jax.ShapeDtypeStruct((S, D), F32), jax.ShapeDtypeStruct((1, 1), F32)],
                  compiler_params=_params(("arbitrary",)))(h, tgt)


def _gate_fwd(zt, bf, name):
    H, S = zt.shape
    nb = S // 128

    def body(z_ref, b_ref, c_ref):
        z = z_ref[...] + b_ref[...]
        lf = jnp.minimum(z, 0.0) - jnp.log(1.0 + jnp.exp(-jnp.abs(z)))
        upper = (lax.broadcasted_iota(jnp.int32, (128, 128), 0) <= lax.broadcasted_iota(jnp.int32, (128, 128), 1)).astype(F32)
        carry = jnp.zeros((H, 1), F32)
        for blk in range(nb):
            cs = _dot(lf[:, blk * 128:(blk + 1) * 128], upper, NN, precision=lax.Precision.HIGHEST) + carry
            c_ref[:, blk * 128:(blk + 1) * 128] = cs
            carry = cs[:, 127:128]

    return _pcall(body, name=name, in_specs=[VMEM_SPEC, VMEM_SPEC], out_specs=VMEM_SPEC,
                  out_shape=jax.ShapeDtypeStruct((H, S), F32))(zt, bf)


def _gate_bwd(dct, zt, bf, name):
    H, S = zt.shape
    nb = S // 128

    def body(dc_ref, z_ref, b_ref, dz_ref, db_ref):
        z = z_ref[...] + b_ref[...]
        e = jnp.exp(-jnp.abs(z))
        sig_neg = jnp.where(z >= 0, e, 1.0) / (1.0 + e)
        lower = (lax.broadcasted_iota(jnp.int32, (128, 128), 0) >= lax.broadcasted_iota(jnp.int32, (128, 128), 1)).astype(F32)
        dc = dc_ref[...]
        carry = jnp.zeros((H, 1), F32)
        db = jnp.zeros((H, 1), F32)
        for blk in reversed(range(nb)):
            sl = slice(blk * 128, (blk + 1) * 128)
            dlf = _dot(dc[:, sl], lower, NN, precision=lax.Precision.HIGHEST) + carry
            carry = dlf[:, 0:1]
            dz = dlf * sig_neg[:, sl]
            dz_ref[:, sl] = dz
            db = db + jnp.sum(dz, axis=1, keepdims=True)
        db_ref[...] = db

    return _pcall(body, name=name, in_specs=[VMEM_SPEC] * 3, out_specs=[VMEM_SPEC] * 2,
                  out_shape=[jax.ShapeDtypeStruct((H, S), F32), jax.ShapeDtypeStruct((H, 1), F32)])(dct, zt, bf)


def _lane_is_a():
    return lax.broadcasted_iota(jnp.int32, (1, LANES), 1) < HEAD_DIM


def _per_head_mean(x, is_a):
    sa = jnp.sum(jnp.where(is_a, x, 0.0), axis=-1, keepdims=True)
    sb = jnp.sum(jnp.where(is_a, 0.0, x), axis=-1, keepdims=True)
    return jnp.where(is_a, sa, sb) / HEAD_DIM


def _pair_norm(raw, gain, is_a):
    return raw * lax.rsqrt(_per_head_mean(raw * raw, is_a) + NORM_EPS) * gain


def _pair_norm_bwd(raw, gain, dnormed, is_a):
    r = lax.rsqrt(_per_head_mean(raw * raw, is_a) + NORM_EPS)
    xh = raw * r
    dgain = jnp.sum(dnormed * xh, axis=0, keepdims=True)
    dxh = dnormed * gain
    return r * (dxh - xh * _per_head_mean(dxh * xh, is_a)), dgain


def _fold_heads(x):
    i = lax.broadcasted_iota(jnp.int32, (LANES, LANES), 0)
    j = lax.broadcasted_iota(jnp.int32, (LANES, LANES), 1)
    fold = ((i == j) | (i == j + HEAD_DIM) | (i + HEAD_DIM == j)).astype(F32)
    return _dot(x, fold, NN, precision=lax.Precision.HIGHEST)


def _fold_row(ref):
    ref[...] = _fold_heads(jnp.broadcast_to(ref[...], (8, LANES)))[0:1, :]


def _tri_mask(t, keys_on_rows):
    r = lax.broadcasted_iota(jnp.int32, (t, t), 0)
    c = lax.broadcasted_iota(jnp.int32, (t, t), 1)
    return (r <= c) if keys_on_rows else (r >= c)


def _fox_fwd(proj, c_col, c_row, gq2, gk2, n_heads, name, t=256):
    S = proj.shape[0]
    H = n_heads
    P = H // 2
    t = min(t, S)
    nq = S // t

    def body(q_ref, k_ref, v_ref, cc_ref, cr_ref, gq_ref, gk_ref, o_ref, lse_ref, qs_s, kb_s, vb_s):
        is_a = _lane_is_a()
        qn = _pair_norm(q_ref[...], gq_ref[...], is_a) * 0.125
        qs_s[0] = jnp.where(is_a, qn, 0.0).astype(BF16)
        qs_s[1] = jnp.where(is_a, 0.0, qn).astype(BF16)
        kb_s[...] = _pair_norm(k_ref[...], gk_ref[...], is_a).astype(BF16)
        vb_s[...] = v_ref[...].astype(BF16)
        causal = _tri_mask(t, False)
        for i in range(nq):
            t0 = i * t
            rows = slice(t0, t0 + t)
            o_pair = None
            for a in range(2):
                qi = qs_s[a, rows, :]
                ci = cc_ref[a, rows, :]
                s_d = jnp.where(causal, _dot(qi, kb_s[rows, :], NT) + ci - cr_ref[a, :, rows], NEG)
                m = jnp.max(s_d, axis=-1, keepdims=True)
                if i > 0:
                    s_l = _dot(qi, kb_s[0:t0, :], NT) + ci - cr_ref[a, :, 0:t0]
                    m = jnp.maximum(m, jnp.max(s_l, axis=-1, keepdims=True))
                p_d = jnp.exp(s_d - m)
                l = jnp.sum(p_d, axis=-1, keepdims=True)
                acc = _dot(p_d.astype(BF16), vb_s[rows, :], NN)
                if i > 0:
                    p_l = jnp.exp(s_l - m)
                    l = l + jnp.sum(p_l, axis=-1, keepdims=True)
                    acc = acc + _dot(p_l.astype(BF16), vb_s[0:t0, :], NN)
                o_a = acc / l
                lse_ref[a, rows, :] = m + jnp.log(l)
                o_pair = o_a if a == 0 else jnp.where(is_a, o_pair, o_a)
            o_ref[rows, :] = o_pair.astype(BF16)

    def cols(off):
        return pl.BlockSpec((S, LANES), lambda p: (0, off + p))

    col = pl.BlockSpec((2, S, 1), lambda p: (p, 0, 0))
    rowv = pl.BlockSpec((2, 1, S), lambda p: (p, 0, 0))
    gain = pl.BlockSpec((1, LANES), lambda p: (0, 0))
    return _pcall(body, name=name, grid=(P,), in_specs=[cols(0), cols(P), cols(2 * P), col, rowv, gain, gain],
                  out_specs=[cols(0), col],
                  out_shape=[jax.ShapeDtypeStruct((S, H * HEAD_DIM), BF16), jax.ShapeDtypeStruct((H, S, 1), F32)],
                  scratch_shapes=[pltpu.VMEM((2, S, LANES), BF16), pltpu.VMEM((S, LANES), BF16), pltpu.VMEM((S, LANES), BF16)],
                  compiler_params=_params(("parallel",)))(proj, proj, proj, c_col, c_row, gq2, gk2)


def _fox_bwd(proj, c_col, c_row, gq2, gk2, lse_row, do, n_heads, name, t=256):
    S = proj.shape[0]
    H = n_heads
    P = H // 2
    t = min(t, S)
    nq = S // t
    assert t % LANES == 0

    def body(q_ref, k_ref, v_ref, cc_ref, cr_ref, gq_ref, gk_ref, lr_ref, do_ref,
             dq_ref, dk_ref, dv_ref, dc_ref, dgq_ref, dgk_ref,
             qs_s, kb_s, kt_s, vb_s, dob_s, dq_s, dk_s, dv_s, dcs_s):
        is_a = _lane_is_a()
        qn = _pair_norm(q_ref[...], gq_ref[...], is_a) * 0.125
        qs_s[0] = jnp.where(is_a, qn, 0.0).astype(BF16)
        qs_s[1] = jnp.where(is_a, 0.0, qn).astype(BF16)
        kn = _pair_norm(k_ref[...], gk_ref[...], is_a)
        kb_s[...] = kn.astype(BF16)
        kt_s[0] = jnp.where(is_a, kn, 0.0).T.astype(BF16)
        kt_s[1] = jnp.where(is_a, 0.0, kn).T.astype(BF16)
        vb_s[...] = v_ref[...].astype(BF16)
        dov = do_ref[...]
        dob_s[0] = jnp.where(is_a, dov, 0.0).astype(BF16)
        dob_s[1] = jnp.where(is_a, 0.0, dov).astype(BF16)
        dk_s[...] = jnp.zeros((S, LANES), F32)
        dv_s[...] = jnp.zeros((S, LANES), F32)
        dcs_s[...] = jnp.zeros((2, S, LANES), F32)
        causal = _tri_mask(t, True)
        for i in range(nq):
            t0 = i * t
            rows = slice(t0, t0 + t)
            dq_t = jnp.zeros((LANES, t), F32)
            for a in range(2):
                qi = qs_s[a, rows, :]
                doi = dob_s[a, rows, :]
                cri = cr_ref[a, :, rows]
                lri = lr_ref[a, :, rows]

                def probs(keys, masked, a=a, qi=qi, doi=doi, cri=cri, lri=lri):
                    p_t = jnp.exp(_dot(kb_s[keys, :], qi, NT) + cri - cc_ref[a, keys, :] - lri)
                    if masked:
                        p_t = jnp.where(causal, p_t, 0.0)
                    return p_t, _dot(vb_s[keys, :], doi, NT)

                parts = [(rows,) + probs(rows, True)]
                if i > 0:
                    parts.append((slice(0, t0),) + probs(slice(0, t0), False))
                delta = sum(jnp.sum(p_t * dp_t, axis=0, keepdims=True) for _, p_t, dp_t in parts)
                for keys, p_t, dp_t in parts:
                    ds_t = p_t * (dp_t - delta)
                    dsb = ds_t.astype(BF16)
                    dv_s[keys, :] += _dot(p_t.astype(BF16), doi, NN)
                    dk_s[keys, :] += _dot(dsb, qi, NN)
                    dq_t = dq_t + _dot(kt_s[a, :, keys], dsb, NN)
                    dcs_s[a, keys, :] += sum(ds_t[:, b * LANES:(b + 1) * LANES] for b in range(t // LANES))
            dq_s[rows, :] = dq_t.T
        first = pl.program_id(0) == 0
        last = pl.program_id(0) == P - 1
        dq_raw, dgq = _pair_norm_bwd(q_ref[...], gq_ref[...], dq_s[...] * 0.125, is_a)
        dq_ref[...] = dq_raw.astype(BF16)
        _accumulate(dgq_ref, dgq, first)
        dk_raw, dgk = _pair_norm_bwd(k_ref[...], gk_ref[...], dk_s[...], is_a)
        dk_ref[...] = dk_raw.astype(BF16)
        _accumulate(dgk_ref, dgk, first)
        dv_ref[...] = dv_s[...].astype(BF16)
        for a in range(2):
            dc_ref[a] = -jnp.sum(dcs_s[a], axis=1, keepdims=True)

        @pl.when(last)
        def _():
            _fold_row(dgq_ref)
            _fold_row(dgk_ref)

    def cols(off):
        return pl.BlockSpec((S, LANES), lambda p: (0, off + p))

    col = pl.BlockSpec((2, S, 1), lambda p: (p, 0, 0))
    rowv = pl.BlockSpec((2, 1, S), lambda p: (p, 0, 0))
    gain = pl.BlockSpec((1, LANES), lambda p: (0, 0))
    wide = jax.ShapeDtypeStruct((S, H * HEAD_DIM), BF16)
    gs = jax.ShapeDtypeStruct((1, LANES), F32)
    return _pcall(body, name=name, grid=(P,),
                  in_specs=[cols(0), cols(P), cols(2 * P), col, rowv, gain, gain, rowv, cols(0)],
                  out_specs=[cols(0), cols(0), cols(0), col, gain, gain],
                  out_shape=[wide, wide, wide, jax.ShapeDtypeStruct((H, S, 1), F32), gs, gs],
                  scratch_shapes=[pltpu.VMEM((2, S, LANES), BF16), pltpu.VMEM((S, LANES), BF16), pltpu.VMEM((2, LANES, S), BF16),
                                  pltpu.VMEM((S, LANES), BF16), pltpu.VMEM((2, S, LANES), BF16)]
                  + [pltpu.VMEM((S, LANES), F32)] * 3 + [pltpu.VMEM((2, S, LANES), F32)],
                  compiler_params=_params(("arbitrary",)))(proj, proj, proj, c_col, c_row, gq2, gk2, lse_row, do)


def _bucket_onehot():
    W = WINDOW
    dist = np.arange(W)[:, None] + W - np.arange(2 * W)[None, :]
    n = np.maximum(dist, 0)
    max_exact = N_BUCKETS // 2
    large = max_exact + (np.log(np.maximum(n, 1) / max_exact) / np.log(REL_MAX_DIST / max_exact)
                         * (N_BUCKETS - max_exact)).astype(np.int32)
    large = np.minimum(large, N_BUCKETS - 1)
    bucket = np.where(n < max_exact, n, large).astype(np.int32)
    valid = (dist >= 0) & (dist < W)
    onehot = (bucket[None] == np.arange(N_BUCKETS)[:, None, None]) & valid[None]
    return onehot.reshape(N_BUCKETS, W * 2 * W).astype(np.float32)


def _bias_expand(rel_bias_t, onehot, name, tn=4096):
    HQ, NB = rel_bias_t.shape
    L = onehot.shape[1]

    def body(r_ref, oh_ref, out_ref):
        out_ref[...] = _dot(r_ref[...], oh_ref[...].astype(F32), NN, precision=lax.Precision.HIGHEST)

    return _pcall(body, name=name, grid=(L // tn,),
                  in_specs=[pl.BlockSpec((HQ, NB), lambda i: (0, 0)), pl.BlockSpec((NB, tn), lambda i: (0, i))],
                  out_specs=pl.BlockSpec((HQ, tn), lambda i: (0, i)),
                  out_shape=jax.ShapeDtypeStruct((HQ, L), F32), compiler_params=_params(("parallel",)))(rel_bias_t, onehot)


def _bias_reduce(dbias, onehot, name, tk=4096):
    HQ, L = dbias.shape
    NB = onehot.shape[0]

    def body(d_ref, oh_ref, out_ref):
        part = _dot(d_ref[...], oh_ref[...].astype(F32), NT, precision=lax.Precision.HIGHEST)
        _accumulate(out_ref, part, pl.program_id(0) == 0)

    return _pcall(body, name=name, grid=(L // tk,),
                  in_specs=[pl.BlockSpec((HQ, tk), lambda i: (0, i)), pl.BlockSpec((NB, tk), lambda i: (0, i))],
                  out_specs=pl.BlockSpec((HQ, NB), lambda i: (0, 0)),
                  out_shape=jax.ShapeDtypeStruct((HQ, NB), F32), compiler_params=_params(("arbitrary",)))(dbias, onehot)


def _stacked_query_index(n_rows_or_cols_axis, shape):
    idx = lax.broadcasted_iota(jnp.int32, shape, n_rows_or_cols_axis)
    return jnp.where(idx >= WINDOW, idx - WINDOW, idx)


def _swa_fwd(qproj, kk, vv, bias_ab, sink_col, gq2, gk2, name):
    S, HQD = qproj.shape
    KVH = kk.shape[0]
    PP = HQD // LANES
    NP = PP // KVH
    W = WINDOW
    nb = S // W

    def body(q_ref, k_ref, v_ref, bias_ref, sink_ref, gq_ref, gk_ref, o_ref, lse_ref, qs_s, kb_s, vb_s):
        is_a = _lane_is_a()
        qn = _pair_norm(q_ref[...], gq_ref[...], is_a) * 0.125
        qs_s[0] = jnp.where(is_a, qn, 0.0).astype(BF16)
        qs_s[1] = jnp.where(is_a, 0.0, qn).astype(BF16)
        kb_s[...] = _pair_norm(k_ref[...], gk_ref[...], is_a).astype(BF16)
        vb_s[...] = v_ref[...].astype(BF16)
        sink = sink_ref[...]
        qi1 = _stacked_query_index(0, (2 * W, W))
        first_valid = lax.broadcasted_iota(jnp.int32, (2 * W, W), 1) <= qi1
        qi2 = _stacked_query_index(0, (2 * W, 2 * W))
        key2 = lax.broadcasted_iota(jnp.int32, (2 * W, 2 * W), 1)
        band_valid = (key2 > qi2) & (key2 <= qi2 + W)
        for n in range(nb):
            rows = slice(n * W, (n + 1) * W)
            keys = slice(0, W) if n == 0 else slice((n - 1) * W, (n + 1) * W)
            lhs = jnp.concatenate([qs_s[0, rows, :], qs_s[1, rows, :]], axis=0)
            s = _dot(lhs, kb_s[keys, :], NT) + (bias_ref[:, W:2 * W] if n == 0 else bias_ref[...])
            s = jnp.where(first_valid if n == 0 else band_valid, s, NEG)
            m = jnp.maximum(jnp.max(s, axis=-1, keepdims=True), sink)
            e = jnp.exp(s - m)
            l = jnp.sum(e, axis=-1, keepdims=True) + jnp.exp(sink - m)
            o_ab = _dot(e.astype(BF16), vb_s[keys, :], NN) / l
            o_ref[rows, :] = jnp.where(is_a, o_ab[0:W, :], o_ab[W:2 * W, :]).astype(BF16)
            lse_ref[n] = m + jnp.log(l)

    qcols = pl.BlockSpec((S, LANES), lambda a, g: (0, a * NP + g))
    kvs = pl.BlockSpec((None, S, LANES), lambda a, g: (a, 0, 0))
    gain = pl.BlockSpec((1, LANES), lambda a, g: (0, 0))
    return _pcall(body, name=name, grid=(KVH, NP),
                  in_specs=[qcols, kvs, kvs, pl.BlockSpec((None, 2 * W, 2 * W), lambda a, g: (a * NP + g, 0, 0)),
                            pl.BlockSpec((None, 2 * W, 1), lambda a, g: (a * NP + g, 0, 0)), gain, gain],
                  out_specs=[qcols, pl.BlockSpec((None, nb, 2 * W, 1), lambda a, g: (a * NP + g, 0, 0, 0))],
                  out_shape=[jax.ShapeDtypeStruct((S, HQD), BF16), jax.ShapeDtypeStruct((PP, nb, 2 * W, 1), F32)],
                  scratch_shapes=[pltpu.VMEM((2, S, LANES), BF16), pltpu.VMEM((S, LANES), BF16), pltpu.VMEM((S, LANES), BF16)],
                  compiler_params=_params(("parallel", "parallel")))(qproj, kk, vv, bias_ab, sink_col, gq2, gk2)


def _swa_bwd(qproj, kk, vv, bias_t_ab, sink_row, gq2, gk2, lse_row, do, name):
    S, HQD = qproj.shape
    KVH = kk.shape[0]
    PP = HQD // LANES
    NP = PP // KVH
    W = WINDOW
    nb = S // W

    def body(q_ref, k_ref, v_ref, bias_ref, sink_ref, gq_ref, gk_ref, lr_ref, do_ref,
             dq_ref, dk_ref, dv_ref, db_ref, dsink_ref, dgq_ref, dgk_ref,
             qs_s, kb_s, kt_s, vb_s, dob_s, dq_s, dk_s, dv_s):
        a, g = pl.program_id(0), pl.program_id(1)
        is_a = _lane_is_a()
        qn = _pair_norm(q_ref[...], gq_ref[...], is_a) * 0.125
        qs_s[0] = jnp.where(is_a, qn, 0.0).astype(BF16)
        qs_s[1] = jnp.where(is_a, 0.0, qn).astype(BF16)
        kn = _pair_norm(k_ref[...], gk_ref[...], is_a)
        kb_s[...] = kn.astype(BF16)
        kt_s[...] = kn.T.astype(BF16)
        vb_s[...] = v_ref[...].astype(BF16)
        dov = do_ref[...]
        dob_s[0] = jnp.where(is_a, dov, 0.0).astype(BF16)
        dob_s[1] = jnp.where(is_a, 0.0, dov).astype(BF16)
        sink = sink_ref[...]

        @pl.when(g == 0)
        def _():
            dk_s[...] = jnp.zeros((S, LANES), F32)
            dv_s[...] = jnp.zeros((S, LANES), F32)

        qi1 = _stacked_query_index(1, (W, 2 * W))
        first_valid = lax.broadcasted_iota(jnp.int32, (W, 2 * W), 0) <= qi1
        qi2 = _stacked_query_index(1, (2 * W, 2 * W))
        key2 = lax.broadcasted_iota(jnp.int32, (2 * W, 2 * W), 0)
        band_valid = (key2 > qi2) & (key2 <= qi2 + W)
        head_rows = lax.broadcasted_iota(jnp.int32, (LANES, W), 0) < HEAD_DIM
        db = jnp.zeros((2 * W, 2 * W), F32)
        dsk = jnp.zeros((1, 2 * W), F32)
        pend_k = pend_v = None
        for n in range(nb):
            rows = slice(n * W, (n + 1) * W)
            keys = slice(0, W) if n == 0 else slice((n - 1) * W, (n + 1) * W)
            lhs_q = jnp.concatenate([qs_s[0, rows, :], qs_s[1, rows, :]], axis=0)
            lhs_do = jnp.concatenate([dob_s[0, rows, :], dob_s[1, rows, :]], axis=0)
            lse = lr_ref[n]
            s_t = _dot(kb_s[keys, :], lhs_q, NT) + (bias_ref[W:2 * W, :] if n == 0 else bias_ref[...])
            p_t = jnp.where(first_valid if n == 0 else band_valid, jnp.exp(s_t - lse), 0.0)
            dp_t = _dot(vb_s[keys, :], lhs_do, NT)
            delta = jnp.sum(p_t * dp_t, axis=0, keepdims=True)
            ds_t = p_t * (dp_t - delta)
            dsb = ds_t.astype(BF16)
            dsk = dsk - jnp.exp(sink - lse) * delta
            dv_band = _dot(p_t.astype(BF16), lhs_do, NN)
            dk_band = _dot(dsb, lhs_q, NN)
            dq_t = _dot(kt_s[:, keys], dsb, NN)
            dq_s[rows, :] = jnp.where(head_rows, dq_t[:, 0:W], dq_t[:, W:2 * W]).T
            if n == 0:
                db = jnp.concatenate([jnp.zeros((W, 2 * W), F32), ds_t], axis=0)
                pend_k, pend_v = dk_band, dv_band
            else:
                db = db + ds_t
                prev = slice((n - 1) * W, n * W)
                dk_s[prev, :] += pend_k + dk_band[0:W, :]
                dv_s[prev, :] += pend_v + dv_band[0:W, :]
                pend_k, pend_v = dk_band[W:2 * W, :], dv_band[W:2 * W, :]
        tail = slice((nb - 1) * W, nb * W)
        dk_s[tail, :] += pend_k
        dv_s[tail, :] += pend_v
        db_ref[...] = db
        dsink_ref[0] = jnp.broadcast_to(jnp.sum(dsk[:, 0:W], axis=1, keepdims=True), (1, LANES))
        dsink_ref[1] = jnp.broadcast_to(jnp.sum(dsk[:, W:2 * W], axis=1, keepdims=True), (1, LANES))
        dq_raw, dgq = _pair_norm_bwd(q_ref[...], gq_ref[...], dq_s[...] * 0.125, is_a)
        dq_ref[...] = dq_raw.astype(BF16)
        _accumulate(dgq_ref, dgq, jnp.logical_and(a == 0, g == 0))

        @pl.when(jnp.logical_and(a == KVH - 1, g == NP - 1))
        def _():
            _fold_row(dgq_ref)

        @pl.when(g == NP - 1)
        def _():
            dk_raw, dgk = _pair_norm_bwd(k_ref[...], gk_ref[...], _fold_heads(dk_s[...]), is_a)
            dk_ref[...] = dk_raw
            _accumulate(dgk_ref, dgk, a == 0)
            dv_ref[...] = _fold_heads(dv_s[...])

    qcols = pl.BlockSpec((S, LANES), lambda a, g: (0, a * NP + g))
    kvs = pl.BlockSpec((None, S, LANES), lambda a, g: (a, 0, 0))
    sq = pl.BlockSpec((None, 2 * W, 2 * W), lambda a, g: (a * NP + g, 0, 0))
    gain = pl.BlockSpec((1, LANES), lambda a, g: (0, 0))
    ks = jax.ShapeDtypeStruct((KVH, S, LANES), F32)
    gs = jax.ShapeDtypeStruct((1, LANES), F32)
    return _pcall(body, name=name, grid=(KVH, NP),
                  in_specs=[qcols, kvs, kvs, sq, pl.BlockSpec((None, 1, 2 * W), lambda a, g: (a * NP + g, 0, 0)), gain, gain,
                            pl.BlockSpec((None, nb, 1, 2 * W), lambda a, g: (a * NP + g, 0, 0, 0)), qcols],
                  out_specs=[qcols, kvs, kvs, sq, pl.BlockSpec((2, 1, LANES), lambda a, g: (a * NP + g, 0, 0)), gain, gain],
                  out_shape=[jax.ShapeDtypeStruct((S, HQD), BF16), ks, ks, jax.ShapeDtypeStruct((PP, 2 * W, 2 * W), F32),
                             jax.ShapeDtypeStruct((2 * PP, 1, LANES), F32), gs, gs],
                  scratch_shapes=[pltpu.VMEM((2, S, LANES), BF16), pltpu.VMEM((S, LANES), BF16), pltpu.VMEM((LANES, S), BF16),
                                  pltpu.VMEM((S, LANES), BF16), pltpu.VMEM((2, S, LANES), BF16)] + [pltpu.VMEM((S, LANES), F32)] * 3,
                  compiler_params=_params(("arbitrary", "arbitrary")))(qproj, kk, vv, bias_t_ab, sink_row, gq2, gk2, lse_row, do)


def _adamw(w, g, m, v, name, tr=256):
    R, C = w.shape
    tr = min(tr, R)
    assert R % tr == 0

    def body(w_ref, g_ref, m_ref, v_ref, d_ref, m2_ref, v2_ref):
        gv = g_ref[...]
        m2 = ADAM_B1 * m_ref[...] + (1.0 - ADAM_B1) * gv
        v2 = ADAM_B2 * v_ref[...] + (1.0 - ADAM_B2) * jnp.square(gv)
        m_hat = m2 / (1.0 - ADAM_B1 ** ADAM_STEP)
        v_hat = v2 / (1.0 - ADAM_B2 ** ADAM_STEP)
        d_ref[...] = -ADAM_LR * (m_hat / (jnp.sqrt(v_hat) + ADAM_EPS) + ADAM_WD * w_ref[...])
        m2_ref[...] = m2
        v2_ref[...] = v2

    blk = pl.BlockSpec((tr, C), lambda i: (i, 0))
    return _pcall(body, name=name, grid=(R // tr,), in_specs=[blk] * 4, out_specs=[blk] * 3,
                  out_shape=[jax.ShapeDtypeStruct((R, C), F32)] * 3, compiler_params=_params(("parallel",)))(w, g, m, v)


def _sum_list(arrays, name, out_dtype, tr=128):
    R, C = arrays[0].shape
    tr = min(tr, R)
    assert R % tr == 0, (R, tr)
    n = len(arrays)

    def body(*refs):
        acc = refs[0][...].astype(F32)
        for r in refs[1:n]:
            acc = acc + r[...].astype(F32)
        refs[n][...] = acc.astype(refs[n].dtype)

    blk = pl.BlockSpec((tr, C), lambda i: (i, 0))
    return _pcall(body, name=name, grid=(R // tr,), in_specs=[blk] * n, out_specs=blk,
                  out_shape=jax.ShapeDtypeStruct((R, C), out_dtype), compiler_params=_params(("parallel",)))(*arrays)


def _sum_parts(parts, name, out_dtype, tr=128):
    P, R, C = parts.shape
    tr = min(tr, R)
    assert R % tr == 0, (R, tr)

    def body(p_ref, o_ref):
        acc = p_ref[0].astype(F32)
        for k in range(1, P):
            acc = acc + p_ref[k].astype(F32)
        o_ref[...] = acc.astype(o_ref.dtype)

    return _pcall(body, name=name, grid=(R // tr,), in_specs=[pl.BlockSpec((P, tr, C), lambda i: (0, i, 0))],
                  out_specs=pl.BlockSpec((tr, C), lambda i: (i, 0)),
                  out_shape=jax.ShapeDtypeStruct((R, C), out_dtype), compiler_params=_params(("parallel",)))(parts)


def _place():
    x, y, c = lax.axis_index("x"), lax.axis_index("y"), lax.axis_index("c")
    others = [(1 - x, y), (x, 1 - y), (1 - x, 1 - y)]
    return x, y, c, others


def _half_rows(ref, hh, lead=()):
    hr = ref.shape[-2] // 2
    return ref.at[(*lead, pl.ds(pl.multiple_of(hh * hr, 16), hr), slice(None))]


def _sem_arrays(*counts):
    return [pltpu.SemaphoreType.DMA((k,)) for k in counts]


def _allgather_group(shards, name):
    n = len(shards)

    def body(*refs):
        ins, outs = refs[:n], refs[n:2 * n]
        send_sems, recv_sems, local_sems = refs[2 * n:]
        x, y, c, others = _place()
        me = 2 * x + y
        sibling = (x, y, 1 - c)

        def copy(w, k, src, dst, to):
            return pltpu.make_async_remote_copy(src_ref=src, dst_ref=dst, send_sem=send_sems.at[6 * w + k],
                                                recv_sem=recv_sems.at[6 * w + k], device_id=to, device_id_type=MESH)

        local = [pltpu.make_async_copy(ins[w], outs[w].at[me], local_sems.at[w]) for w in range(n)]
        first = [copy(w, k, _half_rows(ins[w], c), _half_rows(outs[w], c, (me,)), (cx, cy, c))
                 for w in range(n) for k, (cx, cy) in enumerate(others)]
        for cp in local + first:
            cp.start()
        passed = []
        for w in range(n):
            for k, (cx, cy) in enumerate(others):
                landed = _half_rows(outs[w], c, (2 * cx + cy,))
                copy(w, k, landed, landed, sibling).wait_recv()
                passed.append(copy(w, 3 + k, landed, landed, sibling))
                passed[-1].start()
        for w in range(n):
            for k, (cx, cy) in enumerate(others):
                theirs = _half_rows(outs[w], 1 - c, (2 * cx + cy,))
                copy(w, 3 + k, theirs, theirs, sibling).wait_recv()
        for cp in first + passed:
            cp.wait_send()
        for cp in local:
            cp.wait()

    return _pcall(body, name=name, in_specs=[HBM_SPEC] * n, out_specs=[HBM_SPEC] * n,
                  out_shape=[jax.ShapeDtypeStruct((N_CHIPS,) + s.shape, s.dtype) for s in shards],
                  scratch_shapes=_sem_arrays(6 * n, 6 * n, n))(*shards)


def _swap_halves_group(arrs, name):
    n = len(arrs)

    def body(*refs):
        ins, keeps, gots = refs[:n], refs[n:2 * n], refs[2 * n:3 * n]
        send_sems, recv_sems, local_sems = refs[3 * n:]
        x, y, c, _ = _place()
        local = [pltpu.make_async_copy(_half_rows(ins[w], c, (slice(None),)), keeps[w], local_sems.at[w]) for w in range(n)]
        swaps = [pltpu.make_async_remote_copy(src_ref=_half_rows(ins[w], 1 - c, (slice(None),)), dst_ref=gots[w],
                                              send_sem=send_sems.at[w], recv_sem=recv_sems.at[w],
                                              device_id=(x, y, 1 - c), device_id_type=MESH) for w in range(n)]
        for cp in local + swaps:
            cp.start()
        for cp in swaps + local:
            cp.wait()

    half_shapes = [jax.ShapeDtypeStruct((a.shape[0], a.shape[1] // 2, a.shape[2]), a.dtype) for a in arrs]
    outs = _pcall(body, name=name, in_specs=[HBM_SPEC] * n, out_specs=[HBM_SPEC] * (2 * n), out_shape=half_shapes * 2,
                  scratch_shapes=_sem_arrays(n, n, n))(*arrs)
    return outs[:n], outs[n:]


def _scatter_group(parts, name):
    n = len(parts)

    def body(*refs):
        ins, outs = refs[:n], refs[n:2 * n]
        send_sems, recv_sems, local_sems = refs[2 * n:]
        x, y, c, others = _place()
        me = 2 * x + y

        def copy(w, k, src_chip, dst_chip, to):
            return pltpu.make_async_remote_copy(src_ref=ins[w].at[src_chip], dst_ref=outs[w].at[dst_chip],
                                                send_sem=send_sems.at[3 * w + k], recv_sem=recv_sems.at[3 * w + k],
                                                device_id=to, device_id_type=MESH)

        local = [pltpu.make_async_copy(ins[w].at[me], outs[w].at[me], local_sems.at[w]) for w in range(n)]
        sends = [copy(w, k, 2 * cx + cy, me, (cx, cy, c)) for w in range(n) for k, (cx, cy) in enumerate(others)]
        for cp in local + sends:
            cp.start()
        for w in range(n):
            for k, (cx, cy) in enumerate(others):
                copy(w, k, me, 2 * cx + cy, (cx, cy, c)).wait_recv()
        for cp in sends:
            cp.wait_send()
        for cp in local:
            cp.wait()

    return _pcall(body, name=name, in_specs=[HBM_SPEC] * n, out_specs=[HBM_SPEC] * n,
                  out_shape=[jax.ShapeDtypeStruct(p.shape, p.dtype) for p in parts],
                  scratch_shapes=_sem_arrays(3 * n, 3 * n, n))(*parts)


def _share_halves_group(halves, name):
    n = len(halves)

    def body(*refs):
        ins, outs = refs[:n], refs[n:2 * n]
        send_sems, recv_sems, local_sems = refs[2 * n:]
        x, y, c, _ = _place()
        local = [pltpu.make_async_copy(ins[w], outs[w].at[c], local_sems.at[w]) for w in range(n)]
        swaps = [pltpu.make_async_remote_copy(src_ref=ins[w], dst_ref=outs[w].at[c], send_sem=send_sems.at[w],
                                              recv_sem=recv_sems.at[w], device_id=(x, y, 1 - c), device_id_type=MESH)
                 for w in range(n)]
        for cp in local + swaps:
            cp.start()
        for w in range(n):
            pltpu.make_async_remote_copy(src_ref=ins[w], dst_ref=outs[w].at[1 - c], send_sem=send_sems.at[w],
                                         recv_sem=recv_sems.at[w], device_id=(x, y, 1 - c), device_id_type=MESH).wait()
        for cp in local:
            cp.wait()

    return _pcall(body, name=name, in_specs=[HBM_SPEC] * n, out_specs=[HBM_SPEC] * n,
                  out_shape=[jax.ShapeDtypeStruct((2,) + h.shape, h.dtype) for h in halves],
                  scratch_shapes=_sem_arrays(n, n, n))(*halves)


def _allgather_small(blk, name):
    M, C = blk.shape

    def body(x_ref, out_ref, send_sems, recv_sems, local_sem):
        x, y, c, others = _place()
        me, sibling = (x, y, c), (x, y, 1 - c)

        def rows(px, py, pc):
            return out_ref.at[4 * px + 2 * py + pc]

        def copy(k, block, to, src=None):
            return pltpu.make_async_remote_copy(src_ref=rows(*block) if src is None else src, dst_ref=rows(*block),
                                                send_sem=send_sems.at[k], recv_sem=recv_sems.at[k], device_id=to, device_id_type=MESH)

        mine = pltpu.make_async_copy(x_ref, rows(*me), local_sem)
        mine.start()
        first = [copy(0, me, sibling, src=x_ref)]
        first += [copy(1 + j, me, (*chip, c), src=x_ref) for j, chip in enumerate(others)]
        for cp in first:
            cp.start()
        passed = [copy(4 + j, (*chip, c), sibling) for j, chip in enumerate(others)]
        for j, chip in enumerate(others):
            copy(1 + j, (*chip, c), me).wait_recv()
            passed[j].start()
        copy(0, sibling, me).wait_recv()
        for j, chip in enumerate(others):
            copy(4 + j, (*chip, 1 - c), me).wait_recv()
        for cp in first + passed:
            cp.wait_send()
        mine.wait()

    return _pcall(body, name=name, in_specs=[VMEM_SPEC], out_specs=VMEM_SPEC,
                  out_shape=jax.ShapeDtypeStruct((8, M, C), blk.dtype),
                  scratch_shapes=[pltpu.SemaphoreType.DMA((7,)), pltpu.SemaphoreType.DMA((7,)), pltpu.SemaphoreType.DMA])(blk)


def _pack_rows(n_elems, width=PACK_W, align=PACK_ROW_ALIGN):
    rows = -(-n_elems // width)
    return -(-rows // align) * align


def _pack(arrays, dtype, width=PACK_W, align=PACK_ROW_ALIGN):
    flat = jnp.concatenate([a.astype(dtype).reshape(-1) for a in arrays])
    rows = _pack_rows(flat.shape[0], width, align)
    flat = jnp.pad(flat, (0, rows * width - flat.shape[0]))
    return flat.reshape(rows, width)


def _pack_small(arrays):
    return _pack(arrays, F32, width=128, align=8)


def _unpack(flat, shapes):
    out, off = [], 0
    for shp in shapes:
        n = int(np.prod(shp))
        out.append(flat[..., off:off + n].reshape(flat.shape[:-1] + tuple(shp)))
        off += n
    return out


def _doubled_heads(x2d, n_heads):
    S = x2d.shape[0]
    h = x2d.reshape(S, n_heads, HEAD_DIM).transpose(1, 0, 2)
    return jnp.concatenate([h, h], axis=-1)


def _mlp_fwd(h, g, w_up4, w_down, tag):
    (n,) = _rms_fwd(h, [g], f"rms_mlp{tag}")
    u, a = _matmul(n, w_up4, "nn", f"up{tag}", out_dtypes=(F32, BF16), chipwise="b",
                   epilogue=lambda acc: (acc, jnp.square(jnp.maximum(acc, 0.0))))
    h_out = _matmul(a, w_down, "nn", f"down{tag}", extras=(h,), epilogue=lambda acc, res: (res + acc,), tk=1024)
    return h_out, (n, u, a)


def _mlp_bwd(dh_out, h, g, w_up4, w_down, saved, tag):
    n, u, a = saved
    dw_down = _matmul(a, dh_out, "tn", f"dw_down{tag}", out_dtypes=(BF16,))
    du = _matmul(dh_out, w_down, "nt", f"du{tag}", out_dtypes=(BF16,), extras=(u,),
                 epilogue=lambda acc, uu: (acc * (2.0 * jnp.maximum(uu, 0.0)),))
    dw_up = _matmul(n, du, "tn", f"dw_up{tag}", out_dtypes=(BF16,), chipwise="out")
    dn = _matmul(du, w_up4, "nt", f"dn_mlp{tag}", tk=w_up4.shape[2], chipwise="b")
    dh, (dg,) = _rms_bwd(h, dh_out, [g], [dn], f"rms_mlp_bwd{tag}")
    return dh, dg, dw_up, dw_down


def kernel(x, g_attn, g_mlp, w_in_a, b_f, gq_a, gk_a, w_out_a, g_kv, w_kv, gk_b, w_q_b, gq_b, sinks, rel_bias, w_out_b, w_up, w_down, loss_target, m_g_attn, m_g_mlp, m_w_in_a, m_b_f, m_gq_a, m_gk_a, m_w_out_a, m_g_kv, m_w_kv, m_gk_b, m_w_q_b, m_gq_b, m_sinks, m_rel_bias, m_w_out_b, m_w_up, m_w_down, v_g_attn, v_g_mlp, v_w_in_a, v_b_f, v_gq_a, v_gk_a, v_w_out_a, v_g_kv, v_w_kv, v_gk_b, v_w_q_b, v_gq_b, v_sinks, v_rel_bias, v_w_out_b, v_w_up, v_w_down):
    given = dict(locals())
    S, D = x.shape[1], x.shape[2]
    H = D // HEAD_DIM
    KVH = w_kv.shape[1] // (2 * HEAD_DIM)
    kvw = KVH * HEAD_DIM
    hw = H * HEAD_DIM
    W = WINDOW
    nb = S // W
    c_idx = lax.axis_index("c")
    xs, tgt = x[0], loss_target[0]

    shards = {"w_in_a": w_in_a[0], "w_out_a": w_out_a[0], "w_up0": w_up[0], "w_down0": w_down[0], "w_kv": w_kv,
              "w_q_b": w_q_b[0], "w_out_b": w_out_b[0], "w_up1": w_up[1], "w_down1": w_down[1]}
    parts = list(shards)
    lane_pad = lambda a: jnp.pad(a, [(0, 0)] * (a.ndim - 1) + [(0, (-a.shape[-1]) % LANES)])
    n_in_shard = w_in_a.shape[2]
    gathered = dict(zip(parts, _allgather_group([lane_pad(shards[n].astype(BF16)) for n in parts], "allgather_weights")))
    win = jnp.moveaxis(gathered["w_in_a"][:, :, :n_in_shard], 0, 1).reshape(D, -1)
    win = jnp.pad(win, ((0, 0), (0, (-win.shape[1]) % 128)))
    wout_a, wq_b, wout_b = (gathered[n].reshape(-1, D) for n in ("w_out_a", "w_q_b", "w_out_b"))
    wkv = gathered["w_kv"].reshape(D, -1)
    wup = [gathered["w_up0"], gathered["w_up1"]]
    wdown = [gathered["w_down0"].reshape(-1, D), gathered["w_down1"].reshape(-1, D)]
    n_in = win.shape[1]
    tile_in = 640 if n_in % 640 == 0 else 128

    vec = lambda a: a.reshape(1, -1)
    twice = lambda a: jnp.tile(a.reshape(1, -1), (1, 2))

    (n0,) = _rms_fwd(xs, [vec(g_attn[0])], "rms_attn0")
    proj = _matmul(n0, win, "nn", "proj_in", tn=tile_in)
    zt = proj[:, 3 * hw:3 * hw + H].T
    c_row = _gate_fwd(zt, b_f.reshape(H, 1), "gate_fwd")
    c_col3, c_row3 = c_row.reshape(H, S, 1), c_row.reshape(H, 1, S)
    o_a, lse_a = _fox_fwd(proj, c_col3, c_row3, twice(gq_a[0]), twice(gk_a[0]), H, "fox_fwd")
    h1 = _matmul(o_a, wout_a, "nn", "out_a", extras=(xs,), epilogue=lambda acc, res: (res + acc,))
    h2, mlp0 = _mlp_fwd(h1, vec(g_mlp[0]), wup[0], wdown[0], "0")

    nkv, n2 = _rms_fwd(h2, [vec(g_kv), vec(g_attn[1])], "rms_attn1")
    kv = _matmul(nkv, wkv, "nn", "proj_kv")
    kk, vv = _doubled_heads(kv[:, :kvw], KVH), _doubled_heads(kv[:, kvw:], KVH)
    q2 = _matmul(n2, wq_b, "nn", "proj_q")
    onehot = jnp.asarray(_bucket_onehot(), dtype=BF16)
    bias = _bias_expand(rel_bias.T, onehot, "bias_expand").reshape(H, W, 2 * W)
    bias_ab = bias.reshape(H // 2, 2 * W, 2 * W)
    bias_t_ab = bias.reshape(H // 2, 2, W, 2 * W).transpose(0, 3, 1, 2).reshape(H // 2, 2 * W, 2 * W)
    sink_ab = jnp.repeat(sinks[0].reshape(H // 2, 2), W, axis=1)
    o_b, lse_b = _swa_fwd(q2, kk, vv, bias_ab, sink_ab.reshape(H // 2, 2 * W, 1), twice(gq_b[0]), twice(gk_b), "swa_fwd")
    h3 = _matmul(o_b, wout_b, "nn", "out_b", extras=(h2,), epilogue=lambda acc, res: (res + acc,))
    h4, mlp1 = _mlp_fwd(h3, vec(g_mlp[1]), wup[1], wdown[1], "1")

    dh4, loss_part = _loss_head(h4, tgt, "loss_head")
    loss = lax.psum(loss_part[0, 0], ("x", "y", "c"))

    dh3, dg_mlp1, dw_up1, dw_down1 = _mlp_bwd(dh4, h3, vec(g_mlp[1]), wup[1], wdown[1], mlp1, "1")
    dw_out_b = _matmul(o_b, dh3, "tn", "dw_out_b", out_dtypes=(BF16,))
    do_b = _matmul(dh3, wout_b, "nt", "do_b")
    dq2, dk2, dv2, dbias_t_ab, dsink, dgq_b, dgk_b = _swa_bwd(
        q2, kk, vv, bias_t_ab, sink_ab.reshape(H // 2, 1, 2 * W), twice(gq_b[0]), twice(gk_b),
        lse_b.reshape(H // 2, nb, 1, 2 * W), do_b, "swa_bwd")
    dbias = dbias_t_ab.reshape(H // 2, 2 * W, 2, W).transpose(0, 2, 3, 1).reshape(H, W * 2 * W)
    d_rel_bias = _bias_reduce(dbias, onehot, "bias_reduce").T
    dw_q_b = _matmul(n2, dq2, "tn", "dw_q_b", out_dtypes=(BF16,))
    dn2 = _matmul(dq2, wq_b, "nt", "dn2")
    dkv = jnp.concatenate([dk2[h, :, :HEAD_DIM] for h in range(KVH)] + [dv2[h, :, :HEAD_DIM] for h in range(KVH)],
                          axis=1).astype(BF16)
    dw_kv = _matmul(nkv, dkv, "tn", "dw_kv", out_dtypes=(BF16,))
    dnkv = _matmul(dkv, wkv, "nt", "dnkv")
    dh2, (dg_kv, dg_attn1) = _rms_bwd(h2, dh3, [vec(g_kv), vec(g_attn[1])], [dnkv, dn2], "rms_attn1_bwd")

    dh1, dg_mlp0, dw_up0, dw_down0 = _mlp_bwd(dh2, h1, vec(g_mlp[0]), wup[0], wdown[0], mlp0, "0")
    dw_out_a = _matmul(o_a, dh1, "tn", "dw_out_a", out_dtypes=(BF16,))
    do_a = _matmul(dh1, wout_a, "nt", "do_a")
    dq_a, dk_a, dv_a, dc_col, dgq_a, dgk_a = _fox_bwd(
        proj, c_col3, c_row3, twice(gq_a[0]), twice(gk_a[0]), lse_a.reshape(H, 1, S), do_a, H, "fox_bwd")
    dzt, db_f = _gate_bwd(dc_col.reshape(H, S), zt, b_f.reshape(H, 1), "gate_bwd")
    dproj = jnp.concatenate([dq_a, dk_a, dv_a, dzt.T.astype(BF16), jnp.zeros((S, n_in - 3 * hw - H), BF16)], axis=1)
    dw_in = _matmul(n0, dproj, "tn", "dw_in", out_dtypes=(BF16,), tn=tile_in)
    dn0 = _matmul(dproj, win, "nt", "dn0", tk=tile_in)
    grad_x, (dg_attn0,) = _rms_bwd(xs, dh1, [vec(g_attn[0])], [dn0], "rms_attn0_bwd")

    dw_in4 = lane_pad(dw_in[:, :3 * hw + H].reshape(D, N_CHIPS, -1).transpose(1, 0, 2))
    chipwise = {"w_in_a": dw_in4, "w_out_a": dw_out_a.reshape(N_CHIPS, -1, D), "w_up0": dw_up0,
                "w_down0": dw_down0.reshape(N_CHIPS, -1, D), "w_kv": dw_kv.reshape(N_CHIPS, -1, 2 * kvw),
                "w_q_b": dw_q_b.reshape(N_CHIPS, -1, D), "w_out_b": dw_out_b.reshape(N_CHIPS, -1, D), "w_up1": dw_up1,
                "w_down1": dw_down1.reshape(N_CHIPS, -1, D)}
    keep, got = _swap_halves_group([chipwise[n] for n in parts], "swap_grad_halves")
    pair_sums = [_sum_list([k.reshape(-1, k.shape[2]), g.reshape(-1, g.shape[2])], "sum_core_pair_" + n, BF16).reshape(k.shape)
                 for n, k, g in zip(parts, keep, got)]
    landed = _scatter_group(pair_sums, "scatter_grads")
    halves = [_sum_parts(l, "sum_chips_" + n, F32) for n, l in zip(parts, landed)]
    reduced = {n: r.reshape(-1, r.shape[2])[:, :shards[n].shape[1]]
               for n, r in zip(parts, _share_halves_group(halves, "share_reduced_halves"))}
    big_grads = [reduced["w_in_a"][None], reduced["w_out_a"][None], reduced["w_kv"], reduced["w_q_b"][None],
                 reduced["w_out_b"][None], jnp.stack([reduced["w_up0"], reduced["w_up1"]]),
                 jnp.stack([reduced["w_down0"], reduced["w_down1"]])]

    small_grads = {
        "g_attn": jnp.concatenate([dg_attn0, dg_attn1], axis=0), "g_mlp": jnp.concatenate([dg_mlp0, dg_mlp1], axis=0),
        "b_f": db_f.reshape(1, H), "gq_a": dgq_a[:, :HEAD_DIM], "gk_a": dgk_a[:, :HEAD_DIM], "g_kv": dg_kv.reshape(-1),
        "gk_b": dgk_b[0, :HEAD_DIM], "gq_b": dgq_b[:, :HEAD_DIM], "sinks": dsink[:, 0, 0].reshape(1, H), "rel_bias": d_rel_bias,
    }
    small_shapes = [given[n].shape for n in SMALL]
    spack = _pack_small([small_grads[n] for n in SMALL])
    small_sum = _sum_parts(_allgather_small(spack, "allgather_small"), "sum_small", F32, tr=spack.shape[0])
    small_red = _unpack(small_sum.reshape(-1), small_shapes)

    grads = dict(zip([n for n, _ in BIG], big_grads))
    grads.update(dict(zip(SMALL, small_red)))
    sw = _pack_small([given[n] for n in SMALL])
    sm = _pack_small([given["m_" + n] for n in SMALL])
    sv = _pack_small([given["v_" + n] for n in SMALL])
    sd, sm2, sv2 = _adamw(sw, small_sum, sm, sv, "adamw_small", tr=sw.shape[0])
    delta = dict(zip(SMALL, _unpack(sd.reshape(-1), small_shapes)))
    new_m = dict(zip(SMALL, _unpack(sm2.reshape(-1), small_shapes)))
    new_v = dict(zip(SMALL, _unpack(sv2.reshape(-1), small_shapes)))
    for n, _ in BIG:
        w = given[n]
        two_d = (-1, w.shape[-1])
        d, m2, v2 = _adamw(w.reshape(two_d), grads[n].reshape(two_d), given["m_" + n].reshape(two_d),
                           given["v_" + n].reshape(two_d), "adamw_" + n)
        delta[n], new_m[n], new_v[n] = d.reshape(w.shape), m2.reshape(w.shape), v2.reshape(w.shape)

    order = ["g_attn", "g_mlp", "w_in_a", "b_f", "gq_a", "gk_a", "w_out_a", "g_kv", "w_kv", "gk_b", "w_q_b", "gq_b",
             "sinks", "rel_bias", "w_out_b", "w_up", "w_down"]
    return (loss, grad_x[None], *[grads[n] for n in order], *[delta[n] for n in order],
            *[new_m[n] for n in order], *[new_v[n] for n in order])
```

```python
import numpy as np
import jax
import jax.numpy as jnp
from jax import lax
from jax.experimental import pallas as pl
from jax.experimental.pallas import tpu as pltpu

F32 = jnp.float32
BF16 = jnp.bfloat16
MESH = pl.DeviceIdType.MESH

HEAD_DIM = 64
LANES = 128
WINDOW = 128
N_BUCKETS = 32
REL_MAX_DIST = 128
NORM_EPS = 1e-6
ADAM_LR = 0.001
ADAM_B1 = 0.9
ADAM_B2 = 0.999
ADAM_EPS = 1e-08
ADAM_WD = 0.01
ADAM_STEP = 10
NEG = -1e30
N_CHIPS = 4
PACK_W = 1024
PACK_ROW_ALIGN = 256
VMEM_LIMIT = 56 * 1024 * 1024
HBM_SPEC = pl.BlockSpec(memory_space=pltpu.HBM)
VMEM_SPEC = pl.BlockSpec(memory_space=pltpu.VMEM)

BIG = (("w_in_a", 2), ("w_out_a", 1), ("w_kv", 0), ("w_q_b", 1), ("w_out_b", 1), ("w_up", 2), ("w_down", 1))
SMALL = ("g_attn", "g_mlp", "b_f", "gq_a", "gk_a", "g_kv", "gk_b", "gq_b", "sinks", "rel_bias")


def _pcall(body, **kw):
    return pl.pallas_call(body, **kw)


def _params(sem=None):
    return pltpu.CompilerParams(dimension_semantics=sem, vmem_limit_bytes=VMEM_LIMIT)


def _rinv(x):
    return lax.rsqrt(jnp.mean(x * x, axis=-1, keepdims=True) + NORM_EPS)


def _dot(a, b, dims, precision=None):
    return lax.dot_general(a, b, (dims, ((), ())), precision=precision, preferred_element_type=F32)


NN = ((1,), (0,))
NT = ((1,), (1,))
TN = ((0,), (0,))


def _accumulate(ref, val, first):
    @pl.when(first)
    def _():
        ref[...] = val

    @pl.when(jnp.logical_not(first))
    def _():
        ref[...] += val


def _matmul(a, b, mode, name, out_dtypes=(F32,), extras=(), epilogue=None, tm=512, tn=512, tk=None, chipwise=None):
    if chipwise == "b":
        nc = b.shape[2]
        M, K = a.shape
        (K2, N) = (b.shape[1], N_CHIPS * nc) if mode == "nn" else (N_CHIPS * nc, b.shape[1])
    elif mode == "nn":
        (M, K), (K2, N) = a.shape, b.shape
    elif mode == "nt":
        (M, K), (N, K2) = a.shape, b.shape
    else:
        (K, M), (K2, N) = a.shape, b.shape
    assert K == K2, (a.shape, b.shape, mode)
    tm, tn = min(tm, M), min(tn, N)
    tk = K if tk is None else tk
    assert M % tm == 0 and N % tn == 0 and K % tk == 0, (M, N, K, tm, tn, tk)
    nk = K // tk
    dims = {"nn": NN, "nt": NT, "tn": TN}[mode]
    a_spec = pl.BlockSpec((tk, tm), lambda i, j, k: (k, i)) if mode == "tn" else pl.BlockSpec((tm, tk), lambda i, j, k: (i, k))
    b_spec = pl.BlockSpec((tn, tk), lambda i, j, k: (j, k)) if mode == "nt" else pl.BlockSpec((tk, tn), lambda i, j, k: (k, j))
    o_spec = pl.BlockSpec((tm, tn), lambda i, j, k: (i, j))
    out_shape = (M, N)
    if chipwise == "b" and mode == "nn":
        per = nc // tn
        assert tk == K and nc % tn == 0
        b_spec = pl.BlockSpec((None, tk, tn), lambda i, j, k: (j // per, 0, j % per))
    elif chipwise == "b":
        assert mode == "nt" and tk == nc
        b_spec = pl.BlockSpec((None, tn, tk), lambda i, j, k: (k, j, 0))
    elif chipwise == "out":
        per = (N // N_CHIPS) // tn
        assert (N // N_CHIPS) % tn == 0
        o_spec = pl.BlockSpec((None, tm, tn), lambda i, j, k: (j // per, i, j % per))
        out_shape = (N_CHIPS, M, N // N_CHIPS)
        assert not extras
    n_ex, n_out = len(extras), len(out_dtypes)

    def body(*refs):
        a_ref, b_ref = refs[0], refs[1]
        ex_refs = refs[2:2 + n_ex]
        out_refs = refs[2 + n_ex:2 + n_ex + n_out]
        acc_ref = refs[2 + n_ex + n_out]
        k = pl.program_id(2)
        part = _dot(a_ref[...].astype(BF16), b_ref[...].astype(BF16), dims)

        @pl.when(k == 0)
        def _():
            acc_ref[...] = part

        @pl.when(k > 0)
        def _():
            acc_ref[...] += part

        @pl.when(k == nk - 1)
        def _():
            acc = acc_ref[...]
            outs = (acc,) if epilogue is None else epilogue(acc, *[r[...] for r in ex_refs])
            for r, o in zip(out_refs, outs):
                r[...] = o.astype(r.dtype)

    outs = _pcall(
        body, name=name, grid=(M // tm, N // tn, nk),
        in_specs=[a_spec, b_spec] + [o_spec] * n_ex,
        out_specs=[o_spec] * n_out,
        out_shape=[jax.ShapeDtypeStruct(out_shape, dt) for dt in out_dtypes],
        scratch_shapes=[pltpu.VMEM((tm, tn), F32)],
        compiler_params=_params(("parallel", "parallel", "arbitrary")),
    )(a, b, *extras)
    return outs[0] if n_out == 1 else outs


def _rms_fwd(x, gains, name, ts=256):
    S, D = x.shape
    ts = min(ts, S)
    n = len(gains)

    def body(*refs):
        x_ref, g_refs, o_refs = refs[0], refs[1:1 + n], refs[1 + n:]
        xv = x_ref[...]
        xh = xv * _rinv(xv)
        for g_ref, o_ref in zip(g_refs, o_refs):
            o_ref[...] = (xh * g_ref[...]).astype(BF16)

    row = pl.BlockSpec((ts, D), lambda i: (i, 0))
    vec = pl.BlockSpec((1, D), lambda i: (0, 0))
    return _pcall(body, name=name, grid=(S // ts,), in_specs=[row] + [vec] * n, out_specs=[row] * n,
                  out_shape=[jax.ShapeDtypeStruct((S, D), BF16)] * n, compiler_params=_params(("parallel",)))(x, *gains)


def _rms_bwd(x, dres, gains, dns, name, ts=256):
    S, D = x.shape
    ts = min(ts, S)
    n = len(gains)

    def body(*refs):
        x_ref, dres_ref = refs[0], refs[1]
        g_refs, dn_refs = refs[2:2 + n], refs[2 + n:2 + 2 * n]
        dx_ref, dg_refs = refs[2 + 2 * n], refs[3 + 2 * n:]
        xv = x_ref[...]
        r = _rinv(xv)
        xh = xv * r
        dx = dres_ref[...]
        first = pl.program_id(0) == 0
        for g_ref, dn_ref, dg_ref in zip(g_refs, dn_refs, dg_refs):
            dn = dn_ref[...].astype(F32)
            _accumulate(dg_ref, jnp.sum(dn * xh, axis=0, keepdims=True), first)
            dxh = dn * g_ref[...]
            dx = dx + r * (dxh - xh * jnp.mean(dxh * xh, axis=-1, keepdims=True))
        dx_ref[...] = dx

    row = pl.BlockSpec((ts, D), lambda i: (i, 0))
    vec = pl.BlockSpec((1, D), lambda i: (0, 0))
    outs = _pcall(body, name=name, grid=(S // ts,), in_specs=[row, row] + [vec] * n + [row] * n,
                  out_specs=[row] + [vec] * n,
                  out_shape=[jax.ShapeDtypeStruct((S, D), F32)] + [jax.ShapeDtypeStruct((1, D), F32)] * n,
                  compiler_params=_params(("arbitrary",)))(x, dres, *gains, *dns)
    return outs[0], outs[1:]


def _loss_head(h, tgt, name, ts=256):
    S, D = h.shape
    ts = min(ts, S)

    def body(h_ref, t_ref, dh_ref, loss_ref):
        err = h_ref[...] - t_ref[...]
        dh_ref[...] = err * (1.0 / D)
        part = 0.5 * jnp.sum(jnp.mean(err * err, axis=-1, keepdims=True), axis=0, keepdims=True)
        _accumulate(loss_ref, part, pl.program_id(0) == 0)

    row = pl.BlockSpec((ts, D), lambda i: (i, 0))
    return _pcall(body, name=name, grid=(S // ts,), in_specs=[row, row],
                  out_specs=[row, pl.BlockSpec((1, 1), lambda i: (0, 0))],
                  out_shape=[jax.ShapeDtypeStruct((S, D), F32), jax.ShapeDtypeStruct((1, 1), F32)],
                  compiler_params=_params(("arbitrary",)))(h, tgt)


def _gate_fwd(zt, bf, name):
    H, S = zt.shape
    nb = S // 128

    def body(z_ref, b_ref, c_ref):
        z = z_ref[...] + b_ref[...]
        lf = jnp.minimum(z, 0.0) - jnp.log(1.0 + jnp.exp(-jnp.abs(z)))
        upper = (lax.broadcasted_iota(jnp.int32, (128, 128), 0) <= lax.broadcasted_iota(jnp.int32, (128, 128), 1)).astype(F32)
        carry = jnp.zeros((H, 1), F32)
        for blk in range(nb):
            cs = _dot(lf[:, blk * 128:(blk + 1) * 128], upper, NN, precision=lax.Precision.HIGHEST) + carry
            c_ref[:, blk * 128:(blk + 1) * 128] = cs
            carry = cs[:, 127:128]

    return _pcall(body, name=name, in_specs=[VMEM_SPEC, VMEM_SPEC], out_specs=VMEM_SPEC,
                  out_shape=jax.ShapeDtypeStruct((H, S), F32))(zt, bf)


def _gate_bwd(dct, zt, bf, name):
    H, S = zt.shape
    nb = S // 128

    def body(dc_ref, z_ref, b_ref, dz_ref, db_ref):
        z = z_ref[...] + b_ref[...]
        e = jnp.exp(-jnp.abs(z))
        sig_neg = jnp.where(z >= 0, e, 1.0) / (1.0 + e)
        lower = (lax.broadcasted_iota(jnp.int32, (128, 128), 0) >= lax.broadcasted_iota(jnp.int32, (128, 128), 1)).astype(F32)
        dc = dc_ref[...]
        carry = jnp.zeros((H, 1), F32)
        db = jnp.zeros((H, 1), F32)
        for blk in reversed(range(nb)):
            sl = slice(blk * 128, (blk + 1) * 128)
            dlf = _dot(dc[:, sl], lower, NN, precision=lax.Precision.HIGHEST) + carry
            carry = dlf[:, 0:1]
            dz = dlf * sig_neg[:, sl]
            dz_ref[:, sl] = dz
            db = db + jnp.sum(dz, axis=1, keepdims=True)
        db_ref[...] = db

    return _pcall(body, name=name, in_specs=[VMEM_SPEC] * 3, out_specs=[VMEM_SPEC] * 2,
                  out_shape=[jax.ShapeDtypeStruct((H, S), F32), jax.ShapeDtypeStruct((H, 1), F32)])(dct, zt, bf)


def _lane_is_a():
    return lax.broadcasted_iota(jnp.int32, (1, LANES), 1) < HEAD_DIM


def _per_head_mean(x, is_a):
    sa = jnp.sum(jnp.where(is_a, x, 0.0), axis=-1, keepdims=True)
    sb = jnp.sum(jnp.where(is_a, 0.0, x), axis=-1, keepdims=True)
    return jnp.where(is_a, sa, sb) / HEAD_DIM


def _pair_norm(raw, gain, is_a):
    return raw * lax.rsqrt(_per_head_mean(raw * raw, is_a) + NORM_EPS) * gain


def _pair_norm_bwd(raw, gain, dnormed, is_a):
    r = lax.rsqrt(_per_head_mean(raw * raw, is_a) + NORM_EPS)
    xh = raw * r
    dgain = jnp.sum(dnormed * xh, axis=0, keepdims=True)
    dxh = dnormed * gain
    return r * (dxh - xh * _per_head_mean(dxh * xh, is_a)), dgain


def _fold_heads(x):
    i = lax.broadcasted_iota(jnp.int32, (LANES, LANES), 0)
    j = lax.broadcasted_iota(jnp.int32, (LANES, LANES), 1)
    fold = ((i == j) | (i == j + HEAD_DIM) | (i + HEAD_DIM == j)).astype(F32)
    return _dot(x, fold, NN, precision=lax.Precision.HIGHEST)


def _fold_row(ref):
    ref[...] = _fold_heads(jnp.broadcast_to(ref[...], (8, LANES)))[0:1, :]


def _tri_mask(t, keys_on_rows):
    r = lax.broadcasted_iota(jnp.int32, (t, t), 0)
    c = lax.broadcasted_iota(jnp.int32, (t, t), 1)
    return (r <= c) if keys_on_rows else (r >= c)


def _fox_fwd(proj, c_col, c_row, gq2, gk2, n_heads, name, t=256):
    S = proj.shape[0]
    H = n_heads
    P = H // 2
    t = min(t, S)
    nq = S // t

    def body(q_ref, k_ref, v_ref, cc_ref, cr_ref, gq_ref, gk_ref, o_ref, lse_ref, qs_s, kb_s, vb_s):
        is_a = _lane_is_a()
        qn = _pair_norm(q_ref[...], gq_ref[...], is_a) * 0.125
        qs_s[0] = jnp.where(is_a, qn, 0.0).astype(BF16)
        qs_s[1] = jnp.where(is_a, 0.0, qn).astype(BF16)
        kb_s[...] = _pair_norm(k_ref[...], gk_ref[...], is_a).astype(BF16)
        vb_s[...] = v_ref[...].astype(BF16)
        causal = _tri_mask(t, False)
        for i in range(nq):
            t0 = i * t
            rows = slice(t0, t0 + t)
            o_pair = None
            for a in range(2):
                qi = qs_s[a, rows, :]
                ci = cc_ref[a, rows, :]
                s_d = jnp.where(causal, _dot(qi, kb_s[rows, :], NT) + ci - cr_ref[a, :, rows], NEG)
                m = jnp.max(s_d, axis=-1, keepdims=True)
                if i > 0:
                    s_l = _dot(qi, kb_s[0:t0, :], NT) + ci - cr_ref[a, :, 0:t0]
                    m = jnp.maximum(m, jnp.max(s_l, axis=-1, keepdims=True))
                p_d = jnp.exp(s_d - m)
                l = jnp.sum(p_d, axis=-1, keepdims=True)
                acc = _dot(p_d.astype(BF16), vb_s[rows, :], NN)
                if i > 0:
                    p_l = jnp.exp(s_l - m)
                    l = l + jnp.sum(p_l, axis=-1, keepdims=True)
                    acc = acc + _dot(p_l.astype(BF16), vb_s[0:t0, :], NN)
                o_a = acc / l
                lse_ref[a, rows, :] = m + jnp.log(l)
                o_pair = o_a if a == 0 else jnp.where(is_a, o_pair, o_a)
            o_ref[rows, :] = o_pair.astype(BF16)

    def cols(off):
        return pl.BlockSpec((S, LANES), lambda p: (0, off + p))

    col = pl.BlockSpec((2, S, 1), lambda p: (p, 0, 0))
    rowv = pl.BlockSpec((2, 1, S), lambda p: (p, 0, 0))
    gain = pl.BlockSpec((1, LANES), lambda p: (0, 0))
    return _pcall(body, name=name, grid=(P,), in_specs=[cols(0), cols(P), cols(2 * P), col, rowv, gain, gain],
                  out_specs=[cols(0), col],
                  out_shape=[jax.ShapeDtypeStruct((S, H * HEAD_DIM), BF16), jax.ShapeDtypeStruct((H, S, 1), F32)],
                  scratch_shapes=[pltpu.VMEM((2, S, LANES), BF16), pltpu.VMEM((S, LANES), BF16), pltpu.VMEM((S, LANES), BF16)],
                  compiler_params=_params(("parallel",)))(proj, proj, proj, c_col, c_row, gq2, gk2)


def _fox_bwd(proj, c_col, c_row, gq2, gk2, lse_row, do, n_heads, name, t=256):
    S = proj.shape[0]
    H = n_heads
    P = H // 2
    t = min(t, S)
    nq = S // t
    assert t % LANES == 0

    def body(q_ref, k_ref, v_ref, cc_ref, cr_ref, gq_ref, gk_ref, lr_ref, do_ref,
             dq_ref, dk_ref, dv_ref, dc_ref, dgq_ref, dgk_ref,
             qs_s, kb_s, kt_s, vb_s, dob_s, dq_s, dk_s, dv_s, dcs_s):
        is_a = _lane_is_a()
        qn = _pair_norm(q_ref[...], gq_ref[...], is_a) * 0.125
        qs_s[0] = jnp.where(is_a, qn, 0.0).astype(BF16)
        qs_s[1] = jnp.where(is_a, 0.0, qn).astype(BF16)
        kn = _pair_norm(k_ref[...], gk_ref[...], is_a)
        kb_s[...] = kn.astype(BF16)
        kt_s[0] = jnp.where(is_a, kn, 0.0).T.astype(BF16)
        kt_s[1] = jnp.where(is_a, 0.0, kn).T.astype(BF16)
        vb_s[...] = v_ref[...].astype(BF16)
        dov = do_ref[...]
        dob_s[0] = jnp.where(is_a, dov, 0.0).astype(BF16)
        dob_s[1] = jnp.where(is_a, 0.0, dov).astype(BF16)
        dk_s[...] = jnp.zeros((S, LANES), F32)
        dv_s[...] = jnp.zeros((S, LANES), F32)
        dcs_s[...] = jnp.zeros((2, S, LANES), F32)
        causal = _tri_mask(t, True)
        for i in range(nq):
            t0 = i * t
            rows = slice(t0, t0 + t)
            dq_t = jnp.zeros((LANES, t), F32)
            for a in range(2):
                qi = qs_s[a, rows, :]
                doi = dob_s[a, rows, :]
                cri = cr_ref[a, :, rows]
                lri = lr_ref[a, :, rows]

                def probs(keys, masked, a=a, qi=qi, doi=doi, cri=cri, lri=lri):
                    p_t = jnp.exp(_dot(kb_s[keys, :], qi, NT) + cri - cc_ref[a, keys, :] - lri)
                    if masked:
                        p_t = jnp.where(causal, p_t, 0.0)
                    return p_t, _dot(vb_s[keys, :], doi, NT)

                parts = [(rows,) + probs(rows, True)]
                if i > 0:
                    parts.append((slice(0, t0),) + probs(slice(0, t0), False))
                delta = sum(jnp.sum(p_t * dp_t, axis=0, keepdims=True) for _, p_t, dp_t in parts)
                for keys, p_t, dp_t in parts:
                    ds_t = p_t * (dp_t - delta)
                    dsb = ds_t.astype(BF16)
                    dv_s[keys, :] += _dot(p_t.astype(BF16), doi, NN)
                    dk_s[keys, :] += _dot(dsb, qi, NN)
                    dq_t = dq_t + _dot(kt_s[a, :, keys], dsb, NN)
                    dcs_s[a, keys, :] += sum(ds_t[:, b * LANES:(b + 1) * LANES] for b in range(t // LANES))
            dq_s[rows, :] = dq_t.T
        first = pl.program_id(0) == 0
        last = pl.program_id(0) == P - 1
        dq_raw, dgq = _pair_norm_bwd(q_ref[...], gq_ref[...], dq_s[...] * 0.125, is_a)
        dq_ref[...] = dq_raw.astype(BF16)
        _accumulate(dgq_ref, dgq, first)
        dk_raw, dgk = _pair_norm_bwd(k_ref[...], gk_ref[...], dk_s[...], is_a)
        dk_ref[...] = dk_raw.astype(BF16)
        _accumulate(dgk_ref, dgk, first)
        dv_ref[...] = dv_s[...].astype(BF16)
        for a in range(2):
            dc_ref[a] = -jnp.sum(dcs_s[a], axis=1, keepdims=True)

        @pl.when(last)
        def _():
            _fold_row(dgq_ref)
            _fold_row(dgk_ref)

    def cols(off):
        return pl.BlockSpec((S, LANES), lambda p: (0, off + p))

    col = pl.BlockSpec((2, S, 1), lambda p: (p, 0, 0))
    rowv = pl.BlockSpec((2, 1, S), lambda p: (p, 0, 0))
    gain = pl.BlockSpec((1, LANES), lambda p: (0, 0))
    wide = jax.ShapeDtypeStruct((S, H * HEAD_DIM), BF16)
    gs = jax.ShapeDtypeStruct((1, LANES), F32)
    return _pcall(body, name=name, grid=(P,),
                  in_specs=[cols(0), cols(P), cols(2 * P), col, rowv, gain, gain, rowv, cols(0)],
                  out_specs=[cols(0), cols(0), cols(0), col, gain, gain],
                  out_shape=[wide, wide, wide, jax.ShapeDtypeStruct((H, S, 1), F32), gs, gs],
                  scratch_shapes=[pltpu.VMEM((2, S, LANES), BF16), pltpu.VMEM((S, LANES), BF16), pltpu.VMEM((2, LANES, S), BF16),
                                  pltpu.VMEM((S, LANES), BF16), pltpu.VMEM((2, S, LANES), BF16)]
                  + [pltpu.VMEM((S, LANES), F32)] * 3 + [pltpu.VMEM((2, S, LANES), F32)],
                  compiler_params=_params(("arbitrary",)))(proj, proj, proj, c_col, c_row, gq2, gk2, lse_row, do)


def _bucket_onehot():
    W = WINDOW
    dist = np.arange(W)[:, None] + W - np.arange(2 * W)[None, :]
    n = np.maximum(dist, 0)
    max_exact = N_BUCKETS // 2
    large = max_exact + (np.log(np.maximum(n, 1) / max_exact) / np.log(REL_MAX_DIST / max_exact)
                         * (N_BUCKETS - max_exact)).astype(np.int32)
    large = np.minimum(large, N_BUCKETS - 1)
    bucket = np.where(n < max_exact, n, large).astype(np.int32)
    valid = (dist >= 0) & (dist < W)
    onehot = (bucket[None] == np.arange(N_BUCKETS)[:, None, None]) & valid[None]
    return onehot.reshape(N_BUCKETS, W * 2 * W).astype(np.float32)


def _bias_expand(rel_bias_t, onehot, name, tn=4096):
    HQ, NB = rel_bias_t.shape
    L = onehot.shape[1]

    def body(r_ref, oh_ref, out_ref):
        out_ref[...] = _dot(r_ref[...], oh_ref[...].astype(F32), NN, precision=lax.Precision.HIGHEST)

    return _pcall(body, name=name, grid=(L // tn,),
                  in_specs=[pl.BlockSpec((HQ, NB), lambda i: (0, 0)), pl.BlockSpec((NB, tn), lambda i: (0, i))],
                  out_specs=pl.BlockSpec((HQ, tn), lambda i: (0, i)),
                  out_shape=jax.ShapeDtypeStruct((HQ, L), F32), compiler_params=_params(("parallel",)))(rel_bias_t, onehot)


def _bias_reduce(dbias, onehot, name, tk=4096):
    HQ, L = dbias.shape
    NB = onehot.shape[0]

    def body(d_ref, oh_ref, out_ref):
        part = _dot(d_ref[...], oh_ref[...].astype(F32), NT, precision=lax.Precision.HIGHEST)
        _accumulate(out_ref, part, pl.program_id(0) == 0)

    return _pcall(body, name=name, grid=(L // tk,),
                  in_specs=[pl.BlockSpec((HQ, tk), lambda i: (0, i)), pl.BlockSpec((NB, tk), lambda i: (0, i))],
                  out_specs=pl.BlockSpec((HQ, NB), lambda i: (0, 0)),
                  out_shape=jax.ShapeDtypeStruct((HQ, NB), F32), compiler_params=_params(("arbitrary",)))(dbias, onehot)


def _stacked_query_index(n_rows_or_cols_axis, shape):
    idx = lax.broadcasted_iota(jnp.int32, shape, n_rows_or_cols_axis)
    return jnp.where(idx >= WINDOW, idx - WINDOW, idx)


def _swa_fwd(qproj, kk, vv, bias_ab, sink_col, gq2, gk2, name):
    S, HQD = qproj.shape
    KVH = kk.shape[0]
    PP = HQD // LANES
    NP = PP // KVH
    W = WINDOW
    nb = S // W

    def body(q_ref, k_ref, v_ref, bias_ref, sink_ref, gq_ref, gk_ref, o_ref, lse_ref, qs_s, kb_s, vb_s):
        is_a = _lane_is_a()
        qn = _pair_norm(q_ref[...], gq_ref[...], is_a) * 0.125
        qs_s[0] = jnp.where(is_a, qn, 0.0).astype(BF16)
        qs_s[1] = jnp.where(is_a, 0.0, qn).astype(BF16)
        kb_s[...] = _pair_norm(k_ref[...], gk_ref[...], is_a).astype(BF16)
        vb_s[...] = v_ref[...].astype(BF16)
        sink = sink_ref[...]
        qi1 = _stacked_query_index(0, (2 * W, W))
        first_valid = lax.broadcasted_iota(jnp.int32, (2 * W, W), 1) <= qi1
        qi2 = _stacked_query_index(0, (2 * W, 2 * W))
        key2 = lax.broadcasted_iota(jnp.int32, (2 * W, 2 * W), 1)
        band_valid = (key2 > qi2) & (key2 <= qi2 + W)
        for n in range(nb):
            rows = slice(n * W, (n + 1) * W)
            keys = slice(0, W) if n == 0 else slice((n - 1) * W, (n + 1) * W)
            lhs = jnp.concatenate([qs_s[0, rows, :], qs_s[1, rows, :]], axis=0)
            s = _dot(lhs, kb_s[keys, :], NT) + (bias_ref[:, W:2 * W] if n == 0 else bias_ref[...])
            s = jnp.where(first_valid if n == 0 else band_valid, s, NEG)
            m = jnp.maximum(jnp.max(s, axis=-1, keepdims=True), sink)
            e = jnp.exp(s - m)
            l = jnp.sum(e, axis=-1, keepdims=True) + jnp.exp(sink - m)
            o_ab = _dot(e.astype(BF16), vb_s[keys, :], NN) / l
            o_ref[rows, :] = jnp.where(is_a, o_ab[0:W, :], o_ab[W:2 * W, :]).astype(BF16)
            lse_ref[n] = m + jnp.log(l)

    qcols = pl.BlockSpec((S, LANES), lambda a, g: (0, a * NP + g))
    kvs = pl.BlockSpec((None, S, LANES), lambda a, g: (a, 0, 0))
    gain = pl.BlockSpec((1, LANES), lambda a, g: (0, 0))
    return _pcall(body, name=name, grid=(KVH, NP),
                  in_specs=[qcols, kvs, kvs, pl.BlockSpec((None, 2 * W, 2 * W), lambda a, g: (a * NP + g, 0, 0)),
                            pl.BlockSpec((None, 2 * W, 1), lambda a, g: (a * NP + g, 0, 0)), gain, gain],
                  out_specs=[qcols, pl.BlockSpec((None, nb, 2 * W, 1), lambda a, g: (a * NP + g, 0, 0, 0))],
                  out_shape=[jax.ShapeDtypeStruct((S, HQD), BF16), jax.ShapeDtypeStruct((PP, nb, 2 * W, 1), F32)],
                  scratch_shapes=[pltpu.VMEM((2, S, LANES), BF16), pltpu.VMEM((S, LANES), BF16), pltpu.VMEM((S, LANES), BF16)],
                  compiler_params=_params(("parallel", "parallel")))(qproj, kk, vv, bias_ab, sink_col, gq2, gk2)


def _swa_bwd(qproj, kk, vv, bias_t_ab, sink_row, gq2, gk2, lse_row, do, name):
    S, HQD = qproj.shape
    KVH = kk.shape[0]
    PP = HQD // LANES
    NP = PP // KVH
    W = WINDOW
    nb = S // W

    def body(q_ref, k_ref, v_ref, bias_ref, sink_ref, gq_ref, gk_ref, lr_ref, do_ref,
             dq_ref, dk_ref, dv_ref, db_ref, dsink_ref, dgq_ref, dgk_ref,
             qs_s, kb_s, kt_s, vb_s, dob_s, dq_s, dk_s, dv_s):
        a, g = pl.program_id(0), pl.program_id(1)
        is_a = _lane_is_a()
        qn = _pair_norm(q_ref[...], gq_ref[...], is_a) * 0.125
        qs_s[0] = jnp.where(is_a, qn, 0.0).astype(BF16)
        qs_s[1] = jnp.where(is_a, 0.0, qn).astype(BF16)
        kn = _pair_norm(k_ref[...], gk_ref[...], is_a)
        kb_s[...] = kn.astype(BF16)
        kt_s[...] = kn.T.astype(BF16)
        vb_s[...] = v_ref[...].astype(BF16)
        dov = do_ref[...]
        dob_s[0] = jnp.where(is_a, dov, 0.0).astype(BF16)
        dob_s[1] = jnp.where(is_a, 0.0, dov).astype(BF16)
        sink = sink_ref[...]

        @pl.when(g == 0)
        def _():
            dk_s[...] = jnp.zeros((S, LANES), F32)
            dv_s[...] = jnp.zeros((S, LANES), F32)

        qi1 = _stacked_query_index(1, (W, 2 * W))
        first_valid = lax.broadcasted_iota(jnp.int32, (W, 2 * W), 0) <= qi1
        qi2 = _stacked_query_index(1, (2 * W, 2 * W))
        key2 = lax.broadcasted_iota(jnp.int32, (2 * W, 2 * W), 0)
        band_valid = (key2 > qi2) & (key2 <= qi2 + W)
        head_rows = lax.broadcasted_iota(jnp.int32, (LANES, W), 0) < HEAD_DIM
        db = jnp.zeros((2 * W, 2 * W), F32)
        dsk = jnp.zeros((1, 2 * W), F32)
        pend_k = pend_v = None
        for n in range(nb):
            rows = slice(n * W, (n + 1) * W)
            keys = slice(0, W) if n == 0 else slice((n - 1) * W, (n + 1) * W)
            lhs_q = jnp.concatenate([qs_s[0, rows, :], qs_s[1, rows, :]], axis=0)
            lhs_do = jnp.concatenate([dob_s[0, rows, :], dob_s[1, rows, :]], axis=0)
            lse = lr_ref[n]
            s_t = _dot(kb_s[keys, :], lhs_q, NT) + (bias_ref[W:2 * W, :] if n == 0 else bias_ref[...])
            p_t = jnp.where(first_valid if n == 0 else band_valid, jnp.exp(s_t - lse), 0.0)
            dp_t = _dot(vb_s[keys, :], lhs_do, NT)
            delta = jnp.sum(p_t * dp_t, axis=0, keepdims=True)
            ds_t = p_t * (dp_t - delta)
            dsb = ds_t.astype(BF16)
            dsk = dsk - jnp.exp(sink - lse) * delta
            dv_band = _dot(p_t.astype(BF16), lhs_do, NN)
            dk_band = _dot(dsb, lhs_q, NN)
            dq_t = _dot(kt_s[:, keys], dsb, NN)
            dq_s[rows, :] = jnp.where(head_rows, dq_t[:, 0:W], dq_t[:, W:2 * W]).T
            if n == 0:
                db = jnp.concatenate([jnp.zeros((W, 2 * W), F32), ds_t], axis=0)
                pend_k, pend_v = dk_band, dv_band
            else:
                db = db + ds_t
                prev = slice((n - 1) * W, n * W)
                dk_s[prev, :] += pend_k + dk_band[0:W, :]
                dv_s[prev, :] += pend_v + dv_band[0:W, :]
                pend_k, pend_v = dk_band[W:2 * W, :], dv_band[W:2 * W, :]
        tail = slice((nb - 1) * W, nb * W)
        dk_s[tail, :] += pend_k
        dv_s[tail, :] += pend_v
        db_ref[...] = db
        dsink_ref[0] = jnp.broadcast_to(jnp.sum(dsk[:, 0:W], axis=1, keepdims=True), (1, LANES))
        dsink_ref[1] = jnp.broadcast_to(jnp.sum(dsk[:, W:2 * W], axis=1, keepdims=True), (1, LANES))
        dq_raw, dgq = _pair_norm_bwd(q_ref[...], gq_ref[...], dq_s[...] * 0.125, is_a)
        dq_ref[...] = dq_raw.astype(BF16)
        _accumulate(dgq_ref, dgq, jnp.logical_and(a == 0, g == 0))

        @pl.when(jnp.logical_and(a == KVH - 1, g == NP - 1))
        def _():
            _fold_row(dgq_ref)

        @pl.when(g == NP - 1)
        def _():
            dk_raw, dgk = _pair_norm_bwd(k_ref[...], gk_ref[...], _fold_heads(dk_s[...]), is_a)
            dk_ref[...] = dk_raw
            _accumulate(dgk_ref, dgk, a == 0)
            dv_ref[...] = _fold_heads(dv_s[...])

    qcols = pl.BlockSpec((S, LANES), lambda a, g: (0, a * NP + g))
    kvs = pl.BlockSpec((None, S, LANES), lambda a, g: (a, 0, 0))
    sq = pl.BlockSpec((None, 2 * W, 2 * W), lambda a, g: (a * NP + g, 0, 0))
    gain = pl.BlockSpec((1, LANES), lambda a, g: (0, 0))
    ks = jax.ShapeDtypeStruct((KVH, S, LANES), F32)
    gs = jax.ShapeDtypeStruct((1, LANES), F32)
    return _pcall(body, name=name, grid=(KVH, NP),
                  in_specs=[qcols, kvs, kvs, sq, pl.BlockSpec((None, 1, 2 * W), lambda a, g: (a * NP + g, 0, 0)), gain, gain,
                            pl.BlockSpec((None, nb, 1, 2 * W), lambda a, g: (a * NP + g, 0, 0, 0)), qcols],
                  out_specs=[qcols, kvs, kvs, sq, pl.BlockSpec((2, 1, LANES), lambda a, g: (a * NP + g, 0, 0)), gain, gain],
                  out_shape=[jax.ShapeDtypeStruct((S, HQD), BF16), ks, ks, jax.ShapeDtypeStruct((PP, 2 * W, 2 * W), F32),
                             jax.ShapeDtypeStruct((2 * PP, 1, LANES), F32), gs, gs],
                  scratch_shapes=[pltpu.VMEM((2, S, LANES), BF16), pltpu.VMEM((S, LANES), BF16), pltpu.VMEM((LANES, S), BF16),
                                  pltpu.VMEM((S, LANES), BF16), pltpu.VMEM((2, S, LANES), BF16)] + [pltpu.VMEM((S, LANES), F32)] * 3,
                  compiler_params=_params(("arbitrary", "arbitrary")))(qproj, kk, vv, bias_t_ab, sink_row, gq2, gk2, lse_row, do)


def _adamw(w, g, m, v, name, tr=256):
    R, C = w.shape
    tr = min(tr, R)
    assert R % tr == 0

    def body(w_ref, g_ref, m_ref, v_ref, d_ref, m2_ref, v2_ref):
        gv = g_ref[...]
        m2 = ADAM_B1 * m_ref[...] + (1.0 - ADAM_B1) * gv
        v2 = ADAM_B2 * v_ref[...] + (1.0 - ADAM_B2) * jnp.square(gv)
        m_hat = m2 / (1.0 - ADAM_B1 ** ADAM_STEP)
        v_hat = v2 / (1.0 - ADAM_B2 ** ADAM_STEP)
        d_ref[...] = -ADAM_LR * (m_hat / (jnp.sqrt(v_hat) + ADAM_EPS) + ADAM_WD * w_ref[...])
        m2_ref[...] = m2
        v2_ref[...] = v2

    blk = pl.BlockSpec((tr, C), lambda i: (i, 0))
    return _pcall(body, name=name, grid=(R // tr,), in_specs=[blk] * 4, out_specs=[blk] * 3,
                  out_shape=[jax.ShapeDtypeStruct((R, C), F32)] * 3, compiler_params=_params(("parallel",)))(w, g, m, v)


def _sum_core_pair(arr, got, place, name, tr=128):
    P, hr, C = got.shape
    assert hr % tr == 0
    nt = hr // tr

    def body(place_ref, a_ref, g_ref, o_ref):
        o_ref[...] = (a_ref[...].astype(F32) + g_ref[...].astype(F32)).astype(o_ref.dtype)

    spec = pltpu.PrefetchScalarGridSpec(
        num_scalar_prefetch=1, grid=(P, nt),
        in_specs=[pl.BlockSpec((None, tr, C), lambda j, i, pr: (j, pr[1] * nt + i, 0)),
                  pl.BlockSpec((None, tr, C), lambda j, i, pr: (j, i, 0))],
        out_specs=pl.BlockSpec((None, tr, C), lambda j, i, pr: (j, i, 0)))
    return _pcall(body, name=name, grid_spec=spec, out_shape=jax.ShapeDtypeStruct(got.shape, BF16),
                  compiler_params=_params(("parallel", "parallel")))(place, arr, got)


def _sum_chips(pair, landed, place, name, tr=128):
    _, R, C = landed.shape
    assert R % tr == 0

    def body(place_ref, p_ref, l_ref, o_ref):
        acc = p_ref[...].astype(F32)
        for k in range(3):
            acc = acc + l_ref[k].astype(F32)
        o_ref[...] = acc

    spec = pltpu.PrefetchScalarGridSpec(
        num_scalar_prefetch=1, grid=(R // tr,),
        in_specs=[pl.BlockSpec((None, tr, C), lambda i, pr: (pr[0], i, 0)), pl.BlockSpec((3, tr, C), lambda i, pr: (0, i, 0))],
        out_specs=pl.BlockSpec((None, tr, C), lambda i, pr: (pr[1], i, 0)))
    return _pcall(body, name=name, grid_spec=spec, out_shape=jax.ShapeDtypeStruct((2, R, C), F32),
                  compiler_params=_params(("parallel",)))(place, pair, landed)


def _sum_parts(parts, name, out_dtype, tr=128):
    P, R, C = parts.shape
    tr = min(tr, R)
    assert R % tr == 0, (R, tr)

    def body(p_ref, o_ref):
        acc = p_ref[0].astype(F32)
        for k in range(1, P):
            acc = acc + p_ref[k].astype(F32)
        o_ref[...] = acc.astype(o_ref.dtype)

    return _pcall(body, name=name, grid=(R // tr,), in_specs=[pl.BlockSpec((P, tr, C), lambda i: (0, i, 0))],
                  out_specs=pl.BlockSpec((tr, C), lambda i: (i, 0)),
                  out_shape=jax.ShapeDtypeStruct((R, C), out_dtype), compiler_params=_params(("parallel",)))(parts)


def _place():
    x, y, c = lax.axis_index("x"), lax.axis_index("y"), lax.axis_index("c")
    others = [(1 - x, y), (x, 1 - y), (1 - x, 1 - y)]
    return x, y, c, others


def _half_rows(ref, hh, lead=()):
    hr = ref.shape[-2] // 2
    return ref.at[(*lead, pl.ds(pl.multiple_of(hh * hr, 16), hr), slice(None))]


def _sem_arrays(*counts):
    return [pltpu.SemaphoreType.DMA((k,)) for k in counts]


def _allgather_group(shards, name):
    n = len(shards)
    per = 7

    def body(*refs):
        ins, outs = refs[:n], refs[n:2 * n]
        send_sems, recv_sems = refs[2 * n:]
        x, y, c, others = _place()
        me = 2 * x + y
        sibling = (x, y, 1 - c)

        def copy(w, k, src, dst, to):
            return pltpu.make_async_remote_copy(src_ref=src, dst_ref=dst, send_sem=send_sems.at[per * w + k],
                                                recv_sem=recv_sems.at[per * w + k], device_id=to, device_id_type=MESH)

        first = [copy(w, k, _half_rows(ins[w], c), _half_rows(outs[w], c, (me,)), (cx, cy, c))
                 for w in range(n) for k, (cx, cy) in enumerate(others)]
        own = [copy(w, 6, ins[w], outs[w].at[me], sibling) for w in range(n)]
        for cp in first + own:
            cp.start()
        passed = []
        for w in range(n):
            for k, (cx, cy) in enumerate(others):
                landed = _half_rows(outs[w], c, (2 * cx + cy,))
                copy(w, k, landed, landed, sibling).wait_recv()
                passed.append(copy(w, 3 + k, landed, landed, sibling))
                passed[-1].start()
        for w in range(n):
            for k, (cx, cy) in enumerate(others):
                theirs = _half_rows(outs[w], 1 - c, (2 * cx + cy,))
                copy(w, 3 + k, theirs, theirs, sibling).wait_recv()
            own[w].wait_recv()
        for cp in first + passed + own:
            cp.wait_send()

    return _pcall(body, name=name, in_specs=[HBM_SPEC] * n, out_specs=[HBM_SPEC] * n,
                  out_shape=[jax.ShapeDtypeStruct((N_CHIPS,) + s.shape, s.dtype) for s in shards],
                  scratch_shapes=_sem_arrays(per * n, per * n))(*shards)


def _swap_halves_group(arrs, name):
    n = len(arrs)

    def body(*refs):
        ins, gots = refs[:n], refs[n:2 * n]
        send_sems, recv_sems = refs[2 * n:]
        x, y, c, _ = _place()
        swaps = [pltpu.make_async_remote_copy(src_ref=_half_rows(ins[w], 1 - c, (slice(None),)), dst_ref=gots[w],
                                              send_sem=send_sems.at[w], recv_sem=recv_sems.at[w],
                                              device_id=(x, y, 1 - c), device_id_type=MESH) for w in range(n)]
        for cp in swaps:
            cp.start()
        for cp in swaps:
            cp.wait()

    half_shapes = [jax.ShapeDtypeStruct((a.shape[0], a.shape[1] // 2, a.shape[2]), a.dtype) for a in arrs]
    return _pcall(body, name=name, in_specs=[HBM_SPEC] * n, out_specs=[HBM_SPEC] * n, out_shape=half_shapes,
                  scratch_shapes=_sem_arrays(n, n))(*arrs)


def _scatter_group(parts, name):
    n = len(parts)

    def body(*refs):
        ins, outs = refs[:n], refs[n:2 * n]
        send_sems, recv_sems = refs[2 * n:]
        x, y, c, others = _place()

        def copy(w, k, src_chip, to):
            return pltpu.make_async_remote_copy(src_ref=ins[w].at[src_chip], dst_ref=outs[w].at[k],
                                                send_sem=send_sems.at[3 * w + k], recv_sem=recv_sems.at[3 * w + k],
                                                device_id=to, device_id_type=MESH)

        sends = [copy(w, k, 2 * cx + cy, (cx, cy, c)) for w in range(n) for k, (cx, cy) in enumerate(others)]
        for cp in sends:
            cp.start()
        for cp in sends:
            cp.wait()

    return _pcall(body, name=name, in_specs=[HBM_SPEC] * n, out_specs=[HBM_SPEC] * n,
                  out_shape=[jax.ShapeDtypeStruct((3,) + p.shape[1:], p.dtype) for p in parts],
                  scratch_shapes=_sem_arrays(3 * n, 3 * n))(*parts)


def _share_halves_group(halves, name):
    n = len(halves)

    def body(*refs):
        bufs = refs[n:2 * n]
        send_sems, recv_sems = refs[2 * n:]
        x, y, c, _ = _place()
        swaps = [pltpu.make_async_remote_copy(src_ref=bufs[w].at[c], dst_ref=bufs[w].at[c], send_sem=send_sems.at[w],
                                              recv_sem=recv_sems.at[w], device_id=(x, y, 1 - c), device_id_type=MESH)
                 for w in range(n)]
        for cp in swaps:
            cp.start()
        for w in range(n):
            swaps[w].wait_send()
            pltpu.make_async_remote_copy(src_ref=bufs[w].at[c], dst_ref=bufs[w].at[1 - c], send_sem=send_sems.at[w],
                                         recv_sem=recv_sems.at[w], device_id=(x, y, 1 - c), device_id_type=MESH).wait_recv()

    return _pcall(body, name=name, in_specs=[HBM_SPEC] * n, out_specs=[HBM_SPEC] * n,
                  out_shape=[jax.ShapeDtypeStruct(h.shape, h.dtype) for h in halves],
                  input_output_aliases={w: w for w in range(n)},
                  scratch_shapes=_sem_arrays(n, n))(*halves)


def _allgather_small(blk, name):
    M, C = blk.shape

    def body(x_ref, out_ref, send_sems, recv_sems, local_sem):
        x, y, c, others = _place()
        me, sibling = (x, y, c), (x, y, 1 - c)

        def rows(px, py, pc):
            return out_ref.at[4 * px + 2 * py + pc]

        def copy(k, block, to, src=None):
            return pltpu.make_async_remote_copy(src_ref=rows(*block) if src is None else src, dst_ref=rows(*block),
                                                send_sem=send_sems.at[k], recv_sem=recv_sems.at[k], device_id=to, device_id_type=MESH)

        mine = pltpu.make_async_copy(x_ref, rows(*me), local_sem)
        mine.start()
        first = [copy(0, me, sibling, src=x_ref)]
        first += [copy(1 + j, me, (*chip, c), src=x_ref) for j, chip in enumerate(others)]
        for cp in first:
            cp.start()
        passed = [copy(4 + j, (*chip, c), sibling) for j, chip in enumerate(others)]
        for j, chip in enumerate(others):
            copy(1 + j, (*chip, c), me).wait_recv()
            passed[j].start()
        copy(0, sibling, me).wait_recv()
        for j, chip in enumerate(others):
            copy(4 + j, (*chip, 1 - c), me).wait_recv()
        for cp in first + passed:
            cp.wait_send()
        mine.wait()

    return _pcall(body, name=name, in_specs=[VMEM_SPEC], out_specs=VMEM_SPEC,
                  out_shape=jax.ShapeDtypeStruct((8, M, C), blk.dtype),
                  scratch_shapes=[pltpu.SemaphoreType.DMA((7,)), pltpu.SemaphoreType.DMA((7,)), pltpu.SemaphoreType.DMA])(blk)


def _pack_rows(n_elems, width=PACK_W, align=PACK_ROW_ALIGN):
    rows = -(-n_elems // width)
    return -(-rows // align) * align


def _pack(arrays, dtype, width=PACK_W, align=PACK_ROW_ALIGN):
    flat = jnp.concatenate([a.astype(dtype).reshape(-1) for a in arrays])
    rows = _pack_rows(flat.shape[0], width, align)
    flat = jnp.pad(flat, (0, rows * width - flat.shape[0]))
    return flat.reshape(rows, width)


def _pack_small(arrays):
    return _pack(arrays, F32, width=128, align=8)


def _unpack(flat, shapes):
    out, off = [], 0
    for shp in shapes:
        n = int(np.prod(shp))
        out.append(flat[..., off:off + n].reshape(flat.shape[:-1] + tuple(shp)))
        off += n
    return out


def _doubled_heads(x2d, n_heads):
    S = x2d.shape[0]
    h = x2d.reshape(S, n_heads, HEAD_DIM).transpose(1, 0, 2)
    return jnp.concatenate([h, h], axis=-1)


def _mlp_fwd(h, g, w_up4, w_down, tag):
    (n,) = _rms_fwd(h, [g], f"rms_mlp{tag}")
    u, a = _matmul(n, w_up4, "nn", f"up{tag}", out_dtypes=(F32, BF16), chipwise="b",
                   epilogue=lambda acc: (acc, jnp.square(jnp.maximum(acc, 0.0))))
    h_out = _matmul(a, w_down, "nn", f"down{tag}", extras=(h,), epilogue=lambda acc, res: (res + acc,), tk=1024)
    return h_out, (n, u, a)


def _mlp_bwd(dh_out, h, g, w_up4, w_down, saved, tag):
    n, u, a = saved
    dw_down = _matmul(a, dh_out, "tn", f"dw_down{tag}", out_dtypes=(BF16,))
    du = _matmul(dh_out, w_down, "nt", f"du{tag}", out_dtypes=(BF16,), extras=(u,),
                 epilogue=lambda acc, uu: (acc * (2.0 * jnp.maximum(uu, 0.0)),))
    dw_up = _matmul(n, du, "tn", f"dw_up{tag}", out_dtypes=(BF16,), chipwise="out")
    dn = _matmul(du, w_up4, "nt", f"dn_mlp{tag}", tk=w_up4.shape[2], chipwise="b")
    dh, (dg,) = _rms_bwd(h, dh_out, [g], [dn], f"rms_mlp_bwd{tag}")
    return dh, dg, dw_up, dw_down


def kernel(x, g_attn, g_mlp, w_in_a, b_f, gq_a, gk_a, w_out_a, g_kv, w_kv, gk_b, w_q_b, gq_b, sinks, rel_bias, w_out_b, w_up, w_down, loss_target, m_g_attn, m_g_mlp, m_w_in_a, m_b_f, m_gq_a, m_gk_a, m_w_out_a, m_g_kv, m_w_kv, m_gk_b, m_w_q_b, m_gq_b, m_sinks, m_rel_bias, m_w_out_b, m_w_up, m_w_down, v_g_attn, v_g_mlp, v_w_in_a, v_b_f, v_gq_a, v_gk_a, v_w_out_a, v_g_kv, v_w_kv, v_gk_b, v_w_q_b, v_gq_b, v_sinks, v_rel_bias, v_w_out_b, v_w_up, v_w_down):
    given = dict(locals())
    S, D = x.shape[1], x.shape[2]
    H = D // HEAD_DIM
    KVH = w_kv.shape[1] // (2 * HEAD_DIM)
    kvw = KVH * HEAD_DIM
    hw = H * HEAD_DIM
    W = WINDOW
    nb = S // W
    c_idx = lax.axis_index("c")
    xs, tgt = x[0], loss_target[0]

    shards = {"w_in_a": w_in_a[0], "w_out_a": w_out_a[0], "w_up0": w_up[0], "w_down0": w_down[0], "w_kv": w_kv,
              "w_q_b": w_q_b[0], "w_out_b": w_out_b[0], "w_up1": w_up[1], "w_down1": w_down[1]}
    parts = list(shards)
    lane_pad = lambda a: jnp.pad(a, [(0, 0)] * (a.ndim - 1) + [(0, (-a.shape[-1]) % LANES)])
    n_in_shard = w_in_a.shape[2]
    gathered = dict(zip(parts, _allgather_group([lane_pad(shards[n].astype(BF16)) for n in parts], "allgather_weights")))
    win = jnp.moveaxis(gathered["w_in_a"][:, :, :n_in_shard], 0, 1).reshape(D, -1)
    win = jnp.pad(win, ((0, 0), (0, (-win.shape[1]) % 128)))
    wout_a, wq_b, wout_b = (gathered[n].reshape(-1, D) for n in ("w_out_a", "w_q_b", "w_out_b"))
    wkv = gathered["w_kv"].reshape(D, -1)
    wup = [gathered["w_up0"], gathered["w_up1"]]
    wdown = [gathered["w_down0"].reshape(-1, D), gathered["w_down1"].reshape(-1, D)]
    n_in = win.shape[1]
    tile_in = 640 if n_in % 640 == 0 else 128

    vec = lambda a: a.reshape(1, -1)
    twice = lambda a: jnp.tile(a.reshape(1, -1), (1, 2))

    (n0,) = _rms_fwd(xs, [vec(g_attn[0])], "rms_attn0")
    proj = _matmul(n0, win, "nn", "proj_in", tn=tile_in)
    zt = proj[:, 3 * hw:3 * hw + H].T
    c_row = _gate_fwd(zt, b_f.reshape(H, 1), "gate_fwd")
    c_col3, c_row3 = c_row.reshape(H, S, 1), c_row.reshape(H, 1, S)
    o_a, lse_a = _fox_fwd(proj, c_col3, c_row3, twice(gq_a[0]), twice(gk_a[0]), H, "fox_fwd")
    h1 = _matmul(o_a, wout_a, "nn", "out_a", extras=(xs,), epilogue=lambda acc, res: (res + acc,))
    h2, mlp0 = _mlp_fwd(h1, vec(g_mlp[0]), wup[0], wdown[0], "0")

    nkv, n2 = _rms_fwd(h2, [vec(g_kv), vec(g_attn[1])], "rms_attn1")
    kv = _matmul(nkv, wkv, "nn", "proj_kv")
    kk, vv = _doubled_heads(kv[:, :kvw], KVH), _doubled_heads(kv[:, kvw:], KVH)
    q2 = _matmul(n2, wq_b, "nn", "proj_q")
    onehot = jnp.asarray(_bucket_onehot(), dtype=BF16)
    bias = _bias_expand(rel_bias.T, onehot, "bias_expand").reshape(H, W, 2 * W)
    bias_ab = bias.reshape(H // 2, 2 * W, 2 * W)
    bias_t_ab = bias.reshape(H // 2, 2, W, 2 * W).transpose(0, 3, 1, 2).reshape(H // 2, 2 * W, 2 * W)
    sink_ab = jnp.repeat(sinks[0].reshape(H // 2, 2), W, axis=1)
    o_b, lse_b = _swa_fwd(q2, kk, vv, bias_ab, sink_ab.reshape(H // 2, 2 * W, 1), twice(gq_b[0]), twice(gk_b), "swa_fwd")
    h3 = _matmul(o_b, wout_b, "nn", "out_b", extras=(h2,), epilogue=lambda acc, res: (res + acc,))
    h4, mlp1 = _mlp_fwd(h3, vec(g_mlp[1]), wup[1], wdown[1], "1")

    dh4, loss_part = _loss_head(h4, tgt, "loss_head")
    loss = lax.psum(loss_part[0, 0], ("x", "y", "c"))

    dh3, dg_mlp1, dw_up1, dw_down1 = _mlp_bwd(dh4, h3, vec(g_mlp[1]), wup[1], wdown[1], mlp1, "1")
    dw_out_b = _matmul(o_b, dh3, "tn", "dw_out_b", out_dtypes=(BF16,))
    do_b = _matmul(dh3, wout_b, "nt", "do_b")
    dq2, dk2, dv2, dbias_t_ab, dsink, dgq_b, dgk_b = _swa_bwd(
        q2, kk, vv, bias_t_ab, sink_ab.reshape(H // 2, 1, 2 * W), twice(gq_b[0]), twice(gk_b),
        lse_b.reshape(H // 2, nb, 1, 2 * W), do_b, "swa_bwd")
    dbias = dbias_t_ab.reshape(H // 2, 2 * W, 2, W).transpose(0, 2, 3, 1).reshape(H, W * 2 * W)
    d_rel_bias = _bias_reduce(dbias, onehot, "bias_reduce").T
    dw_q_b = _matmul(n2, dq2, "tn", "dw_q_b", out_dtypes=(BF16,))
    dn2 = _matmul(dq2, wq_b, "nt", "dn2")
    dkv = jnp.concatenate([dk2[h, :, :HEAD_DIM] for h in range(KVH)] + [dv2[h, :, :HEAD_DIM] for h in range(KVH)],
                          axis=1).astype(BF16)
    dw_kv = _matmul(nkv, dkv, "tn", "dw_kv", out_dtypes=(BF16,))
    dnkv = _matmul(dkv, wkv, "nt", "dnkv")
    dh2, (dg_kv, dg_attn1) = _rms_bwd(h2, dh3, [vec(g_kv), vec(g_attn[1])], [dnkv, dn2], "rms_attn1_bwd")

    dh1, dg_mlp0, dw_up0, dw_down0 = _mlp_bwd(dh2, h1, vec(g_mlp[0]), wup[0], wdown[0], mlp0, "0")
    dw_out_a = _matmul(o_a, dh1, "tn", "dw_out_a", out_dtypes=(BF16,))
    do_a = _matmul(dh1, wout_a, "nt", "do_a")
    dq_a, dk_a, dv_a, dc_col, dgq_a, dgk_a = _fox_bwd(
        proj, c_col3, c_row3, twice(gq_a[0]), twice(gk_a[0]), lse_a.reshape(H, 1, S), do_a, H, "fox_bwd")
    dzt, db_f = _gate_bwd(dc_col.reshape(H, S), zt, b_f.reshape(H, 1), "gate_bwd")
    dproj = jnp.concatenate([dq_a, dk_a, dv_a, dzt.T.astype(BF16), jnp.zeros((S, n_in - 3 * hw - H), BF16)], axis=1)
    dw_in = _matmul(n0, dproj, "tn", "dw_in", out_dtypes=(BF16,), tn=tile_in)
    dn0 = _matmul(dproj, win, "nt", "dn0", tk=tile_in)
    grad_x, (dg_attn0,) = _rms_bwd(xs, dh1, [vec(g_attn[0])], [dn0], "rms_attn0_bwd")

    dw_in4 = lane_pad(dw_in[:, :3 * hw + H].reshape(D, N_CHIPS, -1).transpose(1, 0, 2))
    chipwise = {"w_in_a": dw_in4, "w_out_a": dw_out_a.reshape(N_CHIPS, -1, D), "w_up0": dw_up0,
                "w_down0": dw_down0.reshape(N_CHIPS, -1, D), "w_kv": dw_kv.reshape(N_CHIPS, -1, 2 * kvw),
                "w_q_b": dw_q_b.reshape(N_CHIPS, -1, D), "w_out_b": dw_out_b.reshape(N_CHIPS, -1, D), "w_up1": dw_up1,
                "w_down1": dw_down1.reshape(N_CHIPS, -1, D)}
    place = jnp.stack([2 * lax.axis_index("x") + lax.axis_index("y"), c_idx]).astype(jnp.int32)
    got = _swap_halves_group([chipwise[n] for n in parts], "swap_grad_halves")
    pair_sums = [_sum_core_pair(chipwise[n], g, place, "sum_core_pair_" + n) for n, g in zip(parts, got)]
    landed = _scatter_group(pair_sums, "scatter_grads")
    halves = [_sum_chips(p, l, place, "sum_chips_" + n) for n, p, l in zip(parts, pair_sums, landed)]
    reduced = {n: r.reshape(-1, r.shape[2])[:, :shards[n].shape[1]]
               for n, r in zip(parts, _share_halves_group(halves, "share_reduced_halves"))}
    big_grads = [reduced["w_in_a"][None], reduced["w_out_a"][None], reduced["w_kv"], reduced["w_q_b"][None],
                 reduced["w_out_b"][None], jnp.stack([reduced["w_up0"], reduced["w_up1"]]),
                 jnp.stack([reduced["w_down0"], reduced["w_down1"]])]

    small_grads = {
        "g_attn": jnp.concatenate([dg_attn0, dg_attn1], axis=0), "g_mlp": jnp.concatenate([dg_mlp0, dg_mlp1], axis=0),
        "b_f": db_f.reshape(1, H), "gq_a": dgq_a[:, :HEAD_DIM], "gk_a": dgk_a[:, :HEAD_DIM], "g_kv": dg_kv.reshape(-1),
        "gk_b": dgk_b[0, :HEAD_DIM], "gq_b": dgq_b[:, :HEAD_DIM], "sinks": dsink[:, 0, 0].reshape(1, H), "rel_bias": d_rel_bias,
    }
    small_shapes = [given[n].shape for n in SMALL]
    spack = _pack_small([small_grads[n] for n in SMALL])
    small_sum = _sum_parts(_allgather_small(spack, "allgather_small"), "sum_small", F32, tr=spack.shape[0])
    small_red = _unpack(small_sum.reshape(-1), small_shapes)

    grads = dict(zip([n for n, _ in BIG], big_grads))
    grads.update(dict(zip(SMALL, small_red)))
    sw = _pack_small([given[n] for n in SMALL])
    sm = _pack_small([given["m_" + n] for n in SMALL])
    sv = _pack_small([given["v_" + n] for n in SMALL])
    sd, sm2, sv2 = _adamw(sw, small_sum, sm, sv, "adamw_small", tr=sw.shape[0])
    delta = dict(zip(SMALL, _unpack(sd.reshape(-1), small_shapes)))
    new_m = dict(zip(SMALL, _unpack(sm2.reshape(-1), small_shapes)))
    new_v = dict(zip(SMALL, _unpack(sv2.reshape(-1), small_shapes)))
    for n, _ in BIG:
        w = given[n]
        two_d = (-1, w.shape[-1])
        d, m2, v2 = _adamw(w.reshape(two_d), grads[n].reshape(two_d), given["m_" + n].reshape(two_d),
                           given["v_" + n].reshape(two_d), "adamw_" + n)
        delta[n], new_m[n], new_v[n] = d.reshape(w.shape), m2.reshape(w.shape), v2.reshape(w.shape)

    order = ["g_attn", "g_mlp", "w_in_a", "b_f", "gq_a", "gk_a", "w_out_a", "g_kv", "w_kv", "gk_b", "w_q_b", "gq_b",
             "sinks", "rel_bias", "w_out_b", "w_up", "w_down"]
    return (loss, grad_x[None], *[grads[n] for n in order], *[delta[n] for n in order],
            *[new_m[n] for n in order], *[new_v[n] for n in order])
```

```python
import numpy as np
import jax
import jax.numpy as jnp
from jax import lax
from jax.experimental import pallas as pl
from jax.experimental.pallas import tpu as pltpu

F32 = jnp.float32
BF16 = jnp.bfloat16
MESH = pl.DeviceIdType.MESH

HEAD_DIM = 64
LANES = 128
WINDOW = 128
N_BUCKETS = 32
REL_MAX_DIST = 128
NORM_EPS = 1e-6
ADAM_LR = 0.001
ADAM_B1 = 0.9
ADAM_B2 = 0.999
ADAM_EPS = 1e-08
ADAM_WD = 0.01
ADAM_STEP = 10
NEG = -1e30
N_CHIPS = 4
PACK_W = 1024
PACK_ROW_ALIGN = 256
VMEM_LIMIT = 56 * 1024 * 1024
HBM_SPEC = pl.BlockSpec(memory_space=pltpu.HBM)
VMEM_SPEC = pl.BlockSpec(memory_space=pltpu.VMEM)

BIG = (("w_in_a", 2), ("w_out_a", 1), ("w_kv", 0), ("w_q_b", 1), ("w_out_b", 1), ("w_up", 2), ("w_down", 1))
SMALL = ("g_attn", "g_mlp", "b_f", "gq_a", "gk_a", "g_kv", "gk_b", "gq_b", "sinks", "rel_bias")


def _pcall(body, **kw):
    return pl.pallas_call(body, **kw)


def _params(sem=None):
    return pltpu.CompilerParams(dimension_semantics=sem, vmem_limit_bytes=VMEM_LIMIT)


def _rinv(x):
    return lax.rsqrt(jnp.mean(x * x, axis=-1, keepdims=True) + NORM_EPS)


def _dot(a, b, dims, precision=None):
    return lax.dot_general(a, b, (dims, ((), ())), precision=precision, preferred_element_type=F32)


NN = ((1,), (0,))
NT = ((1,), (1,))
TN = ((0,), (0,))


def _accumulate(ref, val, first):
    @pl.when(first)
    def _():
        ref[...] = val

    @pl.when(jnp.logical_not(first))
    def _():
        ref[...] += val


def _matmul(a, b, mode, name, out_dtypes=(F32,), extras=(), epilogue=None, tm=512, tn=512, tk=None, chipwise=None):
    if chipwise == "b":
        nc = b.shape[2]
        M, K = a.shape
        (K2, N) = (b.shape[1], N_CHIPS * nc) if mode == "nn" else (N_CHIPS * nc, b.shape[1])
    elif mode == "nn":
        (M, K), (K2, N) = a.shape, b.shape
    elif mode == "nt":
        (M, K), (N, K2) = a.shape, b.shape
    else:
        (K, M), (K2, N) = a.shape, b.shape
    assert K == K2, (a.shape, b.shape, mode)
    tm, tn = min(tm, M), min(tn, N)
    tk = K if tk is None else tk
    assert M % tm == 0 and N % tn == 0 and K % tk == 0, (M, N, K, tm, tn, tk)
    nk = K // tk
    dims = {"nn": NN, "nt": NT, "tn": TN}[mode]
    a_spec = pl.BlockSpec((tk, tm), lambda i, j, k: (k, i)) if mode == "tn" else pl.BlockSpec((tm, tk), lambda i, j, k: (i, k))
    b_spec = pl.BlockSpec((tn, tk), lambda i, j, k: (j, k)) if mode == "nt" else pl.BlockSpec((tk, tn), lambda i, j, k: (k, j))
    o_spec = pl.BlockSpec((tm, tn), lambda i, j, k: (i, j))
    out_shape = (M, N)
    if chipwise == "b" and mode == "nn":
        per = nc // tn
        assert tk == K and nc % tn == 0
        b_spec = pl.BlockSpec((None, tk, tn), lambda i, j, k: (j // per, 0, j % per))
    elif chipwise == "b":
        assert mode == "nt" and tk == nc
        b_spec = pl.BlockSpec((None, tn, tk), lambda i, j, k: (k, j, 0))
    elif chipwise == "out":
        per = (N // N_CHIPS) // tn
        assert (N // N_CHIPS) % tn == 0
        o_spec = pl.BlockSpec((None, tm, tn), lambda i, j, k: (j // per, i, j % per))
        out_shape = (N_CHIPS, M, N // N_CHIPS)
        assert not extras
    n_ex, n_out = len(extras), len(out_dtypes)

    def body(*refs):
        a_ref, b_ref = refs[0], refs[1]
        ex_refs = refs[2:2 + n_ex]
        out_refs = refs[2 + n_ex:2 + n_ex + n_out]
        acc_ref = refs[2 + n_ex + n_out]
        k = pl.program_id(2)
        part = _dot(a_ref[...].astype(BF16), b_ref[...].astype(BF16), dims)

        @pl.when(k == 0)
        def _():
            acc_ref[...] = part

        @pl.when(k > 0)
        def _():
            acc_ref[...] += part

        @pl.when(k == nk - 1)
        def _():
            acc = acc_ref[...]
            outs = (acc,) if epilogue is None else epilogue(acc, *[r[...] for r in ex_refs])
            for r, o in zip(out_refs, outs):
                r[...] = o.astype(r.dtype)

    outs = _pcall(
        body, name=name, grid=(M // tm, N // tn, nk),
        in_specs=[a_spec, b_spec] + [o_spec] * n_ex,
        out_specs=[o_spec] * n_out,
        out_shape=[jax.ShapeDtypeStruct(out_shape, dt) for dt in out_dtypes],
        scratch_shapes=[pltpu.VMEM((tm, tn), F32)],
        compiler_params=_params(("parallel", "parallel", "arbitrary")),
    )(a, b, *extras)
    return outs[0] if n_out == 1 else outs


def _rms_fwd(x, gains, name, ts=256):
    S, D = x.shape
    ts = min(ts, S)
    n = len(gains)

    def body(*refs):
        x_ref, g_refs, o_refs = refs[0], refs[1:1 + n], refs[1 + n:]
        xv = x_ref[...]
        xh = xv * _rinv(xv)
        for g_ref, o_ref in zip(g_refs, o_refs):
            o_ref[...] = (xh * g_ref[...]).astype(BF16)

    row = pl.BlockSpec((ts, D), lambda i: (i, 0))
    vec = pl.BlockSpec((1, D), lambda i: (0, 0))
    return _pcall(body, name=name, grid=(S // ts,), in_specs=[row] + [vec] * n, out_specs=[row] * n,
                  out_shape=[jax.ShapeDtypeStruct((S, D), BF16)] * n, compiler_params=_params(("parallel",)))(x, *gains)


def _rms_bwd(x, dres, gains, dns, name, ts=256):
    S, D = x.shape
    ts = min(ts, S)
    n = len(gains)

    def body(*refs):
        x_ref, dres_ref = refs[0], refs[1]
        g_refs, dn_refs = refs[2:2 + n], refs[2 + n:2 + 2 * n]
        dx_ref, dg_refs = refs[2 + 2 * n], refs[3 + 2 * n:]
        xv = x_ref[...]
        r = _rinv(xv)
        xh = xv * r
        dx = dres_ref[...]
        first = pl.program_id(0) == 0
        for g_ref, dn_ref, dg_ref in zip(g_refs, dn_refs, dg_refs):
            dn = dn_ref[...].astype(F32)
            _accumulate(dg_ref, jnp.sum(dn * xh, axis=0, keepdims=True), first)
            dxh = dn * g_ref[...]
            dx = dx + r * (dxh - xh * jnp.mean(dxh * xh, axis=-1, keepdims=True))
        dx_ref[...] = dx

    row = pl.BlockSpec((ts, D), lambda i: (i, 0))
    vec = pl.BlockSpec((1, D), lambda i: (0, 0))
    outs = _pcall(body, name=name, grid=(S // ts,), in_specs=[row, row] + [vec] * n + [row] * n,
                  out_specs=[row] + [vec] * n,
                  out_shape=[jax.ShapeDtypeStruct((S, D), F32)] + [jax.ShapeDtypeStruct((1, D), F32)] * n,
                  compiler_params=_params(("arbitrary",)))(x, dres, *gains, *dns)
    return outs[0], outs[1:]


def _loss_head(h, tgt, name, ts=256):
    S, D = h.shape
    ts = min(ts, S)

    def body(h_ref, t_ref, dh_ref, loss_ref):
        err = h_ref[...] - t_ref[...]
        dh_ref[...] = err * (1.0 / D)
        part = 0.5 * jnp.sum(jnp.mean(err * err, axis=-1, keepdims=True), axis=0, keepdims=True)
        _accumulate(loss_ref, part, pl.program_id(0) == 0)

    row = pl.BlockSpec((ts, D), lambda i: (i, 0))
    return _pcall(body, name=name, grid=(S // ts,), in_specs=[row, row],
                  out_specs=[row, pl.BlockSpec((1, 1), lambda i: (0, 0))],
                  out_shape=[jax.ShapeDtypeStruct((S, D), F32), jax.ShapeDtypeStruct((1, 1), F32)],
                  compiler_params=_params(("arbitrary",)))(h, tgt)


def _gate_fwd(zt, bf, name):
    H, S = zt.shape
    nb = S // 128

    def body(z_ref, b_ref, c_ref):
        z = z_ref[...] + b_ref[...]
        lf = jnp.minimum(z, 0.0) - jnp.log(1.0 + jnp.exp(-jnp.abs(z)))
        upper = (lax.broadcasted_iota(jnp.int32, (128, 128), 0) <= lax.broadcasted_iota(jnp.int32, (128, 128), 1)).astype(F32)
        carry = jnp.zeros((H, 1), F32)
        for blk in range(nb):
            cs = _dot(lf[:, blk * 128:(blk + 1) * 128], upper, NN, precision=lax.Precision.HIGHEST) + carry
            c_ref[:, blk * 128:(blk + 1) * 128] = cs
            carry = cs[:, 127:128]

    return _pcall(body, name=name, in_specs=[VMEM_SPEC, VMEM_SPEC], out_specs=VMEM_SPEC,
                  out_shape=jax.ShapeDtypeStruct((H, S), F32))(zt, bf)


def _gate_bwd(dct, zt, bf, name):
    H, S = zt.shape
    nb = S // 128

    def body(dc_ref, z_ref, b_ref, dz_ref, db_ref):
        z = z_ref[...] + b_ref[...]
        e = jnp.exp(-jnp.abs(z))
        sig_neg = jnp.where(z >= 0, e, 1.0) / (1.0 + e)
        lower = (lax.broadcasted_iota(jnp.int32, (128, 128), 0) >= lax.broadcasted_iota(jnp.int32, (128, 128), 1)).astype(F32)
        dc = dc_ref[...]
        carry = jnp.zeros((H, 1), F32)
        db = jnp.zeros((H, 1), F32)
        for blk in reversed(range(nb)):
            sl = slice(blk * 128, (blk + 1) * 128)
            dlf = _dot(dc[:, sl], lower, NN, precision=lax.Precision.HIGHEST) + carry
            carry = dlf[:, 0:1]
            dz = dlf * sig_neg[:, sl]
            dz_ref[:, sl] = dz
            db = db + jnp.sum(dz, axis=1, keepdims=True)
        db_ref[...] = db

    return _pcall(body, name=name, in_specs=[VMEM_SPEC] * 3, out_specs=[VMEM_SPEC] * 2,
                  out_shape=[jax.ShapeDtypeStruct((H, S), F32), jax.ShapeDtypeStruct((H, 1), F32)])(dct, zt, bf)


def _lane_is_a():
    return lax.broadcasted_iota(jnp.int32, (1, LANES), 1) < HEAD_DIM


def _per_head_mean(x, is_a):
    sa = jnp.sum(jnp.where(is_a, x, 0.0), axis=-1, keepdims=True)
    sb = jnp.sum(jnp.where(is_a, 0.0, x), axis=-1, keepdims=True)
    return jnp.where(is_a, sa, sb) / HEAD_DIM


def _pair_norm(raw, gain, is_a):
    return raw * lax.rsqrt(_per_head_mean(raw * raw, is_a) + NORM_EPS) * gain


def _pair_norm_bwd(raw, gain, dnormed, is_a):
    r = lax.rsqrt(_per_head_mean(raw * raw, is_a) + NORM_EPS)
    xh = raw * r
    dgain = jnp.sum(dnormed * xh, axis=0, keepdims=True)
    dxh = dnormed * gain
    return r * (dxh - xh * _per_head_mean(dxh * xh, is_a)), dgain


def _fold_heads(x):
    i = lax.broadcasted_iota(jnp.int32, (LANES, LANES), 0)
    j = lax.broadcasted_iota(jnp.int32, (LANES, LANES), 1)
    fold = ((i == j) | (i == j + HEAD_DIM) | (i + HEAD_DIM == j)).astype(F32)
    return _dot(x, fold, NN, precision=lax.Precision.HIGHEST)


def _fold_row(ref):
    ref[...] = _fold_heads(jnp.broadcast_to(ref[...], (8, LANES)))[0:1, :]


def _tri_mask(t, keys_on_rows):
    r = lax.broadcasted_iota(jnp.int32, (t, t), 0)
    c = lax.broadcasted_iota(jnp.int32, (t, t), 1)
    return (r <= c) if keys_on_rows else (r >= c)


def _fox_fwd(proj, c_col, c_row, gq2, gk2, n_heads, name, t=256):
    S = proj.shape[0]
    H = n_heads
    P = H // 2
    t = min(t, S)
    nq = S // t

    def body(q_ref, k_ref, v_ref, cc_ref, cr_ref, gq_ref, gk_ref, o_ref, lse_ref, qs_s, kb_s, vb_s):
        is_a = _lane_is_a()
        qn = _pair_norm(q_ref[...], gq_ref[...], is_a) * 0.125
        qs_s[0] = jnp.where(is_a, qn, 0.0).astype(BF16)
        qs_s[1] = jnp.where(is_a, 0.0, qn).astype(BF16)
        kb_s[...] = _pair_norm(k_ref[...], gk_ref[...], is_a).astype(BF16)
        vb_s[...] = v_ref[...].astype(BF16)
        causal = _tri_mask(t, False)
        for i in range(nq):
            t0 = i * t
            rows = slice(t0, t0 + t)
            o_pair = None
            for a in range(2):
                qi = qs_s[a, rows, :]
                ci = cc_ref[a, rows, :]
                s_d = jnp.where(causal, _dot(qi, kb_s[rows, :], NT) + ci - cr_ref[a, :, rows], NEG)
                m = jnp.max(s_d, axis=-1, keepdims=True)
                if i > 0:
                    s_l = _dot(qi, kb_s[0:t0, :], NT) + ci - cr_ref[a, :, 0:t0]
                    m = jnp.maximum(m, jnp.max(s_l, axis=-1, keepdims=True))
                p_d = jnp.exp(s_d - m)
                l = jnp.sum(p_d, axis=-1, keepdims=True)
                acc = _dot(p_d.astype(BF16), vb_s[rows, :], NN)
                if i > 0:
                    p_l = jnp.exp(s_l - m)
                    l = l + jnp.sum(p_l, axis=-1, keepdims=True)
                    acc = acc + _dot(p_l.astype(BF16), vb_s[0:t0, :], NN)
                o_a = acc / l
                lse_ref[a, rows, :] = m + jnp.log(l)
                o_pair = o_a if a == 0 else jnp.where(is_a, o_pair, o_a)
            o_ref[rows, :] = o_pair.astype(BF16)

    def cols(off):
        return pl.BlockSpec((S, LANES), lambda p: (0, off + p))

    col = pl.BlockSpec((2, S, 1), lambda p: (p, 0, 0))
    rowv = pl.BlockSpec((2, 1, S), lambda p: (p, 0, 0))
    gain = pl.BlockSpec((1, LANES), lambda p: (0, 0))
    return _pcall(body, name=name, grid=(P,), in_specs=[cols(0), cols(P), cols(2 * P), col, rowv, gain, gain],
                  out_specs=[cols(0), col],
                  out_shape=[jax.ShapeDtypeStruct((S, H * HEAD_DIM), BF16), jax.ShapeDtypeStruct((H, S, 1), F32)],
                  scratch_shapes=[pltpu.VMEM((2, S, LANES), BF16), pltpu.VMEM((S, LANES), BF16), pltpu.VMEM((S, LANES), BF16)],
                  compiler_params=_params(("parallel",)))(proj, proj, proj, c_col, c_row, gq2, gk2)


def _fox_bwd(proj, c_col, c_row, gq2, gk2, lse_row, do, n_heads, name, t=256):
    S = proj.shape[0]
    H = n_heads
    P = H // 2
    t = min(t, S)
    nq = S // t
    assert t % LANES == 0

    def body(q_ref, k_ref, v_ref, cc_ref, cr_ref, gq_ref, gk_ref, lr_ref, do_ref,
             dq_ref, dk_ref, dv_ref, dc_ref, dgq_ref, dgk_ref,
             qs_s, kb_s, kt_s, vb_s, dob_s, dq_s, dk_s, dv_s, dcs_s):
        is_a = _lane_is_a()
        qn = _pair_norm(q_ref[...], gq_ref[...], is_a) * 0.125
        qs_s[0] = jnp.where(is_a, qn, 0.0).astype(BF16)
        qs_s[1] = jnp.where(is_a, 0.0, qn).astype(BF16)
        kn = _pair_norm(k_ref[...], gk_ref[...], is_a)
        kb_s[...] = kn.astype(BF16)
        kt_s[0] = jnp.where(is_a, kn, 0.0).T.astype(BF16)
        kt_s[1] = jnp.where(is_a, 0.0, kn).T.astype(BF16)
        vb_s[...] = v_ref[...].astype(BF16)
        dov = do_ref[...]
        dob_s[0] = jnp.where(is_a, dov, 0.0).astype(BF16)
        dob_s[1] = jnp.where(is_a, 0.0, dov).astype(BF16)
        dk_s[...] = jnp.zeros((S, LANES), F32)
        dv_s[...] = jnp.zeros((S, LANES), F32)
        dcs_s[...] = jnp.zeros((2, S, LANES), F32)
        causal = _tri_mask(t, True)
        for i in range(nq):
            t0 = i * t
            rows = slice(t0, t0 + t)
            dq_t = jnp.zeros((LANES, t), F32)
            for a in range(2):
                qi = qs_s[a, rows, :]
                doi = dob_s[a, rows, :]
                cri = cr_ref[a, :, rows]
                lri = lr_ref[a, :, rows]

                def probs(keys, masked, a=a, qi=qi, doi=doi, cri=cri, lri=lri):
                    p_t = jnp.exp(_dot(kb_s[keys, :], qi, NT) + cri - cc_ref[a, keys, :] - lri)
                    if masked:
                        p_t = jnp.where(causal, p_t, 0.0)
                    return p_t, _dot(vb_s[keys, :], doi, NT)

                parts = [(rows,) + probs(rows, True)]
                if i > 0:
                    parts.append((slice(0, t0),) + probs(slice(0, t0), False))
                delta = sum(jnp.sum(p_t * dp_t, axis=0, keepdims=True) for _, p_t, dp_t in parts)
                for keys, p_t, dp_t in parts:
                    ds_t = p_t * (dp_t - delta)
                    dsb = ds_t.astype(BF16)
                    dv_s[keys, :] += _dot(p_t.astype(BF16), doi, NN)
                    dk_s[keys, :] += _dot(dsb, qi, NN)
                    dq_t = dq_t + _dot(kt_s[a, :, keys], dsb, NN)
                    dcs_s[a, keys, :] += sum(ds_t[:, b * LANES:(b + 1) * LANES] for b in range(t // LANES))
            dq_s[rows, :] = dq_t.T
        first = pl.program_id(0) == 0
        last = pl.program_id(0) == P - 1
        dq_raw, dgq = _pair_norm_bwd(q_ref[...], gq_ref[...], dq_s[...] * 0.125, is_a)
        dq_ref[...] = dq_raw.astype(BF16)
        _accumulate(dgq_ref, dgq, first)
        dk_raw, dgk = _pair_norm_bwd(k_ref[...], gk_ref[...], dk_s[...], is_a)
        dk_ref[...] = dk_raw.astype(BF16)
        _accumulate(dgk_ref, dgk, first)
        dv_ref[...] = dv_s[...].astype(BF16)
        for a in range(2):
            dc_ref[a] = -jnp.sum(dcs_s[a], axis=1, keepdims=True)

        @pl.when(last)
        def _():
            _fold_row(dgq_ref)
            _fold_row(dgk_ref)

    def cols(off):
        return pl.BlockSpec((S, LANES), lambda p: (0, off + p))

    col = pl.BlockSpec((2, S, 1), lambda p: (p, 0, 0))
    rowv = pl.BlockSpec((2, 1, S), lambda p: (p, 0, 0))
    gain = pl.BlockSpec((1, LANES), lambda p: (0, 0))
    wide = jax.ShapeDtypeStruct((S, H * HEAD_DIM), BF16)
    gs = jax.ShapeDtypeStruct((1, LANES), F32)
    return _pcall(body, name=name, grid=(P,),
                  in_specs=[cols(0), cols(P), cols(2 * P), col, rowv, gain, gain, rowv, cols(0)],
                  out_specs=[cols(0), cols(0), cols(0), col, gain, gain],
                  out_shape=[wide, wide, wide, jax.ShapeDtypeStruct((H, S, 1), F32), gs, gs],
                  scratch_shapes=[pltpu.VMEM((2, S, LANES), BF16), pltpu.VMEM((S, LANES), BF16), pltpu.VMEM((2, LANES, S), BF16),
                                  pltpu.VMEM((S, LANES), BF16), pltpu.VMEM((2, S, LANES), BF16)]
                  + [pltpu.VMEM((S, LANES), F32)] * 3 + [pltpu.VMEM((2, S, LANES), F32)],
                  compiler_params=_params(("arbitrary",)))(proj, proj, proj, c_col, c_row, gq2, gk2, lse_row, do)


def _bucket_onehot():
    W = WINDOW
    dist = np.arange(W)[:, None] + W - np.arange(2 * W)[None, :]
    n = np.maximum(dist, 0)
    max_exact = N_BUCKETS // 2
    large = max_exact + (np.log(np.maximum(n, 1) / max_exact) / np.log(REL_MAX_DIST / max_exact)
                         * (N_BUCKETS - max_exact)).astype(np.int32)
    large = np.minimum(large, N_BUCKETS - 1)
    bucket = np.where(n < max_exact, n, large).astype(np.int32)
    valid = (dist >= 0) & (dist < W)
    onehot = (bucket[None] == np.arange(N_BUCKETS)[:, None, None]) & valid[None]
    return onehot.reshape(N_BUCKETS, W * 2 * W).astype(np.float32)


def _bias_expand(rel_bias_t, onehot, name, tn=4096):
    HQ, NB = rel_bias_t.shape
    L = onehot.shape[1]

    def body(r_ref, oh_ref, out_ref):
        out_ref[...] = _dot(r_ref[...], oh_ref[...].astype(F32), NN, precision=lax.Precision.HIGHEST)

    return _pcall(body, name=name, grid=(L // tn,),
                  in_specs=[pl.BlockSpec((HQ, NB), lambda i: (0, 0)), pl.BlockSpec((NB, tn), lambda i: (0, i))],
                  out_specs=pl.BlockSpec((HQ, tn), lambda i: (0, i)),
                  out_shape=jax.ShapeDtypeStruct((HQ, L), F32), compiler_params=_params(("parallel",)))(rel_bias_t, onehot)


def _bias_reduce(dbias, onehot, name, tk=4096):
    HQ, L = dbias.shape
    NB = onehot.shape[0]

    def body(d_ref, oh_ref, out_ref):
        part = _dot(d_ref[...], oh_ref[...].astype(F32), NT, precision=lax.Precision.HIGHEST)
        _accumulate(out_ref, part, pl.program_id(0) == 0)

    return _pcall(body, name=name, grid=(L // tk,),
                  in_specs=[pl.BlockSpec((HQ, tk), lambda i: (0, i)), pl.BlockSpec((NB, tk), lambda i: (0, i))],
                  out_specs=pl.BlockSpec((HQ, NB), lambda i: (0, 0)),
                  out_shape=jax.ShapeDtypeStruct((HQ, NB), F32), compiler_params=_params(("arbitrary",)))(dbias, onehot)


def _stacked_query_index(n_rows_or_cols_axis, shape):
    idx = lax.broadcasted_iota(jnp.int32, shape, n_rows_or_cols_axis)
    return jnp.where(idx >= WINDOW, idx - WINDOW, idx)


def _swa_fwd(qproj, kk, vv, bias_ab, sink_col, gq2, gk2, name):
    S, HQD = qproj.shape
    KVH = kk.shape[0]
    PP = HQD // LANES
    NP = PP // KVH
    W = WINDOW
    nb = S // W

    def body(q_ref, k_ref, v_ref, bias_ref, sink_ref, gq_ref, gk_ref, o_ref, lse_ref, qs_s, kb_s, vb_s):
        is_a = _lane_is_a()
        qn = _pair_norm(q_ref[...], gq_ref[...], is_a) * 0.125
        qs_s[0] = jnp.where(is_a, qn, 0.0).astype(BF16)
        qs_s[1] = jnp.where(is_a, 0.0, qn).astype(BF16)
        kb_s[...] = _pair_norm(k_ref[...], gk_ref[...], is_a).astype(BF16)
        vb_s[...] = v_ref[...].astype(BF16)
        sink = sink_ref[...]
        qi1 = _stacked_query_index(0, (2 * W, W))
        first_valid = lax.broadcasted_iota(jnp.int32, (2 * W, W), 1) <= qi1
        qi2 = _stacked_query_index(0, (2 * W, 2 * W))
        key2 = lax.broadcasted_iota(jnp.int32, (2 * W, 2 * W), 1)
        band_valid = (key2 > qi2) & (key2 <= qi2 + W)
        for n in range(nb):
            rows = slice(n * W, (n + 1) * W)
            keys = slice(0, W) if n == 0 else slice((n - 1) * W, (n + 1) * W)
            lhs = jnp.concatenate([qs_s[0, rows, :], qs_s[1, rows, :]], axis=0)
            s = _dot(lhs, kb_s[keys, :], NT) + (bias_ref[:, W:2 * W] if n == 0 else bias_ref[...])
            s = jnp.where(first_valid if n == 0 else band_valid, s, NEG)
            m = jnp.maximum(jnp.max(s, axis=-1, keepdims=True), sink)
            e = jnp.exp(s - m)
            l = jnp.sum(e, axis=-1, keepdims=True) + jnp.exp(sink - m)
            o_ab = _dot(e.astype(BF16), vb_s[keys, :], NN) / l
            o_ref[rows, :] = jnp.where(is_a, o_ab[0:W, :], o_ab[W:2 * W, :]).astype(BF16)
            lse_ref[n] = m + jnp.log(l)

    qcols = pl.BlockSpec((S, LANES), lambda a, g: (0, a * NP + g))
    kvs = pl.BlockSpec((None, S, LANES), lambda a, g: (a, 0, 0))
    gain = pl.BlockSpec((1, LANES), lambda a, g: (0, 0))
    return _pcall(body, name=name, grid=(KVH, NP),
                  in_specs=[qcols, kvs, kvs, pl.BlockSpec((None, 2 * W, 2 * W), lambda a, g: (a * NP + g, 0, 0)),
                            pl.BlockSpec((None, 2 * W, 1), lambda a, g: (a * NP + g, 0, 0)), gain, gain],
                  out_specs=[qcols, pl.BlockSpec((None, nb, 2 * W, 1), lambda a, g: (a * NP + g, 0, 0, 0))],
                  out_shape=[jax.ShapeDtypeStruct((S, HQD), BF16), jax.ShapeDtypeStruct((PP, nb, 2 * W, 1), F32)],
                  scratch_shapes=[pltpu.VMEM((2, S, LANES), BF16), pltpu.VMEM((S, LANES), BF16), pltpu.VMEM((S, LANES), BF16)],
                  compiler_params=_params(("parallel", "parallel")))(qproj, kk, vv, bias_ab, sink_col, gq2, gk2)


def _swa_bwd(qproj, kk, vv, bias_t_ab, sink_row, gq2, gk2, lse_row, do, name):
    S, HQD = qproj.shape
    KVH = kk.shape[0]
    PP = HQD // LANES
    NP = PP // KVH
    W = WINDOW
    nb = S // W

    def body(q_ref, k_ref, v_ref, bias_ref, sink_ref, gq_ref, gk_ref, lr_ref, do_ref,
             dq_ref, dk_ref, dv_ref, db_ref, dsink_ref, dgq_ref, dgk_ref,
             qs_s, kb_s, kt_s, vb_s, dob_s, dq_s, dk_s, dv_s):
        a, g = pl.program_id(0), pl.program_id(1)
        is_a = _lane_is_a()
        qn = _pair_norm(q_ref[...], gq_ref[...], is_a) * 0.125
        qs_s[0] = jnp.where(is_a, qn, 0.0).astype(BF16)
        qs_s[1] = jnp.where(is_a, 0.0, qn).astype(BF16)
        kn = _pair_norm(k_ref[...], gk_ref[...], is_a)
        kb_s[...] = kn.astype(BF16)
        kt_s[...] = kn.T.astype(BF16)
        vb_s[...] = v_ref[...].astype(BF16)
        dov = do_ref[...]
        dob_s[0] = jnp.where(is_a, dov, 0.0).astype(BF16)
        dob_s[1] = jnp.where(is_a, 0.0, dov).astype(BF16)
        sink = sink_ref[...]

        @pl.when(g == 0)
        def _():
            dk_s[...] = jnp.zeros((S, LANES), F32)
            dv_s[...] = jnp.zeros((S, LANES), F32)

        qi1 = _stacked_query_index(1, (W, 2 * W))
        first_valid = lax.broadcasted_iota(jnp.int32, (W, 2 * W), 0) <= qi1
        qi2 = _stacked_query_index(1, (2 * W, 2 * W))
        key2 = lax.broadcasted_iota(jnp.int32, (2 * W, 2 * W), 0)
        band_valid = (key2 > qi2) & (key2 <= qi2 + W)
        head_rows = lax.broadcasted_iota(jnp.int32, (LANES, W), 0) < HEAD_DIM
        db = jnp.zeros((2 * W, 2 * W), F32)
        dsk = jnp.zeros((1, 2 * W), F32)
        pend_k = pend_v = None
        for n in range(nb):
            rows = slice(n * W, (n + 1) * W)
            keys = slice(0, W) if n == 0 else slice((n - 1) * W, (n + 1) * W)
            lhs_q = jnp.concatenate([qs_s[0, rows, :], qs_s[1, rows, :]], axis=0)
            lhs_do = jnp.concatenate([dob_s[0, rows, :], dob_s[1, rows, :]], axis=0)
            lse = lr_ref[n]
            s_t = _dot(kb_s[keys, :], lhs_q, NT) + (bias_ref[W:2 * W, :] if n == 0 else bias_ref[...])
            p_t = jnp.where(first_valid if n == 0 else band_valid, jnp.exp(s_t - lse), 0.0)
            dp_t = _dot(vb_s[keys, :], lhs_do, NT)
            delta = jnp.sum(p_t * dp_t, axis=0, keepdims=True)
            ds_t = p_t * (dp_t - delta)
            dsb = ds_t.astype(BF16)
            dsk = dsk - jnp.exp(sink - lse) * delta
            dv_band = _dot(p_t.astype(BF16), lhs_do, NN)
            dk_band = _dot(dsb, lhs_q, NN)
            dq_t = _dot(kt_s[:, keys], dsb, NN)
            dq_s[rows, :] = jnp.where(head_rows, dq_t[:, 0:W], dq_t[:, W:2 * W]).T
            if n == 0:
                db = jnp.concatenate([jnp.zeros((W, 2 * W), F32), ds_t], axis=0)
                pend_k, pend_v = dk_band, dv_band
            else:
                db = db + ds_t
                prev = slice((n - 1) * W, n * W)
                dk_s[prev, :] += pend_k + dk_band[0:W, :]
                dv_s[prev, :] += pend_v + dv_band[0:W, :]
                pend_k, pend_v = dk_band[W:2 * W, :], dv_band[W:2 * W, :]
        tail = slice((nb - 1) * W, nb * W)
        dk_s[tail, :] += pend_k
        dv_s[tail, :] += pend_v
        db_ref[...] = db
        dsink_ref[0] = jnp.broadcast_to(jnp.sum(dsk[:, 0:W], axis=1, keepdims=True), (1, LANES))
        dsink_ref[1] = jnp.broadcast_to(jnp.sum(dsk[:, W:2 * W], axis=1, keepdims=True), (1, LANES))
        dq_raw, dgq = _pair_norm_bwd(q_ref[...], gq_ref[...], dq_s[...] * 0.125, is_a)
        dq_ref[...] = dq_raw.astype(BF16)
        _accumulate(dgq_ref, dgq, jnp.logical_and(a == 0, g == 0))

        @pl.when(jnp.logical_and(a == KVH - 1, g == NP - 1))
        def _():
            _fold_row(dgq_ref)

        @pl.when(g == NP - 1)
        def _():
            dk_raw, dgk = _pair_norm_bwd(k_ref[...], gk_ref[...], _fold_heads(dk_s[...]), is_a)
            dk_ref[...] = dk_raw
            _accumulate(dgk_ref, dgk, a == 0)
            dv_ref[...] = _fold_heads(dv_s[...])

    qcols = pl.BlockSpec((S, LANES), lambda a, g: (0, a * NP + g))
    kvs = pl.BlockSpec((None, S, LANES), lambda a, g: (a, 0, 0))
    sq = pl.BlockSpec((None, 2 * W, 2 * W), lambda a, g: (a * NP + g, 0, 0))
    gain = pl.BlockSpec((1, LANES), lambda a, g: (0, 0))
    ks = jax.ShapeDtypeStruct((KVH, S, LANES), F32)
    gs = jax.ShapeDtypeStruct((1, LANES), F32)
    return _pcall(body, name=name, grid=(KVH, NP),
                  in_specs=[qcols, kvs, kvs, sq, pl.BlockSpec((None, 1, 2 * W), lambda a, g: (a * NP + g, 0, 0)), gain, gain,
                            pl.BlockSpec((None, nb, 1, 2 * W), lambda a, g: (a * NP + g, 0, 0, 0)), qcols],
                  out_specs=[qcols, kvs, kvs, sq, pl.BlockSpec((2, 1, LANES), lambda a, g: (a * NP + g, 0, 0)), gain, gain],
                  out_shape=[jax.ShapeDtypeStruct((S, HQD), BF16), ks, ks, jax.ShapeDtypeStruct((PP, 2 * W, 2 * W), F32),
                             jax.ShapeDtypeStruct((2 * PP, 1, LANES), F32), gs, gs],
                  scratch_shapes=[pltpu.VMEM((2, S, LANES), BF16), pltpu.VMEM((S, LANES), BF16), pltpu.VMEM((LANES, S), BF16),
                                  pltpu.VMEM((S, LANES), BF16), pltpu.VMEM((2, S, LANES), BF16)] + [pltpu.VMEM((S, LANES), F32)] * 3,
                  compiler_params=_params(("arbitrary", "arbitrary")))(qproj, kk, vv, bias_t_ab, sink_row, gq2, gk2, lse_row, do)


def _adamw(w, g, m, v, name, tr=256):
    R, C = w.shape
    tr = min(tr, R)
    assert R % tr == 0

    def body(w_ref, g_ref, m_ref, v_ref, d_ref, m2_ref, v2_ref):
        gv = g_ref[...]
        m2 = ADAM_B1 * m_ref[...] + (1.0 - ADAM_B1) * gv
        v2 = ADAM_B2 * v_ref[...] + (1.0 - ADAM_B2) * jnp.square(gv)
        m_hat = m2 / (1.0 - ADAM_B1 ** ADAM_STEP)
        v_hat = v2 / (1.0 - ADAM_B2 ** ADAM_STEP)
        d_ref[...] = -ADAM_LR * (m_hat / (jnp.sqrt(v_hat) + ADAM_EPS) + ADAM_WD * w_ref[...])
        m2_ref[...] = m2
        v2_ref[...] = v2

    blk = pl.BlockSpec((tr, C), lambda i: (i, 0))
    return _pcall(body, name=name, grid=(R // tr,), in_specs=[blk] * 4, out_specs=[blk] * 3,
                  out_shape=[jax.ShapeDtypeStruct((R, C), F32)] * 3, compiler_params=_params(("parallel",)))(w, g, m, v)


def _sum_core_pair(arr, got, place, name, tr=128):
    P, hr, C = got.shape
    assert hr % tr == 0
    nt = hr // tr

    def body(place_ref, a_ref, g_ref, o_ref):
        o_ref[...] = (a_ref[...].astype(F32) + g_ref[...].astype(F32)).astype(o_ref.dtype)

    spec = pltpu.PrefetchScalarGridSpec(
        num_scalar_prefetch=1, grid=(P, nt),
        in_specs=[pl.BlockSpec((None, tr, C), lambda j, i, pr: (j, pr[1] * nt + i, 0)),
                  pl.BlockSpec((None, tr, C), lambda j, i, pr: (j, i, 0))],
        out_specs=pl.BlockSpec((None, tr, C), lambda j, i, pr: (j, i, 0)))
    return _pcall(body, name=name, grid_spec=spec, out_shape=jax.ShapeDtypeStruct(got.shape, BF16),
                  compiler_params=_params(("parallel", "parallel")))(place, arr, got)


def _sum_chips(pair, landed, place, name, tr=128):
    _, R, C = landed.shape
    assert R % tr == 0

    def body(place_ref, p_ref, l_ref, o_ref):
        acc = p_ref[...].astype(F32)
        for k in range(3):
            acc = acc + l_ref[k].astype(F32)
        o_ref[...] = acc

    spec = pltpu.PrefetchScalarGridSpec(
        num_scalar_prefetch=1, grid=(R // tr,),
        in_specs=[pl.BlockSpec((None, tr, C), lambda i, pr: (pr[0], i, 0)), pl.BlockSpec((3, tr, C), lambda i, pr: (0, i, 0))],
        out_specs=pl.BlockSpec((None, tr, C), lambda i, pr: (pr[1], i, 0)))
    return _pcall(body, name=name, grid_spec=spec, out_shape=jax.ShapeDtypeStruct((2, R, C), F32),
                  compiler_params=_params(("parallel",)))(place, pair, landed)


def _sum_parts(parts, name, out_dtype, tr=128):
    P, R, C = parts.shape
    tr = min(tr, R)
    assert R % tr == 0, (R, tr)

    def body(p_ref, o_ref):
        acc = p_ref[0].astype(F32)
        for k in range(1, P):
            acc = acc + p_ref[k].astype(F32)
        o_ref[...] = acc.astype(o_ref.dtype)

    return _pcall(body, name=name, grid=(R // tr,), in_specs=[pl.BlockSpec((P, tr, C), lambda i: (0, i, 0))],
                  out_specs=pl.BlockSpec((tr, C), lambda i: (i, 0)),
                  out_shape=jax.ShapeDtypeStruct((R, C), out_dtype), compiler_params=_params(("parallel",)))(parts)


def _place():
    x, y, c = lax.axis_index("x"), lax.axis_index("y"), lax.axis_index("c")
    others = [(1 - x, y), (x, 1 - y), (1 - x, 1 - y)]
    return x, y, c, others


def _half_rows(ref, hh, lead=()):
    hr = ref.shape[-2] // 2
    return ref.at[(*lead, pl.ds(pl.multiple_of(hh * hr, 16), hr), slice(None))]


def _sem_arrays(*counts):
    return [pltpu.SemaphoreType.DMA((k,)) for k in counts]


SEM_SPEC = pl.BlockSpec(memory_space=pltpu.SEMAPHORE)
ANY_SPEC = pl.BlockSpec(memory_space=pl.ANY)
DATAFLOW = pltpu.SideEffectType.DATAFLOW_SIDE_EFFECTING


def _in_hbm(a):
    return pltpu.with_memory_space_constraint(a, pltpu.HBM)


def _gather_copies(srcs, lands, send_sems, recv_sems):
    x, y, c, others = _place()
    me = 2 * x + y

    def copy(w, k, dst_chip, to):
        return pltpu.make_async_remote_copy(src_ref=_half_rows(srcs[w], c), dst_ref=_half_rows(lands[w], c, (dst_chip,)),
                                            send_sem=send_sems.at[3 * w + k], recv_sem=recv_sems.at[3 * w + k],
                                            device_id=to, device_id_type=MESH)

    pairs = [(w, k, cx, cy) for w in range(len(srcs)) for k, (cx, cy) in enumerate(others)]
    return ([copy(w, k, me, (cx, cy, c)) for w, k, cx, cy in pairs],
            [copy(w, k, 2 * cx + cy, (cx, cy, c)) for w, k, cx, cy in pairs])


def _gather_start(shards, name):
    n = len(shards)

    def body(*refs):
        srcs, lands, send_sems, recv_sems, token = refs[:n], refs[n:2 * n], refs[2 * n], refs[2 * n + 1], refs[-1]
        for cp in _gather_copies(srcs, lands, send_sems, recv_sems)[0]:
            cp.start()
        token[...] = jnp.zeros_like(token)

    lands = [lax.empty((N_CHIPS,) + s.shape, s.dtype) for s in shards]
    outs = _pcall(
        body, name=name, in_specs=[HBM_SPEC] * (2 * n),
        out_specs=[SEM_SPEC, SEM_SPEC] + [HBM_SPEC] * (2 * n) + [VMEM_SPEC],
        out_shape=[pltpu.SemaphoreType.DMA((3 * n,)), pltpu.SemaphoreType.DMA((3 * n,))]
        + [pltpu.HBM(a.shape, a.dtype) for a in list(shards) + lands] + [jax.ShapeDtypeStruct((8, LANES), F32)],
        input_output_aliases={i: 2 + i for i in range(2 * n)},
        compiler_params=pltpu.CompilerParams(has_side_effects=DATAFLOW),
    )(*[_in_hbm(a) for a in list(shards) + lands])
    return outs[0], outs[1], outs[2:2 + n], outs[2 + n:2 + 2 * n], outs[-1]


def _gather_wait(started, after, name):
    send_sems, recv_sems, srcs, lands, _ = started
    n = len(srcs)

    def body(*refs):
        src_refs, land_refs, send_ref, recv_ref = refs[:n], refs[n:2 * n], refs[2 * n], refs[2 * n + 1]
        outgoing, incoming = _gather_copies(src_refs, land_refs, send_ref, recv_ref)
        for out_cp, in_cp in zip(outgoing, incoming):
            out_cp.wait_send()
            in_cp.wait_recv()

    outs = _pcall(
        body, name=name, in_specs=[HBM_SPEC] * (2 * n) + [SEM_SPEC, SEM_SPEC, ANY_SPEC], out_specs=[HBM_SPEC] * (2 * n),
        out_shape=[pltpu.HBM(a.shape, a.dtype) for a in list(srcs) + list(lands)],
        input_output_aliases={i: i for i in range(2 * n)},
        compiler_params=pltpu.CompilerParams(has_side_effects=DATAFLOW),
    )(*srcs, *lands, send_sems, recv_sems, after)
    return outs[:n], outs[n:]


def _gather_pass_on(shards, lands, name):
    n = len(shards)
    per = 4

    def body(*refs):
        srcs, bufs = refs[:n], refs[2 * n:3 * n]
        send_sems, recv_sems = refs[3 * n:]
        x, y, c, others = _place()
        me = 2 * x + y
        sibling = (x, y, 1 - c)

        def copy(w, k, src, dst):
            return pltpu.make_async_remote_copy(src_ref=src, dst_ref=dst, send_sem=send_sems.at[per * w + k],
                                                recv_sem=recv_sems.at[per * w + k], device_id=sibling, device_id_type=MESH)

        sends, recvs = [], []
        for w in range(n):
            for k, (cx, cy) in enumerate(others):
                mine, theirs = _half_rows(bufs[w], c, (2 * cx + cy,)), _half_rows(bufs[w], 1 - c, (2 * cx + cy,))
                sends.append(copy(w, k, mine, mine))
                recvs.append(copy(w, k, theirs, theirs))
            sends.append(copy(w, 3, srcs[w], bufs[w].at[me]))
            recvs.append(sends[-1])
        for cp in sends:
            cp.start()
        for snd, rcv in zip(sends, recvs):
            snd.wait_send()
            rcv.wait_recv()

    return _pcall(body, name=name, in_specs=[HBM_SPEC] * (2 * n), out_specs=[HBM_SPEC] * n,
                  out_shape=[jax.ShapeDtypeStruct(l.shape, l.dtype) for l in lands],
                  input_output_aliases={n + w: w for w in range(n)},
                  scratch_shapes=_sem_arrays(per * n, per * n))(*shards, *lands)


def _allgather_group(shards, name):
    n = len(shards)
    per = 7

    def body(*refs):
        ins, outs = refs[:n], refs[n:2 * n]
        send_sems, recv_sems = refs[2 * n:]
        x, y, c, others = _place()
        me = 2 * x + y
        sibling = (x, y, 1 - c)

        def copy(w, k, src, dst, to):
            return pltpu.make_async_remote_copy(src_ref=src, dst_ref=dst, send_sem=send_sems.at[per * w + k],
                                                recv_sem=recv_sems.at[per * w + k], device_id=to, device_id_type=MESH)

        first = [copy(w, k, _half_rows(ins[w], c), _half_rows(outs[w], c, (me,)), (cx, cy, c))
                 for w in range(n) for k, (cx, cy) in enumerate(others)]
        own = [copy(w, 6, ins[w], outs[w].at[me], sibling) for w in range(n)]
        for cp in first + own:
            cp.start()
        passed = []
        for w in range(n):
            for k, (cx, cy) in enumerate(others):
                landed = _half_rows(outs[w], c, (2 * cx + cy,))
                copy(w, k, landed, landed, sibling).wait_recv()
                passed.append(copy(w, 3 + k, landed, landed, sibling))
                passed[-1].start()
        for w in range(n):
            for k, (cx, cy) in enumerate(others):
                theirs = _half_rows(outs[w], 1 - c, (2 * cx + cy,))
                copy(w, 3 + k, theirs, theirs, sibling).wait_recv()
            own[w].wait_recv()
        for cp in first + passed + own:
            cp.wait_send()

    return _pcall(body, name=name, in_specs=[HBM_SPEC] * n, out_specs=[HBM_SPEC] * n,
                  out_shape=[jax.ShapeDtypeStruct((N_CHIPS,) + s.shape, s.dtype) for s in shards],
                  scratch_shapes=_sem_arrays(per * n, per * n))(*shards)


def _swap_halves_group(arrs, name):
    n = len(arrs)

    def body(*refs):
        ins, gots = refs[:n], refs[n:2 * n]
        send_sems, recv_sems = refs[2 * n:]
        x, y, c, _ = _place()
        swaps = [pltpu.make_async_remote_copy(src_ref=_half_rows(ins[w], 1 - c, (slice(None),)), dst_ref=gots[w],
                                              send_sem=send_sems.at[w], recv_sem=recv_sems.at[w],
                                              device_id=(x, y, 1 - c), device_id_type=MESH) for w in range(n)]
        for cp in swaps:
            cp.start()
        for cp in swaps:
            cp.wait()

    half_shapes = [jax.ShapeDtypeStruct((a.shape[0], a.shape[1] // 2, a.shape[2]), a.dtype) for a in arrs]
    return _pcall(body, name=name, in_specs=[HBM_SPEC] * n, out_specs=[HBM_SPEC] * n, out_shape=half_shapes,
                  scratch_shapes=_sem_arrays(n, n))(*arrs)


def _scatter_group(parts, name):
    n = len(parts)

    def body(*refs):
        ins, outs = refs[:n], refs[n:2 * n]
        send_sems, recv_sems = refs[2 * n:]
        x, y, c, others = _place()

        def copy(w, k, src_chip, to):
            return pltpu.make_async_remote_copy(src_ref=ins[w].at[src_chip], dst_ref=outs[w].at[k],
                                                send_sem=send_sems.at[3 * w + k], recv_sem=recv_sems.at[3 * w + k],
                                                device_id=to, device_id_type=MESH)

        sends = [copy(w, k, 2 * cx + cy, (cx, cy, c)) for w in range(n) for k, (cx, cy) in enumerate(others)]
        for cp in sends:
            cp.start()
        for cp in sends:
            cp.wait()

    return _pcall(body, name=name, in_specs=[HBM_SPEC] * n, out_specs=[HBM_SPEC] * n,
                  out_shape=[jax.ShapeDtypeStruct((3,) + p.shape[1:], p.dtype) for p in parts],
                  scratch_shapes=_sem_arrays(3 * n, 3 * n))(*parts)


def _share_halves_group(halves, name):
    n = len(halves)

    def body(*refs):
        bufs = refs[n:2 * n]
        send_sems, recv_sems = refs[2 * n:]
        x, y, c, _ = _place()
        swaps = [pltpu.make_async_remote_copy(src_ref=bufs[w].at[c], dst_ref=bufs[w].at[c], send_sem=send_sems.at[w],
                                              recv_sem=recv_sems.at[w], device_id=(x, y, 1 - c), device_id_type=MESH)
                 for w in range(n)]
        for cp in swaps:
            cp.start()
        for w in range(n):
            swaps[w].wait_send()
            pltpu.make_async_remote_copy(src_ref=bufs[w].at[c], dst_ref=bufs[w].at[1 - c], send_sem=send_sems.at[w],
                                         recv_sem=recv_sems.at[w], device_id=(x, y, 1 - c), device_id_type=MESH).wait_recv()

    return _pcall(body, name=name, in_specs=[HBM_SPEC] * n, out_specs=[HBM_SPEC] * n,
                  out_shape=[jax.ShapeDtypeStruct(h.shape, h.dtype) for h in halves],
                  input_output_aliases={w: w for w in range(n)},
                  scratch_shapes=_sem_arrays(n, n))(*halves)


def _allgather_small(blk, name):
    M, C = blk.shape

    def body(x_ref, out_ref, send_sems, recv_sems, local_sem):
        x, y, c, others = _place()
        me, sibling = (x, y, c), (x, y, 1 - c)

        def rows(px, py, pc):
            return out_ref.at[4 * px + 2 * py + pc]

        def copy(k, block, to, src=None):
            return pltpu.make_async_remote_copy(src_ref=rows(*block) if src is None else src, dst_ref=rows(*block),
                                                send_sem=send_sems.at[k], recv_sem=recv_sems.at[k], device_id=to, device_id_type=MESH)

        mine = pltpu.make_async_copy(x_ref, rows(*me), local_sem)
        mine.start()
        first = [copy(0, me, sibling, src=x_ref)]
        first += [copy(1 + j, me, (*chip, c), src=x_ref) for j, chip in enumerate(others)]
        for cp in first:
            cp.start()
        passed = [copy(4 + j, (*chip, c), sibling) for j, chip in enumerate(others)]
        for j, chip in enumerate(others):
            copy(1 + j, (*chip, c), me).wait_recv()
            passed[j].start()
        copy(0, sibling, me).wait_recv()
        for j, chip in enumerate(others):
            copy(4 + j, (*chip, 1 - c), me).wait_recv()
        for cp in first + passed:
            cp.wait_send()
        mine.wait()

    return _pcall(body, name=name, in_specs=[VMEM_SPEC], out_specs=VMEM_SPEC,
                  out_shape=jax.ShapeDtypeStruct((8, M, C), blk.dtype),
                  scratch_shapes=[pltpu.SemaphoreType.DMA((7,)), pltpu.SemaphoreType.DMA((7,)), pltpu.SemaphoreType.DMA])(blk)


def _pack_rows(n_elems, width=PACK_W, align=PACK_ROW_ALIGN):
    rows = -(-n_elems // width)
    return -(-rows // align) * align


def _pack(arrays, dtype, width=PACK_W, align=PACK_ROW_ALIGN):
    flat = jnp.concatenate([a.astype(dtype).reshape(-1) for a in arrays])
    rows = _pack_rows(flat.shape[0], width, align)
    flat = jnp.pad(flat, (0, rows * width - flat.shape[0]))
    return flat.reshape(rows, width)


def _pack_small(arrays):
    return _pack(arrays, F32, width=128, align=8)


def _unpack(flat, shapes):
    out, off = [], 0
    for shp in shapes:
        n = int(np.prod(shp))
        out.append(flat[..., off:off + n].reshape(flat.shape[:-1] + tuple(shp)))
        off += n
    return out


def _doubled_heads(x2d, n_heads):
    S = x2d.shape[0]
    h = x2d.reshape(S, n_heads, HEAD_DIM).transpose(1, 0, 2)
    return jnp.concatenate([h, h], axis=-1)


def _mlp_fwd(h, g, w_up4, w_down, tag):
    (n,) = _rms_fwd(h, [g], f"rms_mlp{tag}")
    u, a = _matmul(n, w_up4, "nn", f"up{tag}", out_dtypes=(F32, BF16), chipwise="b",
                   epilogue=lambda acc: (acc, jnp.square(jnp.maximum(acc, 0.0))))
    h_out = _matmul(a, w_down, "nn", f"down{tag}", extras=(h,), epilogue=lambda acc, res: (res + acc,), tk=1024)
    return h_out, (n, u, a)


def _mlp_bwd(dh_out, h, g, w_up4, w_down, saved, tag):
    n, u, a = saved
    dw_down = _matmul(a, dh_out, "tn", f"dw_down{tag}", out_dtypes=(BF16,))
    du = _matmul(dh_out, w_down, "nt", f"du{tag}", out_dtypes=(BF16,), extras=(u,),
                 epilogue=lambda acc, uu: (acc * (2.0 * jnp.maximum(uu, 0.0)),))
    dw_up = _matmul(n, du, "tn", f"dw_up{tag}", out_dtypes=(BF16,), chipwise="out")
    dn = _matmul(du, w_up4, "nt", f"dn_mlp{tag}", tk=w_up4.shape[2], chipwise="b")
    dh, (dg,) = _rms_bwd(h, dh_out, [g], [dn], f"rms_mlp_bwd{tag}")
    return dh, dg, dw_up, dw_down


def kernel(x, g_attn, g_mlp, w_in_a, b_f, gq_a, gk_a, w_out_a, g_kv, w_kv, gk_b, w_q_b, gq_b, sinks, rel_bias, w_out_b, w_up, w_down, loss_target, m_g_attn, m_g_mlp, m_w_in_a, m_b_f, m_gq_a, m_gk_a, m_w_out_a, m_g_kv, m_w_kv, m_gk_b, m_w_q_b, m_gq_b, m_sinks, m_rel_bias, m_w_out_b, m_w_up, m_w_down, v_g_attn, v_g_mlp, v_w_in_a, v_b_f, v_gq_a, v_gk_a, v_w_out_a, v_g_kv, v_w_kv, v_gk_b, v_w_q_b, v_gq_b, v_sinks, v_rel_bias, v_w_out_b, v_w_up, v_w_down):
    given = dict(locals())
    S, D = x.shape[1], x.shape[2]
    H = D // HEAD_DIM
    KVH = w_kv.shape[1] // (2 * HEAD_DIM)
    kvw = KVH * HEAD_DIM
    hw = H * HEAD_DIM
    W = WINDOW
    nb = S // W
    c_idx = lax.axis_index("c")
    xs, tgt = x[0], loss_target[0]

    shards = {"w_in_a": w_in_a[0], "w_out_a": w_out_a[0], "w_up0": w_up[0], "w_down0": w_down[0], "w_kv": w_kv,
              "w_q_b": w_q_b[0], "w_out_b": w_out_b[0], "w_up1": w_up[1], "w_down1": w_down[1]}
    parts = list(shards)
    lane_pad = lambda a: jnp.pad(a, [(0, 0)] * (a.ndim - 1) + [(0, (-a.shape[-1]) % LANES)])
    n_in_shard = w_in_a.shape[2]
    groups = [("w_in_a", "w_out_a"), ("w_up0", "w_down0"), ("w_kv", "w_q_b", "w_out_b", "w_up1", "w_down1")]
    started = [_gather_start([lane_pad(shards[n].astype(BF16)) for n in grp], f"gather_start{i}") for i, grp in enumerate(groups)]
    gathered = {}

    def finish_gather(i, after):
        srcs, lands = _gather_wait(started[i], after, f"gather_wait{i}")
        gathered.update(zip(groups[i], _gather_pass_on(srcs, lands, f"gather_pass_on{i}")))

    vec = lambda a: a.reshape(1, -1)
    twice = lambda a: jnp.tile(a.reshape(1, -1), (1, 2))

    g_attn0 = vec(g_attn[0]) + sum(st[4][0, 0] for st in started)
    finish_gather(0, g_attn0)
    win = jnp.moveaxis(gathered["w_in_a"][:, :, :n_in_shard], 0, 1).reshape(D, -1)
    win = jnp.pad(win, ((0, 0), (0, (-win.shape[1]) % 128)))
    wout_a = gathered["w_out_a"].reshape(-1, D)
    n_in = win.shape[1]
    tile_in = 640 if n_in % 640 == 0 else 128
    (n0,) = _rms_fwd(xs, [g_attn0], "rms_attn0")
    proj = _matmul(n0, win, "nn", "proj_in", tn=tile_in)
    zt = proj[:, 3 * hw:3 * hw + H].T
    c_row = _gate_fwd(zt, b_f.reshape(H, 1), "gate_fwd")
    c_col3, c_row3 = c_row.reshape(H, S, 1), c_row.reshape(H, 1, S)
    o_a, lse_a = _fox_fwd(proj, c_col3, c_row3, twice(gq_a[0]), twice(gk_a[0]), H, "fox_fwd")
    h1 = _matmul(o_a, wout_a, "nn", "out_a", extras=(xs,), epilogue=lambda acc, res: (res + acc,))
    finish_gather(1, h1)
    wup = [gathered["w_up0"], None]
    wdown = [gathered["w_down0"].reshape(-1, D), None]
    h2, mlp0 = _mlp_fwd(h1, vec(g_mlp[0]), wup[0], wdown[0], "0")

    finish_gather(2, h2)
    wq_b, wout_b = gathered["w_q_b"].reshape(-1, D), gathered["w_out_b"].reshape(-1, D)
    wkv = gathered["w_kv"].reshape(D, -1)
    wup[1], wdown[1] = gathered["w_up1"], gathered["w_down1"].reshape(-1, D)
    nkv, n2 = _rms_fwd(h2, [vec(g_kv), vec(g_attn[1])], "rms_attn1")
    kv = _matmul(nkv, wkv, "nn", "proj_kv")
    kk, vv = _doubled_heads(kv[:, :kvw], KVH), _doubled_heads(kv[:, kvw:], KVH)
    q2 = _matmul(n2, wq_b, "nn", "proj_q")
    onehot = jnp.asarray(_bucket_onehot(), dtype=BF16)
    bias = _bias_expand(rel_bias.T, onehot, "bias_expand").reshape(H, W, 2 * W)
    bias_ab = bias.reshape(H // 2, 2 * W, 2 * W)
    bias_t_ab = bias.reshape(H // 2, 2, W, 2 * W).transpose(0, 3, 1, 2).reshape(H // 2, 2 * W, 2 * W)
    sink_ab = jnp.repeat(sinks[0].reshape(H // 2, 2), W, axis=1)
    o_b, lse_b = _swa_fwd(q2, kk, vv, bias_ab, sink_ab.reshape(H // 2, 2 * W, 1), twice(gq_b[0]), twice(gk_b), "swa_fwd")
    h3 = _matmul(o_b, wout_b, "nn", "out_b", extras=(h2,), epilogue=lambda acc, res: (res + acc,))
    h4, mlp1 = _mlp_fwd(h3, vec(g_mlp[1]), wup[1], wdown[1], "1")

    dh4, loss_part = _loss_head(h4, tgt, "loss_head")
    loss = lax.psum(loss_part[0, 0], ("x", "y", "c"))

    dh3, dg_mlp1, dw_up1, dw_down1 = _mlp_bwd(dh4, h3, vec(g_mlp[1]), wup[1], wdown[1], mlp1, "1")
    dw_out_b = _matmul(o_b, dh3, "tn", "dw_out_b", out_dtypes=(BF16,))
    do_b = _matmul(dh3, wout_b, "nt", "do_b")
    dq2, dk2, dv2, dbias_t_ab, dsink, dgq_b, dgk_b = _swa_bwd(
        q2, kk, vv, bias_t_ab, sink_ab.reshape(H // 2, 1, 2 * W), twice(gq_b[0]), twice(gk_b),
        lse_b.reshape(H // 2, nb, 1, 2 * W), do_b, "swa_bwd")
    dbias = dbias_t_ab.reshape(H // 2, 2 * W, 2, W).transpose(0, 2, 3, 1).reshape(H, W * 2 * W)
    d_rel_bias = _bias_reduce(dbias, onehot, "bias_reduce").T
    dw_q_b = _matmul(n2, dq2, "tn", "dw_q_b", out_dtypes=(BF16,))
    dn2 = _matmul(dq2, wq_b, "nt", "dn2")
    dkv = jnp.concatenate([dk2[h, :, :HEAD_DIM] for h in range(KVH)] + [dv2[h, :, :HEAD_DIM] for h in range(KVH)],
                          axis=1).astype(BF16)
    dw_kv = _matmul(nkv, dkv, "tn", "dw_kv", out_dtypes=(BF16,))
    dnkv = _matmul(dkv, wkv, "nt", "dnkv")
    dh2, (dg_kv, dg_attn1) = _rms_bwd(h2, dh3, [vec(g_kv), vec(g_attn[1])], [dnkv, dn2], "rms_attn1_bwd")

    dh1, dg_mlp0, dw_up0, dw_down0 = _mlp_bwd(dh2, h1, vec(g_mlp[0]), wup[0], wdown[0], mlp0, "0")
    dw_out_a = _matmul(o_a, dh1, "tn", "dw_out_a", out_dtypes=(BF16,))
    do_a = _matmul(dh1, wout_a, "nt", "do_a")
    dq_a, dk_a, dv_a, dc_col, dgq_a, dgk_a = _fox_bwd(
        proj, c_col3, c_row3, twice(gq_a[0]), twice(gk_a[0]), lse_a.reshape(H, 1, S), do_a, H, "fox_bwd")
    dzt, db_f = _gate_bwd(dc_col.reshape(H, S), zt, b_f.reshape(H, 1), "gate_bwd")
    dproj = jnp.concatenate([dq_a, dk_a, dv_a, dzt.T.astype(BF16), jnp.zeros((S, n_in - 3 * hw - H), BF16)], axis=1)
    dw_in = _matmul(n0, dproj, "tn", "dw_in", out_dtypes=(BF16,), tn=tile_in)
    dn0 = _matmul(dproj, win, "nt", "dn0", tk=tile_in)
    grad_x, (dg_attn0,) = _rms_bwd(xs, dh1, [vec(g_attn[0])], [dn0], "rms_attn0_bwd")

    dw_in4 = lane_pad(dw_in[:, :3 * hw + H].reshape(D, N_CHIPS, -1).transpose(1, 0, 2))
    chipwise = {"w_in_a": dw_in4, "w_out_a": dw_out_a.reshape(N_CHIPS, -1, D), "w_up0": dw_up0,
                "w_down0": dw_down0.reshape(N_CHIPS, -1, D), "w_kv": dw_kv.reshape(N_CHIPS, -1, 2 * kvw),
                "w_q_b": dw_q_b.reshape(N_CHIPS, -1, D), "w_out_b": dw_out_b.reshape(N_CHIPS, -1, D), "w_up1": dw_up1,
                "w_down1": dw_down1.reshape(N_CHIPS, -1, D)}
    place = jnp.stack([2 * lax.axis_index("x") + lax.axis_index("y"), c_idx]).astype(jnp.int32)
    got = _swap_halves_group([chipwise[n] for n in parts], "swap_grad_halves")
    pair_sums = [_sum_core_pair(chipwise[n], g, place, "sum_core_pair_" + n) for n, g in zip(parts, got)]
    landed = _scatter_group(pair_sums, "scatter_grads")
    halves = [_sum_chips(p, l, place, "sum_chips_" + n) for n, p, l in zip(parts, pair_sums, landed)]
    reduced = {n: r.reshape(-1, r.shape[2])[:, :shards[n].shape[1]]
               for n, r in zip(parts, _share_halves_group(halves, "share_reduced_halves"))}
    big_grads = [reduced["w_in_a"][None], reduced["w_out_a"][None], reduced["w_kv"], reduced["w_q_b"][None],
                 reduced["w_out_b"][None], jnp.stack([reduced["w_up0"], reduced["w_up1"]]),
                 jnp.stack([reduced["w_down0"], reduced["w_down1"]])]

    small_grads = {
        "g_attn": jnp.concatenate([dg_attn0, dg_attn1], axis=0), "g_mlp": jnp.concatenate([dg_mlp0, dg_mlp1], axis=0),
        "b_f": db_f.reshape(1, H), "gq_a": dgq_a[:, :HEAD_DIM], "gk_a": dgk_a[:, :HEAD_DIM], "g_kv": dg_kv.reshape(-1),
        "gk_b": dgk_b[0, :HEAD_DIM], "gq_b": dgq_b[:, :HEAD_DIM], "sinks": dsink[:, 0, 0].reshape(1, H), "rel_bias": d_rel_bias,
    }
    small_shapes = [given[n].shape for n in SMALL]
    spack = _pack_small([small_grads[n] for n in SMALL])
    small_sum = _sum_parts(_allgather_small(spack, "allgather_small"), "sum_small", F32, tr=spack.shape[0])
    small_red = _unpack(small_sum.reshape(-1), small_shapes)

    grads = dict(zip([n for n, _ in BIG], big_grads))
    grads.update(dict(zip(SMALL, small_red)))
    sw = _pack_small([given[n] for n in SMALL])
    sm = _pack_small([given["m_" + n] for n in SMALL])
    sv = _pack_small([given["v_" + n] for n in SMALL])
    sd, sm2, sv2 = _adamw(sw, small_sum, sm, sv, "adamw_small", tr=sw.shape[0])
    delta = dict(zip(SMALL, _unpack(sd.reshape(-1), small_shapes)))
    new_m = dict(zip(SMALL, _unpack(sm2.reshape(-1), small_shapes)))
    new_v = dict(zip(SMALL, _unpack(sv2.reshape(-1), small_shapes)))
    for n, _ in BIG:
        w = given[n]
        two_d = (-1, w.shape[-1])
        d, m2, v2 = _adamw(w.reshape(two_d), grads[n].reshape(two_d), given["m_" + n].reshape(two_d),
                           given["v_" + n].reshape(two_d), "adamw_" + n)
        delta[n], new_m[n], new_v[n] = d.reshape(w.shape), m2.reshape(w.shape), v2.reshape(w.shape)

    order = ["g_attn", "g_mlp", "w_in_a", "b_f", "gq_a", "gk_a", "w_out_a", "g_kv", "w_kv", "gk_b", "w_q_b", "gq_b",
             "sinks", "rel_bias", "w_out_b", "w_up", "w_down"]
    return (loss, grad_x[None], *[grads[n] for n in order], *[delta[n] for n in order],
            *[new_m[n] for n in order], *[new_v[n] for n in order])
```

```python
import numpy as np
import jax
import jax.numpy as jnp
from jax import lax
from jax.experimental import pallas as pl
from jax.experimental.pallas import tpu as pltpu

F32 = jnp.float32
BF16 = jnp.bfloat16
MESH = pl.DeviceIdType.MESH

HEAD_DIM = 64
LANES = 128
WINDOW = 128
N_BUCKETS = 32
REL_MAX_DIST = 128
NORM_EPS = 1e-6
ADAM_LR = 0.001
ADAM_B1 = 0.9
ADAM_B2 = 0.999
ADAM_EPS = 1e-08
ADAM_WD = 0.01
ADAM_STEP = 10
NEG = -1e30
N_CHIPS = 4
PACK_W = 1024
PACK_ROW_ALIGN = 256
VMEM_LIMIT = 56 * 1024 * 1024
HBM_SPEC = pl.BlockSpec(memory_space=pltpu.HBM)
VMEM_SPEC = pl.BlockSpec(memory_space=pltpu.VMEM)

BIG = (("w_in_a", 2), ("w_out_a", 1), ("w_kv", 0), ("w_q_b", 1), ("w_out_b", 1), ("w_up", 2), ("w_down", 1))
SMALL = ("g_attn", "g_mlp", "b_f", "gq_a", "gk_a", "g_kv", "gk_b", "gq_b", "sinks", "rel_bias")


def _pcall(body, **kw):
    return pl.pallas_call(body, **kw)


def _params(sem=None):
    return pltpu.CompilerParams(dimension_semantics=sem, vmem_limit_bytes=VMEM_LIMIT)


def _rinv(x):
    return lax.rsqrt(jnp.mean(x * x, axis=-1, keepdims=True) + NORM_EPS)


def _dot(a, b, dims, precision=None):
    return lax.dot_general(a, b, (dims, ((), ())), precision=precision, preferred_element_type=F32)


NN = ((1,), (0,))
NT = ((1,), (1,))
TN = ((0,), (0,))


def _accumulate(ref, val, first):
    @pl.when(first)
    def _():
        ref[...] = val

    @pl.when(jnp.logical_not(first))
    def _():
        ref[...] += val


def _matmul(a, b, mode, name, out_dtypes=(F32,), extras=(), epilogue=None, tm=512, tn=512, tk=None, chipwise=None):
    if chipwise == "b":
        nc = b.shape[2]
        M, K = a.shape
        (K2, N) = (b.shape[1], N_CHIPS * nc) if mode == "nn" else (N_CHIPS * nc, b.shape[1])
    elif mode == "nn":
        (M, K), (K2, N) = a.shape, b.shape
    elif mode == "nt":
        (M, K), (N, K2) = a.shape, b.shape
    else:
        (K, M), (K2, N) = a.shape, b.shape
    assert K == K2, (a.shape, b.shape, mode)
    tm, tn = min(tm, M), min(tn, N)
    tk = K if tk is None else tk
    assert M % tm == 0 and N % tn == 0 and K % tk == 0, (M, N, K, tm, tn, tk)
    nk = K // tk
    dims = {"nn": NN, "nt": NT, "tn": TN}[mode]
    a_spec = pl.BlockSpec((tk, tm), lambda i, j, k: (k, i)) if mode == "tn" else pl.BlockSpec((tm, tk), lambda i, j, k: (i, k))
    b_spec = pl.BlockSpec((tn, tk), lambda i, j, k: (j, k)) if mode == "nt" else pl.BlockSpec((tk, tn), lambda i, j, k: (k, j))
    o_spec = pl.BlockSpec((tm, tn), lambda i, j, k: (i, j))
    out_shape = (M, N)
    if chipwise == "b" and mode == "nn":
        per = nc // tn
        assert tk == K and nc % tn == 0
        b_spec = pl.BlockSpec((None, tk, tn), lambda i, j, k: (j // per, 0, j % per))
    elif chipwise == "b":
        assert mode == "nt" and tk == nc
        b_spec = pl.BlockSpec((None, tn, tk), lambda i, j, k: (k, j, 0))
    elif chipwise == "out":
        per = (N // N_CHIPS) // tn
        assert (N // N_CHIPS) % tn == 0
        o_spec = pl.BlockSpec((None, tm, tn), lambda i, j, k: (j // per, i, j % per))
        out_shape = (N_CHIPS, M, N // N_CHIPS)
        assert not extras
    n_ex, n_out = len(extras), len(out_dtypes)

    def body(*refs):
        a_ref, b_ref = refs[0], refs[1]
        ex_refs = refs[2:2 + n_ex]
        out_refs = refs[2 + n_ex:2 + n_ex + n_out]
        acc_ref = refs[2 + n_ex + n_out]
        k = pl.program_id(2)
        part = _dot(a_ref[...].astype(BF16), b_ref[...].astype(BF16), dims)

        @pl.when(k == 0)
        def _():
            acc_ref[...] = part

        @pl.when(k > 0)
        def _():
            acc_ref[...] += part

        @pl.when(k == nk - 1)
        def _():
            acc = acc_ref[...]
            outs = (acc,) if epilogue is None else epilogue(acc, *[r[...] for r in ex_refs])
            for r, o in zip(out_refs, outs):
                r[...] = o.astype(r.dtype)

    outs = _pcall(
        body, name=name, grid=(M // tm, N // tn, nk),
        in_specs=[a_spec, b_spec] + [o_spec] * n_ex,
        out_specs=[o_spec] * n_out,
        out_shape=[jax.ShapeDtypeStruct(out_shape, dt) for dt in out_dtypes],
        scratch_shapes=[pltpu.VMEM((tm, tn), F32)],
        compiler_params=_params(("parallel", "parallel", "arbitrary")),
    )(a, b, *extras)
    return outs[0] if n_out == 1 else outs


def _rms_fwd(x, gains, name, ts=256):
    S, D = x.shape
    ts = min(ts, S)
    n = len(gains)

    def body(*refs):
        x_ref, g_refs, o_refs = refs[0], refs[1:1 + n], refs[1 + n:]
        xv = x_ref[...]
        xh = xv * _rinv(xv)
        for g_ref, o_ref in zip(g_refs, o_refs):
            o_ref[...] = (xh * g_ref[...]).astype(BF16)

    row = pl.BlockSpec((ts, D), lambda i: (i, 0))
    vec = pl.BlockSpec((1, D), lambda i: (0, 0))
    return _pcall(body, name=name, grid=(S // ts,), in_specs=[row] + [vec] * n, out_specs=[row] * n,
                  out_shape=[jax.ShapeDtypeStruct((S, D), BF16)] * n, compiler_params=_params(("parallel",)))(x, *gains)


def _rms_bwd(x, dres, gains, dns, name, ts=256):
    S, D = x.shape
    ts = min(ts, S)
    n = len(gains)

    def body(*refs):
        x_ref, dres_ref = refs[0], refs[1]
        g_refs, dn_refs = refs[2:2 + n], refs[2 + n:2 + 2 * n]
        dx_ref, dg_refs = refs[2 + 2 * n], refs[3 + 2 * n:]
        xv = x_ref[...]
        r = _rinv(xv)
        xh = xv * r
        dx = dres_ref[...]
        first = pl.program_id(0) == 0
        for g_ref, dn_ref, dg_ref in zip(g_refs, dn_refs, dg_refs):
            dn = dn_ref[...].astype(F32)
            _accumulate(dg_ref, jnp.sum(dn * xh, axis=0, keepdims=True), first)
            dxh = dn * g_ref[...]
            dx = dx + r * (dxh - xh * jnp.mean(dxh * xh, axis=-1, keepdims=True))
        dx_ref[...] = dx

    row = pl.BlockSpec((ts, D), lambda i: (i, 0))
    vec = pl.BlockSpec((1, D), lambda i: (0, 0))
    outs = _pcall(body, name=name, grid=(S // ts,), in_specs=[row, row] + [vec] * n + [row] * n,
                  out_specs=[row] + [vec] * n,
                  out_shape=[jax.ShapeDtypeStruct((S, D), F32)] + [jax.ShapeDtypeStruct((1, D), F32)] * n,
                  compiler_params=_params(("arbitrary",)))(x, dres, *gains, *dns)
    return outs[0], outs[1:]


def _loss_head(h, tgt, name, ts=256):
    S, D = h.shape
    ts = min(ts, S)

    def body(h_ref, t_ref, dh_ref, loss_ref):
        err = h_ref[...] - t_ref[...]
        dh_ref[...] = err * (1.0 / D)
        part = 0.5 * jnp.sum(jnp.mean(err * err, axis=-1, keepdims=True), axis=0, keepdims=True)
        _accumulate(loss_ref, part, pl.program_id(0) == 0)

    row = pl.BlockSpec((ts, D), lambda i: (i, 0))
    return _pcall(body, name=name, grid=(S // ts,), in_specs=[row, row],
                  out_specs=[row, pl.BlockSpec((1, 1), lambda i: (0, 0))],
                  out_shape=[jax.ShapeDtypeStruct((S, D), F32), jax.ShapeDtypeStruct((1, 1), F32)],
                  compiler_params=_params(("arbitrary",)))(h, tgt)


def _gate_fwd(zt, bf, name):
    H, S = zt.shape
    nb = S // 128

    def body(z_ref, b_ref, c_ref):
        z = z_ref[...] + b_ref[...]
        lf = jnp.minimum(z, 0.0) - jnp.log(1.0 + jnp.exp(-jnp.abs(z)))
        upper = (lax.broadcasted_iota(jnp.int32, (128, 128), 0) <= lax.broadcasted_iota(jnp.int32, (128, 128), 1)).astype(F32)
        carry = jnp.zeros((H, 1), F32)
        for blk in range(nb):
            cs = _dot(lf[:, blk * 128:(blk + 1) * 128], upper, NN, precision=lax.Precision.HIGHEST) + carry
            c_ref[:, blk * 128:(blk + 1) * 128] = cs
            carry = cs[:, 127:128]

    return _pcall(body, name=name, in_specs=[VMEM_SPEC, VMEM_SPEC], out_specs=VMEM_SPEC,
                  out_shape=jax.ShapeDtypeStruct((H, S), F32))(zt, bf)


def _gate_bwd(dct, zt, bf, name):
    H, S = zt.shape
    nb = S // 128

    def body(dc_ref, z_ref, b_ref, dz_ref, db_ref):
        z = z_ref[...] + b_ref[...]
        e = jnp.exp(-jnp.abs(z))
        sig_neg = jnp.where(z >= 0, e, 1.0) / (1.0 + e)
        lower = (lax.broadcasted_iota(jnp.int32, (128, 128), 0) >= lax.broadcasted_iota(jnp.int32, (128, 128), 1)).astype(F32)
        dc = dc_ref[...]
        carry = jnp.zeros((H, 1), F32)
        db = jnp.zeros((H, 1), F32)
        for blk in reversed(range(nb)):
            sl = slice(blk * 128, (blk + 1) * 128)
            dlf = _dot(dc[:, sl], lower, NN, precision=lax.Precision.HIGHEST) + carry
            carry = dlf[:, 0:1]
            dz = dlf * sig_neg[:, sl]
            dz_ref[:, sl] = dz
            db = db + jnp.sum(dz, axis=1, keepdims=True)
        db_ref[...] = db

    return _pcall(body, name=name, in_specs=[VMEM_SPEC] * 3, out_specs=[VMEM_SPEC] * 2,
                  out_shape=[jax.ShapeDtypeStruct((H, S), F32), jax.ShapeDtypeStruct((H, 1), F32)])(dct, zt, bf)


def _lane_is_a():
    return lax.broadcasted_iota(jnp.int32, (1, LANES), 1) < HEAD_DIM


def _per_head_mean(x, is_a):
    sa = jnp.sum(jnp.where(is_a, x, 0.0), axis=-1, keepdims=True)
    sb = jnp.sum(jnp.where(is_a, 0.0, x), axis=-1, keepdims=True)
    return jnp.where(is_a, sa, sb) / HEAD_DIM


def _pair_norm(raw, gain, is_a):
    return raw * lax.rsqrt(_per_head_mean(raw * raw, is_a) + NORM_EPS) * gain


def _pair_norm_bwd(raw, gain, dnormed, is_a):
    r = lax.rsqrt(_per_head_mean(raw * raw, is_a) + NORM_EPS)
    xh = raw * r
    dgain = jnp.sum(dnormed * xh, axis=0, keepdims=True)
    dxh = dnormed * gain
    return r * (dxh - xh * _per_head_mean(dxh * xh, is_a)), dgain


def _fold_heads(x):
    i = lax.broadcasted_iota(jnp.int32, (LANES, LANES), 0)
    j = lax.broadcasted_iota(jnp.int32, (LANES, LANES), 1)
    fold = ((i == j) | (i == j + HEAD_DIM) | (i + HEAD_DIM == j)).astype(F32)
    return _dot(x, fold, NN, precision=lax.Precision.HIGHEST)


def _fold_row(ref):
    ref[...] = _fold_heads(jnp.broadcast_to(ref[...], (8, LANES)))[0:1, :]


def _tri_mask(t, keys_on_rows):
    r = lax.broadcasted_iota(jnp.int32, (t, t), 0)
    c = lax.broadcasted_iota(jnp.int32, (t, t), 1)
    return (r <= c) if keys_on_rows else (r >= c)


def _fox_fwd(proj, c_col, c_row, gq2, gk2, n_heads, name, t=256):
    S = proj.shape[0]
    H = n_heads
    P = H // 2
    t = min(t, S)
    nq = S // t

    def body(q_ref, k_ref, v_ref, cc_ref, cr_ref, gq_ref, gk_ref, o_ref, lse_ref, qs_s, kb_s, vb_s):
        is_a = _lane_is_a()
        qn = _pair_norm(q_ref[...], gq_ref[...], is_a) * 0.125
        qs_s[0] = jnp.where(is_a, qn, 0.0).astype(BF16)
        qs_s[1] = jnp.where(is_a, 0.0, qn).astype(BF16)
        kb_s[...] = _pair_norm(k_ref[...], gk_ref[...], is_a).astype(BF16)
        vb_s[...] = v_ref[...].astype(BF16)
        causal = _tri_mask(t, False)
        for i in range(nq):
            t0 = i * t
            rows = slice(t0, t0 + t)
            o_pair = None
            for a in range(2):
                qi = qs_s[a, rows, :]
                ci = cc_ref[a, rows, :]
                s_d = jnp.where(causal, _dot(qi, kb_s[rows, :], NT) + ci - cr_ref[a, :, rows], NEG)
                m = jnp.max(s_d, axis=-1, keepdims=True)
                if i > 0:
                    s_l = _dot(qi, kb_s[0:t0, :], NT) + ci - cr_ref[a, :, 0:t0]
                    m = jnp.maximum(m, jnp.max(s_l, axis=-1, keepdims=True))
                p_d = jnp.exp(s_d - m)
                l = jnp.sum(p_d, axis=-1, keepdims=True)
                acc = _dot(p_d.astype(BF16), vb_s[rows, :], NN)
                if i > 0:
                    p_l = jnp.exp(s_l - m)
                    l = l + jnp.sum(p_l, axis=-1, keepdims=True)
                    acc = acc + _dot(p_l.astype(BF16), vb_s[0:t0, :], NN)
                o_a = acc / l
                lse_ref[a, rows, :] = m + jnp.log(l)
                o_pair = o_a if a == 0 else jnp.where(is_a, o_pair, o_a)
            o_ref[rows, :] = o_pair.astype(BF16)

    def cols(off):
        return pl.BlockSpec((S, LANES), lambda p: (0, off + p))

    col = pl.BlockSpec((2, S, 1), lambda p: (p, 0, 0))
    rowv = pl.BlockSpec((2, 1, S), lambda p: (p, 0, 0))
    gain = pl.BlockSpec((1, LANES), lambda p: (0, 0))
    return _pcall(body, name=name, grid=(P,), in_specs=[cols(0), cols(P), cols(2 * P), col, rowv, gain, gain],
                  out_specs=[cols(0), col],
                  out_shape=[jax.ShapeDtypeStruct((S, H * HEAD_DIM), BF16), jax.ShapeDtypeStruct((H, S, 1), F32)],
                  scratch_shapes=[pltpu.VMEM((2, S, LANES), BF16), pltpu.VMEM((S, LANES), BF16), pltpu.VMEM((S, LANES), BF16)],
                  compiler_params=_params(("parallel",)))(proj, proj, proj, c_col, c_row, gq2, gk2)


def _fox_bwd(proj, c_col, c_row, gq2, gk2, lse_row, do, n_heads, name, t=256):
    S = proj.shape[0]
    H = n_heads
    P = H // 2
    t = min(t, S)
    nq = S // t
    assert t % LANES == 0

    def body(q_ref, k_ref, v_ref, cc_ref, cr_ref, gq_ref, gk_ref, lr_ref, do_ref,
             dq_ref, dk_ref, dv_ref, dc_ref, dgq_ref, dgk_ref,
             qs_s, kb_s, kt_s, vb_s, dob_s, dq_s, dk_s, dv_s, dcs_s):
        is_a = _lane_is_a()
        qn = _pair_norm(q_ref[...], gq_ref[...], is_a) * 0.125
        qs_s[0] = jnp.where(is_a, qn, 0.0).astype(BF16)
        qs_s[1] = jnp.where(is_a, 0.0, qn).astype(BF16)
        kn = _pair_norm(k_ref[...], gk_ref[...], is_a)
        kb_s[...] = kn.astype(BF16)
        kt_s[0] = jnp.where(is_a, kn, 0.0).T.astype(BF16)
        kt_s[1] = jnp.where(is_a, 0.0, kn).T.astype(BF16)
        vb_s[...] = v_ref[...].astype(BF16)
        dov = do_ref[...]
        dob_s[0] = jnp.where(is_a, dov, 0.0).astype(BF16)
        dob_s[1] = jnp.where(is_a, 0.0, dov).astype(BF16)
        dk_s[...] = jnp.zeros((S, LANES), F32)
        dv_s[...] = jnp.zeros((S, LANES), F32)
        dcs_s[...] = jnp.zeros((2, S, LANES), F32)
        causal = _tri_mask(t, True)
        for i in range(nq):
            t0 = i * t
            rows = slice(t0, t0 + t)
            dq_t = jnp.zeros((LANES, t), F32)
            for a in range(2):
                qi = qs_s[a, rows, :]
                doi = dob_s[a, rows, :]
                cri = cr_ref[a, :, rows]
                lri = lr_ref[a, :, rows]

                def probs(keys, masked, a=a, qi=qi, doi=doi, cri=cri, lri=lri):
                    p_t = jnp.exp(_dot(kb_s[keys, :], qi, NT) + cri - cc_ref[a, keys, :] - lri)
                    if masked:
                        p_t = jnp.where(causal, p_t, 0.0)
                    return p_t, _dot(vb_s[keys, :], doi, NT)

                parts = [(rows,) + probs(rows, True)]
                if i > 0:
                    parts.append((slice(0, t0),) + probs(slice(0, t0), False))
                delta = sum(jnp.sum(p_t * dp_t, axis=0, keepdims=True) for _, p_t, dp_t in parts)
                for keys, p_t, dp_t in parts:
                    ds_t = p_t * (dp_t - delta)
                    dsb = ds_t.astype(BF16)
                    dv_s[keys, :] += _dot(p_t.astype(BF16), doi, NN)
                    dk_s[keys, :] += _dot(dsb, qi, NN)
                    dq_t = dq_t + _dot(kt_s[a, :, keys], dsb, NN)
                    dcs_s[a, keys, :] += sum(ds_t[:, b * LANES:(b + 1) * LANES] for b in range(t // LANES))
            dq_s[rows, :] = dq_t.T
        first = pl.program_id(0) == 0
        last = pl.program_id(0) == P - 1
        dq_raw, dgq = _pair_norm_bwd(q_ref[...], gq_ref[...], dq_s[...] * 0.125, is_a)
        dq_ref[...] = dq_raw.astype(BF16)
        _accumulate(dgq_ref, dgq, first)
        dk_raw, dgk = _pair_norm_bwd(k_ref[...], gk_ref[...], dk_s[...], is_a)
        dk_ref[...] = dk_raw.astype(BF16)
        _accumulate(dgk_ref, dgk, first)
        dv_ref[...] = dv_s[...].astype(BF16)
        for a in range(2):
            dc_ref[a] = -jnp.sum(dcs_s[a], axis=1, keepdims=True)

        @pl.when(last)
        def _():
            _fold_row(dgq_ref)
            _fold_row(dgk_ref)

    def cols(off):
        return pl.BlockSpec((S, LANES), lambda p: (0, off + p))

    col = pl.BlockSpec((2, S, 1), lambda p: (p, 0, 0))
    rowv = pl.BlockSpec((2, 1, S), lambda p: (p, 0, 0))
    gain = pl.BlockSpec((1, LANES), lambda p: (0, 0))
    wide = jax.ShapeDtypeStruct((S, H * HEAD_DIM), BF16)
    gs = jax.ShapeDtypeStruct((1, LANES), F32)
    return _pcall(body, name=name, grid=(P,),
                  in_specs=[cols(0), cols(P), cols(2 * P), col, rowv, gain, gain, rowv, cols(0)],
                  out_specs=[cols(0), cols(0), cols(0), col, gain, gain],
                  out_shape=[wide, wide, wide, jax.ShapeDtypeStruct((H, S, 1), F32), gs, gs],
                  scratch_shapes=[pltpu.VMEM((2, S, LANES), BF16), pltpu.VMEM((S, LANES), BF16), pltpu.VMEM((2, LANES, S), BF16),
                                  pltpu.VMEM((S, LANES), BF16), pltpu.VMEM((2, S, LANES), BF16)]
                  + [pltpu.VMEM((S, LANES), F32)] * 3 + [pltpu.VMEM((2, S, LANES), F32)],
                  compiler_params=_params(("arbitrary",)))(proj, proj, proj, c_col, c_row, gq2, gk2, lse_row, do)


def _bucket_onehot():
    W = WINDOW
    dist = np.arange(W)[:, None] + W - np.arange(2 * W)[None, :]
    n = np.maximum(dist, 0)
    max_exact = N_BUCKETS // 2
    large = max_exact + (np.log(np.maximum(n, 1) / max_exact) / np.log(REL_MAX_DIST / max_exact)
                         * (N_BUCKETS - max_exact)).astype(np.int32)
    large = np.minimum(large, N_BUCKETS - 1)
    bucket = np.where(n < max_exact, n, large).astype(np.int32)
    valid = (dist >= 0) & (dist < W)
    onehot = (bucket[None] == np.arange(N_BUCKETS)[:, None, None]) & valid[None]
    return onehot.reshape(N_BUCKETS, W * 2 * W).astype(np.float32)


def _bias_expand(rel_bias_t, onehot, name, tn=4096):
    HQ, NB = rel_bias_t.shape
    L = onehot.shape[1]

    def body(r_ref, oh_ref, out_ref):
        out_ref[...] = _dot(r_ref[...], oh_ref[...].astype(F32), NN, precision=lax.Precision.HIGHEST)

    return _pcall(body, name=name, grid=(L // tn,),
                  in_specs=[pl.BlockSpec((HQ, NB), lambda i: (0, 0)), pl.BlockSpec((NB, tn), lambda i: (0, i))],
                  out_specs=pl.BlockSpec((HQ, tn), lambda i: (0, i)),
                  out_shape=jax.ShapeDtypeStruct((HQ, L), F32), compiler_params=_params(("parallel",)))(rel_bias_t, onehot)


def _bias_reduce(dbias, onehot, name, tk=4096):
    HQ, L = dbias.shape
    NB = onehot.shape[0]

    def body(d_ref, oh_ref, out_ref):
        part = _dot(d_ref[...], oh_ref[...].astype(F32), NT, precision=lax.Precision.HIGHEST)
        _accumulate(out_ref, part, pl.program_id(0) == 0)

    return _pcall(body, name=name, grid=(L // tk,),
                  in_specs=[pl.BlockSpec((HQ, tk), lambda i: (0, i)), pl.BlockSpec((NB, tk), lambda i: (0, i))],
                  out_specs=pl.BlockSpec((HQ, NB), lambda i: (0, 0)),
                  out_shape=jax.ShapeDtypeStruct((HQ, NB), F32), compiler_params=_params(("arbitrary",)))(dbias, onehot)


def _stacked_query_index(n_rows_or_cols_axis, shape):
    idx = lax.broadcasted_iota(jnp.int32, shape, n_rows_or_cols_axis)
    return jnp.where(idx >= WINDOW, idx - WINDOW, idx)


def _swa_fwd(qproj, kk, vv, bias_ab, sink_col, gq2, gk2, name):
    S, HQD = qproj.shape
    KVH = kk.shape[0]
    PP = HQD // LANES
    NP = PP // KVH
    W = WINDOW
    nb = S // W

    def body(q_ref, k_ref, v_ref, bias_ref, sink_ref, gq_ref, gk_ref, o_ref, lse_ref, qs_s, kb_s, vb_s):
        is_a = _lane_is_a()
        qn = _pair_norm(q_ref[...], gq_ref[...], is_a) * 0.125
        qs_s[0] = jnp.where(is_a, qn, 0.0).astype(BF16)
        qs_s[1] = jnp.where(is_a, 0.0, qn).astype(BF16)
        kb_s[...] = _pair_norm(k_ref[...], gk_ref[...], is_a).astype(BF16)
        vb_s[...] = v_ref[...].astype(BF16)
        sink = sink_ref[...]
        qi1 = _stacked_query_index(0, (2 * W, W))
        first_valid = lax.broadcasted_iota(jnp.int32, (2 * W, W), 1) <= qi1
        qi2 = _stacked_query_index(0, (2 * W, 2 * W))
        key2 = lax.broadcasted_iota(jnp.int32, (2 * W, 2 * W), 1)
        band_valid = (key2 > qi2) & (key2 <= qi2 + W)
        for n in range(nb):
            rows = slice(n * W, (n + 1) * W)
            keys = slice(0, W) if n == 0 else slice((n - 1) * W, (n + 1) * W)
            lhs = jnp.concatenate([qs_s[0, rows, :], qs_s[1, rows, :]], axis=0)
            s = _dot(lhs, kb_s[keys, :], NT) + (bias_ref[:, W:2 * W] if n == 0 else bias_ref[...])
            s = jnp.where(first_valid if n == 0 else band_valid, s, NEG)
            m = jnp.maximum(jnp.max(s, axis=-1, keepdims=True), sink)
            e = jnp.exp(s - m)
            l = jnp.sum(e, axis=-1, keepdims=True) + jnp.exp(sink - m)
            o_ab = _dot(e.astype(BF16), vb_s[keys, :], NN) / l
            o_ref[rows, :] = jnp.where(is_a, o_ab[0:W, :], o_ab[W:2 * W, :]).astype(BF16)
            lse_ref[n] = m + jnp.log(l)

    qcols = pl.BlockSpec((S, LANES), lambda a, g: (0, a * NP + g))
    kvs = pl.BlockSpec((None, S, LANES), lambda a, g: (a, 0, 0))
    gain = pl.BlockSpec((1, LANES), lambda a, g: (0, 0))
    return _pcall(body, name=name, grid=(KVH, NP),
                  in_specs=[qcols, kvs, kvs, pl.BlockSpec((None, 2 * W, 2 * W), lambda a, g: (a * NP + g, 0, 0)),
                            pl.BlockSpec((None, 2 * W, 1), lambda a, g: (a * NP + g, 0, 0)), gain, gain],
                  out_specs=[qcols, pl.BlockSpec((None, nb, 2 * W, 1), lambda a, g: (a * NP + g, 0, 0, 0))],
                  out_shape=[jax.ShapeDtypeStruct((S, HQD), BF16), jax.ShapeDtypeStruct((PP, nb, 2 * W, 1), F32)],
                  scratch_shapes=[pltpu.VMEM((2, S, LANES), BF16), pltpu.VMEM((S, LANES), BF16), pltpu.VMEM((S, LANES), BF16)],
                  compiler_params=_params(("parallel", "parallel")))(qproj, kk, vv, bias_ab, sink_col, gq2, gk2)


def _swa_bwd(qproj, kk, vv, bias_t_ab, sink_row, gq2, gk2, lse_row, do, name):
    S, HQD = qproj.shape
    KVH = kk.shape[0]
    PP = HQD // LANES
    NP = PP // KVH
    W = WINDOW
    nb = S // W

    def body(q_ref, k_ref, v_ref, bias_ref, sink_ref, gq_ref, gk_ref, lr_ref, do_ref,
             dq_ref, dk_ref, dv_ref, db_ref, dsink_ref, dgq_ref, dgk_ref,
             qs_s, kb_s, kt_s, vb_s, dob_s, dq_s, dk_s, dv_s):
        a, g = pl.program_id(0), pl.program_id(1)
        is_a = _lane_is_a()
        qn = _pair_norm(q_ref[...], gq_ref[...], is_a) * 0.125
        qs_s[0] = jnp.where(is_a, qn, 0.0).astype(BF16)
        qs_s[1] = jnp.where(is_a, 0.0, qn).astype(BF16)
        kn = _pair_norm(k_ref[...], gk_ref[...], is_a)
        kb_s[...] = kn.astype(BF16)
        kt_s[...] = kn.T.astype(BF16)
        vb_s[...] = v_ref[...].astype(BF16)
        dov = do_ref[...]
        dob_s[0] = jnp.where(is_a, dov, 0.0).astype(BF16)
        dob_s[1] = jnp.where(is_a, 0.0, dov).astype(BF16)
        sink = sink_ref[...]

        @pl.when(g == 0)
        def _():
            dk_s[...] = jnp.zeros((S, LANES), F32)
            dv_s[...] = jnp.zeros((S, LANES), F32)

        qi1 = _stacked_query_index(1, (W, 2 * W))
        first_valid = lax.broadcasted_iota(jnp.int32, (W, 2 * W), 0) <= qi1
        qi2 = _stacked_query_index(1, (2 * W, 2 * W))
        key2 = lax.broadcasted_iota(jnp.int32, (2 * W, 2 * W), 0)
        band_valid = (key2 > qi2) & (key2 <= qi2 + W)
        head_rows = lax.broadcasted_iota(jnp.int32, (LANES, W), 0) < HEAD_DIM
        db = jnp.zeros((2 * W, 2 * W), F32)
        dsk = jnp.zeros((1, 2 * W), F32)
        pend_k = pend_v = None
        for n in range(nb):
            rows = slice(n * W, (n + 1) * W)
            keys = slice(0, W) if n == 0 else slice((n - 1) * W, (n + 1) * W)
            lhs_q = jnp.concatenate([qs_s[0, rows, :], qs_s[1, rows, :]], axis=0)
            lhs_do = jnp.concatenate([dob_s[0, rows, :], dob_s[1, rows, :]], axis=0)
            lse = lr_ref[n]
            s_t = _dot(kb_s[keys, :], lhs_q, NT) + (bias_ref[W:2 * W, :] if n == 0 else bias_ref[...])
            p_t = jnp.where(first_valid if n == 0 else band_valid, jnp.exp(s_t - lse), 0.0)
            dp_t = _dot(vb_s[keys, :], lhs_do, NT)
            delta = jnp.sum(p_t * dp_t, axis=0, keepdims=True)
            ds_t = p_t * (dp_t - delta)
            dsb = ds_t.astype(BF16)
            dsk = dsk - jnp.exp(sink - lse) * delta
            dv_band = _dot(p_t.astype(BF16), lhs_do, NN)
            dk_band = _dot(dsb, lhs_q, NN)
            dq_t = _dot(kt_s[:, keys], dsb, NN)
            dq_s[rows, :] = jnp.where(head_rows, dq_t[:, 0:W], dq_t[:, W:2 * W]).T
            if n == 0:
                db = jnp.concatenate([jnp.zeros((W, 2 * W), F32), ds_t], axis=0)
                pend_k, pend_v = dk_band, dv_band
            else:
                db = db + ds_t
                prev = slice((n - 1) * W, n * W)
                dk_s[prev, :] += pend_k + dk_band[0:W, :]
                dv_s[prev, :] += pend_v + dv_band[0:W, :]
                pend_k, pend_v = dk_band[W:2 * W, :], dv_band[W:2 * W, :]
        tail = slice((nb - 1) * W, nb * W)
        dk_s[tail, :] += pend_k
        dv_s[tail, :] += pend_v
        db_ref[...] = db
        dsink_ref[0] = jnp.broadcast_to(jnp.sum(dsk[:, 0:W], axis=1, keepdims=True), (1, LANES))
        dsink_ref[1] = jnp.broadcast_to(jnp.sum(dsk[:, W:2 * W], axis=1, keepdims=True), (1, LANES))
        dq_raw, dgq = _pair_norm_bwd(q_ref[...], gq_ref[...], dq_s[...] * 0.125, is_a)
        dq_ref[...] = dq_raw.astype(BF16)
        _accumulate(dgq_ref, dgq, jnp.logical_and(a == 0, g == 0))

        @pl.when(jnp.logical_and(a == KVH - 1, g == NP - 1))
        def _():
            _fold_row(dgq_ref)

        @pl.when(g == NP - 1)
        def _():
            dk_raw, dgk = _pair_norm_bwd(k_ref[...], gk_ref[...], _fold_heads(dk_s[...]), is_a)
            dk_ref[...] = dk_raw
            _accumulate(dgk_ref, dgk, a == 0)
            dv_ref[...] = _fold_heads(dv_s[...])

    qcols = pl.BlockSpec((S, LANES), lambda a, g: (0, a * NP + g))
    kvs = pl.BlockSpec((None, S, LANES), lambda a, g: (a, 0, 0))
    sq = pl.BlockSpec((None, 2 * W, 2 * W), lambda a, g: (a * NP + g, 0, 0))
    gain = pl.BlockSpec((1, LANES), lambda a, g: (0, 0))
    ks = jax.ShapeDtypeStruct((KVH, S, LANES), F32)
    gs = jax.ShapeDtypeStruct((1, LANES), F32)
    return _pcall(body, name=name, grid=(KVH, NP),
                  in_specs=[qcols, kvs, kvs, sq, pl.BlockSpec((None, 1, 2 * W), lambda a, g: (a * NP + g, 0, 0)), gain, gain,
                            pl.BlockSpec((None, nb, 1, 2 * W), lambda a, g: (a * NP + g, 0, 0, 0)), qcols],
                  out_specs=[qcols, kvs, kvs, sq, pl.BlockSpec((2, 1, LANES), lambda a, g: (a * NP + g, 0, 0)), gain, gain],
                  out_shape=[jax.ShapeDtypeStruct((S, HQD), BF16), ks, ks, jax.ShapeDtypeStruct((PP, 2 * W, 2 * W), F32),
                             jax.ShapeDtypeStruct((2 * PP, 1, LANES), F32), gs, gs],
                  scratch_shapes=[pltpu.VMEM((2, S, LANES), BF16), pltpu.VMEM((S, LANES), BF16), pltpu.VMEM((LANES, S), BF16),
                                  pltpu.VMEM((S, LANES), BF16), pltpu.VMEM((2, S, LANES), BF16)] + [pltpu.VMEM((S, LANES), F32)] * 3,
                  compiler_params=_params(("arbitrary", "arbitrary")))(qproj, kk, vv, bias_t_ab, sink_row, gq2, gk2, lse_row, do)


def _adamw(w, g, m, v, name, tr=256):
    R, C = w.shape
    tr = min(tr, R)
    assert R % tr == 0

    def body(w_ref, g_ref, m_ref, v_ref, d_ref, m2_ref, v2_ref):
        gv = g_ref[...]
        m2 = ADAM_B1 * m_ref[...] + (1.0 - ADAM_B1) * gv
        v2 = ADAM_B2 * v_ref[...] + (1.0 - ADAM_B2) * jnp.square(gv)
        m_hat = m2 / (1.0 - ADAM_B1 ** ADAM_STEP)
        v_hat = v2 / (1.0 - ADAM_B2 ** ADAM_STEP)
        d_ref[...] = -ADAM_LR * (m_hat / (jnp.sqrt(v_hat) + ADAM_EPS) + ADAM_WD * w_ref[...])
        m2_ref[...] = m2
        v2_ref[...] = v2

    blk = pl.BlockSpec((tr, C), lambda i: (i, 0))
    return _pcall(body, name=name, grid=(R // tr,), in_specs=[blk] * 4, out_specs=[blk] * 3,
                  out_shape=[jax.ShapeDtypeStruct((R, C), F32)] * 3, compiler_params=_params(("parallel",)))(w, g, m, v)


def _sum_core_pair(arr, got, place, name, tr=128):
    P, hr, C = got.shape
    assert hr % tr == 0
    nt = hr // tr

    def body(place_ref, a_ref, g_ref, o_ref):
        o_ref[...] = (a_ref[...].astype(F32) + g_ref[...].astype(F32)).astype(o_ref.dtype)

    spec = pltpu.PrefetchScalarGridSpec(
        num_scalar_prefetch=1, grid=(P, nt),
        in_specs=[pl.BlockSpec((None, tr, C), lambda j, i, pr: (j, pr[1] * nt + i, 0)),
                  pl.BlockSpec((None, tr, C), lambda j, i, pr: (j, i, 0))],
        out_specs=pl.BlockSpec((None, tr, C), lambda j, i, pr: (j, i, 0)))
    return _pcall(body, name=name, grid_spec=spec, out_shape=jax.ShapeDtypeStruct(got.shape, BF16),
                  compiler_params=_params(("parallel", "parallel")))(place, arr, got)


def _sum_chips(pair, landed, place, name, tr=128):
    _, R, C = landed.shape
    assert R % tr == 0

    def body(place_ref, p_ref, l_ref, o_ref):
        acc = p_ref[...].astype(F32)
        for k in range(3):
            acc = acc + l_ref[k].astype(F32)
        o_ref[...] = acc

    spec = pltpu.PrefetchScalarGridSpec(
        num_scalar_prefetch=1, grid=(R // tr,),
        in_specs=[pl.BlockSpec((None, tr, C), lambda i, pr: (pr[0], i, 0)), pl.BlockSpec((3, tr, C), lambda i, pr: (0, i, 0))],
        out_specs=pl.BlockSpec((None, tr, C), lambda i, pr: (pr[1], i, 0)))
    return _pcall(body, name=name, grid_spec=spec, out_shape=jax.ShapeDtypeStruct((2, R, C), F32),
                  compiler_params=_params(("parallel",)))(place, pair, landed)


def _sum_parts(parts, name, out_dtype, tr=128):
    P, R, C = parts.shape
    tr = min(tr, R)
    assert R % tr == 0, (R, tr)

    def body(p_ref, o_ref):
        acc = p_ref[0].astype(F32)
        for k in range(1, P):
            acc = acc + p_ref[k].astype(F32)
        o_ref[...] = acc.astype(o_ref.dtype)

    return _pcall(body, name=name, grid=(R // tr,), in_specs=[pl.BlockSpec((P, tr, C), lambda i: (0, i, 0))],
                  out_specs=pl.BlockSpec((tr, C), lambda i: (i, 0)),
                  out_shape=jax.ShapeDtypeStruct((R, C), out_dtype), compiler_params=_params(("parallel",)))(parts)


def _place():
    x, y, c = lax.axis_index("x"), lax.axis_index("y"), lax.axis_index("c")
    others = [(1 - x, y), (x, 1 - y), (1 - x, 1 - y)]
    return x, y, c, others


def _half_rows(ref, hh, lead=()):
    hr = ref.shape[-2] // 2
    return ref.at[(*lead, pl.ds(pl.multiple_of(hh * hr, 16), hr), slice(None))]


def _sem_arrays(*counts):
    return [pltpu.SemaphoreType.DMA((k,)) for k in counts]


SEM_SPEC = pl.BlockSpec(memory_space=pltpu.SEMAPHORE)
ANY_SPEC = pl.BlockSpec(memory_space=pl.ANY)
DATAFLOW = pltpu.SideEffectType.DATAFLOW_SIDE_EFFECTING


def _in_hbm(a):
    return pltpu.with_memory_space_constraint(a, pltpu.HBM)


def _gather_copies(srcs, lands, send_sems, recv_sems):
    x, y, c, others = _place()
    me = 2 * x + y

    def copy(w, k, dst_chip, to):
        return pltpu.make_async_remote_copy(src_ref=_half_rows(srcs[w], c), dst_ref=_half_rows(lands[w], c, (dst_chip,)),
                                            send_sem=send_sems.at[3 * w + k], recv_sem=recv_sems.at[3 * w + k],
                                            device_id=to, device_id_type=MESH)

    pairs = [(w, k, cx, cy) for w in range(len(srcs)) for k, (cx, cy) in enumerate(others)]
    return ([copy(w, k, me, (cx, cy, c)) for w, k, cx, cy in pairs],
            [copy(w, k, 2 * cx + cy, (cx, cy, c)) for w, k, cx, cy in pairs])


def _gather_start(shards, name):
    n = len(shards)

    def body(*refs):
        srcs, lands, send_sems, recv_sems, token = refs[:n], refs[n:2 * n], refs[2 * n], refs[2 * n + 1], refs[-1]
        for cp in _gather_copies(srcs, lands, send_sems, recv_sems)[0]:
            cp.start()
        token[...] = jnp.zeros_like(token)

    lands = [lax.empty((N_CHIPS,) + s.shape, s.dtype) for s in shards]
    outs = _pcall(
        body, name=name, in_specs=[HBM_SPEC] * (2 * n),
        out_specs=[SEM_SPEC, SEM_SPEC] + [HBM_SPEC] * (2 * n) + [VMEM_SPEC],
        out_shape=[pltpu.SemaphoreType.DMA((3 * n,)), pltpu.SemaphoreType.DMA((3 * n,))]
        + [pltpu.HBM(a.shape, a.dtype) for a in list(shards) + lands] + [jax.ShapeDtypeStruct((8, LANES), F32)],
        input_output_aliases={i: 2 + i for i in range(2 * n)},
        compiler_params=pltpu.CompilerParams(has_side_effects=DATAFLOW),
    )(*[_in_hbm(a) for a in list(shards) + lands])
    return outs[0], outs[1], outs[2:2 + n], outs[2 + n:2 + 2 * n], outs[-1]


def _gather_wait(started, after, name):
    send_sems, recv_sems, srcs, lands, _ = started
    n = len(srcs)

    def body(*refs):
        src_refs, land_refs, send_ref, recv_ref = refs[:n], refs[n:2 * n], refs[2 * n], refs[2 * n + 1]
        outgoing, incoming = _gather_copies(src_refs, land_refs, send_ref, recv_ref)
        for out_cp, in_cp in zip(outgoing, incoming):
            out_cp.wait_send()
            in_cp.wait_recv()

    outs = _pcall(
        body, name=name, in_specs=[HBM_SPEC] * (2 * n) + [SEM_SPEC, SEM_SPEC, ANY_SPEC], out_specs=[HBM_SPEC] * (2 * n),
        out_shape=[pltpu.HBM(a.shape, a.dtype) for a in list(srcs) + list(lands)],
        input_output_aliases={i: i for i in range(2 * n)},
        compiler_params=pltpu.CompilerParams(has_side_effects=DATAFLOW),
    )(*srcs, *lands, send_sems, recv_sems, after)
    return outs[:n], outs[n:]


def _gather_pass_on(shards, lands, name):
    n = len(shards)
    per = 4

    def body(*refs):
        srcs, bufs = refs[:n], refs[2 * n:3 * n]
        send_sems, recv_sems = refs[3 * n:]
        x, y, c, others = _place()
        me = 2 * x + y
        sibling = (x, y, 1 - c)

        def copy(w, k, src, dst):
            return pltpu.make_async_remote_copy(src_ref=src, dst_ref=dst, send_sem=send_sems.at[per * w + k],
                                                recv_sem=recv_sems.at[per * w + k], device_id=sibling, device_id_type=MESH)

        sends, recvs = [], []
        for w in range(n):
            for k, (cx, cy) in enumerate(others):
                mine, theirs = _half_rows(bufs[w], c, (2 * cx + cy,)), _half_rows(bufs[w], 1 - c, (2 * cx + cy,))
                sends.append(copy(w, k, mine, mine))
                recvs.append(copy(w, k, theirs, theirs))
            sends.append(copy(w, 3, srcs[w], bufs[w].at[me]))
            recvs.append(sends[-1])
        for cp in sends:
            cp.start()
        for snd, rcv in zip(sends, recvs):
            snd.wait_send()
            rcv.wait_recv()

    return _pcall(body, name=name, in_specs=[HBM_SPEC] * (2 * n), out_specs=[HBM_SPEC] * n,
                  out_shape=[jax.ShapeDtypeStruct(l.shape, l.dtype) for l in lands],
                  input_output_aliases={n + w: w for w in range(n)},
                  scratch_shapes=_sem_arrays(per * n, per * n))(*shards, *lands)


def _scatter_copies(srcs, lands, send_sems, recv_sems):
    x, y, c, others = _place()
    return [pltpu.make_async_remote_copy(src_ref=srcs[w].at[2 * cx + cy], dst_ref=lands[w].at[k],
                                         send_sem=send_sems.at[3 * w + k], recv_sem=recv_sems.at[3 * w + k],
                                         device_id=(cx, cy, c), device_id_type=MESH)
            for w in range(len(srcs)) for k, (cx, cy) in enumerate(others)]


def _scatter_start(parts, name):
    n = len(parts)

    def body(*refs):
        srcs, lands, send_sems, recv_sems = refs[:n], refs[n:2 * n], refs[2 * n], refs[2 * n + 1]
        for cp in _scatter_copies(srcs, lands, send_sems, recv_sems):
            cp.start()

    lands = [lax.empty((3,) + p.shape[1:], p.dtype) for p in parts]
    outs = _pcall(
        body, name=name, in_specs=[HBM_SPEC] * (2 * n), out_specs=[SEM_SPEC, SEM_SPEC] + [HBM_SPEC] * (2 * n),
        out_shape=[pltpu.SemaphoreType.DMA((3 * n,)), pltpu.SemaphoreType.DMA((3 * n,))]
        + [pltpu.HBM(a.shape, a.dtype) for a in list(parts) + lands],
        input_output_aliases={i: 2 + i for i in range(2 * n)},
        compiler_params=pltpu.CompilerParams(has_side_effects=DATAFLOW),
    )(*[_in_hbm(a) for a in list(parts) + lands])
    return outs[0], outs[1], outs[2:2 + n], outs[2 + n:]


def _scatter_wait(started, after, name):
    send_sems, recv_sems, srcs, lands = started
    n = len(srcs)

    def body(*refs):
        for cp in _scatter_copies(refs[:n], refs[n:2 * n], refs[2 * n], refs[2 * n + 1]):
            cp.wait_send()
            cp.wait_recv()

    outs = _pcall(
        body, name=name, in_specs=[HBM_SPEC] * (2 * n) + [SEM_SPEC, SEM_SPEC, ANY_SPEC], out_specs=[HBM_SPEC] * (2 * n),
        out_shape=[pltpu.HBM(a.shape, a.dtype) for a in list(srcs) + list(lands)],
        input_output_aliases={i: i for i in range(2 * n)},
        compiler_params=pltpu.CompilerParams(has_side_effects=DATAFLOW),
    )(*srcs, *lands, send_sems, recv_sems, after)
    return outs[:n], outs[n:]


def _allgather_group(shards, name):
    n = len(shards)
    per = 7

    def body(*refs):
        ins, outs = refs[:n], refs[n:2 * n]
        send_sems, recv_sems = refs[2 * n:]
        x, y, c, others = _place()
        me = 2 * x + y
        sibling = (x, y, 1 - c)

        def copy(w, k, src, dst, to):
            return pltpu.make_async_remote_copy(src_ref=src, dst_ref=dst, send_sem=send_sems.at[per * w + k],
                                                recv_sem=recv_sems.at[per * w + k], device_id=to, device_id_type=MESH)

        first = [copy(w, k, _half_rows(ins[w], c), _half_rows(outs[w], c, (me,)), (cx, cy, c))
                 for w in range(n) for k, (cx, cy) in enumerate(others)]
        own = [copy(w, 6, ins[w], outs[w].at[me], sibling) for w in range(n)]
        for cp in first + own:
            cp.start()
        passed = []
        for w in range(n):
            for k, (cx, cy) in enumerate(others):
                landed = _half_rows(outs[w], c, (2 * cx + cy,))
                copy(w, k, landed, landed, sibling).wait_recv()
                passed.append(copy(w, 3 + k, landed, landed, sibling))
                passed[-1].start()
        for w in range(n):
            for k, (cx, cy) in enumerate(others):
                theirs = _half_rows(outs[w], 1 - c, (2 * cx + cy,))
                copy(w, 3 + k, theirs, theirs, sibling).wait_recv()
            own[w].wait_recv()
        for cp in first + passed + own:
            cp.wait_send()

    return _pcall(body, name=name, in_specs=[HBM_SPEC] * n, out_specs=[HBM_SPEC] * n,
                  out_shape=[jax.ShapeDtypeStruct((N_CHIPS,) + s.shape, s.dtype) for s in shards],
                  scratch_shapes=_sem_arrays(per * n, per * n))(*shards)


def _swap_halves_group(arrs, name):
    n = len(arrs)

    def body(*refs):
        ins, gots = refs[:n], refs[n:2 * n]
        send_sems, recv_sems = refs[2 * n:]
        x, y, c, _ = _place()
        swaps = [pltpu.make_async_remote_copy(src_ref=_half_rows(ins[w], 1 - c, (slice(None),)), dst_ref=gots[w],
                                              send_sem=send_sems.at[w], recv_sem=recv_sems.at[w],
                                              device_id=(x, y, 1 - c), device_id_type=MESH) for w in range(n)]
        for cp in swaps:
            cp.start()
        for cp in swaps:
            cp.wait()

    half_shapes = [jax.ShapeDtypeStruct((a.shape[0], a.shape[1] // 2, a.shape[2]), a.dtype) for a in arrs]
    return _pcall(body, name=name, in_specs=[HBM_SPEC] * n, out_specs=[HBM_SPEC] * n, out_shape=half_shapes,
                  scratch_shapes=_sem_arrays(n, n))(*arrs)


def _scatter_group(parts, name):
    n = len(parts)

    def body(*refs):
        ins, outs = refs[:n], refs[n:2 * n]
        send_sems, recv_sems = refs[2 * n:]
        x, y, c, others = _place()

        def copy(w, k, src_chip, to):
            return pltpu.make_async_remote_copy(src_ref=ins[w].at[src_chip], dst_ref=outs[w].at[k],
                                                send_sem=send_sems.at[3 * w + k], recv_sem=recv_sems.at[3 * w + k],
                                                device_id=to, device_id_type=MESH)

        sends = [copy(w, k, 2 * cx + cy, (cx, cy, c)) for w in range(n) for k, (cx, cy) in enumerate(others)]
        for cp in sends:
            cp.start()
        for cp in sends:
            cp.wait()

    return _pcall(body, name=name, in_specs=[HBM_SPEC] * n, out_specs=[HBM_SPEC] * n,
                  out_shape=[jax.ShapeDtypeStruct((3,) + p.shape[1:], p.dtype) for p in parts],
                  scratch_shapes=_sem_arrays(3 * n, 3 * n))(*parts)


def _share_halves_group(halves, name):
    n = len(halves)

    def body(*refs):
        bufs = refs[n:2 * n]
        send_sems, recv_sems = refs[2 * n:]
        x, y, c, _ = _place()
        swaps = [pltpu.make_async_remote_copy(src_ref=bufs[w].at[c], dst_ref=bufs[w].at[c], send_sem=send_sems.at[w],
                                              recv_sem=recv_sems.at[w], device_id=(x, y, 1 - c), device_id_type=MESH)
                 for w in range(n)]
        for cp in swaps:
            cp.start()
        for w in range(n):
            swaps[w].wait_send()
            pltpu.make_async_remote_copy(src_ref=bufs[w].at[c], dst_ref=bufs[w].at[1 - c], send_sem=send_sems.at[w],
                                         recv_sem=recv_sems.at[w], device_id=(x, y, 1 - c), device_id_type=MESH).wait_recv()

    return _pcall(body, name=name, in_specs=[HBM_SPEC] * n, out_specs=[HBM_SPEC] * n,
                  out_shape=[jax.ShapeDtypeStruct(h.shape, h.dtype) for h in halves],
                  input_output_aliases={w: w for w in range(n)},
                  scratch_shapes=_sem_arrays(n, n))(*halves)


def _allgather_small(blk, name):
    M, C = blk.shape

    def body(x_ref, out_ref, send_sems, recv_sems, local_sem):
        x, y, c, others = _place()
        me, sibling = (x, y, c), (x, y, 1 - c)

        def rows(px, py, pc):
            return out_ref.at[4 * px + 2 * py + pc]

        def copy(k, block, to, src=None):
            return pltpu.make_async_remote_copy(src_ref=rows(*block) if src is None else src, dst_ref=rows(*block),
                                                send_sem=send_sems.at[k], recv_sem=recv_sems.at[k], device_id=to, device_id_type=MESH)

        mine = pltpu.make_async_copy(x_ref, rows(*me), local_sem)
        mine.start()
        first = [copy(0, me, sibling, src=x_ref)]
        first += [copy(1 + j, me, (*chip, c), src=x_ref) for j, chip in enumerate(others)]
        for cp in first:
            cp.start()
        passed = [copy(4 + j, (*chip, c), sibling) for j, chip in enumerate(others)]
        for j, chip in enumerate(others):
            copy(1 + j, (*chip, c), me).wait_recv()
            passed[j].start()
        copy(0, sibling, me).wait_recv()
        for j, chip in enumerate(others):
            copy(4 + j, (*chip, 1 - c), me).wait_recv()
        for cp in first + passed:
            cp.wait_send()
        mine.wait()

    return _pcall(body, name=name, in_specs=[VMEM_SPEC], out_specs=VMEM_SPEC,
                  out_shape=jax.ShapeDtypeStruct((8, M, C), blk.dtype),
                  scratch_shapes=[pltpu.SemaphoreType.DMA((7,)), pltpu.SemaphoreType.DMA((7,)), pltpu.SemaphoreType.DMA])(blk)


def _pack_rows(n_elems, width=PACK_W, align=PACK_ROW_ALIGN):
    rows = -(-n_elems // width)
    return -(-rows // align) * align


def _pack(arrays, dtype, width=PACK_W, align=PACK_ROW_ALIGN):
    flat = jnp.concatenate([a.astype(dtype).reshape(-1) for a in arrays])
    rows = _pack_rows(flat.shape[0], width, align)
    flat = jnp.pad(flat, (0, rows * width - flat.shape[0]))
    return flat.reshape(rows, width)


def _pack_small(arrays):
    return _pack(arrays, F32, width=128, align=8)


def _unpack(flat, shapes):
    out, off = [], 0
    for shp in shapes:
        n = int(np.prod(shp))
        out.append(flat[..., off:off + n].reshape(flat.shape[:-1] + tuple(shp)))
        off += n
    return out


def _doubled_heads(x2d, n_heads):
    S = x2d.shape[0]
    h = x2d.reshape(S, n_heads, HEAD_DIM).transpose(1, 0, 2)
    return jnp.concatenate([h, h], axis=-1)


def _mlp_fwd(h, g, w_up4, w_down, tag):
    (n,) = _rms_fwd(h, [g], f"rms_mlp{tag}")
    u, a = _matmul(n, w_up4, "nn", f"up{tag}", out_dtypes=(F32, BF16), chipwise="b",
                   epilogue=lambda acc: (acc, jnp.square(jnp.maximum(acc, 0.0))))
    h_out = _matmul(a, w_down, "nn", f"down{tag}", extras=(h,), epilogue=lambda acc, res: (res + acc,), tk=1024)
    return h_out, (n, u, a)


def _mlp_bwd(dh_out, h, g, w_up4, w_down, saved, tag):
    n, u, a = saved
    dw_down = _matmul(a, dh_out, "tn", f"dw_down{tag}", out_dtypes=(BF16,))
    du = _matmul(dh_out, w_down, "nt", f"du{tag}", out_dtypes=(BF16,), extras=(u,),
                 epilogue=lambda acc, uu: (acc * (2.0 * jnp.maximum(uu, 0.0)),))
    dw_up = _matmul(n, du, "tn", f"dw_up{tag}", out_dtypes=(BF16,), chipwise="out")
    dn = _matmul(du, w_up4, "nt", f"dn_mlp{tag}", tk=w_up4.shape[2], chipwise="b")
    dh, (dg,) = _rms_bwd(h, dh_out, [g], [dn], f"rms_mlp_bwd{tag}")
    return dh, dg, dw_up, dw_down


def kernel(x, g_attn, g_mlp, w_in_a, b_f, gq_a, gk_a, w_out_a, g_kv, w_kv, gk_b, w_q_b, gq_b, sinks, rel_bias, w_out_b, w_up, w_down, loss_target, m_g_attn, m_g_mlp, m_w_in_a, m_b_f, m_gq_a, m_gk_a, m_w_out_a, m_g_kv, m_w_kv, m_gk_b, m_w_q_b, m_gq_b, m_sinks, m_rel_bias, m_w_out_b, m_w_up, m_w_down, v_g_attn, v_g_mlp, v_w_in_a, v_b_f, v_gq_a, v_gk_a, v_w_out_a, v_g_kv, v_w_kv, v_gk_b, v_w_q_b, v_gq_b, v_sinks, v_rel_bias, v_w_out_b, v_w_up, v_w_down):
    given = dict(locals())
    S, D = x.shape[1], x.shape[2]
    H = D // HEAD_DIM
    KVH = w_kv.shape[1] // (2 * HEAD_DIM)
    kvw = KVH * HEAD_DIM
    hw = H * HEAD_DIM
    W = WINDOW
    nb = S // W
    c_idx = lax.axis_index("c")
    xs, tgt = x[0], loss_target[0]

    shards = {"w_in_a": w_in_a[0], "w_out_a": w_out_a[0], "w_up0": w_up[0], "w_down0": w_down[0], "w_kv": w_kv,
              "w_q_b": w_q_b[0], "w_out_b": w_out_b[0], "w_up1": w_up[1], "w_down1": w_down[1]}
    parts = list(shards)
    lane_pad = lambda a: jnp.pad(a, [(0, 0)] * (a.ndim - 1) + [(0, (-a.shape[-1]) % LANES)])
    n_in_shard = w_in_a.shape[2]
    groups = [("w_in_a", "w_out_a"), ("w_up0", "w_down0"), ("w_kv", "w_q_b", "w_out_b", "w_up1", "w_down1")]
    started = [_gather_start([lane_pad(shards[n].astype(BF16)) for n in grp], f"gather_start{i}") for i, grp in enumerate(groups)]
    gathered = {}

    def finish_gather(i, after):
        srcs, lands = _gather_wait(started[i], after, f"gather_wait{i}")
        gathered.update(zip(groups[i], _gather_pass_on(srcs, lands, f"gather_pass_on{i}")))

    vec = lambda a: a.reshape(1, -1)
    twice = lambda a: jnp.tile(a.reshape(1, -1), (1, 2))

    g_attn0 = vec(g_attn[0]) + sum(st[4][0, 0] for st in started)
    finish_gather(0, g_attn0)
    win = jnp.moveaxis(gathered["w_in_a"][:, :, :n_in_shard], 0, 1).reshape(D, -1)
    win = jnp.pad(win, ((0, 0), (0, (-win.shape[1]) % 128)))
    wout_a = gathered["w_out_a"].reshape(-1, D)
    n_in = win.shape[1]
    tile_in = 640 if n_in % 640 == 0 else 128
    (n0,) = _rms_fwd(xs, [g_attn0], "rms_attn0")
    proj = _matmul(n0, win, "nn", "proj_in", tn=tile_in)
    zt = proj[:, 3 * hw:3 * hw + H].T
    c_row = _gate_fwd(zt, b_f.reshape(H, 1), "gate_fwd")
    c_col3, c_row3 = c_row.reshape(H, S, 1), c_row.reshape(H, 1, S)
    o_a, lse_a = _fox_fwd(proj, c_col3, c_row3, twice(gq_a[0]), twice(gk_a[0]), H, "fox_fwd")
    h1 = _matmul(o_a, wout_a, "nn", "out_a", extras=(xs,), epilogue=lambda acc, res: (res + acc,))
    finish_gather(1, h1)
    wup = [gathered["w_up0"], None]
    wdown = [gathered["w_down0"].reshape(-1, D), None]
    h2, mlp0 = _mlp_fwd(h1, vec(g_mlp[0]), wup[0], wdown[0], "0")

    finish_gather(2, h2)
    wq_b, wout_b = gathered["w_q_b"].reshape(-1, D), gathered["w_out_b"].reshape(-1, D)
    wkv = gathered["w_kv"].reshape(D, -1)
    wup[1], wdown[1] = gathered["w_up1"], gathered["w_down1"].reshape(-1, D)
    nkv, n2 = _rms_fwd(h2, [vec(g_kv), vec(g_attn[1])], "rms_attn1")
    kv = _matmul(nkv, wkv, "nn", "proj_kv")
    kk, vv = _doubled_heads(kv[:, :kvw], KVH), _doubled_heads(kv[:, kvw:], KVH)
    q2 = _matmul(n2, wq_b, "nn", "proj_q")
    onehot = jnp.asarray(_bucket_onehot(), dtype=BF16)
    bias = _bias_expand(rel_bias.T, onehot, "bias_expand").reshape(H, W, 2 * W)
    bias_ab = bias.reshape(H // 2, 2 * W, 2 * W)
    bias_t_ab = bias.reshape(H // 2, 2, W, 2 * W).transpose(0, 3, 1, 2).reshape(H // 2, 2 * W, 2 * W)
    sink_ab = jnp.repeat(sinks[0].reshape(H // 2, 2), W, axis=1)
    o_b, lse_b = _swa_fwd(q2, kk, vv, bias_ab, sink_ab.reshape(H // 2, 2 * W, 1), twice(gq_b[0]), twice(gk_b), "swa_fwd")
    h3 = _matmul(o_b, wout_b, "nn", "out_b", extras=(h2,), epilogue=lambda acc, res: (res + acc,))
    h4, mlp1 = _mlp_fwd(h3, vec(g_mlp[1]), wup[1], wdown[1], "1")

    dh4, loss_part = _loss_head(h4, tgt, "loss_head")
    loss = lax.psum(loss_part[0, 0], ("x", "y", "c"))

    place = jnp.stack([2 * lax.axis_index("x") + lax.axis_index("y"), c_idx]).astype(jnp.int32)
    scattering = []

    def start_reduce(named):
        names = list(named)
        got = _swap_halves_group([named[n] for n in names], "swap_grad_halves_" + names[0])
        pair_sums = [_sum_core_pair(named[n], g, place, "sum_core_pair_" + n) for n, g in zip(names, got)]
        scattering.append((names, _scatter_start(pair_sums, "scatter_start_" + names[0])))

    dh3, dg_mlp1, dw_up1, dw_down1 = _mlp_bwd(dh4, h3, vec(g_mlp[1]), wup[1], wdown[1], mlp1, "1")
    start_reduce({"w_down1": dw_down1.reshape(N_CHIPS, -1, D), "w_up1": dw_up1})
    dw_out_b = _matmul(o_b, dh3, "tn", "dw_out_b", out_dtypes=(BF16,))
    do_b = _matmul(dh3, wout_b, "nt", "do_b")
    dq2, dk2, dv2, dbias_t_ab, dsink, dgq_b, dgk_b = _swa_bwd(
        q2, kk, vv, bias_t_ab, sink_ab.reshape(H // 2, 1, 2 * W), twice(gq_b[0]), twice(gk_b),
        lse_b.reshape(H // 2, nb, 1, 2 * W), do_b, "swa_bwd")
    dbias = dbias_t_ab.reshape(H // 2, 2 * W, 2, W).transpose(0, 2, 3, 1).reshape(H, W * 2 * W)
    d_rel_bias = _bias_reduce(dbias, onehot, "bias_reduce").T
    dw_q_b = _matmul(n2, dq2, "tn", "dw_q_b", out_dtypes=(BF16,))
    dn2 = _matmul(dq2, wq_b, "nt", "dn2")
    dkv = jnp.concatenate([dk2[h, :, :HEAD_DIM] for h in range(KVH)] + [dv2[h, :, :HEAD_DIM] for h in range(KVH)],
                          axis=1).astype(BF16)
    dw_kv = _matmul(nkv, dkv, "tn", "dw_kv", out_dtypes=(BF16,))
    dnkv = _matmul(dkv, wkv, "nt", "dnkv")
    start_reduce({"w_out_b": dw_out_b.reshape(N_CHIPS, -1, D), "w_q_b": dw_q_b.reshape(N_CHIPS, -1, D),
                  "w_kv": dw_kv.reshape(N_CHIPS, -1, 2 * kvw)})
    dh2, (dg_kv, dg_attn1) = _rms_bwd(h2, dh3, [vec(g_kv), vec(g_attn[1])], [dnkv, dn2], "rms_attn1_bwd")

    dh1, dg_mlp0, dw_up0, dw_down0 = _mlp_bwd(dh2, h1, vec(g_mlp[0]), wup[0], wdown[0], mlp0, "0")
    start_reduce({"w_down0": dw_down0.reshape(N_CHIPS, -1, D), "w_up0": dw_up0})
    dw_out_a = _matmul(o_a, dh1, "tn", "dw_out_a", out_dtypes=(BF16,))
    do_a = _matmul(dh1, wout_a, "nt", "do_a")
    dq_a, dk_a, dv_a, dc_col, dgq_a, dgk_a = _fox_bwd(
        proj, c_col3, c_row3, twice(gq_a[0]), twice(gk_a[0]), lse_a.reshape(H, 1, S), do_a, H, "fox_bwd")
    dzt, db_f = _gate_bwd(dc_col.reshape(H, S), zt, b_f.reshape(H, 1), "gate_bwd")
    dproj = jnp.concatenate([dq_a, dk_a, dv_a, dzt.T.astype(BF16), jnp.zeros((S, n_in - 3 * hw - H), BF16)], axis=1)
    dw_in = _matmul(n0, dproj, "tn", "dw_in", out_dtypes=(BF16,), tn=tile_in)
    dw_in4 = lane_pad(dw_in[:, :3 * hw + H].reshape(D, N_CHIPS, -1).transpose(1, 0, 2))
    start_reduce({"w_out_a": dw_out_a.reshape(N_CHIPS, -1, D), "w_in_a": dw_in4})
    dn0 = _matmul(dproj, win, "nt", "dn0", tk=tile_in)
    grad_x, (dg_attn0,) = _rms_bwd(xs, dh1, [vec(g_attn[0])], [dn0], "rms_attn0_bwd")

    reduced = {}
    for names, started_scatter in scattering:
        pair_sums, landed = _scatter_wait(started_scatter, grad_x, "scatter_wait_" + names[0])
        halves = [_sum_chips(p, l, place, "sum_chips_" + n) for n, p, l in zip(names, pair_sums, landed)]
        for n, r in zip(names, _share_halves_group(halves, "share_reduced_halves_" + names[0])):
            reduced[n] = r.reshape(-1, r.shape[2])[:, :shards[n].shape[1]]
    big_grads = [reduced["w_in_a"][None], reduced["w_out_a"][None], reduced["w_kv"], reduced["w_q_b"][None],
                 reduced["w_out_b"][None], jnp.stack([reduced["w_up0"], reduced["w_up1"]]),
                 jnp.stack([reduced["w_down0"], reduced["w_down1"]])]

    small_grads = {
        "g_attn": jnp.concatenate([dg_attn0, dg_attn1], axis=0), "g_mlp": jnp.concatenate([dg_mlp0, dg_mlp1], axis=0),
        "b_f": db_f.reshape(1, H), "gq_a": dgq_a[:, :HEAD_DIM], "gk_a": dgk_a[:, :HEAD_DIM], "g_kv": dg_kv.reshape(-1),
        "gk_b": dgk_b[0, :HEAD_DIM], "gq_b": dgq_b[:, :HEAD_DIM], "sinks": dsink[:, 0, 0].reshape(1, H), "rel_bias": d_rel_bias,
    }
    small_shapes = [given[n].shape for n in SMALL]
    spack = _pack_small([small_grads[n] for n in SMALL])
    small_sum = _sum_parts(_allgather_small(spack, "allgather_small"), "sum_small", F32, tr=spack.shape[0])
    small_red = _unpack(small_sum.reshape(-1), small_shapes)

    grads = dict(zip([n for n, _ in BIG], big_grads))
    grads.update(dict(zip(SMALL, small_red)))
    sw = _pack_small([given[n] for n in SMALL])
    sm = _pack_small([given["m_" + n] for n in SMALL])
    sv = _pack_small([given["v_" + n] for n in SMALL])
    sd, sm2, sv2 = _adamw(sw, small_sum, sm, sv, "adamw_small", tr=sw.shape[0])
    delta = dict(zip(SMALL, _unpack(sd.reshape(-1), small_shapes)))
    new_m = dict(zip(SMALL, _unpack(sm2.reshape(-1), small_shapes)))
    new_v = dict(zip(SMALL, _unpack(sv2.reshape(-1), small_shapes)))
    for n, _ in BIG:
        w = given[n]
        two_d = (-1, w.shape[-1])
        d, m2, v2 = _adamw(w.reshape(two_d), grads[n].reshape(two_d), given["m_" + n].reshape(two_d),
                           given["v_" + n].reshape(two_d), "adamw_" + n)
        delta[n], new_m[n], new_v[n] = d.reshape(w.shape), m2.reshape(w.shape), v2.reshape(w.shape)

    order = ["g_attn", "g_mlp", "w_in_a", "b_f", "gq_a", "gk_a", "w_out_a", "g_kv", "w_kv", "gk_b", "w_q_b", "gq_b",
             "sinks", "rel_bias", "w_out_b", "w_up", "w_down"]
    return (loss, grad_x[None], *[grads[n] for n in order], *[delta[n] for n in order],
            *[new_m[n] for n in order], *[new_v[n] for n in order])
```

```python
import numpy as np
import jax
import jax.numpy as jnp
from jax import lax
from jax.experimental import pallas as pl
from jax.experimental.pallas import tpu as pltpu

F32 = jnp.float32
BF16 = jnp.bfloat16
MESH = pl.DeviceIdType.MESH

HEAD_DIM = 64
LANES = 128
WINDOW = 128
N_BUCKETS = 32
REL_MAX_DIST = 128
NORM_EPS = 1e-6
ADAM_LR = 0.001
ADAM_B1 = 0.9
ADAM_B2 = 0.999
ADAM_EPS = 1e-08
ADAM_WD = 0.01
ADAM_STEP = 10
NEG = -1e30
N_CHIPS = 4
PACK_W = 1024
PACK_ROW_ALIGN = 256
VMEM_LIMIT = 56 * 1024 * 1024
HBM_SPEC = pl.BlockSpec(memory_space=pltpu.HBM)
VMEM_SPEC = pl.BlockSpec(memory_space=pltpu.VMEM)

BIG = (("w_in_a", 2), ("w_out_a", 1), ("w_kv", 0), ("w_q_b", 1), ("w_out_b", 1), ("w_up", 2), ("w_down", 1))
SMALL = ("g_attn", "g_mlp", "b_f", "gq_a", "gk_a", "g_kv", "gk_b", "gq_b", "sinks", "rel_bias")


def _pcall(body, **kw):
    return pl.pallas_call(body, **kw)


def _params(sem=None):
    return pltpu.CompilerParams(dimension_semantics=sem, vmem_limit_bytes=VMEM_LIMIT)


def _rinv(x):
    return lax.rsqrt(jnp.mean(x * x, axis=-1, keepdims=True) + NORM_EPS)


def _dot(a, b, dims, precision=None):
    return lax.dot_general(a, b, (dims, ((), ())), precision=precision, preferred_element_type=F32)


NN = ((1,), (0,))
NT = ((1,), (1,))
TN = ((0,), (0,))


def _accumulate(ref, val, first):
    @pl.when(first)
    def _():
        ref[...] = val

    @pl.when(jnp.logical_not(first))
    def _():
        ref[...] += val


def _matmul(a, b, mode, name, out_dtypes=(F32,), extras=(), epilogue=None, tm=512, tn=512, tk=None, chipwise=None):
    if chipwise == "b":
        nc = b.shape[2]
        M, K = a.shape
        (K2, N) = (b.shape[1], N_CHIPS * nc) if mode == "nn" else (N_CHIPS * nc, b.shape[1])
    elif mode == "nn":
        (M, K), (K2, N) = a.shape, b.shape
    elif mode == "nt":
        (M, K), (N, K2) = a.shape, b.shape
    else:
        (K, M), (K2, N) = a.shape, b.shape
    assert K == K2, (a.shape, b.shape, mode)
    tm, tn = min(tm, M), min(tn, N)
    tk = K if tk is None else tk
    assert M % tm == 0 and N % tn == 0 and K % tk == 0, (M, N, K, tm, tn, tk)
    nk = K // tk
    dims = {"nn": NN, "nt": NT, "tn": TN}[mode]
    a_spec = pl.BlockSpec((tk, tm), lambda i, j, k: (k, i)) if mode == "tn" else pl.BlockSpec((tm, tk), lambda i, j, k: (i, k))
    b_spec = pl.BlockSpec((tn, tk), lambda i, j, k: (j, k)) if mode == "nt" else pl.BlockSpec((tk, tn), lambda i, j, k: (k, j))
    o_spec = pl.BlockSpec((tm, tn), lambda i, j, k: (i, j))
    out_shape = (M, N)
    if chipwise == "b" and mode == "nn":
        per = nc // tn
        assert tk == K and nc % tn == 0
        b_spec = pl.BlockSpec((None, tk, tn), lambda i, j, k: (j // per, 0, j % per))
    elif chipwise == "b":
        assert mode == "nt" and tk == nc
        b_spec = pl.BlockSpec((None, tn, tk), lambda i, j, k: (k, j, 0))
    elif chipwise == "out":
        per = (N // N_CHIPS) // tn
        assert (N // N_CHIPS) % tn == 0
        o_spec = pl.BlockSpec((None, tm, tn), lambda i, j, k: (j // per, i, j % per))
        out_shape = (N_CHIPS, M, N // N_CHIPS)
        assert not extras
    n_ex, n_out = len(extras), len(out_dtypes)

    def body(*refs):
        a_ref, b_ref = refs[0], refs[1]
        ex_refs = refs[2:2 + n_ex]
        out_refs = refs[2 + n_ex:2 + n_ex + n_out]
        acc_ref = refs[2 + n_ex + n_out]
        k = pl.program_id(2)
        part = _dot(a_ref[...].astype(BF16), b_ref[...].astype(BF16), dims)

        @pl.when(k == 0)
        def _():
            acc_ref[...] = part

        @pl.when(k > 0)
        def _():
            acc_ref[...] += part

        @pl.when(k == nk - 1)
        def _():
            acc = acc_ref[...]
            outs = (acc,) if epilogue is None else epilogue(acc, *[r[...] for r in ex_refs])
            for r, o in zip(out_refs, outs):
                r[...] = o.astype(r.dtype)

    outs = _pcall(
        body, name=name, grid=(M // tm, N // tn, nk),
        in_specs=[a_spec, b_spec] + [o_spec] * n_ex,
        out_specs=[o_spec] * n_out,
        out_shape=[jax.ShapeDtypeStruct(out_shape, dt) for dt in out_dtypes],
        scratch_shapes=[pltpu.VMEM((tm, tn), F32)],
        compiler_params=_params(("parallel", "parallel", "arbitrary")),
    )(a, b, *extras)
    return outs[0] if n_out == 1 else outs


def _rms_fwd(x, gains, name, ts=256):
    S, D = x.shape
    ts = min(ts, S)
    n = len(gains)

    def body(*refs):
        x_ref, g_refs, o_refs = refs[0], refs[1:1 + n], refs[1 + n:]
        xv = x_ref[...]
        xh = xv * _rinv(xv)
        for g_ref, o_ref in zip(g_refs, o_refs):
            o_ref[...] = (xh * g_ref[...]).astype(BF16)

    row = pl.BlockSpec((ts, D), lambda i: (i, 0))
    vec = pl.BlockSpec((1, D), lambda i: (0, 0))
    return _pcall(body, name=name, grid=(S // ts,), in_specs=[row] + [vec] * n, out_specs=[row] * n,
                  out_shape=[jax.ShapeDtypeStruct((S, D), BF16)] * n, compiler_params=_params(("parallel",)))(x, *gains)


def _rms_bwd(x, dres, gains, dns, name, ts=256):
    S, D = x.shape
    ts = min(ts, S)
    n = len(gains)

    def body(*refs):
        x_ref, dres_ref = refs[0], refs[1]
        g_refs, dn_refs = refs[2:2 + n], refs[2 + n:2 + 2 * n]
        dx_ref, dg_refs = refs[2 + 2 * n], refs[3 + 2 * n:]
        xv = x_ref[...]
        r = _rinv(xv)
        xh = xv * r
        dx = dres_ref[...]
        first = pl.program_id(0) == 0
        for g_ref, dn_ref, dg_ref in zip(g_refs, dn_refs, dg_refs):
            dn = dn_ref[...].astype(F32)
            _accumulate(dg_ref, jnp.sum(dn * xh, axis=0, keepdims=True), first)
            dxh = dn * g_ref[...]
            dx = dx + r * (dxh - xh * jnp.mean(dxh * xh, axis=-1, keepdims=True))
        dx_ref[...] = dx

    row = pl.BlockSpec((ts, D), lambda i: (i, 0))
    vec = pl.BlockSpec((1, D), lambda i: (0, 0))
    outs = _pcall(body, name=name, grid=(S // ts,), in_specs=[row, row] + [vec] * n + [row] * n,
                  out_specs=[row] + [vec] * n,
                  out_shape=[jax.ShapeDtypeStruct((S, D), F32)] + [jax.ShapeDtypeStruct((1, D), F32)] * n,
                  compiler_params=_params(("arbitrary",)))(x, dres, *gains, *dns)
    return outs[0], outs[1:]


def _loss_head(h, tgt, name, ts=256):
    S, D = h.shape
    ts = min(ts, S)

    def body(h_ref, t_ref, dh_ref, loss_ref):
        err = h_ref[...] - t_ref[...]
        dh_ref[...] = err * (1.0 / D)
        part = 0.5 * jnp.sum(jnp.mean(err * err, axis=-1, keepdims=True), axis=0, keepdims=True)
        _accumulate(loss_ref, part, pl.program_id(0) == 0)

    row = pl.BlockSpec((ts, D), lambda i: (i, 0))
    return _pcall(body, name=name, grid=(S // ts,), in_specs=[row, row],
                  out_specs=[row, pl.BlockSpec((1, 1), lambda i: (0, 0))],
                  out_shape=[jax.ShapeDtypeStruct((S, D), F32), jax.ShapeDtypeStruct((1, 1), F32)],
                  compiler_params=_params(("arbitrary",)))(h, tgt)


def _gate_fwd(zt, bf, name):
    H, S = zt.shape
    nb = S // 128

    def body(z_ref, b_ref, c_ref):
        z = z_ref[...] + b_ref[...]
        lf = jnp.minimum(z, 0.0) - jnp.log(1.0 + jnp.exp(-jnp.abs(z)))
        upper = (lax.broadcasted_iota(jnp.int32, (128, 128), 0) <= lax.broadcasted_iota(jnp.int32, (128, 128), 1)).astype(F32)
        carry = jnp.zeros((H, 1), F32)
        for blk in range(nb):
            cs = _dot(lf[:, blk * 128:(blk + 1) * 128], upper, NN, precision=lax.Precision.HIGHEST) + carry
            c_ref[:, blk * 128:(blk + 1) * 128] = cs
            carry = cs[:, 127:128]

    return _pcall(body, name=name, in_specs=[VMEM_SPEC, VMEM_SPEC], out_specs=VMEM_SPEC,
                  out_shape=jax.ShapeDtypeStruct((H, S), F32))(zt, bf)


def _gate_bwd(dct, zt, bf, name):
    H, S = zt.shape
    nb = S // 128

    def body(dc_ref, z_ref, b_ref, dz_ref, db_ref):
        z = z_ref[...] + b_ref[...]
        e = jnp.exp(-jnp.abs(z))
        sig_neg = jnp.where(z >= 0, e, 1.0) / (1.0 + e)
        lower = (lax.broadcasted_iota(jnp.int32, (128, 128), 0) >= lax.broadcasted_iota(jnp.int32, (128, 128), 1)).astype(F32)
        dc = dc_ref[...]
        carry = jnp.zeros((H, 1), F32)
        db = jnp.zeros((H, 1), F32)
        for blk in reversed(range(nb)):
            sl = slice(blk * 128, (blk + 1) * 128)
            dlf = _dot(dc[:, sl], lower, NN, precision=lax.Precision.HIGHEST) + carry
            carry = dlf[:, 0:1]
            dz = dlf * sig_neg[:, sl]
            dz_ref[:, sl] = dz
            db = db + jnp.sum(dz, axis=1, keepdims=True)
        db_ref[...] = db

    return _pcall(body, name=name, in_specs=[VMEM_SPEC] * 3, out_specs=[VMEM_SPEC] * 2,
                  out_shape=[jax.ShapeDtypeStruct((H, S), F32), jax.ShapeDtypeStruct((H, 1), F32)])(dct, zt, bf)


def _lane_is_a():
    return lax.broadcasted_iota(jnp.int32, (1, LANES), 1) < HEAD_DIM


def _per_head_mean(x, is_a):
    sa = jnp.sum(jnp.where(is_a, x, 0.0), axis=-1, keepdims=True)
    sb = jnp.sum(jnp.where(is_a, 0.0, x), axis=-1, keepdims=True)
    return jnp.where(is_a, sa, sb) / HEAD_DIM


def _pair_norm(raw, gain, is_a):
    return raw * lax.rsqrt(_per_head_mean(raw * raw, is_a) + NORM_EPS) * gain


def _pair_norm_bwd(raw, gain, dnormed, is_a):
    r = lax.rsqrt(_per_head_mean(raw * raw, is_a) + NORM_EPS)
    xh = raw * r
    dgain = jnp.sum(dnormed * xh, axis=0, keepdims=True)
    dxh = dnormed * gain
    return r * (dxh - xh * _per_head_mean(dxh * xh, is_a)), dgain


def _fold_heads(x):
    i = lax.broadcasted_iota(jnp.int32, (LANES, LANES), 0)
    j = lax.broadcasted_iota(jnp.int32, (LANES, LANES), 1)
    fold = ((i == j) | (i == j + HEAD_DIM) | (i + HEAD_DIM == j)).astype(F32)
    return _dot(x, fold, NN, precision=lax.Precision.HIGHEST)


def _fold_row(ref):
    ref[...] = _fold_heads(jnp.broadcast_to(ref[...], (8, LANES)))[0:1, :]


def _tri_mask(t, keys_on_rows):
    r = lax.broadcasted_iota(jnp.int32, (t, t), 0)
    c = lax.broadcasted_iota(jnp.int32, (t, t), 1)
    return (r <= c) if keys_on_rows else (r >= c)


def _fox_fwd(proj, c_col, c_row, gq2, gk2, n_heads, name, t=256):
    S = proj.shape[0]
    H = n_heads
    P = H // 2
    t = min(t, S)
    nq = S // t

    def body(q_ref, k_ref, v_ref, cc_ref, cr_ref, gq_ref, gk_ref, o_ref, lse_ref, qs_s, kb_s, vb_s):
        is_a = _lane_is_a()
        qn = _pair_norm(q_ref[...], gq_ref[...], is_a) * 0.125
        qs_s[0] = jnp.where(is_a, qn, 0.0).astype(BF16)
        qs_s[1] = jnp.where(is_a, 0.0, qn).astype(BF16)
        kb_s[...] = _pair_norm(k_ref[...], gk_ref[...], is_a).astype(BF16)
        vb_s[...] = v_ref[...].astype(BF16)
        causal = _tri_mask(t, False)
        for i in range(nq):
            t0 = i * t
            rows = slice(t0, t0 + t)
            o_pair = None
            for a in range(2):
                qi = qs_s[a, rows, :]
                ci = cc_ref[a, rows, :]
                s_d = jnp.where(causal, _dot(qi, kb_s[rows, :], NT) + ci - cr_ref[a, :, rows], NEG)
                m = jnp.max(s_d, axis=-1, keepdims=True)
                if i > 0:
                    s_l = _dot(qi, kb_s[0:t0, :], NT) + ci - cr_ref[a, :, 0:t0]
                    m = jnp.maximum(m, jnp.max(s_l, axis=-1, keepdims=True))
                p_d = jnp.exp(s_d - m)
                l = jnp.sum(p_d, axis=-1, keepdims=True)
                acc = _dot(p_d.astype(BF16), vb_s[rows, :], NN)
                if i > 0:
                    p_l = jnp.exp(s_l - m)
                    l = l + jnp.sum(p_l, axis=-1, keepdims=True)
                    acc = acc + _dot(p_l.astype(BF16), vb_s[0:t0, :], NN)
                o_a = acc / l
                lse_ref[a, rows, :] = m + jnp.log(l)
                o_pair = o_a if a == 0 else jnp.where(is_a, o_pair, o_a)
            o_ref[rows, :] = o_pair.astype(BF16)

    def cols(off):
        return pl.BlockSpec((S, LANES), lambda p: (0, off + p))

    col = pl.BlockSpec((2, S, 1), lambda p: (p, 0, 0))
    rowv = pl.BlockSpec((2, 1, S), lambda p: (p, 0, 0))
    gain = pl.BlockSpec((1, LANES), lambda p: (0, 0))
    return _pcall(body, name=name, grid=(P,), in_specs=[cols(0), cols(P), cols(2 * P), col, rowv, gain, gain],
                  out_specs=[cols(0), col],
                  out_shape=[jax.ShapeDtypeStruct((S, H * HEAD_DIM), BF16), jax.ShapeDtypeStruct((H, S, 1), F32)],
                  scratch_shapes=[pltpu.VMEM((2, S, LANES), BF16), pltpu.VMEM((S, LANES), BF16), pltpu.VMEM((S, LANES), BF16)],
                  compiler_params=_params(("parallel",)))(proj, proj, proj, c_col, c_row, gq2, gk2)


def _fox_bwd(proj, c_col, c_row, gq2, gk2, lse_row, do, n_heads, name, t=256):
    S = proj.shape[0]
    H = n_heads
    P = H // 2
    t = min(t, S)
    nq = S // t
    assert t % LANES == 0

    def body(q_ref, k_ref, v_ref, cc_ref, cr_ref, gq_ref, gk_ref, lr_ref, do_ref,
             dq_ref, dk_ref, dv_ref, dc_ref, dgq_ref, dgk_ref,
             qs_s, kb_s, kt_s, vb_s, dob_s, dq_s, dk_s, dv_s, dcs_s):
        is_a = _lane_is_a()
        qn = _pair_norm(q_ref[...], gq_ref[...], is_a) * 0.125
        qs_s[0] = jnp.where(is_a, qn, 0.0).astype(BF16)
        qs_s[1] = jnp.where(is_a, 0.0, qn).astype(BF16)
        kn = _pair_norm(k_ref[...], gk_ref[...], is_a)
        kb_s[...] = kn.astype(BF16)
        kt_s[0] = jnp.where(is_a, kn, 0.0).T.astype(BF16)
        kt_s[1] = jnp.where(is_a, 0.0, kn).T.astype(BF16)
        vb_s[...] = v_ref[...].astype(BF16)
        dov = do_ref[...]
        dob_s[0] = jnp.where(is_a, dov, 0.0).astype(BF16)
        dob_s[1] = jnp.where(is_a, 0.0, dov).astype(BF16)
        dk_s[...] = jnp.zeros((S, LANES), F32)
        dv_s[...] = jnp.zeros((S, LANES), F32)
        dcs_s[...] = jnp.zeros((2, S, LANES), F32)
        causal = _tri_mask(t, True)
        for i in range(nq):
            t0 = i * t
            rows = slice(t0, t0 + t)
            dq_t = jnp.zeros((LANES, t), F32)
            for a in range(2):
                qi = qs_s[a, rows, :]
                doi = dob_s[a, rows, :]
                cri = cr_ref[a, :, rows]
                lri = lr_ref[a, :, rows]

                def probs(keys, masked, a=a, qi=qi, doi=doi, cri=cri, lri=lri):
                    p_t = jnp.exp(_dot(kb_s[keys, :], qi, NT) + cri - cc_ref[a, keys, :] - lri)
                    if masked:
                        p_t = jnp.where(causal, p_t, 0.0)
                    return p_t, _dot(vb_s[keys, :], doi, NT)

                parts = [(rows,) + probs(rows, True)]
                if i > 0:
                    parts.append((slice(0, t0),) + probs(slice(0, t0), False))
                delta = sum(jnp.sum(p_t * dp_t, axis=0, keepdims=True) for _, p_t, dp_t in parts)
                for keys, p_t, dp_t in parts:
                    ds_t = p_t * (dp_t - delta)
                    dsb = ds_t.astype(BF16)
                    dv_s[keys, :] += _dot(p_t.astype(BF16), doi, NN)
                    dk_s[keys, :] += _dot(dsb, qi, NN)
                    dq_t = dq_t + _dot(kt_s[a, :, keys], dsb, NN)
                    dcs_s[a, keys, :] += sum(ds_t[:, b * LANES:(b + 1) * LANES] for b in range(t // LANES))
            dq_s[rows, :] = dq_t.T
        first = pl.program_id(0) == 0
        last = pl.program_id(0) == P - 1
        dq_raw, dgq = _pair_norm_bwd(q_ref[...], gq_ref[...], dq_s[...] * 0.125, is_a)
        dq_ref[...] = dq_raw.astype(BF16)
        _accumulate(dgq_ref, dgq, first)
        dk_raw, dgk = _pair_norm_bwd(k_ref[...], gk_ref[...], dk_s[...], is_a)
        dk_ref[...] = dk_raw.astype(BF16)
        _accumulate(dgk_ref, dgk, first)
        dv_ref[...] = dv_s[...].astype(BF16)
        for a in range(2):
            dc_ref[a] = -jnp.sum(dcs_s[a], axis=1, keepdims=True)

        @pl.when(last)
        def _():
            _fold_row(dgq_ref)
            _fold_row(dgk_ref)

    def cols(off):
        return pl.BlockSpec((S, LANES), lambda p: (0, off + p))

    col = pl.BlockSpec((2, S, 1), lambda p: (p, 0, 0))
    rowv = pl.BlockSpec((2, 1, S), lambda p: (p, 0, 0))
    gain = pl.BlockSpec((1, LANES), lambda p: (0, 0))
    wide = jax.ShapeDtypeStruct((S, H * HEAD_DIM), BF16)
    gs = jax.ShapeDtypeStruct((1, LANES), F32)
    return _pcall(body, name=name, grid=(P,),
                  in_specs=[cols(0), cols(P), cols(2 * P), col, rowv, gain, gain, rowv, cols(0)],
                  out_specs=[cols(0), cols(0), cols(0), col, gain, gain],
                  out_shape=[wide, wide, wide, jax.ShapeDtypeStruct((H, S, 1), F32), gs, gs],
                  scratch_shapes=[pltpu.VMEM((2, S, LANES), BF16), pltpu.VMEM((S, LANES), BF16), pltpu.VMEM((2, LANES, S), BF16),
                                  pltpu.VMEM((S, LANES), BF16), pltpu.VMEM((2, S, LANES), BF16)]
                  + [pltpu.VMEM((S, LANES), F32)] * 3 + [pltpu.VMEM((2, S, LANES), F32)],
                  compiler_params=_params(("arbitrary",)))(proj, proj, proj, c_col, c_row, gq2, gk2, lse_row, do)


def _bucket_onehot():
    W = WINDOW
    dist = np.arange(W)[:, None] + W - np.arange(2 * W)[None, :]
    n = np.maximum(dist, 0)
    max_exact = N_BUCKETS // 2
    large = max_exact + (np.log(np.maximum(n, 1) / max_exact) / np.log(REL_MAX_DIST / max_exact)
                         * (N_BUCKETS - max_exact)).astype(np.int32)
    large = np.minimum(large, N_BUCKETS - 1)
    bucket = np.where(n < max_exact, n, large).astype(np.int32)
    valid = (dist >= 0) & (dist < W)
    onehot = (bucket[None] == np.arange(N_BUCKETS)[:, None, None]) & valid[None]
    return onehot.reshape(N_BUCKETS, W * 2 * W).astype(np.float32)


def _bias_expand(rel_bias_t, onehot, name, tn=4096):
    HQ, NB = rel_bias_t.shape
    L = onehot.shape[1]

    def body(r_ref, oh_ref, out_ref):
        out_ref[...] = _dot(r_ref[...], oh_ref[...].astype(F32), NN, precision=lax.Precision.HIGHEST)

    return _pcall(body, name=name, grid=(L // tn,),
                  in_specs=[pl.BlockSpec((HQ, NB), lambda i: (0, 0)), pl.BlockSpec((NB, tn), lambda i: (0, i))],
                  out_specs=pl.BlockSpec((HQ, tn), lambda i: (0, i)),
                  out_shape=jax.ShapeDtypeStruct((HQ, L), F32), compiler_params=_params(("parallel",)))(rel_bias_t, onehot)


def _bias_reduce(dbias, onehot, name, tk=4096):
    HQ, L = dbias.shape
    NB = onehot.shape[0]

    def body(d_ref, oh_ref, out_ref):
        part = _dot(d_ref[...], oh_ref[...].astype(F32), NT, precision=lax.Precision.HIGHEST)
        _accumulate(out_ref, part, pl.program_id(0) == 0)

    return _pcall(body, name=name, grid=(L // tk,),
                  in_specs=[pl.BlockSpec((HQ, tk), lambda i: (0, i)), pl.BlockSpec((NB, tk), lambda i: (0, i))],
                  out_specs=pl.BlockSpec((HQ, NB), lambda i: (0, 0)),
                  out_shape=jax.ShapeDtypeStruct((HQ, NB), F32), compiler_params=_params(("arbitrary",)))(dbias, onehot)


def _stacked_query_index(n_rows_or_cols_axis, shape):
    idx = lax.broadcasted_iota(jnp.int32, shape, n_rows_or_cols_axis)
    return jnp.where(idx >= WINDOW, idx - WINDOW, idx)


def _swa_fwd(qproj, kk, vv, bias_ab, sink_col, gq2, gk2, name):
    S, HQD = qproj.shape
    KVH = kk.shape[0]
    PP = HQD // LANES
    NP = PP // KVH
    W = WINDOW
    nb = S // W

    def body(q_ref, k_ref, v_ref, bias_ref, sink_ref, gq_ref, gk_ref, o_ref, lse_ref, qs_s, kb_s, vb_s):
        is_a = _lane_is_a()
        qn = _pair_norm(q_ref[...], gq_ref[...], is_a) * 0.125
        qs_s[0] = jnp.where(is_a, qn, 0.0).astype(BF16)
        qs_s[1] = jnp.where(is_a, 0.0, qn).astype(BF16)
        kb_s[...] = _pair_norm(k_ref[...], gk_ref[...], is_a).astype(BF16)
        vb_s[...] = v_ref[...].astype(BF16)
        sink = sink_ref[...]
        qi1 = _stacked_query_index(0, (2 * W, W))
        first_valid = lax.broadcasted_iota(jnp.int32, (2 * W, W), 1) <= qi1
        qi2 = _stacked_query_index(0, (2 * W, 2 * W))
        key2 = lax.broadcasted_iota(jnp.int32, (2 * W, 2 * W), 1)
        band_valid = (key2 > qi2) & (key2 <= qi2 + W)
        for n in range(nb):
            rows = slice(n * W, (n + 1) * W)
            keys = slice(0, W) if n == 0 else slice((n - 1) * W, (n + 1) * W)
            lhs = jnp.concatenate([qs_s[0, rows, :], qs_s[1, rows, :]], axis=0)
            s = _dot(lhs, kb_s[keys, :], NT) + (bias_ref[:, W:2 * W] if n == 0 else bias_ref[...])
            s = jnp.where(first_valid if n == 0 else band_valid, s, NEG)
            m = jnp.maximum(jnp.max(s, axis=-1, keepdims=True), sink)
            e = jnp.exp(s - m)
            l = jnp.sum(e, axis=-1, keepdims=True) + jnp.exp(sink - m)
            o_ab = _dot(e.astype(BF16), vb_s[keys, :], NN) / l
            o_ref[rows, :] = jnp.where(is_a, o_ab[0:W, :], o_ab[W:2 * W, :]).astype(BF16)
            lse_ref[n] = m + jnp.log(l)

    qcols = pl.BlockSpec((S, LANES), lambda a, g: (0, a * NP + g))
    kvs = pl.BlockSpec((None, S, LANES), lambda a, g: (a, 0, 0))
    gain = pl.BlockSpec((1, LANES), lambda a, g: (0, 0))
    return _pcall(body, name=name, grid=(KVH, NP),
                  in_specs=[qcols, kvs, kvs, pl.BlockSpec((None, 2 * W, 2 * W), lambda a, g: (a * NP + g, 0, 0)),
                            pl.BlockSpec((None, 2 * W, 1), lambda a, g: (a * NP + g, 0, 0)), gain, gain],
                  out_specs=[qcols, pl.BlockSpec((None, nb, 2 * W, 1), lambda a, g: (a * NP + g, 0, 0, 0))],
                  out_shape=[jax.ShapeDtypeStruct((S, HQD), BF16), jax.ShapeDtypeStruct((PP, nb, 2 * W, 1), F32)],
                  scratch_shapes=[pltpu.VMEM((2, S, LANES), BF16), pltpu.VMEM((S, LANES), BF16), pltpu.VMEM((S, LANES), BF16)],
                  compiler_params=_params(("parallel", "parallel")))(qproj, kk, vv, bias_ab, sink_col, gq2, gk2)


def _swa_bwd(qproj, kk, vv, bias_t_ab, sink_row, gq2, gk2, lse_row, do, name):
    S, HQD = qproj.shape
    KVH = kk.shape[0]
    PP = HQD // LANES
    NP = PP // KVH
    W = WINDOW
    nb = S // W

    def body(q_ref, k_ref, v_ref, bias_ref, sink_ref, gq_ref, gk_ref, lr_ref, do_ref,
             dq_ref, dk_ref, dv_ref, db_ref, dsink_ref, dgq_ref, dgk_ref,
             qs_s, kb_s, kt_s, vb_s, dob_s, dq_s, dk_s, dv_s):
        a, g = pl.program_id(0), pl.program_id(1)
        is_a = _lane_is_a()
        qn = _pair_norm(q_ref[...], gq_ref[...], is_a) * 0.125
        qs_s[0] = jnp.where(is_a, qn, 0.0).astype(BF16)
        qs_s[1] = jnp.where(is_a, 0.0, qn).astype(BF16)
        kn = _pair_norm(k_ref[...], gk_ref[...], is_a)
        kb_s[...] = kn.astype(BF16)
        kt_s[...] = kn.T.astype(BF16)
        vb_s[...] = v_ref[...].astype(BF16)
        dov = do_ref[...]
        dob_s[0] = jnp.where(is_a, dov, 0.0).astype(BF16)
        dob_s[1] = jnp.where(is_a, 0.0, dov).astype(BF16)
        sink = sink_ref[...]

        @pl.when(g == 0)
        def _():
            dk_s[...] = jnp.zeros((S, LANES), F32)
            dv_s[...] = jnp.zeros((S, LANES), F32)

        qi1 = _stacked_query_index(1, (W, 2 * W))
        first_valid = lax.broadcasted_iota(jnp.int32, (W, 2 * W), 0) <= qi1
        qi2 = _stacked_query_index(1, (2 * W, 2 * W))
        key2 = lax.broadcasted_iota(jnp.int32, (2 * W, 2 * W), 0)
        band_valid = (key2 > qi2) & (key2 <= qi2 + W)
        head_rows = lax.broadcasted_iota(jnp.int32, (LANES, W), 0) < HEAD_DIM
        db = jnp.zeros((2 * W, 2 * W), F32)
        dsk = jnp.zeros((1, 2 * W), F32)
        pend_k = pend_v = None
        for n in range(nb):
            rows = slice(n * W, (n + 1) * W)
            keys = slice(0, W) if n == 0 else slice((n - 1) * W, (n + 1) * W)
            lhs_q = jnp.concatenate([qs_s[0, rows, :], qs_s[1, rows, :]], axis=0)
            lhs_do = jnp.concatenate([dob_s[0, rows, :], dob_s[1, rows, :]], axis=0)
            lse = lr_ref[n]
            s_t = _dot(kb_s[keys, :], lhs_q, NT) + (bias_ref[W:2 * W, :] if n == 0 else bias_ref[...])
            p_t = jnp.where(first_valid if n == 0 else band_valid, jnp.exp(s_t - lse), 0.0)
            dp_t = _dot(vb_s[keys, :], lhs_do, NT)
            delta = jnp.sum(p_t * dp_t, axis=0, keepdims=True)
            ds_t = p_t * (dp_t - delta)
            dsb = ds_t.astype(BF16)
            dsk = dsk - jnp.exp(sink - lse) * delta
            dv_band = _dot(p_t.astype(BF16), lhs_do, NN)
            dk_band = _dot(dsb, lhs_q, NN)
            dq_t = _dot(kt_s[:, keys], dsb, NN)
            dq_s[rows, :] = jnp.where(head_rows, dq_t[:, 0:W], dq_t[:, W:2 * W]).T
            if n == 0:
                db = jnp.concatenate([jnp.zeros((W, 2 * W), F32), ds_t], axis=0)
                pend_k, pend_v = dk_band, dv_band
            else:
                db = db + ds_t
                prev = slice((n - 1) * W, n * W)
                dk_s[prev, :] += pend_k + dk_band[0:W, :]
                dv_s[prev, :] += pend_v + dv_band[0:W, :]
                pend_k, pend_v = dk_band[W:2 * W, :], dv_band[W:2 * W, :]
        tail = slice((nb - 1) * W, nb * W)
        dk_s[tail, :] += pend_k
        dv_s[tail, :] += pend_v
        db_ref[...] = db
        dsink_ref[0] = jnp.broadcast_to(jnp.sum(dsk[:, 0:W], axis=1, keepdims=True), (1, LANES))
        dsink_ref[1] = jnp.broadcast_to(jnp.sum(dsk[:, W:2 * W], axis=1, keepdims=True), (1, LANES))
        dq_raw, dgq = _pair_norm_bwd(q_ref[...], gq_ref[...], dq_s[...] * 0.125, is_a)
        dq_ref[...] = dq_raw.astype(BF16)
        _accumulate(dgq_ref, dgq, jnp.logical_and(a == 0, g == 0))

        @pl.when(jnp.logical_and(a == KVH - 1, g == NP - 1))
        def _():
            _fold_row(dgq_ref)

        @pl.when(g == NP - 1)
        def _():
            dk_raw, dgk = _pair_norm_bwd(k_ref[...], gk_ref[...], _fold_heads(dk_s[...]), is_a)
            dk_ref[...] = dk_raw
            _accumulate(dgk_ref, dgk, a == 0)
            dv_ref[...] = _fold_heads(dv_s[...])

    qcols = pl.BlockSpec((S, LANES), lambda a, g: (0, a * NP + g))
    kvs = pl.BlockSpec((None, S, LANES), lambda a, g: (a, 0, 0))
    sq = pl.BlockSpec((None, 2 * W, 2 * W), lambda a, g: (a * NP + g, 0, 0))
    gain = pl.BlockSpec((1, LANES), lambda a, g: (0, 0))
    ks = jax.ShapeDtypeStruct((KVH, S, LANES), F32)
    gs = jax.ShapeDtypeStruct((1, LANES), F32)
    return _pcall(body, name=name, grid=(KVH, NP),
                  in_specs=[qcols, kvs, kvs, sq, pl.BlockSpec((None, 1, 2 * W), lambda a, g: (a * NP + g, 0, 0)), gain, gain,
                            pl.BlockSpec((None, nb, 1, 2 * W), lambda a, g: (a * NP + g, 0, 0, 0)), qcols],
                  out_specs=[qcols, kvs, kvs, sq, pl.BlockSpec((2, 1, LANES), lambda a, g: (a * NP + g, 0, 0)), gain, gain],
                  out_shape=[jax.ShapeDtypeStruct((S, HQD), BF16), ks, ks, jax.ShapeDtypeStruct((PP, 2 * W, 2 * W), F32),
                             jax.ShapeDtypeStruct((2 * PP, 1, LANES), F32), gs, gs],
                  scratch_shapes=[pltpu.VMEM((2, S, LANES), BF16), pltpu.VMEM((S, LANES), BF16), pltpu.VMEM((LANES, S), BF16),
                                  pltpu.VMEM((S, LANES), BF16), pltpu.VMEM((2, S, LANES), BF16)] + [pltpu.VMEM((S, LANES), F32)] * 3,
                  compiler_params=_params(("arbitrary", "arbitrary")))(qproj, kk, vv, bias_t_ab, sink_row, gq2, gk2, lse_row, do)


def _adamw(w, g, m, v, name, tr=256):
    R, C = w.shape
    tr = min(tr, R)
    assert R % tr == 0

    def body(w_ref, g_ref, m_ref, v_ref, d_ref, m2_ref, v2_ref):
        gv = g_ref[...]
        m2 = ADAM_B1 * m_ref[...] + (1.0 - ADAM_B1) * gv
        v2 = ADAM_B2 * v_ref[...] + (1.0 - ADAM_B2) * jnp.square(gv)
        m_hat = m2 / (1.0 - ADAM_B1 ** ADAM_STEP)
        v_hat = v2 / (1.0 - ADAM_B2 ** ADAM_STEP)
        d_ref[...] = -ADAM_LR * (m_hat / (jnp.sqrt(v_hat) + ADAM_EPS) + ADAM_WD * w_ref[...])
        m2_ref[...] = m2
        v2_ref[...] = v2

    blk = pl.BlockSpec((tr, C), lambda i: (i, 0))
    return _pcall(body, name=name, grid=(R // tr,), in_specs=[blk] * 4, out_specs=[blk] * 3,
                  out_shape=[jax.ShapeDtypeStruct((R, C), F32)] * 3, compiler_params=_params(("parallel",)))(w, g, m, v)


def _sum_core_pair(arr, got, place, name, tr=128):
    P, hr, C = got.shape
    assert hr % tr == 0
    nt = hr // tr

    def body(place_ref, a_ref, g_ref, o_ref):
        o_ref[...] = (a_ref[...].astype(F32) + g_ref[...].astype(F32)).astype(o_ref.dtype)

    spec = pltpu.PrefetchScalarGridSpec(
        num_scalar_prefetch=1, grid=(P, nt),
        in_specs=[pl.BlockSpec((None, tr, C), lambda j, i, pr: (j, pr[1] * nt + i, 0)),
                  pl.BlockSpec((None, tr, C), lambda j, i, pr: (j, i, 0))],
        out_specs=pl.BlockSpec((None, tr, C), lambda j, i, pr: (j, i, 0)))
    return _pcall(body, name=name, grid_spec=spec, out_shape=jax.ShapeDtypeStruct(got.shape, BF16),
                  compiler_params=_params(("parallel", "parallel")))(place, arr, got)


def _sum_chips(pair, landed, place, name, tr=128):
    _, R, C = landed.shape
    assert R % tr == 0

    def body(place_ref, p_ref, l_ref, o_ref):
        acc = p_ref[...].astype(F32)
        for k in range(3):
            acc = acc + l_ref[k].astype(F32)
        o_ref[...] = acc

    spec = pltpu.PrefetchScalarGridSpec(
        num_scalar_prefetch=1, grid=(R // tr,),
        in_specs=[pl.BlockSpec((None, tr, C), lambda i, pr: (pr[0], i, 0)), pl.BlockSpec((3, tr, C), lambda i, pr: (0, i, 0))],
        out_specs=pl.BlockSpec((None, tr, C), lambda i, pr: (pr[1], i, 0)))
    return _pcall(body, name=name, grid_spec=spec, out_shape=jax.ShapeDtypeStruct((2, R, C), F32),
                  compiler_params=_params(("parallel",)))(place, pair, landed)


def _sum_parts(parts, name, out_dtype, tr=128):
    P, R, C = parts.shape
    tr = min(tr, R)
    assert R % tr == 0, (R, tr)

    def body(p_ref, o_ref):
        acc = p_ref[0].astype(F32)
        for k in range(1, P):
            acc = acc + p_ref[k].astype(F32)
        o_ref[...] = acc.astype(o_ref.dtype)

    return _pcall(body, name=name, grid=(R // tr,), in_specs=[pl.BlockSpec((P, tr, C), lambda i: (0, i, 0))],
                  out_specs=pl.BlockSpec((tr, C), lambda i: (i, 0)),
                  out_shape=jax.ShapeDtypeStruct((R, C), out_dtype), compiler_params=_params(("parallel",)))(parts)


def _place():
    x, y, c = lax.axis_index("x"), lax.axis_index("y"), lax.axis_index("c")
    others = [(1 - x, y), (x, 1 - y), (1 - x, 1 - y)]
    return x, y, c, others


def _half_rows(ref, hh, lead=()):
    hr = ref.shape[-2] // 2
    return ref.at[(*lead, pl.ds(pl.multiple_of(hh * hr, 16), hr), slice(None))]


def _sem_arrays(*counts):
    return [pltpu.SemaphoreType.DMA((k,)) for k in counts]


SEM_SPEC = pl.BlockSpec(memory_space=pltpu.SEMAPHORE)
ANY_SPEC = pl.BlockSpec(memory_space=pl.ANY)
DATAFLOW = pltpu.SideEffectType.DATAFLOW_SIDE_EFFECTING


def _in_hbm(a):
    return pltpu.with_memory_space_constraint(a, pltpu.HBM)


def _gather_copies(srcs, lands, send_sems, recv_sems):
    x, y, c, others = _place()
    me = 2 * x + y

    def copy(w, k, dst_chip, to):
        return pltpu.make_async_remote_copy(src_ref=_half_rows(srcs[w], c), dst_ref=_half_rows(lands[w], c, (dst_chip,)),
                                            send_sem=send_sems.at[3 * w + k], recv_sem=recv_sems.at[3 * w + k],
                                            device_id=to, device_id_type=MESH)

    pairs = [(w, k, cx, cy) for w in range(len(srcs)) for k, (cx, cy) in enumerate(others)]
    return ([copy(w, k, me, (cx, cy, c)) for w, k, cx, cy in pairs],
            [copy(w, k, 2 * cx + cy, (cx, cy, c)) for w, k, cx, cy in pairs])


def _gather_start(shards, name):
    n = len(shards)

    def body(*refs):
        srcs, lands, send_sems, recv_sems, token = refs[:n], refs[n:2 * n], refs[2 * n], refs[2 * n + 1], refs[-1]
        for cp in _gather_copies(srcs, lands, send_sems, recv_sems)[0]:
            cp.start()
        token[...] = jnp.zeros_like(token)

    lands = [lax.empty((N_CHIPS,) + s.shape, s.dtype) for s in shards]
    outs = _pcall(
        body, name=name, in_specs=[HBM_SPEC] * (2 * n),
        out_specs=[SEM_SPEC, SEM_SPEC] + [HBM_SPEC] * (2 * n) + [VMEM_SPEC],
        out_shape=[pltpu.SemaphoreType.DMA((3 * n,)), pltpu.SemaphoreType.DMA((3 * n,))]
        + [pltpu.HBM(a.shape, a.dtype) for a in list(shards) + lands] + [jax.ShapeDtypeStruct((8, LANES), F32)],
        input_output_aliases={i: 2 + i for i in range(2 * n)},
        compiler_params=pltpu.CompilerParams(has_side_effects=DATAFLOW),
    )(*[_in_hbm(a) for a in list(shards) + lands])
    return outs[0], outs[1], outs[2:2 + n], outs[2 + n:2 + 2 * n], outs[-1]


def _gather_wait(started, after, name):
    send_sems, recv_sems, srcs, lands, _ = started
    n = len(srcs)

    def body(*refs):
        src_refs, land_refs, send_ref, recv_ref = refs[:n], refs[n:2 * n], refs[2 * n], refs[2 * n + 1]
        outgoing, incoming = _gather_copies(src_refs, land_refs, send_ref, recv_ref)
        for out_cp, in_cp in zip(outgoing, incoming):
            out_cp.wait_send()
            in_cp.wait_recv()

    outs = _pcall(
        body, name=name, in_specs=[HBM_SPEC] * (2 * n) + [SEM_SPEC, SEM_SPEC, ANY_SPEC], out_specs=[HBM_SPEC] * (2 * n),
        out_shape=[pltpu.HBM(a.shape, a.dtype) for a in list(srcs) + list(lands)],
        input_output_aliases={i: i for i in range(2 * n)},
        compiler_params=pltpu.CompilerParams(has_side_effects=DATAFLOW),
    )(*srcs, *lands, send_sems, recv_sems, after)
    return outs[:n], outs[n:]


def _gather_pass_on(shards, lands, name):
    n = len(shards)
    per = 4

    def body(*refs):
        srcs, bufs = refs[:n], refs[2 * n:3 * n]
        send_sems, recv_sems = refs[3 * n:]
        x, y, c, others = _place()
        me = 2 * x + y
        sibling = (x, y, 1 - c)

        def copy(w, k, src, dst):
            return pltpu.make_async_remote_copy(src_ref=src, dst_ref=dst, send_sem=send_sems.at[per * w + k],
                                                recv_sem=recv_sems.at[per * w + k], device_id=sibling, device_id_type=MESH)

        sends, recvs = [], []
        for w in range(n):
            for k, (cx, cy) in enumerate(others):
                mine, theirs = _half_rows(bufs[w], c, (2 * cx + cy,)), _half_rows(bufs[w], 1 - c, (2 * cx + cy,))
                sends.append(copy(w, k, mine, mine))
                recvs.append(copy(w, k, theirs, theirs))
            sends.append(copy(w, 3, srcs[w], bufs[w].at[me]))
            recvs.append(sends[-1])
        for cp in sends:
            cp.start()
        for snd, rcv in zip(sends, recvs):
            snd.wait_send()
            rcv.wait_recv()

    return _pcall(body, name=name, in_specs=[HBM_SPEC] * (2 * n), out_specs=[HBM_SPEC] * n,
                  out_shape=[jax.ShapeDtypeStruct(l.shape, l.dtype) for l in lands],
                  input_output_aliases={n + w: w for w in range(n)},
                  scratch_shapes=_sem_arrays(per * n, per * n))(*shards, *lands)


def _scatter_copies(srcs, lands, send_sems, recv_sems):
    x, y, c, others = _place()
    return [pltpu.make_async_remote_copy(src_ref=srcs[w].at[2 * cx + cy], dst_ref=lands[w].at[k],
                                         send_sem=send_sems.at[3 * w + k], recv_sem=recv_sems.at[3 * w + k],
                                         device_id=(cx, cy, c), device_id_type=MESH)
            for w in range(len(srcs)) for k, (cx, cy) in enumerate(others)]


def _scatter_start(parts, name):
    n = len(parts)

    def body(*refs):
        srcs, lands, send_sems, recv_sems, token = refs[:n], refs[n:2 * n], refs[2 * n], refs[2 * n + 1], refs[-1]
        for cp in _scatter_copies(srcs, lands, send_sems, recv_sems):
            cp.start()
        token[...] = jnp.zeros_like(token)

    lands = [lax.empty((3,) + p.shape[1:], p.dtype) for p in parts]
    outs = _pcall(
        body, name=name, in_specs=[HBM_SPEC] * (2 * n), out_specs=[SEM_SPEC, SEM_SPEC] + [HBM_SPEC] * (2 * n) + [VMEM_SPEC],
        out_shape=[pltpu.SemaphoreType.DMA((3 * n,)), pltpu.SemaphoreType.DMA((3 * n,))]
        + [pltpu.HBM(a.shape, a.dtype) for a in list(parts) + lands] + [jax.ShapeDtypeStruct((8, LANES), F32)],
        input_output_aliases={i: 2 + i for i in range(2 * n)},
        compiler_params=pltpu.CompilerParams(has_side_effects=DATAFLOW),
    )(*[_in_hbm(a) for a in list(parts) + lands])
    return (outs[0], outs[1], outs[2:2 + n], outs[2 + n:2 + 2 * n]), outs[-1]


def _scatter_wait(started, after, name):
    send_sems, recv_sems, srcs, lands = started
    n = len(srcs)

    def body(*refs):
        for cp in _scatter_copies(refs[:n], refs[n:2 * n], refs[2 * n], refs[2 * n + 1]):
            cp.wait_send()
            cp.wait_recv()

    outs = _pcall(
        body, name=name, in_specs=[HBM_SPEC] * (2 * n) + [SEM_SPEC, SEM_SPEC, ANY_SPEC], out_specs=[HBM_SPEC] * (2 * n),
        out_shape=[pltpu.HBM(a.shape, a.dtype) for a in list(srcs) + list(lands)],
        input_output_aliases={i: i for i in range(2 * n)},
        compiler_params=pltpu.CompilerParams(has_side_effects=DATAFLOW),
    )(*srcs, *lands, send_sems, recv_sems, after)
    return outs[:n], outs[n:]


def _allgather_group(shards, name):
    n = len(shards)
    per = 7

    def body(*refs):
        ins, outs = refs[:n], refs[n:2 * n]
        send_sems, recv_sems = refs[2 * n:]
        x, y, c, others = _place()
        me = 2 * x + y
        sibling = (x, y, 1 - c)

        def copy(w, k, src, dst, to):
            return pltpu.make_async_remote_copy(src_ref=src, dst_ref=dst, send_sem=send_sems.at[per * w + k],
                                                recv_sem=recv_sems.at[per * w + k], device_id=to, device_id_type=MESH)

        first = [copy(w, k, _half_rows(ins[w], c), _half_rows(outs[w], c, (me,)), (cx, cy, c))
                 for w in range(n) for k, (cx, cy) in enumerate(others)]
        own = [copy(w, 6, ins[w], outs[w].at[me], sibling) for w in range(n)]
        for cp in first + own:
            cp.start()
        passed = []
        for w in range(n):
            for k, (cx, cy) in enumerate(others):
                landed = _half_rows(outs[w], c, (2 * cx + cy,))
                copy(w, k, landed, landed, sibling).wait_recv()
                passed.append(copy(w, 3 + k, landed, landed, sibling))
                passed[-1].start()
        for w in range(n):
            for k, (cx, cy) in enumerate(others):
                theirs = _half_rows(outs[w], 1 - c, (2 * cx + cy,))
                copy(w, 3 + k, theirs, theirs, sibling).wait_recv()
            own[w].wait_recv()
        for cp in first + passed + own:
            cp.wait_send()

    return _pcall(body, name=name, in_specs=[HBM_SPEC] * n, out_specs=[HBM_SPEC] * n,
                  out_shape=[jax.ShapeDtypeStruct((N_CHIPS,) + s.shape, s.dtype) for s in shards],
                  scratch_shapes=_sem_arrays(per * n, per * n))(*shards)


def _swap_halves_group(arrs, name):
    n = len(arrs)

    def body(*refs):
        ins, gots = refs[:n], refs[n:2 * n]
        send_sems, recv_sems = refs[2 * n:]
        x, y, c, _ = _place()
        swaps = [pltpu.make_async_remote_copy(src_ref=_half_rows(ins[w], 1 - c, (slice(None),)), dst_ref=gots[w],
                                              send_sem=send_sems.at[w], recv_sem=recv_sems.at[w],
                                              device_id=(x, y, 1 - c), device_id_type=MESH) for w in range(n)]
        for cp in swaps:
            cp.start()
        for cp in swaps:
            cp.wait()

    half_shapes = [jax.ShapeDtypeStruct((a.shape[0], a.shape[1] // 2, a.shape[2]), a.dtype) for a in arrs]
    return _pcall(body, name=name, in_specs=[HBM_SPEC] * n, out_specs=[HBM_SPEC] * n, out_shape=half_shapes,
                  scratch_shapes=_sem_arrays(n, n))(*arrs)


def _scatter_group(parts, name):
    n = len(parts)

    def body(*refs):
        ins, outs = refs[:n], refs[n:2 * n]
        send_sems, recv_sems = refs[2 * n:]
        x, y, c, others = _place()

        def copy(w, k, src_chip, to):
            return pltpu.make_async_remote_copy(src_ref=ins[w].at[src_chip], dst_ref=outs[w].at[k],
                                                send_sem=send_sems.at[3 * w + k], recv_sem=recv_sems.at[3 * w + k],
                                                device_id=to, device_id_type=MESH)

        sends = [copy(w, k, 2 * cx + cy, (cx, cy, c)) for w in range(n) for k, (cx, cy) in enumerate(others)]
        for cp in sends:
            cp.start()
        for cp in sends:
            cp.wait()

    return _pcall(body, name=name, in_specs=[HBM_SPEC] * n, out_specs=[HBM_SPEC] * n,
                  out_shape=[jax.ShapeDtypeStruct((3,) + p.shape[1:], p.dtype) for p in parts],
                  scratch_shapes=_sem_arrays(3 * n, 3 * n))(*parts)


def _share_halves_group(halves, name):
    n = len(halves)

    def body(*refs):
        bufs = refs[n:2 * n]
        send_sems, recv_sems = refs[2 * n:]
        x, y, c, _ = _place()
        swaps = [pltpu.make_async_remote_copy(src_ref=bufs[w].at[c], dst_ref=bufs[w].at[c], send_sem=send_sems.at[w],
                                              recv_sem=recv_sems.at[w], device_id=(x, y, 1 - c), device_id_type=MESH)
                 for w in range(n)]
        for cp in swaps:
            cp.start()
        for w in range(n):
            swaps[w].wait_send()
            pltpu.make_async_remote_copy(src_ref=bufs[w].at[c], dst_ref=bufs[w].at[1 - c], send_sem=send_sems.at[w],
                                         recv_sem=recv_sems.at[w], device_id=(x, y, 1 - c), device_id_type=MESH).wait_recv()

    return _pcall(body, name=name, in_specs=[HBM_SPEC] * n, out_specs=[HBM_SPEC] * n,
                  out_shape=[jax.ShapeDtypeStruct(h.shape, h.dtype) for h in halves],
                  input_output_aliases={w: w for w in range(n)},
                  scratch_shapes=_sem_arrays(n, n))(*halves)


def _allgather_small(blk, name):
    M, C = blk.shape

    def body(x_ref, out_ref, send_sems, recv_sems, local_sem):
        x, y, c, others = _place()
        me, sibling = (x, y, c), (x, y, 1 - c)

        def rows(px, py, pc):
            return out_ref.at[4 * px + 2 * py + pc]

        def copy(k, block, to, src=None):
            return pltpu.make_async_remote_copy(src_ref=rows(*block) if src is None else src, dst_ref=rows(*block),
                                                send_sem=send_sems.at[k], recv_sem=recv_sems.at[k], device_id=to, device_id_type=MESH)

        mine = pltpu.make_async_copy(x_ref, rows(*me), local_sem)
        mine.start()
        first = [copy(0, me, sibling, src=x_ref)]
        first += [copy(1 + j, me, (*chip, c), src=x_ref) for j, chip in enumerate(others)]
        for cp in first:
            cp.start()
        passed = [copy(4 + j, (*chip, c), sibling) for j, chip in enumerate(others)]
        for j, chip in enumerate(others):
            copy(1 + j, (*chip, c), me).wait_recv()
            passed[j].start()
        copy(0, sibling, me).wait_recv()
        for j, chip in enumerate(others):
            copy(4 + j, (*chip, 1 - c), me).wait_recv()
        for cp in first + passed:
            cp.wait_send()
        mine.wait()

    return _pcall(body, name=name, in_specs=[VMEM_SPEC], out_specs=VMEM_SPEC,
                  out_shape=jax.ShapeDtypeStruct((8, M, C), blk.dtype),
                  scratch_shapes=[pltpu.SemaphoreType.DMA((7,)), pltpu.SemaphoreType.DMA((7,)), pltpu.SemaphoreType.DMA])(blk)


def _pack_rows(n_elems, width=PACK_W, align=PACK_ROW_ALIGN):
    rows = -(-n_elems // width)
    return -(-rows // align) * align


def _pack(arrays, dtype, width=PACK_W, align=PACK_ROW_ALIGN):
    flat = jnp.concatenate([a.astype(dtype).reshape(-1) for a in arrays])
    rows = _pack_rows(flat.shape[0], width, align)
    flat = jnp.pad(flat, (0, rows * width - flat.shape[0]))
    return flat.reshape(rows, width)


def _pack_small(arrays):
    return _pack(arrays, F32, width=128, align=8)


def _unpack(flat, shapes):
    out, off = [], 0
    for shp in shapes:
        n = int(np.prod(shp))
        out.append(flat[..., off:off + n].reshape(flat.shape[:-1] + tuple(shp)))
        off += n
    return out


def _doubled_heads(x2d, n_heads):
    S = x2d.shape[0]
    h = x2d.reshape(S, n_heads, HEAD_DIM).transpose(1, 0, 2)
    return jnp.concatenate([h, h], axis=-1)


def _mlp_fwd(h, g, w_up4, w_down, tag):
    (n,) = _rms_fwd(h, [g], f"rms_mlp{tag}")
    u, a = _matmul(n, w_up4, "nn", f"up{tag}", out_dtypes=(F32, BF16), chipwise="b",
                   epilogue=lambda acc: (acc, jnp.square(jnp.maximum(acc, 0.0))))
    h_out = _matmul(a, w_down, "nn", f"down{tag}", extras=(h,), epilogue=lambda acc, res: (res + acc,), tk=1024)
    return h_out, (n, u, a)


def _mlp_bwd(dh_out, h, g, w_up4, w_down, saved, tag):
    n, u, a = saved
    dw_down = _matmul(a, dh_out, "tn", f"dw_down{tag}", out_dtypes=(BF16,))
    du = _matmul(dh_out, w_down, "nt", f"du{tag}", out_dtypes=(BF16,), extras=(u,),
                 epilogue=lambda acc, uu: (acc * (2.0 * jnp.maximum(uu, 0.0)),))
    dw_up = _matmul(n, du, "tn", f"dw_up{tag}", out_dtypes=(BF16,), chipwise="out")
    dn = _matmul(du, w_up4, "nt", f"dn_mlp{tag}", tk=w_up4.shape[2], chipwise="b")
    dh, (dg,) = _rms_bwd(h, dh_out, [g], [dn], f"rms_mlp_bwd{tag}")
    return dh, dg, dw_up, dw_down


def kernel(x, g_attn, g_mlp, w_in_a, b_f, gq_a, gk_a, w_out_a, g_kv, w_kv, gk_b, w_q_b, gq_b, sinks, rel_bias, w_out_b, w_up, w_down, loss_target, m_g_attn, m_g_mlp, m_w_in_a, m_b_f, m_gq_a, m_gk_a, m_w_out_a, m_g_kv, m_w_kv, m_gk_b, m_w_q_b, m_gq_b, m_sinks, m_rel_bias, m_w_out_b, m_w_up, m_w_down, v_g_attn, v_g_mlp, v_w_in_a, v_b_f, v_gq_a, v_gk_a, v_w_out_a, v_g_kv, v_w_kv, v_gk_b, v_w_q_b, v_gq_b, v_sinks, v_rel_bias, v_w_out_b, v_w_up, v_w_down):
    given = dict(locals())
    S, D = x.shape[1], x.shape[2]
    H = D // HEAD_DIM
    KVH = w_kv.shape[1] // (2 * HEAD_DIM)
    kvw = KVH * HEAD_DIM
    hw = H * HEAD_DIM
    W = WINDOW
    nb = S // W
    c_idx = lax.axis_index("c")
    xs, tgt = x[0], loss_target[0]

    shards = {"w_in_a": w_in_a[0], "w_out_a": w_out_a[0], "w_up0": w_up[0], "w_down0": w_down[0], "w_kv": w_kv,
              "w_q_b": w_q_b[0], "w_out_b": w_out_b[0], "w_up1": w_up[1], "w_down1": w_down[1]}
    parts = list(shards)
    lane_pad = lambda a: jnp.pad(a, [(0, 0)] * (a.ndim - 1) + [(0, (-a.shape[-1]) % LANES)])
    n_in_shard = w_in_a.shape[2]
    groups = [("w_in_a", "w_out_a"), ("w_up0", "w_down0"), ("w_kv", "w_q_b", "w_out_b", "w_up1", "w_down1")]
    started = [_gather_start([lane_pad(shards[n].astype(BF16)) for n in grp], f"gather_start{i}") for i, grp in enumerate(groups)]
    gathered = {}

    def finish_gather(i, after):
        srcs, lands = _gather_wait(started[i], after, f"gather_wait{i}")
        gathered.update(zip(groups[i], _gather_pass_on(srcs, lands, f"gather_pass_on{i}")))

    vec = lambda a: a.reshape(1, -1)
    twice = lambda a: jnp.tile(a.reshape(1, -1), (1, 2))

    g_attn0 = vec(g_attn[0]) + sum(st[4][0, 0] for st in started)
    finish_gather(0, g_attn0)
    win = jnp.moveaxis(gathered["w_in_a"][:, :, :n_in_shard], 0, 1).reshape(D, -1)
    win = jnp.pad(win, ((0, 0), (0, (-win.shape[1]) % 128)))
    wout_a = gathered["w_out_a"].reshape(-1, D)
    n_in = win.shape[1]
    tile_in = 640 if n_in % 640 == 0 else 128
    (n0,) = _rms_fwd(xs, [g_attn0], "rms_attn0")
    proj = _matmul(n0, win, "nn", "proj_in", tn=tile_in)
    zt = proj[:, 3 * hw:3 * hw + H].T
    c_row = _gate_fwd(zt, b_f.reshape(H, 1), "gate_fwd")
    c_col3, c_row3 = c_row.reshape(H, S, 1), c_row.reshape(H, 1, S)
    o_a, lse_a = _fox_fwd(proj, c_col3, c_row3, twice(gq_a[0]), twice(gk_a[0]), H, "fox_fwd")
    h1 = _matmul(o_a, wout_a, "nn", "out_a", extras=(xs,), epilogue=lambda acc, res: (res + acc,))
    finish_gather(1, h1)
    wup = [gathered["w_up0"], None]
    wdown = [gathered["w_down0"].reshape(-1, D), None]
    h2, mlp0 = _mlp_fwd(h1, vec(g_mlp[0]), wup[0], wdown[0], "0")

    finish_gather(2, h2)
    wq_b, wout_b = gathered["w_q_b"].reshape(-1, D), gathered["w_out_b"].reshape(-1, D)
    wkv = gathered["w_kv"].reshape(D, -1)
    wup[1], wdown[1] = gathered["w_up1"], gathered["w_down1"].reshape(-1, D)
    nkv, n2 = _rms_fwd(h2, [vec(g_kv), vec(g_attn[1])], "rms_attn1")
    kv = _matmul(nkv, wkv, "nn", "proj_kv")
    kk, vv = _doubled_heads(kv[:, :kvw], KVH), _doubled_heads(kv[:, kvw:], KVH)
    q2 = _matmul(n2, wq_b, "nn", "proj_q")
    onehot = jnp.asarray(_bucket_onehot(), dtype=BF16)
    bias = _bias_expand(rel_bias.T, onehot, "bias_expand").reshape(H, W, 2 * W)
    bias_ab = bias.reshape(H // 2, 2 * W, 2 * W)
    bias_t_ab = bias.reshape(H // 2, 2, W, 2 * W).transpose(0, 3, 1, 2).reshape(H // 2, 2 * W, 2 * W)
    sink_ab = jnp.repeat(sinks[0].reshape(H // 2, 2), W, axis=1)
    o_b, lse_b = _swa_fwd(q2, kk, vv, bias_ab, sink_ab.reshape(H // 2, 2 * W, 1), twice(gq_b[0]), twice(gk_b), "swa_fwd")
    h3 = _matmul(o_b, wout_b, "nn", "out_b", extras=(h2,), epilogue=lambda acc, res: (res + acc,))
    h4, mlp1 = _mlp_fwd(h3, vec(g_mlp[1]), wup[1], wdown[1], "1")

    dh4, loss_part = _loss_head(h4, tgt, "loss_head")
    loss = lax.psum(loss_part[0, 0], ("x", "y", "c"))

    place = jnp.stack([2 * lax.axis_index("x") + lax.axis_index("y"), c_idx]).astype(jnp.int32)
    scattering = []

    def start_reduce(named):
        names = list(named)
        got = _swap_halves_group([named[n] for n in names], "swap_grad_halves_" + names[0])
        pair_sums = [_sum_core_pair(named[n], g, place, "sum_core_pair_" + n) for n, g in zip(names, got)]
        started_scatter, token = _scatter_start(pair_sums, "scatter_start_" + names[0])
        scattering.append((names, started_scatter))
        return token[0:1, :]

    dh3, dg_mlp1, dw_up1, dw_down1 = _mlp_bwd(dh4, h3, vec(g_mlp[1]), wup[1], wdown[1], mlp1, "1")
    tie1 = start_reduce({"w_down1": dw_down1.reshape(N_CHIPS, -1, D), "w_up1": dw_up1})
    dw_out_b = _matmul(o_b, dh3, "tn", "dw_out_b", out_dtypes=(BF16,))
    do_b = _matmul(dh3, wout_b, "nt", "do_b")
    dq2, dk2, dv2, dbias_t_ab, dsink, dgq_b, dgk_b = _swa_bwd(
        q2, kk, vv, bias_t_ab, sink_ab.reshape(H // 2, 1, 2 * W), twice(gq_b[0]) + tie1, twice(gk_b),
        lse_b.reshape(H // 2, nb, 1, 2 * W), do_b, "swa_bwd")
    dbias = dbias_t_ab.reshape(H // 2, 2 * W, 2, W).transpose(0, 2, 3, 1).reshape(H, W * 2 * W)
    d_rel_bias = _bias_reduce(dbias, onehot, "bias_reduce").T
    dw_q_b = _matmul(n2, dq2, "tn", "dw_q_b", out_dtypes=(BF16,))
    dn2 = _matmul(dq2, wq_b, "nt", "dn2")
    dkv = jnp.concatenate([dk2[h, :, :HEAD_DIM] for h in range(KVH)] + [dv2[h, :, :HEAD_DIM] for h in range(KVH)],
                          axis=1).astype(BF16)
    dw_kv = _matmul(nkv, dkv, "tn", "dw_kv", out_dtypes=(BF16,))
    dnkv = _matmul(dkv, wkv, "nt", "dnkv")
    tie2 = start_reduce({"w_out_b": dw_out_b.reshape(N_CHIPS, -1, D), "w_q_b": dw_q_b.reshape(N_CHIPS, -1, D),
                         "w_kv": dw_kv.reshape(N_CHIPS, -1, 2 * kvw)})
    dh2, (dg_kv, dg_attn1) = _rms_bwd(h2, dh3, [vec(g_kv) + tie2[:, :1], vec(g_attn[1])], [dnkv, dn2], "rms_attn1_bwd")

    dh1, dg_mlp0, dw_up0, dw_down0 = _mlp_bwd(dh2, h1, vec(g_mlp[0]), wup[0], wdown[0], mlp0, "0")
    tie3 = start_reduce({"w_down0": dw_down0.reshape(N_CHIPS, -1, D), "w_up0": dw_up0})
    dw_out_a = _matmul(o_a, dh1, "tn", "dw_out_a", out_dtypes=(BF16,))
    do_a = _matmul(dh1, wout_a, "nt", "do_a")
    dq_a, dk_a, dv_a, dc_col, dgq_a, dgk_a = _fox_bwd(
        proj, c_col3, c_row3, twice(gq_a[0]) + tie3, twice(gk_a[0]), lse_a.reshape(H, 1, S), do_a, H, "fox_bwd")
    dzt, db_f = _gate_bwd(dc_col.reshape(H, S), zt, b_f.reshape(H, 1), "gate_bwd")
    dproj = jnp.concatenate([dq_a, dk_a, dv_a, dzt.T.astype(BF16), jnp.zeros((S, n_in - 3 * hw - H), BF16)], axis=1)
    dw_in = _matmul(n0, dproj, "tn", "dw_in", out_dtypes=(BF16,), tn=tile_in)
    dw_in4 = lane_pad(dw_in[:, :3 * hw + H].reshape(D, N_CHIPS, -1).transpose(1, 0, 2))
    tie4 = start_reduce({"w_out_a": dw_out_a.reshape(N_CHIPS, -1, D), "w_in_a": dw_in4})
    dn0 = _matmul(dproj, win, "nt", "dn0", tk=tile_in)
    grad_x, (dg_attn0,) = _rms_bwd(xs, dh1, [vec(g_attn[0]) + tie4[:, :1]], [dn0], "rms_attn0_bwd")

    reduced = {}
    for names, started_scatter in scattering:
        pair_sums, landed = _scatter_wait(started_scatter, grad_x, "scatter_wait_" + names[0])
        halves = [_sum_chips(p, l, place, "sum_chips_" + n) for n, p, l in zip(names, pair_sums, landed)]
        for n, r in zip(names, _share_halves_group(halves, "share_reduced_halves_" + names[0])):
            reduced[n] = r.reshape(-1, r.shape[2])[:, :shards[n].shape[1]]
    big_grads = [reduced["w_in_a"][None], reduced["w_out_a"][None], reduced["w_kv"], reduced["w_q_b"][None],
                 reduced["w_out_b"][None], jnp.stack([reduced["w_up0"], reduced["w_up1"]]),
                 jnp.stack([reduced["w_down0"], reduced["w_down1"]])]

    small_grads = {
        "g_attn": jnp.concatenate([dg_attn0, dg_attn1], axis=0), "g_mlp": jnp.concatenate([dg_mlp0, dg_mlp1], axis=0),
        "b_f": db_f.reshape(1, H), "gq_a": dgq_a[:, :HEAD_DIM], "gk_a": dgk_a[:, :HEAD_DIM], "g_kv": dg_kv.reshape(-1),
        "gk_b": dgk_b[0, :HEAD_DIM], "gq_b": dgq_b[:, :HEAD_DIM], "sinks": dsink[:, 0, 0].reshape(1, H), "rel_bias": d_rel_bias,
    }
    small_shapes = [given[n].shape for n in SMALL]
    spack = _pack_small([small_grads[n] for n in SMALL])
    small_sum = _sum_parts(_allgather_small(spack, "allgather_small"), "sum_small", F32, tr=spack.shape[0])
    small_red = _unpack(small_sum.reshape(-1), small_shapes)

    grads = dict(zip([n for n, _ in BIG], big_grads))
    grads.update(dict(zip(SMALL, small_red)))
    sw = _pack_small([given[n] for n in SMALL])
    sm = _pack_small([given["m_" + n] for n in SMALL])
    sv = _pack_small([given["v_" + n] for n in SMALL])
    sd, sm2, sv2 = _adamw(sw, small_sum, sm, sv, "adamw_small", tr=sw.shape[0])
    delta = dict(zip(SMALL, _unpack(sd.reshape(-1), small_shapes)))
    new_m = dict(zip(SMALL, _unpack(sm2.reshape(-1), small_shapes)))
    new_v = dict(zip(SMALL, _unpack(sv2.reshape(-1), small_shapes)))
    for n, _ in BIG:
        w = given[n]
        two_d = (-1, w.shape[-1])
        d, m2, v2 = _adamw(w.reshape(two_d), grads[n].reshape(two_d), given["m_" + n].reshape(two_d),
                           given["v_" + n].reshape(two_d), "adamw_" + n)
        delta[n], new_m[n], new_v[n] = d.reshape(w.shape), m2.reshape(w.shape), v2.reshape(w.shape)

    order = ["g_attn", "g_mlp", "w_in_a", "b_f", "gq_a", "gk_a", "w_out_a", "g_kv", "w_kv", "gk_b", "w_q_b", "gq_b",
             "sinks", "rel_bias", "w_out_b", "w_up", "w_down"]
    return (loss, grad_x[None], *[grads[n] for n in order], *[delta[n] for n in order],
            *[new_m[n] for n in order], *[new_v[n] for n in order])
```

```python
import numpy as np
import jax
import jax.numpy as jnp
from jax import lax
from jax.experimental import pallas as pl
from jax.experimental.pallas import tpu as pltpu

F32 = jnp.float32
BF16 = jnp.bfloat16
MESH = pl.DeviceIdType.MESH

HEAD_DIM = 64
LANES = 128
WINDOW = 128
N_BUCKETS = 32
REL_MAX_DIST = 128
NORM_EPS = 1e-6
ADAM_LR = 0.001
ADAM_B1 = 0.9
ADAM_B2 = 0.999
ADAM_EPS = 1e-08
ADAM_WD = 0.01
ADAM_STEP = 10
NEG = -1e30
N_CHIPS = 4
PACK_W = 1024
PACK_ROW_ALIGN = 256
VMEM_LIMIT = 56 * 1024 * 1024
HBM_SPEC = pl.BlockSpec(memory_space=pltpu.HBM)
VMEM_SPEC = pl.BlockSpec(memory_space=pltpu.VMEM)

BIG = (("w_in_a", 2), ("w_out_a", 1), ("w_kv", 0), ("w_q_b", 1), ("w_out_b", 1), ("w_up", 2), ("w_down", 1))
SMALL = ("g_attn", "g_mlp", "b_f", "gq_a", "gk_a", "g_kv", "gk_b", "gq_b", "sinks", "rel_bias")


def _pcall(body, **kw):
    return pl.pallas_call(body, **kw)


def _params(sem=None):
    return pltpu.CompilerParams(dimension_semantics=sem, vmem_limit_bytes=VMEM_LIMIT)


def _rinv(x):
    return lax.rsqrt(jnp.mean(x * x, axis=-1, keepdims=True) + NORM_EPS)


def _dot(a, b, dims, precision=None):
    return lax.dot_general(a, b, (dims, ((), ())), precision=precision, preferred_element_type=F32)


NN = ((1,), (0,))
NT = ((1,), (1,))
TN = ((0,), (0,))


def _accumulate(ref, val, first):
    @pl.when(first)
    def _():
        ref[...] = val

    @pl.when(jnp.logical_not(first))
    def _():
        ref[...] += val


def _matmul(a, b, mode, name, out_dtypes=(F32,), extras=(), epilogue=None, tm=512, tn=512, tk=None, chipwise=None):
    if chipwise == "b":
        nc = b.shape[2]
        M, K = a.shape
        (K2, N) = (b.shape[1], N_CHIPS * nc) if mode == "nn" else (N_CHIPS * nc, b.shape[1])
    elif mode == "nn":
        (M, K), (K2, N) = a.shape, b.shape
    elif mode == "nt":
        (M, K), (N, K2) = a.shape, b.shape
    else:
        (K, M), (K2, N) = a.shape, b.shape
    assert K == K2, (a.shape, b.shape, mode)
    tm, tn = min(tm, M), min(tn, N)
    tk = K if tk is None else tk
    assert M % tm == 0 and N % tn == 0 and K % tk == 0, (M, N, K, tm, tn, tk)
    nk = K // tk
    dims = {"nn": NN, "nt": NT, "tn": TN}[mode]
    a_spec = pl.BlockSpec((tk, tm), lambda i, j, k: (k, i)) if mode == "tn" else pl.BlockSpec((tm, tk), lambda i, j, k: (i, k))
    b_spec = pl.BlockSpec((tn, tk), lambda i, j, k: (j, k)) if mode == "nt" else pl.BlockSpec((tk, tn), lambda i, j, k: (k, j))
    o_spec = pl.BlockSpec((tm, tn), lambda i, j, k: (i, j))
    out_shape = (M, N)
    if chipwise == "b" and mode == "nn":
        per = nc // tn
        assert tk == K and nc % tn == 0
        b_spec = pl.BlockSpec((None, tk, tn), lambda i, j, k: (j // per, 0, j % per))
    elif chipwise == "b":
        assert mode == "nt" and tk == nc
        b_spec = pl.BlockSpec((None, tn, tk), lambda i, j, k: (k, j, 0))
    elif chipwise == "out":
        per = (N // N_CHIPS) // tn
        assert (N // N_CHIPS) % tn == 0
        o_spec = pl.BlockSpec((None, tm, tn), lambda i, j, k: (j // per, i, j % per))
        out_shape = (N_CHIPS, M, N // N_CHIPS)
        assert not extras
    n_ex, n_out = len(extras), len(out_dtypes)

    def body(*refs):
        a_ref, b_ref = refs[0], refs[1]
        ex_refs = refs[2:2 + n_ex]
        out_refs = refs[2 + n_ex:2 + n_ex + n_out]
        acc_ref = refs[2 + n_ex + n_out]
        k = pl.program_id(2)
        part = _dot(a_ref[...].astype(BF16), b_ref[...].astype(BF16), dims)

        @pl.when(k == 0)
        def _():
            acc_ref[...] = part

        @pl.when(k > 0)
        def _():
            acc_ref[...] += part

        @pl.when(k == nk - 1)
        def _():
            acc = acc_ref[...]
            outs = (acc,) if epilogue is None else epilogue(acc, *[r[...] for r in ex_refs])
            for r, o in zip(out_refs, outs):
                r[...] = o.astype(r.dtype)

    outs = _pcall(
        body, name=name, grid=(M // tm, N // tn, nk),
        in_specs=[a_spec, b_spec] + [o_spec] * n_ex,
        out_specs=[o_spec] * n_out,
        out_shape=[jax.ShapeDtypeStruct(out_shape, dt) for dt in out_dtypes],
        scratch_shapes=[pltpu.VMEM((tm, tn), F32)],
        compiler_params=_params(("parallel", "parallel", "arbitrary")),
    )(a, b, *extras)
    return outs[0] if n_out == 1 else outs


def _rms_fwd(x, gains, name, ts=256):
    S, D = x.shape
    ts = min(ts, S)
    n = len(gains)

    def body(*refs):
        x_ref, g_refs, o_refs = refs[0], refs[1:1 + n], refs[1 + n:]
        xv = x_ref[...]
        xh = xv * _rinv(xv)
        for g_ref, o_ref in zip(g_refs, o_refs):
            o_ref[...] = (xh * g_ref[...]).astype(BF16)

    row = pl.BlockSpec((ts, D), lambda i: (i, 0))
    vec = pl.BlockSpec((1, D), lambda i: (0, 0))
    return _pcall(body, name=name, grid=(S // ts,), in_specs=[row] + [vec] * n, out_specs=[row] * n,
                  out_shape=[jax.ShapeDtypeStruct((S, D), BF16)] * n, compiler_params=_params(("parallel",)))(x, *gains)


def _rms_bwd(x, dres, gains, dns, name, ts=256):
    S, D = x.shape
    ts = min(ts, S)
    n = len(gains)

    def body(*refs):
        x_ref, dres_ref = refs[0], refs[1]
        g_refs, dn_refs = refs[2:2 + n], refs[2 + n:2 + 2 * n]
        dx_ref, dxb_ref, dg_refs = refs[2 + 2 * n], refs[3 + 2 * n], refs[4 + 2 * n:]
        xv = x_ref[...]
        r = _rinv(xv)
        xh = xv * r
        dx = dres_ref[...]
        first = pl.program_id(0) == 0
        for g_ref, dn_ref, dg_ref in zip(g_refs, dn_refs, dg_refs):
            dn = dn_ref[...].astype(F32)
            _accumulate(dg_ref, jnp.sum(dn * xh, axis=0, keepdims=True), first)
            dxh = dn * g_ref[...]
            dx = dx + r * (dxh - xh * jnp.mean(dxh * xh, axis=-1, keepdims=True))
        dx_ref[...] = dx
        dxb_ref[...] = dx.astype(BF16)

    row = pl.BlockSpec((ts, D), lambda i: (i, 0))
    vec = pl.BlockSpec((1, D), lambda i: (0, 0))
    outs = _pcall(body, name=name, grid=(S // ts,), in_specs=[row, row] + [vec] * n + [row] * n,
                  out_specs=[row, row] + [vec] * n,
                  out_shape=[jax.ShapeDtypeStruct((S, D), F32), jax.ShapeDtypeStruct((S, D), BF16)]
                  + [jax.ShapeDtypeStruct((1, D), F32)] * n,
                  compiler_params=_params(("arbitrary",)))(x, dres, *gains, *dns)
    return outs[0], outs[1], outs[2:]


def _loss_head(h, tgt, name, ts=256):
    S, D = h.shape
    ts = min(ts, S)

    def body(h_ref, t_ref, dh_ref, dhb_ref, loss_ref):
        err = h_ref[...] - t_ref[...]
        dh = err * (1.0 / D)
        dh_ref[...] = dh
        dhb_ref[...] = dh.astype(BF16)
        part = 0.5 * jnp.sum(jnp.mean(err * err, axis=-1, keepdims=True), axis=0, keepdims=True)
        _accumulate(loss_ref, part, pl.program_id(0) == 0)

    row = pl.BlockSpec((ts, D), lambda i: (i, 0))
    return _pcall(body, name=name, grid=(S // ts,), in_specs=[row, row],
                  out_specs=[row, row, pl.BlockSpec((1, 1), lambda i: (0, 0))],
                  out_shape=[jax.ShapeDtypeStruct((S, D), F32), jax.ShapeDtypeStruct((S, D), BF16),
                             jax.ShapeDtypeStruct((1, 1), F32)],
                  compiler_params=_params(("arbitrary",)))(h, tgt)


def _gate_fwd(zt, bf, name):
    H, S = zt.shape
    nb = S // 128

    def body(z_ref, b_ref, c_ref):
        z = z_ref[...] + b_ref[...]
        lf = jnp.minimum(z, 0.0) - jnp.log(1.0 + jnp.exp(-jnp.abs(z)))
        upper = (lax.broadcasted_iota(jnp.int32, (128, 128), 0) <= lax.broadcasted_iota(jnp.int32, (128, 128), 1)).astype(F32)
        carry = jnp.zeros((H, 1), F32)
        for blk in range(nb):
            cs = _dot(lf[:, blk * 128:(blk + 1) * 128], upper, NN, precision=lax.Precision.HIGHEST) + carry
            c_ref[:, blk * 128:(blk + 1) * 128] = cs
            carry = cs[:, 127:128]

    return _pcall(body, name=name, in_specs=[VMEM_SPEC, VMEM_SPEC], out_specs=VMEM_SPEC,
                  out_shape=jax.ShapeDtypeStruct((H, S), F32))(zt, bf)


def _gate_bwd(dct, zt, bf, name):
    H, S = zt.shape
    nb = S // 128

    def body(dc_ref, z_ref, b_ref, dz_ref, db_ref):
        z = z_ref[...] + b_ref[...]
        e = jnp.exp(-jnp.abs(z))
        sig_neg = jnp.where(z >= 0, e, 1.0) / (1.0 + e)
        lower = (lax.broadcasted_iota(jnp.int32, (128, 128), 0) >= lax.broadcasted_iota(jnp.int32, (128, 128), 1)).astype(F32)
        dc = dc_ref[...]
        carry = jnp.zeros((H, 1), F32)
        db = jnp.zeros((H, 1), F32)
        for blk in reversed(range(nb)):
            sl = slice(blk * 128, (blk + 1) * 128)
            dlf = _dot(dc[:, sl], lower, NN, precision=lax.Precision.HIGHEST) + carry
            carry = dlf[:, 0:1]
            dz = dlf * sig_neg[:, sl]
            dz_ref[:, sl] = dz
            db = db + jnp.sum(dz, axis=1, keepdims=True)
        db_ref[...] = db

    return _pcall(body, name=name, in_specs=[VMEM_SPEC] * 3, out_specs=[VMEM_SPEC] * 2,
                  out_shape=[jax.ShapeDtypeStruct((H, S), F32), jax.ShapeDtypeStruct((H, 1), F32)])(dct, zt, bf)


def _lane_is_a():
    return lax.broadcasted_iota(jnp.int32, (1, LANES), 1) < HEAD_DIM


def _per_head_mean(x, is_a):
    sa = jnp.sum(jnp.where(is_a, x, 0.0), axis=-1, keepdims=True)
    sb = jnp.sum(jnp.where(is_a, 0.0, x), axis=-1, keepdims=True)
    return jnp.where(is_a, sa, sb) / HEAD_DIM


def _pair_norm(raw, gain, is_a):
    return raw * lax.rsqrt(_per_head_mean(raw * raw, is_a) + NORM_EPS) * gain


def _pair_norm_bwd(raw, gain, dnormed, is_a):
    r = lax.rsqrt(_per_head_mean(raw * raw, is_a) + NORM_EPS)
    xh = raw * r
    dgain = jnp.sum(dnormed * xh, axis=0, keepdims=True)
    dxh = dnormed * gain
    return r * (dxh - xh * _per_head_mean(dxh * xh, is_a)), dgain


def _fold_heads(x):
    i = lax.broadcasted_iota(jnp.int32, (LANES, LANES), 0)
    j = lax.broadcasted_iota(jnp.int32, (LANES, LANES), 1)
    fold = ((i == j) | (i == j + HEAD_DIM) | (i + HEAD_DIM == j)).astype(F32)
    return _dot(x, fold, NN, precision=lax.Precision.HIGHEST)


def _fold_row(ref):
    ref[...] = _fold_heads(jnp.broadcast_to(ref[...], (8, LANES)))[0:1, :]


def _tri_mask(t, keys_on_rows):
    r = lax.broadcasted_iota(jnp.int32, (t, t), 0)
    c = lax.broadcasted_iota(jnp.int32, (t, t), 1)
    return (r <= c) if keys_on_rows else (r >= c)


def _fox_fwd(proj, c_col, c_row, gq2, gk2, n_heads, name, t=256):
    S = proj.shape[0]
    H = n_heads
    P = H // 2
    t = min(t, S)
    nq = S // t

    def body(q_ref, k_ref, v_ref, cc_ref, cr_ref, gq_ref, gk_ref, o_ref, lse_ref, qs_s, kb_s, vb_s):
        is_a = _lane_is_a()
        qn = _pair_norm(q_ref[...], gq_ref[...], is_a) * 0.125
        qs_s[0] = jnp.where(is_a, qn, 0.0).astype(BF16)
        qs_s[1] = jnp.where(is_a, 0.0, qn).astype(BF16)
        kb_s[...] = _pair_norm(k_ref[...], gk_ref[...], is_a).astype(BF16)
        vb_s[...] = v_ref[...].astype(BF16)
        causal = _tri_mask(t, False)
        for i in range(nq):
            t0 = i * t
            rows = slice(t0, t0 + t)
            o_pair = None
            for a in range(2):
                qi = qs_s[a, rows, :]
                ci = cc_ref[a, rows, :]
                s_d = jnp.where(causal, _dot(qi, kb_s[rows, :], NT) + ci - cr_ref[a, :, rows], NEG)
                m = jnp.max(s_d, axis=-1, keepdims=True)
                if i > 0:
                    s_l = _dot(qi, kb_s[0:t0, :], NT) + ci - cr_ref[a, :, 0:t0]
                    m = jnp.maximum(m, jnp.max(s_l, axis=-1, keepdims=True))
                p_d = jnp.exp(s_d - m)
                l = jnp.sum(p_d, axis=-1, keepdims=True)
                acc = _dot(p_d.astype(BF16), vb_s[rows, :], NN)
                if i > 0:
                    p_l = jnp.exp(s_l - m)
                    l = l + jnp.sum(p_l, axis=-1, keepdims=True)
                    acc = acc + _dot(p_l.astype(BF16), vb_s[0:t0, :], NN)
                o_a = acc / l
                lse_ref[a, rows, :] = m + jnp.log(l)
                o_pair = o_a if a == 0 else jnp.where(is_a, o_pair, o_a)
            o_ref[rows, :] = o_pair.astype(BF16)

    def cols(off):
        return pl.BlockSpec((S, LANES), lambda p: (0, off + p))

    col = pl.BlockSpec((2, S, 1), lambda p: (p, 0, 0))
    rowv = pl.BlockSpec((2, 1, S), lambda p: (p, 0, 0))
    gain = pl.BlockSpec((1, LANES), lambda p: (0, 0))
    return _pcall(body, name=name, grid=(P,), in_specs=[cols(0), cols(P), cols(2 * P), col, rowv, gain, gain],
                  out_specs=[cols(0), col],
                  out_shape=[jax.ShapeDtypeStruct((S, H * HEAD_DIM), BF16), jax.ShapeDtypeStruct((H, S, 1), F32)],
                  scratch_shapes=[pltpu.VMEM((2, S, LANES), BF16), pltpu.VMEM((S, LANES), BF16), pltpu.VMEM((S, LANES), BF16)],
                  compiler_params=_params(("parallel",)))(proj, proj, proj, c_col, c_row, gq2, gk2)


def _fox_bwd(proj, c_col, c_row, gq2, gk2, lse_row, do, n_heads, name, t=256):
    S = proj.shape[0]
    H = n_heads
    P = H // 2
    t = min(t, S)
    nq = S // t
    assert t % LANES == 0

    def body(q_ref, k_ref, v_ref, cc_ref, cr_ref, gq_ref, gk_ref, lr_ref, do_ref,
             dq_ref, dk_ref, dv_ref, dc_ref, dgq_ref, dgk_ref,
             qs_s, kb_s, kt_s, vb_s, dob_s, dq_s, dk_s, dv_s, dcs_s):
        is_a = _lane_is_a()
        qn = _pair_norm(q_ref[...], gq_ref[...], is_a) * 0.125
        qs_s[0] = jnp.where(is_a, qn, 0.0).astype(BF16)
        qs_s[1] = jnp.where(is_a, 0.0, qn).astype(BF16)
        kn = _pair_norm(k_ref[...], gk_ref[...], is_a)
        kb_s[...] = kn.astype(BF16)
        kt_s[0] = jnp.where(is_a, kn, 0.0).T.astype(BF16)
        kt_s[1] = jnp.where(is_a, 0.0, kn).T.astype(BF16)
        vb_s[...] = v_ref[...].astype(BF16)
        dov = do_ref[...]
        dob_s[0] = jnp.where(is_a, dov, 0.0).astype(BF16)
        dob_s[1] = jnp.where(is_a, 0.0, dov).astype(BF16)
        dk_s[...] = jnp.zeros((S, LANES), F32)
        dv_s[...] = jnp.zeros((S, LANES), F32)
        dcs_s[...] = jnp.zeros((2, S, LANES), F32)
        causal = _tri_mask(t, True)
        for i in range(nq):
            t0 = i * t
            rows = slice(t0, t0 + t)
            dq_t = jnp.zeros((LANES, t), F32)
            for a in range(2):
                qi = qs_s[a, rows, :]
                doi = dob_s[a, rows, :]
                cri = cr_ref[a, :, rows]
                lri = lr_ref[a, :, rows]

                def probs(keys, masked, a=a, qi=qi, doi=doi, cri=cri, lri=lri):
                    p_t = jnp.exp(_dot(kb_s[keys, :], qi, NT) + cri - cc_ref[a, keys, :] - lri)
                    if masked:
                        p_t = jnp.where(causal, p_t, 0.0)
                    return p_t, _dot(vb_s[keys, :], doi, NT)

                parts = [(rows,) + probs(rows, True)]
                if i > 0:
                    parts.append((slice(0, t0),) + probs(slice(0, t0), False))
                delta = sum(jnp.sum(p_t * dp_t, axis=0, keepdims=True) for _, p_t, dp_t in parts)
                for keys, p_t, dp_t in parts:
                    ds_t = p_t * (dp_t - delta)
                    dsb = ds_t.astype(BF16)
                    dv_s[keys, :] += _dot(p_t.astype(BF16), doi, NN)
                    dk_s[keys, :] += _dot(dsb, qi, NN)
                    dq_t = dq_t + _dot(kt_s[a, :, keys], dsb, NN)
                    dcs_s[a, keys, :] += sum(ds_t[:, b * LANES:(b + 1) * LANES] for b in range(t // LANES))
            dq_s[rows, :] = dq_t.T
        first = pl.program_id(0) == 0
        last = pl.program_id(0) == P - 1
        dq_raw, dgq = _pair_norm_bwd(q_ref[...], gq_ref[...], dq_s[...] * 0.125, is_a)
        dq_ref[...] = dq_raw.astype(BF16)
        _accumulate(dgq_ref, dgq, first)
        dk_raw, dgk = _pair_norm_bwd(k_ref[...], gk_ref[...], dk_s[...], is_a)
        dk_ref[...] = dk_raw.astype(BF16)
        _accumulate(dgk_ref, dgk, first)
        dv_ref[...] = dv_s[...].astype(BF16)
        for a in range(2):
            dc_ref[a] = -jnp.sum(dcs_s[a], axis=1, keepdims=True)

        @pl.when(last)
        def _():
            _fold_row(dgq_ref)
            _fold_row(dgk_ref)

    def cols(off):
        return pl.BlockSpec((S, LANES), lambda p: (0, off + p))

    col = pl.BlockSpec((2, S, 1), lambda p: (p, 0, 0))
    rowv = pl.BlockSpec((2, 1, S), lambda p: (p, 0, 0))
    gain = pl.BlockSpec((1, LANES), lambda p: (0, 0))
    wide = jax.ShapeDtypeStruct((S, H * HEAD_DIM), BF16)
    gs = jax.ShapeDtypeStruct((1, LANES), F32)
    return _pcall(body, name=name, grid=(P,),
                  in_specs=[cols(0), cols(P), cols(2 * P), col, rowv, gain, gain, rowv, cols(0)],
                  out_specs=[cols(0), cols(0), cols(0), col, gain, gain],
                  out_shape=[wide, wide, wide, jax.ShapeDtypeStruct((H, S, 1), F32), gs, gs],
                  scratch_shapes=[pltpu.VMEM((2, S, LANES), BF16), pltpu.VMEM((S, LANES), BF16), pltpu.VMEM((2, LANES, S), BF16),
                                  pltpu.VMEM((S, LANES), BF16), pltpu.VMEM((2, S, LANES), BF16)]
                  + [pltpu.VMEM((S, LANES), F32)] * 3 + [pltpu.VMEM((2, S, LANES), F32)],
                  compiler_params=_params(("arbitrary",)))(proj, proj, proj, c_col, c_row, gq2, gk2, lse_row, do)


def _bucket_onehot():
    W = WINDOW
    dist = np.arange(W)[:, None] + W - np.arange(2 * W)[None, :]
    n = np.maximum(dist, 0)
    max_exact = N_BUCKETS // 2
    large = max_exact + (np.log(np.maximum(n, 1) / max_exact) / np.log(REL_MAX_DIST / max_exact)
                         * (N_BUCKETS - max_exact)).astype(np.int32)
    large = np.minimum(large, N_BUCKETS - 1)
    bucket = np.where(n < max_exact, n, large).astype(np.int32)
    valid = (dist >= 0) & (dist < W)
    onehot = (bucket[None] == np.arange(N_BUCKETS)[:, None, None]) & valid[None]
    return onehot.reshape(N_BUCKETS, W * 2 * W).astype(np.float32)


def _bias_expand(rel_bias_t, onehot, name, tn=4096):
    HQ, NB = rel_bias_t.shape
    L = onehot.shape[1]

    def body(r_ref, oh_ref, out_ref):
        out_ref[...] = _dot(r_ref[...], oh_ref[...].astype(F32), NN, precision=lax.Precision.HIGHEST)

    return _pcall(body, name=name, grid=(L // tn,),
                  in_specs=[pl.BlockSpec((HQ, NB), lambda i: (0, 0)), pl.BlockSpec((NB, tn), lambda i: (0, i))],
                  out_specs=pl.BlockSpec((HQ, tn), lambda i: (0, i)),
                  out_shape=jax.ShapeDtypeStruct((HQ, L), F32), compiler_params=_params(("parallel",)))(rel_bias_t, onehot)


def _bias_reduce(dbias, onehot, name, tk=4096):
    HQ, L = dbias.shape
    NB = onehot.shape[0]

    def body(d_ref, oh_ref, out_ref):
        part = _dot(d_ref[...], oh_ref[...].astype(F32), NT, precision=lax.Precision.HIGHEST)
        _accumulate(out_ref, part, pl.program_id(0) == 0)

    return _pcall(body, name=name, grid=(L // tk,),
                  in_specs=[pl.BlockSpec((HQ, tk), lambda i: (0, i)), pl.BlockSpec((NB, tk), lambda i: (0, i))],
                  out_specs=pl.BlockSpec((HQ, NB), lambda i: (0, 0)),
                  out_shape=jax.ShapeDtypeStruct((HQ, NB), F32), compiler_params=_params(("arbitrary",)))(dbias, onehot)


def _stacked_query_index(n_rows_or_cols_axis, shape):
    idx = lax.broadcasted_iota(jnp.int32, shape, n_rows_or_cols_axis)
    return jnp.where(idx >= WINDOW, idx - WINDOW, idx)


def _swa_fwd(qproj, kk, vv, bias_ab, sink_col, gq2, gk2, name):
    S, HQD = qproj.shape
    KVH = kk.shape[0]
    PP = HQD // LANES
    NP = PP // KVH
    W = WINDOW
    nb = S // W

    def body(q_ref, k_ref, v_ref, bias_ref, sink_ref, gq_ref, gk_ref, o_ref, lse_ref, qs_s, kb_s, vb_s):
        is_a = _lane_is_a()
        qn = _pair_norm(q_ref[...], gq_ref[...], is_a) * 0.125
        qs_s[0] = jnp.where(is_a, qn, 0.0).astype(BF16)
        qs_s[1] = jnp.where(is_a, 0.0, qn).astype(BF16)
        kb_s[...] = _pair_norm(k_ref[...], gk_ref[...], is_a).astype(BF16)
        vb_s[...] = v_ref[...].astype(BF16)
        sink = sink_ref[...]
        qi1 = _stacked_query_index(0, (2 * W, W))
        first_valid = lax.broadcasted_iota(jnp.int32, (2 * W, W), 1) <= qi1
        qi2 = _stacked_query_index(0, (2 * W, 2 * W))
        key2 = lax.broadcasted_iota(jnp.int32, (2 * W, 2 * W), 1)
        band_valid = (key2 > qi2) & (key2 <= qi2 + W)
        for n in range(nb):
            rows = slice(n * W, (n + 1) * W)
            keys = slice(0, W) if n == 0 else slice((n - 1) * W, (n + 1) * W)
            lhs = jnp.concatenate([qs_s[0, rows, :], qs_s[1, rows, :]], axis=0)
            s = _dot(lhs, kb_s[keys, :], NT) + (bias_ref[:, W:2 * W] if n == 0 else bias_ref[...])
            s = jnp.where(first_valid if n == 0 else band_valid, s, NEG)
            m = jnp.maximum(jnp.max(s, axis=-1, keepdims=True), sink)
            e = jnp.exp(s - m)
            l = jnp.sum(e, axis=-1, keepdims=True) + jnp.exp(sink - m)
            o_ab = _dot(e.astype(BF16), vb_s[keys, :], NN) / l
            o_ref[rows, :] = jnp.where(is_a, o_ab[0:W, :], o_ab[W:2 * W, :]).astype(BF16)
            lse_ref[n] = m + jnp.log(l)

    qcols = pl.BlockSpec((S, LANES), lambda a, g: (0, a * NP + g))
    kvs = pl.BlockSpec((None, S, LANES), lambda a, g: (a, 0, 0))
    gain = pl.BlockSpec((1, LANES), lambda a, g: (0, 0))
    return _pcall(body, name=name, grid=(KVH, NP),
                  in_specs=[qcols, kvs, kvs, pl.BlockSpec((None, 2 * W, 2 * W), lambda a, g: (a * NP + g, 0, 0)),
                            pl.BlockSpec((None, 2 * W, 1), lambda a, g: (a * NP + g, 0, 0)), gain, gain],
                  out_specs=[qcols, pl.BlockSpec((None, nb, 2 * W, 1), lambda a, g: (a * NP + g, 0, 0, 0))],
                  out_shape=[jax.ShapeDtypeStruct((S, HQD), BF16), jax.ShapeDtypeStruct((PP, nb, 2 * W, 1), F32)],
                  scratch_shapes=[pltpu.VMEM((2, S, LANES), BF16), pltpu.VMEM((S, LANES), BF16), pltpu.VMEM((S, LANES), BF16)],
                  compiler_params=_params(("parallel", "parallel")))(qproj, kk, vv, bias_ab, sink_col, gq2, gk2)


def _swa_bwd(qproj, kk, vv, bias_t_ab, sink_row, gq2, gk2, lse_row, do, name):
    S, HQD = qproj.shape
    KVH = kk.shape[0]
    PP = HQD // LANES
    NP = PP // KVH
    W = WINDOW
    nb = S // W

    def body(q_ref, k_ref, v_ref, bias_ref, sink_ref, gq_ref, gk_ref, lr_ref, do_ref,
             dq_ref, dk_ref, dv_ref, db_ref, dsink_ref, dgq_ref, dgk_ref,
             qs_s, kb_s, kt_s, vb_s, dob_s, dq_s, dk_s, dv_s):
        a, g = pl.program_id(0), pl.program_id(1)
        is_a = _lane_is_a()
        qn = _pair_norm(q_ref[...], gq_ref[...], is_a) * 0.125
        qs_s[0] = jnp.where(is_a, qn, 0.0).astype(BF16)
        qs_s[1] = jnp.where(is_a, 0.0, qn).astype(BF16)
        kn = _pair_norm(k_ref[...], gk_ref[...], is_a)
        kb_s[...] = kn.astype(BF16)
        kt_s[...] = kn.T.astype(BF16)
        vb_s[...] = v_ref[...].astype(BF16)
        dov = do_ref[...]
        dob_s[0] = jnp.where(is_a, dov, 0.0).astype(BF16)
        dob_s[1] = jnp.where(is_a, 0.0, dov).astype(BF16)
        sink = sink_ref[...]

        @pl.when(g == 0)
        def _():
            dk_s[...] = jnp.zeros((S, LANES), F32)
            dv_s[...] = jnp.zeros((S, LANES), F32)

        qi1 = _stacked_query_index(1, (W, 2 * W))
        first_valid = lax.broadcasted_iota(jnp.int32, (W, 2 * W), 0) <= qi1
        qi2 = _stacked_query_index(1, (2 * W, 2 * W))
        key2 = lax.broadcasted_iota(jnp.int32, (2 * W, 2 * W), 0)
        band_valid = (key2 > qi2) & (key2 <= qi2 + W)
        head_rows = lax.broadcasted_iota(jnp.int32, (LANES, W), 0) < HEAD_DIM
        db = jnp.zeros((2 * W, 2 * W), F32)
        dsk = jnp.zeros((1, 2 * W), F32)
        pend_k = pend_v = None
        for n in range(nb):
            rows = slice(n * W, (n + 1) * W)
            keys = slice(0, W) if n == 0 else slice((n - 1) * W, (n + 1) * W)
            lhs_q = jnp.concatenate([qs_s[0, rows, :], qs_s[1, rows, :]], axis=0)
            lhs_do = jnp.concatenate([dob_s[0, rows, :], dob_s[1, rows, :]], axis=0)
            lse = lr_ref[n]
            s_t = _dot(kb_s[keys, :], lhs_q, NT) + (bias_ref[W:2 * W, :] if n == 0 else bias_ref[...])
            p_t = jnp.where(first_valid if n == 0 else band_valid, jnp.exp(s_t - lse), 0.0)
            dp_t = _dot(vb_s[keys, :], lhs_do, NT)
            delta = jnp.sum(p_t * dp_t, axis=0, keepdims=True)
            ds_t = p_t * (dp_t - delta)
            dsb = ds_t.astype(BF16)
            dsk = dsk - jnp.exp(sink - lse) * delta
            dv_band = _dot(p_t.astype(BF16), lhs_do, NN)
            dk_band = _dot(dsb, lhs_q, NN)
            dq_t = _dot(kt_s[:, keys], dsb, NN)
            dq_s[rows, :] = jnp.where(head_rows, dq_t[:, 0:W], dq_t[:, W:2 * W]).T
            if n == 0:
                db = jnp.concatenate([jnp.zeros((W, 2 * W), F32), ds_t], axis=0)
                pend_k, pend_v = dk_band, dv_band
            else:
                db = db + ds_t
                prev = slice((n - 1) * W, n * W)
                dk_s[prev, :] += pend_k + dk_band[0:W, :]
                dv_s[prev, :] += pend_v + dv_band[0:W, :]
                pend_k, pend_v = dk_band[W:2 * W, :], dv_band[W:2 * W, :]
        tail = slice((nb - 1) * W, nb * W)
        dk_s[tail, :] += pend_k
        dv_s[tail, :] += pend_v
        db_ref[...] = db
        dsink_ref[0] = jnp.broadcast_to(jnp.sum(dsk[:, 0:W], axis=1, keepdims=True), (1, LANES))
        dsink_ref[1] = jnp.broadcast_to(jnp.sum(dsk[:, W:2 * W], axis=1, keepdims=True), (1, LANES))
        dq_raw, dgq = _pair_norm_bwd(q_ref[...], gq_ref[...], dq_s[...] * 0.125, is_a)
        dq_ref[...] = dq_raw.astype(BF16)
        _accumulate(dgq_ref, dgq, jnp.logical_and(a == 0, g == 0))

        @pl.when(jnp.logical_and(a == KVH - 1, g == NP - 1))
        def _():
            _fold_row(dgq_ref)

        @pl.when(g == NP - 1)
        def _():
            dk_raw, dgk = _pair_norm_bwd(k_ref[...], gk_ref[...], _fold_heads(dk_s[...]), is_a)
            dk_ref[...] = dk_raw
            _accumulate(dgk_ref, dgk, a == 0)
            dv_ref[...] = _fold_heads(dv_s[...])

    qcols = pl.BlockSpec((S, LANES), lambda a, g: (0, a * NP + g))
    kvs = pl.BlockSpec((None, S, LANES), lambda a, g: (a, 0, 0))
    sq = pl.BlockSpec((None, 2 * W, 2 * W), lambda a, g: (a * NP + g, 0, 0))
    gain = pl.BlockSpec((1, LANES), lambda a, g: (0, 0))
    ks = jax.ShapeDtypeStruct((KVH, S, LANES), F32)
    gs = jax.ShapeDtypeStruct((1, LANES), F32)
    return _pcall(body, name=name, grid=(KVH, NP),
                  in_specs=[qcols, kvs, kvs, sq, pl.BlockSpec((None, 1, 2 * W), lambda a, g: (a * NP + g, 0, 0)), gain, gain,
                            pl.BlockSpec((None, nb, 1, 2 * W), lambda a, g: (a * NP + g, 0, 0, 0)), qcols],
                  out_specs=[qcols, kvs, kvs, sq, pl.BlockSpec((2, 1, LANES), lambda a, g: (a * NP + g, 0, 0)), gain, gain],
                  out_shape=[jax.ShapeDtypeStruct((S, HQD), BF16), ks, ks, jax.ShapeDtypeStruct((PP, 2 * W, 2 * W), F32),
                             jax.ShapeDtypeStruct((2 * PP, 1, LANES), F32), gs, gs],
                  scratch_shapes=[pltpu.VMEM((2, S, LANES), BF16), pltpu.VMEM((S, LANES), BF16), pltpu.VMEM((LANES, S), BF16),
                                  pltpu.VMEM((S, LANES), BF16), pltpu.VMEM((2, S, LANES), BF16)] + [pltpu.VMEM((S, LANES), F32)] * 3,
                  compiler_params=_params(("arbitrary", "arbitrary")))(qproj, kk, vv, bias_t_ab, sink_row, gq2, gk2, lse_row, do)


def _adamw(w, g, m, v, name, tr=256):
    R, C = w.shape
    tr = min(tr, R)
    assert R % tr == 0

    def body(w_ref, g_ref, m_ref, v_ref, d_ref, m2_ref, v2_ref):
        gv = g_ref[...]
        m2 = ADAM_B1 * m_ref[...] + (1.0 - ADAM_B1) * gv
        v2 = ADAM_B2 * v_ref[...] + (1.0 - ADAM_B2) * jnp.square(gv)
        m_hat = m2 / (1.0 - ADAM_B1 ** ADAM_STEP)
        v_hat = v2 / (1.0 - ADAM_B2 ** ADAM_STEP)
        d_ref[...] = -ADAM_LR * (m_hat / (jnp.sqrt(v_hat) + ADAM_EPS) + ADAM_WD * w_ref[...])
        m2_ref[...] = m2
        v2_ref[...] = v2

    blk = pl.BlockSpec((tr, C), lambda i: (i, 0))
    return _pcall(body, name=name, grid=(R // tr,), in_specs=[blk] * 4, out_specs=[blk] * 3,
                  out_shape=[jax.ShapeDtypeStruct((R, C), F32)] * 3, compiler_params=_params(("parallel",)))(w, g, m, v)


def _sum_core_pair(arr, got, place, name, tr=128):
    P, hr, C = got.shape
    assert hr % tr == 0
    nt = hr // tr

    def body(place_ref, a_ref, g_ref, o_ref):
        o_ref[...] = (a_ref[...].astype(F32) + g_ref[...].astype(F32)).astype(o_ref.dtype)

    spec = pltpu.PrefetchScalarGridSpec(
        num_scalar_prefetch=1, grid=(P, nt),
        in_specs=[pl.BlockSpec((None, tr, C), lambda j, i, pr: (j, pr[1] * nt + i, 0)),
                  pl.BlockSpec((None, tr, C), lambda j, i, pr: (j, i, 0))],
        out_specs=pl.BlockSpec((None, tr, C), lambda j, i, pr: (j, i, 0)))
    return _pcall(body, name=name, grid_spec=spec, out_shape=jax.ShapeDtypeStruct(got.shape, BF16),
                  compiler_params=_params(("parallel", "parallel")))(place, arr, got)


def _sum_chips(pair, landed, place, name, tr=128):
    _, R, C = landed.shape
    assert R % tr == 0

    def body(place_ref, p_ref, l_ref, o_ref):
        acc = p_ref[...].astype(F32)
        for k in range(3):
            acc = acc + l_ref[k].astype(F32)
        o_ref[...] = acc

    spec = pltpu.PrefetchScalarGridSpec(
        num_scalar_prefetch=1, grid=(R // tr,),
        in_specs=[pl.BlockSpec((None, tr, C), lambda i, pr: (pr[0], i, 0)), pl.BlockSpec((3, tr, C), lambda i, pr: (0, i, 0))],
        out_specs=pl.BlockSpec((None, tr, C), lambda i, pr: (pr[1], i, 0)))
    return _pcall(body, name=name, grid_spec=spec, out_shape=jax.ShapeDtypeStruct((2, R, C), F32),
                  compiler_params=_params(("parallel",)))(place, pair, landed)


def _sum_parts(parts, name, out_dtype, tr=128):
    P, R, C = parts.shape
    tr = min(tr, R)
    assert R % tr == 0, (R, tr)

    def body(p_ref, o_ref):
        acc = p_ref[0].astype(F32)
        for k in range(1, P):
            acc = acc + p_ref[k].astype(F32)
        o_ref[...] = acc.astype(o_ref.dtype)

    return _pcall(body, name=name, grid=(R // tr,), in_specs=[pl.BlockSpec((P, tr, C), lambda i: (0, i, 0))],
                  out_specs=pl.BlockSpec((tr, C), lambda i: (i, 0)),
                  out_shape=jax.ShapeDtypeStruct((R, C), out_dtype), compiler_params=_params(("parallel",)))(parts)


def _place():
    x, y, c = lax.axis_index("x"), lax.axis_index("y"), lax.axis_index("c")
    others = [(1 - x, y), (x, 1 - y), (1 - x, 1 - y)]
    return x, y, c, others


def _half_rows(ref, hh, lead=()):
    hr = ref.shape[-2] // 2
    return ref.at[(*lead, pl.ds(pl.multiple_of(hh * hr, 16), hr), slice(None))]


def _sem_arrays(*counts):
    return [pltpu.SemaphoreType.DMA((k,)) for k in counts]


SEM_SPEC = pl.BlockSpec(memory_space=pltpu.SEMAPHORE)
ANY_SPEC = pl.BlockSpec(memory_space=pl.ANY)
DATAFLOW = pltpu.SideEffectType.DATAFLOW_SIDE_EFFECTING


def _in_hbm(a):
    return pltpu.with_memory_space_constraint(a, pltpu.HBM)


def _gather_copies(srcs, lands, send_sems, recv_sems):
    x, y, c, others = _place()
    me = 2 * x + y

    def copy(w, k, dst_chip, to):
        return pltpu.make_async_remote_copy(src_ref=_half_rows(srcs[w], c), dst_ref=_half_rows(lands[w], c, (dst_chip,)),
                                            send_sem=send_sems.at[3 * w + k], recv_sem=recv_sems.at[3 * w + k],
                                            device_id=to, device_id_type=MESH)

    pairs = [(w, k, cx, cy) for w in range(len(srcs)) for k, (cx, cy) in enumerate(others)]
    return ([copy(w, k, me, (cx, cy, c)) for w, k, cx, cy in pairs],
            [copy(w, k, 2 * cx + cy, (cx, cy, c)) for w, k, cx, cy in pairs])


def _gather_start(shards, name):
    n = len(shards)

    def body(*refs):
        srcs, lands, send_sems, recv_sems, token = refs[:n], refs[n:2 * n], refs[2 * n], refs[2 * n + 1], refs[-1]
        for cp in _gather_copies(srcs, lands, send_sems, recv_sems)[0]:
            cp.start()
        token[...] = jnp.zeros_like(token)

    lands = [lax.empty((N_CHIPS,) + s.shape, s.dtype) for s in shards]
    outs = _pcall(
        body, name=name, in_specs=[HBM_SPEC] * (2 * n),
        out_specs=[SEM_SPEC, SEM_SPEC] + [HBM_SPEC] * (2 * n) + [VMEM_SPEC],
        out_shape=[pltpu.SemaphoreType.DMA((3 * n,)), pltpu.SemaphoreType.DMA((3 * n,))]
        + [pltpu.HBM(a.shape, a.dtype) for a in list(shards) + lands] + [jax.ShapeDtypeStruct((8, LANES), F32)],
        input_output_aliases={i: 2 + i for i in range(2 * n)},
        compiler_params=pltpu.CompilerParams(has_side_effects=DATAFLOW),
    )(*[_in_hbm(a) for a in list(shards) + lands])
    return outs[0], outs[1], outs[2:2 + n], outs[2 + n:2 + 2 * n], outs[-1]


def _gather_wait(started, after, name):
    send_sems, recv_sems, srcs, lands, _ = started
    n = len(srcs)

    def body(*refs):
        src_refs, land_refs, send_ref, recv_ref = refs[:n], refs[n:2 * n], refs[2 * n], refs[2 * n + 1]
        outgoing, incoming = _gather_copies(src_refs, land_refs, send_ref, recv_ref)
        for out_cp, in_cp in zip(outgoing, incoming):
            out_cp.wait_send()
            in_cp.wait_recv()

    outs = _pcall(
        body, name=name, in_specs=[HBM_SPEC] * (2 * n) + [SEM_SPEC, SEM_SPEC, ANY_SPEC], out_specs=[HBM_SPEC] * (2 * n),
        out_shape=[pltpu.HBM(a.shape, a.dtype) for a in list(srcs) + list(lands)],
        input_output_aliases={i: i for i in range(2 * n)},
        compiler_params=pltpu.CompilerParams(has_side_effects=DATAFLOW),
    )(*srcs, *lands, send_sems, recv_sems, after)
    return outs[:n], outs[n:]


def _gather_pass_on(shards, lands, name):
    n = len(shards)
    per = 4

    def body(*refs):
        srcs, bufs = refs[:n], refs[2 * n:3 * n]
        send_sems, recv_sems = refs[3 * n:]
        x, y, c, others = _place()
        me = 2 * x + y
        sibling = (x, y, 1 - c)

        def copy(w, k, src, dst):
            return pltpu.make_async_remote_copy(src_ref=src, dst_ref=dst, send_sem=send_sems.at[per * w + k],
                                                recv_sem=recv_sems.at[per * w + k], device_id=sibling, device_id_type=MESH)

        sends, recvs = [], []
        for w in range(n):
            for k, (cx, cy) in enumerate(others):
                mine, theirs = _half_rows(bufs[w], c, (2 * cx + cy,)), _half_rows(bufs[w], 1 - c, (2 * cx + cy,))
                sends.append(copy(w, k, mine, mine))
                recvs.append(copy(w, k, theirs, theirs))
            sends.append(copy(w, 3, srcs[w], bufs[w].at[me]))
            recvs.append(sends[-1])
        for cp in sends:
            cp.start()
        for snd, rcv in zip(sends, recvs):
            snd.wait_send()
            rcv.wait_recv()

    return _pcall(body, name=name, in_specs=[HBM_SPEC] * (2 * n), out_specs=[HBM_SPEC] * n,
                  out_shape=[jax.ShapeDtypeStruct(l.shape, l.dtype) for l in lands],
                  input_output_aliases={n + w: w for w in range(n)},
                  scratch_shapes=_sem_arrays(per * n, per * n))(*shards, *lands)


def _scatter_copies(srcs, lands, send_sems, recv_sems):
    x, y, c, others = _place()
    return [pltpu.make_async_remote_copy(src_ref=srcs[w].at[2 * cx + cy], dst_ref=lands[w].at[k],
                                         send_sem=send_sems.at[3 * w + k], recv_sem=recv_sems.at[3 * w + k],
                                         device_id=(cx, cy, c), device_id_type=MESH)
            for w in range(len(srcs)) for k, (cx, cy) in enumerate(others)]


def _scatter_start(parts, name):
    n = len(parts)

    def body(*refs):
        srcs, lands, send_sems, recv_sems, token = refs[:n], refs[n:2 * n], refs[2 * n], refs[2 * n + 1], refs[-1]
        for cp in _scatter_copies(srcs, lands, send_sems, recv_sems):
            cp.start()
        token[...] = jnp.zeros_like(token)

    lands = [lax.empty((3,) + p.shape[1:], p.dtype) for p in parts]
    outs = _pcall(
        body, name=name, in_specs=[HBM_SPEC] * (2 * n), out_specs=[SEM_SPEC, SEM_SPEC] + [HBM_SPEC] * (2 * n) + [VMEM_SPEC],
        out_shape=[pltpu.SemaphoreType.DMA((3 * n,)), pltpu.SemaphoreType.DMA((3 * n,))]
        + [pltpu.HBM(a.shape, a.dtype) for a in list(parts) + lands] + [jax.ShapeDtypeStruct((8, LANES), F32)],
        input_output_aliases={i: 2 + i for i in range(2 * n)},
        compiler_params=pltpu.CompilerParams(has_side_effects=DATAFLOW),
    )(*[_in_hbm(a) for a in list(parts) + lands])
    return (outs[0], outs[1], outs[2:2 + n], outs[2 + n:2 + 2 * n]), outs[-1]


def _scatter_wait(started, after, name):
    send_sems, recv_sems, srcs, lands = started
    n = len(srcs)

    def body(*refs):
        for cp in _scatter_copies(refs[:n], refs[n:2 * n], refs[2 * n], refs[2 * n + 1]):
            cp.wait_send()
            cp.wait_recv()

    outs = _pcall(
        body, name=name, in_specs=[HBM_SPEC] * (2 * n) + [SEM_SPEC, SEM_SPEC, ANY_SPEC], out_specs=[HBM_SPEC] * (2 * n),
        out_shape=[pltpu.HBM(a.shape, a.dtype) for a in list(srcs) + list(lands)],
        input_output_aliases={i: i for i in range(2 * n)},
        compiler_params=pltpu.CompilerParams(has_side_effects=DATAFLOW),
    )(*srcs, *lands, send_sems, recv_sems, after)
    return outs[:n], outs[n:]


def _allgather_group(shards, name):
    n = len(shards)
    per = 7

    def body(*refs):
        ins, outs = refs[:n], refs[n:2 * n]
        send_sems, recv_sems = refs[2 * n:]
        x, y, c, others = _place()
        me = 2 * x + y
        sibling = (x, y, 1 - c)

        def copy(w, k, src, dst, to):
            return pltpu.make_async_remote_copy(src_ref=src, dst_ref=dst, send_sem=send_sems.at[per * w + k],
                                                recv_sem=recv_sems.at[per * w + k], device_id=to, device_id_type=MESH)

        first = [copy(w, k, _half_rows(ins[w], c), _half_rows(outs[w], c, (me,)), (cx, cy, c))
                 for w in range(n) for k, (cx, cy) in enumerate(others)]
        own = [copy(w, 6, ins[w], outs[w].at[me], sibling) for w in range(n)]
        for cp in first + own:
            cp.start()
        passed = []
        for w in range(n):
            for k, (cx, cy) in enumerate(others):
                landed = _half_rows(outs[w], c, (2 * cx + cy,))
                copy(w, k, landed, landed, sibling).wait_recv()
                passed.append(copy(w, 3 + k, landed, landed, sibling))
                passed[-1].start()
        for w in range(n):
            for k, (cx, cy) in enumerate(others):
                theirs = _half_rows(outs[w], 1 - c, (2 * cx + cy,))
                copy(w, 3 + k, theirs, theirs, sibling).wait_recv()
            own[w].wait_recv()
        for cp in first + passed + own:
            cp.wait_send()

    return _pcall(body, name=name, in_specs=[HBM_SPEC] * n, out_specs=[HBM_SPEC] * n,
                  out_shape=[jax.ShapeDtypeStruct((N_CHIPS,) + s.shape, s.dtype) for s in shards],
                  scratch_shapes=_sem_arrays(per * n, per * n))(*shards)


def _swap_halves_group(arrs, name):
    n = len(arrs)

    def body(*refs):
        ins, gots = refs[:n], refs[n:2 * n]
        send_sems, recv_sems = refs[2 * n:]
        x, y, c, _ = _place()
        swaps = [pltpu.make_async_remote_copy(src_ref=_half_rows(ins[w], 1 - c, (slice(None),)), dst_ref=gots[w],
                                              send_sem=send_sems.at[w], recv_sem=recv_sems.at[w],
                                              device_id=(x, y, 1 - c), device_id_type=MESH) for w in range(n)]
        for cp in swaps:
            cp.start()
        for cp in swaps:
            cp.wait()

    half_shapes = [jax.ShapeDtypeStruct((a.shape[0], a.shape[1] // 2, a.shape[2]), a.dtype) for a in arrs]
    return _pcall(body, name=name, in_specs=[HBM_SPEC] * n, out_specs=[HBM_SPEC] * n, out_shape=half_shapes,
                  scratch_shapes=_sem_arrays(n, n))(*arrs)


def _scatter_group(parts, name):
    n = len(parts)

    def body(*refs):
        ins, outs = refs[:n], refs[n:2 * n]
        send_sems, recv_sems = refs[2 * n:]
        x, y, c, others = _place()

        def copy(w, k, src_chip, to):
            return pltpu.make_async_remote_copy(src_ref=ins[w].at[src_chip], dst_ref=outs[w].at[k],
                                                send_sem=send_sems.at[3 * w + k], recv_sem=recv_sems.at[3 * w + k],
                                                device_id=to, device_id_type=MESH)

        sends = [copy(w, k, 2 * cx + cy, (cx, cy, c)) for w in range(n) for k, (cx, cy) in enumerate(others)]
        for cp in sends:
            cp.start()
        for cp in sends:
            cp.wait()

    return _pcall(body, name=name, in_specs=[HBM_SPEC] * n, out_specs=[HBM_SPEC] * n,
                  out_shape=[jax.ShapeDtypeStruct((3,) + p.shape[1:], p.dtype) for p in parts],
                  scratch_shapes=_sem_arrays(3 * n, 3 * n))(*parts)


def _share_halves_group(halves, name):
    n = len(halves)

    def body(*refs):
        bufs = refs[n:2 * n]
        send_sems, recv_sems = refs[2 * n:]
        x, y, c, _ = _place()
        swaps = [pltpu.make_async_remote_copy(src_ref=bufs[w].at[c], dst_ref=bufs[w].at[c], send_sem=send_sems.at[w],
                                              recv_sem=recv_sems.at[w], device_id=(x, y, 1 - c), device_id_type=MESH)
                 for w in range(n)]
        for cp in swaps:
            cp.start()
        for w in range(n):
            swaps[w].wait_send()
            pltpu.make_async_remote_copy(src_ref=bufs[w].at[c], dst_ref=bufs[w].at[1 - c], send_sem=send_sems.at[w],
                                         recv_sem=recv_sems.at[w], device_id=(x, y, 1 - c), device_id_type=MESH).wait_recv()

    return _pcall(body, name=name, in_specs=[HBM_SPEC] * n, out_specs=[HBM_SPEC] * n,
                  out_shape=[jax.ShapeDtypeStruct(h.shape, h.dtype) for h in halves],
                  input_output_aliases={w: w for w in range(n)},
                  scratch_shapes=_sem_arrays(n, n))(*halves)


def _allgather_small(blk, name):
    M, C = blk.shape

    def body(x_ref, out_ref, send_sems, recv_sems, local_sem):
        x, y, c, others = _place()
        me, sibling = (x, y, c), (x, y, 1 - c)

        def rows(px, py, pc):
            return out_ref.at[4 * px + 2 * py + pc]

        def copy(k, block, to, src=None):
            return pltpu.make_async_remote_copy(src_ref=rows(*block) if src is None else src, dst_ref=rows(*block),
                                                send_sem=send_sems.at[k], recv_sem=recv_sems.at[k], device_id=to, device_id_type=MESH)

        mine = pltpu.make_async_copy(x_ref, rows(*me), local_sem)
        mine.start()
        first = [copy(0, me, sibling, src=x_ref)]
        first += [copy(1 + j, me, (*chip, c), src=x_ref) for j, chip in enumerate(others)]
        for cp in first:
            cp.start()
        passed = [copy(4 + j, (*chip, c), sibling) for j, chip in enumerate(others)]
        for j, chip in enumerate(others):
            copy(1 + j, (*chip, c), me).wait_recv()
            passed[j].start()
        copy(0, sibling, me).wait_recv()
        for j, chip in enumerate(others):
            copy(4 + j, (*chip, 1 - c), me).wait_recv()
        for cp in first + passed:
            cp.wait_send()
        mine.wait()

    return _pcall(body, name=name, in_specs=[VMEM_SPEC], out_specs=VMEM_SPEC,
                  out_shape=jax.ShapeDtypeStruct((8, M, C), blk.dtype),
                  scratch_shapes=[pltpu.SemaphoreType.DMA((7,)), pltpu.SemaphoreType.DMA((7,)), pltpu.SemaphoreType.DMA])(blk)


def _pack_rows(n_elems, width=PACK_W, align=PACK_ROW_ALIGN):
    rows = -(-n_elems // width)
    return -(-rows // align) * align


def _pack(arrays, dtype, width=PACK_W, align=PACK_ROW_ALIGN):
    flat = jnp.concatenate([a.astype(dtype).reshape(-1) for a in arrays])
    rows = _pack_rows(flat.shape[0], width, align)
    flat = jnp.pad(flat, (0, rows * width - flat.shape[0]))
    return flat.reshape(rows, width)


def _pack_small(arrays):
    return _pack(arrays, F32, width=128, align=8)


def _unpack(flat, shapes):
    out, off = [], 0
    for shp in shapes:
        n = int(np.prod(shp))
        out.append(flat[..., off:off + n].reshape(flat.shape[:-1] + tuple(shp)))
        off += n
    return out


def _doubled_heads(x2d, n_heads):
    S = x2d.shape[0]
    h = x2d.reshape(S, n_heads, HEAD_DIM).transpose(1, 0, 2)
    return jnp.concatenate([h, h], axis=-1)


def _mlp_fwd(h, g, w_up4, w_down, tag):
    (n,) = _rms_fwd(h, [g], f"rms_mlp{tag}")
    u, a = _matmul(n, w_up4, "nn", f"up{tag}", out_dtypes=(F32, BF16), chipwise="b", tm=2048, tn=512,
                   epilogue=lambda acc: (acc, jnp.square(jnp.maximum(acc, 0.0))))
    h_out = _matmul(a, w_down, "nn", f"down{tag}", extras=(h,), epilogue=lambda acc, res: (res + acc,), tm=1024, tn=1024, tk=1024)
    return h_out, (n, u, a)


def _mlp_bwd(dh_out, dh_out_b, h, g, w_up4, w_down, saved, tag):
    n, u, a = saved
    dw_down = _matmul(a, dh_out_b, "tn", f"dw_down{tag}", out_dtypes=(BF16,), tm=2048, tn=512)
    du = _matmul(dh_out_b, w_down, "nt", f"du{tag}", out_dtypes=(BF16,), extras=(u,), tm=2048, tn=512,
                 epilogue=lambda acc, uu: (acc * (2.0 * jnp.maximum(uu, 0.0)),))
    dw_up = _matmul(n, du, "tn", f"dw_up{tag}", out_dtypes=(BF16,), chipwise="out", tm=1024, tn=512)
    dn = _matmul(du, w_up4, "nt", f"dn_mlp{tag}", tm=1024, tn=1024, tk=w_up4.shape[2], chipwise="b")
    dh, dh_b, (dg,) = _rms_bwd(h, dh_out, [g], [dn], f"rms_mlp_bwd{tag}")
    return dh, dh_b, dg, dw_up, dw_down


def kernel(x, g_attn, g_mlp, w_in_a, b_f, gq_a, gk_a, w_out_a, g_kv, w_kv, gk_b, w_q_b, gq_b, sinks, rel_bias, w_out_b, w_up, w_down, loss_target, m_g_attn, m_g_mlp, m_w_in_a, m_b_f, m_gq_a, m_gk_a, m_w_out_a, m_g_kv, m_w_kv, m_gk_b, m_w_q_b, m_gq_b, m_sinks, m_rel_bias, m_w_out_b, m_w_up, m_w_down, v_g_attn, v_g_mlp, v_w_in_a, v_b_f, v_gq_a, v_gk_a, v_w_out_a, v_g_kv, v_w_kv, v_gk_b, v_w_q_b, v_gq_b, v_sinks, v_rel_bias, v_w_out_b, v_w_up, v_w_down):
    given = dict(locals())
    S, D = x.shape[1], x.shape[2]
    H = D // HEAD_DIM
    KVH = w_kv.shape[1] // (2 * HEAD_DIM)
    kvw = KVH * HEAD_DIM
    hw = H * HEAD_DIM
    W = WINDOW
    nb = S // W
    c_idx = lax.axis_index("c")
    xs, tgt = x[0], loss_target[0]

    shards = {"w_in_a": w_in_a[0], "w_out_a": w_out_a[0], "w_up0": w_up[0], "w_down0": w_down[0], "w_kv": w_kv,
              "w_q_b": w_q_b[0], "w_out_b": w_out_b[0], "w_up1": w_up[1], "w_down1": w_down[1]}
    parts = list(shards)
    lane_pad = lambda a: jnp.pad(a, [(0, 0)] * (a.ndim - 1) + [(0, (-a.shape[-1]) % LANES)])
    n_in_shard = w_in_a.shape[2]
    groups = [("w_in_a", "w_out_a"), ("w_up0", "w_down0"), ("w_kv", "w_q_b", "w_out_b", "w_up1", "w_down1")]
    started = [_gather_start([lane_pad(shards[n].astype(BF16)) for n in grp], f"gather_start{i}") for i, grp in enumerate(groups)]
    gathered = {}

    def finish_gather(i, after):
        srcs, lands = _gather_wait(started[i], after, f"gather_wait{i}")
        gathered.update(zip(groups[i], _gather_pass_on(srcs, lands, f"gather_pass_on{i}")))

    vec = lambda a: a.reshape(1, -1)
    twice = lambda a: jnp.tile(a.reshape(1, -1), (1, 2))

    g_attn0 = vec(g_attn[0]) + sum(st[4][0, 0] for st in started)
    finish_gather(0, g_attn0)
    win = jnp.moveaxis(gathered["w_in_a"][:, :, :n_in_shard], 0, 1).reshape(D, -1)
    win = jnp.pad(win, ((0, 0), (0, (-win.shape[1]) % 128)))
    wout_a = gathered["w_out_a"].reshape(-1, D)
    n_in = win.shape[1]
    tile_in = 640 if n_in % 640 == 0 else 128
    (n0,) = _rms_fwd(xs, [g_attn0], "rms_attn0")
    proj = _matmul(n0, win, "nn", "proj_in", tm=2048, tn=tile_in)
    zt = proj[:, 3 * hw:3 * hw + H].T
    c_row = _gate_fwd(zt, b_f.reshape(H, 1), "gate_fwd")
    c_col3, c_row3 = c_row.reshape(H, S, 1), c_row.reshape(H, 1, S)
    o_a, lse_a = _fox_fwd(proj, c_col3, c_row3, twice(gq_a[0]), twice(gk_a[0]), H, "fox_fwd")
    h1 = _matmul(o_a, wout_a, "nn", "out_a", extras=(xs,), epilogue=lambda acc, res: (res + acc,), tm=1024, tn=1024)
    finish_gather(1, h1)
    wup = [gathered["w_up0"], None]
    wdown = [gathered["w_down0"].reshape(-1, D), None]
    h2, mlp0 = _mlp_fwd(h1, vec(g_mlp[0]), wup[0], wdown[0], "0")

    finish_gather(2, h2)
    wq_b, wout_b = gathered["w_q_b"].reshape(-1, D), gathered["w_out_b"].reshape(-1, D)
    wkv = gathered["w_kv"].reshape(D, -1)
    wup[1], wdown[1] = gathered["w_up1"], gathered["w_down1"].reshape(-1, D)
    nkv, n2 = _rms_fwd(h2, [vec(g_kv), vec(g_attn[1])], "rms_attn1")
    kv = _matmul(nkv, wkv, "nn", "proj_kv", tm=2048)
    kk, vv = _doubled_heads(kv[:, :kvw], KVH), _doubled_heads(kv[:, kvw:], KVH)
    q2 = _matmul(n2, wq_b, "nn", "proj_q", tm=1024, tn=1024)
    onehot = jnp.asarray(_bucket_onehot(), dtype=BF16)
    bias = _bias_expand(rel_bias.T, onehot, "bias_expand").reshape(H, W, 2 * W)
    bias_ab = bias.reshape(H // 2, 2 * W, 2 * W)
    bias_t_ab = bias.reshape(H // 2, 2, W, 2 * W).transpose(0, 3, 1, 2).reshape(H // 2, 2 * W, 2 * W)
    sink_ab = jnp.repeat(sinks[0].reshape(H // 2, 2), W, axis=1)
    o_b, lse_b = _swa_fwd(q2, kk, vv, bias_ab, sink_ab.reshape(H // 2, 2 * W, 1), twice(gq_b[0]), twice(gk_b), "swa_fwd")
    h3 = _matmul(o_b, wout_b, "nn", "out_b", extras=(h2,), epilogue=lambda acc, res: (res + acc,), tm=1024, tn=1024)
    h4, mlp1 = _mlp_fwd(h3, vec(g_mlp[1]), wup[1], wdown[1], "1")

    dh4, dh4_b, loss_part = _loss_head(h4, tgt, "loss_head")
    loss = lax.psum(loss_part[0, 0], ("x", "y", "c"))

    place = jnp.stack([2 * lax.axis_index("x") + lax.axis_index("y"), c_idx]).astype(jnp.int32)
    scattering = []

    def start_reduce(named):
        names = list(named)
        got = _swap_halves_group([named[n] for n in names], "swap_grad_halves_" + names[0])
        pair_sums = [_sum_core_pair(named[n], g, place, "sum_core_pair_" + n) for n, g in zip(names, got)]
        started_scatter, token = _scatter_start(pair_sums, "scatter_start_" + names[0])
        scattering.append((names, started_scatter))
        return token[0:1, :]

    dh3, dh3_b, dg_mlp1, dw_up1, dw_down1 = _mlp_bwd(dh4, dh4_b, h3, vec(g_mlp[1]), wup[1], wdown[1], mlp1, "1")
    tie1 = start_reduce({"w_down1": dw_down1.reshape(N_CHIPS, -1, D), "w_up1": dw_up1})
    dw_out_b = _matmul(o_b, dh3_b, "tn", "dw_out_b", out_dtypes=(BF16,), tm=1024, tn=1024)
    do_b = _matmul(dh3_b, wout_b, "nt", "do_b", tm=1024, tn=1024)
    dq2, dk2, dv2, dbias_t_ab, dsink, dgq_b, dgk_b = _swa_bwd(
        q2, kk, vv, bias_t_ab, sink_ab.reshape(H // 2, 1, 2 * W), twice(gq_b[0]) + tie1, twice(gk_b),
        lse_b.reshape(H // 2, nb, 1, 2 * W), do_b, "swa_bwd")
    dbias = dbias_t_ab.reshape(H // 2, 2 * W, 2, W).transpose(0, 2, 3, 1).reshape(H, W * 2 * W)
    d_rel_bias = _bias_reduce(dbias, onehot, "bias_reduce").T
    dw_q_b = _matmul(n2, dq2, "tn", "dw_q_b", out_dtypes=(BF16,), tm=1024, tn=1024)
    dn2 = _matmul(dq2, wq_b, "nt", "dn2", tm=1024, tn=1024)
    dkv = jnp.concatenate([dk2[h, :, :HEAD_DIM] for h in range(KVH)] + [dv2[h, :, :HEAD_DIM] for h in range(KVH)],
                          axis=1).astype(BF16)
    dw_kv = _matmul(nkv, dkv, "tn", "dw_kv", out_dtypes=(BF16,), tm=1024)
    dnkv = _matmul(dkv, wkv, "nt", "dnkv", tm=1024, tn=1024)
    tie2 = start_reduce({"w_out_b": dw_out_b.reshape(N_CHIPS, -1, D), "w_q_b": dw_q_b.reshape(N_CHIPS, -1, D),
                         "w_kv": dw_kv.reshape(N_CHIPS, -1, 2 * kvw)})
    dh2, dh2_b, (dg_kv, dg_attn1) = _rms_bwd(h2, dh3, [vec(g_kv) + tie2[:, :1], vec(g_attn[1])], [dnkv, dn2], "rms_attn1_bwd")

    dh1, dh1_b, dg_mlp0, dw_up0, dw_down0 = _mlp_bwd(dh2, dh2_b, h1, vec(g_mlp[0]), wup[0], wdown[0], mlp0, "0")
    tie3 = start_reduce({"w_down0": dw_down0.reshape(N_CHIPS, -1, D), "w_up0": dw_up0})
    dw_out_a = _matmul(o_a, dh1_b, "tn", "dw_out_a", out_dtypes=(BF16,), tm=1024, tn=1024)
    do_a = _matmul(dh1_b, wout_a, "nt", "do_a", tm=1024, tn=1024)
    dq_a, dk_a, dv_a, dc_col, dgq_a, dgk_a = _fox_bwd(
        proj, c_col3, c_row3, twice(gq_a[0]) + tie3, twice(gk_a[0]), lse_a.reshape(H, 1, S), do_a, H, "fox_bwd")
    dzt, db_f = _gate_bwd(dc_col.reshape(H, S), zt, b_f.reshape(H, 1), "gate_bwd")
    dproj = jnp.concatenate([dq_a, dk_a, dv_a, dzt.T.astype(BF16), jnp.zeros((S, n_in - 3 * hw - H), BF16)], axis=1)
    dw_in = _matmul(n0, dproj, "tn", "dw_in", out_dtypes=(BF16,), tm=1024, tn=tile_in)
    dw_in4 = lane_pad(dw_in[:, :3 * hw + H].reshape(D, N_CHIPS, -1).transpose(1, 0, 2))
    tie4 = start_reduce({"w_out_a": dw_out_a.reshape(N_CHIPS, -1, D), "w_in_a": dw_in4})
    dn0 = _matmul(dproj, win, "nt", "dn0", tm=1024, tn=1024, tk=tile_in)
    grad_x, _, (dg_attn0,) = _rms_bwd(xs, dh1, [vec(g_attn[0]) + tie4[:, :1]], [dn0], "rms_attn0_bwd")

    reduced = {}
    for names, started_scatter in scattering:
        pair_sums, landed = _scatter_wait(started_scatter, grad_x, "scatter_wait_" + names[0])
        halves = [_sum_chips(p, l, place, "sum_chips_" + n) for n, p, l in zip(names, pair_sums, landed)]
        for n, r in zip(names, _share_halves_group(halves, "share_reduced_halves_" + names[0])):
            reduced[n] = r.reshape(-1, r.shape[2])[:, :shards[n].shape[1]]
    big_grads = [reduced["w_in_a"][None], reduced["w_out_a"][None], reduced["w_kv"], reduced["w_q_b"][None],
                 reduced["w_out_b"][None], jnp.stack([reduced["w_up0"], reduced["w_up1"]]),
                 jnp.stack([reduced["w_down0"], reduced["w_down1"]])]

    small_grads = {
        "g_attn": jnp.concatenate([dg_attn0, dg_attn1], axis=0), "g_mlp": jnp.concatenate([dg_mlp0, dg_mlp1], axis=0),
        "b_f": db_f.reshape(1, H), "gq_a": dgq_a[:, :HEAD_DIM], "gk_a": dgk_a[:, :HEAD_DIM], "g_kv": dg_kv.reshape(-1),
        "gk_b": dgk_b[0, :HEAD_DIM], "gq_b": dgq_b[:, :HEAD_DIM], "sinks": dsink[:, 0, 0].reshape(1, H), "rel_bias": d_rel_bias,
    }
    small_shapes = [given[n].shape for n in SMALL]
    spack = _pack_small([small_grads[n] for n in SMALL])
    small_sum = _sum_parts(_allgather_small(spack, "allgather_small"), "sum_small", F32, tr=spack.shape[0])
    small_red = _unpack(small_sum.reshape(-1), small_shapes)

    grads = dict(zip([n for n, _ in BIG], big_grads))
    grads.update(dict(zip(SMALL, small_red)))
    sw = _pack_small([given[n] for n in SMALL])
    sm = _pack_small([given["m_" + n] for n in SMALL])
    sv = _pack_small([given["v_" + n] for n in SMALL])
    sd, sm2, sv2 = _adamw(sw, small_sum, sm, sv, "adamw_small", tr=sw.shape[0])
    delta = dict(zip(SMALL, _unpack(sd.reshape(-1), small_shapes)))
    new_m = dict(zip(SMALL, _unpack(sm2.reshape(-1), small_shapes)))
    new_v = dict(zip(SMALL, _unpack(sv2.reshape(-1), small_shapes)))
    for n, _ in BIG:
        w = given[n]
        two_d = (-1, w.shape[-1])
        d, m2, v2 = _adamw(w.reshape(two_d), grads[n].reshape(two_d), given["m_" + n].reshape(two_d),
                           given["v_" + n].reshape(two_d), "adamw_" + n)
        delta[n], new_m[n], new_v[n] = d.reshape(w.shape), m2.reshape(w.shape), v2.reshape(w.shape)

    order = ["g_attn", "g_mlp", "w_in_a", "b_f", "gq_a", "gk_a", "w_out_a", "g_kv", "w_kv", "gk_b", "w_q_b", "gq_b",
             "sinks", "rel_bias", "w_out_b", "w_up", "w_down"]
    return (loss, grad_x[None], *[grads[n] for n in order], *[delta[n] for n in order],
            *[new_m[n] for n in order], *[new_v[n] for n in order])
```

```python
import numpy as np
import jax
import jax.numpy as jnp
from jax import lax
from jax.experimental import pallas as pl
from jax.experimental.pallas import tpu as pltpu

F32 = jnp.float32
BF16 = jnp.bfloat16
MESH = pl.DeviceIdType.MESH

HEAD_DIM = 64
LANES = 128
WINDOW = 128
N_BUCKETS = 32
REL_MAX_DIST = 128
NORM_EPS = 1e-6
ADAM_LR = 0.001
ADAM_B1 = 0.9
ADAM_B2 = 0.999
ADAM_EPS = 1e-08
ADAM_WD = 0.01
ADAM_STEP = 10
NEG = -1e30
N_CHIPS = 4
PACK_W = 1024
PACK_ROW_ALIGN = 256
VMEM_LIMIT = 56 * 1024 * 1024
HBM_SPEC = pl.BlockSpec(memory_space=pltpu.HBM)
VMEM_SPEC = pl.BlockSpec(memory_space=pltpu.VMEM)

BIG = (("w_in_a", 2), ("w_out_a", 1), ("w_kv", 0), ("w_q_b", 1), ("w_out_b", 1), ("w_up", 2), ("w_down", 1))
SMALL = ("g_attn", "g_mlp", "b_f", "gq_a", "gk_a", "g_kv", "gk_b", "gq_b", "sinks", "rel_bias")


def _pcall(body, **kw):
    return pl.pallas_call(body, **kw)


def _params(sem=None):
    return pltpu.CompilerParams(dimension_semantics=sem, vmem_limit_bytes=VMEM_LIMIT)


def _rinv(x):
    return lax.rsqrt(jnp.mean(x * x, axis=-1, keepdims=True) + NORM_EPS)


def _dot(a, b, dims, precision=None):
    return lax.dot_general(a, b, (dims, ((), ())), precision=precision, preferred_element_type=F32)


NN = ((1,), (0,))
NT = ((1,), (1,))
TN = ((0,), (0,))


def _accumulate(ref, val, first):
    @pl.when(first)
    def _():
        ref[...] = val

    @pl.when(jnp.logical_not(first))
    def _():
        ref[...] += val


def _matmul(a, b, mode, name, out_dtypes=(F32,), extras=(), epilogue=None, tm=512, tn=512, tk=None, chipwise=None):
    if chipwise == "b":
        nc = b.shape[2]
        M, K = a.shape
        (K2, N) = (b.shape[1], N_CHIPS * nc) if mode == "nn" else (N_CHIPS * nc, b.shape[1])
    elif mode == "nn":
        (M, K), (K2, N) = a.shape, b.shape
    elif mode == "nt":
        (M, K), (N, K2) = a.shape, b.shape
    else:
        (K, M), (K2, N) = a.shape, b.shape
    assert K == K2, (a.shape, b.shape, mode)
    tm, tn = min(tm, M), min(tn, N)
    tk = K if tk is None else tk
    assert M % tm == 0 and N % tn == 0 and K % tk == 0, (M, N, K, tm, tn, tk)
    nk = K // tk
    dims = {"nn": NN, "nt": NT, "tn": TN}[mode]
    a_spec = pl.BlockSpec((tk, tm), lambda i, j, k: (k, i)) if mode == "tn" else pl.BlockSpec((tm, tk), lambda i, j, k: (i, k))
    b_spec = pl.BlockSpec((tn, tk), lambda i, j, k: (j, k)) if mode == "nt" else pl.BlockSpec((tk, tn), lambda i, j, k: (k, j))
    o_spec = pl.BlockSpec((tm, tn), lambda i, j, k: (i, j))
    out_shape = (M, N)
    if chipwise == "b" and mode == "nn":
        per = nc // tn
        assert tk == K and nc % tn == 0
        b_spec = pl.BlockSpec((None, tk, tn), lambda i, j, k: (j // per, 0, j % per))
    elif chipwise == "b":
        assert mode == "nt" and tk == nc
        b_spec = pl.BlockSpec((None, tn, tk), lambda i, j, k: (k, j, 0))
    elif chipwise == "out":
        per = (N // N_CHIPS) // tn
        assert (N // N_CHIPS) % tn == 0
        o_spec = pl.BlockSpec((None, tm, tn), lambda i, j, k: (j // per, i, j % per))
        out_shape = (N_CHIPS, M, N // N_CHIPS)
        assert not extras
    n_ex, n_out = len(extras), len(out_dtypes)

    def body(*refs):
        a_ref, b_ref = refs[0], refs[1]
        ex_refs = refs[2:2 + n_ex]
        out_refs = refs[2 + n_ex:2 + n_ex + n_out]
        acc_ref = refs[2 + n_ex + n_out]
        k = pl.program_id(2)
        part = _dot(a_ref[...].astype(BF16), b_ref[...].astype(BF16), dims)

        @pl.when(k == 0)
        def _():
            acc_ref[...] = part

        @pl.when(k > 0)
        def _():
            acc_ref[...] += part

        @pl.when(k == nk - 1)
        def _():
            acc = acc_ref[...]
            outs = (acc,) if epilogue is None else epilogue(acc, *[r[...] for r in ex_refs])
            for r, o in zip(out_refs, outs):
                r[...] = o.astype(r.dtype)

    outs = _pcall(
        body, name=name, grid=(M // tm, N // tn, nk),
        in_specs=[a_spec, b_spec] + [o_spec] * n_ex,
        out_specs=[o_spec] * n_out,
        out_shape=[jax.ShapeDtypeStruct(out_shape, dt) for dt in out_dtypes],
        scratch_shapes=[pltpu.VMEM((tm, tn), F32)],
        compiler_params=_params(("parallel", "parallel", "arbitrary")),
    )(a, b, *extras)
    return outs[0] if n_out == 1 else outs


def _rms_fwd(x, gains, name, ts=256):
    S, D = x.shape
    ts = min(ts, S)
    n = len(gains)

    def body(*refs):
        x_ref, g_refs, o_refs = refs[0], refs[1:1 + n], refs[1 + n:]
        xv = x_ref[...]
        xh = xv * _rinv(xv)
        for g_ref, o_ref in zip(g_refs, o_refs):
            o_ref[...] = (xh * g_ref[...]).astype(BF16)

    row = pl.BlockSpec((ts, D), lambda i: (i, 0))
    vec = pl.BlockSpec((1, D), lambda i: (0, 0))
    return _pcall(body, name=name, grid=(S // ts,), in_specs=[row] + [vec] * n, out_specs=[row] * n,
                  out_shape=[jax.ShapeDtypeStruct((S, D), BF16)] * n, compiler_params=_params(("parallel",)))(x, *gains)


def _rms_bwd(x, dres, gains, dns, name, ts=256):
    S, D = x.shape
    ts = min(ts, S)
    n = len(gains)

    def body(*refs):
        x_ref, dres_ref = refs[0], refs[1]
        g_refs, dn_refs = refs[2:2 + n], refs[2 + n:2 + 2 * n]
        dx_ref, dxb_ref, dg_refs = refs[2 + 2 * n], refs[3 + 2 * n], refs[4 + 2 * n:]
        xv = x_ref[...]
        r = _rinv(xv)
        xh = xv * r
        dx = dres_ref[...]
        first = pl.program_id(0) == 0
        for g_ref, dn_ref, dg_ref in zip(g_refs, dn_refs, dg_refs):
            dn = dn_ref[...].astype(F32)
            _accumulate(dg_ref, jnp.sum(dn * xh, axis=0, keepdims=True), first)
            dxh = dn * g_ref[...]
            dx = dx + r * (dxh - xh * jnp.mean(dxh * xh, axis=-1, keepdims=True))
        dx_ref[...] = dx
        dxb_ref[...] = dx.astype(BF16)

    row = pl.BlockSpec((ts, D), lambda i: (i, 0))
    vec = pl.BlockSpec((1, D), lambda i: (0, 0))
    outs = _pcall(body, name=name, grid=(S // ts,), in_specs=[row, row] + [vec] * n + [row] * n,
                  out_specs=[row, row] + [vec] * n,
                  out_shape=[jax.ShapeDtypeStruct((S, D), F32), jax.ShapeDtypeStruct((S, D), BF16)]
                  + [jax.ShapeDtypeStruct((1, D), F32)] * n,
                  compiler_params=_params(("arbitrary",)))(x, dres, *gains, *dns)
    return outs[0], outs[1], outs[2:]


def _loss_head(h, tgt, name, ts=256):
    S, D = h.shape
    ts = min(ts, S)

    def body(h_ref, t_ref, dh_ref, dhb_ref, loss_ref):
        err = h_ref[...] - t_ref[...]
        dh = err * (1.0 / D)
        dh_ref[...] = dh
        dhb_ref[...] = dh.astype(BF16)
        part = 0.5 * jnp.sum(jnp.mean(err * err, axis=-1, keepdims=True), axis=0, keepdims=True)
        _accumulate(loss_ref, part, pl.program_id(0) == 0)

    row = pl.BlockSpec((ts, D), lambda i: (i, 0))
    return _pcall(body, name=name, grid=(S // ts,), in_specs=[row, row],
                  out_specs=[row, row, pl.BlockSpec((1, 1), lambda i: (0, 0))],
                  out_shape=[jax.ShapeDtypeStruct((S, D), F32), jax.ShapeDtypeStruct((S, D), BF16),
                             jax.ShapeDtypeStruct((1, 1), F32)],
                  compiler_params=_params(("arbitrary",)))(h, tgt)


def _gate_fwd(zt, bf, name):
    H, S = zt.shape
    nb = S // 128

    def body(z_ref, b_ref, c_ref):
        z = z_ref[...] + b_ref[...]
        lf = jnp.minimum(z, 0.0) - jnp.log(1.0 + jnp.exp(-jnp.abs(z)))
        upper = (lax.broadcasted_iota(jnp.int32, (128, 128), 0) <= lax.broadcasted_iota(jnp.int32, (128, 128), 1)).astype(F32)
        carry = jnp.zeros((H, 1), F32)
        for blk in range(nb):
            cs = _dot(lf[:, blk * 128:(blk + 1) * 128], upper, NN, precision=lax.Precision.HIGHEST) + carry
            c_ref[:, blk * 128:(blk + 1) * 128] = cs
            carry = cs[:, 127:128]

    return _pcall(body, name=name, in_specs=[VMEM_SPEC, VMEM_SPEC], out_specs=VMEM_SPEC,
                  out_shape=jax.ShapeDtypeStruct((H, S), F32))(zt, bf)


def _gate_bwd(dct, zt, bf, name):
    H, S = zt.shape
    nb = S // 128

    def body(dc_ref, z_ref, b_ref, dz_ref, db_ref):
        z = z_ref[...] + b_ref[...]
        e = jnp.exp(-jnp.abs(z))
        sig_neg = jnp.where(z >= 0, e, 1.0) / (1.0 + e)
        lower = (lax.broadcasted_iota(jnp.int32, (128, 128), 0) >= lax.broadcasted_iota(jnp.int32, (128, 128), 1)).astype(F32)
        dc = dc_ref[...]
        carry = jnp.zeros((H, 1), F32)
        db = jnp.zeros((H, 1), F32)
        for blk in reversed(range(nb)):
            sl = slice(blk * 128, (blk + 1) * 128)
            dlf = _dot(dc[:, sl], lower, NN, precision=lax.Precision.HIGHEST) + carry
            carry = dlf[:, 0:1]
            dz = dlf * sig_neg[:, sl]
            dz_ref[:, sl] = dz
            db = db + jnp.sum(dz, axis=1, keepdims=True)
        db_ref[...] = db

    return _pcall(body, name=name, in_specs=[VMEM_SPEC] * 3, out_specs=[VMEM_SPEC] * 2,
                  out_shape=[jax.ShapeDtypeStruct((H, S), F32), jax.ShapeDtypeStruct((H, 1), F32)])(dct, zt, bf)


def _lane_is_a():
    return lax.broadcasted_iota(jnp.int32, (1, LANES), 1) < HEAD_DIM


def _per_head_mean(x, is_a):
    sa = jnp.sum(jnp.where(is_a, x, 0.0), axis=-1, keepdims=True)
    sb = jnp.sum(jnp.where(is_a, 0.0, x), axis=-1, keepdims=True)
    return jnp.where(is_a, sa, sb) / HEAD_DIM


def _pair_norm(raw, gain, is_a):
    return raw * lax.rsqrt(_per_head_mean(raw * raw, is_a) + NORM_EPS) * gain


def _pair_norm_bwd(raw, gain, dnormed, is_a):
    r = lax.rsqrt(_per_head_mean(raw * raw, is_a) + NORM_EPS)
    xh = raw * r
    dgain = jnp.sum(dnormed * xh, axis=0, keepdims=True)
    dxh = dnormed * gain
    return r * (dxh - xh * _per_head_mean(dxh * xh, is_a)), dgain


def _fold_heads(x):
    i = lax.broadcasted_iota(jnp.int32, (LANES, LANES), 0)
    j = lax.broadcasted_iota(jnp.int32, (LANES, LANES), 1)
    fold = ((i == j) | (i == j + HEAD_DIM) | (i + HEAD_DIM == j)).astype(F32)
    return _dot(x, fold, NN, precision=lax.Precision.HIGHEST)


def _fold_row(ref):
    ref[...] = _fold_heads(jnp.broadcast_to(ref[...], (8, LANES)))[0:1, :]


def _as_col(row):
    return jnp.broadcast_to(row, (LANES, row.shape[1])).T[:, 0:1]


def _as_row(col):
    return jnp.broadcast_to(col, (col.shape[0], LANES)).T[0:1, :]


def _tri_mask(t, keys_on_rows):
    r = lax.broadcasted_iota(jnp.int32, (t, t), 0)
    c = lax.broadcasted_iota(jnp.int32, (t, t), 1)
    return (r <= c) if keys_on_rows else (r >= c)


def _fox_fwd(proj, c_row, gq2, gk2, n_heads, name, t=256):
    S = proj.shape[0]
    H = n_heads
    P = H // 2
    t = min(t, S)
    nq = S // t

    def body(q_ref, k_ref, v_ref, cr_ref, gq_ref, gk_ref, o_ref, lse_ref, qs_s, kb_s, vb_s):
        is_a = _lane_is_a()
        qn = _pair_norm(q_ref[...], gq_ref[...], is_a) * 0.125
        qs_s[0] = jnp.where(is_a, qn, 0.0).astype(BF16)
        qs_s[1] = jnp.where(is_a, 0.0, qn).astype(BF16)
        kb_s[...] = _pair_norm(k_ref[...], gk_ref[...], is_a).astype(BF16)
        vb_s[...] = v_ref[...].astype(BF16)
        causal = _tri_mask(t, False)
        for i in range(nq):
            t0 = i * t
            rows = slice(t0, t0 + t)
            o_pair = None
            for a in range(2):
                qi = qs_s[a, rows, :]
                ci = _as_col(cr_ref[a, :, rows])
                s_d = jnp.where(causal, _dot(qi, kb_s[rows, :], NT) + ci - cr_ref[a, :, rows], NEG)
                m = jnp.max(s_d, axis=-1, keepdims=True)
                if i > 0:
                    s_l = _dot(qi, kb_s[0:t0, :], NT) + ci - cr_ref[a, :, 0:t0]
                    m = jnp.maximum(m, jnp.max(s_l, axis=-1, keepdims=True))
                p_d = jnp.exp(s_d - m)
                l = jnp.sum(p_d, axis=-1, keepdims=True)
                acc = _dot(p_d.astype(BF16), vb_s[rows, :], NN)
                if i > 0:
                    p_l = jnp.exp(s_l - m)
                    l = l + jnp.sum(p_l, axis=-1, keepdims=True)
                    acc = acc + _dot(p_l.astype(BF16), vb_s[0:t0, :], NN)
                o_a = acc / l
                lse_ref[a, :, rows] = _as_row(m + jnp.log(l))
                o_pair = o_a if a == 0 else jnp.where(is_a, o_pair, o_a)
            o_ref[rows, :] = o_pair.astype(BF16)

    def cols(off):
        return pl.BlockSpec((S, LANES), lambda p: (0, off + p))

    rowv = pl.BlockSpec((2, 1, S), lambda p: (p, 0, 0))
    gain = pl.BlockSpec((1, LANES), lambda p: (0, 0))
    return _pcall(body, name=name, grid=(P,), in_specs=[cols(0), cols(P), cols(2 * P), rowv, gain, gain],
                  out_specs=[cols(0), rowv],
                  out_shape=[jax.ShapeDtypeStruct((S, H * HEAD_DIM), BF16), jax.ShapeDtypeStruct((H, 1, S), F32)],
                  scratch_shapes=[pltpu.VMEM((2, S, LANES), BF16), pltpu.VMEM((S, LANES), BF16), pltpu.VMEM((S, LANES), BF16)],
                  compiler_params=_params(("parallel",)))(proj, proj, proj, c_row, gq2, gk2)


def _fox_bwd(proj, c_row, gq2, gk2, lse_row, do, n_heads, name, t=256):
    S = proj.shape[0]
    H = n_heads
    P = H // 2
    t = min(t, S)
    nq = S // t
    assert t % LANES == 0

    def body(q_ref, k_ref, v_ref, cr_ref, gq_ref, gk_ref, lr_ref, do_ref,
             dq_ref, dk_ref, dv_ref, dc_ref, dgq_ref, dgk_ref,
             qs_s, kb_s, kt_s, vb_s, dob_s, dq_s, dk_s, dv_s, dcs_s, cc_s):
        is_a = _lane_is_a()
        for a in range(2):
            for i in range(nq):
                cc_s[a, i * t:(i + 1) * t, :] = _as_col(cr_ref[a, :, i * t:(i + 1) * t])
        qn = _pair_norm(q_ref[...], gq_ref[...], is_a) * 0.125
        qs_s[0] = jnp.where(is_a, qn, 0.0).astype(BF16)
        qs_s[1] = jnp.where(is_a, 0.0, qn).astype(BF16)
        kn = _pair_norm(k_ref[...], gk_ref[...], is_a)
        kb_s[...] = kn.astype(BF16)
        kt_s[0] = jnp.where(is_a, kn, 0.0).T.astype(BF16)
        kt_s[1] = jnp.where(is_a, 0.0, kn).T.astype(BF16)
        vb_s[...] = v_ref[...].astype(BF16)
        dov = do_ref[...]
        dob_s[0] = jnp.where(is_a, dov, 0.0).astype(BF16)
        dob_s[1] = jnp.where(is_a, 0.0, dov).astype(BF16)
        dk_s[...] = jnp.zeros((S, LANES), F32)
        dv_s[...] = jnp.zeros((S, LANES), F32)
        dcs_s[...] = jnp.zeros((2, S, LANES), F32)
        causal = _tri_mask(t, True)
        for i in range(nq):
            t0 = i * t
            rows = slice(t0, t0 + t)
            dq_t = jnp.zeros((LANES, t), F32)
            for a in range(2):
                qi = qs_s[a, rows, :]
                doi = dob_s[a, rows, :]
                cri = cr_ref[a, :, rows]
                lri = lr_ref[a, :, rows]

                def probs(keys, masked, a=a, qi=qi, doi=doi, cri=cri, lri=lri):
                    p_t = jnp.exp(_dot(kb_s[keys, :], qi, NT) + cri - cc_s[a, keys, :] - lri)
                    if masked:
                        p_t = jnp.where(causal, p_t, 0.0)
                    return p_t, _dot(vb_s[keys, :], doi, NT)

                parts = [(rows,) + probs(rows, True)]
                if i > 0:
                    parts.append((slice(0, t0),) + probs(slice(0, t0), False))
                delta = sum(jnp.sum(p_t * dp_t, axis=0, keepdims=True) for _, p_t, dp_t in parts)
                for keys, p_t, dp_t in parts:
                    ds_t = p_t * (dp_t - delta)
                    dsb = ds_t.astype(BF16)
                    dv_s[keys, :] += _dot(p_t.astype(BF16), doi, NN)
                    dk_s[keys, :] += _dot(dsb, qi, NN)
                    dq_t = dq_t + _dot(kt_s[a, :, keys], dsb, NN)
                    dcs_s[a, keys, :] += sum(ds_t[:, b * LANES:(b + 1) * LANES] for b in range(t // LANES))
            dq_s[rows, :] = dq_t.T
        first = pl.program_id(0) == 0
        last = pl.program_id(0) == P - 1
        dq_raw, dgq = _pair_norm_bwd(q_ref[...], gq_ref[...], dq_s[...] * 0.125, is_a)
        dq_ref[...] = dq_raw.astype(BF16)
        _accumulate(dgq_ref, dgq, first)
        dk_raw, dgk = _pair_norm_bwd(k_ref[...], gk_ref[...], dk_s[...], is_a)
        dk_ref[...] = dk_raw.astype(BF16)
        _accumulate(dgk_ref, dgk, first)
        dv_ref[...] = dv_s[...].astype(BF16)
        for a in range(2):
            for i in range(nq):
                rows = slice(i * t, (i + 1) * t)
                dc_ref[a, :, rows] = _as_row(-jnp.sum(dcs_s[a, rows, :], axis=1, keepdims=True))

        @pl.when(last)
        def _():
            _fold_row(dgq_ref)
            _fold_row(dgk_ref)

    def cols(off):
        return pl.BlockSpec((S, LANES), lambda p: (0, off + p))

    rowv = pl.BlockSpec((2, 1, S), lambda p: (p, 0, 0))
    gain = pl.BlockSpec((1, LANES), lambda p: (0, 0))
    wide = jax.ShapeDtypeStruct((S, H * HEAD_DIM), BF16)
    gs = jax.ShapeDtypeStruct((1, LANES), F32)
    return _pcall(body, name=name, grid=(P,),
                  in_specs=[cols(0), cols(P), cols(2 * P), rowv, gain, gain, rowv, cols(0)],
                  out_specs=[cols(0), cols(0), cols(0), rowv, gain, gain],
                  out_shape=[wide, wide, wide, jax.ShapeDtypeStruct((H, 1, S), F32), gs, gs],
                  scratch_shapes=[pltpu.VMEM((2, S, LANES), BF16), pltpu.VMEM((S, LANES), BF16), pltpu.VMEM((2, LANES, S), BF16),
                                  pltpu.VMEM((S, LANES), BF16), pltpu.VMEM((2, S, LANES), BF16)]
                  + [pltpu.VMEM((S, LANES), F32)] * 3 + [pltpu.VMEM((2, S, LANES), F32), pltpu.VMEM((2, S, 1), F32)],
                  compiler_params=_params(("arbitrary",)))(proj, proj, proj, c_row, gq2, gk2, lse_row, do)


def _bucket_onehot():
    W = WINDOW
    dist = np.arange(W)[:, None] + W - np.arange(2 * W)[None, :]
    n = np.maximum(dist, 0)
    max_exact = N_BUCKETS // 2
    large = max_exact + (np.log(np.maximum(n, 1) / max_exact) / np.log(REL_MAX_DIST / max_exact)
                         * (N_BUCKETS - max_exact)).astype(np.int32)
    large = np.minimum(large, N_BUCKETS - 1)
    bucket = np.where(n < max_exact, n, large).astype(np.int32)
    valid = (dist >= 0) & (dist < W)
    onehot = (bucket[None] == np.arange(N_BUCKETS)[:, None, None]) & valid[None]
    return onehot.reshape(N_BUCKETS, W * 2 * W).astype(np.float32)


def _bias_expand(rel_bias_t, onehot, name, tn=4096):
    HQ, NB = rel_bias_t.shape
    L = onehot.shape[1]

    def body(r_ref, oh_ref, out_ref):
        out_ref[...] = _dot(r_ref[...], oh_ref[...].astype(F32), NN, precision=lax.Precision.HIGHEST)

    return _pcall(body, name=name, grid=(L // tn,),
                  in_specs=[pl.BlockSpec((HQ, NB), lambda i: (0, 0)), pl.BlockSpec((NB, tn), lambda i: (0, i))],
                  out_specs=pl.BlockSpec((HQ, tn), lambda i: (0, i)),
                  out_shape=jax.ShapeDtypeStruct((HQ, L), F32), compiler_params=_params(("parallel",)))(rel_bias_t, onehot)


def _bias_reduce(dbias, onehot, name, tk=4096):
    HQ, L = dbias.shape
    NB = onehot.shape[0]

    def body(d_ref, oh_ref, out_ref):
        part = _dot(d_ref[...], oh_ref[...].astype(F32), NT, precision=lax.Precision.HIGHEST)
        _accumulate(out_ref, part, pl.program_id(0) == 0)

    return _pcall(body, name=name, grid=(L // tk,),
                  in_specs=[pl.BlockSpec((HQ, tk), lambda i: (0, i)), pl.BlockSpec((NB, tk), lambda i: (0, i))],
                  out_specs=pl.BlockSpec((HQ, NB), lambda i: (0, 0)),
                  out_shape=jax.ShapeDtypeStruct((HQ, NB), F32), compiler_params=_params(("arbitrary",)))(dbias, onehot)


def _stacked_query_index(n_rows_or_cols_axis, shape):
    idx = lax.broadcasted_iota(jnp.int32, shape, n_rows_or_cols_axis)
    return jnp.where(idx >= WINDOW, idx - WINDOW, idx)


def _swa_fwd(qproj, kk, vv, bias_ab, sink_col, gq2, gk2, name):
    S, HQD = qproj.shape
    KVH = kk.shape[0]
    PP = HQD // LANES
    NP = PP // KVH
    W = WINDOW
    nb = S // W

    def body(q_ref, k_ref, v_ref, bias_ref, sink_ref, gq_ref, gk_ref, o_ref, lse_ref, qs_s, kb_s, vb_s):
        is_a = _lane_is_a()
        qn = _pair_norm(q_ref[...], gq_ref[...], is_a) * 0.125
        qs_s[0] = jnp.where(is_a, qn, 0.0).astype(BF16)
        qs_s[1] = jnp.where(is_a, 0.0, qn).astype(BF16)
        kb_s[...] = _pair_norm(k_ref[...], gk_ref[...], is_a).astype(BF16)
        vb_s[...] = v_ref[...].astype(BF16)
        sink = sink_ref[...]
        qi1 = _stacked_query_index(0, (2 * W, W))
        first_valid = lax.broadcasted_iota(jnp.int32, (2 * W, W), 1) <= qi1
        qi2 = _stacked_query_index(0, (2 * W, 2 * W))
        key2 = lax.broadcasted_iota(jnp.int32, (2 * W, 2 * W), 1)
        band_valid = (key2 > qi2) & (key2 <= qi2 + W)
        for n in range(nb):
            rows = slice(n * W, (n + 1) * W)
            keys = slice(0, W) if n == 0 else slice((n - 1) * W, (n + 1) * W)
            lhs = jnp.concatenate([qs_s[0, rows, :], qs_s[1, rows, :]], axis=0)
            s = _dot(lhs, kb_s[keys, :], NT) + (bias_ref[:, W:2 * W] if n == 0 else bias_ref[...])
            s = jnp.where(first_valid if n == 0 else band_valid, s, NEG)
            m = jnp.maximum(jnp.max(s, axis=-1, keepdims=True), sink)
            e = jnp.exp(s - m)
            l = jnp.sum(e, axis=-1, keepdims=True) + jnp.exp(sink - m)
            o_ab = _dot(e.astype(BF16), vb_s[keys, :], NN) / l
            o_ref[rows, :] = jnp.where(is_a, o_ab[0:W, :], o_ab[W:2 * W, :]).astype(BF16)
            lse_ref[n] = _as_row(m + jnp.log(l))

    qcols = pl.BlockSpec((S, LANES), lambda a, g: (0, a * NP + g))
    kvs = pl.BlockSpec((None, S, LANES), lambda a, g: (a, 0, 0))
    gain = pl.BlockSpec((1, LANES), lambda a, g: (0, 0))
    return _pcall(body, name=name, grid=(KVH, NP),
                  in_specs=[qcols, kvs, kvs, pl.BlockSpec((None, 2 * W, 2 * W), lambda a, g: (a * NP + g, 0, 0)),
                            pl.BlockSpec((None, 2 * W, 1), lambda a, g: (a * NP + g, 0, 0)), gain, gain],
                  out_specs=[qcols, pl.BlockSpec((None, nb, 1, 2 * W), lambda a, g: (a * NP + g, 0, 0, 0))],
                  out_shape=[jax.ShapeDtypeStruct((S, HQD), BF16), jax.ShapeDtypeStruct((PP, nb, 1, 2 * W), F32)],
                  scratch_shapes=[pltpu.VMEM((2, S, LANES), BF16), pltpu.VMEM((S, LANES), BF16), pltpu.VMEM((S, LANES), BF16)],
                  compiler_params=_params(("parallel", "parallel")))(qproj, kk, vv, bias_ab, sink_col, gq2, gk2)


def _swa_bwd(qproj, kk, vv, bias_t_ab, sink_row, gq2, gk2, lse_row, do, name):
    S, HQD = qproj.shape
    KVH = kk.shape[0]
    PP = HQD // LANES
    NP = PP // KVH
    W = WINDOW
    nb = S // W

    def body(q_ref, k_ref, v_ref, bias_ref, sink_ref, gq_ref, gk_ref, lr_ref, do_ref,
             dq_ref, dk_ref, dv_ref, db_ref, dsink_ref, dgq_ref, dgk_ref,
             qs_s, kb_s, kt_s, vb_s, dob_s, dq_s, dk_s, dv_s):
        a, g = pl.program_id(0), pl.program_id(1)
        is_a = _lane_is_a()
        qn = _pair_norm(q_ref[...], gq_ref[...], is_a) * 0.125
        qs_s[0] = jnp.where(is_a, qn, 0.0).astype(BF16)
        qs_s[1] = jnp.where(is_a, 0.0, qn).astype(BF16)
        kn = _pair_norm(k_ref[...], gk_ref[...], is_a)
        kb_s[...] = kn.astype(BF16)
        kt_s[...] = kn.T.astype(BF16)
        vb_s[...] = v_ref[...].astype(BF16)
        dov = do_ref[...]
        dob_s[0] = jnp.where(is_a, dov, 0.0).astype(BF16)
        dob_s[1] = jnp.where(is_a, 0.0, dov).astype(BF16)
        sink = sink_ref[...]

        @pl.when(g == 0)
        def _():
            dk_s[...] = jnp.zeros((S, LANES), F32)
            dv_s[...] = jnp.zeros((S, LANES), F32)

        qi1 = _stacked_query_index(1, (W, 2 * W))
        first_valid = lax.broadcasted_iota(jnp.int32, (W, 2 * W), 0) <= qi1
        qi2 = _stacked_query_index(1, (2 * W, 2 * W))
        key2 = lax.broadcasted_iota(jnp.int32, (2 * W, 2 * W), 0)
        band_valid = (key2 > qi2) & (key2 <= qi2 + W)
        head_rows = lax.broadcasted_iota(jnp.int32, (LANES, W), 0) < HEAD_DIM
        db = jnp.zeros((2 * W, 2 * W), F32)
        dsk = jnp.zeros((1, 2 * W), F32)
        pend_k = pend_v = None
        for n in range(nb):
            rows = slice(n * W, (n + 1) * W)
            keys = slice(0, W) if n == 0 else slice((n - 1) * W, (n + 1) * W)
            lhs_q = jnp.concatenate([qs_s[0, rows, :], qs_s[1, rows, :]], axis=0)
            lhs_do = jnp.concatenate([dob_s[0, rows, :], dob_s[1, rows, :]], axis=0)
            lse = lr_ref[n]
            s_t = _dot(kb_s[keys, :], lhs_q, NT) + (bias_ref[W:2 * W, :] if n == 0 else bias_ref[...])
            p_t = jnp.where(first_valid if n == 0 else band_valid, jnp.exp(s_t - lse), 0.0)
            dp_t = _dot(vb_s[keys, :], lhs_do, NT)
            delta = jnp.sum(p_t * dp_t, axis=0, keepdims=True)
            ds_t = p_t * (dp_t - delta)
            dsb = ds_t.astype(BF16)
            dsk = dsk - jnp.exp(sink - lse) * delta
            dv_band = _dot(p_t.astype(BF16), lhs_do, NN)
            dk_band = _dot(dsb, lhs_q, NN)
            dq_t = _dot(kt_s[:, keys], dsb, NN)
            dq_s[rows, :] = jnp.where(head_rows, dq_t[:, 0:W], dq_t[:, W:2 * W]).T
            if n == 0:
                db = jnp.concatenate([jnp.zeros((W, 2 * W), F32), ds_t], axis=0)
                pend_k, pend_v = dk_band, dv_band
            else:
                db = db + ds_t
                prev = slice((n - 1) * W, n * W)
                dk_s[prev, :] += pend_k + dk_band[0:W, :]
                dv_s[prev, :] += pend_v + dv_band[0:W, :]
                pend_k, pend_v = dk_band[W:2 * W, :], dv_band[W:2 * W, :]
        tail = slice((nb - 1) * W, nb * W)
        dk_s[tail, :] += pend_k
        dv_s[tail, :] += pend_v
        db_ref[...] = db
        dsink_ref[0] = jnp.broadcast_to(jnp.sum(dsk[:, 0:W], axis=1, keepdims=True), (1, LANES))
        dsink_ref[1] = jnp.broadcast_to(jnp.sum(dsk[:, W:2 * W], axis=1, keepdims=True), (1, LANES))
        dq_raw, dgq = _pair_norm_bwd(q_ref[...], gq_ref[...], dq_s[...] * 0.125, is_a)
        dq_ref[...] = dq_raw.astype(BF16)
        _accumulate(dgq_ref, dgq, jnp.logical_and(a == 0, g == 0))

        @pl.when(jnp.logical_and(a == KVH - 1, g == NP - 1))
        def _():
            _fold_row(dgq_ref)

        @pl.when(g == NP - 1)
        def _():
            dk_raw, dgk = _pair_norm_bwd(k_ref[...], gk_ref[...], _fold_heads(dk_s[...]), is_a)
            dk_ref[...] = dk_raw
            _accumulate(dgk_ref, dgk, a == 0)
            dv_ref[...] = _fold_heads(dv_s[...])

    qcols = pl.BlockSpec((S, LANES), lambda a, g: (0, a * NP + g))
    kvs = pl.BlockSpec((None, S, LANES), lambda a, g: (a, 0, 0))
    sq = pl.BlockSpec((None, 2 * W, 2 * W), lambda a, g: (a * NP + g, 0, 0))
    gain = pl.BlockSpec((1, LANES), lambda a, g: (0, 0))
    ks = jax.ShapeDtypeStruct((KVH, S, LANES), F32)
    gs = jax.ShapeDtypeStruct((1, LANES), F32)
    return _pcall(body, name=name, grid=(KVH, NP),
                  in_specs=[qcols, kvs, kvs, sq, pl.BlockSpec((None, 1, 2 * W), lambda a, g: (a * NP + g, 0, 0)), gain, gain,
                            pl.BlockSpec((None, nb, 1, 2 * W), lambda a, g: (a * NP + g, 0, 0, 0)), qcols],
                  out_specs=[qcols, kvs, kvs, sq, pl.BlockSpec((2, 1, LANES), lambda a, g: (a * NP + g, 0, 0)), gain, gain],
                  out_shape=[jax.ShapeDtypeStruct((S, HQD), BF16), ks, ks, jax.ShapeDtypeStruct((PP, 2 * W, 2 * W), F32),
                             jax.ShapeDtypeStruct((2 * PP, 1, LANES), F32), gs, gs],
                  scratch_shapes=[pltpu.VMEM((2, S, LANES), BF16), pltpu.VMEM((S, LANES), BF16), pltpu.VMEM((LANES, S), BF16),
                                  pltpu.VMEM((S, LANES), BF16), pltpu.VMEM((2, S, LANES), BF16)] + [pltpu.VMEM((S, LANES), F32)] * 3,
                  compiler_params=_params(("arbitrary", "arbitrary")))(qproj, kk, vv, bias_t_ab, sink_row, gq2, gk2, lse_row, do)


def _adamw(w, g, m, v, name, tr=256):
    R, C = w.shape
    tr = min(tr, R)
    assert R % tr == 0

    def body(w_ref, g_ref, m_ref, v_ref, d_ref, m2_ref, v2_ref):
        gv = g_ref[...]
        m2 = ADAM_B1 * m_ref[...] + (1.0 - ADAM_B1) * gv
        v2 = ADAM_B2 * v_ref[...] + (1.0 - ADAM_B2) * jnp.square(gv)
        m_hat = m2 / (1.0 - ADAM_B1 ** ADAM_STEP)
        v_hat = v2 / (1.0 - ADAM_B2 ** ADAM_STEP)
        d_ref[...] = -ADAM_LR * (m_hat / (jnp.sqrt(v_hat) + ADAM_EPS) + ADAM_WD * w_ref[...])
        m2_ref[...] = m2
        v2_ref[...] = v2

    blk = pl.BlockSpec((tr, C), lambda i: (i, 0))
    return _pcall(body, name=name, grid=(R // tr,), in_specs=[blk] * 4, out_specs=[blk] * 3,
                  out_shape=[jax.ShapeDtypeStruct((R, C), F32)] * 3, compiler_params=_params(("parallel",)))(w, g, m, v)


def _sum_core_pair(arr, got, place, name, tr=512):
    P, hr, C = got.shape
    tr = min(tr, hr)
    assert hr % tr == 0
    nt = hr // tr

    def body(place_ref, a_ref, g_ref, o_ref):
        o_ref[...] = (a_ref[...].astype(F32) + g_ref[...].astype(F32)).astype(o_ref.dtype)

    spec = pltpu.PrefetchScalarGridSpec(
        num_scalar_prefetch=1, grid=(P, nt),
        in_specs=[pl.BlockSpec((None, tr, C), lambda j, i, pr: (j, pr[1] * nt + i, 0)),
                  pl.BlockSpec((None, tr, C), lambda j, i, pr: (j, i, 0))],
        out_specs=pl.BlockSpec((None, tr, C), lambda j, i, pr: (j, i, 0)))
    return _pcall(body, name=name, grid_spec=spec, out_shape=jax.ShapeDtypeStruct(got.shape, BF16),
                  compiler_params=_params(("parallel", "parallel")))(place, arr, got)


def _sum_chips(pair, landed, place, name, tr=256):
    _, R, C = landed.shape
    tr = min(tr, R)
    assert R % tr == 0

    def body(place_ref, p_ref, l_ref, o_ref):
        acc = p_ref[...].astype(F32)
        for k in range(3):
            acc = acc + l_ref[k].astype(F32)
        o_ref[...] = acc

    spec = pltpu.PrefetchScalarGridSpec(
        num_scalar_prefetch=1, grid=(R // tr,),
        in_specs=[pl.BlockSpec((None, tr, C), lambda i, pr: (pr[0], i, 0)), pl.BlockSpec((3, tr, C), lambda i, pr: (0, i, 0))],
        out_specs=pl.BlockSpec((None, tr, C), lambda i, pr: (pr[1], i, 0)))
    return _pcall(body, name=name, grid_spec=spec, out_shape=jax.ShapeDtypeStruct((2, R, C), F32),
                  compiler_params=_params(("parallel",)))(place, pair, landed)


def _sum_parts(parts, name, out_dtype, tr=128):
    P, R, C = parts.shape
    tr = min(tr, R)
    assert R % tr == 0, (R, tr)

    def body(p_ref, o_ref):
        acc = p_ref[0].astype(F32)
        for k in range(1, P):
            acc = acc + p_ref[k].astype(F32)
        o_ref[...] = acc.astype(o_ref.dtype)

    return _pcall(body, name=name, grid=(R // tr,), in_specs=[pl.BlockSpec((P, tr, C), lambda i: (0, i, 0))],
                  out_specs=pl.BlockSpec((tr, C), lambda i: (i, 0)),
                  out_shape=jax.ShapeDtypeStruct((R, C), out_dtype), compiler_params=_params(("parallel",)))(parts)


def _place():
    x, y, c = lax.axis_index("x"), lax.axis_index("y"), lax.axis_index("c")
    others = [(1 - x, y), (x, 1 - y), (1 - x, 1 - y)]
    return x, y, c, others


def _half_rows(ref, hh, lead=()):
    hr = ref.shape[-2] // 2
    return ref.at[(*lead, pl.ds(pl.multiple_of(hh * hr, 16), hr), slice(None))]


def _sem_arrays(*counts):
    return [pltpu.SemaphoreType.DMA((k,)) for k in counts]


SEM_SPEC = pl.BlockSpec(memory_space=pltpu.SEMAPHORE)
ANY_SPEC = pl.BlockSpec(memory_space=pl.ANY)
DATAFLOW = pltpu.SideEffectType.DATAFLOW_SIDE_EFFECTING


def _in_hbm(a):
    return pltpu.with_memory_space_constraint(a, pltpu.HBM)


def _gather_copies(srcs, lands, send_sems, recv_sems):
    x, y, c, others = _place()
    me = 2 * x + y

    def copy(w, k, dst_chip, to):
        return pltpu.make_async_remote_copy(src_ref=_half_rows(srcs[w], c), dst_ref=_half_rows(lands[w], c, (dst_chip,)),
                                            send_sem=send_sems.at[3 * w + k], recv_sem=recv_sems.at[3 * w + k],
                                            device_id=to, device_id_type=MESH)

    pairs = [(w, k, cx, cy) for w in range(len(srcs)) for k, (cx, cy) in enumerate(others)]
    return ([copy(w, k, me, (cx, cy, c)) for w, k, cx, cy in pairs],
            [copy(w, k, 2 * cx + cy, (cx, cy, c)) for w, k, cx, cy in pairs])


def _gather_start(shards, name):
    n = len(shards)

    def body(*refs):
        srcs, lands, send_sems, recv_sems, token = refs[:n], refs[n:2 * n], refs[2 * n], refs[2 * n + 1], refs[-1]
        for cp in _gather_copies(srcs, lands, send_sems, recv_sems)[0]:
            cp.start()
        token[...] = jnp.zeros_like(token)

    lands = [lax.empty((N_CHIPS,) + s.shape, s.dtype) for s in shards]
    outs = _pcall(
        body, name=name, in_specs=[HBM_SPEC] * (2 * n),
        out_specs=[SEM_SPEC, SEM_SPEC] + [HBM_SPEC] * (2 * n) + [VMEM_SPEC],
        out_shape=[pltpu.SemaphoreType.DMA((3 * n,)), pltpu.SemaphoreType.DMA((3 * n,))]
        + [pltpu.HBM(a.shape, a.dtype) for a in list(shards) + lands] + [jax.ShapeDtypeStruct((8, LANES), F32)],
        input_output_aliases={i: 2 + i for i in range(2 * n)},
        compiler_params=pltpu.CompilerParams(has_side_effects=DATAFLOW),
    )(*[_in_hbm(a) for a in list(shards) + lands])
    return outs[0], outs[1], outs[2:2 + n], outs[2 + n:2 + 2 * n], outs[-1]


def _gather_wait(started, after, name):
    send_sems, recv_sems, srcs, lands, _ = started
    n = len(srcs)

    def body(*refs):
        src_refs, land_refs, send_ref, recv_ref = refs[:n], refs[n:2 * n], refs[2 * n], refs[2 * n + 1]
        outgoing, incoming = _gather_copies(src_refs, land_refs, send_ref, recv_ref)
        for out_cp, in_cp in zip(outgoing, incoming):
            out_cp.wait_send()
            in_cp.wait_recv()

    outs = _pcall(
        body, name=name, in_specs=[HBM_SPEC] * (2 * n) + [SEM_SPEC, SEM_SPEC, ANY_SPEC], out_specs=[HBM_SPEC] * (2 * n),
        out_shape=[pltpu.HBM(a.shape, a.dtype) for a in list(srcs) + list(lands)],
        input_output_aliases={i: i for i in range(2 * n)},
        compiler_params=pltpu.CompilerParams(has_side_effects=DATAFLOW),
    )(*srcs, *lands, send_sems, recv_sems, after)
    return outs[:n], outs[n:]


def _gather_pass_on(shards, lands, name):
    n = len(shards)
    per = 4

    def body(*refs):
        srcs, bufs = refs[:n], refs[2 * n:3 * n]
        send_sems, recv_sems = refs[3 * n:]
        x, y, c, others = _place()
        me = 2 * x + y
        sibling = (x, y, 1 - c)

        def copy(w, k, src, dst):
            return pltpu.make_async_remote_copy(src_ref=src, dst_ref=dst, send_sem=send_sems.at[per * w + k],
                                                recv_sem=recv_sems.at[per * w + k], device_id=sibling, device_id_type=MESH)

        sends, recvs = [], []
        for w in range(n):
            for k, (cx, cy) in enumerate(others):
                mine, theirs = _half_rows(bufs[w], c, (2 * cx + cy,)), _half_rows(bufs[w], 1 - c, (2 * cx + cy,))
                sends.append(copy(w, k, mine, mine))
                recvs.append(copy(w, k, theirs, theirs))
            sends.append(copy(w, 3, srcs[w], bufs[w].at[me]))
            recvs.append(sends[-1])
        for cp in sends:
            cp.start()
        for snd, rcv in zip(sends, recvs):
            snd.wait_send()
            rcv.wait_recv()

    return _pcall(body, name=name, in_specs=[HBM_SPEC] * (2 * n), out_specs=[HBM_SPEC] * n,
                  out_shape=[jax.ShapeDtypeStruct(l.shape, l.dtype) for l in lands],
                  input_output_aliases={n + w: w for w in range(n)},
                  scratch_shapes=_sem_arrays(per * n, per * n))(*shards, *lands)


def _scatter_copies(srcs, lands, send_sems, recv_sems):
    x, y, c, others = _place()
    return [pltpu.make_async_remote_copy(src_ref=srcs[w].at[2 * cx + cy], dst_ref=lands[w].at[k],
                                         send_sem=send_sems.at[3 * w + k], recv_sem=recv_sems.at[3 * w + k],
                                         device_id=(cx, cy, c), device_id_type=MESH)
            for w in range(len(srcs)) for k, (cx, cy) in enumerate(others)]


def _scatter_start(parts, name):
    n = len(parts)

    def body(*refs):
        srcs, lands, send_sems, recv_sems, token = refs[:n], refs[n:2 * n], refs[2 * n], refs[2 * n + 1], refs[-1]
        for cp in _scatter_copies(srcs, lands, send_sems, recv_sems):
            cp.start()
        token[...] = jnp.zeros_like(token)

    lands = [lax.empty((3,) + p.shape[1:], p.dtype) for p in parts]
    outs = _pcall(
        body, name=name, in_specs=[HBM_SPEC] * (2 * n), out_specs=[SEM_SPEC, SEM_SPEC] + [HBM_SPEC] * (2 * n) + [VMEM_SPEC],
        out_shape=[pltpu.SemaphoreType.DMA((3 * n,)), pltpu.SemaphoreType.DMA((3 * n,))]
        + [pltpu.HBM(a.shape, a.dtype) for a in list(parts) + lands] + [jax.ShapeDtypeStruct((8, LANES), F32)],
        input_output_aliases={i: 2 + i for i in range(2 * n)},
        compiler_params=pltpu.CompilerParams(has_side_effects=DATAFLOW),
    )(*[_in_hbm(a) for a in list(parts) + lands])
    return (outs[0], outs[1], outs[2:2 + n], outs[2 + n:2 + 2 * n]), outs[-1]


def _scatter_wait(started, after, name):
    send_sems, recv_sems, srcs, lands = started
    n = len(srcs)

    def body(*refs):
        for cp in _scatter_copies(refs[:n], refs[n:2 * n], refs[2 * n], refs[2 * n + 1]):
            cp.wait_send()
            cp.wait_recv()

    outs = _pcall(
        body, name=name, in_specs=[HBM_SPEC] * (2 * n) + [SEM_SPEC, SEM_SPEC, ANY_SPEC], out_specs=[HBM_SPEC] * (2 * n),
        out_shape=[pltpu.HBM(a.shape, a.dtype) for a in list(srcs) + list(lands)],
        input_output_aliases={i: i for i in range(2 * n)},
        compiler_params=pltpu.CompilerParams(has_side_effects=DATAFLOW),
    )(*srcs, *lands, send_sems, recv_sems, after)
    return outs[:n], outs[n:]


def _allgather_group(shards, name):
    n = len(shards)
    per = 7

    def body(*refs):
        ins, outs = refs[:n], refs[n:2 * n]
        send_sems, recv_sems = refs[2 * n:]
        x, y, c, others = _place()
        me = 2 * x + y
        sibling = (x, y, 1 - c)

        def copy(w, k, src, dst, to):
            return pltpu.make_async_remote_copy(src_ref=src, dst_ref=dst, send_sem=send_sems.at[per * w + k],
                                                recv_sem=recv_sems.at[per * w + k], device_id=to, device_id_type=MESH)

        first = [copy(w, k, _half_rows(ins[w], c), _half_rows(outs[w], c, (me,)), (cx, cy, c))
                 for w in range(n) for k, (cx, cy) in enumerate(others)]
        own = [copy(w, 6, ins[w], outs[w].at[me], sibling) for w in range(n)]
        for cp in first + own:
            cp.start()
        passed = []
        for w in range(n):
            for k, (cx, cy) in enumerate(others):
                landed = _half_rows(outs[w], c, (2 * cx + cy,))
                copy(w, k, landed, landed, sibling).wait_recv()
                passed.append(copy(w, 3 + k, landed, landed, sibling))
                passed[-1].start()
        for w in range(n):
            for k, (cx, cy) in enumerate(others):
                theirs = _half_rows(outs[w], 1 - c, (2 * cx + cy,))
                copy(w, 3 + k, theirs, theirs, sibling).wait_recv()
            own[w].wait_recv()
        for cp in first + passed + own:
            cp.wait_send()

    return _pcall(body, name=name, in_specs=[HBM_SPEC] * n, out_specs=[HBM_SPEC] * n,
                  out_shape=[jax.ShapeDtypeStruct((N_CHIPS,) + s.shape, s.dtype) for s in shards],
                  scratch_shapes=_sem_arrays(per * n, per * n))(*shards)


def _swap_halves_group(arrs, name):
    n = len(arrs)

    def body(*refs):
        ins, gots = refs[:n], refs[n:2 * n]
        send_sems, recv_sems = refs[2 * n:]
        x, y, c, _ = _place()
        swaps = [pltpu.make_async_remote_copy(src_ref=_half_rows(ins[w], 1 - c, (slice(None),)), dst_ref=gots[w],
                                              send_sem=send_sems.at[w], recv_sem=recv_sems.at[w],
                                              device_id=(x, y, 1 - c), device_id_type=MESH) for w in range(n)]
        for cp in swaps:
            cp.start()
        for cp in swaps:
            cp.wait()

    half_shapes = [jax.ShapeDtypeStruct((a.shape[0], a.shape[1] // 2, a.shape[2]), a.dtype) for a in arrs]
    return _pcall(body, name=name, in_specs=[HBM_SPEC] * n, out_specs=[HBM_SPEC] * n, out_shape=half_shapes,
                  scratch_shapes=_sem_arrays(n, n))(*arrs)


def _scatter_group(parts, name):
    n = len(parts)

    def body(*refs):
        ins, outs = refs[:n], refs[n:2 * n]
        send_sems, recv_sems = refs[2 * n:]
        x, y, c, others = _place()

        def copy(w, k, src_chip, to):
            return pltpu.make_async_remote_copy(src_ref=ins[w].at[src_chip], dst_ref=outs[w].at[k],
                                                send_sem=send_sems.at[3 * w + k], recv_sem=recv_sems.at[3 * w + k],
                                                device_id=to, device_id_type=MESH)

        sends = [copy(w, k, 2 * cx + cy, (cx, cy, c)) for w in range(n) for k, (cx, cy) in enumerate(others)]
        for cp in sends:
            cp.start()
        for cp in sends:
            cp.wait()

    return _pcall(body, name=name, in_specs=[HBM_SPEC] * n, out_specs=[HBM_SPEC] * n,
                  out_shape=[jax.ShapeDtypeStruct((3,) + p.shape[1:], p.dtype) for p in parts],
                  scratch_shapes=_sem_arrays(3 * n, 3 * n))(*parts)


def _share_halves_group(halves, name):
    n = len(halves)

    def body(*refs):
        bufs = refs[n:2 * n]
        send_sems, recv_sems = refs[2 * n:]
        x, y, c, _ = _place()
        swaps = [pltpu.make_async_remote_copy(src_ref=bufs[w].at[c], dst_ref=bufs[w].at[c], send_sem=send_sems.at[w],
                                              recv_sem=recv_sems.at[w], device_id=(x, y, 1 - c), device_id_type=MESH)
                 for w in range(n)]
        for cp in swaps:
            cp.start()
        for w in range(n):
            swaps[w].wait_send()
            pltpu.make_async_remote_copy(src_ref=bufs[w].at[c], dst_ref=bufs[w].at[1 - c], send_sem=send_sems.at[w],
                                         recv_sem=recv_sems.at[w], device_id=(x, y, 1 - c), device_id_type=MESH).wait_recv()

    return _pcall(body, name=name, in_specs=[HBM_SPEC] * n, out_specs=[HBM_SPEC] * n,
                  out_shape=[jax.ShapeDtypeStruct(h.shape, h.dtype) for h in halves],
                  input_output_aliases={w: w for w in range(n)},
                  scratch_shapes=_sem_arrays(n, n))(*halves)


def _allgather_small(blk, name):
    M, C = blk.shape

    def body(x_ref, out_ref, send_sems, recv_sems, local_sem):
        x, y, c, others = _place()
        me, sibling = (x, y, c), (x, y, 1 - c)

        def rows(px, py, pc):
            return out_ref.at[4 * px + 2 * py + pc]

        def copy(k, block, to, src=None):
            return pltpu.make_async_remote_copy(src_ref=rows(*block) if src is None else src, dst_ref=rows(*block),
                                                send_sem=send_sems.at[k], recv_sem=recv_sems.at[k], device_id=to, device_id_type=MESH)

        mine = pltpu.make_async_copy(x_ref, rows(*me), local_sem)
        mine.start()
        first = [copy(0, me, sibling, src=x_ref)]
        first += [copy(1 + j, me, (*chip, c), src=x_ref) for j, chip in enumerate(others)]
        for cp in first:
            cp.start()
        passed = [copy(4 + j, (*chip, c), sibling) for j, chip in enumerate(others)]
        for j, chip in enumerate(others):
            copy(1 + j, (*chip, c), me).wait_recv()
            passed[j].start()
        copy(0, sibling, me).wait_recv()
        for j, chip in enumerate(others):
            copy(4 + j, (*chip, 1 - c), me).wait_recv()
        for cp in first + passed:
            cp.wait_send()
        mine.wait()

    return _pcall(body, name=name, in_specs=[VMEM_SPEC], out_specs=VMEM_SPEC,
                  out_shape=jax.ShapeDtypeStruct((8, M, C), blk.dtype),
                  scratch_shapes=[pltpu.SemaphoreType.DMA((7,)), pltpu.SemaphoreType.DMA((7,)), pltpu.SemaphoreType.DMA])(blk)


def _pack_rows(n_elems, width=PACK_W, align=PACK_ROW_ALIGN):
    rows = -(-n_elems // width)
    return -(-rows // align) * align


def _pack(arrays, dtype, width=PACK_W, align=PACK_ROW_ALIGN):
    flat = jnp.concatenate([a.astype(dtype).reshape(-1) for a in arrays])
    rows = _pack_rows(flat.shape[0], width, align)
    flat = jnp.pad(flat, (0, rows * width - flat.shape[0]))
    return flat.reshape(rows, width)


def _pack_small(arrays):
    return _pack(arrays, F32, width=128, align=8)


def _unpack(flat, shapes):
    out, off = [], 0
    for shp in shapes:
        n = int(np.prod(shp))
        out.append(flat[..., off:off + n].reshape(flat.shape[:-1] + tuple(shp)))
        off += n
    return out


def _doubled_heads(x2d, n_heads):
    S = x2d.shape[0]
    h = x2d.reshape(S, n_heads, HEAD_DIM).transpose(1, 0, 2)
    return jnp.concatenate([h, h], axis=-1)


def _mlp_fwd(h, g, w_up4, w_down, tag):
    (n,) = _rms_fwd(h, [g], f"rms_mlp{tag}")
    u, a = _matmul(n, w_up4, "nn", f"up{tag}", out_dtypes=(F32, BF16), chipwise="b", tm=2048, tn=512,
                   epilogue=lambda acc: (acc, jnp.square(jnp.maximum(acc, 0.0))))
    h_out = _matmul(a, w_down, "nn", f"down{tag}", extras=(h,), epilogue=lambda acc, res: (res + acc,), tm=1024, tn=1024, tk=1024)
    return h_out, (n, u, a)


def _mlp_bwd(dh_out, dh_out_b, h, g, w_up4, w_down, saved, tag):
    n, u, a = saved
    dw_down = _matmul(a, dh_out_b, "tn", f"dw_down{tag}", out_dtypes=(BF16,), tm=2048, tn=512)
    du = _matmul(dh_out_b, w_down, "nt", f"du{tag}", out_dtypes=(BF16,), extras=(u,), tm=2048, tn=512,
                 epilogue=lambda acc, uu: (acc * (2.0 * jnp.maximum(uu, 0.0)),))
    dw_up = _matmul(n, du, "tn", f"dw_up{tag}", out_dtypes=(BF16,), chipwise="out", tm=1024, tn=512)
    dn = _matmul(du, w_up4, "nt", f"dn_mlp{tag}", tm=1024, tn=1024, tk=w_up4.shape[2], chipwise="b")
    dh, dh_b, (dg,) = _rms_bwd(h, dh_out, [g], [dn], f"rms_mlp_bwd{tag}")
    return dh, dh_b, dg, dw_up, dw_down


def kernel(x, g_attn, g_mlp, w_in_a, b_f, gq_a, gk_a, w_out_a, g_kv, w_kv, gk_b, w_q_b, gq_b, sinks, rel_bias, w_out_b, w_up, w_down, loss_target, m_g_attn, m_g_mlp, m_w_in_a, m_b_f, m_gq_a, m_gk_a, m_w_out_a, m_g_kv, m_w_kv, m_gk_b, m_w_q_b, m_gq_b, m_sinks, m_rel_bias, m_w_out_b, m_w_up, m_w_down, v_g_attn, v_g_mlp, v_w_in_a, v_b_f, v_gq_a, v_gk_a, v_w_out_a, v_g_kv, v_w_kv, v_gk_b, v_w_q_b, v_gq_b, v_sinks, v_rel_bias, v_w_out_b, v_w_up, v_w_down):
    given = dict(locals())
    S, D = x.shape[1], x.shape[2]
    H = D // HEAD_DIM
    KVH = w_kv.shape[1] // (2 * HEAD_DIM)
    kvw = KVH * HEAD_DIM
    hw = H * HEAD_DIM
    W = WINDOW
    nb = S // W
    c_idx = lax.axis_index("c")
    xs, tgt = x[0], loss_target[0]

    shards = {"w_in_a": w_in_a[0], "w_out_a": w_out_a[0], "w_up0": w_up[0], "w_down0": w_down[0], "w_kv": w_kv,
              "w_q_b": w_q_b[0], "w_out_b": w_out_b[0], "w_up1": w_up[1], "w_down1": w_down[1]}
    parts = list(shards)
    lane_pad = lambda a: jnp.pad(a, [(0, 0)] * (a.ndim - 1) + [(0, (-a.shape[-1]) % LANES)])
    n_in_shard = w_in_a.shape[2]
    groups = [("w_in_a", "w_out_a"), ("w_up0", "w_down0"), ("w_kv", "w_q_b", "w_out_b", "w_up1", "w_down1")]
    started = [_gather_start([lane_pad(shards[n].astype(BF16)) for n in grp], f"gather_start{i}") for i, grp in enumerate(groups)]
    gathered = {}

    def finish_gather(i, after):
        srcs, lands = _gather_wait(started[i], after, f"gather_wait{i}")
        gathered.update(zip(groups[i], _gather_pass_on(srcs, lands, f"gather_pass_on{i}")))

    vec = lambda a: a.reshape(1, -1)
    twice = lambda a: jnp.tile(a.reshape(1, -1), (1, 2))

    g_attn0 = vec(g_attn[0]) + sum(st[4][0, 0] for st in started)
    (n0,) = _rms_fwd(xs, [g_attn0], "rms_attn0")
    finish_gather(0, n0)
    win = jnp.moveaxis(gathered["w_in_a"][:, :, :n_in_shard], 0, 1).reshape(D, -1)
    win = jnp.pad(win, ((0, 0), (0, (-win.shape[1]) % 128)))
    wout_a = gathered["w_out_a"].reshape(-1, D)
    n_in = win.shape[1]
    tile_in = 640 if n_in % 640 == 0 else 128
    proj =_matmul(n0, win, "nn", "proj_in", tm=2048, tn=tile_in)
    zt = proj[:, 3 * hw:3 * hw + H].T
    c_row = _gate_fwd(zt, b_f.reshape(H, 1), "gate_fwd")
    c_row3 = c_row.reshape(H, 1, S)
    o_a, lse_a = _fox_fwd(proj, c_row3, twice(gq_a[0]), twice(gk_a[0]), H, "fox_fwd")
    h1 = _matmul(o_a, wout_a, "nn", "out_a", extras=(xs,), epilogue=lambda acc, res: (res + acc,), tm=1024, tn=1024)
    finish_gather(1, h1)
    wup = [gathered["w_up0"], None]
    wdown = [gathered["w_down0"].reshape(-1, D), None]
    h2, mlp0 = _mlp_fwd(h1, vec(g_mlp[0]), wup[0], wdown[0], "0")

    finish_gather(2, h2)
    wq_b, wout_b = gathered["w_q_b"].reshape(-1, D), gathered["w_out_b"].reshape(-1, D)
    wkv = gathered["w_kv"].reshape(D, -1)
    wup[1], wdown[1] = gathered["w_up1"], gathered["w_down1"].reshape(-1, D)
    nkv, n2 = _rms_fwd(h2, [vec(g_kv), vec(g_attn[1])], "rms_attn1")
    kv = _matmul(nkv, wkv, "nn", "proj_kv", tm=2048)
    kk, vv = _doubled_heads(kv[:, :kvw], KVH), _doubled_heads(kv[:, kvw:], KVH)
    q2 = _matmul(n2, wq_b, "nn", "proj_q", tm=1024, tn=1024)
    onehot = jnp.asarray(_bucket_onehot(), dtype=BF16)
    bias = _bias_expand(rel_bias.T, onehot, "bias_expand").reshape(H, W, 2 * W)
    bias_ab = bias.reshape(H // 2, 2 * W, 2 * W)
    bias_t_ab = bias.reshape(H // 2, 2, W, 2 * W).transpose(0, 3, 1, 2).reshape(H // 2, 2 * W, 2 * W)
    sink_ab = jnp.repeat(sinks[0].reshape(H // 2, 2), W, axis=1)
    o_b, lse_b = _swa_fwd(q2, kk, vv, bias_ab, sink_ab.reshape(H // 2, 2 * W, 1), twice(gq_b[0]), twice(gk_b), "swa_fwd")
    h3 = _matmul(o_b, wout_b, "nn", "out_b", extras=(h2,), epilogue=lambda acc, res: (res + acc,), tm=1024, tn=1024)
    h4, mlp1 = _mlp_fwd(h3, vec(g_mlp[1]), wup[1], wdown[1], "1")

    dh4, dh4_b, loss_part = _loss_head(h4, tgt, "loss_head")

    place = jnp.stack([2 * lax.axis_index("x") + lax.axis_index("y"), c_idx]).astype(jnp.int32)
    scattering = []

    def start_reduce(named):
        names = list(named)
        got = _swap_halves_group([named[n] for n in names], "swap_grad_halves_" + names[0])
        pair_sums = [_sum_core_pair(named[n], g, place, "sum_core_pair_" + n) for n, g in zip(names, got)]
        started_scatter, token = _scatter_start(pair_sums, "scatter_start_" + names[0])
        scattering.append((names, started_scatter))
        return token[0:1, :]

    dh3, dh3_b, dg_mlp1, dw_up1, dw_down1 = _mlp_bwd(dh4, dh4_b, h3, vec(g_mlp[1]), wup[1], wdown[1], mlp1, "1")
    tie1 = start_reduce({"w_down1": dw_down1.reshape(N_CHIPS, -1, D), "w_up1": dw_up1})
    dw_out_b = _matmul(o_b, dh3_b, "tn", "dw_out_b", out_dtypes=(BF16,), tm=1024, tn=1024)
    do_b = _matmul(dh3_b, wout_b, "nt", "do_b", tm=1024, tn=1024)
    dq2, dk2, dv2, dbias_t_ab, dsink, dgq_b, dgk_b = _swa_bwd(
        q2, kk, vv, bias_t_ab, sink_ab.reshape(H // 2, 1, 2 * W), twice(gq_b[0]) + tie1, twice(gk_b),
        lse_b, do_b, "swa_bwd")
    dbias = dbias_t_ab.reshape(H // 2, 2 * W, 2, W).transpose(0, 2, 3, 1).reshape(H, W * 2 * W)
    d_rel_bias = _bias_reduce(dbias, onehot, "bias_reduce").T
    dw_q_b = _matmul(n2, dq2, "tn", "dw_q_b", out_dtypes=(BF16,), tm=1024, tn=1024)
    dn2 = _matmul(dq2, wq_b, "nt", "dn2", tm=1024, tn=1024)
    dkv = jnp.concatenate([dk2[h, :, :HEAD_DIM] for h in range(KVH)] + [dv2[h, :, :HEAD_DIM] for h in range(KVH)],
                          axis=1).astype(BF16)
    dw_kv = _matmul(nkv, dkv, "tn", "dw_kv", out_dtypes=(BF16,), tm=1024)
    dnkv = _matmul(dkv, wkv, "nt", "dnkv", tm=1024, tn=1024)
    tie2 = start_reduce({"w_out_b": dw_out_b.reshape(N_CHIPS, -1, D), "w_q_b": dw_q_b.reshape(N_CHIPS, -1, D),
                         "w_kv": dw_kv.reshape(N_CHIPS, -1, 2 * kvw)})
    dh2, dh2_b, (dg_kv, dg_attn1) = _rms_bwd(h2, dh3, [vec(g_kv) + tie2[:, :1], vec(g_attn[1])], [dnkv, dn2], "rms_attn1_bwd")

    dh1, dh1_b, dg_mlp0, dw_up0, dw_down0 = _mlp_bwd(dh2, dh2_b, h1, vec(g_mlp[0]), wup[0], wdown[0], mlp0, "0")
    tie3 = start_reduce({"w_down0": dw_down0.reshape(N_CHIPS, -1, D), "w_up0": dw_up0})
    dw_out_a = _matmul(o_a, dh1_b, "tn", "dw_out_a", out_dtypes=(BF16,), tm=1024, tn=1024)
    do_a = _matmul(dh1_b, wout_a, "nt", "do_a", tm=1024, tn=1024)
    dq_a, dk_a, dv_a, dc_row, dgq_a, dgk_a = _fox_bwd(
        proj, c_row3, twice(gq_a[0]) + tie3, twice(gk_a[0]), lse_a, do_a, H, "fox_bwd")
    dzt, db_f = _gate_bwd(dc_row.reshape(H, S), zt, b_f.reshape(H, 1), "gate_bwd")
    dproj = jnp.concatenate([dq_a, dk_a, dv_a, dzt.T.astype(BF16), jnp.zeros((S, n_in - 3 * hw - H), BF16)], axis=1)
    dw_in = _matmul(n0, dproj, "tn", "dw_in", out_dtypes=(BF16,), tm=1024, tn=tile_in)
    dw_in4 = lane_pad(dw_in[:, :3 * hw + H].reshape(D, N_CHIPS, -1).transpose(1, 0, 2))
    tie4 = start_reduce({"w_out_a": dw_out_a.reshape(N_CHIPS, -1, D), "w_in_a": dw_in4})
    dn0 = _matmul(dproj, win, "nt", "dn0", tm=1024, tn=1024, tk=tile_in)
    grad_x, _, (dg_attn0,) = _rms_bwd(xs, dh1, [vec(g_attn[0]) + tie4[:, :1]], [dn0], "rms_attn0_bwd")

    reduced = {}
    for names, started_scatter in scattering:
        pair_sums, landed = _scatter_wait(started_scatter, grad_x, "scatter_wait_" + names[0])
        halves = [_sum_chips(p, l, place, "sum_chips_" + n) for n, p, l in zip(names, pair_sums, landed)]
        for n, r in zip(names, _share_halves_group(halves, "share_reduced_halves_" + names[0])):
            reduced[n] = r.reshape(-1, r.shape[2])[:, :shards[n].shape[1]]
    big_grads = [reduced["w_in_a"][None], reduced["w_out_a"][None], reduced["w_kv"], reduced["w_q_b"][None],
                 reduced["w_out_b"][None], jnp.stack([reduced["w_up0"], reduced["w_up1"]]),
                 jnp.stack([reduced["w_down0"], reduced["w_down1"]])]

    small_grads = {
        "g_attn": jnp.concatenate([dg_attn0, dg_attn1], axis=0), "g_mlp": jnp.concatenate([dg_mlp0, dg_mlp1], axis=0),
        "b_f": db_f.reshape(1, H), "gq_a": dgq_a[:, :HEAD_DIM], "gk_a": dgk_a[:, :HEAD_DIM], "g_kv": dg_kv.reshape(-1),
        "gk_b": dgk_b[0, :HEAD_DIM], "gq_b": dgq_b[:, :HEAD_DIM], "sinks": dsink[:, 0, 0].reshape(1, H), "rel_bias": d_rel_bias,
    }
    small_shapes = [given[n].shape for n in SMALL] + [(1,)]
    spack = _pack_small([small_grads[n] for n in SMALL] + [loss_part])
    small_sum = _sum_parts(_allgather_small(spack, "allgather_small"), "sum_small", F32, tr=spack.shape[0])
    small_red = _unpack(small_sum.reshape(-1), small_shapes)
    loss = small_red[-1][0]

    grads = dict(zip([n for n, _ in BIG], big_grads))
    grads.update(dict(zip(SMALL, small_red)))
    no_loss = [jnp.zeros((1,), F32)]
    sw = _pack_small([given[n] for n in SMALL] + no_loss)
    sm = _pack_small([given["m_" + n] for n in SMALL] + no_loss)
    sv = _pack_small([given["v_" + n] for n in SMALL] + no_loss)
    sd, sm2, sv2 = _adamw(sw, small_sum, sm, sv, "adamw_small", tr=sw.shape[0])
    delta = dict(zip(SMALL, _unpack(sd.reshape(-1), small_shapes)))
    new_m = dict(zip(SMALL, _unpack(sm2.reshape(-1), small_shapes)))
    new_v = dict(zip(SMALL, _unpack(sv2.reshape(-1), small_shapes)))
    for n, _ in BIG:
        w = given[n]
        two_d = (-1, w.shape[-1])
        d, m2, v2 = _adamw(w.reshape(two_d), grads[n].reshape(two_d), given["m_" + n].reshape(two_d),
                           given["v_" + n].reshape(two_d), "adamw_" + n)
        delta[n], new_m[n], new_v[n] = d.reshape(w.shape), m2.reshape(w.shape), v2.reshape(w.shape)

    order = ["g_attn", "g_mlp", "w_in_a", "b_f", "gq_a", "gk_a", "w_out_a", "g_kv", "w_kv", "gk_b", "w_q_b", "gq_b",
             "sinks", "rel_bias", "w_out_b", "w_up", "w_down"]
    return (loss, grad_x[None], *[grads[n] for n in order], *[delta[n] for n in order],
            *[new_m[n] for n in order], *[new_v[n] for n in order])
```

```python
import numpy as np
import jax
import jax.numpy as jnp
from jax import lax
from jax.experimental import pallas as pl
from jax.experimental.pallas import tpu as pltpu

F32 = jnp.float32
BF16 = jnp.bfloat16
MESH = pl.DeviceIdType.MESH

HEAD_DIM = 64
LANES = 128
WINDOW = 128
N_BUCKETS = 32
REL_MAX_DIST = 128
NORM_EPS = 1e-6
ADAM_LR = 0.001
ADAM_B1 = 0.9
ADAM_B2 = 0.999
ADAM_EPS = 1e-08
ADAM_WD = 0.01
ADAM_STEP = 10
NEG = -1e30
N_CHIPS = 4
PACK_W = 1024
PACK_ROW_ALIGN = 256
VMEM_LIMIT = 56 * 1024 * 1024
HBM_SPEC = pl.BlockSpec(memory_space=pltpu.HBM)
VMEM_SPEC = pl.BlockSpec(memory_space=pltpu.VMEM)

BIG = (("w_in_a", 2), ("w_out_a", 1), ("w_kv", 0), ("w_q_b", 1), ("w_out_b", 1), ("w_up", 2), ("w_down", 1))
SMALL = ("g_attn", "g_mlp", "b_f", "gq_a", "gk_a", "g_kv", "gk_b", "gq_b", "sinks", "rel_bias")


def _pcall(body, **kw):
    return pl.pallas_call(body, **kw)


def _params(sem=None):
    return pltpu.CompilerParams(dimension_semantics=sem, vmem_limit_bytes=VMEM_LIMIT)


def _rinv(x):
    return lax.rsqrt(jnp.mean(x * x, axis=-1, keepdims=True) + NORM_EPS)


def _dot(a, b, dims, precision=None):
    return lax.dot_general(a, b, (dims, ((), ())), precision=precision, preferred_element_type=F32)


NN = ((1,), (0,))
NT = ((1,), (1,))
TN = ((0,), (0,))


def _accumulate(ref, val, first):
    @pl.when(first)
    def _():
        ref[...] = val

    @pl.when(jnp.logical_not(first))
    def _():
        ref[...] += val


def _matmul(a, b, mode, name, out_dtypes=(F32,), extras=(), epilogue=None, tm=512, tn=512, tk=None, chipwise=None):
    if chipwise == "b":
        nc = b.shape[2]
        M, K = a.shape
        (K2, N) = (b.shape[1], N_CHIPS * nc) if mode == "nn" else (N_CHIPS * nc, b.shape[1])
    elif mode == "nn":
        (M, K), (K2, N) = a.shape, b.shape
    elif mode == "nt":
        (M, K), (N, K2) = a.shape, b.shape
    else:
        (K, M), (K2, N) = a.shape, b.shape
    assert K == K2, (a.shape, b.shape, mode)
    tm, tn = min(tm, M), min(tn, N)
    tk = K if tk is None else tk
    assert M % tm == 0 and N % tn == 0 and K % tk == 0, (M, N, K, tm, tn, tk)
    nk = K // tk
    dims = {"nn": NN, "nt": NT, "tn": TN}[mode]
    a_spec = pl.BlockSpec((tk, tm), lambda i, j, k: (k, i)) if mode == "tn" else pl.BlockSpec((tm, tk), lambda i, j, k: (i, k))
    b_spec = pl.BlockSpec((tn, tk), lambda i, j, k: (j, k)) if mode == "nt" else pl.BlockSpec((tk, tn), lambda i, j, k: (k, j))
    o_spec = pl.BlockSpec((tm, tn), lambda i, j, k: (i, j))
    out_shape = (M, N)
    if chipwise == "b" and mode == "nn":
        per = nc // tn
        assert tk == K and nc % tn == 0
        b_spec = pl.BlockSpec((None, tk, tn), lambda i, j, k: (j // per, 0, j % per))
    elif chipwise == "b":
        assert mode == "nt" and tk == nc
        b_spec = pl.BlockSpec((None, tn, tk), lambda i, j, k: (k, j, 0))
    elif chipwise == "out":
        per = (N // N_CHIPS) // tn
        assert (N // N_CHIPS) % tn == 0
        o_spec = pl.BlockSpec((None, tm, tn), lambda i, j, k: (j // per, i, j % per))
        out_shape = (N_CHIPS, M, N // N_CHIPS)
        assert not extras
    n_ex, n_out = len(extras), len(out_dtypes)

    def body(*refs):
        a_ref, b_ref = refs[0], refs[1]
        ex_refs = refs[2:2 + n_ex]
        out_refs = refs[2 + n_ex:2 + n_ex + n_out]
        acc_ref = refs[2 + n_ex + n_out]
        k = pl.program_id(2)
        part = _dot(a_ref[...].astype(BF16), b_ref[...].astype(BF16), dims)

        @pl.when(k == 0)
        def _():
            acc_ref[...] = part

        @pl.when(k > 0)
        def _():
            acc_ref[...] += part

        @pl.when(k == nk - 1)
        def _():
            acc = acc_ref[...]
            outs = (acc,) if epilogue is None else epilogue(acc, *[r[...] for r in ex_refs])
            for r, o in zip(out_refs, outs):
                r[...] = o.astype(r.dtype)

    outs = _pcall(
        body, name=name, grid=(M // tm, N // tn, nk),
        in_specs=[a_spec, b_spec] + [o_spec] * n_ex,
        out_specs=[o_spec] * n_out,
        out_shape=[jax.ShapeDtypeStruct(out_shape, dt) for dt in out_dtypes],
        scratch_shapes=[pltpu.VMEM((tm, tn), F32)],
        compiler_params=_params(("parallel", "parallel", "arbitrary")),
    )(a, b, *extras)
    return outs[0] if n_out == 1 else outs


def _rms_fwd(x, gains, name, ts=256):
    S, D = x.shape
    ts = min(ts, S)
    n = len(gains)

    def body(*refs):
        x_ref, g_refs, o_refs = refs[0], refs[1:1 + n], refs[1 + n:]
        xv = x_ref[...]
        xh = xv * _rinv(xv)
        for g_ref, o_ref in zip(g_refs, o_refs):
            o_ref[...] = (xh * g_ref[...]).astype(BF16)

    row = pl.BlockSpec((ts, D), lambda i: (i, 0))
    vec = pl.BlockSpec((1, D), lambda i: (0, 0))
    return _pcall(body, name=name, grid=(S // ts,), in_specs=[row] + [vec] * n, out_specs=[row] * n,
                  out_shape=[jax.ShapeDtypeStruct((S, D), BF16)] * n, compiler_params=_params(("parallel",)))(x, *gains)


def _rms_bwd(x, dres, gains, dns, name, ts=256):
    S, D = x.shape
    ts = min(ts, S)
    n = len(gains)

    def body(*refs):
        x_ref, dres_ref = refs[0], refs[1]
        g_refs, dn_refs = refs[2:2 + n], refs[2 + n:2 + 2 * n]
        dx_ref, dxb_ref, dg_refs = refs[2 + 2 * n], refs[3 + 2 * n], refs[4 + 2 * n:]
        xv = x_ref[...]
        r = _rinv(xv)
        xh = xv * r
        dx = dres_ref[...]
        first = pl.program_id(0) == 0
        for g_ref, dn_ref, dg_ref in zip(g_refs, dn_refs, dg_refs):
            dn = dn_ref[...].astype(F32)
            _accumulate(dg_ref, jnp.sum(dn * xh, axis=0, keepdims=True), first)
            dxh = dn * g_ref[...]
            dx = dx + r * (dxh - xh * jnp.mean(dxh * xh, axis=-1, keepdims=True))
        dx_ref[...] = dx
        dxb_ref[...] = dx.astype(BF16)

    row = pl.BlockSpec((ts, D), lambda i: (i, 0))
    vec = pl.BlockSpec((1, D), lambda i: (0, 0))
    outs = _pcall(body, name=name, grid=(S // ts,), in_specs=[row, row] + [vec] * n + [row] * n,
                  out_specs=[row, row] + [vec] * n,
                  out_shape=[jax.ShapeDtypeStruct((S, D), F32), jax.ShapeDtypeStruct((S, D), BF16)]
                  + [jax.ShapeDtypeStruct((1, D), F32)] * n,
                  compiler_params=_params(("arbitrary",)))(x, dres, *gains, *dns)
    return outs[0], outs[1], outs[2:]


def _loss_head(h, tgt, name, ts=256):
    S, D = h.shape
    ts = min(ts, S)

    def body(h_ref, t_ref, dh_ref, dhb_ref, loss_ref):
        err = h_ref[...] - t_ref[...]
        dh = err * (1.0 / D)
        dh_ref[...] = dh
        dhb_ref[...] = dh.astype(BF16)
        part = 0.5 * jnp.sum(jnp.mean(err * err, axis=-1, keepdims=True), axis=0, keepdims=True)
        _accumulate(loss_ref, part, pl.program_id(0) == 0)

    row = pl.BlockSpec((ts, D), lambda i: (i, 0))
    return _pcall(body, name=name, grid=(S // ts,), in_specs=[row, row],
                  out_specs=[row, row, pl.BlockSpec((1, 1), lambda i: (0, 0))],
                  out_shape=[jax.ShapeDtypeStruct((S, D), F32), jax.ShapeDtypeStruct((S, D), BF16),
                             jax.ShapeDtypeStruct((1, 1), F32)],
                  compiler_params=_params(("arbitrary",)))(h, tgt)


def _gate_fwd(zt, bf, name):
    H, S = zt.shape
    nb = S // 128

    def body(z_ref, b_ref, c_ref):
        z = z_ref[...] + b_ref[...]
        lf = jnp.minimum(z, 0.0) - jnp.log(1.0 + jnp.exp(-jnp.abs(z)))
        upper = (lax.broadcasted_iota(jnp.int32, (128, 128), 0) <= lax.broadcasted_iota(jnp.int32, (128, 128), 1)).astype(F32)
        carry = jnp.zeros((H, 1), F32)
        for blk in range(nb):
            cs = _dot(lf[:, blk * 128:(blk + 1) * 128], upper, NN, precision=lax.Precision.HIGHEST) + carry
            c_ref[:, blk * 128:(blk + 1) * 128] = cs
            carry = cs[:, 127:128]

    return _pcall(body, name=name, in_specs=[VMEM_SPEC, VMEM_SPEC], out_specs=VMEM_SPEC,
                  out_shape=jax.ShapeDtypeStruct((H, S), F32))(zt, bf)


def _gate_bwd(dct, zt, bf, name):
    H, S = zt.shape
    nb = S // 128

    def body(dc_ref, z_ref, b_ref, dz_ref, db_ref):
        z = z_ref[...] + b_ref[...]
        e = jnp.exp(-jnp.abs(z))
        sig_neg = jnp.where(z >= 0, e, 1.0) / (1.0 + e)
        lower = (lax.broadcasted_iota(jnp.int32, (128, 128), 0) >= lax.broadcasted_iota(jnp.int32, (128, 128), 1)).astype(F32)
        dc = dc_ref[...]
        carry = jnp.zeros((H, 1), F32)
        db = jnp.zeros((H, 1), F32)
        for blk in reversed(range(nb)):
            sl = slice(blk * 128, (blk + 1) * 128)
            dlf = _dot(dc[:, sl], lower, NN, precision=lax.Precision.HIGHEST) + carry
            carry = dlf[:, 0:1]
            dz = dlf * sig_neg[:, sl]
            dz_ref[:, sl] = dz
            db = db + jnp.sum(dz, axis=1, keepdims=True)
        db_ref[...] = db

    return _pcall(body, name=name, in_specs=[VMEM_SPEC] * 3, out_specs=[VMEM_SPEC] * 2,
                  out_shape=[jax.ShapeDtypeStruct((H, S), F32), jax.ShapeDtypeStruct((H, 1), F32)])(dct, zt, bf)


def _lane_is_a():
    return lax.broadcasted_iota(jnp.int32, (1, LANES), 1) < HEAD_DIM


def _per_head_mean(x, is_a):
    sa = jnp.sum(jnp.where(is_a, x, 0.0), axis=-1, keepdims=True)
    sb = jnp.sum(jnp.where(is_a, 0.0, x), axis=-1, keepdims=True)
    return jnp.where(is_a, sa, sb) / HEAD_DIM


def _pair_norm(raw, gain, is_a):
    return raw * lax.rsqrt(_per_head_mean(raw * raw, is_a) + NORM_EPS) * gain


def _pair_norm_bwd(raw, gain, dnormed, is_a):
    r = lax.rsqrt(_per_head_mean(raw * raw, is_a) + NORM_EPS)
    xh = raw * r
    dgain = jnp.sum(dnormed * xh, axis=0, keepdims=True)
    dxh = dnormed * gain
    return r * (dxh - xh * _per_head_mean(dxh * xh, is_a)), dgain


def _fold_heads(x):
    i = lax.broadcasted_iota(jnp.int32, (LANES, LANES), 0)
    j = lax.broadcasted_iota(jnp.int32, (LANES, LANES), 1)
    fold = ((i == j) | (i == j + HEAD_DIM) | (i + HEAD_DIM == j)).astype(F32)
    return _dot(x, fold, NN, precision=lax.Precision.HIGHEST)


def _fold_row(ref):
    ref[...] = _fold_heads(jnp.broadcast_to(ref[...], (8, LANES)))[0:1, :]


def _as_col(row):
    return jnp.broadcast_to(row, (LANES, row.shape[1])).T[:, 0:1]


def _as_row(col):
    return jnp.broadcast_to(col, (col.shape[0], LANES)).T[0:1, :]


def _tri_mask(t, keys_on_rows):
    r = lax.broadcasted_iota(jnp.int32, (t, t), 0)
    c = lax.broadcasted_iota(jnp.int32, (t, t), 1)
    return (r <= c) if keys_on_rows else (r >= c)


def _fox_fwd(proj, c_row, gq2, gk2, n_heads, name, t=256):
    S = proj.shape[0]
    H = n_heads
    P = H // 2
    t = min(t, S)
    nq = S // t

    def body(q_ref, k_ref, v_ref, cr_ref, gq_ref, gk_ref, o_ref, lse_ref, qs_s, kb_s, vb_s):
        is_a = _lane_is_a()
        qn = _pair_norm(q_ref[...], gq_ref[...], is_a) * 0.125
        qs_s[0] = jnp.where(is_a, qn, 0.0).astype(BF16)
        qs_s[1] = jnp.where(is_a, 0.0, qn).astype(BF16)
        kb_s[...] = _pair_norm(k_ref[...], gk_ref[...], is_a).astype(BF16)
        vb_s[...] = v_ref[...].astype(BF16)
        causal = _tri_mask(t, False)
        for i in range(nq):
            t0 = i * t
            rows = slice(t0, t0 + t)
            o_pair = None
            for a in range(2):
                qi = qs_s[a, rows, :]
                ci = _as_col(cr_ref[a, :, rows])
                s_d = jnp.where(causal, _dot(qi, kb_s[rows, :], NT) + ci - cr_ref[a, :, rows], NEG)
                m = jnp.max(s_d, axis=-1, keepdims=True)
                if i > 0:
                    s_l = _dot(qi, kb_s[0:t0, :], NT) + ci - cr_ref[a, :, 0:t0]
                    m = jnp.maximum(m, jnp.max(s_l, axis=-1, keepdims=True))
                p_d = jnp.exp(s_d - m)
                l = jnp.sum(p_d, axis=-1, keepdims=True)
                acc = _dot(p_d.astype(BF16), vb_s[rows, :], NN)
                if i > 0:
                    p_l = jnp.exp(s_l - m)
                    l = l + jnp.sum(p_l, axis=-1, keepdims=True)
                    acc = acc + _dot(p_l.astype(BF16), vb_s[0:t0, :], NN)
                o_a = acc / l
                lse_ref[a, :, rows] = _as_row(m + jnp.log(l))
                o_pair = o_a if a == 0 else jnp.where(is_a, o_pair, o_a)
            o_ref[rows, :] = o_pair.astype(BF16)

    def cols(off):
        return pl.BlockSpec((S, LANES), lambda p: (0, off + p))

    rowv = pl.BlockSpec((2, 1, S), lambda p: (p, 0, 0))
    gain = pl.BlockSpec((1, LANES), lambda p: (0, 0))
    return _pcall(body, name=name, grid=(P,), in_specs=[cols(0), cols(P), cols(2 * P), rowv, gain, gain],
                  out_specs=[cols(0), rowv],
                  out_shape=[jax.ShapeDtypeStruct((S, H * HEAD_DIM), BF16), jax.ShapeDtypeStruct((H, 1, S), F32)],
                  scratch_shapes=[pltpu.VMEM((2, S, LANES), BF16), pltpu.VMEM((S, LANES), BF16), pltpu.VMEM((S, LANES), BF16)],
                  compiler_params=_params(("parallel",)))(proj, proj, proj, c_row, gq2, gk2)


def _fox_bwd(proj, c_row, gq2, gk2, lse_row, do, n_heads, name, t=256):
    S = proj.shape[0]
    H = n_heads
    P = H // 2
    t = min(t, S)
    nq = S // t
    assert t % LANES == 0

    def body(q_ref, k_ref, v_ref, cr_ref, gq_ref, gk_ref, lr_ref, do_ref,
             dq_ref, dk_ref, dv_ref, dc_ref, dgq_ref, dgk_ref,
             qs_s, kb_s, kt_s, vb_s, dob_s, dq_s, dk_s, dv_s, dcs_s, cc_s):
        is_a = _lane_is_a()
        for a in range(2):
            for i in range(nq):
                cc_s[a, i * t:(i + 1) * t, :] = _as_col(cr_ref[a, :, i * t:(i + 1) * t])
        qn = _pair_norm(q_ref[...], gq_ref[...], is_a) * 0.125
        qs_s[0] = jnp.where(is_a, qn, 0.0).astype(BF16)
        qs_s[1] = jnp.where(is_a, 0.0, qn).astype(BF16)
        kn = _pair_norm(k_ref[...], gk_ref[...], is_a)
        kb_s[...] = kn.astype(BF16)
        kt_s[0] = jnp.where(is_a, kn, 0.0).T.astype(BF16)
        kt_s[1] = jnp.where(is_a, 0.0, kn).T.astype(BF16)
        vb_s[...] = v_ref[...].astype(BF16)
        dov = do_ref[...]
        dob_s[0] = jnp.where(is_a, dov, 0.0).astype(BF16)
        dob_s[1] = jnp.where(is_a, 0.0, dov).astype(BF16)
        dk_s[...] = jnp.zeros((S, LANES), F32)
        dv_s[...] = jnp.zeros((S, LANES), F32)
        dcs_s[...] = jnp.zeros((2, S, LANES), F32)
        causal = _tri_mask(t, True)
        for i in range(nq):
            t0 = i * t
            rows = slice(t0, t0 + t)
            dq_t = jnp.zeros((LANES, t), F32)
            for a in range(2):
                qi = qs_s[a, rows, :]
                doi = dob_s[a, rows, :]
                cri = cr_ref[a, :, rows]
                lri = lr_ref[a, :, rows]

                def probs(keys, masked, a=a, qi=qi, doi=doi, cri=cri, lri=lri):
                    p_t = jnp.exp(_dot(kb_s[keys, :], qi, NT) + cri - cc_s[a, keys, :] - lri)
                    if masked:
                        p_t = jnp.where(causal, p_t, 0.0)
                    return p_t, _dot(vb_s[keys, :], doi, NT)

                parts = [(rows,) + probs(rows, True)]
                if i > 0:
                    parts.append((slice(0, t0),) + probs(slice(0, t0), False))
                delta = sum(jnp.sum(p_t * dp_t, axis=0, keepdims=True) for _, p_t, dp_t in parts)
                for keys, p_t, dp_t in parts:
                    ds_t = p_t * (dp_t - delta)
                    dsb = ds_t.astype(BF16)
                    dv_s[keys, :] += _dot(p_t.astype(BF16), doi, NN)
                    dk_s[keys, :] += _dot(dsb, qi, NN)
                    dq_t = dq_t + _dot(kt_s[a, :, keys], dsb, NN)
                    dcs_s[a, keys, :] += sum(ds_t[:, b * LANES:(b + 1) * LANES] for b in range(t // LANES))
            dq_s[rows, :] = dq_t.T
        first = pl.program_id(0) == 0
        last = pl.program_id(0) == P - 1
        dq_raw, dgq = _pair_norm_bwd(q_ref[...], gq_ref[...], dq_s[...] * 0.125, is_a)
        dq_ref[...] = dq_raw.astype(BF16)
        _accumulate(dgq_ref, dgq, first)
        dk_raw, dgk = _pair_norm_bwd(k_ref[...], gk_ref[...], dk_s[...], is_a)
        dk_ref[...] = dk_raw.astype(BF16)
        _accumulate(dgk_ref, dgk, first)
        dv_ref[...] = dv_s[...].astype(BF16)
        for a in range(2):
            for i in range(nq):
                rows = slice(i * t, (i + 1) * t)
                dc_ref[a, :, rows] = _as_row(-jnp.sum(dcs_s[a, rows, :], axis=1, keepdims=True))

        @pl.when(last)
        def _():
            _fold_row(dgq_ref)
            _fold_row(dgk_ref)

    def cols(off):
        return pl.BlockSpec((S, LANES), lambda p: (0, off + p))

    rowv = pl.BlockSpec((2, 1, S), lambda p: (p, 0, 0))
    gain = pl.BlockSpec((1, LANES), lambda p: (0, 0))
    wide = jax.ShapeDtypeStruct((S, H * HEAD_DIM), BF16)
    gs = jax.ShapeDtypeStruct((1, LANES), F32)
    return _pcall(body, name=name, grid=(P,),
                  in_specs=[cols(0), cols(P), cols(2 * P), rowv, gain, gain, rowv, cols(0)],
                  out_specs=[cols(0), cols(0), cols(0), rowv, gain, gain],
                  out_shape=[wide, wide, wide, jax.ShapeDtypeStruct((H, 1, S), F32), gs, gs],
                  scratch_shapes=[pltpu.VMEM((2, S, LANES), BF16), pltpu.VMEM((S, LANES), BF16), pltpu.VMEM((2, LANES, S), BF16),
                                  pltpu.VMEM((S, LANES), BF16), pltpu.VMEM((2, S, LANES), BF16)]
                  + [pltpu.VMEM((S, LANES), F32)] * 3 + [pltpu.VMEM((2, S, LANES), F32), pltpu.VMEM((2, S, 1), F32)],
                  compiler_params=_params(("arbitrary",)))(proj, proj, proj, c_row, gq2, gk2, lse_row, do)


def _bucket_onehot():
    W = WINDOW
    dist = np.arange(W)[:, None] + W - np.arange(2 * W)[None, :]
    n = np.maximum(dist, 0)
    max_exact = N_BUCKETS // 2
    large = max_exact + (np.log(np.maximum(n, 1) / max_exact) / np.log(REL_MAX_DIST / max_exact)
                         * (N_BUCKETS - max_exact)).astype(np.int32)
    large = np.minimum(large, N_BUCKETS - 1)
    bucket = np.where(n < max_exact, n, large).astype(np.int32)
    valid = (dist >= 0) & (dist < W)
    onehot = (bucket[None] == np.arange(N_BUCKETS)[:, None, None]) & valid[None]
    return onehot.reshape(N_BUCKETS, W * 2 * W).astype(np.float32)


def _bias_expand(rel_bias_t, onehot, name, tn=4096):
    HQ, NB = rel_bias_t.shape
    L = onehot.shape[1]

    def body(r_ref, oh_ref, out_ref):
        out_ref[...] = _dot(r_ref[...], oh_ref[...].astype(F32), NN, precision=lax.Precision.HIGHEST)

    return _pcall(body, name=name, grid=(L // tn,),
                  in_specs=[pl.BlockSpec((HQ, NB), lambda i: (0, 0)), pl.BlockSpec((NB, tn), lambda i: (0, i))],
                  out_specs=pl.BlockSpec((HQ, tn), lambda i: (0, i)),
                  out_shape=jax.ShapeDtypeStruct((HQ, L), F32), compiler_params=_params(("parallel",)))(rel_bias_t, onehot)


def _bias_reduce(dbias, onehot, name, tk=4096):
    HQ, L = dbias.shape
    NB = onehot.shape[0]

    def body(d_ref, oh_ref, out_ref):
        part = _dot(d_ref[...], oh_ref[...].astype(F32), NT, precision=lax.Precision.HIGHEST)
        _accumulate(out_ref, part, pl.program_id(0) == 0)

    return _pcall(body, name=name, grid=(L // tk,),
                  in_specs=[pl.BlockSpec((HQ, tk), lambda i: (0, i)), pl.BlockSpec((NB, tk), lambda i: (0, i))],
                  out_specs=pl.BlockSpec((HQ, NB), lambda i: (0, 0)),
                  out_shape=jax.ShapeDtypeStruct((HQ, NB), F32), compiler_params=_params(("arbitrary",)))(dbias, onehot)


def _stacked_query_index(n_rows_or_cols_axis, shape):
    idx = lax.broadcasted_iota(jnp.int32, shape, n_rows_or_cols_axis)
    return jnp.where(idx >= WINDOW, idx - WINDOW, idx)


def _swa_fwd(qproj, kk, vv, bias_ab, sink_col, gq2, gk2, name):
    S, HQD = qproj.shape
    KVH = kk.shape[0]
    PP = HQD // LANES
    NP = PP // KVH
    W = WINDOW
    nb = S // W

    def body(q_ref, k_ref, v_ref, bias_ref, sink_ref, gq_ref, gk_ref, o_ref, lse_ref, qs_s, kb_s, vb_s):
        is_a = _lane_is_a()
        qn = _pair_norm(q_ref[...], gq_ref[...], is_a) * 0.125
        qs_s[0] = jnp.where(is_a, qn, 0.0).astype(BF16)
        qs_s[1] = jnp.where(is_a, 0.0, qn).astype(BF16)
        kb_s[...] = _pair_norm(k_ref[...], gk_ref[...], is_a).astype(BF16)
        vb_s[...] = v_ref[...].astype(BF16)
        sink = sink_ref[...]
        qi1 = _stacked_query_index(0, (2 * W, W))
        first_valid = lax.broadcasted_iota(jnp.int32, (2 * W, W), 1) <= qi1
        qi2 = _stacked_query_index(0, (2 * W, 2 * W))
        key2 = lax.broadcasted_iota(jnp.int32, (2 * W, 2 * W), 1)
        band_valid = (key2 > qi2) & (key2 <= qi2 + W)
        for n in range(nb):
            rows = slice(n * W, (n + 1) * W)
            keys = slice(0, W) if n == 0 else slice((n - 1) * W, (n + 1) * W)
            lhs = jnp.concatenate([qs_s[0, rows, :], qs_s[1, rows, :]], axis=0)
            s = _dot(lhs, kb_s[keys, :], NT) + (bias_ref[:, W:2 * W] if n == 0 else bias_ref[...])
            s = jnp.where(first_valid if n == 0 else band_valid, s, NEG)
            m = jnp.maximum(jnp.max(s, axis=-1, keepdims=True), sink)
            e = jnp.exp(s - m)
            l = jnp.sum(e, axis=-1, keepdims=True) + jnp.exp(sink - m)
            o_ab = _dot(e.astype(BF16), vb_s[keys, :], NN) / l
            o_ref[rows, :] = jnp.where(is_a, o_ab[0:W, :], o_ab[W:2 * W, :]).astype(BF16)
            lse_ref[n] = _as_row(m + jnp.log(l))

    qcols = pl.BlockSpec((S, LANES), lambda a, g: (0, a * NP + g))
    kvs = pl.BlockSpec((None, S, LANES), lambda a, g: (a, 0, 0))
    gain = pl.BlockSpec((1, LANES), lambda a, g: (0, 0))
    return _pcall(body, name=name, grid=(KVH, NP),
                  in_specs=[qcols, kvs, kvs, pl.BlockSpec((None, 2 * W, 2 * W), lambda a, g: (a * NP + g, 0, 0)),
                            pl.BlockSpec((None, 2 * W, 1), lambda a, g: (a * NP + g, 0, 0)), gain, gain],
                  out_specs=[qcols, pl.BlockSpec((None, nb, 1, 2 * W), lambda a, g: (a * NP + g, 0, 0, 0))],
                  out_shape=[jax.ShapeDtypeStruct((S, HQD), BF16), jax.ShapeDtypeStruct((PP, nb, 1, 2 * W), F32)],
                  scratch_shapes=[pltpu.VMEM((2, S, LANES), BF16), pltpu.VMEM((S, LANES), BF16), pltpu.VMEM((S, LANES), BF16)],
                  compiler_params=_params(("parallel", "parallel")))(qproj, kk, vv, bias_ab, sink_col, gq2, gk2)


def _swa_bwd(qproj, kk, vv, bias_t_ab, sink_row, gq2, gk2, lse_row, do, name):
    S, HQD = qproj.shape
    KVH = kk.shape[0]
    PP = HQD // LANES
    NP = PP // KVH
    W = WINDOW
    nb = S // W

    def body(q_ref, k_ref, v_ref, bias_ref, sink_ref, gq_ref, gk_ref, lr_ref, do_ref,
             dq_ref, dk_ref, dv_ref, db_ref, dsink_ref, dgq_ref, dgk_ref,
             qs_s, kb_s, kt_s, vb_s, dob_s, dq_s, dk_s, dv_s):
        a, g = pl.program_id(0), pl.program_id(1)
        is_a = _lane_is_a()
        qn = _pair_norm(q_ref[...], gq_ref[...], is_a) * 0.125
        qs_s[0] = jnp.where(is_a, qn, 0.0).astype(BF16)
        qs_s[1] = jnp.where(is_a, 0.0, qn).astype(BF16)
        kn = _pair_norm(k_ref[...], gk_ref[...], is_a)
        kb_s[...] = kn.astype(BF16)
        kt_s[...] = kn.T.astype(BF16)
        vb_s[...] = v_ref[...].astype(BF16)
        dov = do_ref[...]
        dob_s[0] = jnp.where(is_a, dov, 0.0).astype(BF16)
        dob_s[1] = jnp.where(is_a, 0.0, dov).astype(BF16)
        sink = sink_ref[...]

        @pl.when(g == 0)
        def _():
            dk_s[...] = jnp.zeros((S, LANES), F32)
            dv_s[...] = jnp.zeros((S, LANES), F32)

        qi1 = _stacked_query_index(1, (W, 2 * W))
        first_valid = lax.broadcasted_iota(jnp.int32, (W, 2 * W), 0) <= qi1
        qi2 = _stacked_query_index(1, (2 * W, 2 * W))
        key2 = lax.broadcasted_iota(jnp.int32, (2 * W, 2 * W), 0)
        band_valid = (key2 > qi2) & (key2 <= qi2 + W)
        head_rows = lax.broadcasted_iota(jnp.int32, (LANES, W), 0) < HEAD_DIM
        db = jnp.zeros((2 * W, 2 * W), F32)
        dsk = jnp.zeros((1, 2 * W), F32)
        pend_k = pend_v = None
        for n in range(nb):
            rows = slice(n * W, (n + 1) * W)
            keys = slice(0, W) if n == 0 else slice((n - 1) * W, (n + 1) * W)
            lhs_q = jnp.concatenate([qs_s[0, rows, :], qs_s[1, rows, :]], axis=0)
            lhs_do = jnp.concatenate([dob_s[0, rows, :], dob_s[1, rows, :]], axis=0)
            lse = lr_ref[n]
            s_t = _dot(kb_s[keys, :], lhs_q, NT) + (bias_ref[W:2 * W, :] if n == 0 else bias_ref[...])
            p_t = jnp.where(first_valid if n == 0 else band_valid, jnp.exp(s_t - lse), 0.0)
            dp_t = _dot(vb_s[keys, :], lhs_do, NT)
            delta = jnp.sum(p_t * dp_t, axis=0, keepdims=True)
            ds_t = p_t * (dp_t - delta)
            dsb = ds_t.astype(BF16)
            dsk = dsk - jnp.exp(sink - lse) * delta
            dv_band = _dot(p_t.astype(BF16), lhs_do, NN)
            dk_band = _dot(dsb, lhs_q, NN)
            dq_t = _dot(kt_s[:, keys], dsb, NN)
            dq_s[rows, :] = jnp.where(head_rows, dq_t[:, 0:W], dq_t[:, W:2 * W]).T
            if n == 0:
                db = jnp.concatenate([jnp.zeros((W, 2 * W), F32), ds_t], axis=0)
                pend_k, pend_v = dk_band, dv_band
            else:
                db = db + ds_t
                prev = slice((n - 1) * W, n * W)
                dk_s[prev, :] += pend_k + dk_band[0:W, :]
                dv_s[prev, :] += pend_v + dv_band[0:W, :]
                pend_k, pend_v = dk_band[W:2 * W, :], dv_band[W:2 * W, :]
        tail = slice((nb - 1) * W, nb * W)
        dk_s[tail, :] += pend_k
        dv_s[tail, :] += pend_v
        db_ref[...] = db
        dsink_ref[0] = jnp.broadcast_to(jnp.sum(dsk[:, 0:W], axis=1, keepdims=True), (1, LANES))
        dsink_ref[1] = jnp.broadcast_to(jnp.sum(dsk[:, W:2 * W], axis=1, keepdims=True), (1, LANES))
        dq_raw, dgq = _pair_norm_bwd(q_ref[...], gq_ref[...], dq_s[...] * 0.125, is_a)
        dq_ref[...] = dq_raw.astype(BF16)
        _accumulate(dgq_ref, dgq, jnp.logical_and(a == 0, g == 0))

        @pl.when(jnp.logical_and(a == KVH - 1, g == NP - 1))
        def _():
            _fold_row(dgq_ref)

        @pl.when(g == NP - 1)
        def _():
            dk_raw, dgk = _pair_norm_bwd(k_ref[...], gk_ref[...], _fold_heads(dk_s[...]), is_a)
            dk_ref[...] = dk_raw
            _accumulate(dgk_ref, dgk, a == 0)
            dv_ref[...] = _fold_heads(dv_s[...])

    qcols = pl.BlockSpec((S, LANES), lambda a, g: (0, a * NP + g))
    kvs = pl.BlockSpec((None, S, LANES), lambda a, g: (a, 0, 0))
    sq = pl.BlockSpec((None, 2 * W, 2 * W), lambda a, g: (a * NP + g, 0, 0))
    gain = pl.BlockSpec((1, LANES), lambda a, g: (0, 0))
    ks = jax.ShapeDtypeStruct((KVH, S, LANES), F32)
    gs = jax.ShapeDtypeStruct((1, LANES), F32)
    return _pcall(body, name=name, grid=(KVH, NP),
                  in_specs=[qcols, kvs, kvs, sq, pl.BlockSpec((None, 1, 2 * W), lambda a, g: (a * NP + g, 0, 0)), gain, gain,
                            pl.BlockSpec((None, nb, 1, 2 * W), lambda a, g: (a * NP + g, 0, 0, 0)), qcols],
                  out_specs=[qcols, kvs, kvs, sq, pl.BlockSpec((2, 1, LANES), lambda a, g: (a * NP + g, 0, 0)), gain, gain],
                  out_shape=[jax.ShapeDtypeStruct((S, HQD), BF16), ks, ks, jax.ShapeDtypeStruct((PP, 2 * W, 2 * W), F32),
                             jax.ShapeDtypeStruct((2 * PP, 1, LANES), F32), gs, gs],
                  scratch_shapes=[pltpu.VMEM((2, S, LANES), BF16), pltpu.VMEM((S, LANES), BF16), pltpu.VMEM((LANES, S), BF16),
                                  pltpu.VMEM((S, LANES), BF16), pltpu.VMEM((2, S, LANES), BF16)] + [pltpu.VMEM((S, LANES), F32)] * 3,
                  compiler_params=_params(("arbitrary", "arbitrary")))(qproj, kk, vv, bias_t_ab, sink_row, gq2, gk2, lse_row, do)


def _adamw(w, g, m, v, name, tr=256):
    R, C = w.shape
    tr = min(tr, R)
    assert R % tr == 0

    def body(w_ref, g_ref, m_ref, v_ref, d_ref, m2_ref, v2_ref):
        gv = g_ref[...]
        m2 = ADAM_B1 * m_ref[...] + (1.0 - ADAM_B1) * gv
        v2 = ADAM_B2 * v_ref[...] + (1.0 - ADAM_B2) * jnp.square(gv)
        m_hat = m2 / (1.0 - ADAM_B1 ** ADAM_STEP)
        v_hat = v2 / (1.0 - ADAM_B2 ** ADAM_STEP)
        d_ref[...] = -ADAM_LR * (m_hat / (jnp.sqrt(v_hat) + ADAM_EPS) + ADAM_WD * w_ref[...])
        m2_ref[...] = m2
        v2_ref[...] = v2

    blk = pl.BlockSpec((tr, C), lambda i: (i, 0))
    return _pcall(body, name=name, grid=(R // tr,), in_specs=[blk] * 4, out_specs=[blk] * 3,
                  out_shape=[jax.ShapeDtypeStruct((R, C), F32)] * 3, compiler_params=_params(("parallel",)))(w, g, m, v)


def _sum_core_pair(arr, got, place, name, tr=512):
    P, hr, C = got.shape
    tr = min(tr, hr)
    assert hr % tr == 0
    nt = hr // tr

    def body(place_ref, a_ref, g_ref, o_ref):
        o_ref[...] = (a_ref[...].astype(F32) + g_ref[...].astype(F32)).astype(o_ref.dtype)

    spec = pltpu.PrefetchScalarGridSpec(
        num_scalar_prefetch=1, grid=(P, nt),
        in_specs=[pl.BlockSpec((None, tr, C), lambda j, i, pr: (j, pr[1] * nt + i, 0)),
                  pl.BlockSpec((None, tr, C), lambda j, i, pr: (j, i, 0))],
        out_specs=pl.BlockSpec((None, tr, C), lambda j, i, pr: (j, i, 0)))
    return _pcall(body, name=name, grid_spec=spec, out_shape=jax.ShapeDtypeStruct(got.shape, BF16),
                  compiler_params=_params(("parallel", "parallel")))(place, arr, got)


def _sum_chips(pair, landed, place, name, tr=256):
    _, R, C = landed.shape
    tr = min(tr, R)
    assert R % tr == 0

    def body(place_ref, p_ref, l_ref, o_ref):
        acc = p_ref[...].astype(F32)
        for k in range(3):
            acc = acc + l_ref[k].astype(F32)
        o_ref[...] = acc

    spec = pltpu.PrefetchScalarGridSpec(
        num_scalar_prefetch=1, grid=(R // tr,),
        in_specs=[pl.BlockSpec((None, tr, C), lambda i, pr: (pr[0], i, 0)), pl.BlockSpec((3, tr, C), lambda i, pr: (0, i, 0))],
        out_specs=pl.BlockSpec((None, tr, C), lambda i, pr: (pr[1], i, 0)))
    return _pcall(body, name=name, grid_spec=spec, out_shape=jax.ShapeDtypeStruct((2, R, C), F32),
                  compiler_params=_params(("parallel",)))(place, pair, landed)


def _sum_parts(parts, name, out_dtype, tr=128):
    P, R, C = parts.shape
    tr = min(tr, R)
    assert R % tr == 0, (R, tr)

    def body(p_ref, o_ref):
        acc = p_ref[0].astype(F32)
        for k in range(1, P):
            acc = acc + p_ref[k].astype(F32)
        o_ref[...] = acc.astype(o_ref.dtype)

    return _pcall(body, name=name, grid=(R // tr,), in_specs=[pl.BlockSpec((P, tr, C), lambda i: (0, i, 0))],
                  out_specs=pl.BlockSpec((tr, C), lambda i: (i, 0)),
                  out_shape=jax.ShapeDtypeStruct((R, C), out_dtype), compiler_params=_params(("parallel",)))(parts)


def _place():
    x, y, c = lax.axis_index("x"), lax.axis_index("y"), lax.axis_index("c")
    others = [(1 - x, y), (x, 1 - y), (1 - x, 1 - y)]
    return x, y, c, others


def _half_rows(ref, hh, lead=()):
    hr = ref.shape[-2] // 2
    return ref.at[(*lead, pl.ds(pl.multiple_of(hh * hr, 16), hr), slice(None))]


def _sem_arrays(*counts):
    return [pltpu.SemaphoreType.DMA((k,)) for k in counts]


SEM_SPEC = pl.BlockSpec(memory_space=pltpu.SEMAPHORE)
ANY_SPEC = pl.BlockSpec(memory_space=pl.ANY)
DATAFLOW = pltpu.SideEffectType.DATAFLOW_SIDE_EFFECTING


def _in_hbm(a):
    return pltpu.with_memory_space_constraint(a, pltpu.HBM)


def _gather_copies(srcs, lands, send_sems, recv_sems):
    x, y, c, others = _place()
    me = 2 * x + y

    def copy(w, k, dst_chip, to):
        return pltpu.make_async_remote_copy(src_ref=_half_rows(srcs[w], c), dst_ref=_half_rows(lands[w], c, (dst_chip,)),
                                            send_sem=send_sems.at[3 * w + k], recv_sem=recv_sems.at[3 * w + k],
                                            device_id=to, device_id_type=MESH)

    pairs = [(w, k, cx, cy) for w in range(len(srcs)) for k, (cx, cy) in enumerate(others)]
    return ([copy(w, k, me, (cx, cy, c)) for w, k, cx, cy in pairs],
            [copy(w, k, 2 * cx + cy, (cx, cy, c)) for w, k, cx, cy in pairs])


def _gather_start(shards, after, name):
    n = len(shards)

    def body(*refs):
        srcs, lands, send_sems, recv_sems, token = refs[:n], refs[n:2 * n], refs[2 * n + 1], refs[2 * n + 2], refs[-1]
        for cp in _gather_copies(srcs, lands, send_sems, recv_sems)[0]:
            cp.start()
        token[...] = jnp.zeros_like(token)

    lands = [lax.empty((N_CHIPS,) + s.shape, s.dtype) for s in shards]
    outs = _pcall(
        body, name=name, in_specs=[HBM_SPEC] * (2 * n) + [ANY_SPEC],
        out_specs=[SEM_SPEC, SEM_SPEC] + [HBM_SPEC] * (2 * n) + [VMEM_SPEC],
        out_shape=[pltpu.SemaphoreType.DMA((3 * n,)), pltpu.SemaphoreType.DMA((3 * n,))]
        + [pltpu.HBM(a.shape, a.dtype) for a in list(shards) + lands] + [jax.ShapeDtypeStruct((8, LANES), F32)],
        input_output_aliases={i: 2 + i for i in range(2 * n)},
        compiler_params=pltpu.CompilerParams(has_side_effects=DATAFLOW),
    )(*[_in_hbm(a) for a in list(shards) + lands], after)
    return outs[0], outs[1], outs[2:2 + n], outs[2 + n:2 + 2 * n], outs[-1]


def _gather_wait(started, after, name):
    send_sems, recv_sems, srcs, lands, _ = started
    n = len(srcs)

    def body(*refs):
        src_refs, land_refs, send_ref, recv_ref = refs[:n], refs[n:2 * n], refs[2 * n], refs[2 * n + 1]
        outgoing, incoming = _gather_copies(src_refs, land_refs, send_ref, recv_ref)
        for out_cp, in_cp in zip(outgoing, incoming):
            out_cp.wait_send()
            in_cp.wait_recv()

    outs = _pcall(
        body, name=name, in_specs=[HBM_SPEC] * (2 * n) + [SEM_SPEC, SEM_SPEC, ANY_SPEC], out_specs=[HBM_SPEC] * (2 * n),
        out_shape=[pltpu.HBM(a.shape, a.dtype) for a in list(srcs) + list(lands)],
        input_output_aliases={i: i for i in range(2 * n)},
        compiler_params=pltpu.CompilerParams(has_side_effects=DATAFLOW),
    )(*srcs, *lands, send_sems, recv_sems, after)
    return outs[:n], outs[n:]


def _gather_pass_on(shards, lands, name):
    n = len(shards)
    per = 4

    def body(*refs):
        srcs, bufs = refs[:n], refs[2 * n:3 * n]
        send_sems, recv_sems = refs[3 * n:]
        x, y, c, others = _place()
        me = 2 * x + y
        sibling = (x, y, 1 - c)

        def copy(w, k, src, dst):
            return pltpu.make_async_remote_copy(src_ref=src, dst_ref=dst, send_sem=send_sems.at[per * w + k],
                                                recv_sem=recv_sems.at[per * w + k], device_id=sibling, device_id_type=MESH)

        sends, recvs = [], []
        for w in range(n):
            for k, (cx, cy) in enumerate(others):
                mine, theirs = _half_rows(bufs[w], c, (2 * cx + cy,)), _half_rows(bufs[w], 1 - c, (2 * cx + cy,))
                sends.append(copy(w, k, mine, mine))
                recvs.append(copy(w, k, theirs, theirs))
            sends.append(copy(w, 3, srcs[w], bufs[w].at[me]))
            recvs.append(sends[-1])
        for cp in sends:
            cp.start()
        for snd, rcv in zip(sends, recvs):
            snd.wait_send()
            rcv.wait_recv()

    return _pcall(body, name=name, in_specs=[HBM_SPEC] * (2 * n), out_specs=[HBM_SPEC] * n,
                  out_shape=[jax.ShapeDtypeStruct(l.shape, l.dtype) for l in lands],
                  input_output_aliases={n + w: w for w in range(n)},
                  scratch_shapes=_sem_arrays(per * n, per * n))(*shards, *lands)


def _scatter_copies(srcs, lands, send_sems, recv_sems):
    x, y, c, others = _place()
    return [pltpu.make_async_remote_copy(src_ref=srcs[w].at[2 * cx + cy], dst_ref=lands[w].at[k],
                                         send_sem=send_sems.at[3 * w + k], recv_sem=recv_sems.at[3 * w + k],
                                         device_id=(cx, cy, c), device_id_type=MESH)
            for w in range(len(srcs)) for k, (cx, cy) in enumerate(others)]


def _scatter_start(parts, name):
    n = len(parts)

    def body(*refs):
        srcs, lands, send_sems, recv_sems, token = refs[:n], refs[n:2 * n], refs[2 * n], refs[2 * n + 1], refs[-1]
        for cp in _scatter_copies(srcs, lands, send_sems, recv_sems):
            cp.start()
        token[...] = jnp.zeros_like(token)

    lands = [lax.empty((3,) + p.shape[1:], p.dtype) for p in parts]
    outs = _pcall(
        body, name=name, in_specs=[HBM_SPEC] * (2 * n), out_specs=[SEM_SPEC, SEM_SPEC] + [HBM_SPEC] * (2 * n) + [VMEM_SPEC],
        out_shape=[pltpu.SemaphoreType.DMA((3 * n,)), pltpu.SemaphoreType.DMA((3 * n,))]
        + [pltpu.HBM(a.shape, a.dtype) for a in list(parts) + lands] + [jax.ShapeDtypeStruct((8, LANES), F32)],
        input_output_aliases={i: 2 + i for i in range(2 * n)},
        compiler_params=pltpu.CompilerParams(has_side_effects=DATAFLOW),
    )(*[_in_hbm(a) for a in list(parts) + lands])
    return (outs[0], outs[1], outs[2:2 + n], outs[2 + n:2 + 2 * n]), outs[-1]


def _scatter_wait(started, after, name):
    send_sems, recv_sems, srcs, lands = started
    n = len(srcs)

    def body(*refs):
        for cp in _scatter_copies(refs[:n], refs[n:2 * n], refs[2 * n], refs[2 * n + 1]):
            cp.wait_send()
            cp.wait_recv()

    outs = _pcall(
        body, name=name, in_specs=[HBM_SPEC] * (2 * n) + [SEM_SPEC, SEM_SPEC, ANY_SPEC], out_specs=[HBM_SPEC] * (2 * n),
        out_shape=[pltpu.HBM(a.shape, a.dtype) for a in list(srcs) + list(lands)],
        input_output_aliases={i: i for i in range(2 * n)},
        compiler_params=pltpu.CompilerParams(has_side_effects=DATAFLOW),
    )(*srcs, *lands, send_sems, recv_sems, after)
    return outs[:n], outs[n:]


def _allgather_group(shards, name):
    n = len(shards)
    per = 7

    def body(*refs):
        ins, outs = refs[:n], refs[n:2 * n]
        send_sems, recv_sems = refs[2 * n:]
        x, y, c, others = _place()
        me = 2 * x + y
        sibling = (x, y, 1 - c)

        def copy(w, k, src, dst, to):
            return pltpu.make_async_remote_copy(src_ref=src, dst_ref=dst, send_sem=send_sems.at[per * w + k],
                                                recv_sem=recv_sems.at[per * w + k], device_id=to, device_id_type=MESH)

        first = [copy(w, k, _half_rows(ins[w], c), _half_rows(outs[w], c, (me,)), (cx, cy, c))
                 for w in range(n) for k, (cx, cy) in enumerate(others)]
        own = [copy(w, 6, ins[w], outs[w].at[me], sibling) for w in range(n)]
        for cp in first + own:
            cp.start()
        passed = []
        for w in range(n):
            for k, (cx, cy) in enumerate(others):
                landed = _half_rows(outs[w], c, (2 * cx + cy,))
                copy(w, k, landed, landed, sibling).wait_recv()
                passed.append(copy(w, 3 + k, landed, landed, sibling))
                passed[-1].start()
        for w in range(n):
            for k, (cx, cy) in enumerate(others):
                theirs = _half_rows(outs[w], 1 - c, (2 * cx + cy,))
                copy(w, 3 + k, theirs, theirs, sibling).wait_recv()
            own[w].wait_recv()
        for cp in first + passed + own:
            cp.wait_send()

    return _pcall(body, name=name, in_specs=[HBM_SPEC] * n, out_specs=[HBM_SPEC] * n,
                  out_shape=[jax.ShapeDtypeStruct((N_CHIPS,) + s.shape, s.dtype) for s in shards],
                  scratch_shapes=_sem_arrays(per * n, per * n))(*shards)


def _swap_halves_group(arrs, name):
    n = len(arrs)

    def body(*refs):
        ins, gots = refs[:n], refs[n:2 * n]
        send_sems, recv_sems = refs[2 * n:]
        x, y, c, _ = _place()
        swaps = [pltpu.make_async_remote_copy(src_ref=_half_rows(ins[w], 1 - c, (slice(None),)), dst_ref=gots[w],
                                              send_sem=send_sems.at[w], recv_sem=recv_sems.at[w],
                                              device_id=(x, y, 1 - c), device_id_type=MESH) for w in range(n)]
        for cp in swaps:
            cp.start()
        for cp in swaps:
            cp.wait()

    half_shapes = [jax.ShapeDtypeStruct((a.shape[0], a.shape[1] // 2, a.shape[2]), a.dtype) for a in arrs]
    return _pcall(body, name=name, in_specs=[HBM_SPEC] * n, out_specs=[HBM_SPEC] * n, out_shape=half_shapes,
                  scratch_shapes=_sem_arrays(n, n))(*arrs)


def _scatter_group(parts, name):
    n = len(parts)

    def body(*refs):
        ins, outs = refs[:n], refs[n:2 * n]
        send_sems, recv_sems = refs[2 * n:]
        x, y, c, others = _place()

        def copy(w, k, src_chip, to):
            return pltpu.make_async_remote_copy(src_ref=ins[w].at[src_chip], dst_ref=outs[w].at[k],
                                                send_sem=send_sems.at[3 * w + k], recv_sem=recv_sems.at[3 * w + k],
                                                device_id=to, device_id_type=MESH)

        sends = [copy(w, k, 2 * cx + cy, (cx, cy, c)) for w in range(n) for k, (cx, cy) in enumerate(others)]
        for cp in sends:
            cp.start()
        for cp in sends:
            cp.wait()

    return _pcall(body, name=name, in_specs=[HBM_SPEC] * n, out_specs=[HBM_SPEC] * n,
                  out_shape=[jax.ShapeDtypeStruct((3,) + p.shape[1:], p.dtype) for p in parts],
                  scratch_shapes=_sem_arrays(3 * n, 3 * n))(*parts)


def _share_halves_group(halves, name):
    n = len(halves)

    def body(*refs):
        bufs = refs[n:2 * n]
        send_sems, recv_sems = refs[2 * n:]
        x, y, c, _ = _place()
        swaps = [pltpu.make_async_remote_copy(src_ref=bufs[w].at[c], dst_ref=bufs[w].at[c], send_sem=send_sems.at[w],
                                              recv_sem=recv_sems.at[w], device_id=(x, y, 1 - c), device_id_type=MESH)
                 for w in range(n)]
        for cp in swaps:
            cp.start()
        for w in range(n):
            swaps[w].wait_send()
            pltpu.make_async_remote_copy(src_ref=bufs[w].at[c], dst_ref=bufs[w].at[1 - c], send_sem=send_sems.at[w],
                                         recv_sem=recv_sems.at[w], device_id=(x, y, 1 - c), device_id_type=MESH).wait_recv()

    return _pcall(body, name=name, in_specs=[HBM_SPEC] * n, out_specs=[HBM_SPEC] * n,
                  out_shape=[jax.ShapeDtypeStruct(h.shape, h.dtype) for h in halves],
                  input_output_aliases={w: w for w in range(n)},
                  scratch_shapes=_sem_arrays(n, n))(*halves)


def _allgather_small(blk, name):
    M, C = blk.shape

    def body(x_ref, out_ref, send_sems, recv_sems, local_sem):
        x, y, c, others = _place()
        me, sibling = (x, y, c), (x, y, 1 - c)

        def rows(px, py, pc):
            return out_ref.at[4 * px + 2 * py + pc]

        def copy(k, block, to, src=None):
            return pltpu.make_async_remote_copy(src_ref=rows(*block) if src is None else src, dst_ref=rows(*block),
                                                send_sem=send_sems.at[k], recv_sem=recv_sems.at[k], device_id=to, device_id_type=MESH)

        mine = pltpu.make_async_copy(x_ref, rows(*me), local_sem)
        mine.start()
        first = [copy(0, me, sibling, src=x_ref)]
        first += [copy(1 + j, me, (*chip, c), src=x_ref) for j, chip in enumerate(others)]
        for cp in first:
            cp.start()
        passed = [copy(4 + j, (*chip, c), sibling) for j, chip in enumerate(others)]
        for j, chip in enumerate(others):
            copy(1 + j, (*chip, c), me).wait_recv()
            passed[j].start()
        copy(0, sibling, me).wait_recv()
        for j, chip in enumerate(others):
            copy(4 + j, (*chip, 1 - c), me).wait_recv()
        for cp in first + passed:
            cp.wait_send()
        mine.wait()

    return _pcall(body, name=name, in_specs=[VMEM_SPEC], out_specs=VMEM_SPEC,
                  out_shape=jax.ShapeDtypeStruct((8, M, C), blk.dtype),
                  scratch_shapes=[pltpu.SemaphoreType.DMA((7,)), pltpu.SemaphoreType.DMA((7,)), pltpu.SemaphoreType.DMA])(blk)


def _pack_rows(n_elems, width=PACK_W, align=PACK_ROW_ALIGN):
    rows = -(-n_elems // width)
    return -(-rows // align) * align


def _pack(arrays, dtype, width=PACK_W, align=PACK_ROW_ALIGN):
    flat = jnp.concatenate([a.astype(dtype).reshape(-1) for a in arrays])
    rows = _pack_rows(flat.shape[0], width, align)
    flat = jnp.pad(flat, (0, rows * width - flat.shape[0]))
    return flat.reshape(rows, width)


def _pack_small(arrays):
    return _pack(arrays, F32, width=128, align=8)


def _unpack(flat, shapes):
    out, off = [], 0
    for shp in shapes:
        n = int(np.prod(shp))
        out.append(flat[..., off:off + n].reshape(flat.shape[:-1] + tuple(shp)))
        off += n
    return out


def _doubled_heads(x2d, n_heads):
    S = x2d.shape[0]
    h = x2d.reshape(S, n_heads, HEAD_DIM).transpose(1, 0, 2)
    return jnp.concatenate([h, h], axis=-1)


def _mlp_fwd(h, g, w_up4, w_down, tag):
    (n,) = _rms_fwd(h, [g], f"rms_mlp{tag}")
    u, a = _matmul(n, w_up4, "nn", f"up{tag}", out_dtypes=(F32, BF16), chipwise="b", tm=2048, tn=512,
                   epilogue=lambda acc: (acc, jnp.square(jnp.maximum(acc, 0.0))))
    h_out = _matmul(a, w_down, "nn", f"down{tag}", extras=(h,), epilogue=lambda acc, res: (res + acc,), tm=1024, tn=1024, tk=1024)
    return h_out, (n, u, a)


def _mlp_bwd(dh_out, dh_out_b, h, g, w_up4, w_down, saved, tag):
    n, u, a = saved
    dw_down = _matmul(a, dh_out_b, "tn", f"dw_down{tag}", out_dtypes=(BF16,), tm=2048, tn=512)
    du = _matmul(dh_out_b, w_down, "nt", f"du{tag}", out_dtypes=(BF16,), extras=(u,), tm=2048, tn=512,
                 epilogue=lambda acc, uu: (acc * (2.0 * jnp.maximum(uu, 0.0)),))
    dw_up = _matmul(n, du, "tn", f"dw_up{tag}", out_dtypes=(BF16,), chipwise="out", tm=1024, tn=512)
    dn = _matmul(du, w_up4, "nt", f"dn_mlp{tag}", tm=1024, tn=1024, tk=w_up4.shape[2], chipwise="b")
    dh, dh_b, (dg,) = _rms_bwd(h, dh_out, [g], [dn], f"rms_mlp_bwd{tag}")
    return dh, dh_b, dg, dw_up, dw_down


def kernel(x, g_attn, g_mlp, w_in_a, b_f, gq_a, gk_a, w_out_a, g_kv, w_kv, gk_b, w_q_b, gq_b, sinks, rel_bias, w_out_b, w_up, w_down, loss_target, m_g_attn, m_g_mlp, m_w_in_a, m_b_f, m_gq_a, m_gk_a, m_w_out_a, m_g_kv, m_w_kv, m_gk_b, m_w_q_b, m_gq_b, m_sinks, m_rel_bias, m_w_out_b, m_w_up, m_w_down, v_g_attn, v_g_mlp, v_w_in_a, v_b_f, v_gq_a, v_gk_a, v_w_out_a, v_g_kv, v_w_kv, v_gk_b, v_w_q_b, v_gq_b, v_sinks, v_rel_bias, v_w_out_b, v_w_up, v_w_down):
    given = dict(locals())
    S, D = x.shape[1], x.shape[2]
    H = D // HEAD_DIM
    KVH = w_kv.shape[1] // (2 * HEAD_DIM)
    kvw = KVH * HEAD_DIM
    hw = H * HEAD_DIM
    W = WINDOW
    nb = S // W
    c_idx = lax.axis_index("c")
    xs, tgt = x[0], loss_target[0]

    shards = {"w_in_a": w_in_a[0], "w_out_a": w_out_a[0], "w_up0": w_up[0], "w_down0": w_down[0], "w_kv": w_kv,
              "w_q_b": w_q_b[0], "w_out_b": w_out_b[0], "w_up1": w_up[1], "w_down1": w_down[1]}
    parts = list(shards)
    lane_pad = lambda a: jnp.pad(a, [(0, 0)] * (a.ndim - 1) + [(0, (-a.shape[-1]) % LANES)])
    n_in_shard = w_in_a.shape[2]
    groups = [("w_in_a", "w_out_a"), ("w_up0", "w_down0"), ("w_kv", "w_q_b", "w_out_b", "w_up1", "w_down1")]
    started = []
    for i, grp in enumerate(groups):
        behind = started[-1][4] if started else g_attn[0]
        started.append(_gather_start([lane_pad(shards[n].astype(BF16)) for n in grp], behind, f"gather_start{i}"))
    gathered = {}

    def finish_gather(i, after):
        srcs, lands = _gather_wait(started[i], after, f"gather_wait{i}")
        gathered.update(zip(groups[i], _gather_pass_on(srcs, lands, f"gather_pass_on{i}")))

    vec = lambda a: a.reshape(1, -1)
    twice = lambda a: jnp.tile(a.reshape(1, -1), (1, 2))

    g_attn0 = vec(g_attn[0]) + sum(st[4][0, 0] for st in started)
    (n0,) = _rms_fwd(xs, [g_attn0], "rms_attn0")
    finish_gather(0, n0)
    win = jnp.moveaxis(gathered["w_in_a"][:, :, :n_in_shard], 0, 1).reshape(D, -1)
    win = jnp.pad(win, ((0, 0), (0, (-win.shape[1]) % 128)))
    wout_a = gathered["w_out_a"].reshape(-1, D)
    n_in = win.shape[1]
    tile_in = 640 if n_in % 640 == 0 else 128
    proj =_matmul(n0, win, "nn", "proj_in", tm=2048, tn=tile_in)
    zt = proj[:, 3 * hw:3 * hw + H].T
    c_row = _gate_fwd(zt, b_f.reshape(H, 1), "gate_fwd")
    c_row3 = c_row.reshape(H, 1, S)
    o_a, lse_a = _fox_fwd(proj, c_row3, twice(gq_a[0]), twice(gk_a[0]), H, "fox_fwd")
    h1 = _matmul(o_a, wout_a, "nn", "out_a", extras=(xs,), epilogue=lambda acc, res: (res + acc,), tm=1024, tn=1024)
    finish_gather(1, h1)
    wup = [gathered["w_up0"], None]
    wdown = [gathered["w_down0"].reshape(-1, D), None]
    h2, mlp0 = _mlp_fwd(h1, vec(g_mlp[0]), wup[0], wdown[0], "0")

    finish_gather(2, h2)
    wq_b, wout_b = gathered["w_q_b"].reshape(-1, D), gathered["w_out_b"].reshape(-1, D)
    wkv = gathered["w_kv"].reshape(D, -1)
    wup[1], wdown[1] = gathered["w_up1"], gathered["w_down1"].reshape(-1, D)
    nkv, n2 = _rms_fwd(h2, [vec(g_kv), vec(g_attn[1])], "rms_attn1")
    kv = _matmul(nkv, wkv, "nn", "proj_kv", tm=2048)
    kk, vv = _doubled_heads(kv[:, :kvw], KVH), _doubled_heads(kv[:, kvw:], KVH)
    q2 = _matmul(n2, wq_b, "nn", "proj_q", tm=1024, tn=1024)
    onehot = jnp.asarray(_bucket_onehot(), dtype=BF16)
    bias = _bias_expand(rel_bias.T, onehot, "bias_expand").reshape(H, W, 2 * W)
    bias_ab = bias.reshape(H // 2, 2 * W, 2 * W)
    bias_t_ab = bias.reshape(H // 2, 2, W, 2 * W).transpose(0, 3, 1, 2).reshape(H // 2, 2 * W, 2 * W)
    sink_ab = jnp.repeat(sinks[0].reshape(H // 2, 2), W, axis=1)
    o_b, lse_b = _swa_fwd(q2, kk, vv, bias_ab, sink_ab.reshape(H // 2, 2 * W, 1), twice(gq_b[0]), twice(gk_b), "swa_fwd")
    h3 = _matmul(o_b, wout_b, "nn", "out_b", extras=(h2,), epilogue=lambda acc, res: (res + acc,), tm=1024, tn=1024)
    h4, mlp1 = _mlp_fwd(h3, vec(g_mlp[1]), wup[1], wdown[1], "1")

    dh4, dh4_b, loss_part = _loss_head(h4, tgt, "loss_head")

    place = jnp.stack([2 * lax.axis_index("x") + lax.axis_index("y"), c_idx]).astype(jnp.int32)
    scattering = []

    def start_reduce(named):
        names = list(named)
        got = _swap_halves_group([named[n] for n in names], "swap_grad_halves_" + names[0])
        pair_sums = [_sum_core_pair(named[n], g, place, "sum_core_pair_" + n) for n, g in zip(names, got)]
        started_scatter, token = _scatter_start(pair_sums, "scatter_start_" + names[0])
        scattering.append((names, started_scatter))
        return token[0:1, :]

    dh3, dh3_b, dg_mlp1, dw_up1, dw_down1 = _mlp_bwd(dh4, dh4_b, h3, vec(g_mlp[1]), wup[1], wdown[1], mlp1, "1")
    tie1 = start_reduce({"w_down1": dw_down1.reshape(N_CHIPS, -1, D), "w_up1": dw_up1})
    dw_out_b = _matmul(o_b, dh3_b, "tn", "dw_out_b", out_dtypes=(BF16,), tm=1024, tn=1024)
    do_b = _matmul(dh3_b, wout_b, "nt", "do_b", tm=1024, tn=1024)
    dq2, dk2, dv2, dbias_t_ab, dsink, dgq_b, dgk_b = _swa_bwd(
        q2, kk, vv, bias_t_ab, sink_ab.reshape(H // 2, 1, 2 * W), twice(gq_b[0]) + tie1, twice(gk_b),
        lse_b, do_b, "swa_bwd")
    dbias = dbias_t_ab.reshape(H // 2, 2 * W, 2, W).transpose(0, 2, 3, 1).reshape(H, W * 2 * W)
    d_rel_bias = _bias_reduce(dbias, onehot, "bias_reduce").T
    dw_q_b = _matmul(n2, dq2, "tn", "dw_q_b", out_dtypes=(BF16,), tm=1024, tn=1024)
    dn2 = _matmul(dq2, wq_b, "nt", "dn2", tm=1024, tn=1024)
    dkv = jnp.concatenate([dk2[h, :, :HEAD_DIM] for h in range(KVH)] + [dv2[h, :, :HEAD_DIM] for h in range(KVH)],
                          axis=1).astype(BF16)
    dw_kv = _matmul(nkv, dkv, "tn", "dw_kv", out_dtypes=(BF16,), tm=1024)
    dnkv = _matmul(dkv, wkv, "nt", "dnkv", tm=1024, tn=1024)
    tie2 = start_reduce({"w_out_b": dw_out_b.reshape(N_CHIPS, -1, D), "w_q_b": dw_q_b.reshape(N_CHIPS, -1, D),
                         "w_kv": dw_kv.reshape(N_CHIPS, -1, 2 * kvw)})
    dh2, dh2_b, (dg_kv, dg_attn1) = _rms_bwd(h2, dh3, [vec(g_kv) + tie2[:, :1], vec(g_attn[1])], [dnkv, dn2], "rms_attn1_bwd")

    dh1, dh1_b, dg_mlp0, dw_up0, dw_down0 = _mlp_bwd(dh2, dh2_b, h1, vec(g_mlp[0]), wup[0], wdown[0], mlp0, "0")
    tie3 = start_reduce({"w_down0": dw_down0.reshape(N_CHIPS, -1, D), "w_up0": dw_up0})
    dw_out_a = _matmul(o_a, dh1_b, "tn", "dw_out_a", out_dtypes=(BF16,), tm=1024, tn=1024)
    do_a = _matmul(dh1_b, wout_a, "nt", "do_a", tm=1024, tn=1024)
    dq_a, dk_a, dv_a, dc_row, dgq_a, dgk_a = _fox_bwd(
        proj, c_row3, twice(gq_a[0]) + tie3, twice(gk_a[0]), lse_a, do_a, H, "fox_bwd")
    dzt, db_f = _gate_bwd(dc_row.reshape(H, S), zt, b_f.reshape(H, 1), "gate_bwd")
    dproj = jnp.concatenate([dq_a, dk_a, dv_a, dzt.T.astype(BF16), jnp.zeros((S, n_in - 3 * hw - H), BF16)], axis=1)
    dw_in = _matmul(n0, dproj, "tn", "dw_in", out_dtypes=(BF16,), tm=1024, tn=tile_in)
    dw_in4 = lane_pad(dw_in[:, :3 * hw + H].reshape(D, N_CHIPS, -1).transpose(1, 0, 2))
    tie4 = start_reduce({"w_out_a": dw_out_a.reshape(N_CHIPS, -1, D), "w_in_a": dw_in4})
    dn0 = _matmul(dproj, win, "nt", "dn0", tm=1024, tn=1024, tk=tile_in)
    grad_x, _, (dg_attn0,) = _rms_bwd(xs, dh1, [vec(g_attn[0]) + tie4[:, :1]], [dn0], "rms_attn0_bwd")

    reduced = {}
    for names, started_scatter in scattering:
        pair_sums, landed = _scatter_wait(started_scatter, grad_x, "scatter_wait_" + names[0])
        halves = [_sum_chips(p, l, place, "sum_chips_" + n) for n, p, l in zip(names, pair_sums, landed)]
        for n, r in zip(names, _share_halves_group(halves, "share_reduced_halves_" + names[0])):
            reduced[n] = r.reshape(-1, r.shape[2])[:, :shards[n].shape[1]]
    big_grads = [reduced["w_in_a"][None], reduced["w_out_a"][None], reduced["w_kv"], reduced["w_q_b"][None],
                 reduced["w_out_b"][None], jnp.stack([reduced["w_up0"], reduced["w_up1"]]),
                 jnp.stack([reduced["w_down0"], reduced["w_down1"]])]

    small_grads = {
        "g_attn": jnp.concatenate([dg_attn0, dg_attn1], axis=0), "g_mlp": jnp.concatenate([dg_mlp0, dg_mlp1], axis=0),
        "b_f": db_f.reshape(1, H), "gq_a": dgq_a[:, :HEAD_DIM], "gk_a": dgk_a[:, :HEAD_DIM], "g_kv": dg_kv.reshape(-1),
        "gk_b": dgk_b[0, :HEAD_DIM], "gq_b": dgq_b[:, :HEAD_DIM], "sinks": dsink[:, 0, 0].reshape(1, H), "rel_bias": d_rel_bias,
    }
    small_shapes = [given[n].shape for n in SMALL] + [(1,)]
    spack = _pack_small([small_grads[n] for n in SMALL] + [loss_part])
    small_sum = _sum_parts(_allgather_small(spack, "allgather_small"), "sum_small", F32, tr=spack.shape[0])
    small_red = _unpack(small_sum.reshape(-1), small_shapes)
    loss = small_red[-1][0]

    grads = dict(zip([n for n, _ in BIG], big_grads))
    grads.update(dict(zip(SMALL, small_red)))
    no_loss = [jnp.zeros((1,), F32)]
    sw = _pack_small([given[n] for n in SMALL] + no_loss)
    sm = _pack_small([given["m_" + n] for n in SMALL] + no_loss)
    sv = _pack_small([given["v_" + n] for n in SMALL] + no_loss)
    sd, sm2, sv2 = _adamw(sw, small_sum, sm, sv, "adamw_small", tr=sw.shape[0])
    delta = dict(zip(SMALL, _unpack(sd.reshape(-1), small_shapes)))
    new_m = dict(zip(SMALL, _unpack(sm2.reshape(-1), small_shapes)))
    new_v = dict(zip(SMALL, _unpack(sv2.reshape(-1), small_shapes)))
    for n, _ in BIG:
        w = given[n]
        two_d = (-1, w.shape[-1])
        d, m2, v2 = _adamw(w.reshape(two_d), grads[n].reshape(two_d), given["m_" + n].reshape(two_d),
                           given["v_" + n].reshape(two_d), "adamw_" + n)
        delta[n], new_m[n], new_v[n] = d.reshape(w.shape), m2.reshape(w.shape), v2.reshape(w.shape)

    order = ["g_attn", "g_mlp", "w_in_a", "b_f", "gq_a", "gk_a", "w_out_a", "g_kv", "w_kv", "gk_b", "w_q_b", "gq_b",
             "sinks", "rel_bias", "w_out_b", "w_up", "w_down"]
    return (loss, grad_x[None], *[grads[n] for n in order], *[delta[n] for n in order],
            *[new_m[n] for n in order], *[new_v[n] for n in order])
```

```python
import numpy as np
import jax
import jax.numpy as jnp
from jax import lax
from jax.experimental import pallas as pl
from jax.experimental.pallas import tpu as pltpu

F32 = jnp.float32
BF16 = jnp.bfloat16
MESH = pl.DeviceIdType.MESH

HEAD_DIM = 64
LANES = 128
WINDOW = 128
N_BUCKETS = 32
REL_MAX_DIST = 128
NORM_EPS = 1e-6
ADAM_LR = 0.001
ADAM_B1 = 0.9
ADAM_B2 = 0.999
ADAM_EPS = 1e-08
ADAM_WD = 0.01
ADAM_STEP = 10
NEG = -1e30
N_CHIPS = 4
PACK_W = 1024
PACK_ROW_ALIGN = 256
VMEM_LIMIT = 56 * 1024 * 1024
HBM_SPEC = pl.BlockSpec(memory_space=pltpu.HBM)
VMEM_SPEC = pl.BlockSpec(memory_space=pltpu.VMEM)

BIG = (("w_in_a", 2), ("w_out_a", 1), ("w_kv", 0), ("w_q_b", 1), ("w_out_b", 1), ("w_up", 2), ("w_down", 1))
SMALL = ("g_attn", "g_mlp", "b_f", "gq_a", "gk_a", "g_kv", "gk_b", "gq_b", "sinks", "rel_bias")


def _pcall(body, **kw):
    return pl.pallas_call(body, **kw)


def _params(sem=None):
    return pltpu.CompilerParams(dimension_semantics=sem, vmem_limit_bytes=VMEM_LIMIT)


def _rinv(x):
    return lax.rsqrt(jnp.mean(x * x, axis=-1, keepdims=True) + NORM_EPS)


def _dot(a, b, dims, precision=None):
    return lax.dot_general(a, b, (dims, ((), ())), precision=precision, preferred_element_type=F32)


NN = ((1,), (0,))
NT = ((1,), (1,))
TN = ((0,), (0,))


def _accumulate(ref, val, first):
    @pl.when(first)
    def _():
        ref[...] = val

    @pl.when(jnp.logical_not(first))
    def _():
        ref[...] += val


def _matmul(a, b, mode, name, out_dtypes=(F32,), extras=(), row_extras=(), epilogue=None, tm=512, tn=512, tk=None, chipwise=None):
    if chipwise == "b":
        nc = b.shape[2]
        M, K = a.shape
        (K2, N) = (b.shape[1], N_CHIPS * nc) if mode == "nn" else (N_CHIPS * nc, b.shape[1])
    elif mode == "nn":
        (M, K), (K2, N) = a.shape, b.shape
    elif mode == "nt":
        (M, K), (N, K2) = a.shape, b.shape
    else:
        (K, M), (K2, N) = a.shape, b.shape
    assert K == K2, (a.shape, b.shape, mode)
    tm, tn = min(tm, M), min(tn, N)
    tk = K if tk is None else tk
    assert M % tm == 0 and N % tn == 0 and K % tk == 0, (M, N, K, tm, tn, tk)
    nk = K // tk
    dims = {"nn": NN, "nt": NT, "tn": TN}[mode]
    a_spec = pl.BlockSpec((tk, tm), lambda i, j, k: (k, i)) if mode == "tn" else pl.BlockSpec((tm, tk), lambda i, j, k: (i, k))
    b_spec = pl.BlockSpec((tn, tk), lambda i, j, k: (j, k)) if mode == "nt" else pl.BlockSpec((tk, tn), lambda i, j, k: (k, j))
    o_spec = pl.BlockSpec((tm, tn), lambda i, j, k: (i, j))
    out_shape = (M, N)
    if chipwise == "b" and mode == "nn":
        per = nc // tn
        assert tk == K and nc % tn == 0
        b_spec = pl.BlockSpec((None, tk, tn), lambda i, j, k: (j // per, 0, j % per))
    elif chipwise == "b":
        assert mode == "nt" and tk == nc
        b_spec = pl.BlockSpec((None, tn, tk), lambda i, j, k: (k, j, 0))
    elif chipwise == "out":
        per = (N // N_CHIPS) // tn
        assert (N // N_CHIPS) % tn == 0
        o_spec = pl.BlockSpec((None, tm, tn), lambda i, j, k: (j // per, i, j % per))
        out_shape = (N_CHIPS, M, N // N_CHIPS)
        assert not extras
    n_ex, n_rex, n_out = len(extras), len(row_extras), len(out_dtypes)
    n_in = 2 + n_ex + n_rex

    def body(*refs):
        a_ref, b_ref = refs[0], refs[1]
        ex_refs = refs[2:n_in]
        out_refs = refs[n_in:n_in + n_out]
        part = _dot(a_ref[...].astype(BF16), b_ref[...].astype(BF16), dims)

        def finish(acc):
            outs = (acc,) if epilogue is None else epilogue(acc, *[r[...] for r in ex_refs])
            for r, o in zip(out_refs, outs):
                r[...] = o.astype(r.dtype)

        if nk == 1:
            finish(part)
            return
        acc_ref = refs[n_in + n_out]
        k = pl.program_id(2)

        @pl.when(k == 0)
        def _():
            acc_ref[...] = part

        @pl.when(jnp.logical_and(k > 0, k < nk - 1))
        def _():
            acc_ref[...] += part

        @pl.when(k == nk - 1)
        def _():
            finish(acc_ref[...] + part)

    row_spec = pl.BlockSpec((1, tn), lambda i, j, k: (0, j))
    outs = _pcall(
        body, name=name, grid=(M // tm, N // tn, nk),
        in_specs=[a_spec, b_spec] + [o_spec] * n_ex + [row_spec] * n_rex,
        out_specs=[o_spec] * n_out,
        out_shape=[jax.ShapeDtypeStruct(out_shape, dt) for dt in out_dtypes],
        scratch_shapes=[pltpu.VMEM((tm, tn), F32)] if nk > 1 else [],
        compiler_params=_params(("parallel", "parallel", "arbitrary")),
    )(a, b, *extras, *row_extras)
    return outs[0] if n_out == 1 else outs


def _rms_fwd(x, gains, name, ts=256):
    S, D = x.shape
    ts = min(ts, S)
    n = len(gains)

    def body(*refs):
        x_ref, g_refs, o_refs = refs[0], refs[1:1 + n], refs[1 + n:]
        xv = x_ref[...]
        xh = xv * _rinv(xv)
        for g_ref, o_ref in zip(g_refs, o_refs):
            o_ref[...] = (xh * g_ref[...]).astype(BF16)

    row = pl.BlockSpec((ts, D), lambda i: (i, 0))
    vec = pl.BlockSpec((1, D), lambda i: (0, 0))
    return _pcall(body, name=name, grid=(S // ts,), in_specs=[row] + [vec] * n, out_specs=[row] * n,
                  out_shape=[jax.ShapeDtypeStruct((S, D), BF16)] * n, compiler_params=_params(("parallel",)))(x, *gains)


def _rms_bwd(x, dres, gains, dns, name, ts=256):
    S, D = x.shape
    ts = min(ts, S)
    n = len(gains)

    def body(*refs):
        x_ref, dres_ref = refs[0], refs[1]
        g_refs, dn_refs = refs[2:2 + n], refs[2 + n:2 + 2 * n]
        dx_ref, dxb_ref, dg_refs = refs[2 + 2 * n], refs[3 + 2 * n], refs[4 + 2 * n:]
        xv = x_ref[...]
        r = _rinv(xv)
        xh = xv * r
        dx = dres_ref[...]
        first = pl.program_id(0) == 0
        for g_ref, dn_ref, dg_ref in zip(g_refs, dn_refs, dg_refs):
            dn = dn_ref[...].astype(F32)
            _accumulate(dg_ref, jnp.sum(dn * xh, axis=0, keepdims=True), first)
            dxh = dn * g_ref[...]
            dx = dx + r * (dxh - xh * jnp.mean(dxh * xh, axis=-1, keepdims=True))
        dx_ref[...] = dx
        dxb_ref[...] = dx.astype(BF16)

    row = pl.BlockSpec((ts, D), lambda i: (i, 0))
    vec = pl.BlockSpec((1, D), lambda i: (0, 0))
    outs = _pcall(body, name=name, grid=(S // ts,), in_specs=[row, row] + [vec] * n + [row] * n,
                  out_specs=[row, row] + [vec] * n,
                  out_shape=[jax.ShapeDtypeStruct((S, D), F32), jax.ShapeDtypeStruct((S, D), BF16)]
                  + [jax.ShapeDtypeStruct((1, D), F32)] * n,
                  compiler_params=_params(("arbitrary",)))(x, dres, *gains, *dns)
    return outs[0], outs[1], outs[2:]


def _loss_head(h, tgt, name, ts=256):
    S, D = h.shape
    ts = min(ts, S)

    def body(h_ref, t_ref, dh_ref, dhb_ref, loss_ref):
        err = h_ref[...] - t_ref[...]
        dh = err * (1.0 / D)
        dh_ref[...] = dh
        dhb_ref[...] = dh.astype(BF16)
        part = 0.5 * jnp.sum(jnp.mean(err * err, axis=-1, keepdims=True), axis=0, keepdims=True)
        _accumulate(loss_ref, part, pl.program_id(0) == 0)

    row = pl.BlockSpec((ts, D), lambda i: (i, 0))
    return _pcall(body, name=name, grid=(S // ts,), in_specs=[row, row],
                  out_specs=[row, row, pl.BlockSpec((1, 1), lambda i: (0, 0))],
                  out_shape=[jax.ShapeDtypeStruct((S, D), F32), jax.ShapeDtypeStruct((S, D), BF16),
                             jax.ShapeDtypeStruct((1, 1), F32)],
                  compiler_params=_params(("arbitrary",)))(h, tgt)


def _gate_fwd(zt, bf, name):
    H, S = zt.shape
    nb = S // 128

    def body(z_ref, b_ref, c_ref):
        z = z_ref[...] + b_ref[...]
        lf = jnp.minimum(z, 0.0) - jnp.log(1.0 + jnp.exp(-jnp.abs(z)))
        upper = (lax.broadcasted_iota(jnp.int32, (128, 128), 0) <= lax.broadcasted_iota(jnp.int32, (128, 128), 1)).astype(F32)
        carry = jnp.zeros((H, 1), F32)
        for blk in range(nb):
            cs = _dot(lf[:, blk * 128:(blk + 1) * 128], upper, NN, precision=lax.Precision.HIGHEST) + carry
            c_ref[:, blk * 128:(blk + 1) * 128] = cs
            carry = cs[:, 127:128]

    return _pcall(body, name=name, in_specs=[VMEM_SPEC, VMEM_SPEC], out_specs=VMEM_SPEC,
                  out_shape=jax.ShapeDtypeStruct((H, S), F32))(zt, bf)


def _gate_bwd(dct, zt, bf, name):
    H, S = zt.shape
    nb = S // 128

    def body(dc_ref, z_ref, b_ref, dz_ref, db_ref):
        z = z_ref[...] + b_ref[...]
        e = jnp.exp(-jnp.abs(z))
        sig_neg = jnp.where(z >= 0, e, 1.0) / (1.0 + e)
        lower = (lax.broadcasted_iota(jnp.int32, (128, 128), 0) >= lax.broadcasted_iota(jnp.int32, (128, 128), 1)).astype(F32)
        dc = dc_ref[...]
        carry = jnp.zeros((H, 1), F32)
        db = jnp.zeros((H, 1), F32)
        for blk in reversed(range(nb)):
            sl = slice(blk * 128, (blk + 1) * 128)
            dlf = _dot(dc[:, sl], lower, NN, precision=lax.Precision.HIGHEST) + carry
            carry = dlf[:, 0:1]
            dz = dlf * sig_neg[:, sl]
            dz_ref[:, sl] = dz
            db = db + jnp.sum(dz, axis=1, keepdims=True)
        db_ref[...] = db

    return _pcall(body, name=name, in_specs=[VMEM_SPEC] * 3, out_specs=[VMEM_SPEC] * 2,
                  out_shape=[jax.ShapeDtypeStruct((H, S), F32), jax.ShapeDtypeStruct((H, 1), F32)])(dct, zt, bf)


def _lane_is_a():
    return lax.broadcasted_iota(jnp.int32, (1, LANES), 1) < HEAD_DIM


def _per_head_mean(x, is_a):
    sa = jnp.sum(jnp.where(is_a, x, 0.0), axis=-1, keepdims=True)
    sb = jnp.sum(jnp.where(is_a, 0.0, x), axis=-1, keepdims=True)
    return jnp.where(is_a, sa, sb) / HEAD_DIM


def _pair_norm(raw, gain, is_a):
    return raw * lax.rsqrt(_per_head_mean(raw * raw, is_a) + NORM_EPS) * gain


def _pair_norm_bwd(raw, gain, dnormed, is_a):
    r = lax.rsqrt(_per_head_mean(raw * raw, is_a) + NORM_EPS)
    xh = raw * r
    dgain = jnp.sum(dnormed * xh, axis=0, keepdims=True)
    dxh = dnormed * gain
    return r * (dxh - xh * _per_head_mean(dxh * xh, is_a)), dgain


def _fold_heads(x):
    i = lax.broadcasted_iota(jnp.int32, (LANES, LANES), 0)
    j = lax.broadcasted_iota(jnp.int32, (LANES, LANES), 1)
    fold = ((i == j) | (i == j + HEAD_DIM) | (i + HEAD_DIM == j)).astype(F32)
    return _dot(x, fold, NN, precision=lax.Precision.HIGHEST)


def _fold_row(ref):
    ref[...] = _fold_heads(jnp.broadcast_to(ref[...], (8, LANES)))[0:1, :]


def _as_col(row):
    return jnp.broadcast_to(row, (LANES, row.shape[1])).T[:, 0:1]


def _as_row(col):
    return jnp.broadcast_to(col, (col.shape[0], LANES)).T[0:1, :]


def _tri_mask(t, keys_on_rows):
    r = lax.broadcasted_iota(jnp.int32, (t, t), 0)
    c = lax.broadcasted_iota(jnp.int32, (t, t), 1)
    return (r <= c) if keys_on_rows else (r >= c)


def _fox_fwd(proj, c_row, gq2, gk2, n_heads, name, t=256):
    S = proj.shape[0]
    H = n_heads
    P = H // 2
    t = min(t, S)
    nq = S // t

    def body(q_ref, k_ref, v_ref, cr_ref, gq_ref, gk_ref, o_ref, lse_ref, qs_s, kb_s, vb_s):
        is_a = _lane_is_a()
        qn = _pair_norm(q_ref[...], gq_ref[...], is_a) * 0.125
        qs_s[0] = jnp.where(is_a, qn, 0.0).astype(BF16)
        qs_s[1] = jnp.where(is_a, 0.0, qn).astype(BF16)
        kb_s[...] = _pair_norm(k_ref[...], gk_ref[...], is_a).astype(BF16)
        vb_s[...] = v_ref[...].astype(BF16)
        causal = _tri_mask(t, False)
        for i in range(nq):
            t0 = i * t
            rows = slice(t0, t0 + t)
            o_pair = None
            for a in range(2):
                qi = qs_s[a, rows, :]
                ci = _as_col(cr_ref[a, :, rows])
                s_d = jnp.where(causal, _dot(qi, kb_s[rows, :], NT) + ci - cr_ref[a, :, rows], NEG)
                m = jnp.max(s_d, axis=-1, keepdims=True)
                if i > 0:
                    s_l = _dot(qi, kb_s[0:t0, :], NT) + ci - cr_ref[a, :, 0:t0]
                    m = jnp.maximum(m, jnp.max(s_l, axis=-1, keepdims=True))
                p_d = jnp.exp(s_d - m)
                l = jnp.sum(p_d, axis=-1, keepdims=True)
                acc = _dot(p_d.astype(BF16), vb_s[rows, :], NN)
                if i > 0:
                    p_l = jnp.exp(s_l - m)
                    l = l + jnp.sum(p_l, axis=-1, keepdims=True)
                    acc = acc + _dot(p_l.astype(BF16), vb_s[0:t0, :], NN)
                o_a = acc / l
                lse_ref[a, :, rows] = _as_row(m + jnp.log(l))
                o_pair = o_a if a == 0 else jnp.where(is_a, o_pair, o_a)
            o_ref[rows, :] = o_pair.astype(BF16)

    def cols(off):
        return pl.BlockSpec((S, LANES), lambda p: (0, off + p))

    rowv = pl.BlockSpec((2, 1, S), lambda p: (p, 0, 0))
    gain = pl.BlockSpec((1, LANES), lambda p: (0, 0))
    return _pcall(body, name=name, grid=(P,), in_specs=[cols(0), cols(P), cols(2 * P), rowv, gain, gain],
                  out_specs=[cols(0), rowv],
                  out_shape=[jax.ShapeDtypeStruct((S, H * HEAD_DIM), BF16), jax.ShapeDtypeStruct((H, 1, S), F32)],
                  scratch_shapes=[pltpu.VMEM((2, S, LANES), BF16), pltpu.VMEM((S, LANES), BF16), pltpu.VMEM((S, LANES), BF16)],
                  compiler_params=_params(("parallel",)))(proj, proj, proj, c_row, gq2, gk2)


def _fox_bwd(proj, c_row, gq2, gk2, lse_row, do, n_heads, name, t=256):
    S = proj.shape[0]
    H = n_heads
    P = H // 2
    t = min(t, S)
    nq = S // t
    assert t % LANES == 0

    def body(q_ref, k_ref, v_ref, cr_ref, gq_ref, gk_ref, lr_ref, do_ref,
             dq_ref, dk_ref, dv_ref, dc_ref, dgq_ref, dgk_ref,
             qs_s, kb_s, kt_s, vb_s, dob_s, dq_s, dk_s, dv_s, dcs_s, cc_s):
        is_a = _lane_is_a()
        for a in range(2):
            for i in range(nq):
                cc_s[a, i * t:(i + 1) * t, :] = _as_col(cr_ref[a, :, i * t:(i + 1) * t])
        qn = _pair_norm(q_ref[...], gq_ref[...], is_a) * 0.125
        qs_s[0] = jnp.where(is_a, qn, 0.0).astype(BF16)
        qs_s[1] = jnp.where(is_a, 0.0, qn).astype(BF16)
        kn = _pair_norm(k_ref[...], gk_ref[...], is_a)
        kb_s[...] = kn.astype(BF16)
        kt_s[0] = jnp.where(is_a, kn, 0.0).T.astype(BF16)
        kt_s[1] = jnp.where(is_a, 0.0, kn).T.astype(BF16)
        vb_s[...] = v_ref[...].astype(BF16)
        dov = do_ref[...]
        dob_s[0] = jnp.where(is_a, dov, 0.0).astype(BF16)
        dob_s[1] = jnp.where(is_a, 0.0, dov).astype(BF16)
        dk_s[...] = jnp.zeros((S, LANES), F32)
        dv_s[...] = jnp.zeros((S, LANES), F32)
        dcs_s[...] = jnp.zeros((2, S, LANES), F32)
        causal = _tri_mask(t, True)
        for i in range(nq):
            t0 = i * t
            rows = slice(t0, t0 + t)
            dq_t = jnp.zeros((LANES, t), F32)
            for a in range(2):
                qi = qs_s[a, rows, :]
                doi = dob_s[a, rows, :]
                cri = cr_ref[a, :, rows]
                lri = lr_ref[a, :, rows]

                def probs(keys, masked, a=a, qi=qi, doi=doi, cri=cri, lri=lri):
                    p_t = jnp.exp(_dot(kb_s[keys, :], qi, NT) + cri - cc_s[a, keys, :] - lri)
                    if masked:
                        p_t = jnp.where(causal, p_t, 0.0)
                    return p_t, _dot(vb_s[keys, :], doi, NT)

                parts = [(rows,) + probs(rows, True)]
                if i > 0:
                    parts.append((slice(0, t0),) + probs(slice(0, t0), False))
                delta = sum(jnp.sum(p_t * dp_t, axis=0, keepdims=True) for _, p_t, dp_t in parts)
                for keys, p_t, dp_t in parts:
                    ds_t = p_t * (dp_t - delta)
                    dsb = ds_t.astype(BF16)
                    dv_s[keys, :] += _dot(p_t.astype(BF16), doi, NN)
                    dk_s[keys, :] += _dot(dsb, qi, NN)
                    dq_t = dq_t + _dot(kt_s[a, :, keys], dsb, NN)
                    dcs_s[a, keys, :] += sum(ds_t[:, b * LANES:(b + 1) * LANES] for b in range(t // LANES))
            dq_s[rows, :] = dq_t.T
        first = pl.program_id(0) == 0
        last = pl.program_id(0) == P - 1
        dq_raw, dgq = _pair_norm_bwd(q_ref[...], gq_ref[...], dq_s[...] * 0.125, is_a)
        dq_ref[...] = dq_raw.astype(BF16)
        _accumulate(dgq_ref, dgq, first)
        dk_raw, dgk = _pair_norm_bwd(k_ref[...], gk_ref[...], dk_s[...], is_a)
        dk_ref[...] = dk_raw.astype(BF16)
        _accumulate(dgk_ref, dgk, first)
        dv_ref[...] = dv_s[...].astype(BF16)
        for a in range(2):
            for i in range(nq):
                rows = slice(i * t, (i + 1) * t)
                dc_ref[a, :, rows] = _as_row(-jnp.sum(dcs_s[a, rows, :], axis=1, keepdims=True))

        @pl.when(last)
        def _():
            _fold_row(dgq_ref)
            _fold_row(dgk_ref)

    def cols(off):
        return pl.BlockSpec((S, LANES), lambda p: (0, off + p))

    rowv = pl.BlockSpec((2, 1, S), lambda p: (p, 0, 0))
    gain = pl.BlockSpec((1, LANES), lambda p: (0, 0))
    wide = jax.ShapeDtypeStruct((S, H * HEAD_DIM), BF16)
    gs = jax.ShapeDtypeStruct((1, LANES), F32)
    return _pcall(body, name=name, grid=(P,),
                  in_specs=[cols(0), cols(P), cols(2 * P), rowv, gain, gain, rowv, cols(0)],
                  out_specs=[cols(0), cols(0), cols(0), rowv, gain, gain],
                  out_shape=[wide, wide, wide, jax.ShapeDtypeStruct((H, 1, S), F32), gs, gs],
                  scratch_shapes=[pltpu.VMEM((2, S, LANES), BF16), pltpu.VMEM((S, LANES), BF16), pltpu.VMEM((2, LANES, S), BF16),
                                  pltpu.VMEM((S, LANES), BF16), pltpu.VMEM((2, S, LANES), BF16)]
                  + [pltpu.VMEM((S, LANES), F32)] * 3 + [pltpu.VMEM((2, S, LANES), F32), pltpu.VMEM((2, S, 1), F32)],
                  compiler_params=_params(("arbitrary",)))(proj, proj, proj, c_row, gq2, gk2, lse_row, do)


def _bucket_onehot():
    W = WINDOW
    dist = np.arange(W)[:, None] + W - np.arange(2 * W)[None, :]
    n = np.maximum(dist, 0)
    max_exact = N_BUCKETS // 2
    large = max_exact + (np.log(np.maximum(n, 1) / max_exact) / np.log(REL_MAX_DIST / max_exact)
                         * (N_BUCKETS - max_exact)).astype(np.int32)
    large = np.minimum(large, N_BUCKETS - 1)
    bucket = np.where(n < max_exact, n, large).astype(np.int32)
    valid = (dist >= 0) & (dist < W)
    onehot = (bucket[None] == np.arange(N_BUCKETS)[:, None, None]) & valid[None]
    return onehot.reshape(N_BUCKETS, W * 2 * W).astype(np.float32)


def _bias_expand(rel_bias_t, onehot, name, tn=4096):
    HQ, NB = rel_bias_t.shape
    L = onehot.shape[1]

    def body(r_ref, oh_ref, out_ref):
        out_ref[...] = _dot(r_ref[...], oh_ref[...].astype(F32), NN, precision=lax.Precision.HIGHEST)

    return _pcall(body, name=name, grid=(L // tn,),
                  in_specs=[pl.BlockSpec((HQ, NB), lambda i: (0, 0)), pl.BlockSpec((NB, tn), lambda i: (0, i))],
                  out_specs=pl.BlockSpec((HQ, tn), lambda i: (0, i)),
                  out_shape=jax.ShapeDtypeStruct((HQ, L), F32), compiler_params=_params(("parallel",)))(rel_bias_t, onehot)


def _bias_reduce(dbias, onehot, name, tk=4096):
    HQ, L = dbias.shape
    NB = onehot.shape[0]

    def body(d_ref, oh_ref, out_ref):
        part = _dot(d_ref[...], oh_ref[...].astype(F32), NT, precision=lax.Precision.HIGHEST)
        _accumulate(out_ref, part, pl.program_id(0) == 0)

    return _pcall(body, name=name, grid=(L // tk,),
                  in_specs=[pl.BlockSpec((HQ, tk), lambda i: (0, i)), pl.BlockSpec((NB, tk), lambda i: (0, i))],
                  out_specs=pl.BlockSpec((HQ, NB), lambda i: (0, 0)),
                  out_shape=jax.ShapeDtypeStruct((HQ, NB), F32), compiler_params=_params(("arbitrary",)))(dbias, onehot)


def _stacked_query_index(n_rows_or_cols_axis, shape):
    idx = lax.broadcasted_iota(jnp.int32, shape, n_rows_or_cols_axis)
    return jnp.where(idx >= WINDOW, idx - WINDOW, idx)


def _swa_fwd(qproj, kk, vv, bias_ab, sink_col, gq2, gk2, name):
    S, HQD = qproj.shape
    KVH = kk.shape[0]
    PP = HQD // LANES
    NP = PP // KVH
    W = WINDOW
    nb = S // W

    def body(q_ref, k_ref, v_ref, bias_ref, sink_ref, gq_ref, gk_ref, o_ref, lse_ref, qs_s, kb_s, vb_s):
        is_a = _lane_is_a()
        qn = _pair_norm(q_ref[...], gq_ref[...], is_a) * 0.125
        qs_s[0] = jnp.where(is_a, qn, 0.0).astype(BF16)
        qs_s[1] = jnp.where(is_a, 0.0, qn).astype(BF16)
        kb_s[...] = _pair_norm(k_ref[...], gk_ref[...], is_a).astype(BF16)
        vb_s[...] = v_ref[...].astype(BF16)
        sink = sink_ref[...]
        qi1 = _stacked_query_index(0, (2 * W, W))
        first_valid = lax.broadcasted_iota(jnp.int32, (2 * W, W), 1) <= qi1
        qi2 = _stacked_query_index(0, (2 * W, 2 * W))
        key2 = lax.broadcasted_iota(jnp.int32, (2 * W, 2 * W), 1)
        band_valid = (key2 > qi2) & (key2 <= qi2 + W)
        for n in range(nb):
            rows = slice(n * W, (n + 1) * W)
            keys = slice(0, W) if n == 0 else slice((n - 1) * W, (n + 1) * W)
            lhs = jnp.concatenate([qs_s[0, rows, :], qs_s[1, rows, :]], axis=0)
            s = _dot(lhs, kb_s[keys, :], NT) + (bias_ref[:, W:2 * W] if n == 0 else bias_ref[...])
            s = jnp.where(first_valid if n == 0 else band_valid, s, NEG)
            m = jnp.maximum(jnp.max(s, axis=-1, keepdims=True), sink)
            e = jnp.exp(s - m)
            l = jnp.sum(e, axis=-1, keepdims=True) + jnp.exp(sink - m)
            o_ab = _dot(e.astype(BF16), vb_s[keys, :], NN) / l
            o_ref[rows, :] = jnp.where(is_a, o_ab[0:W, :], o_ab[W:2 * W, :]).astype(BF16)
            lse_ref[n] = _as_row(m + jnp.log(l))

    qcols = pl.BlockSpec((S, LANES), lambda a, g: (0, a * NP + g))
    kvs = pl.BlockSpec((None, S, LANES), lambda a, g: (a, 0, 0))
    gain = pl.BlockSpec((1, LANES), lambda a, g: (0, 0))
    return _pcall(body, name=name, grid=(KVH, NP),
                  in_specs=[qcols, kvs, kvs, pl.BlockSpec((None, 2 * W, 2 * W), lambda a, g: (a * NP + g, 0, 0)),
                            pl.BlockSpec((None, 2 * W, 1), lambda a, g: (a * NP + g, 0, 0)), gain, gain],
                  out_specs=[qcols, pl.BlockSpec((None, nb, 1, 2 * W), lambda a, g: (a * NP + g, 0, 0, 0))],
                  out_shape=[jax.ShapeDtypeStruct((S, HQD), BF16), jax.ShapeDtypeStruct((PP, nb, 1, 2 * W), F32)],
                  scratch_shapes=[pltpu.VMEM((2, S, LANES), BF16), pltpu.VMEM((S, LANES), BF16), pltpu.VMEM((S, LANES), BF16)],
                  compiler_params=_params(("parallel", "parallel")))(qproj, kk, vv, bias_ab, sink_col, gq2, gk2)


def _swa_bwd(qproj, kk, vv, bias_t_ab, sink_row, gq2, gk2, lse_row, do, name):
    S, HQD = qproj.shape
    KVH = kk.shape[0]
    PP = HQD // LANES
    NP = PP // KVH
    W = WINDOW
    nb = S // W

    def body(q_ref, k_ref, v_ref, bias_ref, sink_ref, gq_ref, gk_ref, lr_ref, do_ref,
             dq_ref, dk_ref, dv_ref, db_ref, dsink_ref, dgq_ref, dgk_ref,
             qs_s, kb_s, kt_s, vb_s, dob_s, dq_s, dk_s, dv_s):
        a, g = pl.program_id(0), pl.program_id(1)
        is_a = _lane_is_a()
        qn = _pair_norm(q_ref[...], gq_ref[...], is_a) * 0.125
        qs_s[0] = jnp.where(is_a, qn, 0.0).astype(BF16)
        qs_s[1] = jnp.where(is_a, 0.0, qn).astype(BF16)
        kn = _pair_norm(k_ref[...], gk_ref[...], is_a)
        kb_s[...] = kn.astype(BF16)
        kt_s[...] = kn.T.astype(BF16)
        vb_s[...] = v_ref[...].astype(BF16)
        dov = do_ref[...]
        dob_s[0] = jnp.where(is_a, dov, 0.0).astype(BF16)
        dob_s[1] = jnp.where(is_a, 0.0, dov).astype(BF16)
        sink = sink_ref[...]

        @pl.when(g == 0)
        def _():
            dk_s[...] = jnp.zeros((S, LANES), F32)
            dv_s[...] = jnp.zeros((S, LANES), F32)

        qi1 = _stacked_query_index(1, (W, 2 * W))
        first_valid = lax.broadcasted_iota(jnp.int32, (W, 2 * W), 0) <= qi1
        qi2 = _stacked_query_index(1, (2 * W, 2 * W))
        key2 = lax.broadcasted_iota(jnp.int32, (2 * W, 2 * W), 0)
        band_valid = (key2 > qi2) & (key2 <= qi2 + W)
        head_rows = lax.broadcasted_iota(jnp.int32, (LANES, W), 0) < HEAD_DIM
        db = jnp.zeros((2 * W, 2 * W), F32)
        dsk = jnp.zeros((1, 2 * W), F32)
        pend_k = pend_v = None
        for n in range(nb):
            rows = slice(n * W, (n + 1) * W)
            keys = slice(0, W) if n == 0 else slice((n - 1) * W, (n + 1) * W)
            lhs_q = jnp.concatenate([qs_s[0, rows, :], qs_s[1, rows, :]], axis=0)
            lhs_do = jnp.concatenate([dob_s[0, rows, :], dob_s[1, rows, :]], axis=0)
            lse = lr_ref[n]
            s_t = _dot(kb_s[keys, :], lhs_q, NT) + (bias_ref[W:2 * W, :] if n == 0 else bias_ref[...])
            p_t = jnp.where(first_valid if n == 0 else band_valid, jnp.exp(s_t - lse), 0.0)
            dp_t = _dot(vb_s[keys, :], lhs_do, NT)
            delta = jnp.sum(p_t * dp_t, axis=0, keepdims=True)
            ds_t = p_t * (dp_t - delta)
            dsb = ds_t.astype(BF16)
            dsk = dsk - jnp.exp(sink - lse) * delta
            dv_band = _dot(p_t.astype(BF16), lhs_do, NN)
            dk_band = _dot(dsb, lhs_q, NN)
            dq_t = _dot(kt_s[:, keys], dsb, NN)
            dq_s[rows, :] = jnp.where(head_rows, dq_t[:, 0:W], dq_t[:, W:2 * W]).T
            if n == 0:
                db = jnp.concatenate([jnp.zeros((W, 2 * W), F32), ds_t], axis=0)
                pend_k, pend_v = dk_band, dv_band
            else:
                db = db + ds_t
                prev = slice((n - 1) * W, n * W)
                dk_s[prev, :] += pend_k + dk_band[0:W, :]
                dv_s[prev, :] += pend_v + dv_band[0:W, :]
                pend_k, pend_v = dk_band[W:2 * W, :], dv_band[W:2 * W, :]
        tail = slice((nb - 1) * W, nb * W)
        dk_s[tail, :] += pend_k
        dv_s[tail, :] += pend_v
        db_ref[...] = db
        dsink_ref[0] = jnp.broadcast_to(jnp.sum(dsk[:, 0:W], axis=1, keepdims=True), (1, LANES))
        dsink_ref[1] = jnp.broadcast_to(jnp.sum(dsk[:, W:2 * W], axis=1, keepdims=True), (1, LANES))
        dq_raw, dgq = _pair_norm_bwd(q_ref[...], gq_ref[...], dq_s[...] * 0.125, is_a)
        dq_ref[...] = dq_raw.astype(BF16)
        _accumulate(dgq_ref, dgq, jnp.logical_and(a == 0, g == 0))

        @pl.when(jnp.logical_and(a == KVH - 1, g == NP - 1))
        def _():
            _fold_row(dgq_ref)

        @pl.when(g == NP - 1)
        def _():
            dk_raw, dgk = _pair_norm_bwd(k_ref[...], gk_ref[...], _fold_heads(dk_s[...]), is_a)
            dk_ref[...] = dk_raw
            _accumulate(dgk_ref, dgk, a == 0)
            dv_ref[...] = _fold_heads(dv_s[...])

    qcols = pl.BlockSpec((S, LANES), lambda a, g: (0, a * NP + g))
    kvs = pl.BlockSpec((None, S, LANES), lambda a, g: (a, 0, 0))
    sq = pl.BlockSpec((None, 2 * W, 2 * W), lambda a, g: (a * NP + g, 0, 0))
    gain = pl.BlockSpec((1, LANES), lambda a, g: (0, 0))
    ks = jax.ShapeDtypeStruct((KVH, S, LANES), F32)
    gs = jax.ShapeDtypeStruct((1, LANES), F32)
    return _pcall(body, name=name, grid=(KVH, NP),
                  in_specs=[qcols, kvs, kvs, sq, pl.BlockSpec((None, 1, 2 * W), lambda a, g: (a * NP + g, 0, 0)), gain, gain,
                            pl.BlockSpec((None, nb, 1, 2 * W), lambda a, g: (a * NP + g, 0, 0, 0)), qcols],
                  out_specs=[qcols, kvs, kvs, sq, pl.BlockSpec((2, 1, LANES), lambda a, g: (a * NP + g, 0, 0)), gain, gain],
                  out_shape=[jax.ShapeDtypeStruct((S, HQD), BF16), ks, ks, jax.ShapeDtypeStruct((PP, 2 * W, 2 * W), F32),
                             jax.ShapeDtypeStruct((2 * PP, 1, LANES), F32), gs, gs],
                  scratch_shapes=[pltpu.VMEM((2, S, LANES), BF16), pltpu.VMEM((S, LANES), BF16), pltpu.VMEM((LANES, S), BF16),
                                  pltpu.VMEM((S, LANES), BF16), pltpu.VMEM((2, S, LANES), BF16)] + [pltpu.VMEM((S, LANES), F32)] * 3,
                  compiler_params=_params(("arbitrary", "arbitrary")))(qproj, kk, vv, bias_t_ab, sink_row, gq2, gk2, lse_row, do)


def _adamw(w, g, m, v, name, tr=256):
    R, C = w.shape
    tr = min(tr, R)
    assert R % tr == 0

    def body(w_ref, g_ref, m_ref, v_ref, d_ref, m2_ref, v2_ref):
        gv = g_ref[...]
        m2 = ADAM_B1 * m_ref[...] + (1.0 - ADAM_B1) * gv
        v2 = ADAM_B2 * v_ref[...] + (1.0 - ADAM_B2) * jnp.square(gv)
        m_hat = m2 / (1.0 - ADAM_B1 ** ADAM_STEP)
        v_hat = v2 / (1.0 - ADAM_B2 ** ADAM_STEP)
        d_ref[...] = -ADAM_LR * (m_hat / (jnp.sqrt(v_hat) + ADAM_EPS) + ADAM_WD * w_ref[...])
        m2_ref[...] = m2
        v2_ref[...] = v2

    blk = pl.BlockSpec((tr, C), lambda i: (i, 0))
    return _pcall(body, name=name, grid=(R // tr,), in_specs=[blk] * 4, out_specs=[blk] * 3,
                  out_shape=[jax.ShapeDtypeStruct((R, C), F32)] * 3, compiler_params=_params(("parallel",)))(w, g, m, v)


def _sum_core_pair(arr, got, place, name, tr=512):
    P, hr, C = got.shape
    tr = min(tr, hr)
    assert hr % tr == 0
    nt = hr // tr

    def body(place_ref, a_ref, g_ref, o_ref):
        o_ref[...] = (a_ref[...].astype(F32) + g_ref[...].astype(F32)).astype(o_ref.dtype)

    spec = pltpu.PrefetchScalarGridSpec(
        num_scalar_prefetch=1, grid=(P, nt),
        in_specs=[pl.BlockSpec((None, tr, C), lambda j, i, pr: (j, pr[1] * nt + i, 0)),
                  pl.BlockSpec((None, tr, C), lambda j, i, pr: (j, i, 0))],
        out_specs=pl.BlockSpec((None, tr, C), lambda j, i, pr: (j, i, 0)))
    return _pcall(body, name=name, grid_spec=spec, out_shape=jax.ShapeDtypeStruct(got.shape, BF16),
                  compiler_params=_params(("parallel", "parallel")))(place, arr, got)


def _sum_chips(pair, landed, place, name, tr=256):
    _, R, C = landed.shape
    tr = min(tr, R)
    assert R % tr == 0

    def body(place_ref, p_ref, l_ref, o_ref):
        acc = p_ref[...].astype(F32)
        for k in range(3):
            acc = acc + l_ref[k].astype(F32)
        o_ref[...] = acc

    spec = pltpu.PrefetchScalarGridSpec(
        num_scalar_prefetch=1, grid=(R // tr,),
        in_specs=[pl.BlockSpec((None, tr, C), lambda i, pr: (pr[0], i, 0)), pl.BlockSpec((3, tr, C), lambda i, pr: (0, i, 0))],
        out_specs=pl.BlockSpec((None, tr, C), lambda i, pr: (pr[1], i, 0)))
    return _pcall(body, name=name, grid_spec=spec, out_shape=jax.ShapeDtypeStruct((2, R, C), F32),
                  compiler_params=_params(("parallel",)))(place, pair, landed)


def _sum_parts(parts, name, out_dtype, tr=128):
    P, R, C = parts.shape
    tr = min(tr, R)
    assert R % tr == 0, (R, tr)

    def body(p_ref, o_ref):
        acc = p_ref[0].astype(F32)
        for k in range(1, P):
            acc = acc + p_ref[k].astype(F32)
        o_ref[...] = acc.astype(o_ref.dtype)

    return _pcall(body, name=name, grid=(R // tr,), in_specs=[pl.BlockSpec((P, tr, C), lambda i: (0, i, 0))],
                  out_specs=pl.BlockSpec((tr, C), lambda i: (i, 0)),
                  out_shape=jax.ShapeDtypeStruct((R, C), out_dtype), compiler_params=_params(("parallel",)))(parts)


def _place():
    x, y, c = lax.axis_index("x"), lax.axis_index("y"), lax.axis_index("c")
    others = [(1 - x, y), (x, 1 - y), (1 - x, 1 - y)]
    return x, y, c, others


def _half_rows(ref, hh, lead=()):
    hr = ref.shape[-2] // 2
    return ref.at[(*lead, pl.ds(pl.multiple_of(hh * hr, 16), hr), slice(None))]


def _sem_arrays(*counts):
    return [pltpu.SemaphoreType.DMA((k,)) for k in counts]


SEM_SPEC = pl.BlockSpec(memory_space=pltpu.SEMAPHORE)
ANY_SPEC = pl.BlockSpec(memory_space=pl.ANY)
DATAFLOW = pltpu.SideEffectType.DATAFLOW_SIDE_EFFECTING


def _in_hbm(a):
    return pltpu.with_memory_space_constraint(a, pltpu.HBM)


def _gather_copies(srcs, lands, send_sems, recv_sems):
    x, y, c, others = _place()
    me = 2 * x + y

    def copy(w, k, dst_chip, to):
        return pltpu.make_async_remote_copy(src_ref=_half_rows(srcs[w], c), dst_ref=_half_rows(lands[w], c, (dst_chip,)),
                                            send_sem=send_sems.at[3 * w + k], recv_sem=recv_sems.at[3 * w + k],
                                            device_id=to, device_id_type=MESH)

    pairs = [(w, k, cx, cy) for w in range(len(srcs)) for k, (cx, cy) in enumerate(others)]
    return ([copy(w, k, me, (cx, cy, c)) for w, k, cx, cy in pairs],
            [copy(w, k, 2 * cx + cy, (cx, cy, c)) for w, k, cx, cy in pairs])


def _gather_start(shards, after, name):
    n = len(shards)

    def body(*refs):
        srcs, lands, send_sems, recv_sems, token = refs[:n], refs[n:2 * n], refs[2 * n + 1], refs[2 * n + 2], refs[-1]
        for cp in _gather_copies(srcs, lands, send_sems, recv_sems)[0]:
            cp.start()
        token[...] = jnp.zeros_like(token)

    lands = [lax.empty((N_CHIPS,) + s.shape, s.dtype) for s in shards]
    outs = _pcall(
        body, name=name, in_specs=[HBM_SPEC] * (2 * n) + [ANY_SPEC],
        out_specs=[SEM_SPEC, SEM_SPEC] + [HBM_SPEC] * (2 * n) + [VMEM_SPEC],
        out_shape=[pltpu.SemaphoreType.DMA((3 * n,)), pltpu.SemaphoreType.DMA((3 * n,))]
        + [pltpu.HBM(a.shape, a.dtype) for a in list(shards) + lands] + [jax.ShapeDtypeStruct((8, LANES), F32)],
        input_output_aliases={i: 2 + i for i in range(2 * n)},
        compiler_params=pltpu.CompilerParams(has_side_effects=DATAFLOW),
    )(*[_in_hbm(a) for a in list(shards) + lands], after)
    return outs[0], outs[1], outs[2:2 + n], outs[2 + n:2 + 2 * n], outs[-1]


def _gather_wait(started, after, name):
    send_sems, recv_sems, srcs, lands, _ = started
    n = len(srcs)

    def body(*refs):
        src_refs, land_refs, send_ref, recv_ref = refs[:n], refs[n:2 * n], refs[2 * n], refs[2 * n + 1]
        outgoing, incoming = _gather_copies(src_refs, land_refs, send_ref, recv_ref)
        for out_cp, in_cp in zip(outgoing, incoming):
            out_cp.wait_send()
            in_cp.wait_recv()

    outs = _pcall(
        body, name=name, in_specs=[HBM_SPEC] * (2 * n) + [SEM_SPEC, SEM_SPEC, ANY_SPEC], out_specs=[HBM_SPEC] * (2 * n),
        out_shape=[pltpu.HBM(a.shape, a.dtype) for a in list(srcs) + list(lands)],
        input_output_aliases={i: i for i in range(2 * n)},
        compiler_params=pltpu.CompilerParams(has_side_effects=DATAFLOW),
    )(*srcs, *lands, send_sems, recv_sems, after)
    return outs[:n], outs[n:]


def _gather_pass_on(shards, lands, name):
    n = len(shards)
    per = 4

    def body(*refs):
        srcs, bufs = refs[:n], refs[2 * n:3 * n]
        send_sems, recv_sems = refs[3 * n:]
        x, y, c, others = _place()
        me = 2 * x + y
        sibling = (x, y, 1 - c)

        def copy(w, k, src, dst):
            return pltpu.make_async_remote_copy(src_ref=src, dst_ref=dst, send_sem=send_sems.at[per * w + k],
                                                recv_sem=recv_sems.at[per * w + k], device_id=sibling, device_id_type=MESH)

        sends, recvs = [], []
        for w in range(n):
            for k, (cx, cy) in enumerate(others):
                mine, theirs = _half_rows(bufs[w], c, (2 * cx + cy,)), _half_rows(bufs[w], 1 - c, (2 * cx + cy,))
                sends.append(copy(w, k, mine, mine))
                recvs.append(copy(w, k, theirs, theirs))
            sends.append(copy(w, 3, srcs[w], bufs[w].at[me]))
            recvs.append(sends[-1])
        for cp in sends:
            cp.start()
        for snd, rcv in zip(sends, recvs):
            snd.wait_send()
            rcv.wait_recv()

    return _pcall(body, name=name, in_specs=[HBM_SPEC] * (2 * n), out_specs=[HBM_SPEC] * n,
                  out_shape=[jax.ShapeDtypeStruct(l.shape, l.dtype) for l in lands],
                  input_output_aliases={n + w: w for w in range(n)},
                  scratch_shapes=_sem_arrays(per * n, per * n))(*shards, *lands)


def _scatter_copies(srcs, lands, send_sems, recv_sems):
    x, y, c, others = _place()
    return [pltpu.make_async_remote_copy(src_ref=srcs[w].at[2 * cx + cy], dst_ref=lands[w].at[k],
                                         send_sem=send_sems.at[3 * w + k], recv_sem=recv_sems.at[3 * w + k],
                                         device_id=(cx, cy, c), device_id_type=MESH)
            for w in range(len(srcs)) for k, (cx, cy) in enumerate(others)]


def _scatter_start(parts, name):
    n = len(parts)

    def body(*refs):
        srcs, lands, send_sems, recv_sems, token = refs[:n], refs[n:2 * n], refs[2 * n], refs[2 * n + 1], refs[-1]
        for cp in _scatter_copies(srcs, lands, send_sems, recv_sems):
            cp.start()
        token[...] = jnp.zeros_like(token)

    lands = [lax.empty((3,) + p.shape[1:], p.dtype) for p in parts]
    outs = _pcall(
        body, name=name, in_specs=[HBM_SPEC] * (2 * n), out_specs=[SEM_SPEC, SEM_SPEC] + [HBM_SPEC] * (2 * n) + [VMEM_SPEC],
        out_shape=[pltpu.SemaphoreType.DMA((3 * n,)), pltpu.SemaphoreType.DMA((3 * n,))]
        + [pltpu.HBM(a.shape, a.dtype) for a in list(parts) + lands] + [jax.ShapeDtypeStruct((8, LANES), F32)],
        input_output_aliases={i: 2 + i for i in range(2 * n)},
        compiler_params=pltpu.CompilerParams(has_side_effects=DATAFLOW),
    )(*[_in_hbm(a) for a in list(parts) + lands])
    return (outs[0], outs[1], outs[2:2 + n], outs[2 + n:2 + 2 * n]), outs[-1]


def _scatter_wait(started, after, name):
    send_sems, recv_sems, srcs, lands = started
    n = len(srcs)

    def body(*refs):
        for cp in _scatter_copies(refs[:n], refs[n:2 * n], refs[2 * n], refs[2 * n + 1]):
            cp.wait_send()
            cp.wait_recv()

    outs = _pcall(
        body, name=name, in_specs=[HBM_SPEC] * (2 * n) + [SEM_SPEC, SEM_SPEC, ANY_SPEC], out_specs=[HBM_SPEC] * (2 * n),
        out_shape=[pltpu.HBM(a.shape, a.dtype) for a in list(srcs) + list(lands)],
        input_output_aliases={i: i for i in range(2 * n)},
        compiler_params=pltpu.CompilerParams(has_side_effects=DATAFLOW),
    )(*srcs, *lands, send_sems, recv_sems, after)
    return outs[:n], outs[n:]


def _allgather_group(shards, name):
    n = len(shards)
    per = 7

    def body(*refs):
        ins, outs = refs[:n], refs[n:2 * n]
        send_sems, recv_sems = refs[2 * n:]
        x, y, c, others = _place()
        me = 2 * x + y
        sibling = (x, y, 1 - c)

        def copy(w, k, src, dst, to):
            return pltpu.make_async_remote_copy(src_ref=src, dst_ref=dst, send_sem=send_sems.at[per * w + k],
                                                recv_sem=recv_sems.at[per * w + k], device_id=to, device_id_type=MESH)

        first = [copy(w, k, _half_rows(ins[w], c), _half_rows(outs[w], c, (me,)), (cx, cy, c))
                 for w in range(n) for k, (cx, cy) in enumerate(others)]
        own = [copy(w, 6, ins[w], outs[w].at[me], sibling) for w in range(n)]
        for cp in first + own:
            cp.start()
        passed = []
        for w in range(n):
            for k, (cx, cy) in enumerate(others):
                landed = _half_rows(outs[w], c, (2 * cx + cy,))
                copy(w, k, landed, landed, sibling).wait_recv()
                passed.append(copy(w, 3 + k, landed, landed, sibling))
                passed[-1].start()
        for w in range(n):
            for k, (cx, cy) in enumerate(others):
                theirs = _half_rows(outs[w], 1 - c, (2 * cx + cy,))
                copy(w, 3 + k, theirs, theirs, sibling).wait_recv()
            own[w].wait_recv()
        for cp in first + passed + own:
            cp.wait_send()

    return _pcall(body, name=name, in_specs=[HBM_SPEC] * n, out_specs=[HBM_SPEC] * n,
                  out_shape=[jax.ShapeDtypeStruct((N_CHIPS,) + s.shape, s.dtype) for s in shards],
                  scratch_shapes=_sem_arrays(per * n, per * n))(*shards)


def _swap_halves_group(arrs, name):
    n = len(arrs)

    def body(*refs):
        ins, gots = refs[:n], refs[n:2 * n]
        send_sems, recv_sems = refs[2 * n:]
        x, y, c, _ = _place()
        swaps = [pltpu.make_async_remote_copy(src_ref=_half_rows(ins[w], 1 - c, (slice(None),)), dst_ref=gots[w],
                                              send_sem=send_sems.at[w], recv_sem=recv_sems.at[w],
                                              device_id=(x, y, 1 - c), device_id_type=MESH) for w in range(n)]
        for cp in swaps:
            cp.start()
        for cp in swaps:
            cp.wait()

    half_shapes = [jax.ShapeDtypeStruct((a.shape[0], a.shape[1] // 2, a.shape[2]), a.dtype) for a in arrs]
    return _pcall(body, name=name, in_specs=[HBM_SPEC] * n, out_specs=[HBM_SPEC] * n, out_shape=half_shapes,
                  scratch_shapes=_sem_arrays(n, n))(*arrs)


def _scatter_group(parts, name):
    n = len(parts)

    def body(*refs):
        ins, outs = refs[:n], refs[n:2 * n]
        send_sems, recv_sems = refs[2 * n:]
        x, y, c, others = _place()

        def copy(w, k, src_chip, to):
            return pltpu.make_async_remote_copy(src_ref=ins[w].at[src_chip], dst_ref=outs[w].at[k],
                                                send_sem=send_sems.at[3 * w + k], recv_sem=recv_sems.at[3 * w + k],
                                                device_id=to, device_id_type=MESH)

        sends = [copy(w, k, 2 * cx + cy, (cx, cy, c)) for w in range(n) for k, (cx, cy) in enumerate(others)]
        for cp in sends:
            cp.start()
        for cp in sends:
            cp.wait()

    return _pcall(body, name=name, in_specs=[HBM_SPEC] * n, out_specs=[HBM_SPEC] * n,
                  out_shape=[jax.ShapeDtypeStruct((3,) + p.shape[1:], p.dtype) for p in parts],
                  scratch_shapes=_sem_arrays(3 * n, 3 * n))(*parts)


def _share_halves_group(halves, name):
    n = len(halves)

    def body(*refs):
        bufs = refs[n:2 * n]
        send_sems, recv_sems = refs[2 * n:]
        x, y, c, _ = _place()
        swaps = [pltpu.make_async_remote_copy(src_ref=bufs[w].at[c], dst_ref=bufs[w].at[c], send_sem=send_sems.at[w],
                                              recv_sem=recv_sems.at[w], device_id=(x, y, 1 - c), device_id_type=MESH)
                 for w in range(n)]
        for cp in swaps:
            cp.start()
        for w in range(n):
            swaps[w].wait_send()
            pltpu.make_async_remote_copy(src_ref=bufs[w].at[c], dst_ref=bufs[w].at[1 - c], send_sem=send_sems.at[w],
                                         recv_sem=recv_sems.at[w], device_id=(x, y, 1 - c), device_id_type=MESH).wait_recv()

    return _pcall(body, name=name, in_specs=[HBM_SPEC] * n, out_specs=[HBM_SPEC] * n,
                  out_shape=[jax.ShapeDtypeStruct(h.shape, h.dtype) for h in halves],
                  input_output_aliases={w: w for w in range(n)},
                  scratch_shapes=_sem_arrays(n, n))(*halves)


def _allgather_small(blk, name):
    M, C = blk.shape

    def body(x_ref, out_ref, send_sems, recv_sems, local_sem):
        x, y, c, others = _place()
        me, sibling = (x, y, c), (x, y, 1 - c)

        def rows(px, py, pc):
            return out_ref.at[4 * px + 2 * py + pc]

        def copy(k, block, to, src=None):
            return pltpu.make_async_remote_copy(src_ref=rows(*block) if src is None else src, dst_ref=rows(*block),
                                                send_sem=send_sems.at[k], recv_sem=recv_sems.at[k], device_id=to, device_id_type=MESH)

        mine = pltpu.make_async_copy(x_ref, rows(*me), local_sem)
        mine.start()
        first = [copy(0, me, sibling, src=x_ref)]
        first += [copy(1 + j, me, (*chip, c), src=x_ref) for j, chip in enumerate(others)]
        for cp in first:
            cp.start()
        passed = [copy(4 + j, (*chip, c), sibling) for j, chip in enumerate(others)]
        for j, chip in enumerate(others):
            copy(1 + j, (*chip, c), me).wait_recv()
            passed[j].start()
        copy(0, sibling, me).wait_recv()
        for j, chip in enumerate(others):
            copy(4 + j, (*chip, 1 - c), me).wait_recv()
        for cp in first + passed:
            cp.wait_send()
        mine.wait()

    return _pcall(body, name=name, in_specs=[VMEM_SPEC], out_specs=VMEM_SPEC,
                  out_shape=jax.ShapeDtypeStruct((8, M, C), blk.dtype),
                  scratch_shapes=[pltpu.SemaphoreType.DMA((7,)), pltpu.SemaphoreType.DMA((7,)), pltpu.SemaphoreType.DMA])(blk)


def _pack_rows(n_elems, width=PACK_W, align=PACK_ROW_ALIGN):
    rows = -(-n_elems // width)
    return -(-rows // align) * align


def _pack(arrays, dtype, width=PACK_W, align=PACK_ROW_ALIGN):
    flat = jnp.concatenate([a.astype(dtype).reshape(-1) for a in arrays])
    rows = _pack_rows(flat.shape[0], width, align)
    flat = jnp.pad(flat, (0, rows * width - flat.shape[0]))
    return flat.reshape(rows, width)


def _pack_small(arrays):
    return _pack(arrays, F32, width=128, align=8)


def _unpack(flat, shapes):
    out, off = [], 0
    for shp in shapes:
        n = int(np.prod(shp))
        out.append(flat[..., off:off + n].reshape(flat.shape[:-1] + tuple(shp)))
        off += n
    return out


def _doubled_heads(x2d, n_heads):
    S = x2d.shape[0]
    h = x2d.reshape(S, n_heads, HEAD_DIM).transpose(1, 0, 2)
    return jnp.concatenate([h, h], axis=-1)


def _residual_then_norms(acc, res, *gains):
    h = res + acc
    hn = h * _rinv(h)
    return (h,) + tuple(hn * g for g in gains)


def _mlp_fwd(h, n, w_up4, w_down, next_gains, tag):
    u, a = _matmul(n, w_up4, "nn", f"up{tag}", out_dtypes=(F32, BF16), chipwise="b", tm=2048, tn=512,
                   epilogue=lambda acc: (acc, jnp.square(jnp.maximum(acc, 0.0))))
    assert w_down.shape[1] == 1024
    outs = _matmul(a, w_down, "nn", f"down{tag}", out_dtypes=(F32,) + (BF16,) * len(next_gains), extras=(h,),
                   row_extras=tuple(next_gains), epilogue=_residual_then_norms, tm=1024, tn=1024, tk=1024)
    outs = outs if next_gains else (outs,)
    return outs[0], outs[1:], (n, u, a)


def _mlp_bwd(dh_out, dh_out_b, h, g, w_up4, w_down, saved, tag):
    n, u, a = saved
    dw_down = _matmul(a, dh_out_b, "tn", f"dw_down{tag}", out_dtypes=(BF16,), tm=2048, tn=512)
    du = _matmul(dh_out_b, w_down, "nt", f"du{tag}", out_dtypes=(BF16,), extras=(u,), tm=2048, tn=512,
                 epilogue=lambda acc, uu: (acc * (2.0 * jnp.maximum(uu, 0.0)),))
    dw_up = _matmul(n, du, "tn", f"dw_up{tag}", out_dtypes=(BF16,), chipwise="out", tm=1024, tn=512)
    dn = _matmul(du, w_up4, "nt", f"dn_mlp{tag}", tm=1024, tn=1024, tk=w_up4.shape[2], chipwise="b")
    dh, dh_b, (dg,) = _rms_bwd(h, dh_out, [g], [dn], f"rms_mlp_bwd{tag}")
    return dh, dh_b, dg, dw_up, dw_down


def kernel(x, g_attn, g_mlp, w_in_a, b_f, gq_a, gk_a, w_out_a, g_kv, w_kv, gk_b, w_q_b, gq_b, sinks, rel_bias, w_out_b, w_up, w_down, loss_target, m_g_attn, m_g_mlp, m_w_in_a, m_b_f, m_gq_a, m_gk_a, m_w_out_a, m_g_kv, m_w_kv, m_gk_b, m_w_q_b, m_gq_b, m_sinks, m_rel_bias, m_w_out_b, m_w_up, m_w_down, v_g_attn, v_g_mlp, v_w_in_a, v_b_f, v_gq_a, v_gk_a, v_w_out_a, v_g_kv, v_w_kv, v_gk_b, v_w_q_b, v_gq_b, v_sinks, v_rel_bias, v_w_out_b, v_w_up, v_w_down):
    given = dict(locals())
    S, D = x.shape[1], x.shape[2]
    H = D // HEAD_DIM
    KVH = w_kv.shape[1] // (2 * HEAD_DIM)
    kvw = KVH * HEAD_DIM
    hw = H * HEAD_DIM
    W = WINDOW
    nb = S // W
    c_idx = lax.axis_index("c")
    xs, tgt = x[0], loss_target[0]

    shards = {"w_in_a": w_in_a[0], "w_out_a": w_out_a[0], "w_up0": w_up[0], "w_down0": w_down[0], "w_kv": w_kv,
              "w_q_b": w_q_b[0], "w_out_b": w_out_b[0], "w_up1": w_up[1], "w_down1": w_down[1]}
    parts = list(shards)
    lane_pad = lambda a: jnp.pad(a, [(0, 0)] * (a.ndim - 1) + [(0, (-a.shape[-1]) % LANES)])
    n_in_shard = w_in_a.shape[2]
    groups = [("w_in_a", "w_out_a"), ("w_up0", "w_down0"), ("w_kv", "w_q_b", "w_out_b", "w_up1", "w_down1")]
    started = []
    for i, grp in enumerate(groups):
        behind = started[-1][4] if started else g_attn[0]
        started.append(_gather_start([lane_pad(shards[n].astype(BF16)) for n in grp], behind, f"gather_start{i}"))
    gathered = {}

    def finish_gather(i, after):
        srcs, lands = _gather_wait(started[i], after, f"gather_wait{i}")
        gathered.update(zip(groups[i], _gather_pass_on(srcs, lands, f"gather_pass_on{i}")))

    vec = lambda a: a.reshape(1, -1)
    twice = lambda a: jnp.tile(a.reshape(1, -1), (1, 2))

    g_attn0 = vec(g_attn[0]) + sum(st[4][0, 0] for st in started)
    (n0,) = _rms_fwd(xs, [g_attn0], "rms_attn0")
    finish_gather(0, n0)
    win = jnp.moveaxis(gathered["w_in_a"][:, :, :n_in_shard], 0, 1).reshape(D, -1)
    win = jnp.pad(win, ((0, 0), (0, (-win.shape[1]) % 128)))
    wout_a = gathered["w_out_a"].reshape(-1, D)
    n_in = win.shape[1]
    tile_in = 640 if n_in % 640 == 0 else 128
    proj =_matmul(n0, win, "nn", "proj_in", tm=2048, tn=tile_in)
    zt = proj[:, 3 * hw:3 * hw + H].T
    c_row = _gate_fwd(zt, b_f.reshape(H, 1), "gate_fwd")
    c_row3 = c_row.reshape(H, 1, S)
    o_a, lse_a = _fox_fwd(proj, c_row3, twice(gq_a[0]), twice(gk_a[0]), H, "fox_fwd")
    h1, n1 = _matmul(o_a, wout_a, "nn", "out_a", out_dtypes=(F32, BF16), extras=(xs,), row_extras=(vec(g_mlp[0]),),
                     epilogue=_residual_then_norms, tm=1024, tn=1024)
    finish_gather(1, h1)
    wup = [gathered["w_up0"], None]
    wdown = [gathered["w_down0"].reshape(-1, D), None]
    h2, (nkv, n2), mlp0 = _mlp_fwd(h1, n1, wup[0], wdown[0], [vec(g_kv), vec(g_attn[1])], "0")

    finish_gather(2, h2)
    wq_b, wout_b = gathered["w_q_b"].reshape(-1, D), gathered["w_out_b"].reshape(-1, D)
    wkv = gathered["w_kv"].reshape(D, -1)
    wup[1], wdown[1] = gathered["w_up1"], gathered["w_down1"].reshape(-1, D)
    kv = _matmul(nkv, wkv, "nn", "proj_kv", tm=2048)
    kk, vv = _doubled_heads(kv[:, :kvw], KVH), _doubled_heads(kv[:, kvw:], KVH)
    q2 = _matmul(n2, wq_b, "nn", "proj_q", tm=1024, tn=1024)
    onehot = jnp.asarray(_bucket_onehot(), dtype=BF16)
    bias = _bias_expand(rel_bias.T, onehot, "bias_expand").reshape(H, W, 2 * W)
    bias_ab = bias.reshape(H // 2, 2 * W, 2 * W)
    bias_t_ab = bias.reshape(H // 2, 2, W, 2 * W).transpose(0, 3, 1, 2).reshape(H // 2, 2 * W, 2 * W)
    sink_ab = jnp.repeat(sinks[0].reshape(H // 2, 2), W, axis=1)
    o_b, lse_b = _swa_fwd(q2, kk, vv, bias_ab, sink_ab.reshape(H // 2, 2 * W, 1), twice(gq_b[0]), twice(gk_b), "swa_fwd")
    h3, n3 = _matmul(o_b, wout_b, "nn", "out_b", out_dtypes=(F32, BF16), extras=(h2,), row_extras=(vec(g_mlp[1]),),
                     epilogue=_residual_then_norms, tm=1024, tn=1024)
    h4, _, mlp1 = _mlp_fwd(h3, n3, wup[1], wdown[1], [], "1")

    dh4, dh4_b, loss_part = _loss_head(h4, tgt, "loss_head")

    place = jnp.stack([2 * lax.axis_index("x") + lax.axis_index("y"), c_idx]).astype(jnp.int32)
    scattering = []

    def start_reduce(named):
        names = list(named)
        got = _swap_halves_group([named[n] for n in names], "swap_grad_halves_" + names[0])
        pair_sums = [_sum_core_pair(named[n], g, place, "sum_core_pair_" + n) for n, g in zip(names, got)]
        started_scatter, token = _scatter_start(pair_sums, "scatter_start_" + names[0])
        scattering.append((names, started_scatter))
        return token[0:1, :]

    dh3, dh3_b, dg_mlp1, dw_up1, dw_down1 = _mlp_bwd(dh4, dh4_b, h3, vec(g_mlp[1]), wup[1], wdown[1], mlp1, "1")
    tie1 = start_reduce({"w_down1": dw_down1.reshape(N_CHIPS, -1, D), "w_up1": dw_up1})
    dw_out_b = _matmul(o_b, dh3_b, "tn", "dw_out_b", out_dtypes=(BF16,), tm=1024, tn=1024)
    do_b = _matmul(dh3_b, wout_b, "nt", "do_b", tm=1024, tn=1024)
    dq2, dk2, dv2, dbias_t_ab, dsink, dgq_b, dgk_b = _swa_bwd(
        q2, kk, vv, bias_t_ab, sink_ab.reshape(H // 2, 1, 2 * W), twice(gq_b[0]) + tie1, twice(gk_b),
        lse_b, do_b, "swa_bwd")
    dbias = dbias_t_ab.reshape(H // 2, 2 * W, 2, W).transpose(0, 2, 3, 1).reshape(H, W * 2 * W)
    d_rel_bias = _bias_reduce(dbias, onehot, "bias_reduce").T
    dw_q_b = _matmul(n2, dq2, "tn", "dw_q_b", out_dtypes=(BF16,), tm=1024, tn=1024)
    dn2 = _matmul(dq2, wq_b, "nt", "dn2", tm=1024, tn=1024)
    dkv = jnp.concatenate([dk2[h, :, :HEAD_DIM] for h in range(KVH)] + [dv2[h, :, :HEAD_DIM] for h in range(KVH)],
                          axis=1).astype(BF16)
    dw_kv = _matmul(nkv, dkv, "tn", "dw_kv", out_dtypes=(BF16,), tm=1024)
    dnkv = _matmul(dkv, wkv, "nt", "dnkv", tm=1024, tn=1024)
    tie2 = start_reduce({"w_out_b": dw_out_b.reshape(N_CHIPS, -1, D), "w_q_b": dw_q_b.reshape(N_CHIPS, -1, D),
                         "w_kv": dw_kv.reshape(N_CHIPS, -1, 2 * kvw)})
    dh2, dh2_b, (dg_kv, dg_attn1) = _rms_bwd(h2, dh3, [vec(g_kv) + tie2[:, :1], vec(g_attn[1])], [dnkv, dn2], "rms_attn1_bwd")

    dh1, dh1_b, dg_mlp0, dw_up0, dw_down0 = _mlp_bwd(dh2, dh2_b, h1, vec(g_mlp[0]), wup[0], wdown[0], mlp0, "0")
    tie3 = start_reduce({"w_down0": dw_down0.reshape(N_CHIPS, -1, D), "w_up0": dw_up0})
    dw_out_a = _matmul(o_a, dh1_b, "tn", "dw_out_a", out_dtypes=(BF16,), tm=1024, tn=1024)
    do_a = _matmul(dh1_b, wout_a, "nt", "do_a", tm=1024, tn=1024)
    dq_a, dk_a, dv_a, dc_row, dgq_a, dgk_a = _fox_bwd(
        proj, c_row3, twice(gq_a[0]) + tie3, twice(gk_a[0]), lse_a, do_a, H, "fox_bwd")
    dzt, db_f = _gate_bwd(dc_row.reshape(H, S), zt, b_f.reshape(H, 1), "gate_bwd")
    dproj = jnp.concatenate([dq_a, dk_a, dv_a, dzt.T.astype(BF16), jnp.zeros((S, n_in - 3 * hw - H), BF16)], axis=1)
    dw_in = _matmul(n0, dproj, "tn", "dw_in", out_dtypes=(BF16,), tm=1024, tn=tile_in)
    dw_in4 = lane_pad(dw_in[:, :3 * hw + H].reshape(D, N_CHIPS, -1).transpose(1, 0, 2))
    tie4 = start_reduce({"w_out_a": dw_out_a.reshape(N_CHIPS, -1, D), "w_in_a": dw_in4})
    dn0 = _matmul(dproj, win, "nt", "dn0", tm=1024, tn=1024, tk=tile_in)
    grad_x, _, (dg_attn0,) = _rms_bwd(xs, dh1, [vec(g_attn[0]) + tie4[:, :1]], [dn0], "rms_attn0_bwd")

    reduced = {}
    for names, started_scatter in scattering:
        pair_sums, landed = _scatter_wait(started_scatter, grad_x, "scatter_wait_" + names[0])
        halves = [_sum_chips(p, l, place, "sum_chips_" + n) for n, p, l in zip(names, pair_sums, landed)]
        for n, r in zip(names, _share_halves_group(halves, "share_reduced_halves_" + names[0])):
            reduced[n] = r.reshape(-1, r.shape[2])[:, :shards[n].shape[1]]
    big_grads = [reduced["w_in_a"][None], reduced["w_out_a"][None], reduced["w_kv"], reduced["w_q_b"][None],
                 reduced["w_out_b"][None], jnp.stack([reduced["w_up0"], reduced["w_up1"]]),
                 jnp.stack([reduced["w_down0"], reduced["w_down1"]])]

    small_grads = {
        "g_attn": jnp.concatenate([dg_attn0, dg_attn1], axis=0), "g_mlp": jnp.concatenate([dg_mlp0, dg_mlp1], axis=0),
        "b_f": db_f.reshape(1, H), "gq_a": dgq_a[:, :HEAD_DIM], "gk_a": dgk_a[:, :HEAD_DIM], "g_kv": dg_kv.reshape(-1),
        "gk_b": dgk_b[0, :HEAD_DIM], "gq_b": dgq_b[:, :HEAD_DIM], "sinks": dsink[:, 0, 0].reshape(1, H), "rel_bias": d_rel_bias,
    }
    small_shapes = [given[n].shape for n in SMALL] + [(1,)]
    spack = _pack_small([small_grads[n] for n in SMALL] + [loss_part])
    small_sum = _sum_parts(_allgather_small(spack, "allgather_small"), "sum_small", F32, tr=spack.shape[0])
    small_red = _unpack(small_sum.reshape(-1), small_shapes)
    loss = small_red[-1][0]

    grads = dict(zip([n for n, _ in BIG], big_grads))
    grads.update(dict(zip(SMALL, small_red)))
    no_loss = [jnp.zeros((1,), F32)]
    sw = _pack_small([given[n] for n in SMALL] + no_loss)
    sm = _pack_small([given["m_" + n] for n in SMALL] + no_loss)
    sv = _pack_small([given["v_" + n] for n in SMALL] + no_loss)
    sd, sm2, sv2 = _adamw(sw, small_sum, sm, sv, "adamw_small", tr=sw.shape[0])
    delta = dict(zip(SMALL, _unpack(sd.reshape(-1), small_shapes)))
    new_m = dict(zip(SMALL, _unpack(sm2.reshape(-1), small_shapes)))
    new_v = dict(zip(SMALL, _unpack(sv2.reshape(-1), small_shapes)))
    for n, _ in BIG:
        w = given[n]
        two_d = (-1, w.shape[-1])
        d, m2, v2 = _adamw(w.reshape(two_d), grads[n].reshape(two_d), given["m_" + n].reshape(two_d),
                           given["v_" + n].reshape(two_d), "adamw_" + n)
        delta[n], new_m[n], new_v[n] = d.reshape(w.shape), m2.reshape(w.shape), v2.reshape(w.shape)

    order = ["g_attn", "g_mlp", "w_in_a", "b_f", "gq_a", "gk_a", "w_out_a", "g_kv", "w_kv", "gk_b", "w_q_b", "gq_b",
             "sinks", "rel_bias", "w_out_b", "w_up", "w_down"]
    return (loss, grad_x[None], *[grads[n] for n in order], *[delta[n] for n in order],
            *[new_m[n] for n in order], *[new_v[n] for n in order])
```

```python
import numpy as np
import jax
import jax.numpy as jnp
from jax import lax
from jax.experimental import pallas as pl
from jax.experimental.pallas import tpu as pltpu

F32 = jnp.float32
BF16 = jnp.bfloat16
MESH = pl.DeviceIdType.MESH

HEAD_DIM = 64
LANES = 128
WINDOW = 128
N_BUCKETS = 32
REL_MAX_DIST = 128
NORM_EPS = 1e-6
ADAM_LR = 0.001
ADAM_B1 = 0.9
ADAM_B2 = 0.999
ADAM_EPS = 1e-08
ADAM_WD = 0.01
ADAM_STEP = 10
NEG = -1e30
N_CHIPS = 4
PACK_W = 1024
PACK_ROW_ALIGN = 256
VMEM_LIMIT = 56 * 1024 * 1024
HBM_SPEC = pl.BlockSpec(memory_space=pltpu.HBM)
VMEM_SPEC = pl.BlockSpec(memory_space=pltpu.VMEM)

BIG = (("w_in_a", 2), ("w_out_a", 1), ("w_kv", 0), ("w_q_b", 1), ("w_out_b", 1), ("w_up", 2), ("w_down", 1))
SMALL = ("g_attn", "g_mlp", "b_f", "gq_a", "gk_a", "g_kv", "gk_b", "gq_b", "sinks", "rel_bias")


def _pcall(body, **kw):
    return pl.pallas_call(body, **kw)


def _params(sem=None):
    return pltpu.CompilerParams(dimension_semantics=sem, vmem_limit_bytes=VMEM_LIMIT)


def _rinv(x):
    return lax.rsqrt(jnp.mean(x * x, axis=-1, keepdims=True) + NORM_EPS)


def _dot(a, b, dims, precision=None):
    return lax.dot_general(a, b, (dims, ((), ())), precision=precision, preferred_element_type=F32)


NN = ((1,), (0,))
NT = ((1,), (1,))
TN = ((0,), (0,))


def _accumulate(ref, val, first):
    @pl.when(first)
    def _():
        ref[...] = val

    @pl.when(jnp.logical_not(first))
    def _():
        ref[...] += val


def _matmul(a, b, mode, name, out_dtypes=(F32,), extras=(), row_extras=(), epilogue=None, tm=512, tn=512, tk=None, chipwise=None):
    if chipwise == "b":
        nc = b.shape[2]
        M, K = a.shape
        (K2, N) = (b.shape[1], N_CHIPS * nc) if mode == "nn" else (N_CHIPS * nc, b.shape[1])
    elif mode == "nn":
        (M, K), (K2, N) = a.shape, b.shape
    elif mode == "nt":
        (M, K), (N, K2) = a.shape, b.shape
    else:
        (K, M), (K2, N) = a.shape, b.shape
    assert K == K2, (a.shape, b.shape, mode)
    tm, tn = min(tm, M), min(tn, N)
    tk = K if tk is None else tk
    assert M % tm == 0 and N % tn == 0 and K % tk == 0, (M, N, K, tm, tn, tk)
    nk = K // tk
    dims = {"nn": NN, "nt": NT, "tn": TN}[mode]
    a_spec = pl.BlockSpec((tk, tm), lambda i, j, k: (k, i)) if mode == "tn" else pl.BlockSpec((tm, tk), lambda i, j, k: (i, k))
    b_spec = pl.BlockSpec((tn, tk), lambda i, j, k: (j, k)) if mode == "nt" else pl.BlockSpec((tk, tn), lambda i, j, k: (k, j))
    o_spec = pl.BlockSpec((tm, tn), lambda i, j, k: (i, j))
    out_shape = (M, N)
    if chipwise == "b" and mode == "nn":
        per = nc // tn
        assert tk == K and nc % tn == 0
        b_spec = pl.BlockSpec((None, tk, tn), lambda i, j, k: (j // per, 0, j % per))
    elif chipwise == "b":
        assert mode == "nt" and tk == nc
        b_spec = pl.BlockSpec((None, tn, tk), lambda i, j, k: (k, j, 0))
    elif chipwise == "out":
        per = (N // N_CHIPS) // tn
        assert (N // N_CHIPS) % tn == 0
        o_spec = pl.BlockSpec((None, tm, tn), lambda i, j, k: (j // per, i, j % per))
        out_shape = (N_CHIPS, M, N // N_CHIPS)
        assert not extras
    n_ex, n_rex, n_out = len(extras), len(row_extras), len(out_dtypes)
    n_in = 2 + n_ex + n_rex

    def body(*refs):
        a_ref, b_ref = refs[0], refs[1]
        ex_refs = refs[2:n_in]
        out_refs = refs[n_in:n_in + n_out]
        part = _dot(a_ref[...].astype(BF16), b_ref[...].astype(BF16), dims)

        def finish(acc):
            outs = (acc,) if epilogue is None else epilogue(acc, *[r[...] for r in ex_refs])
            for r, o in zip(out_refs, outs):
                r[...] = o.astype(r.dtype)

        if nk == 1:
            finish(part)
            return
        acc_ref = refs[n_in + n_out]
        k = pl.program_id(2)

        @pl.when(k == 0)
        def _():
            acc_ref[...] = part

        @pl.when(jnp.logical_and(k > 0, k < nk - 1))
        def _():
            acc_ref[...] += part

        @pl.when(k == nk - 1)
        def _():
            finish(acc_ref[...] + part)

    row_spec = pl.BlockSpec((1, tn), lambda i, j, k: (0, j))
    outs = _pcall(
        body, name=name, grid=(M // tm, N // tn, nk),
        in_specs=[a_spec, b_spec] + [o_spec] * n_ex + [row_spec] * n_rex,
        out_specs=[o_spec] * n_out,
        out_shape=[jax.ShapeDtypeStruct(out_shape, dt) for dt in out_dtypes],
        scratch_shapes=[pltpu.VMEM((tm, tn), F32)] if nk > 1 else [],
        compiler_params=_params(("parallel", "parallel", "arbitrary")),
    )(a, b, *extras, *row_extras)
    return outs[0] if n_out == 1 else outs


def _rms_fwd(x, gains, name, ts=256):
    S, D = x.shape
    ts = min(ts, S)
    n = len(gains)

    def body(*refs):
        x_ref, g_refs, o_refs = refs[0], refs[1:1 + n], refs[1 + n:]
        xv = x_ref[...]
        xh = xv * _rinv(xv)
        for g_ref, o_ref in zip(g_refs, o_refs):
            o_ref[...] = (xh * g_ref[...]).astype(BF16)

    row = pl.BlockSpec((ts, D), lambda i: (i, 0))
    vec = pl.BlockSpec((1, D), lambda i: (0, 0))
    return _pcall(body, name=name, grid=(S // ts,), in_specs=[row] + [vec] * n, out_specs=[row] * n,
                  out_shape=[jax.ShapeDtypeStruct((S, D), BF16)] * n, compiler_params=_params(("parallel",)))(x, *gains)


def _rms_bwd(x, dres, gains, dns, name, ts=256):
    S, D = x.shape
    ts = min(ts, S)
    n = len(gains)

    def body(*refs):
        x_ref, dres_ref = refs[0], refs[1]
        g_refs, dn_refs = refs[2:2 + n], refs[2 + n:2 + 2 * n]
        dx_ref, dxb_ref, dg_refs = refs[2 + 2 * n], refs[3 + 2 * n], refs[4 + 2 * n:]
        xv = x_ref[...]
        r = _rinv(xv)
        xh = xv * r
        dx = dres_ref[...]
        first = pl.program_id(0) == 0
        for g_ref, dn_ref, dg_ref in zip(g_refs, dn_refs, dg_refs):
            dn = dn_ref[...].astype(F32)
            _accumulate(dg_ref, jnp.sum(dn * xh, axis=0, keepdims=True), first)
            dxh = dn * g_ref[...]
            dx = dx + r * (dxh - xh * jnp.mean(dxh * xh, axis=-1, keepdims=True))
        dx_ref[...] = dx
        dxb_ref[...] = dx.astype(BF16)

    row = pl.BlockSpec((ts, D), lambda i: (i, 0))
    vec = pl.BlockSpec((1, D), lambda i: (0, 0))
    outs = _pcall(body, name=name, grid=(S // ts,), in_specs=[row, row] + [vec] * n + [row] * n,
                  out_specs=[row, row] + [vec] * n,
                  out_shape=[jax.ShapeDtypeStruct((S, D), F32), jax.ShapeDtypeStruct((S, D), BF16)]
                  + [jax.ShapeDtypeStruct((1, D), F32)] * n,
                  compiler_params=_params(("arbitrary",)))(x, dres, *gains, *dns)
    return outs[0], outs[1], outs[2:]


def _loss_head(h, tgt, name, ts=256):
    S, D = h.shape
    ts = min(ts, S)

    def body(h_ref, t_ref, dh_ref, dhb_ref, loss_ref):
        err = h_ref[...] - t_ref[...]
        dh = err * (1.0 / D)
        dh_ref[...] = dh
        dhb_ref[...] = dh.astype(BF16)
        part = 0.5 * jnp.sum(jnp.mean(err * err, axis=-1, keepdims=True), axis=0, keepdims=True)
        _accumulate(loss_ref, part, pl.program_id(0) == 0)

    row = pl.BlockSpec((ts, D), lambda i: (i, 0))
    return _pcall(body, name=name, grid=(S // ts,), in_specs=[row, row],
                  out_specs=[row, row, pl.BlockSpec((1, 1), lambda i: (0, 0))],
                  out_shape=[jax.ShapeDtypeStruct((S, D), F32), jax.ShapeDtypeStruct((S, D), BF16),
                             jax.ShapeDtypeStruct((1, 1), F32)],
                  compiler_params=_params(("arbitrary",)))(h, tgt)


def _gate_fwd(zt, bf, name):
    H, S = zt.shape
    nb = S // 128

    def body(z_ref, b_ref, c_ref):
        z = z_ref[...] + b_ref[...]
        lf = jnp.minimum(z, 0.0) - jnp.log(1.0 + jnp.exp(-jnp.abs(z)))
        upper = (lax.broadcasted_iota(jnp.int32, (128, 128), 0) <= lax.broadcasted_iota(jnp.int32, (128, 128), 1)).astype(F32)
        carry = jnp.zeros((H, 1), F32)
        for blk in range(nb):
            cs = _dot(lf[:, blk * 128:(blk + 1) * 128], upper, NN, precision=lax.Precision.HIGHEST) + carry
            c_ref[:, blk * 128:(blk + 1) * 128] = cs
            carry = cs[:, 127:128]

    return _pcall(body, name=name, in_specs=[VMEM_SPEC, VMEM_SPEC], out_specs=VMEM_SPEC,
                  out_shape=jax.ShapeDtypeStruct((H, S), F32))(zt, bf)


def _gate_bwd(dct, zt, bf, name):
    H, S = zt.shape
    nb = S // 128

    def body(dc_ref, z_ref, b_ref, dz_ref, db_ref):
        z = z_ref[...] + b_ref[...]
        e = jnp.exp(-jnp.abs(z))
        sig_neg = jnp.where(z >= 0, e, 1.0) / (1.0 + e)
        lower = (lax.broadcasted_iota(jnp.int32, (128, 128), 0) >= lax.broadcasted_iota(jnp.int32, (128, 128), 1)).astype(F32)
        dc = dc_ref[...]
        carry = jnp.zeros((H, 1), F32)
        db = jnp.zeros((H, 1), F32)
        for blk in reversed(range(nb)):
            sl = slice(blk * 128, (blk + 1) * 128)
            dlf = _dot(dc[:, sl], lower, NN, precision=lax.Precision.HIGHEST) + carry
            carry = dlf[:, 0:1]
            dz = dlf * sig_neg[:, sl]
            dz_ref[:, sl] = dz
            db = db + jnp.sum(dz, axis=1, keepdims=True)
        db_ref[...] = db

    return _pcall(body, name=name, in_specs=[VMEM_SPEC] * 3, out_specs=[VMEM_SPEC] * 2,
                  out_shape=[jax.ShapeDtypeStruct((H, S), F32), jax.ShapeDtypeStruct((H, 1), F32)])(dct, zt, bf)


def _lane_is_a():
    return lax.broadcasted_iota(jnp.int32, (1, LANES), 1) < HEAD_DIM


def _per_head_mean(x, is_a):
    sa = jnp.sum(jnp.where(is_a, x, 0.0), axis=-1, keepdims=True)
    sb = jnp.sum(jnp.where(is_a, 0.0, x), axis=-1, keepdims=True)
    return jnp.where(is_a, sa, sb) / HEAD_DIM


def _pair_norm(raw, gain, is_a):
    return raw * lax.rsqrt(_per_head_mean(raw * raw, is_a) + NORM_EPS) * gain


def _pair_norm_bwd(raw, gain, dnormed, is_a):
    r = lax.rsqrt(_per_head_mean(raw * raw, is_a) + NORM_EPS)
    xh = raw * r
    dgain = jnp.sum(dnormed * xh, axis=0, keepdims=True)
    dxh = dnormed * gain
    return r * (dxh - xh * _per_head_mean(dxh * xh, is_a)), dgain


def _fold_heads(x):
    i = lax.broadcasted_iota(jnp.int32, (LANES, LANES), 0)
    j = lax.broadcasted_iota(jnp.int32, (LANES, LANES), 1)
    fold = ((i == j) | (i == j + HEAD_DIM) | (i + HEAD_DIM == j)).astype(F32)
    return _dot(x, fold, NN, precision=lax.Precision.HIGHEST)


def _fold_row(ref):
    ref[...] = _fold_heads(jnp.broadcast_to(ref[...], (8, LANES)))[0:1, :]


def _as_col(row):
    return jnp.broadcast_to(row, (LANES, row.shape[1])).T[:, 0:1]


def _as_row(col):
    return jnp.broadcast_to(col, (col.shape[0], LANES)).T[0:1, :]


def _tri_mask(t, keys_on_rows):
    r = lax.broadcasted_iota(jnp.int32, (t, t), 0)
    c = lax.broadcasted_iota(jnp.int32, (t, t), 1)
    return (r <= c) if keys_on_rows else (r >= c)


def _fox_fwd(proj, c_row, gq2, gk2, n_heads, name, t=256):
    S = proj.shape[0]
    H = n_heads
    P = H // 2
    t = min(t, S)
    nq = S // t

    def body(q_ref, k_ref, v_ref, cr_ref, gq_ref, gk_ref, o_ref, lse_ref, qs_s, kb_s, vb_s):
        is_a = _lane_is_a()
        qn = _pair_norm(q_ref[...], gq_ref[...], is_a) * 0.125
        qs_s[0] = jnp.where(is_a, qn, 0.0).astype(BF16)
        qs_s[1] = jnp.where(is_a, 0.0, qn).astype(BF16)
        kb_s[...] = _pair_norm(k_ref[...], gk_ref[...], is_a).astype(BF16)
        vb_s[...] = v_ref[...].astype(BF16)
        causal = _tri_mask(t, False)
        for i in range(nq):
            t0 = i * t
            rows = slice(t0, t0 + t)
            o_pair = None
            for a in range(2):
                qi = qs_s[a, rows, :]
                ci = _as_col(cr_ref[a, :, rows])
                s_d = jnp.where(causal, _dot(qi, kb_s[rows, :], NT) + ci - cr_ref[a, :, rows], NEG)
                m = jnp.max(s_d, axis=-1, keepdims=True)
                if i > 0:
                    s_l = _dot(qi, kb_s[0:t0, :], NT) + ci - cr_ref[a, :, 0:t0]
                    m = jnp.maximum(m, jnp.max(s_l, axis=-1, keepdims=True))
                p_d = jnp.exp(s_d - m)
                l = jnp.sum(p_d, axis=-1, keepdims=True)
                acc = _dot(p_d.astype(BF16), vb_s[rows, :], NN)
                if i > 0:
                    p_l = jnp.exp(s_l - m)
                    l = l + jnp.sum(p_l, axis=-1, keepdims=True)
                    acc = acc + _dot(p_l.astype(BF16), vb_s[0:t0, :], NN)
                o_a = acc / l
                lse_ref[a, :, rows] = _as_row(m + jnp.log(l))
                o_pair = o_a if a == 0 else jnp.where(is_a, o_pair, o_a)
            o_ref[rows, :] = o_pair.astype(BF16)

    def cols(off):
        return pl.BlockSpec((S, LANES), lambda p: (0, off + p))

    rowv = pl.BlockSpec((2, 1, S), lambda p: (p, 0, 0))
    gain = pl.BlockSpec((1, LANES), lambda p: (0, 0))
    return _pcall(body, name=name, grid=(P,), in_specs=[cols(0), cols(P), cols(2 * P), rowv, gain, gain],
                  out_specs=[cols(0), rowv],
                  out_shape=[jax.ShapeDtypeStruct((S, H * HEAD_DIM), BF16), jax.ShapeDtypeStruct((H, 1, S), F32)],
                  scratch_shapes=[pltpu.VMEM((2, S, LANES), BF16), pltpu.VMEM((S, LANES), BF16), pltpu.VMEM((S, LANES), BF16)],
                  compiler_params=_params(("parallel",)))(proj, proj, proj, c_row, gq2, gk2)


def _fox_bwd(proj, c_row, gq2, gk2, lse_row, do, n_heads, name, t=256):
    S = proj.shape[0]
    H = n_heads
    P = H // 2
    t = min(t, S)
    nq = S // t
    assert t % LANES == 0

    def body(q_ref, k_ref, v_ref, cr_ref, gq_ref, gk_ref, lr_ref, do_ref,
             dq_ref, dk_ref, dv_ref, dc_ref, dgq_ref, dgk_ref,
             qs_s, kb_s, kt_s, vb_s, dob_s, dq_s, dk_s, dv_s, dcs_s, cc_s):
        is_a = _lane_is_a()
        for a in range(2):
            for i in range(nq):
                cc_s[a, i * t:(i + 1) * t, :] = _as_col(cr_ref[a, :, i * t:(i + 1) * t])
        qn = _pair_norm(q_ref[...], gq_ref[...], is_a) * 0.125
        qs_s[0] = jnp.where(is_a, qn, 0.0).astype(BF16)
        qs_s[1] = jnp.where(is_a, 0.0, qn).astype(BF16)
        kn = _pair_norm(k_ref[...], gk_ref[...], is_a)
        kb_s[...] = kn.astype(BF16)
        kt_s[0] = jnp.where(is_a, kn, 0.0).T.astype(BF16)
        kt_s[1] = jnp.where(is_a, 0.0, kn).T.astype(BF16)
        vb_s[...] = v_ref[...].astype(BF16)
        dov = do_ref[...]
        dob_s[0] = jnp.where(is_a, dov, 0.0).astype(BF16)
        dob_s[1] = jnp.where(is_a, 0.0, dov).astype(BF16)
        dk_s[...] = jnp.zeros((S, LANES), F32)
        dv_s[...] = jnp.zeros((S, LANES), F32)
        dcs_s[...] = jnp.zeros((2, S, LANES), F32)
        causal = _tri_mask(t, True)
        for i in range(nq):
            t0 = i * t
            rows = slice(t0, t0 + t)
            dq_t = jnp.zeros((LANES, t), F32)
            for a in range(2):
                qi = qs_s[a, rows, :]
                doi = dob_s[a, rows, :]
                cri = cr_ref[a, :, rows]
                lri = lr_ref[a, :, rows]

                def probs(keys, masked, a=a, qi=qi, doi=doi, cri=cri, lri=lri):
                    p_t = jnp.exp(_dot(kb_s[keys, :], qi, NT) + cri - cc_s[a, keys, :] - lri)
                    if masked:
                        p_t = jnp.where(causal, p_t, 0.0)
                    return p_t, _dot(vb_s[keys, :], doi, NT)

                parts = [(rows,) + probs(rows, True)]
                if i > 0:
                    parts.append((slice(0, t0),) + probs(slice(0, t0), False))
                delta = sum(jnp.sum(p_t * dp_t, axis=0, keepdims=True) for _, p_t, dp_t in parts)
                for keys, p_t, dp_t in parts:
                    ds_t = p_t * (dp_t - delta)
                    dsb = ds_t.astype(BF16)
                    dv_s[keys, :] += _dot(p_t.astype(BF16), doi, NN)
                    dk_s[keys, :] += _dot(dsb, qi, NN)
                    dq_t = dq_t + _dot(kt_s[a, :, keys], dsb, NN)
                    dcs_s[a, keys, :] += sum(ds_t[:, b * LANES:(b + 1) * LANES] for b in range(t // LANES))
            dq_s[rows, :] = dq_t.T
        first = pl.program_id(0) == 0
        last = pl.program_id(0) == P - 1
        dq_raw, dgq = _pair_norm_bwd(q_ref[...], gq_ref[...], dq_s[...] * 0.125, is_a)
        dq_ref[...] = dq_raw.astype(BF16)
        _accumulate(dgq_ref, dgq, first)
        dk_raw, dgk = _pair_norm_bwd(k_ref[...], gk_ref[...], dk_s[...], is_a)
        dk_ref[...] = dk_raw.astype(BF16)
        _accumulate(dgk_ref, dgk, first)
        dv_ref[...] = dv_s[...].astype(BF16)
        for a in range(2):
            for i in range(nq):
                rows = slice(i * t, (i + 1) * t)
                dc_ref[a, :, rows] = _as_row(-jnp.sum(dcs_s[a, rows, :], axis=1, keepdims=True))

        @pl.when(last)
        def _():
            _fold_row(dgq_ref)
            _fold_row(dgk_ref)

    def cols(off):
        return pl.BlockSpec((S, LANES), lambda p: (0, off + p))

    rowv = pl.BlockSpec((2, 1, S), lambda p: (p, 0, 0))
    gain = pl.BlockSpec((1, LANES), lambda p: (0, 0))
    wide = jax.ShapeDtypeStruct((S, H * HEAD_DIM), BF16)
    gs = jax.ShapeDtypeStruct((1, LANES), F32)
    return _pcall(body, name=name, grid=(P,),
                  in_specs=[cols(0), cols(P), cols(2 * P), rowv, gain, gain, rowv, cols(0)],
                  out_specs=[cols(0), cols(0), cols(0), rowv, gain, gain],
                  out_shape=[wide, wide, wide, jax.ShapeDtypeStruct((H, 1, S), F32), gs, gs],
                  scratch_shapes=[pltpu.VMEM((2, S, LANES), BF16), pltpu.VMEM((S, LANES), BF16), pltpu.VMEM((2, LANES, S), BF16),
                                  pltpu.VMEM((S, LANES), BF16), pltpu.VMEM((2, S, LANES), BF16)]
                  + [pltpu.VMEM((S, LANES), F32)] * 3 + [pltpu.VMEM((2, S, LANES), F32), pltpu.VMEM((2, S, 1), F32)],
                  compiler_params=_params(("arbitrary",)))(proj, proj, proj, c_row, gq2, gk2, lse_row, do)


def _bucket_onehot():
    W = WINDOW
    dist = np.arange(W)[:, None] + W - np.arange(2 * W)[None, :]
    n = np.maximum(dist, 0)
    max_exact = N_BUCKETS // 2
    large = max_exact + (np.log(np.maximum(n, 1) / max_exact) / np.log(REL_MAX_DIST / max_exact)
                         * (N_BUCKETS - max_exact)).astype(np.int32)
    large = np.minimum(large, N_BUCKETS - 1)
    bucket = np.where(n < max_exact, n, large).astype(np.int32)
    valid = (dist >= 0) & (dist < W)
    onehot = (bucket[None] == np.arange(N_BUCKETS)[:, None, None]) & valid[None]
    return onehot.reshape(N_BUCKETS, W * 2 * W).astype(np.float32)


def _bias_expand(rel_bias_t, onehot, name, tn=4096):
    HQ, NB = rel_bias_t.shape
    L = onehot.shape[1]

    def body(r_ref, oh_ref, out_ref):
        out_ref[...] = _dot(r_ref[...], oh_ref[...].astype(F32), NN, precision=lax.Precision.HIGHEST)

    return _pcall(body, name=name, grid=(L // tn,),
                  in_specs=[pl.BlockSpec((HQ, NB), lambda i: (0, 0)), pl.BlockSpec((NB, tn), lambda i: (0, i))],
                  out_specs=pl.BlockSpec((HQ, tn), lambda i: (0, i)),
                  out_shape=jax.ShapeDtypeStruct((HQ, L), F32), compiler_params=_params(("parallel",)))(rel_bias_t, onehot)


def _bias_reduce(dbias, onehot, name, tk=4096):
    HQ, L = dbias.shape
    NB = onehot.shape[0]

    def body(d_ref, oh_ref, out_ref):
        part = _dot(d_ref[...], oh_ref[...].astype(F32), NT, precision=lax.Precision.HIGHEST)
        _accumulate(out_ref, part, pl.program_id(0) == 0)

    return _pcall(body, name=name, grid=(L // tk,),
                  in_specs=[pl.BlockSpec((HQ, tk), lambda i: (0, i)), pl.BlockSpec((NB, tk), lambda i: (0, i))],
                  out_specs=pl.BlockSpec((HQ, NB), lambda i: (0, 0)),
                  out_shape=jax.ShapeDtypeStruct((HQ, NB), F32), compiler_params=_params(("arbitrary",)))(dbias, onehot)


def _stacked_query_index(n_rows_or_cols_axis, shape):
    idx = lax.broadcasted_iota(jnp.int32, shape, n_rows_or_cols_axis)
    return jnp.where(idx >= WINDOW, idx - WINDOW, idx)


def _swa_fwd(qproj, kk, vv, bias_ab, sink_col, gq2, gk2, name):
    S, HQD = qproj.shape
    KVH = kk.shape[0]
    PP = HQD // LANES
    NP = PP // KVH
    W = WINDOW
    nb = S // W

    def body(q_ref, k_ref, v_ref, bias_ref, sink_ref, gq_ref, gk_ref, o_ref, lse_ref, qs_s, kb_s, vb_s):
        is_a = _lane_is_a()
        qn = _pair_norm(q_ref[...], gq_ref[...], is_a) * 0.125
        qs_s[0] = jnp.where(is_a, qn, 0.0).astype(BF16)
        qs_s[1] = jnp.where(is_a, 0.0, qn).astype(BF16)
        kb_s[...] = _pair_norm(k_ref[...], gk_ref[...], is_a).astype(BF16)
        vb_s[...] = v_ref[...].astype(BF16)
        sink = sink_ref[...]
        qi1 = _stacked_query_index(0, (2 * W, W))
        first_valid = lax.broadcasted_iota(jnp.int32, (2 * W, W), 1) <= qi1
        qi2 = _stacked_query_index(0, (2 * W, 2 * W))
        key2 = lax.broadcasted_iota(jnp.int32, (2 * W, 2 * W), 1)
        band_valid = (key2 > qi2) & (key2 <= qi2 + W)
        for n in range(nb):
            rows = slice(n * W, (n + 1) * W)
            keys = slice(0, W) if n == 0 else slice((n - 1) * W, (n + 1) * W)
            lhs = jnp.concatenate([qs_s[0, rows, :], qs_s[1, rows, :]], axis=0)
            s = _dot(lhs, kb_s[keys, :], NT) + (bias_ref[:, W:2 * W] if n == 0 else bias_ref[...])
            s = jnp.where(first_valid if n == 0 else band_valid, s, NEG)
            m = jnp.maximum(jnp.max(s, axis=-1, keepdims=True), sink)
            e = jnp.exp(s - m)
            l = jnp.sum(e, axis=-1, keepdims=True) + jnp.exp(sink - m)
            o_ab = _dot(e.astype(BF16), vb_s[keys, :], NN) / l
            o_ref[rows, :] = jnp.where(is_a, o_ab[0:W, :], o_ab[W:2 * W, :]).astype(BF16)
            lse_ref[n] = _as_row(m + jnp.log(l))

    qcols = pl.BlockSpec((S, LANES), lambda a, g: (0, a * NP + g))
    kvs = pl.BlockSpec((None, S, LANES), lambda a, g: (a, 0, 0))
    gain = pl.BlockSpec((1, LANES), lambda a, g: (0, 0))
    return _pcall(body, name=name, grid=(KVH, NP),
                  in_specs=[qcols, kvs, kvs, pl.BlockSpec((None, 2 * W, 2 * W), lambda a, g: (a * NP + g, 0, 0)),
                            pl.BlockSpec((None, 2 * W, 1), lambda a, g: (a * NP + g, 0, 0)), gain, gain],
                  out_specs=[qcols, pl.BlockSpec((None, nb, 1, 2 * W), lambda a, g: (a * NP + g, 0, 0, 0))],
                  out_shape=[jax.ShapeDtypeStruct((S, HQD), BF16), jax.ShapeDtypeStruct((PP, nb, 1, 2 * W), F32)],
                  scratch_shapes=[pltpu.VMEM((2, S, LANES), BF16), pltpu.VMEM((S, LANES), BF16), pltpu.VMEM((S, LANES), BF16)],
                  compiler_params=_params(("parallel", "parallel")))(qproj, kk, vv, bias_ab, sink_col, gq2, gk2)


def _swa_bwd(qproj, kk, vv, bias_t_ab, sink_row, gq2, gk2, lse_row, do, name):
    S, HQD = qproj.shape
    KVH = kk.shape[0]
    PP = HQD // LANES
    NP = PP // KVH
    W = WINDOW
    nb = S // W

    def body(q_ref, k_ref, v_ref, bias_ref, sink_ref, gq_ref, gk_ref, lr_ref, do_ref,
             dq_ref, dk_ref, dv_ref, db_ref, dsink_ref, dgq_ref, dgk_ref,
             qs_s, kb_s, kt_s, vb_s, dob_s, dq_s, dk_s, dv_s):
        a, g = pl.program_id(0), pl.program_id(1)
        is_a = _lane_is_a()
        qn = _pair_norm(q_ref[...], gq_ref[...], is_a) * 0.125
        qs_s[0] = jnp.where(is_a, qn, 0.0).astype(BF16)
        qs_s[1] = jnp.where(is_a, 0.0, qn).astype(BF16)
        kn = _pair_norm(k_ref[...], gk_ref[...], is_a)
        kb_s[...] = kn.astype(BF16)
        kt_s[...] = kn.T.astype(BF16)
        vb_s[...] = v_ref[...].astype(BF16)
        dov = do_ref[...]
        dob_s[0] = jnp.where(is_a, dov, 0.0).astype(BF16)
        dob_s[1] = jnp.where(is_a, 0.0, dov).astype(BF16)
        sink = sink_ref[...]

        @pl.when(g == 0)
        def _():
            dk_s[...] = jnp.zeros((S, LANES), F32)
            dv_s[...] = jnp.zeros((S, LANES), F32)

        qi1 = _stacked_query_index(1, (W, 2 * W))
        first_valid = lax.broadcasted_iota(jnp.int32, (W, 2 * W), 0) <= qi1
        qi2 = _stacked_query_index(1, (2 * W, 2 * W))
        key2 = lax.broadcasted_iota(jnp.int32, (2 * W, 2 * W), 0)
        band_valid = (key2 > qi2) & (key2 <= qi2 + W)
        head_rows = lax.broadcasted_iota(jnp.int32, (LANES, W), 0) < HEAD_DIM
        db = jnp.zeros((2 * W, 2 * W), F32)
        dsk = jnp.zeros((1, 2 * W), F32)
        pend_k = pend_v = None
        for n in range(nb):
            rows = slice(n * W, (n + 1) * W)
            keys = slice(0, W) if n == 0 else slice((n - 1) * W, (n + 1) * W)
            lhs_q = jnp.concatenate([qs_s[0, rows, :], qs_s[1, rows, :]], axis=0)
            lhs_do = jnp.concatenate([dob_s[0, rows, :], dob_s[1, rows, :]], axis=0)
            lse = lr_ref[n]
            s_t = _dot(kb_s[keys, :], lhs_q, NT) + (bias_ref[W:2 * W, :] if n == 0 else bias_ref[...])
            p_t = jnp.where(first_valid if n == 0 else band_valid, jnp.exp(s_t - lse), 0.0)
            dp_t = _dot(vb_s[keys, :], lhs_do, NT)
            delta = jnp.sum(p_t * dp_t, axis=0, keepdims=True)
            ds_t = p_t * (dp_t - delta)
            dsb = ds_t.astype(BF16)
            dsk = dsk - jnp.exp(sink - lse) * delta
            dv_band = _dot(p_t.astype(BF16), lhs_do, NN)
            dk_band = _dot(dsb, lhs_q, NN)
            dq_t = _dot(kt_s[:, keys], dsb, NN)
            dq_s[rows, :] = jnp.where(head_rows, dq_t[:, 0:W], dq_t[:, W:2 * W]).T
            if n == 0:
                db = jnp.concatenate([jnp.zeros((W, 2 * W), F32), ds_t], axis=0)
                pend_k, pend_v = dk_band, dv_band
            else:
                db = db + ds_t
                prev = slice((n - 1) * W, n * W)
                dk_s[prev, :] += pend_k + dk_band[0:W, :]
                dv_s[prev, :] += pend_v + dv_band[0:W, :]
                pend_k, pend_v = dk_band[W:2 * W, :], dv_band[W:2 * W, :]
        tail = slice((nb - 1) * W, nb * W)
        dk_s[tail, :] += pend_k
        dv_s[tail, :] += pend_v
        db_ref[...] = db
        dsink_ref[0] = jnp.broadcast_to(jnp.sum(dsk[:, 0:W], axis=1, keepdims=True), (1, LANES))
        dsink_ref[1] = jnp.broadcast_to(jnp.sum(dsk[:, W:2 * W], axis=1, keepdims=True), (1, LANES))
        dq_raw, dgq = _pair_norm_bwd(q_ref[...], gq_ref[...], dq_s[...] * 0.125, is_a)
        dq_ref[...] = dq_raw.astype(BF16)
        _accumulate(dgq_ref, dgq, jnp.logical_and(a == 0, g == 0))

        @pl.when(jnp.logical_and(a == KVH - 1, g == NP - 1))
        def _():
            _fold_row(dgq_ref)

        @pl.when(g == NP - 1)
        def _():
            dk_raw, dgk = _pair_norm_bwd(k_ref[...], gk_ref[...], _fold_heads(dk_s[...]), is_a)
            dk_ref[...] = dk_raw
            _accumulate(dgk_ref, dgk, a == 0)
            dv_ref[...] = _fold_heads(dv_s[...])

    qcols = pl.BlockSpec((S, LANES), lambda a, g: (0, a * NP + g))
    kvs = pl.BlockSpec((None, S, LANES), lambda a, g: (a, 0, 0))
    sq = pl.BlockSpec((None, 2 * W, 2 * W), lambda a, g: (a * NP + g, 0, 0))
    gain = pl.BlockSpec((1, LANES), lambda a, g: (0, 0))
    ks = jax.ShapeDtypeStruct((KVH, S, LANES), F32)
    gs = jax.ShapeDtypeStruct((1, LANES), F32)
    return _pcall(body, name=name, grid=(KVH, NP),
                  in_specs=[qcols, kvs, kvs, sq, pl.BlockSpec((None, 1, 2 * W), lambda a, g: (a * NP + g, 0, 0)), gain, gain,
                            pl.BlockSpec((None, nb, 1, 2 * W), lambda a, g: (a * NP + g, 0, 0, 0)), qcols],
                  out_specs=[qcols, kvs, kvs, sq, pl.BlockSpec((2, 1, LANES), lambda a, g: (a * NP + g, 0, 0)), gain, gain],
                  out_shape=[jax.ShapeDtypeStruct((S, HQD), BF16), ks, ks, jax.ShapeDtypeStruct((PP, 2 * W, 2 * W), F32),
                             jax.ShapeDtypeStruct((2 * PP, 1, LANES), F32), gs, gs],
                  scratch_shapes=[pltpu.VMEM((2, S, LANES), BF16), pltpu.VMEM((S, LANES), BF16), pltpu.VMEM((LANES, S), BF16),
                                  pltpu.VMEM((S, LANES), BF16), pltpu.VMEM((2, S, LANES), BF16)] + [pltpu.VMEM((S, LANES), F32)] * 3,
                  compiler_params=_params(("arbitrary", "arbitrary")))(qproj, kk, vv, bias_t_ab, sink_row, gq2, gk2, lse_row, do)


def _adamw_update(w, g, m, v):
    m2 = ADAM_B1 * m + (1.0 - ADAM_B1) * g
    v2 = ADAM_B2 * v + (1.0 - ADAM_B2) * jnp.square(g)
    m_hat = m2 / (1.0 - ADAM_B1 ** ADAM_STEP)
    v_hat = v2 / (1.0 - ADAM_B2 ** ADAM_STEP)
    return -ADAM_LR * (m_hat / (jnp.sqrt(v_hat) + ADAM_EPS) + ADAM_WD * w), m2, v2


def _adamw(w, g, m, v, name, tr=256, tc=256):
    R, C = w.shape
    tr = min(tr, R)
    if R % tr == 0:
        grid, blk = (R // tr,), pl.BlockSpec((tr, C), lambda i: (i, 0))
    else:
        assert C % tc == 0
        grid, blk = (C // tc,), pl.BlockSpec((R, tc), lambda i: (0, i))

    def body(w_ref, g_ref, m_ref, v_ref, d_ref, m2_ref, v2_ref):
        d_ref[...], m2_ref[...], v2_ref[...] = _adamw_update(w_ref[...], g_ref[...], m_ref[...], v_ref[...])

    return _pcall(body, name=name, grid=grid, in_specs=[blk] * 4, out_specs=[blk] * 3,
                  out_shape=[jax.ShapeDtypeStruct((R, C), F32)] * 3, compiler_params=_params(("parallel",)))(w, g, m, v)


def _adamw_two_layers(w, g0, g1, m, v, name, tr=256):
    R, C = g0.shape
    assert R % tr == 0 and w.shape == (2 * R, C)
    nr = R // tr

    def body(w_ref, g0_ref, g1_ref, m_ref, v_ref, g_ref, d_ref, m2_ref, v2_ref):
        g = jnp.where(pl.program_id(0) == 0, g0_ref[...], g1_ref[...])
        g_ref[...] = g
        d_ref[...], m2_ref[...], v2_ref[...] = _adamw_update(w_ref[...], g, m_ref[...], v_ref[...])

    both = pl.BlockSpec((tr, C), lambda l, i: (l * nr + i, 0))
    first = pl.BlockSpec((tr, C), lambda l, i: (i * (1 - l) + (nr - 1) * l, 0))
    second = pl.BlockSpec((tr, C), lambda l, i: (i * l, 0))
    return _pcall(body, name=name, grid=(2, nr), in_specs=[both, first, second, both, both], out_specs=[both] * 4,
                  out_shape=[jax.ShapeDtypeStruct((2 * R, C), F32)] * 4,
                  compiler_params=_params(("arbitrary", "arbitrary")))(w, g0, g1, m, v)


def _sum_core_pair(arr, got, place, name, tr=512):
    P, hr, C = got.shape
    tr = tr if hr % tr == 0 else hr
    nt = hr // tr

    def body(place_ref, a_ref, g_ref, o_ref):
        o_ref[...] = (a_ref[...].astype(F32) + g_ref[...].astype(F32)).astype(o_ref.dtype)

    spec = pltpu.PrefetchScalarGridSpec(
        num_scalar_prefetch=1, grid=(P, nt),
        in_specs=[pl.BlockSpec((None, tr, C), lambda j, i, pr: (j, pr[1] * nt + i, 0)),
                  pl.BlockSpec((None, tr, C), lambda j, i, pr: (j, i, 0))],
        out_specs=pl.BlockSpec((None, tr, C), lambda j, i, pr: (j, i, 0)))
    return _pcall(body, name=name, grid_spec=spec, out_shape=jax.ShapeDtypeStruct(got.shape, BF16),
                  compiler_params=_params(("parallel", "parallel")))(place, arr, got)


def _sum_chips(pair, landed, place, name, tr=256):
    _, R, C = landed.shape
    tr = tr if R % tr == 0 else R

    def body(place_ref, p_ref, l_ref, o_ref):
        acc = p_ref[...].astype(F32)
        for k in range(3):
            acc = acc + l_ref[k].astype(F32)
        o_ref[...] = acc

    spec = pltpu.PrefetchScalarGridSpec(
        num_scalar_prefetch=1, grid=(R // tr,),
        in_specs=[pl.BlockSpec((None, tr, C), lambda i, pr: (pr[0], i, 0)), pl.BlockSpec((3, tr, C), lambda i, pr: (0, i, 0))],
        out_specs=pl.BlockSpec((None, tr, C), lambda i, pr: (pr[1], i, 0)))
    return _pcall(body, name=name, grid_spec=spec, out_shape=jax.ShapeDtypeStruct((2, R, C), F32),
                  compiler_params=_params(("parallel",)))(place, pair, landed)


def _sum_parts(parts, name, out_dtype, tr=128):
    P, R, C = parts.shape
    tr = min(tr, R)
    assert R % tr == 0, (R, tr)

    def body(p_ref, o_ref):
        acc = p_ref[0].astype(F32)
        for k in range(1, P):
            acc = acc + p_ref[k].astype(F32)
        o_ref[...] = acc.astype(o_ref.dtype)

    return _pcall(body, name=name, grid=(R // tr,), in_specs=[pl.BlockSpec((P, tr, C), lambda i: (0, i, 0))],
                  out_specs=pl.BlockSpec((tr, C), lambda i: (i, 0)),
                  out_shape=jax.ShapeDtypeStruct((R, C), out_dtype), compiler_params=_params(("parallel",)))(parts)


def _place():
    x, y, c = lax.axis_index("x"), lax.axis_index("y"), lax.axis_index("c")
    others = [(1 - x, y), (x, 1 - y), (1 - x, 1 - y)]
    return x, y, c, others


def _half_rows(ref, hh, lead=()):
    hr = ref.shape[-2] // 2
    return ref.at[(*lead, pl.ds(pl.multiple_of(hh * hr, 16), hr), slice(None))]


def _sem_arrays(*counts):
    return [pltpu.SemaphoreType.DMA((k,)) for k in counts]


SEM_SPEC = pl.BlockSpec(memory_space=pltpu.SEMAPHORE)
ANY_SPEC = pl.BlockSpec(memory_space=pl.ANY)
DATAFLOW = pltpu.SideEffectType.DATAFLOW_SIDE_EFFECTING


def _in_hbm(a):
    return pltpu.with_memory_space_constraint(a, pltpu.HBM)


def _gather_copies(srcs, lands, send_sems, recv_sems):
    x, y, c, others = _place()
    me = 2 * x + y

    def copy(w, k, dst_chip, to):
        return pltpu.make_async_remote_copy(src_ref=_half_rows(srcs[w], c), dst_ref=_half_rows(lands[w], c, (dst_chip,)),
                                            send_sem=send_sems.at[3 * w + k], recv_sem=recv_sems.at[3 * w + k],
                                            device_id=to, device_id_type=MESH)

    pairs = [(w, k, cx, cy) for w in range(len(srcs)) for k, (cx, cy) in enumerate(others)]
    return ([copy(w, k, me, (cx, cy, c)) for w, k, cx, cy in pairs],
            [copy(w, k, 2 * cx + cy, (cx, cy, c)) for w, k, cx, cy in pairs])


def _gather_start(shards, after, name):
    n = len(shards)

    def body(*refs):
        srcs, lands, send_sems, recv_sems, token = refs[:n], refs[n:2 * n], refs[2 * n + 1], refs[2 * n + 2], refs[-1]
        for cp in _gather_copies(srcs, lands, send_sems, recv_sems)[0]:
            cp.start()
        token[...] = jnp.zeros_like(token)

    lands = [lax.empty((N_CHIPS,) + s.shape, s.dtype) for s in shards]
    outs = _pcall(
        body, name=name, in_specs=[HBM_SPEC] * (2 * n) + [ANY_SPEC],
        out_specs=[SEM_SPEC, SEM_SPEC] + [HBM_SPEC] * (2 * n) + [VMEM_SPEC],
        out_shape=[pltpu.SemaphoreType.DMA((3 * n,)), pltpu.SemaphoreType.DMA((3 * n,))]
        + [pltpu.HBM(a.shape, a.dtype) for a in list(shards) + lands] + [jax.ShapeDtypeStruct((8, LANES), F32)],
        input_output_aliases={i: 2 + i for i in range(2 * n)},
        compiler_params=pltpu.CompilerParams(has_side_effects=DATAFLOW),
    )(*[_in_hbm(a) for a in list(shards) + lands], after)
    return outs[0], outs[1], outs[2:2 + n], outs[2 + n:2 + 2 * n], outs[-1]


def _gather_wait(started, after, name):
    send_sems, recv_sems, srcs, lands, _ = started
    n = len(srcs)

    def body(*refs):
        src_refs, land_refs, send_ref, recv_ref = refs[:n], refs[n:2 * n], refs[2 * n], refs[2 * n + 1]
        outgoing, incoming = _gather_copies(src_refs, land_refs, send_ref, recv_ref)
        for out_cp, in_cp in zip(outgoing, incoming):
            out_cp.wait_send()
            in_cp.wait_recv()

    outs = _pcall(
        body, name=name, in_specs=[HBM_SPEC] * (2 * n) + [SEM_SPEC, SEM_SPEC, ANY_SPEC], out_specs=[HBM_SPEC] * (2 * n),
        out_shape=[pltpu.HBM(a.shape, a.dtype) for a in list(srcs) + list(lands)],
        input_output_aliases={i: i for i in range(2 * n)},
        compiler_params=pltpu.CompilerParams(has_side_effects=DATAFLOW),
    )(*srcs, *lands, send_sems, recv_sems, after)
    return outs[:n], outs[n:]


def _gather_pass_on(shards, lands, name):
    n = len(shards)
    per = 4

    def body(*refs):
        srcs, bufs = refs[:n], refs[2 * n:3 * n]
        send_sems, recv_sems = refs[3 * n:]
        x, y, c, others = _place()
        me = 2 * x + y
        sibling = (x, y, 1 - c)

        def copy(w, k, src, dst):
            return pltpu.make_async_remote_copy(src_ref=src, dst_ref=dst, send_sem=send_sems.at[per * w + k],
                                                recv_sem=recv_sems.at[per * w + k], device_id=sibling, device_id_type=MESH)

        sends, recvs = [], []
        for w in range(n):
            for k, (cx, cy) in enumerate(others):
                mine, theirs = _half_rows(bufs[w], c, (2 * cx + cy,)), _half_rows(bufs[w], 1 - c, (2 * cx + cy,))
                sends.append(copy(w, k, mine, mine))
                recvs.append(copy(w, k, theirs, theirs))
            sends.append(copy(w, 3, srcs[w], bufs[w].at[me]))
            recvs.append(sends[-1])
        for cp in sends:
            cp.start()
        for snd, rcv in zip(sends, recvs):
            snd.wait_send()
            rcv.wait_recv()

    return _pcall(body, name=name, in_specs=[HBM_SPEC] * (2 * n), out_specs=[HBM_SPEC] * n,
                  out_shape=[jax.ShapeDtypeStruct(l.shape, l.dtype) for l in lands],
                  input_output_aliases={n + w: w for w in range(n)},
                  scratch_shapes=_sem_arrays(per * n, per * n))(*shards, *lands)


def _scatter_copies(srcs, lands, send_sems, recv_sems):
    x, y, c, others = _place()
    return [pltpu.make_async_remote_copy(src_ref=srcs[w].at[2 * cx + cy], dst_ref=lands[w].at[k],
                                         send_sem=send_sems.at[3 * w + k], recv_sem=recv_sems.at[3 * w + k],
                                         device_id=(cx, cy, c), device_id_type=MESH)
            for w in range(len(srcs)) for k, (cx, cy) in enumerate(others)]


def _scatter_start(parts, name):
    n = len(parts)

    def body(*refs):
        srcs, lands, send_sems, recv_sems, token = refs[:n], refs[n:2 * n], refs[2 * n], refs[2 * n + 1], refs[-1]
        for cp in _scatter_copies(srcs, lands, send_sems, recv_sems):
            cp.start()
        token[...] = jnp.zeros_like(token)

    lands = [lax.empty((3,) + p.shape[1:], p.dtype) for p in parts]
    outs = _pcall(
        body, name=name, in_specs=[HBM_SPEC] * (2 * n), out_specs=[SEM_SPEC, SEM_SPEC] + [HBM_SPEC] * (2 * n) + [VMEM_SPEC],
        out_shape=[pltpu.SemaphoreType.DMA((3 * n,)), pltpu.SemaphoreType.DMA((3 * n,))]
        + [pltpu.HBM(a.shape, a.dtype) for a in list(parts) + lands] + [jax.ShapeDtypeStruct((8, LANES), F32)],
        input_output_aliases={i: 2 + i for i in range(2 * n)},
        compiler_params=pltpu.CompilerParams(has_side_effects=DATAFLOW),
    )(*[_in_hbm(a) for a in list(parts) + lands])
    return (outs[0], outs[1], outs[2:2 + n], outs[2 + n:2 + 2 * n]), outs[-1]


def _scatter_wait(started, after, name):
    send_sems, recv_sems, srcs, lands = started
    n = len(srcs)

    def body(*refs):
        for cp in _scatter_copies(refs[:n], refs[n:2 * n], refs[2 * n], refs[2 * n + 1]):
            cp.wait_send()
            cp.wait_recv()

    outs = _pcall(
        body, name=name, in_specs=[HBM_SPEC] * (2 * n) + [SEM_SPEC, SEM_SPEC, ANY_SPEC], out_specs=[HBM_SPEC] * (2 * n),
        out_shape=[pltpu.HBM(a.shape, a.dtype) for a in list(srcs) + list(lands)],
        input_output_aliases={i: i for i in range(2 * n)},
        compiler_params=pltpu.CompilerParams(has_side_effects=DATAFLOW),
    )(*srcs, *lands, send_sems, recv_sems, after)
    return outs[:n], outs[n:]


def _allgather_group(shards, name):
    n = len(shards)
    per = 7

    def body(*refs):
        ins, outs = refs[:n], refs[n:2 * n]
        send_sems, recv_sems = refs[2 * n:]
        x, y, c, others = _place()
        me = 2 * x + y
        sibling = (x, y, 1 - c)

        def copy(w, k, src, dst, to):
            return pltpu.make_async_remote_copy(src_ref=src, dst_ref=dst, send_sem=send_sems.at[per * w + k],
                                                recv_sem=recv_sems.at[per * w + k], device_id=to, device_id_type=MESH)

        first = [copy(w, k, _half_rows(ins[w], c), _half_rows(outs[w], c, (me,)), (cx, cy, c))
                 for w in range(n) for k, (cx, cy) in enumerate(others)]
        own = [copy(w, 6, ins[w], outs[w].at[me], sibling) for w in range(n)]
        for cp in first + own:
            cp.start()
        passed = []
        for w in range(n):
            for k, (cx, cy) in enumerate(others):
                landed = _half_rows(outs[w], c, (2 * cx + cy,))
                copy(w, k, landed, landed, sibling).wait_recv()
                passed.append(copy(w, 3 + k, landed, landed, sibling))
                passed[-1].start()
        for w in range(n):
            for k, (cx, cy) in enumerate(others):
                theirs = _half_rows(outs[w], 1 - c, (2 * cx + cy,))
                copy(w, 3 + k, theirs, theirs, sibling).wait_recv()
            own[w].wait_recv()
        for cp in first + passed + own:
            cp.wait_send()

    return _pcall(body, name=name, in_specs=[HBM_SPEC] * n, out_specs=[HBM_SPEC] * n,
                  out_shape=[jax.ShapeDtypeStruct((N_CHIPS,) + s.shape, s.dtype) for s in shards],
                  scratch_shapes=_sem_arrays(per * n, per * n))(*shards)


def _swap_halves_group(arrs, name):
    n = len(arrs)

    def body(*refs):
        ins, gots = refs[:n], refs[n:2 * n]
        send_sems, recv_sems = refs[2 * n:]
        x, y, c, _ = _place()
        swaps = [pltpu.make_async_remote_copy(src_ref=_half_rows(ins[w], 1 - c, (slice(None),)), dst_ref=gots[w],
                                              send_sem=send_sems.at[w], recv_sem=recv_sems.at[w],
                                              device_id=(x, y, 1 - c), device_id_type=MESH) for w in range(n)]
        for cp in swaps:
            cp.start()
        for cp in swaps:
            cp.wait()

    half_shapes = [jax.ShapeDtypeStruct((a.shape[0], a.shape[1] // 2, a.shape[2]), a.dtype) for a in arrs]
    return _pcall(body, name=name, in_specs=[HBM_SPEC] * n, out_specs=[HBM_SPEC] * n, out_shape=half_shapes,
                  scratch_shapes=_sem_arrays(n, n))(*arrs)


def _scatter_group(parts, name):
    n = len(parts)

    def body(*refs):
        ins, outs = refs[:n], refs[n:2 * n]
        send_sems, recv_sems = refs[2 * n:]
        x, y, c, others = _place()

        def copy(w, k, src_chip, to):
            return pltpu.make_async_remote_copy(src_ref=ins[w].at[src_chip], dst_ref=outs[w].at[k],
                                                send_sem=send_sems.at[3 * w + k], recv_sem=recv_sems.at[3 * w + k],
                                                device_id=to, device_id_type=MESH)

        sends = [copy(w, k, 2 * cx + cy, (cx, cy, c)) for w in range(n) for k, (cx, cy) in enumerate(others)]
        for cp in sends:
            cp.start()
        for cp in sends:
            cp.wait()

    return _pcall(body, name=name, in_specs=[HBM_SPEC] * n, out_specs=[HBM_SPEC] * n,
                  out_shape=[jax.ShapeDtypeStruct((3,) + p.shape[1:], p.dtype) for p in parts],
                  scratch_shapes=_sem_arrays(3 * n, 3 * n))(*parts)


def _share_halves_group(halves, name):
    n = len(halves)

    def body(*refs):
        bufs = refs[n:2 * n]
        send_sems, recv_sems = refs[2 * n:]
        x, y, c, _ = _place()
        swaps = [pltpu.make_async_remote_copy(src_ref=bufs[w].at[c], dst_ref=bufs[w].at[c], send_sem=send_sems.at[w],
                                              recv_sem=recv_sems.at[w], device_id=(x, y, 1 - c), device_id_type=MESH)
                 for w in range(n)]
        for cp in swaps:
            cp.start()
        for w in range(n):
            swaps[w].wait_send()
            pltpu.make_async_remote_copy(src_ref=bufs[w].at[c], dst_ref=bufs[w].at[1 - c], send_sem=send_sems.at[w],
                                         recv_sem=recv_sems.at[w], device_id=(x, y, 1 - c), device_id_type=MESH).wait_recv()

    return _pcall(body, name=name, in_specs=[HBM_SPEC] * n, out_specs=[HBM_SPEC] * n,
                  out_shape=[jax.ShapeDtypeStruct(h.shape, h.dtype) for h in halves],
                  input_output_aliases={w: w for w in range(n)},
                  scratch_shapes=_sem_arrays(n, n))(*halves)


def _allgather_small(blk, name):
    M, C = blk.shape

    def body(x_ref, out_ref, send_sems, recv_sems, local_sem):
        x, y, c, others = _place()
        me, sibling = (x, y, c), (x, y, 1 - c)

        def rows(px, py, pc):
            return out_ref.at[4 * px + 2 * py + pc]

        def copy(k, block, to, src=None):
            return pltpu.make_async_remote_copy(src_ref=rows(*block) if src is None else src, dst_ref=rows(*block),
                                                send_sem=send_sems.at[k], recv_sem=recv_sems.at[k], device_id=to, device_id_type=MESH)

        mine = pltpu.make_async_copy(x_ref, rows(*me), local_sem)
        mine.start()
        first = [copy(0, me, sibling, src=x_ref)]
        first += [copy(1 + j, me, (*chip, c), src=x_ref) for j, chip in enumerate(others)]
        for cp in first:
            cp.start()
        passed = [copy(4 + j, (*chip, c), sibling) for j, chip in enumerate(others)]
        for j, chip in enumerate(others):
            copy(1 + j, (*chip, c), me).wait_recv()
            passed[j].start()
        copy(0, sibling, me).wait_recv()
        for j, chip in enumerate(others):
            copy(4 + j, (*chip, 1 - c), me).wait_recv()
        for cp in first + passed:
            cp.wait_send()
        mine.wait()

    return _pcall(body, name=name, in_specs=[VMEM_SPEC], out_specs=VMEM_SPEC,
                  out_shape=jax.ShapeDtypeStruct((8, M, C), blk.dtype),
                  scratch_shapes=[pltpu.SemaphoreType.DMA((7,)), pltpu.SemaphoreType.DMA((7,)), pltpu.SemaphoreType.DMA])(blk)


def _pack_rows(n_elems, width=PACK_W, align=PACK_ROW_ALIGN):
    rows = -(-n_elems // width)
    return -(-rows // align) * align


def _pack(arrays, dtype, width=PACK_W, align=PACK_ROW_ALIGN):
    flat = jnp.concatenate([a.astype(dtype).reshape(-1) for a in arrays])
    rows = _pack_rows(flat.shape[0], width, align)
    flat = jnp.pad(flat, (0, rows * width - flat.shape[0]))
    return flat.reshape(rows, width)


def _pack_small(arrays):
    return _pack(arrays, F32, width=128, align=8)


def _unpack(flat, shapes):
    out, off = [], 0
    for shp in shapes:
        n = int(np.prod(shp))
        out.append(flat[..., off:off + n].reshape(flat.shape[:-1] + tuple(shp)))
        off += n
    return out


def _doubled_heads(x2d, n_heads):
    S = x2d.shape[0]
    h = x2d.reshape(S, n_heads, HEAD_DIM).transpose(1, 0, 2)
    return jnp.concatenate([h, h], axis=-1)


def _residual_then_norms(acc, res, *gains):
    h = res + acc
    hn = h * _rinv(h)
    return (h,) + tuple(hn * g for g in gains)


def _mlp_fwd(h, n, w_up4, w_down, next_gains, tag):
    u, a = _matmul(n, w_up4, "nn", f"up{tag}", out_dtypes=(F32, BF16), chipwise="b", tm=2048, tn=512,
                   epilogue=lambda acc: (acc, jnp.square(jnp.maximum(acc, 0.0))))
    assert w_down.shape[1] == 1024
    outs = _matmul(a, w_down, "nn", f"down{tag}", out_dtypes=(F32,) + (BF16,) * len(next_gains), extras=(h,),
                   row_extras=tuple(next_gains), epilogue=_residual_then_norms, tm=1024, tn=1024, tk=1024)
    outs = outs if next_gains else (outs,)
    return outs[0], outs[1:], (n, u, a)


def _mlp_bwd(dh_out, dh_out_b, h, g, w_up4, w_down, saved, tag):
    n, u, a = saved
    dw_down = _matmul(a, dh_out_b, "tn", f"dw_down{tag}", out_dtypes=(BF16,), tm=2048, tn=512)
    du = _matmul(dh_out_b, w_down, "nt", f"du{tag}", out_dtypes=(BF16,), extras=(u,), tm=2048, tn=512,
                 epilogue=lambda acc, uu: (acc * (2.0 * jnp.maximum(uu, 0.0)),))
    dw_up = _matmul(n, du, "tn", f"dw_up{tag}", out_dtypes=(BF16,), chipwise="out", tm=1024, tn=512)
    dn = _matmul(du, w_up4, "nt", f"dn_mlp{tag}", tm=1024, tn=1024, tk=w_up4.shape[2], chipwise="b")
    dh, dh_b, (dg,) = _rms_bwd(h, dh_out, [g], [dn], f"rms_mlp_bwd{tag}")
    return dh, dh_b, dg, dw_up, dw_down


def kernel(x, g_attn, g_mlp, w_in_a, b_f, gq_a, gk_a, w_out_a, g_kv, w_kv, gk_b, w_q_b, gq_b, sinks, rel_bias, w_out_b, w_up, w_down, loss_target, m_g_attn, m_g_mlp, m_w_in_a, m_b_f, m_gq_a, m_gk_a, m_w_out_a, m_g_kv, m_w_kv, m_gk_b, m_w_q_b, m_gq_b, m_sinks, m_rel_bias, m_w_out_b, m_w_up, m_w_down, v_g_attn, v_g_mlp, v_w_in_a, v_b_f, v_gq_a, v_gk_a, v_w_out_a, v_g_kv, v_w_kv, v_gk_b, v_w_q_b, v_gq_b, v_sinks, v_rel_bias, v_w_out_b, v_w_up, v_w_down):
    given = dict(locals())
    S, D = x.shape[1], x.shape[2]
    H = D // HEAD_DIM
    KVH = w_kv.shape[1] // (2 * HEAD_DIM)
    kvw = KVH * HEAD_DIM
    hw = H * HEAD_DIM
    W = WINDOW
    nb = S // W
    c_idx = lax.axis_index("c")
    xs, tgt = x[0], loss_target[0]

    n_in_shard = w_in_a.shape[2]
    rows_in = -(-n_in_shard // 32) * 32
    row_pad = lambda a: jnp.pad(a, [(0, 0)] * (a.ndim - 2) + [(0, rows_in - a.shape[-2]), (0, 0)])
    t_in = lambda a: jnp.swapaxes(a[0], 0, 1)
    shards = {"w_in_a": row_pad(t_in(w_in_a)), "w_out_a": w_out_a[0], "w_up0": w_up[0], "w_down0": w_down[0], "w_kv": w_kv,
              "w_q_b": w_q_b[0], "w_out_b": w_out_b[0], "w_up1": w_up[1], "w_down1": w_down[1]}
    parts = list(shards)
    groups = [("w_in_a", "w_out_a"), ("w_up0", "w_down0"), ("w_kv", "w_q_b", "w_out_b", "w_up1", "w_down1")]
    started = []
    for i, grp in enumerate(groups):
        behind = started[-1][4] if started else g_attn[0]
        started.append(_gather_start([shards[n].astype(BF16) for n in grp], behind, f"gather_start{i}"))
    gathered = {}

    def finish_gather(i, after):
        srcs, lands = _gather_wait(started[i], after, f"gather_wait{i}")
        gathered.update(zip(groups[i], _gather_pass_on(srcs, lands, f"gather_pass_on{i}")))

    vec = lambda a: a.reshape(1, -1)
    twice = lambda a: jnp.tile(a.reshape(1, -1), (1, 2))

    g_attn0 = vec(g_attn[0]) + sum(st[4][0, 0] for st in started)
    (n0,) = _rms_fwd(xs, [g_attn0], "rms_attn0")
    finish_gather(0, n0)
    win_t = gathered["w_in_a"][:, :n_in_shard].reshape(-1, D)
    win_t = jnp.pad(win_t, ((0, (-win_t.shape[0]) % 128), (0, 0)))
    wout_a = gathered["w_out_a"].reshape(-1, D)
    n_in = win_t.shape[0]
    tile_in = 640 if n_in % 640 == 0 else 128
    proj = _matmul(n0, win_t, "nt", "proj_in", tm=2048, tn=tile_in)
    zt = proj[:, 3 * hw:3 * hw + H].T
    c_row = _gate_fwd(zt, b_f.reshape(H, 1), "gate_fwd")
    c_row3 = c_row.reshape(H, 1, S)
    o_a, lse_a = _fox_fwd(proj, c_row3, twice(gq_a[0]), twice(gk_a[0]), H, "fox_fwd")
    h1, n1 = _matmul(o_a, wout_a, "nn", "out_a", out_dtypes=(F32, BF16), extras=(xs,), row_extras=(vec(g_mlp[0]),),
                     epilogue=_residual_then_norms, tm=1024, tn=1024)
    finish_gather(1, h1)
    wup = [gathered["w_up0"], None]
    wdown = [gathered["w_down0"].reshape(-1, D), None]
    h2, (nkv, n2), mlp0 = _mlp_fwd(h1, n1, wup[0], wdown[0], [vec(g_kv), vec(g_attn[1])], "0")

    finish_gather(2, h2)
    wq_b, wout_b = gathered["w_q_b"].reshape(-1, D), gathered["w_out_b"].reshape(-1, D)
    wkv = gathered["w_kv"].reshape(D, -1)
    wup[1], wdown[1] = gathered["w_up1"], gathered["w_down1"].reshape(-1, D)
    kv = _matmul(nkv, wkv, "nn", "proj_kv", tm=2048)
    kk, vv = _doubled_heads(kv[:, :kvw], KVH), _doubled_heads(kv[:, kvw:], KVH)
    q2 = _matmul(n2, wq_b, "nn", "proj_q", tm=1024, tn=1024)
    onehot = jnp.asarray(_bucket_onehot(), dtype=BF16)
    bias = _bias_expand(rel_bias.T, onehot, "bias_expand").reshape(H, W, 2 * W)
    bias_ab = bias.reshape(H // 2, 2 * W, 2 * W)
    bias_t_ab = bias.reshape(H // 2, 2, W, 2 * W).transpose(0, 3, 1, 2).reshape(H // 2, 2 * W, 2 * W)
    sink_ab = jnp.repeat(sinks[0].reshape(H // 2, 2), W, axis=1)
    o_b, lse_b = _swa_fwd(q2, kk, vv, bias_ab, sink_ab.reshape(H // 2, 2 * W, 1), twice(gq_b[0]), twice(gk_b), "swa_fwd")
    h3, n3 = _matmul(o_b, wout_b, "nn", "out_b", out_dtypes=(F32, BF16), extras=(h2,), row_extras=(vec(g_mlp[1]),),
                     epilogue=_residual_then_norms, tm=1024, tn=1024)
    h4, _, mlp1 = _mlp_fwd(h3, n3, wup[1], wdown[1], [], "1")

    dh4, dh4_b, loss_part = _loss_head(h4, tgt, "loss_head")

    place = jnp.stack([2 * lax.axis_index("x") + lax.axis_index("y"), c_idx]).astype(jnp.int32)
    scattering = []

    def start_reduce(named):
        names = list(named)
        got = _swap_halves_group([named[n] for n in names], "swap_grad_halves_" + names[0])
        pair_sums = [_sum_core_pair(named[n], g, place, "sum_core_pair_" + n) for n, g in zip(names, got)]
        started_scatter, token = _scatter_start(pair_sums, "scatter_start_" + names[0])
        scattering.append((names, started_scatter))
        return token[0:1, :]

    dh3, dh3_b, dg_mlp1, dw_up1, dw_down1 = _mlp_bwd(dh4, dh4_b, h3, vec(g_mlp[1]), wup[1], wdown[1], mlp1, "1")
    tie1 = start_reduce({"w_down1": dw_down1.reshape(N_CHIPS, -1, D), "w_up1": dw_up1})
    dw_out_b = _matmul(o_b, dh3_b, "tn", "dw_out_b", out_dtypes=(BF16,), tm=1024, tn=1024)
    do_b = _matmul(dh3_b, wout_b, "nt", "do_b", tm=1024, tn=1024)
    dq2, dk2, dv2, dbias_t_ab, dsink, dgq_b, dgk_b = _swa_bwd(
        q2, kk, vv, bias_t_ab, sink_ab.reshape(H // 2, 1, 2 * W), twice(gq_b[0]) + tie1, twice(gk_b),
        lse_b, do_b, "swa_bwd")
    dbias = dbias_t_ab.reshape(H // 2, 2 * W, 2, W).transpose(0, 2, 3, 1).reshape(H, W * 2 * W)
    d_rel_bias = _bias_reduce(dbias, onehot, "bias_reduce").T
    dw_q_b = _matmul(n2, dq2, "tn", "dw_q_b", out_dtypes=(BF16,), tm=1024, tn=1024)
    dn2 = _matmul(dq2, wq_b, "nt", "dn2", tm=1024, tn=1024)
    dkv = jnp.concatenate([dk2[h, :, :HEAD_DIM] for h in range(KVH)] + [dv2[h, :, :HEAD_DIM] for h in range(KVH)],
                          axis=1).astype(BF16)
    dw_kv = _matmul(nkv, dkv, "tn", "dw_kv", out_dtypes=(BF16,), tm=1024)
    dnkv = _matmul(dkv, wkv, "nt", "dnkv", tm=1024, tn=1024)
    tie2 = start_reduce({"w_out_b": dw_out_b.reshape(N_CHIPS, -1, D), "w_q_b": dw_q_b.reshape(N_CHIPS, -1, D),
                         "w_kv": dw_kv.reshape(N_CHIPS, -1, 2 * kvw)})
    dh2, dh2_b, (dg_kv, dg_attn1) = _rms_bwd(h2, dh3, [vec(g_kv) + tie2[:, :1], vec(g_attn[1])], [dnkv, dn2], "rms_attn1_bwd")

    dh1, dh1_b, dg_mlp0, dw_up0, dw_down0 = _mlp_bwd(dh2, dh2_b, h1, vec(g_mlp[0]), wup[0], wdown[0], mlp0, "0")
    tie3 = start_reduce({"w_down0": dw_down0.reshape(N_CHIPS, -1, D), "w_up0": dw_up0})
    dw_out_a = _matmul(o_a, dh1_b, "tn", "dw_out_a", out_dtypes=(BF16,), tm=1024, tn=1024)
    do_a = _matmul(dh1_b, wout_a, "nt", "do_a", tm=1024, tn=1024)
    dq_a, dk_a, dv_a, dc_row, dgq_a, dgk_a = _fox_bwd(
        proj, c_row3, twice(gq_a[0]) + tie3, twice(gk_a[0]), lse_a, do_a, H, "fox_bwd")
    dzt, db_f = _gate_bwd(dc_row.reshape(H, S), zt, b_f.reshape(H, 1), "gate_bwd")
    dproj = jnp.concatenate([dq_a, dk_a, dv_a, dzt.T.astype(BF16), jnp.zeros((S, n_in - 3 * hw - H), BF16)], axis=1)
    dw_in_t = _matmul(dproj, n0, "tn", "dw_in", out_dtypes=(BF16,), tm=tile_in, tn=1024)
    dw_in4 = row_pad(dw_in_t[:3 * hw + H].reshape(N_CHIPS, -1, D))
    tie4 = start_reduce({"w_out_a": dw_out_a.reshape(N_CHIPS, -1, D), "w_in_a": dw_in4})
    dn0 = _matmul(dproj, win_t, "nn", "dn0", tm=1024, tn=1024, tk=tile_in)
    grad_x, _, (dg_attn0,) = _rms_bwd(xs, dh1, [vec(g_attn[0]) + tie4[:, :1]], [dn0], "rms_attn0_bwd")

    reduced = {}
    for names, started_scatter in scattering:
        pair_sums, landed = _scatter_wait(started_scatter, grad_x, "scatter_wait_" + names[0])
        halves = [_sum_chips(p, l, place, "sum_chips_" + n) for n, p, l in zip(names, pair_sums, landed)]
        for n, r in zip(names, _share_halves_group(halves, "share_reduced_halves_" + names[0])):
            reduced[n] = r.reshape(-1, r.shape[2])
    reduced["w_in_a"] = reduced["w_in_a"][:n_in_shard]

    small_grads = {
        "g_attn": jnp.concatenate([dg_attn0, dg_attn1], axis=0), "g_mlp": jnp.concatenate([dg_mlp0, dg_mlp1], axis=0),
        "b_f": db_f.reshape(1, H), "gq_a": dgq_a[:, :HEAD_DIM], "gk_a": dgk_a[:, :HEAD_DIM], "g_kv": dg_kv.reshape(-1),
        "gk_b": dgk_b[0, :HEAD_DIM], "gq_b": dgq_b[:, :HEAD_DIM], "sinks": dsink[:, 0, 0].reshape(1, H), "rel_bias": d_rel_bias,
    }
    small_shapes = [given[n].shape for n in SMALL] + [(1,)]
    spack = _pack_small([small_grads[n] for n in SMALL] + [loss_part])
    small_sum = _sum_parts(_allgather_small(spack, "allgather_small"), "sum_small", F32, tr=spack.shape[0])
    small_red = _unpack(small_sum.reshape(-1), small_shapes)
    loss = small_red[-1][0]

    grads = dict(zip(SMALL, small_red))
    no_loss = [jnp.zeros((1,), F32)]
    sw = _pack_small([given[n] for n in SMALL] + no_loss)
    sm = _pack_small([given["m_" + n] for n in SMALL] + no_loss)
    sv = _pack_small([given["v_" + n] for n in SMALL] + no_loss)
    sd, sm2, sv2 = _adamw(sw, small_sum, sm, sv, "adamw_small", tr=sw.shape[0])
    delta = dict(zip(SMALL, _unpack(sd.reshape(-1), small_shapes)))
    new_m = dict(zip(SMALL, _unpack(sm2.reshape(-1), small_shapes)))
    new_v = dict(zip(SMALL, _unpack(sv2.reshape(-1), small_shapes)))
    for n in ("w_out_a", "w_kv", "w_q_b", "w_out_b"):
        w = given[n]
        two_d = (-1, w.shape[-1])
        d, m2, v2 = _adamw(w.reshape(two_d), reduced[n], given["m_" + n].reshape(two_d), given["v_" + n].reshape(two_d),
                           "adamw_" + n)
        grads[n] = reduced[n].reshape(w.shape)
        delta[n], new_m[n], new_v[n] = d.reshape(w.shape), m2.reshape(w.shape), v2.reshape(w.shape)
    d, m2, v2 = _adamw(t_in(w_in_a), reduced["w_in_a"], t_in(m_w_in_a), t_in(v_w_in_a), "adamw_w_in_a")
    back = lambda a: jnp.swapaxes(a, 0, 1)[None]
    grads["w_in_a"], delta["w_in_a"], new_m["w_in_a"], new_v["w_in_a"] = back(reduced["w_in_a"]), back(d), back(m2), back(v2)
    for n in ("w_up", "w_down"):
        w = given[n]
        two_d = (-1, w.shape[-1])
        g, d, m2, v2 = _adamw_two_layers(w.reshape(two_d), reduced[n + "0"], reduced[n + "1"], given["m_" + n].reshape(two_d),
                                         given["v_" + n].reshape(two_d), "adamw_" + n)
        grads[n], delta[n], new_m[n], new_v[n] = g.reshape(w.shape), d.reshape(w.shape), m2.reshape(w.shape), v2.reshape(w.shape)

    order = ["g_attn", "g_mlp", "w_in_a", "b_f", "gq_a", "gk_a", "w_out_a", "g_kv", "w_kv", "gk_b", "w_q_b", "gq_b",
             "sinks", "rel_bias", "w_out_b", "w_up", "w_down"]
    return (loss, grad_x[None], *[grads[n] for n in order], *[delta[n] for n in order],
            *[new_m[n] for n in order], *[new_v[n] for n in order])
```

```python
import numpy as np
import jax
import jax.numpy as jnp
from jax import lax
from jax.experimental import pallas as pl
from jax.experimental.pallas import tpu as pltpu

F32 = jnp.float32
BF16 = jnp.bfloat16
MESH = pl.DeviceIdType.MESH

HEAD_DIM = 64
LANES = 128
WINDOW = 128
N_BUCKETS = 32
REL_MAX_DIST = 128
NORM_EPS = 1e-6
ADAM_LR = 0.001
ADAM_B1 = 0.9
ADAM_B2 = 0.999
ADAM_EPS = 1e-08
ADAM_WD = 0.01
ADAM_STEP = 10
NEG = -1e30
N_CHIPS = 4
PACK_W = 1024
PACK_ROW_ALIGN = 256
VMEM_LIMIT = 56 * 1024 * 1024
HBM_SPEC = pl.BlockSpec(memory_space=pltpu.HBM)
VMEM_SPEC = pl.BlockSpec(memory_space=pltpu.VMEM)

BIG = (("w_in_a", 2), ("w_out_a", 1), ("w_kv", 0), ("w_q_b", 1), ("w_out_b", 1), ("w_up", 2), ("w_down", 1))
SMALL = ("g_attn", "g_mlp", "b_f", "gq_a", "gk_a", "g_kv", "gk_b", "gq_b", "sinks", "rel_bias")


def _pcall(body, **kw):
    return pl.pallas_call(body, **kw)


def _params(sem=None):
    return pltpu.CompilerParams(dimension_semantics=sem, vmem_limit_bytes=VMEM_LIMIT)


def _rinv(x):
    return lax.rsqrt(jnp.mean(x * x, axis=-1, keepdims=True) + NORM_EPS)


def _dot(a, b, dims, precision=None):
    return lax.dot_general(a, b, (dims, ((), ())), precision=precision, preferred_element_type=F32)


NN = ((1,), (0,))
NT = ((1,), (1,))
TN = ((0,), (0,))


def _accumulate(ref, val, first):
    @pl.when(first)
    def _():
        ref[...] = val

    @pl.when(jnp.logical_not(first))
    def _():
        ref[...] += val


def _matmul(a, b, mode, name, out_dtypes=(F32,), extras=(), row_extras=(), epilogue=None, tm=512, tn=512, tk=None, chipwise=None,
            after=None):
    if chipwise == "b":
        nc = b.shape[2]
        M, K = a.shape
        (K2, N) = (b.shape[1], N_CHIPS * nc) if mode == "nn" else (N_CHIPS * nc, b.shape[1])
    elif mode == "nn":
        (M, K), (K2, N) = a.shape, b.shape
    elif mode == "nt":
        (M, K), (N, K2) = a.shape, b.shape
    else:
        (K, M), (K2, N) = a.shape, b.shape
    assert K == K2, (a.shape, b.shape, mode)
    tm, tn = min(tm, M), min(tn, N)
    tk = K if tk is None else tk
    assert M % tm == 0 and N % tn == 0 and K % tk == 0, (M, N, K, tm, tn, tk)
    nk = K // tk
    dims = {"nn": NN, "nt": NT, "tn": TN}[mode]
    a_spec = pl.BlockSpec((tk, tm), lambda i, j, k: (k, i)) if mode == "tn" else pl.BlockSpec((tm, tk), lambda i, j, k: (i, k))
    b_spec = pl.BlockSpec((tn, tk), lambda i, j, k: (j, k)) if mode == "nt" else pl.BlockSpec((tk, tn), lambda i, j, k: (k, j))
    o_spec = pl.BlockSpec((tm, tn), lambda i, j, k: (i, j))
    out_shape = (M, N)
    if chipwise == "b" and mode == "nn":
        per = nc // tn
        assert tk == K and nc % tn == 0
        b_spec = pl.BlockSpec((None, tk, tn), lambda i, j, k: (j // per, 0, j % per))
    elif chipwise == "b":
        assert mode == "nt" and tk == nc
        b_spec = pl.BlockSpec((None, tn, tk), lambda i, j, k: (k, j, 0))
    elif chipwise == "out":
        per = (N // N_CHIPS) // tn
        assert (N // N_CHIPS) % tn == 0
        o_spec = pl.BlockSpec((None, tm, tn), lambda i, j, k: (j // per, i, j % per))
        out_shape = (N_CHIPS, M, N // N_CHIPS)
        assert not extras
    n_ex, n_rex, n_out = len(extras), len(row_extras), len(out_dtypes)
    tail = () if after is None else (after,)
    n_in = 2 + n_ex + n_rex + len(tail)

    def body(*refs):
        a_ref, b_ref = refs[0], refs[1]
        ex_refs = refs[2:2 + n_ex + n_rex]
        out_refs = refs[n_in:n_in + n_out]
        part = _dot(a_ref[...].astype(BF16), b_ref[...].astype(BF16), dims)

        def finish(acc):
            outs = (acc,) if epilogue is None else epilogue(acc, *[r[...] for r in ex_refs])
            for r, o in zip(out_refs, outs):
                r[...] = o.astype(r.dtype)

        if nk == 1:
            finish(part)
            return
        acc_ref = refs[n_in + n_out]
        k = pl.program_id(2)

        @pl.when(k == 0)
        def _():
            acc_ref[...] = part

        @pl.when(jnp.logical_and(k > 0, k < nk - 1))
        def _():
            acc_ref[...] += part

        @pl.when(k == nk - 1)
        def _():
            finish(acc_ref[...] + part)

    row_spec = pl.BlockSpec((1, tn), lambda i, j, k: (0, j))
    outs = _pcall(
        body, name=name, grid=(M // tm, N // tn, nk),
        in_specs=[a_spec, b_spec] + [o_spec] * n_ex + [row_spec] * n_rex + [pl.BlockSpec(memory_space=pl.ANY)] * len(tail),
        out_specs=[o_spec] * n_out,
        out_shape=[jax.ShapeDtypeStruct(out_shape, dt) for dt in out_dtypes],
        scratch_shapes=[pltpu.VMEM((tm, tn), F32)] if nk > 1 else [],
        compiler_params=_params(("parallel", "parallel", "arbitrary")),
    )(a, b, *extras, *row_extras, *tail)
    return outs[0] if n_out == 1 else outs


def _rms_fwd(x, gains, name, ts=256):
    S, D = x.shape
    ts = min(ts, S)
    n = len(gains)

    def body(*refs):
        x_ref, g_refs, o_refs = refs[0], refs[1:1 + n], refs[1 + n:]
        xv = x_ref[...]
        xh = xv * _rinv(xv)
        for g_ref, o_ref in zip(g_refs, o_refs):
            o_ref[...] = (xh * g_ref[...]).astype(BF16)

    row = pl.BlockSpec((ts, D), lambda i: (i, 0))
    vec = pl.BlockSpec((1, D), lambda i: (0, 0))
    return _pcall(body, name=name, grid=(S // ts,), in_specs=[row] + [vec] * n, out_specs=[row] * n,
                  out_shape=[jax.ShapeDtypeStruct((S, D), BF16)] * n, compiler_params=_params(("parallel",)))(x, *gains)


def _rms_bwd(x, dres, gains, dns, name, ts=256):
    S, D = x.shape
    ts = min(ts, S)
    n = len(gains)

    def body(*refs):
        x_ref, dres_ref = refs[0], refs[1]
        g_refs, dn_refs = refs[2:2 + n], refs[2 + n:2 + 2 * n]
        dx_ref, dxb_ref, dg_refs = refs[2 + 2 * n], refs[3 + 2 * n], refs[4 + 2 * n:]
        xv = x_ref[...]
        r = _rinv(xv)
        xh = xv * r
        dx = dres_ref[...]
        first = pl.program_id(0) == 0
        for g_ref, dn_ref, dg_ref in zip(g_refs, dn_refs, dg_refs):
            dn = dn_ref[...].astype(F32)
            _accumulate(dg_ref, jnp.sum(dn * xh, axis=0, keepdims=True), first)
            dxh = dn * g_ref[...]
            dx = dx + r * (dxh - xh * jnp.mean(dxh * xh, axis=-1, keepdims=True))
        dx_ref[...] = dx
        dxb_ref[...] = dx.astype(BF16)

    row = pl.BlockSpec((ts, D), lambda i: (i, 0))
    vec = pl.BlockSpec((1, D), lambda i: (0, 0))
    outs = _pcall(body, name=name, grid=(S // ts,), in_specs=[row, row] + [vec] * n + [row] * n,
                  out_specs=[row, row] + [vec] * n,
                  out_shape=[jax.ShapeDtypeStruct((S, D), F32), jax.ShapeDtypeStruct((S, D), BF16)]
                  + [jax.ShapeDtypeStruct((1, D), F32)] * n,
                  compiler_params=_params(("arbitrary",)))(x, dres, *gains, *dns)
    return outs[0], outs[1], outs[2:]


def _loss_head(h, tgt, name, ts=256):
    S, D = h.shape
    ts = min(ts, S)

    def body(h_ref, t_ref, dh_ref, dhb_ref, loss_ref):
        err = h_ref[...] - t_ref[...]
        dh = err * (1.0 / D)
        dh_ref[...] = dh
        dhb_ref[...] = dh.astype(BF16)
        part = 0.5 * jnp.sum(jnp.mean(err * err, axis=-1, keepdims=True), axis=0, keepdims=True)
        _accumulate(loss_ref, part, pl.program_id(0) == 0)

    row = pl.BlockSpec((ts, D), lambda i: (i, 0))
    return _pcall(body, name=name, grid=(S // ts,), in_specs=[row, row],
                  out_specs=[row, row, pl.BlockSpec((1, 1), lambda i: (0, 0))],
                  out_shape=[jax.ShapeDtypeStruct((S, D), F32), jax.ShapeDtypeStruct((S, D), BF16),
                             jax.ShapeDtypeStruct((1, 1), F32)],
                  compiler_params=_params(("arbitrary",)))(h, tgt)


def _gate_fwd(zt, bf, name):
    H, S = zt.shape
    nb = S // 128

    def body(z_ref, b_ref, c_ref):
        z = z_ref[...] + b_ref[...]
        lf = jnp.minimum(z, 0.0) - jnp.log(1.0 + jnp.exp(-jnp.abs(z)))
        upper = (lax.broadcasted_iota(jnp.int32, (128, 128), 0) <= lax.broadcasted_iota(jnp.int32, (128, 128), 1)).astype(F32)
        carry = jnp.zeros((H, 1), F32)
        for blk in range(nb):
            cs = _dot(lf[:, blk * 128:(blk + 1) * 128], upper, NN, precision=lax.Precision.HIGHEST) + carry
            c_ref[:, blk * 128:(blk + 1) * 128] = cs
            carry = cs[:, 127:128]

    return _pcall(body, name=name, in_specs=[VMEM_SPEC, VMEM_SPEC], out_specs=VMEM_SPEC,
                  out_shape=jax.ShapeDtypeStruct((H, S), F32))(zt, bf)


def _gate_bwd(dct, zt, bf, name):
    H, S = zt.shape
    nb = S // 128

    def body(dc_ref, z_ref, b_ref, dz_ref, db_ref):
        z = z_ref[...] + b_ref[...]
        e = jnp.exp(-jnp.abs(z))
        sig_neg = jnp.where(z >= 0, e, 1.0) / (1.0 + e)
        lower = (lax.broadcasted_iota(jnp.int32, (128, 128), 0) >= lax.broadcasted_iota(jnp.int32, (128, 128), 1)).astype(F32)
        dc = dc_ref[...]
        carry = jnp.zeros((H, 1), F32)
        db = jnp.zeros((H, 1), F32)
        for blk in reversed(range(nb)):
            sl = slice(blk * 128, (blk + 1) * 128)
            dlf = _dot(dc[:, sl], lower, NN, precision=lax.Precision.HIGHEST) + carry
            carry = dlf[:, 0:1]
            dz = dlf * sig_neg[:, sl]
            dz_ref[:, sl] = dz
            db = db + jnp.sum(dz, axis=1, keepdims=True)
        db_ref[...] = db

    return _pcall(body, name=name, in_specs=[VMEM_SPEC] * 3, out_specs=[VMEM_SPEC] * 2,
                  out_shape=[jax.ShapeDtypeStruct((H, S), F32), jax.ShapeDtypeStruct((H, 1), F32)])(dct, zt, bf)


def _lane_is_a():
    return lax.broadcasted_iota(jnp.int32, (1, LANES), 1) < HEAD_DIM


def _per_head_mean(x, is_a):
    sa = jnp.sum(jnp.where(is_a, x, 0.0), axis=-1, keepdims=True)
    sb = jnp.sum(jnp.where(is_a, 0.0, x), axis=-1, keepdims=True)
    return jnp.where(is_a, sa, sb) / HEAD_DIM


def _pair_norm(raw, gain, is_a):
    return raw * lax.rsqrt(_per_head_mean(raw * raw, is_a) + NORM_EPS) * gain


def _pair_norm_bwd(raw, gain, dnormed, is_a):
    r = lax.rsqrt(_per_head_mean(raw * raw, is_a) + NORM_EPS)
    xh = raw * r
    dgain = jnp.sum(dnormed * xh, axis=0, keepdims=True)
    dxh = dnormed * gain
    return r * (dxh - xh * _per_head_mean(dxh * xh, is_a)), dgain


def _fold_heads(x):
    i = lax.broadcasted_iota(jnp.int32, (LANES, LANES), 0)
    j = lax.broadcasted_iota(jnp.int32, (LANES, LANES), 1)
    fold = ((i == j) | (i == j + HEAD_DIM) | (i + HEAD_DIM == j)).astype(F32)
    return _dot(x, fold, NN, precision=lax.Precision.HIGHEST)


def _fold_row(ref):
    ref[...] = _fold_heads(jnp.broadcast_to(ref[...], (8, LANES)))[0:1, :]


def _as_col(row):
    return jnp.broadcast_to(row, (LANES, row.shape[1])).T[:, 0:1]


def _as_row(col):
    return jnp.broadcast_to(col, (col.shape[0], LANES)).T[0:1, :]


def _tri_mask(t, keys_on_rows):
    r = lax.broadcasted_iota(jnp.int32, (t, t), 0)
    c = lax.broadcasted_iota(jnp.int32, (t, t), 1)
    return (r <= c) if keys_on_rows else (r >= c)


def _fox_fwd(proj, c_row, gq2, gk2, n_heads, name, t=256):
    S = proj.shape[0]
    H = n_heads
    P = H // 2
    t = min(t, S)
    nq = S // t

    def body(q_ref, k_ref, v_ref, cr_ref, gq_ref, gk_ref, o_ref, lse_ref, qs_s, kb_s, vb_s):
        is_a = _lane_is_a()
        qn = _pair_norm(q_ref[...], gq_ref[...], is_a) * 0.125
        qs_s[0] = jnp.where(is_a, qn, 0.0).astype(BF16)
        qs_s[1] = jnp.where(is_a, 0.0, qn).astype(BF16)
        kb_s[...] = _pair_norm(k_ref[...], gk_ref[...], is_a).astype(BF16)
        vb_s[...] = v_ref[...].astype(BF16)
        causal = _tri_mask(t, False)
        for i in range(nq):
            t0 = i * t
            rows = slice(t0, t0 + t)
            o_pair = None
            for a in range(2):
                qi = qs_s[a, rows, :]
                ci = _as_col(cr_ref[a, :, rows])
                s_d = jnp.where(causal, _dot(qi, kb_s[rows, :], NT) + ci - cr_ref[a, :, rows], NEG)
                m = jnp.max(s_d, axis=-1, keepdims=True)
                if i > 0:
                    s_l = _dot(qi, kb_s[0:t0, :], NT) + ci - cr_ref[a, :, 0:t0]
                    m = jnp.maximum(m, jnp.max(s_l, axis=-1, keepdims=True))
                p_d = jnp.exp(s_d - m)
                l = jnp.sum(p_d, axis=-1, keepdims=True)
                acc = _dot(p_d.astype(BF16), vb_s[rows, :], NN)
                if i > 0:
                    p_l = jnp.exp(s_l - m)
                    l = l + jnp.sum(p_l, axis=-1, keepdims=True)
                    acc = acc + _dot(p_l.astype(BF16), vb_s[0:t0, :], NN)
                o_a = acc / l
                lse_ref[a, :, rows] = _as_row(m + jnp.log(l))
                o_pair = o_a if a == 0 else jnp.where(is_a, o_pair, o_a)
            o_ref[rows, :] = o_pair.astype(BF16)

    def cols(off):
        return pl.BlockSpec((S, LANES), lambda p: (0, off + p))

    rowv = pl.BlockSpec((2, 1, S), lambda p: (p, 0, 0))
    gain = pl.BlockSpec((1, LANES), lambda p: (0, 0))
    return _pcall(body, name=name, grid=(P,), in_specs=[cols(0), cols(P), cols(2 * P), rowv, gain, gain],
                  out_specs=[cols(0), rowv],
                  out_shape=[jax.ShapeDtypeStruct((S, H * HEAD_DIM), BF16), jax.ShapeDtypeStruct((H, 1, S), F32)],
                  scratch_shapes=[pltpu.VMEM((2, S, LANES), BF16), pltpu.VMEM((S, LANES), BF16), pltpu.VMEM((S, LANES), BF16)],
                  compiler_params=_params(("parallel",)))(proj, proj, proj, c_row, gq2, gk2)


def _fox_bwd(proj, c_row, gq2, gk2, lse_row, do, n_heads, name, t=256):
    S = proj.shape[0]
    H = n_heads
    P = H // 2
    t = min(t, S)
    nq = S // t
    assert t % LANES == 0

    def body(q_ref, k_ref, v_ref, cr_ref, gq_ref, gk_ref, lr_ref, do_ref,
             dq_ref, dk_ref, dv_ref, dc_ref, dgq_ref, dgk_ref,
             qs_s, kb_s, kt_s, vb_s, dob_s, dq_s, dk_s, dv_s, dcs_s, cc_s):
        is_a = _lane_is_a()
        for a in range(2):
            for i in range(nq):
                cc_s[a, i * t:(i + 1) * t, :] = _as_col(cr_ref[a, :, i * t:(i + 1) * t])
        qn = _pair_norm(q_ref[...], gq_ref[...], is_a) * 0.125
        qs_s[0] = jnp.where(is_a, qn, 0.0).astype(BF16)
        qs_s[1] = jnp.where(is_a, 0.0, qn).astype(BF16)
        kn = _pair_norm(k_ref[...], gk_ref[...], is_a)
        kb_s[...] = kn.astype(BF16)
        kt_s[0] = jnp.where(is_a, kn, 0.0).T.astype(BF16)
        kt_s[1] = jnp.where(is_a, 0.0, kn).T.astype(BF16)
        vb_s[...] = v_ref[...].astype(BF16)
        dov = do_ref[...]
        dob_s[0] = jnp.where(is_a, dov, 0.0).astype(BF16)
        dob_s[1] = jnp.where(is_a, 0.0, dov).astype(BF16)
        dk_s[...] = jnp.zeros((S, LANES), F32)
        dv_s[...] = jnp.zeros((S, LANES), F32)
        dcs_s[...] = jnp.zeros((2, S, LANES), F32)
        causal = _tri_mask(t, True)
        for i in range(nq):
            t0 = i * t
            rows = slice(t0, t0 + t)
            dq_t = jnp.zeros((LANES, t), F32)
            for a in range(2):
                qi = qs_s[a, rows, :]
                doi = dob_s[a, rows, :]
                cri = cr_ref[a, :, rows]
                lri = lr_ref[a, :, rows]

                def probs(keys, masked, a=a, qi=qi, doi=doi, cri=cri, lri=lri):
                    p_t = jnp.exp(_dot(kb_s[keys, :], qi, NT) + cri - cc_s[a, keys, :] - lri)
                    if masked:
                        p_t = jnp.where(causal, p_t, 0.0)
                    return p_t, _dot(vb_s[keys, :], doi, NT)

                parts = [(rows,) + probs(rows, True)]
                if i > 0:
                    parts.append((slice(0, t0),) + probs(slice(0, t0), False))
                delta = sum(jnp.sum(p_t * dp_t, axis=0, keepdims=True) for _, p_t, dp_t in parts)
                for keys, p_t, dp_t in parts:
                    ds_t = p_t * (dp_t - delta)
                    dsb = ds_t.astype(BF16)
                    dv_s[keys, :] += _dot(p_t.astype(BF16), doi, NN)
                    dk_s[keys, :] += _dot(dsb, qi, NN)
                    dq_t = dq_t + _dot(kt_s[a, :, keys], dsb, NN)
                    dcs_s[a, keys, :] += sum(ds_t[:, b * LANES:(b + 1) * LANES] for b in range(t // LANES))
            dq_s[rows, :] = dq_t.T
        first = pl.program_id(0) == 0
        last = pl.program_id(0) == P - 1
        dq_raw, dgq = _pair_norm_bwd(q_ref[...], gq_ref[...], dq_s[...] * 0.125, is_a)
        dq_ref[...] = dq_raw.astype(BF16)
        _accumulate(dgq_ref, dgq, first)
        dk_raw, dgk = _pair_norm_bwd(k_ref[...], gk_ref[...], dk_s[...], is_a)
        dk_ref[...] = dk_raw.astype(BF16)
        _accumulate(dgk_ref, dgk, first)
        dv_ref[...] = dv_s[...].astype(BF16)
        for a in range(2):
            for i in range(nq):
                rows = slice(i * t, (i + 1) * t)
                dc_ref[a, :, rows] = _as_row(-jnp.sum(dcs_s[a, rows, :], axis=1, keepdims=True))

        @pl.when(last)
        def _():
            _fold_row(dgq_ref)
            _fold_row(dgk_ref)

    def cols(off):
        return pl.BlockSpec((S, LANES), lambda p: (0, off + p))

    rowv = pl.BlockSpec((2, 1, S), lambda p: (p, 0, 0))
    gain = pl.BlockSpec((1, LANES), lambda p: (0, 0))
    wide = jax.ShapeDtypeStruct((S, H * HEAD_DIM), BF16)
    gs = jax.ShapeDtypeStruct((1, LANES), F32)
    return _pcall(body, name=name, grid=(P,),
                  in_specs=[cols(0), cols(P), cols(2 * P), rowv, gain, gain, rowv, cols(0)],
                  out_specs=[cols(0), cols(0), cols(0), rowv, gain, gain],
                  out_shape=[wide, wide, wide, jax.ShapeDtypeStruct((H, 1, S), F32), gs, gs],
                  scratch_shapes=[pltpu.VMEM((2, S, LANES), BF16), pltpu.VMEM((S, LANES), BF16), pltpu.VMEM((2, LANES, S), BF16),
                                  pltpu.VMEM((S, LANES), BF16), pltpu.VMEM((2, S, LANES), BF16)]
                  + [pltpu.VMEM((S, LANES), F32)] * 3 + [pltpu.VMEM((2, S, LANES), F32), pltpu.VMEM((2, S, 1), F32)],
                  compiler_params=_params(("arbitrary",)))(proj, proj, proj, c_row, gq2, gk2, lse_row, do)


def _bucket_onehot():
    W = WINDOW
    dist = np.arange(W)[:, None] + W - np.arange(2 * W)[None, :]
    n = np.maximum(dist, 0)
    max_exact = N_BUCKETS // 2
    large = max_exact + (np.log(np.maximum(n, 1) / max_exact) / np.log(REL_MAX_DIST / max_exact)
                         * (N_BUCKETS - max_exact)).astype(np.int32)
    large = np.minimum(large, N_BUCKETS - 1)
    bucket = np.where(n < max_exact, n, large).astype(np.int32)
    valid = (dist >= 0) & (dist < W)
    onehot = (bucket[None] == np.arange(N_BUCKETS)[:, None, None]) & valid[None]
    return onehot.reshape(N_BUCKETS, W * 2 * W).astype(np.float32)


def _bias_expand(rel_bias_t, onehot, name, tn=4096):
    HQ, NB = rel_bias_t.shape
    L = onehot.shape[1]

    def body(r_ref, oh_ref, out_ref):
        out_ref[...] = _dot(r_ref[...], oh_ref[...].astype(F32), NN, precision=lax.Precision.HIGHEST)

    return _pcall(body, name=name, grid=(L // tn,),
                  in_specs=[pl.BlockSpec((HQ, NB), lambda i: (0, 0)), pl.BlockSpec((NB, tn), lambda i: (0, i))],
                  out_specs=pl.BlockSpec((HQ, tn), lambda i: (0, i)),
                  out_shape=jax.ShapeDtypeStruct((HQ, L), F32), compiler_params=_params(("parallel",)))(rel_bias_t, onehot)


def _bias_reduce(dbias, onehot, name, tk=4096):
    HQ, L = dbias.shape
    NB = onehot.shape[0]

    def body(d_ref, oh_ref, out_ref):
        part = _dot(d_ref[...], oh_ref[...].astype(F32), NT, precision=lax.Precision.HIGHEST)
        _accumulate(out_ref, part, pl.program_id(0) == 0)

    return _pcall(body, name=name, grid=(L // tk,),
                  in_specs=[pl.BlockSpec((HQ, tk), lambda i: (0, i)), pl.BlockSpec((NB, tk), lambda i: (0, i))],
                  out_specs=pl.BlockSpec((HQ, NB), lambda i: (0, 0)),
                  out_shape=jax.ShapeDtypeStruct((HQ, NB), F32), compiler_params=_params(("arbitrary",)))(dbias, onehot)


def _stacked_query_index(n_rows_or_cols_axis, shape):
    idx = lax.broadcasted_iota(jnp.int32, shape, n_rows_or_cols_axis)
    return jnp.where(idx >= WINDOW, idx - WINDOW, idx)


def _swa_fwd(qproj, kk, vv, bias_ab, sink_col, gq2, gk2, name):
    S, HQD = qproj.shape
    KVH = kk.shape[0]
    PP = HQD // LANES
    NP = PP // KVH
    W = WINDOW
    nb = S // W

    def body(q_ref, k_ref, v_ref, bias_ref, sink_ref, gq_ref, gk_ref, o_ref, lse_ref, qs_s, kb_s, vb_s):
        is_a = _lane_is_a()
        qn = _pair_norm(q_ref[...], gq_ref[...], is_a) * 0.125
        qs_s[0] = jnp.where(is_a, qn, 0.0).astype(BF16)
        qs_s[1] = jnp.where(is_a, 0.0, qn).astype(BF16)
        kb_s[...] = _pair_norm(k_ref[...], gk_ref[...], is_a).astype(BF16)
        vb_s[...] = v_ref[...].astype(BF16)
        sink = sink_ref[...]
        qi1 = _stacked_query_index(0, (2 * W, W))
        first_valid = lax.broadcasted_iota(jnp.int32, (2 * W, W), 1) <= qi1
        qi2 = _stacked_query_index(0, (2 * W, 2 * W))
        key2 = lax.broadcasted_iota(jnp.int32, (2 * W, 2 * W), 1)
        band_valid = (key2 > qi2) & (key2 <= qi2 + W)
        for n in range(nb):
            rows = slice(n * W, (n + 1) * W)
            keys = slice(0, W) if n == 0 else slice((n - 1) * W, (n + 1) * W)
            lhs = jnp.concatenate([qs_s[0, rows, :], qs_s[1, rows, :]], axis=0)
            s = _dot(lhs, kb_s[keys, :], NT) + (bias_ref[:, W:2 * W] if n == 0 else bias_ref[...])
            s = jnp.where(first_valid if n == 0 else band_valid, s, NEG)
            m = jnp.maximum(jnp.max(s, axis=-1, keepdims=True), sink)
            e = jnp.exp(s - m)
            l = jnp.sum(e, axis=-1, keepdims=True) + jnp.exp(sink - m)
            o_ab = _dot(e.astype(BF16), vb_s[keys, :], NN) / l
            o_ref[rows, :] = jnp.where(is_a, o_ab[0:W, :], o_ab[W:2 * W, :]).astype(BF16)
            lse_ref[n] = _as_row(m + jnp.log(l))

    qcols = pl.BlockSpec((S, LANES), lambda a, g: (0, a * NP + g))
    kvs = pl.BlockSpec((None, S, LANES), lambda a, g: (a, 0, 0))
    gain = pl.BlockSpec((1, LANES), lambda a, g: (0, 0))
    return _pcall(body, name=name, grid=(KVH, NP),
                  in_specs=[qcols, kvs, kvs, pl.BlockSpec((None, 2 * W, 2 * W), lambda a, g: (a * NP + g, 0, 0)),
                            pl.BlockSpec((None, 2 * W, 1), lambda a, g: (a * NP + g, 0, 0)), gain, gain],
                  out_specs=[qcols, pl.BlockSpec((None, nb, 1, 2 * W), lambda a, g: (a * NP + g, 0, 0, 0))],
                  out_shape=[jax.ShapeDtypeStruct((S, HQD), BF16), jax.ShapeDtypeStruct((PP, nb, 1, 2 * W), F32)],
                  scratch_shapes=[pltpu.VMEM((2, S, LANES), BF16), pltpu.VMEM((S, LANES), BF16), pltpu.VMEM((S, LANES), BF16)],
                  compiler_params=_params(("parallel", "parallel")))(qproj, kk, vv, bias_ab, sink_col, gq2, gk2)


def _swa_bwd(qproj, kk, vv, bias_t_ab, sink_row, gq2, gk2, lse_row, do, name):
    S, HQD = qproj.shape
    KVH = kk.shape[0]
    PP = HQD // LANES
    NP = PP // KVH
    W = WINDOW
    nb = S // W

    def body(q_ref, k_ref, v_ref, bias_ref, sink_ref, gq_ref, gk_ref, lr_ref, do_ref,
             dq_ref, dk_ref, dv_ref, db_ref, dsink_ref, dgq_ref, dgk_ref,
             qs_s, kb_s, kt_s, vb_s, dob_s, dq_s, dk_s, dv_s):
        a, g = pl.program_id(0), pl.program_id(1)
        is_a = _lane_is_a()
        qn = _pair_norm(q_ref[...], gq_ref[...], is_a) * 0.125
        qs_s[0] = jnp.where(is_a, qn, 0.0).astype(BF16)
        qs_s[1] = jnp.where(is_a, 0.0, qn).astype(BF16)
        kn = _pair_norm(k_ref[...], gk_ref[...], is_a)
        kb_s[...] = kn.astype(BF16)
        kt_s[...] = kn.T.astype(BF16)
        vb_s[...] = v_ref[...].astype(BF16)
        dov = do_ref[...]
        dob_s[0] = jnp.where(is_a, dov, 0.0).astype(BF16)
        dob_s[1] = jnp.where(is_a, 0.0, dov).astype(BF16)
        sink = sink_ref[...]

        @pl.when(g == 0)
        def _():
            dk_s[...] = jnp.zeros((S, LANES), F32)
            dv_s[...] = jnp.zeros((S, LANES), F32)

        qi1 = _stacked_query_index(1, (W, 2 * W))
        first_valid = lax.broadcasted_iota(jnp.int32, (W, 2 * W), 0) <= qi1
        qi2 = _stacked_query_index(1, (2 * W, 2 * W))
        key2 = lax.broadcasted_iota(jnp.int32, (2 * W, 2 * W), 0)
        band_valid = (key2 > qi2) & (key2 <= qi2 + W)
        head_rows = lax.broadcasted_iota(jnp.int32, (LANES, W), 0) < HEAD_DIM
        db = jnp.zeros((2 * W, 2 * W), F32)
        dsk = jnp.zeros((1, 2 * W), F32)
        pend_k = pend_v = None
        for n in range(nb):
            rows = slice(n * W, (n + 1) * W)
            keys = slice(0, W) if n == 0 else slice((n - 1) * W, (n + 1) * W)
            lhs_q = jnp.concatenate([qs_s[0, rows, :], qs_s[1, rows, :]], axis=0)
            lhs_do = jnp.concatenate([dob_s[0, rows, :], dob_s[1, rows, :]], axis=0)
            lse = lr_ref[n]
            s_t = _dot(kb_s[keys, :], lhs_q, NT) + (bias_ref[W:2 * W, :] if n == 0 else bias_ref[...])
            p_t = jnp.where(first_valid if n == 0 else band_valid, jnp.exp(s_t - lse), 0.0)
            dp_t = _dot(vb_s[keys, :], lhs_do, NT)
            delta = jnp.sum(p_t * dp_t, axis=0, keepdims=True)
            ds_t = p_t * (dp_t - delta)
            dsb = ds_t.astype(BF16)
            dsk = dsk - jnp.exp(sink - lse) * delta
            dv_band = _dot(p_t.astype(BF16), lhs_do, NN)
            dk_band = _dot(dsb, lhs_q, NN)
            dq_t = _dot(kt_s[:, keys], dsb, NN)
            dq_s[rows, :] = jnp.where(head_rows, dq_t[:, 0:W], dq_t[:, W:2 * W]).T
            if n == 0:
                db = jnp.concatenate([jnp.zeros((W, 2 * W), F32), ds_t], axis=0)
                pend_k, pend_v = dk_band, dv_band
            else:
                db = db + ds_t
                prev = slice((n - 1) * W, n * W)
                dk_s[prev, :] += pend_k + dk_band[0:W, :]
                dv_s[prev, :] += pend_v + dv_band[0:W, :]
                pend_k, pend_v = dk_band[W:2 * W, :], dv_band[W:2 * W, :]
        tail = slice((nb - 1) * W, nb * W)
        dk_s[tail, :] += pend_k
        dv_s[tail, :] += pend_v
        db_ref[...] = db
        dsink_ref[0] = jnp.broadcast_to(jnp.sum(dsk[:, 0:W], axis=1, keepdims=True), (1, LANES))
        dsink_ref[1] = jnp.broadcast_to(jnp.sum(dsk[:, W:2 * W], axis=1, keepdims=True), (1, LANES))
        dq_raw, dgq = _pair_norm_bwd(q_ref[...], gq_ref[...], dq_s[...] * 0.125, is_a)
        dq_ref[...] = dq_raw.astype(BF16)
        _accumulate(dgq_ref, dgq, jnp.logical_and(a == 0, g == 0))

        @pl.when(jnp.logical_and(a == KVH - 1, g == NP - 1))
        def _():
            _fold_row(dgq_ref)

        @pl.when(g == NP - 1)
        def _():
            dk_raw, dgk = _pair_norm_bwd(k_ref[...], gk_ref[...], _fold_heads(dk_s[...]), is_a)
            dk_ref[...] = dk_raw
            _accumulate(dgk_ref, dgk, a == 0)
            dv_ref[...] = _fold_heads(dv_s[...])

    qcols = pl.BlockSpec((S, LANES), lambda a, g: (0, a * NP + g))
    kvs = pl.BlockSpec((None, S, LANES), lambda a, g: (a, 0, 0))
    sq = pl.BlockSpec((None, 2 * W, 2 * W), lambda a, g: (a * NP + g, 0, 0))
    gain = pl.BlockSpec((1, LANES), lambda a, g: (0, 0))
    ks = jax.ShapeDtypeStruct((KVH, S, LANES), F32)
    gs = jax.ShapeDtypeStruct((1, LANES), F32)
    return _pcall(body, name=name, grid=(KVH, NP),
                  in_specs=[qcols, kvs, kvs, sq, pl.BlockSpec((None, 1, 2 * W), lambda a, g: (a * NP + g, 0, 0)), gain, gain,
                            pl.BlockSpec((None, nb, 1, 2 * W), lambda a, g: (a * NP + g, 0, 0, 0)), qcols],
                  out_specs=[qcols, kvs, kvs, sq, pl.BlockSpec((2, 1, LANES), lambda a, g: (a * NP + g, 0, 0)), gain, gain],
                  out_shape=[jax.ShapeDtypeStruct((S, HQD), BF16), ks, ks, jax.ShapeDtypeStruct((PP, 2 * W, 2 * W), F32),
                             jax.ShapeDtypeStruct((2 * PP, 1, LANES), F32), gs, gs],
                  scratch_shapes=[pltpu.VMEM((2, S, LANES), BF16), pltpu.VMEM((S, LANES), BF16), pltpu.VMEM((LANES, S), BF16),
                                  pltpu.VMEM((S, LANES), BF16), pltpu.VMEM((2, S, LANES), BF16)] + [pltpu.VMEM((S, LANES), F32)] * 3,
                  compiler_params=_params(("arbitrary", "arbitrary")))(qproj, kk, vv, bias_t_ab, sink_row, gq2, gk2, lse_row, do)


def _adamw_update(w, g, m, v):
    m2 = ADAM_B1 * m + (1.0 - ADAM_B1) * g
    v2 = ADAM_B2 * v + (1.0 - ADAM_B2) * jnp.square(g)
    m_hat = m2 / (1.0 - ADAM_B1 ** ADAM_STEP)
    v_hat = v2 / (1.0 - ADAM_B2 ** ADAM_STEP)
    return -ADAM_LR * (m_hat / (jnp.sqrt(v_hat) + ADAM_EPS) + ADAM_WD * w), m2, v2


def _adamw(w, g, m, v, name, tr=256, tc=256):
    R, C = w.shape
    tr = min(tr, R)
    if R % tr == 0:
        grid, blk = (R // tr,), pl.BlockSpec((tr, C), lambda i: (i, 0))
    else:
        assert C % tc == 0
        grid, blk = (C // tc,), pl.BlockSpec((R, tc), lambda i: (0, i))

    def body(w_ref, g_ref, m_ref, v_ref, d_ref, m2_ref, v2_ref):
        d_ref[...], m2_ref[...], v2_ref[...] = _adamw_update(w_ref[...], g_ref[...], m_ref[...], v_ref[...])

    return _pcall(body, name=name, grid=grid, in_specs=[blk] * 4, out_specs=[blk] * 3,
                  out_shape=[jax.ShapeDtypeStruct((R, C), F32)] * 3, compiler_params=_params(("parallel",)))(w, g, m, v)


def _adamw_two_layers(w, g0, g1, m, v, name, tr=256):
    R, C = g0.shape
    assert R % tr == 0 and w.shape == (2 * R, C)
    nr = R // tr

    def body(w_ref, g0_ref, g1_ref, m_ref, v_ref, g_ref, d_ref, m2_ref, v2_ref):
        g = jnp.where(pl.program_id(0) == 0, g0_ref[...], g1_ref[...])
        g_ref[...] = g
        d_ref[...], m2_ref[...], v2_ref[...] = _adamw_update(w_ref[...], g, m_ref[...], v_ref[...])

    both = pl.BlockSpec((tr, C), lambda l, i: (l * nr + i, 0))
    first = pl.BlockSpec((tr, C), lambda l, i: (i * (1 - l) + (nr - 1) * l, 0))
    second = pl.BlockSpec((tr, C), lambda l, i: (i * l, 0))
    return _pcall(body, name=name, grid=(2, nr), in_specs=[both, first, second, both, both], out_specs=[both] * 4,
                  out_shape=[jax.ShapeDtypeStruct((2 * R, C), F32)] * 4,
                  compiler_params=_params(("arbitrary", "arbitrary")))(w, g0, g1, m, v)


def _sum_core_pair(arr, got, place, name, tr=512):
    P, hr, C = got.shape
    tr = tr if hr % tr == 0 else hr
    nt = hr // tr

    def body(place_ref, a_ref, g_ref, o_ref):
        o_ref[...] = (a_ref[...].astype(F32) + g_ref[...].astype(F32)).astype(o_ref.dtype)

    spec = pltpu.PrefetchScalarGridSpec(
        num_scalar_prefetch=1, grid=(P, nt),
        in_specs=[pl.BlockSpec((None, tr, C), lambda j, i, pr: (j, pr[1] * nt + i, 0)),
                  pl.BlockSpec((None, tr, C), lambda j, i, pr: (j, i, 0))],
        out_specs=pl.BlockSpec((None, tr, C), lambda j, i, pr: (j, i, 0)))
    return _pcall(body, name=name, grid_spec=spec, out_shape=jax.ShapeDtypeStruct(got.shape, BF16),
                  compiler_params=_params(("parallel", "parallel")))(place, arr, got)


def _sum_chips(pair, landed, place, name, tr=256):
    _, R, C = landed.shape
    tr = tr if R % tr == 0 else R

    def body(place_ref, p_ref, l_ref, o_ref):
        acc = p_ref[...].astype(F32)
        for k in range(3):
            acc = acc + l_ref[k].astype(F32)
        o_ref[...] = acc

    spec = pltpu.PrefetchScalarGridSpec(
        num_scalar_prefetch=1, grid=(R // tr,),
        in_specs=[pl.BlockSpec((None, tr, C), lambda i, pr: (pr[0], i, 0)), pl.BlockSpec((3, tr, C), lambda i, pr: (0, i, 0))],
        out_specs=pl.BlockSpec((None, tr, C), lambda i, pr: (pr[1], i, 0)))
    return _pcall(body, name=name, grid_spec=spec, out_shape=jax.ShapeDtypeStruct((2, R, C), F32),
                  compiler_params=_params(("parallel",)))(place, pair, landed)


def _sum_parts(parts, name, out_dtype, tr=128):
    P, R, C = parts.shape
    tr = min(tr, R)
    assert R % tr == 0, (R, tr)

    def body(p_ref, o_ref):
        acc = p_ref[0].astype(F32)
        for k in range(1, P):
            acc = acc + p_ref[k].astype(F32)
        o_ref[...] = acc.astype(o_ref.dtype)

    return _pcall(body, name=name, grid=(R // tr,), in_specs=[pl.BlockSpec((P, tr, C), lambda i: (0, i, 0))],
                  out_specs=pl.BlockSpec((tr, C), lambda i: (i, 0)),
                  out_shape=jax.ShapeDtypeStruct((R, C), out_dtype), compiler_params=_params(("parallel",)))(parts)


def _place():
    x, y, c = lax.axis_index("x"), lax.axis_index("y"), lax.axis_index("c")
    others = [(1 - x, y), (x, 1 - y), (1 - x, 1 - y)]
    return x, y, c, others


def _half_rows(ref, hh, lead=()):
    hr = ref.shape[-2] // 2
    return ref.at[(*lead, pl.ds(pl.multiple_of(hh * hr, 16), hr), slice(None))]


def _sem_arrays(*counts):
    return [pltpu.SemaphoreType.DMA((k,)) for k in counts]


SEM_SPEC = pl.BlockSpec(memory_space=pltpu.SEMAPHORE)
ANY_SPEC = pl.BlockSpec(memory_space=pl.ANY)
DATAFLOW = pltpu.SideEffectType.DATAFLOW_SIDE_EFFECTING


def _in_hbm(a):
    return pltpu.with_memory_space_constraint(a, pltpu.HBM)


def _gather_copies(srcs, lands, send_sems, recv_sems):
    x, y, c, others = _place()
    me = 2 * x + y

    def copy(w, k, dst_chip, to):
        return pltpu.make_async_remote_copy(src_ref=_half_rows(srcs[w], c), dst_ref=_half_rows(lands[w], c, (dst_chip,)),
                                            send_sem=send_sems.at[3 * w + k], recv_sem=recv_sems.at[3 * w + k],
                                            device_id=to, device_id_type=MESH)

    pairs = [(w, k, cx, cy) for w in range(len(srcs)) for k, (cx, cy) in enumerate(others)]
    return ([copy(w, k, me, (cx, cy, c)) for w, k, cx, cy in pairs],
            [copy(w, k, 2 * cx + cy, (cx, cy, c)) for w, k, cx, cy in pairs])


def _gather_start(shards, after, name):
    n = len(shards)

    def body(*refs):
        srcs, lands, send_sems, recv_sems, token = refs[:n], refs[n:2 * n], refs[2 * n + 1], refs[2 * n + 2], refs[-1]
        for cp in _gather_copies(srcs, lands, send_sems, recv_sems)[0]:
            cp.start()
        token[...] = jnp.zeros_like(token)

    lands = [lax.empty((N_CHIPS,) + s.shape, s.dtype) for s in shards]
    outs = _pcall(
        body, name=name, in_specs=[HBM_SPEC] * (2 * n) + [ANY_SPEC],
        out_specs=[SEM_SPEC, SEM_SPEC] + [HBM_SPEC] * (2 * n) + [VMEM_SPEC],
        out_shape=[pltpu.SemaphoreType.DMA((3 * n,)), pltpu.SemaphoreType.DMA((3 * n,))]
        + [pltpu.HBM(a.shape, a.dtype) for a in list(shards) + lands] + [jax.ShapeDtypeStruct((8, LANES), F32)],
        input_output_aliases={i: 2 + i for i in range(2 * n)},
        compiler_params=pltpu.CompilerParams(has_side_effects=DATAFLOW),
    )(*[_in_hbm(a) for a in list(shards) + lands], after)
    return outs[0], outs[1], outs[2:2 + n], outs[2 + n:2 + 2 * n], outs[-1]


def _gather_wait(started, after, name):
    send_sems, recv_sems, srcs, lands, _ = started
    n = len(srcs)

    def body(*refs):
        src_refs, land_refs, send_ref, recv_ref = refs[:n], refs[n:2 * n], refs[2 * n], refs[2 * n + 1]
        outgoing, incoming = _gather_copies(src_refs, land_refs, send_ref, recv_ref)
        for out_cp, in_cp in zip(outgoing, incoming):
            out_cp.wait_send()
            in_cp.wait_recv()

    outs = _pcall(
        body, name=name, in_specs=[HBM_SPEC] * (2 * n) + [SEM_SPEC, SEM_SPEC, ANY_SPEC], out_specs=[HBM_SPEC] * (2 * n),
        out_shape=[pltpu.HBM(a.shape, a.dtype) for a in list(srcs) + list(lands)],
        input_output_aliases={i: i for i in range(2 * n)},
        compiler_params=pltpu.CompilerParams(has_side_effects=DATAFLOW),
    )(*srcs, *lands, send_sems, recv_sems, after)
    return outs[:n], outs[n:]


def _gather_pass_on(shards, lands, name):
    n = len(shards)
    per = 4

    def body(*refs):
        srcs, bufs = refs[:n], refs[2 * n:3 * n]
        send_sems, recv_sems = refs[3 * n:]
        x, y, c, others = _place()
        me = 2 * x + y
        sibling = (x, y, 1 - c)

        def copy(w, k, src, dst):
            return pltpu.make_async_remote_copy(src_ref=src, dst_ref=dst, send_sem=send_sems.at[per * w + k],
                                                recv_sem=recv_sems.at[per * w + k], device_id=sibling, device_id_type=MESH)

        sends, recvs = [], []
        for w in range(n):
            for k, (cx, cy) in enumerate(others):
                mine, theirs = _half_rows(bufs[w], c, (2 * cx + cy,)), _half_rows(bufs[w], 1 - c, (2 * cx + cy,))
                sends.append(copy(w, k, mine, mine))
                recvs.append(copy(w, k, theirs, theirs))
            sends.append(copy(w, 3, srcs[w], bufs[w].at[me]))
            recvs.append(sends[-1])
        for cp in sends:
            cp.start()
        for snd, rcv in zip(sends, recvs):
            snd.wait_send()
            rcv.wait_recv()

    return _pcall(body, name=name, in_specs=[HBM_SPEC] * (2 * n), out_specs=[HBM_SPEC] * n,
                  out_shape=[jax.ShapeDtypeStruct(l.shape, l.dtype) for l in lands],
                  input_output_aliases={n + w: w for w in range(n)},
                  scratch_shapes=_sem_arrays(per * n, per * n))(*shards, *lands)


def _scatter_copies(srcs, lands, send_sems, recv_sems):
    x, y, c, others = _place()
    return [pltpu.make_async_remote_copy(src_ref=srcs[w].at[2 * cx + cy], dst_ref=lands[w].at[k],
                                         send_sem=send_sems.at[3 * w + k], recv_sem=recv_sems.at[3 * w + k],
                                         device_id=(cx, cy, c), device_id_type=MESH)
            for w in range(len(srcs)) for k, (cx, cy) in enumerate(others)]


def _scatter_start(parts, name):
    n = len(parts)

    def body(*refs):
        srcs, lands, send_sems, recv_sems, token = refs[:n], refs[n:2 * n], refs[2 * n], refs[2 * n + 1], refs[-1]
        for cp in _scatter_copies(srcs, lands, send_sems, recv_sems):
            cp.start()
        token[...] = jnp.zeros_like(token)

    lands = [lax.empty((3,) + p.shape[1:], p.dtype) for p in parts]
    outs = _pcall(
        body, name=name, in_specs=[HBM_SPEC] * (2 * n), out_specs=[SEM_SPEC, SEM_SPEC] + [HBM_SPEC] * (2 * n) + [VMEM_SPEC],
        out_shape=[pltpu.SemaphoreType.DMA((3 * n,)), pltpu.SemaphoreType.DMA((3 * n,))]
        + [pltpu.HBM(a.shape, a.dtype) for a in list(parts) + lands] + [jax.ShapeDtypeStruct((8, LANES), F32)],
        input_output_aliases={i: 2 + i for i in range(2 * n)},
        compiler_params=pltpu.CompilerParams(has_side_effects=DATAFLOW),
    )(*[_in_hbm(a) for a in list(parts) + lands])
    return (outs[0], outs[1], outs[2:2 + n], outs[2 + n:2 + 2 * n]), outs[-1]


def _scatter_wait(started, after, name):
    send_sems, recv_sems, srcs, lands = started
    n = len(srcs)

    def body(*refs):
        for cp in _scatter_copies(refs[:n], refs[n:2 * n], refs[2 * n], refs[2 * n + 1]):
            cp.wait_send()
            cp.wait_recv()

    outs = _pcall(
        body, name=name, in_specs=[HBM_SPEC] * (2 * n) + [SEM_SPEC, SEM_SPEC, ANY_SPEC], out_specs=[HBM_SPEC] * (2 * n),
        out_shape=[pltpu.HBM(a.shape, a.dtype) for a in list(srcs) + list(lands)],
        input_output_aliases={i: i for i in range(2 * n)},
        compiler_params=pltpu.CompilerParams(has_side_effects=DATAFLOW),
    )(*srcs, *lands, send_sems, recv_sems, after)
    return outs[:n], outs[n:]


def _split_start(plan, arrays, n_copies, after, name):
    n = len(arrays)

    def body(*refs):
        for cp in plan(refs[:n], refs[n + 1], refs[n + 2])[0]:
            cp.start()
        refs[-1][...] = jnp.zeros_like(refs[-1])

    outs = _pcall(
        body, name=name, in_specs=[HBM_SPEC] * n + [ANY_SPEC], out_specs=[SEM_SPEC, SEM_SPEC] + [HBM_SPEC] * n + [VMEM_SPEC],
        out_shape=[pltpu.SemaphoreType.DMA((n_copies,)), pltpu.SemaphoreType.DMA((n_copies,))]
        + [pltpu.HBM(a.shape, a.dtype) for a in arrays] + [jax.ShapeDtypeStruct((8, LANES), F32)],
        input_output_aliases={i: 2 + i for i in range(n)},
        compiler_params=pltpu.CompilerParams(has_side_effects=DATAFLOW),
    )(*[_in_hbm(a) for a in arrays], after)
    return (outs[0], outs[1], outs[2:2 + n]), outs[-1]


def _split_wait(plan, started, after, name):
    send_sems, recv_sems, arrays = started
    n = len(arrays)

    def body(*refs):
        outgoing, incoming = plan(refs[:n], refs[n], refs[n + 1])
        for cp in outgoing:
            cp.wait_send()
        for cp in incoming:
            cp.wait_recv()

    return _pcall(
        body, name=name, in_specs=[HBM_SPEC] * n + [SEM_SPEC, SEM_SPEC, ANY_SPEC], out_specs=[HBM_SPEC] * n,
        out_shape=[pltpu.HBM(a.shape, a.dtype) for a in arrays], input_output_aliases={i: i for i in range(n)},
        compiler_params=pltpu.CompilerParams(has_side_effects=DATAFLOW),
    )(*arrays, send_sems, recv_sems, after)


def _to_sibling(src, dst, k, send_sems, recv_sems):
    x, y, c, _ = _place()
    return pltpu.make_async_remote_copy(src_ref=src, dst_ref=dst, send_sem=send_sems.at[k], recv_sem=recv_sems.at[k],
                                        device_id=(x, y, 1 - c), device_id_type=MESH)


def _plan_pass_on(n):
    def plan(refs, send_sems, recv_sems):
        x, y, c, others = _place()
        cps = []
        for w in range(n):
            for k, (cx, cy) in enumerate(others):
                mine = _half_rows(refs[n + w], c, (2 * cx + cy,))
                cps.append(_to_sibling(mine, mine, 4 * w + k, send_sems, recv_sems))
            cps.append(_to_sibling(refs[w], refs[n + w].at[2 * x + y], 4 * w + 3, send_sems, recv_sems))
        return cps, cps
    return plan


def _plan_swap_halves(n):
    def plan(refs, send_sems, recv_sems):
        c = lax.axis_index("c")
        cps = [_to_sibling(_half_rows(refs[w], 1 - c, (slice(None),)), refs[n + w], w, send_sems, recv_sems) for w in range(n)]
        return cps, cps
    return plan


def _plan_share_halves(n):
    def plan(refs, send_sems, recv_sems):
        c = lax.axis_index("c")
        cps = [_to_sibling(refs[w].at[c], refs[w].at[c], w, send_sems, recv_sems) for w in range(n)]
        return cps, cps
    return plan


def _plan_gather_small(refs, send_sems, recv_sems):
    x, y, c, _ = _place()
    flips = [(dx, dy, dc) for dx in (0, 1) for dy in (0, 1) for dc in (0, 1)][1:]
    flip = lambda v, d: 1 - v if d else v
    cps = [pltpu.make_async_remote_copy(src_ref=refs[0], dst_ref=refs[1].at[4 * x + 2 * y + c], send_sem=send_sems.at[k],
                                        recv_sem=recv_sems.at[k], device_id=(flip(x, dx), flip(y, dy), flip(c, dc)),
                                        device_id_type=MESH)
           for k, (dx, dy, dc) in enumerate(flips)]
    return cps, cps


def _sum_gathered_small(gathered, own, place, name):
    _, M, C = gathered.shape

    def body(place_ref, g_ref, own_ref, o_ref):
        me = 2 * place_ref[0] + place_ref[1]
        acc = jnp.zeros((M, C), F32)
        for k in range(8):
            acc = acc + jnp.where(me == k, own_ref[...], g_ref[k])
        o_ref[...] = acc

    spec = pltpu.PrefetchScalarGridSpec(
        num_scalar_prefetch=1, grid=(1,),
        in_specs=[pl.BlockSpec((8, M, C), lambda i, pr: (0, 0, 0)), pl.BlockSpec((M, C), lambda i, pr: (0, 0))],
        out_specs=pl.BlockSpec((M, C), lambda i, pr: (0, 0)))
    return _pcall(body, name=name, grid_spec=spec, out_shape=jax.ShapeDtypeStruct((M, C), F32),
                  compiler_params=_params(("arbitrary",)))(place, gathered, own)


def _allgather_group(shards, name):
    n = len(shards)
    per = 7

    def body(*refs):
        ins, outs = refs[:n], refs[n:2 * n]
        send_sems, recv_sems = refs[2 * n:]
        x, y, c, others = _place()
        me = 2 * x + y
        sibling = (x, y, 1 - c)

        def copy(w, k, src, dst, to):
            return pltpu.make_async_remote_copy(src_ref=src, dst_ref=dst, send_sem=send_sems.at[per * w + k],
                                                recv_sem=recv_sems.at[per * w + k], device_id=to, device_id_type=MESH)

        first = [copy(w, k, _half_rows(ins[w], c), _half_rows(outs[w], c, (me,)), (cx, cy, c))
                 for w in range(n) for k, (cx, cy) in enumerate(others)]
        own = [copy(w, 6, ins[w], outs[w].at[me], sibling) for w in range(n)]
        for cp in first + own:
            cp.start()
        passed = []
        for w in range(n):
            for k, (cx, cy) in enumerate(others):
                landed = _half_rows(outs[w], c, (2 * cx + cy,))
                copy(w, k, landed, landed, sibling).wait_recv()
                passed.append(copy(w, 3 + k, landed, landed, sibling))
                passed[-1].start()
        for w in range(n):
            for k, (cx, cy) in enumerate(others):
                theirs = _half_rows(outs[w], 1 - c, (2 * cx + cy,))
                copy(w, 3 + k, theirs, theirs, sibling).wait_recv()
            own[w].wait_recv()
        for cp in first + passed + own:
            cp.wait_send()

    return _pcall(body, name=name, in_specs=[HBM_SPEC] * n, out_specs=[HBM_SPEC] * n,
                  out_shape=[jax.ShapeDtypeStruct((N_CHIPS,) + s.shape, s.dtype) for s in shards],
                  scratch_shapes=_sem_arrays(per * n, per * n))(*shards)


def _swap_halves_group(arrs, name):
    n = len(arrs)

    def body(*refs):
        ins, gots = refs[:n], refs[n:2 * n]
        send_sems, recv_sems = refs[2 * n:]
        x, y, c, _ = _place()
        swaps = [pltpu.make_async_remote_copy(src_ref=_half_rows(ins[w], 1 - c, (slice(None),)), dst_ref=gots[w],
                                              send_sem=send_sems.at[w], recv_sem=recv_sems.at[w],
                                              device_id=(x, y, 1 - c), device_id_type=MESH) for w in range(n)]
        for cp in swaps:
            cp.start()
        for cp in swaps:
            cp.wait()

    half_shapes = [jax.ShapeDtypeStruct((a.shape[0], a.shape[1] // 2, a.shape[2]), a.dtype) for a in arrs]
    return _pcall(body, name=name, in_specs=[HBM_SPEC] * n, out_specs=[HBM_SPEC] * n, out_shape=half_shapes,
                  scratch_shapes=_sem_arrays(n, n))(*arrs)


def _scatter_group(parts, name):
    n = len(parts)

    def body(*refs):
        ins, outs = refs[:n], refs[n:2 * n]
        send_sems, recv_sems = refs[2 * n:]
        x, y, c, others = _place()

        def copy(w, k, src_chip, to):
            return pltpu.make_async_remote_copy(src_ref=ins[w].at[src_chip], dst_ref=outs[w].at[k],
                                                send_sem=send_sems.at[3 * w + k], recv_sem=recv_sems.at[3 * w + k],
                                                device_id=to, device_id_type=MESH)

        sends = [copy(w, k, 2 * cx + cy, (cx, cy, c)) for w in range(n) for k, (cx, cy) in enumerate(others)]
        for cp in sends:
            cp.start()
        for cp in sends:
            cp.wait()

    return _pcall(body, name=name, in_specs=[HBM_SPEC] * n, out_specs=[HBM_SPEC] * n,
                  out_shape=[jax.ShapeDtypeStruct((3,) + p.shape[1:], p.dtype) for p in parts],
                  scratch_shapes=_sem_arrays(3 * n, 3 * n))(*parts)


def _share_halves_group(halves, name):
    n = len(halves)

    def body(*refs):
        bufs = refs[n:2 * n]
        send_sems, recv_sems = refs[2 * n:]
        x, y, c, _ = _place()
        swaps = [pltpu.make_async_remote_copy(src_ref=bufs[w].at[c], dst_ref=bufs[w].at[c], send_sem=send_sems.at[w],
                                              recv_sem=recv_sems.at[w], device_id=(x, y, 1 - c), device_id_type=MESH)
                 for w in range(n)]
        for cp in swaps:
            cp.start()
        for w in range(n):
            swaps[w].wait_send()
            pltpu.make_async_remote_copy(src_ref=bufs[w].at[c], dst_ref=bufs[w].at[1 - c], send_sem=send_sems.at[w],
                                         recv_sem=recv_sems.at[w], device_id=(x, y, 1 - c), device_id_type=MESH).wait_recv()

    return _pcall(body, name=name, in_specs=[HBM_SPEC] * n, out_specs=[HBM_SPEC] * n,
                  out_shape=[jax.ShapeDtypeStruct(h.shape, h.dtype) for h in halves],
                  input_output_aliases={w: w for w in range(n)},
                  scratch_shapes=_sem_arrays(n, n))(*halves)


def _allgather_small(blk, name):
    M, C = blk.shape

    def body(x_ref, out_ref, send_sems, recv_sems, local_sem):
        x, y, c, others = _place()
        me, sibling = (x, y, c), (x, y, 1 - c)

        def rows(px, py, pc):
            return out_ref.at[4 * px + 2 * py + pc]

        def copy(k, block, to, src=None):
            return pltpu.make_async_remote_copy(src_ref=rows(*block) if src is None else src, dst_ref=rows(*block),
                                                send_sem=send_sems.at[k], recv_sem=recv_sems.at[k], device_id=to, device_id_type=MESH)

        mine = pltpu.make_async_copy(x_ref, rows(*me), local_sem)
        mine.start()
        first = [copy(0, me, sibling, src=x_ref)]
        first += [copy(1 + j, me, (*chip, c), src=x_ref) for j, chip in enumerate(others)]
        for cp in first:
            cp.start()
        passed = [copy(4 + j, (*chip, c), sibling) for j, chip in enumerate(others)]
        for j, chip in enumerate(others):
            copy(1 + j, (*chip, c), me).wait_recv()
            passed[j].start()
        copy(0, sibling, me).wait_recv()
        for j, chip in enumerate(others):
            copy(4 + j, (*chip, 1 - c), me).wait_recv()
        for cp in first + passed:
            cp.wait_send()
        mine.wait()

    return _pcall(body, name=name, in_specs=[VMEM_SPEC], out_specs=VMEM_SPEC,
                  out_shape=jax.ShapeDtypeStruct((8, M, C), blk.dtype),
                  scratch_shapes=[pltpu.SemaphoreType.DMA((7,)), pltpu.SemaphoreType.DMA((7,)), pltpu.SemaphoreType.DMA])(blk)


def _pack_rows(n_elems, width=PACK_W, align=PACK_ROW_ALIGN):
    rows = -(-n_elems // width)
    return -(-rows // align) * align


def _pack(arrays, dtype, width=PACK_W, align=PACK_ROW_ALIGN):
    flat = jnp.concatenate([a.astype(dtype).reshape(-1) for a in arrays])
    rows = _pack_rows(flat.shape[0], width, align)
    flat = jnp.pad(flat, (0, rows * width - flat.shape[0]))
    return flat.reshape(rows, width)


def _pack_small(arrays):
    return _pack(arrays, F32, width=128, align=8)


def _unpack(flat, shapes):
    out, off = [], 0
    for shp in shapes:
        n = int(np.prod(shp))
        out.append(flat[..., off:off + n].reshape(flat.shape[:-1] + tuple(shp)))
        off += n
    return out


def _doubled_heads(x2d, n_heads):
    S = x2d.shape[0]
    h = x2d.reshape(S, n_heads, HEAD_DIM).transpose(1, 0, 2)
    return jnp.concatenate([h, h], axis=-1)


def _residual_then_norms(acc, res, *gains):
    h = res + acc
    hn = h * _rinv(h)
    return (h,) + tuple(hn * g for g in gains)


def _mlp_fwd(h, n, w_up4, w_down, next_gains, tag):
    u, a = _matmul(n, w_up4, "nn", f"up{tag}", out_dtypes=(F32, BF16), chipwise="b", tm=2048, tn=512,
                   epilogue=lambda acc: (acc, jnp.square(jnp.maximum(acc, 0.0))))
    assert w_down.shape[1] == 1024
    outs = _matmul(a, w_down, "nn", f"down{tag}", out_dtypes=(F32,) + (BF16,) * len(next_gains), extras=(h,),
                   row_extras=tuple(next_gains), epilogue=_residual_then_norms, tm=1024, tn=1024, tk=1024)
    outs = outs if next_gains else (outs,)
    return outs[0], outs[1:], (n, u, a)


def _mlp_bwd(dh_out, dh_out_b, h, g, w_up4, w_down, saved, tag):
    n, u, a = saved
    dw_down = _matmul(a, dh_out_b, "tn", f"dw_down{tag}", out_dtypes=(BF16,), tm=2048, tn=512)
    du = _matmul(dh_out_b, w_down, "nt", f"du{tag}", out_dtypes=(BF16,), extras=(u,), tm=2048, tn=512,
                 epilogue=lambda acc, uu: (acc * (2.0 * jnp.maximum(uu, 0.0)),))
    dw_up = _matmul(n, du, "tn", f"dw_up{tag}", out_dtypes=(BF16,), chipwise="out", tm=1024, tn=512)
    dn = _matmul(du, w_up4, "nt", f"dn_mlp{tag}", tm=1024, tn=1024, tk=w_up4.shape[2], chipwise="b")
    dh, dh_b, (dg,) = _rms_bwd(h, dh_out, [g], [dn], f"rms_mlp_bwd{tag}")
    return dh, dh_b, dg, dw_up, dw_down


def kernel(x, g_attn, g_mlp, w_in_a, b_f, gq_a, gk_a, w_out_a, g_kv, w_kv, gk_b, w_q_b, gq_b, sinks, rel_bias, w_out_b, w_up, w_down, loss_target, m_g_attn, m_g_mlp, m_w_in_a, m_b_f, m_gq_a, m_gk_a, m_w_out_a, m_g_kv, m_w_kv, m_gk_b, m_w_q_b, m_gq_b, m_sinks, m_rel_bias, m_w_out_b, m_w_up, m_w_down, v_g_attn, v_g_mlp, v_w_in_a, v_b_f, v_gq_a, v_gk_a, v_w_out_a, v_g_kv, v_w_kv, v_gk_b, v_w_q_b, v_gq_b, v_sinks, v_rel_bias, v_w_out_b, v_w_up, v_w_down):
    given = dict(locals())
    S, D = x.shape[1], x.shape[2]
    H = D // HEAD_DIM
    KVH = w_kv.shape[1] // (2 * HEAD_DIM)
    kvw = KVH * HEAD_DIM
    hw = H * HEAD_DIM
    W = WINDOW
    nb = S // W
    c_idx = lax.axis_index("c")
    xs, tgt = x[0], loss_target[0]

    n_in_shard = w_in_a.shape[2]
    rows_in = -(-n_in_shard // 32) * 32
    row_pad = lambda a: jnp.pad(a, [(0, 0)] * (a.ndim - 2) + [(0, rows_in - a.shape[-2]), (0, 0)])
    t_in = lambda a: jnp.swapaxes(a[0], 0, 1)
    shards = {"w_in_a": row_pad(t_in(w_in_a)), "w_out_a": w_out_a[0], "w_up0": w_up[0], "w_down0": w_down[0], "w_kv": w_kv,
              "w_q_b": w_q_b[0], "w_out_b": w_out_b[0], "w_up1": w_up[1], "w_down1": w_down[1]}
    parts = list(shards)
    groups = [("w_in_a", "w_out_a"), ("w_up0", "w_down0"), ("w_kv", "w_q_b", "w_out_b", "w_up1", "w_down1")]
    started = []
    for i, grp in enumerate(groups):
        behind = started[-1][4] if started else g_attn[0]
        started.append(_gather_start([shards[n].astype(BF16) for n in grp], behind, f"gather_start{i}"))
    gathered = {}

    def finish_gather(i, after):
        srcs, lands = _gather_wait(started[i], after, f"gather_wait{i}")
        gathered.update(zip(groups[i], _gather_pass_on(srcs, lands, f"gather_pass_on{i}")))

    def land_gather(i, after):
        srcs, lands = _gather_wait(started[i], after, f"gather_wait{i}")
        n = len(srcs)
        passing, token = _split_start(_plan_pass_on(n), list(srcs) + list(lands), 4 * n, after, f"pass_on_start{i}")
        return passing, token[0:1, 0:1]

    def finish_pass_on(i, passing, after):
        n = len(groups[i])
        gathered.update(zip(groups[i], _split_wait(_plan_pass_on(n), passing, after, f"pass_on_wait{i}")[n:]))

    vec = lambda a: a.reshape(1, -1)
    twice = lambda a: jnp.tile(a.reshape(1, -1), (1, 2))

    g_attn0 = vec(g_attn[0]) + sum(st[4][0, 0] for st in started)
    (n0,) = _rms_fwd(xs, [g_attn0], "rms_attn0")
    finish_gather(0, n0)
    win_t = gathered["w_in_a"][:, :n_in_shard].reshape(-1, D)
    win_t = jnp.pad(win_t, ((0, (-win_t.shape[0]) % 128), (0, 0)))
    wout_a = gathered["w_out_a"].reshape(-1, D)
    n_in = win_t.shape[0]
    tile_in = 640 if n_in % 640 == 0 else 128
    proj = _matmul(n0, win_t, "nt", "proj_in", tm=2048, tn=tile_in)
    zt = proj[:, 3 * hw:3 * hw + H].T
    c_row = _gate_fwd(zt, b_f.reshape(H, 1), "gate_fwd")
    c_row3 = c_row.reshape(H, 1, S)
    o_a, lse_a = _fox_fwd(proj, c_row3, twice(gq_a[0]), twice(gk_a[0]), H, "fox_fwd")
    passing1, tie = land_gather(1, o_a)
    h1, n1 = _matmul(o_a, wout_a, "nn", "out_a", out_dtypes=(F32, BF16), extras=(xs,), row_extras=(vec(g_mlp[0]) + tie,),
                     epilogue=_residual_then_norms, tm=1024, tn=1024)
    finish_pass_on(1, passing1, n1)
    wup = [gathered["w_up0"], None]
    wdown = [gathered["w_down0"].reshape(-1, D), None]
    passing2, tie = land_gather(2, n1)
    h2, (nkv, n2), mlp0 = _mlp_fwd(h1, n1, wup[0], wdown[0], [vec(g_kv) + tie, vec(g_attn[1])], "0")

    finish_pass_on(2, passing2, h2)
    wq_b, wout_b = gathered["w_q_b"].reshape(-1, D), gathered["w_out_b"].reshape(-1, D)
    wkv = gathered["w_kv"].reshape(D, -1)
    wup[1], wdown[1] = gathered["w_up1"], gathered["w_down1"].reshape(-1, D)
    kv = _matmul(nkv, wkv, "nn", "proj_kv", tm=2048)
    kk, vv = _doubled_heads(kv[:, :kvw], KVH), _doubled_heads(kv[:, kvw:], KVH)
    q2 = _matmul(n2, wq_b, "nn", "proj_q", tm=1024, tn=1024)
    onehot = jnp.asarray(_bucket_onehot(), dtype=BF16)
    bias = _bias_expand(rel_bias.T, onehot, "bias_expand").reshape(H, W, 2 * W)
    bias_ab = bias.reshape(H // 2, 2 * W, 2 * W)
    bias_t_ab = bias.reshape(H // 2, 2, W, 2 * W).transpose(0, 3, 1, 2).reshape(H // 2, 2 * W, 2 * W)
    sink_ab = jnp.repeat(sinks[0].reshape(H // 2, 2), W, axis=1)
    o_b, lse_b = _swa_fwd(q2, kk, vv, bias_ab, sink_ab.reshape(H // 2, 2 * W, 1), twice(gq_b[0]), twice(gk_b), "swa_fwd")
    h3, n3 = _matmul(o_b, wout_b, "nn", "out_b", out_dtypes=(F32, BF16), extras=(h2,), row_extras=(vec(g_mlp[1]),),
                     epilogue=_residual_then_norms, tm=1024, tn=1024)
    h4, _, mlp1 = _mlp_fwd(h3, n3, wup[1], wdown[1], [], "1")

    dh4, dh4_b, loss_part = _loss_head(h4, tgt, "loss_head")

    place = jnp.stack([2 * lax.axis_index("x") + lax.axis_index("y"), c_idx]).astype(jnp.int32)
    scattering = []

    def pair_and_scatter(names, mine, got):
        pair_sums = [_sum_core_pair(a, g, place, "sum_core_pair_" + n) for n, a, g in zip(names, mine, got)]
        started_scatter, token = _scatter_start(pair_sums, "scatter_start_" + names[0])
        scattering.append((names, started_scatter))
        return token[0:1, :]

    def start_reduce(named):
        names = list(named)
        mine = [named[n] for n in names]
        return pair_and_scatter(names, mine, _swap_halves_group(mine, "swap_grad_halves_" + names[0]))

    def start_swap(named):
        names = list(named)
        mine = [named[n] for n in names]
        lands = [lax.empty((a.shape[0], a.shape[1] // 2, a.shape[2]), a.dtype) for a in mine]
        swapping, token = _split_start(_plan_swap_halves(len(mine)), mine + lands, len(mine), mine[0], "swap_start_" + names[0])
        return names, swapping, token

    def finish_swap(swap, after):
        names, swapping, _ = swap
        n = len(names)
        arrays = _split_wait(_plan_swap_halves(n), swapping, after, "swap_wait_" + names[0])
        return pair_and_scatter(names, arrays[:n], arrays[n:])

    dh3, dh3_b, dg_mlp1, dw_up1, dw_down1 = _mlp_bwd(dh4, dh4_b, h3, vec(g_mlp[1]), wup[1], wdown[1], mlp1, "1")
    swap1 = start_swap({"w_down1": dw_down1.reshape(N_CHIPS, -1, D), "w_up1": dw_up1})
    dw_out_b = _matmul(o_b, dh3_b, "tn", "dw_out_b", out_dtypes=(BF16,), tm=1024, tn=1024, after=swap1[2])
    do_b = _matmul(dh3_b, wout_b, "nt", "do_b", tm=1024, tn=1024)
    tie1 = finish_swap(swap1, do_b)
    dq2, dk2, dv2, dbias_t_ab, dsink, dgq_b, dgk_b = _swa_bwd(
        q2, kk, vv, bias_t_ab, sink_ab.reshape(H // 2, 1, 2 * W), twice(gq_b[0]) + tie1, twice(gk_b),
        lse_b, do_b, "swa_bwd")
    dbias = dbias_t_ab.reshape(H // 2, 2 * W, 2, W).transpose(0, 2, 3, 1).reshape(H, W * 2 * W)
    d_rel_bias = _bias_reduce(dbias, onehot, "bias_reduce").T
    dw_q_b = _matmul(n2, dq2, "tn", "dw_q_b", out_dtypes=(BF16,), tm=1024, tn=1024)
    dn2 = _matmul(dq2, wq_b, "nt", "dn2", tm=1024, tn=1024)
    dkv = jnp.concatenate([dk2[h, :, :HEAD_DIM] for h in range(KVH)] + [dv2[h, :, :HEAD_DIM] for h in range(KVH)],
                          axis=1).astype(BF16)
    dw_kv = _matmul(nkv, dkv, "tn", "dw_kv", out_dtypes=(BF16,), tm=1024)
    dnkv = _matmul(dkv, wkv, "nt", "dnkv", tm=1024, tn=1024)
    tie2 = start_reduce({"w_out_b": dw_out_b.reshape(N_CHIPS, -1, D), "w_q_b": dw_q_b.reshape(N_CHIPS, -1, D),
                         "w_kv": dw_kv.reshape(N_CHIPS, -1, 2 * kvw)})
    dh2, dh2_b, (dg_kv, dg_attn1) = _rms_bwd(h2, dh3, [vec(g_kv) + tie2[:, :1], vec(g_attn[1])], [dnkv, dn2], "rms_attn1_bwd")

    dh1, dh1_b, dg_mlp0, dw_up0, dw_down0 = _mlp_bwd(dh2, dh2_b, h1, vec(g_mlp[0]), wup[0], wdown[0], mlp0, "0")
    swap3 = start_swap({"w_down0": dw_down0.reshape(N_CHIPS, -1, D), "w_up0": dw_up0})
    dw_out_a = _matmul(o_a, dh1_b, "tn", "dw_out_a", out_dtypes=(BF16,), tm=1024, tn=1024, after=swap3[2])
    do_a = _matmul(dh1_b, wout_a, "nt", "do_a", tm=1024, tn=1024)
    tie3 = finish_swap(swap3, do_a)
    dq_a, dk_a, dv_a, dc_row, dgq_a, dgk_a = _fox_bwd(
        proj, c_row3, twice(gq_a[0]) + tie3, twice(gk_a[0]), lse_a, do_a, H, "fox_bwd")
    dzt, db_f = _gate_bwd(dc_row.reshape(H, S), zt, b_f.reshape(H, 1), "gate_bwd")
    dproj = jnp.concatenate([dq_a, dk_a, dv_a, dzt.T.astype(BF16), jnp.zeros((S, n_in - 3 * hw - H), BF16)], axis=1)
    dw_in_t = _matmul(dproj, n0, "tn", "dw_in", out_dtypes=(BF16,), tm=tile_in, tn=1024)
    dw_in4 = row_pad(dw_in_t[:3 * hw + H].reshape(N_CHIPS, -1, D))
    tie4 = start_reduce({"w_out_a": dw_out_a.reshape(N_CHIPS, -1, D), "w_in_a": dw_in4})
    dn0 = _matmul(dproj, win_t, "nn", "dn0", tm=1024, tn=1024, tk=tile_in)
    grad_x, _, (dg_attn0,) = _rms_bwd(xs, dh1, [vec(g_attn[0]) + tie4[:, :1]], [dn0], "rms_attn0_bwd")

    small_grads = {
        "g_attn": jnp.concatenate([dg_attn0, dg_attn1], axis=0), "g_mlp": jnp.concatenate([dg_mlp0, dg_mlp1], axis=0),
        "b_f": db_f.reshape(1, H), "gq_a": dgq_a[:, :HEAD_DIM], "gk_a": dgk_a[:, :HEAD_DIM], "g_kv": dg_kv.reshape(-1),
        "gk_b": dgk_b[0, :HEAD_DIM], "gq_b": dgq_b[:, :HEAD_DIM], "sinks": dsink[:, 0, 0].reshape(1, H), "rel_bias": d_rel_bias,
    }
    small_shapes = [given[n].shape for n in SMALL] + [(1,)]
    spack = _pack_small([small_grads[n] for n in SMALL] + [loss_part])
    gathering_small, token = _split_start(_plan_gather_small, [spack, lax.empty((8,) + spack.shape, F32)], 7, grad_x,
                                          "gather_small_start")
    sharing = []
    for names, started_scatter in scattering:
        pair_sums, landed = _scatter_wait(started_scatter, grad_x, "scatter_wait_" + names[0])
        halves = [_sum_chips(p, l, place, "sum_chips_" + n) for n, p, l in zip(names, pair_sums, landed)]
        started_share, token = _split_start(_plan_share_halves(len(halves)), halves, len(halves), token, "share_start_" + names[0])
        sharing.append((names, started_share))
    own_small, others_small = _split_wait(_plan_gather_small, gathering_small, token, "gather_small_wait")
    small_sum = _sum_gathered_small(others_small, own_small, place, "sum_small")
    small_red = _unpack(small_sum.reshape(-1), small_shapes)
    reduced = {}
    for names, started_share in sharing:
        for n, r in zip(names, _split_wait(_plan_share_halves(len(names)), started_share, token, "share_wait_" + names[0])):
            reduced[n] = r.reshape(-1, r.shape[2])
    reduced["w_in_a"] = reduced["w_in_a"][:n_in_shard]
    loss = small_red[-1][0]

    grads = dict(zip(SMALL, small_red))
    no_loss = [jnp.zeros((1,), F32)]
    sw = _pack_small([given[n] for n in SMALL] + no_loss)
    sm = _pack_small([given["m_" + n] for n in SMALL] + no_loss)
    sv = _pack_small([given["v_" + n] for n in SMALL] + no_loss)
    sd, sm2, sv2 = _adamw(sw, small_sum, sm, sv, "adamw_small", tr=sw.shape[0])
    delta = dict(zip(SMALL, _unpack(sd.reshape(-1), small_shapes)))
    new_m = dict(zip(SMALL, _unpack(sm2.reshape(-1), small_shapes)))
    new_v = dict(zip(SMALL, _unpack(sv2.reshape(-1), small_shapes)))
    for n in ("w_out_a", "w_kv", "w_q_b", "w_out_b"):
        w = given[n]
        two_d = (-1, w.shape[-1])
        d, m2, v2 = _adamw(w.reshape(two_d), reduced[n], given["m_" + n].reshape(two_d), given["v_" + n].reshape(two_d),
                           "adamw_" + n)
        grads[n] = reduced[n].reshape(w.shape)
        delta[n], new_m[n], new_v[n] = d.reshape(w.shape), m2.reshape(w.shape), v2.reshape(w.shape)
    d, m2, v2 = _adamw(t_in(w_in_a), reduced["w_in_a"], t_in(m_w_in_a), t_in(v_w_in_a), "adamw_w_in_a")
    back = lambda a: jnp.swapaxes(a, 0, 1)[None]
    grads["w_in_a"], delta["w_in_a"], new_m["w_in_a"], new_v["w_in_a"] = back(reduced["w_in_a"]), back(d), back(m2), back(v2)
    for n in ("w_up", "w_down"):
        w = given[n]
        two_d = (-1, w.shape[-1])
        g, d, m2, v2 = _adamw_two_layers(w.reshape(two_d), reduced[n + "0"], reduced[n + "1"], given["m_" + n].reshape(two_d),
                                         given["v_" + n].reshape(two_d), "adamw_" + n)
        grads[n], delta[n], new_m[n], new_v[n] = g.reshape(w.shape), d.reshape(w.shape), m2.reshape(w.shape), v2.reshape(w.shape)

    order = ["g_attn", "g_mlp", "w_in_a", "b_f", "gq_a", "gk_a", "w_out_a", "g_kv", "w_kv", "gk_b", "w_q_b", "gq_b",
             "sinks", "rel_bias", "w_out_b", "w_up", "w_down"]
    return (loss, grad_x[None], *[grads[n] for n in order], *[delta[n] for n in order],
            *[new_m[n] for n in order], *[new_v[n] for n in order])
```

```python
import numpy as np
import jax
import jax.numpy as jnp
from jax import lax
from jax.experimental import pallas as pl
from jax.experimental.pallas import tpu as pltpu

F32 = jnp.float32
BF16 = jnp.bfloat16
MESH = pl.DeviceIdType.MESH

HEAD_DIM = 64
LANES = 128
WINDOW = 128
N_BUCKETS = 32
REL_MAX_DIST = 128
NORM_EPS = 1e-6
ADAM_LR = 0.001
ADAM_B1 = 0.9
ADAM_B2 = 0.999
ADAM_EPS = 1e-08
ADAM_WD = 0.01
ADAM_STEP = 10
NEG = -1e30
N_CHIPS = 4
PACK_W = 1024
PACK_ROW_ALIGN = 256
VMEM_LIMIT = 56 * 1024 * 1024
HBM_SPEC = pl.BlockSpec(memory_space=pltpu.HBM)
VMEM_SPEC = pl.BlockSpec(memory_space=pltpu.VMEM)

BIG = (("w_in_a", 2), ("w_out_a", 1), ("w_kv", 0), ("w_q_b", 1), ("w_out_b", 1), ("w_up", 2), ("w_down", 1))
SMALL = ("g_attn", "g_mlp", "b_f", "gq_a", "gk_a", "g_kv", "gk_b", "gq_b", "sinks", "rel_bias")


def _pcall(body, **kw):
    return pl.pallas_call(body, **kw)


def _params(sem=None):
    return pltpu.CompilerParams(dimension_semantics=sem, vmem_limit_bytes=VMEM_LIMIT)


def _rinv(x):
    return lax.rsqrt(jnp.mean(x * x, axis=-1, keepdims=True) + NORM_EPS)


def _dot(a, b, dims, precision=None):
    return lax.dot_general(a, b, (dims, ((), ())), precision=precision, preferred_element_type=F32)


NN = ((1,), (0,))
NT = ((1,), (1,))
TN = ((0,), (0,))


def _accumulate(ref, val, first):
    @pl.when(first)
    def _():
        ref[...] = val

    @pl.when(jnp.logical_not(first))
    def _():
        ref[...] += val


def _matmul(a, b, mode, name, out_dtypes=(F32,), extras=(), row_extras=(), epilogue=None, tm=512, tn=512, tk=None, chipwise=None,
            after=None, row_accums=0):
    if chipwise == "b":
        nc = b.shape[2]
        M, K = a.shape
        (K2, N) = (b.shape[1], N_CHIPS * nc) if mode == "nn" else (N_CHIPS * nc, b.shape[1])
    elif mode == "nn":
        (M, K), (K2, N) = a.shape, b.shape
    elif mode == "nt":
        (M, K), (N, K2) = a.shape, b.shape
    else:
        (K, M), (K2, N) = a.shape, b.shape
    assert K == K2, (a.shape, b.shape, mode)
    tm, tn = min(tm, M), min(tn, N)
    tk = K if tk is None else tk
    assert M % tm == 0 and N % tn == 0 and K % tk == 0, (M, N, K, tm, tn, tk)
    nk = K // tk
    dims = {"nn": NN, "nt": NT, "tn": TN}[mode]
    a_spec = pl.BlockSpec((tk, tm), lambda i, j, k: (k, i)) if mode == "tn" else pl.BlockSpec((tm, tk), lambda i, j, k: (i, k))
    b_spec = pl.BlockSpec((tn, tk), lambda i, j, k: (j, k)) if mode == "nt" else pl.BlockSpec((tk, tn), lambda i, j, k: (k, j))
    o_spec = pl.BlockSpec((tm, tn), lambda i, j, k: (i, j))
    out_shape = (M, N)
    if chipwise == "b" and mode == "nn":
        per = nc // tn
        assert tk == K and nc % tn == 0
        b_spec = pl.BlockSpec((None, tk, tn), lambda i, j, k: (j // per, 0, j % per))
    elif chipwise == "b":
        assert mode == "nt" and tk == nc
        b_spec = pl.BlockSpec((None, tn, tk), lambda i, j, k: (k, j, 0))
    elif chipwise == "out":
        per = (N // N_CHIPS) // tn
        assert (N // N_CHIPS) % tn == 0
        o_spec = pl.BlockSpec((None, tm, tn), lambda i, j, k: (j // per, i, j % per))
        out_shape = (N_CHIPS, M, N // N_CHIPS)
        assert not extras
    n_ex, n_rex, n_out = len(extras), len(row_extras), len(out_dtypes)
    assert not row_accums or tn == N
    tail = () if after is None else (after,)
    n_in = 2 + n_ex + n_rex + len(tail)

    def body(*refs):
        a_ref, b_ref = refs[0], refs[1]
        ex_refs = refs[2:2 + n_ex + n_rex]
        out_refs = refs[n_in:n_in + n_out]
        part = _dot(a_ref[...].astype(BF16), b_ref[...].astype(BF16), dims)

        def finish(acc):
            outs = (acc,) if epilogue is None else epilogue(acc, *[r[...] for r in ex_refs])
            for idx, (r, o) in enumerate(zip(out_refs, outs)):
                if idx >= n_out - row_accums:
                    _accumulate(r, o, pl.program_id(0) == 0)
                else:
                    r[...] = o.astype(r.dtype)

        if nk == 1:
            finish(part)
            return
        acc_ref = refs[n_in + n_out]
        k = pl.program_id(2)

        @pl.when(k == 0)
        def _():
            acc_ref[...] = part

        @pl.when(jnp.logical_and(k > 0, k < nk - 1))
        def _():
            acc_ref[...] += part

        @pl.when(k == nk - 1)
        def _():
            finish(acc_ref[...] + part)

    row_spec = pl.BlockSpec((1, tn), lambda i, j, k: (0, j))
    outs = _pcall(
        body, name=name, grid=(M // tm, N // tn, nk),
        in_specs=[a_spec, b_spec] + [o_spec] * n_ex + [row_spec] * n_rex + [pl.BlockSpec(memory_space=pl.ANY)] * len(tail),
        out_specs=[o_spec] * (n_out - row_accums) + [row_spec] * row_accums,
        out_shape=[jax.ShapeDtypeStruct(out_shape, dt) for dt in out_dtypes[:n_out - row_accums]]
        + [jax.ShapeDtypeStruct((1, N), F32)] * row_accums,
        scratch_shapes=[pltpu.VMEM((tm, tn), F32)] if nk > 1 else [],
        compiler_params=_params(("arbitrary",) * 3 if row_accums else ("parallel", "parallel", "arbitrary")),
    )(a, b, *extras, *row_extras, *tail)
    return outs[0] if n_out == 1 else outs


def _rms_fwd(x, gains, name, ts=256):
    S, D = x.shape
    ts = min(ts, S)
    n = len(gains)

    def body(*refs):
        x_ref, g_refs, o_refs = refs[0], refs[1:1 + n], refs[1 + n:]
        xv = x_ref[...]
        xh = xv * _rinv(xv)
        for g_ref, o_ref in zip(g_refs, o_refs):
            o_ref[...] = (xh * g_ref[...]).astype(BF16)

    row = pl.BlockSpec((ts, D), lambda i: (i, 0))
    vec = pl.BlockSpec((1, D), lambda i: (0, 0))
    return _pcall(body, name=name, grid=(S // ts,), in_specs=[row] + [vec] * n, out_specs=[row] * n,
                  out_shape=[jax.ShapeDtypeStruct((S, D), BF16)] * n, compiler_params=_params(("parallel",)))(x, *gains)


def _rms_bwd(x, dres, gains, dns, name, ts=256):
    S, D = x.shape
    ts = min(ts, S)
    n = len(gains)

    def body(*refs):
        x_ref, dres_ref = refs[0], refs[1]
        g_refs, dn_refs = refs[2:2 + n], refs[2 + n:2 + 2 * n]
        dx_ref, dxb_ref, dg_refs = refs[2 + 2 * n], refs[3 + 2 * n], refs[4 + 2 * n:]
        xv = x_ref[...]
        r = _rinv(xv)
        xh = xv * r
        dx = dres_ref[...]
        first = pl.program_id(0) == 0
        for g_ref, dn_ref, dg_ref in zip(g_refs, dn_refs, dg_refs):
            dn = dn_ref[...].astype(F32)
            _accumulate(dg_ref, jnp.sum(dn * xh, axis=0, keepdims=True), first)
            dxh = dn * g_ref[...]
            dx = dx + r * (dxh - xh * jnp.mean(dxh * xh, axis=-1, keepdims=True))
        dx_ref[...] = dx
        dxb_ref[...] = dx.astype(BF16)

    row = pl.BlockSpec((ts, D), lambda i: (i, 0))
    vec = pl.BlockSpec((1, D), lambda i: (0, 0))
    outs = _pcall(body, name=name, grid=(S // ts,), in_specs=[row, row] + [vec] * n + [row] * n,
                  out_specs=[row, row] + [vec] * n,
                  out_shape=[jax.ShapeDtypeStruct((S, D), F32), jax.ShapeDtypeStruct((S, D), BF16)]
                  + [jax.ShapeDtypeStruct((1, D), F32)] * n,
                  compiler_params=_params(("arbitrary",)))(x, dres, *gains, *dns)
    return outs[0], outs[1], outs[2:]


def _loss_head(h, tgt, name, ts=256):
    S, D = h.shape
    ts = min(ts, S)

    def body(h_ref, t_ref, dh_ref, dhb_ref, loss_ref):
        err = h_ref[...] - t_ref[...]
        dh = err * (1.0 / D)
        dh_ref[...] = dh
        dhb_ref[...] = dh.astype(BF16)
        part = 0.5 * jnp.sum(jnp.mean(err * err, axis=-1, keepdims=True), axis=0, keepdims=True)
        _accumulate(loss_ref, part, pl.program_id(0) == 0)

    row = pl.BlockSpec((ts, D), lambda i: (i, 0))
    return _pcall(body, name=name, grid=(S // ts,), in_specs=[row, row],
                  out_specs=[row, row, pl.BlockSpec((1, 1), lambda i: (0, 0))],
                  out_shape=[jax.ShapeDtypeStruct((S, D), F32), jax.ShapeDtypeStruct((S, D), BF16),
                             jax.ShapeDtypeStruct((1, 1), F32)],
                  compiler_params=_params(("arbitrary",)))(h, tgt)


def _gate_fwd(zt, bf, name):
    H, S = zt.shape
    nb = S // 128

    def body(z_ref, b_ref, c_ref):
        z = z_ref[...] + b_ref[...]
        lf = jnp.minimum(z, 0.0) - jnp.log(1.0 + jnp.exp(-jnp.abs(z)))
        upper = (lax.broadcasted_iota(jnp.int32, (128, 128), 0) <= lax.broadcasted_iota(jnp.int32, (128, 128), 1)).astype(F32)
        carry = jnp.zeros((H, 1), F32)
        for blk in range(nb):
            cs = _dot(lf[:, blk * 128:(blk + 1) * 128], upper, NN, precision=lax.Precision.HIGHEST) + carry
            c_ref[:, blk * 128:(blk + 1) * 128] = cs
            carry = cs[:, 127:128]

    return _pcall(body, name=name, in_specs=[VMEM_SPEC, VMEM_SPEC], out_specs=VMEM_SPEC,
                  out_shape=jax.ShapeDtypeStruct((H, S), F32))(zt, bf)


def _gate_bwd(dct, zt, bf, name):
    H, S = zt.shape
    nb = S // 128

    def body(dc_ref, z_ref, b_ref, dz_ref, db_ref):
        z = z_ref[...] + b_ref[...]
        e = jnp.exp(-jnp.abs(z))
        sig_neg = jnp.where(z >= 0, e, 1.0) / (1.0 + e)
        lower = (lax.broadcasted_iota(jnp.int32, (128, 128), 0) >= lax.broadcasted_iota(jnp.int32, (128, 128), 1)).astype(F32)
        dc = dc_ref[...]
        carry = jnp.zeros((H, 1), F32)
        db = jnp.zeros((H, 1), F32)
        for blk in reversed(range(nb)):
            sl = slice(blk * 128, (blk + 1) * 128)
            dlf = _dot(dc[:, sl], lower, NN, precision=lax.Precision.HIGHEST) + carry
            carry = dlf[:, 0:1]
            dz = dlf * sig_neg[:, sl]
            dz_ref[:, sl] = dz
            db = db + jnp.sum(dz, axis=1, keepdims=True)
        db_ref[...] = db

    return _pcall(body, name=name, in_specs=[VMEM_SPEC] * 3, out_specs=[VMEM_SPEC] * 2,
                  out_shape=[jax.ShapeDtypeStruct((H, S), F32), jax.ShapeDtypeStruct((H, 1), F32)])(dct, zt, bf)


def _lane_is_a():
    return lax.broadcasted_iota(jnp.int32, (1, LANES), 1) < HEAD_DIM


def _per_head_mean(x, is_a):
    sa = jnp.sum(jnp.where(is_a, x, 0.0), axis=-1, keepdims=True)
    sb = jnp.sum(jnp.where(is_a, 0.0, x), axis=-1, keepdims=True)
    return jnp.where(is_a, sa, sb) / HEAD_DIM


def _pair_norm(raw, gain, is_a):
    return raw * lax.rsqrt(_per_head_mean(raw * raw, is_a) + NORM_EPS) * gain


def _pair_norm_bwd(raw, gain, dnormed, is_a):
    r = lax.rsqrt(_per_head_mean(raw * raw, is_a) + NORM_EPS)
    xh = raw * r
    dgain = jnp.sum(dnormed * xh, axis=0, keepdims=True)
    dxh = dnormed * gain
    return r * (dxh - xh * _per_head_mean(dxh * xh, is_a)), dgain


def _fold_heads(x):
    i = lax.broadcasted_iota(jnp.int32, (LANES, LANES), 0)
    j = lax.broadcasted_iota(jnp.int32, (LANES, LANES), 1)
    fold = ((i == j) | (i == j + HEAD_DIM) | (i + HEAD_DIM == j)).astype(F32)
    return _dot(x, fold, NN, precision=lax.Precision.HIGHEST)


def _fold_row(ref):
    ref[...] = _fold_heads(jnp.broadcast_to(ref[...], (8, LANES)))[0:1, :]


def _as_col(row):
    return jnp.broadcast_to(row, (LANES, row.shape[1])).T[:, 0:1]


def _as_row(col):
    return jnp.broadcast_to(col, (col.shape[0], LANES)).T[0:1, :]


def _tri_mask(t, keys_on_rows):
    r = lax.broadcasted_iota(jnp.int32, (t, t), 0)
    c = lax.broadcasted_iota(jnp.int32, (t, t), 1)
    return (r <= c) if keys_on_rows else (r >= c)


def _fox_fwd(proj, c_row, gq2, gk2, n_heads, name, t=256):
    S = proj.shape[0]
    H = n_heads
    P = H // 2
    t = min(t, S)
    nq = S // t

    def body(q_ref, k_ref, v_ref, cr_ref, gq_ref, gk_ref, o_ref, lse_ref, qs_s, kb_s, vb_s):
        is_a = _lane_is_a()
        qn = _pair_norm(q_ref[...], gq_ref[...], is_a) * 0.125
        qs_s[0] = jnp.where(is_a, qn, 0.0).astype(BF16)
        qs_s[1] = jnp.where(is_a, 0.0, qn).astype(BF16)
        kb_s[...] = _pair_norm(k_ref[...], gk_ref[...], is_a).astype(BF16)
        vb_s[...] = v_ref[...].astype(BF16)
        causal = _tri_mask(t, False)
        for i in range(nq):
            t0 = i * t
            rows = slice(t0, t0 + t)
            o_pair = None
            for a in range(2):
                qi = qs_s[a, rows, :]
                ci = _as_col(cr_ref[a, :, rows])
                s_d = jnp.where(causal, _dot(qi, kb_s[rows, :], NT) + ci - cr_ref[a, :, rows], NEG)
                m = jnp.max(s_d, axis=-1, keepdims=True)
                if i > 0:
                    s_l = _dot(qi, kb_s[0:t0, :], NT) + ci - cr_ref[a, :, 0:t0]
                    m = jnp.maximum(m, jnp.max(s_l, axis=-1, keepdims=True))
                p_d = jnp.exp(s_d - m)
                l = jnp.sum(p_d, axis=-1, keepdims=True)
                acc = _dot(p_d.astype(BF16), vb_s[rows, :], NN)
                if i > 0:
                    p_l = jnp.exp(s_l - m)
                    l = l + jnp.sum(p_l, axis=-1, keepdims=True)
                    acc = acc + _dot(p_l.astype(BF16), vb_s[0:t0, :], NN)
                o_a = acc / l
                lse_ref[a, :, rows] = _as_row(m + jnp.log(l))
                o_pair = o_a if a == 0 else jnp.where(is_a, o_pair, o_a)
            o_ref[rows, :] = o_pair.astype(BF16)

    def cols(off):
        return pl.BlockSpec((S, LANES), lambda p: (0, off + p))

    rowv = pl.BlockSpec((2, 1, S), lambda p: (p, 0, 0))
    gain = pl.BlockSpec((1, LANES), lambda p: (0, 0))
    return _pcall(body, name=name, grid=(P,), in_specs=[cols(0), cols(P), cols(2 * P), rowv, gain, gain],
                  out_specs=[cols(0), rowv],
                  out_shape=[jax.ShapeDtypeStruct((S, H * HEAD_DIM), BF16), jax.ShapeDtypeStruct((H, 1, S), F32)],
                  scratch_shapes=[pltpu.VMEM((2, S, LANES), BF16), pltpu.VMEM((S, LANES), BF16), pltpu.VMEM((S, LANES), BF16)],
                  compiler_params=_params(("parallel",)))(proj, proj, proj, c_row, gq2, gk2)


def _fox_bwd(proj, c_row, gq2, gk2, lse_row, do, n_heads, name, t=256):
    S = proj.shape[0]
    H = n_heads
    P = H // 2
    t = min(t, S)
    nq = S // t
    assert t % LANES == 0

    def body(q_ref, k_ref, v_ref, cr_ref, gq_ref, gk_ref, lr_ref, do_ref,
             dq_ref, dk_ref, dv_ref, dc_ref, dgq_ref, dgk_ref,
             qs_s, kb_s, kt_s, vb_s, dob_s, dq_s, dk_s, dv_s, dcs_s, cc_s):
        is_a = _lane_is_a()
        for a in range(2):
            for i in range(nq):
                cc_s[a, i * t:(i + 1) * t, :] = _as_col(cr_ref[a, :, i * t:(i + 1) * t])
        qn = _pair_norm(q_ref[...], gq_ref[...], is_a) * 0.125
        qs_s[0] = jnp.where(is_a, qn, 0.0).astype(BF16)
        qs_s[1] = jnp.where(is_a, 0.0, qn).astype(BF16)
        kn = _pair_norm(k_ref[...], gk_ref[...], is_a)
        kb_s[...] = kn.astype(BF16)
        kt_s[0] = jnp.where(is_a, kn, 0.0).T.astype(BF16)
        kt_s[1] = jnp.where(is_a, 0.0, kn).T.astype(BF16)
        vb_s[...] = v_ref[...].astype(BF16)
        dov = do_ref[...]
        dob_s[0] = jnp.where(is_a, dov, 0.0).astype(BF16)
        dob_s[1] = jnp.where(is_a, 0.0, dov).astype(BF16)
        dk_s[...] = jnp.zeros((S, LANES), F32)
        dv_s[...] = jnp.zeros((S, LANES), F32)
        dcs_s[...] = jnp.zeros((2, S, LANES), F32)
        causal = _tri_mask(t, True)
        for i in range(nq):
            t0 = i * t
            rows = slice(t0, t0 + t)
            dq_t = jnp.zeros((LANES, t), F32)
            for a in range(2):
                qi = qs_s[a, rows, :]
                doi = dob_s[a, rows, :]
                cri = cr_ref[a, :, rows]
                lri = lr_ref[a, :, rows]

                def probs(keys, masked, a=a, qi=qi, doi=doi, cri=cri, lri=lri):
                    p_t = jnp.exp(_dot(kb_s[keys, :], qi, NT) + cri - cc_s[a, keys, :] - lri)
                    if masked:
                        p_t = jnp.where(causal, p_t, 0.0)
                    return p_t, _dot(vb_s[keys, :], doi, NT)

                parts = [(rows,) + probs(rows, True)]
                if i > 0:
                    parts.append((slice(0, t0),) + probs(slice(0, t0), False))
                delta = sum(jnp.sum(p_t * dp_t, axis=0, keepdims=True) for _, p_t, dp_t in parts)
                for keys, p_t, dp_t in parts:
                    ds_t = p_t * (dp_t - delta)
                    dsb = ds_t.astype(BF16)
                    dv_s[keys, :] += _dot(p_t.astype(BF16), doi, NN)
                    dk_s[keys, :] += _dot(dsb, qi, NN)
                    dq_t = dq_t + _dot(kt_s[a, :, keys], dsb, NN)
                    dcs_s[a, keys, :] += sum(ds_t[:, b * LANES:(b + 1) * LANES] for b in range(t // LANES))
            dq_s[rows, :] = dq_t.T
        first = pl.program_id(0) == 0
        last = pl.program_id(0) == P - 1
        dq_raw, dgq = _pair_norm_bwd(q_ref[...], gq_ref[...], dq_s[...] * 0.125, is_a)
        dq_ref[...] = dq_raw.astype(BF16)
        _accumulate(dgq_ref, dgq, first)
        dk_raw, dgk = _pair_norm_bwd(k_ref[...], gk_ref[...], dk_s[...], is_a)
        dk_ref[...] = dk_raw.astype(BF16)
        _accumulate(dgk_ref, dgk, first)
        dv_ref[...] = dv_s[...].astype(BF16)
        for a in range(2):
            for i in range(nq):
                rows = slice(i * t, (i + 1) * t)
                dc_ref[a, :, rows] = _as_row(-jnp.sum(dcs_s[a, rows, :], axis=1, keepdims=True))

        @pl.when(last)
        def _():
            _fold_row(dgq_ref)
            _fold_row(dgk_ref)

    def cols(off):
        return pl.BlockSpec((S, LANES), lambda p: (0, off + p))

    rowv = pl.BlockSpec((2, 1, S), lambda p: (p, 0, 0))
    gain = pl.BlockSpec((1, LANES), lambda p: (0, 0))
    wide = jax.ShapeDtypeStruct((S, H * HEAD_DIM), BF16)
    gs = jax.ShapeDtypeStruct((1, LANES), F32)
    return _pcall(body, name=name, grid=(P,),
                  in_specs=[cols(0), cols(P), cols(2 * P), rowv, gain, gain, rowv, cols(0)],
                  out_specs=[cols(0), cols(0), cols(0), rowv, gain, gain],
                  out_shape=[wide, wide, wide, jax.ShapeDtypeStruct((H, 1, S), F32), gs, gs],
                  scratch_shapes=[pltpu.VMEM((2, S, LANES), BF16), pltpu.VMEM((S, LANES), BF16), pltpu.VMEM((2, LANES, S), BF16),
                                  pltpu.VMEM((S, LANES), BF16), pltpu.VMEM((2, S, LANES), BF16)]
                  + [pltpu.VMEM((S, LANES), F32)] * 3 + [pltpu.VMEM((2, S, LANES), F32), pltpu.VMEM((2, S, 1), F32)],
                  compiler_params=_params(("arbitrary",)))(proj, proj, proj, c_row, gq2, gk2, lse_row, do)


def _bucket_onehot():
    W = WINDOW
    dist = np.arange(W)[:, None] + W - np.arange(2 * W)[None, :]
    n = np.maximum(dist, 0)
    max_exact = N_BUCKETS // 2
    large = max_exact + (np.log(np.maximum(n, 1) / max_exact) / np.log(REL_MAX_DIST / max_exact)
                         * (N_BUCKETS - max_exact)).astype(np.int32)
    large = np.minimum(large, N_BUCKETS - 1)
    bucket = np.where(n < max_exact, n, large).astype(np.int32)
    valid = (dist >= 0) & (dist < W)
    onehot = (bucket[None] == np.arange(N_BUCKETS)[:, None, None]) & valid[None]
    return onehot.reshape(N_BUCKETS, W * 2 * W).astype(np.float32)


def _bias_expand(rel_bias_t, onehot, name, tn=4096):
    HQ, NB = rel_bias_t.shape
    L = onehot.shape[1]

    def body(r_ref, oh_ref, out_ref):
        out_ref[...] = _dot(r_ref[...], oh_ref[...].astype(F32), NN, precision=lax.Precision.HIGHEST)

    return _pcall(body, name=name, grid=(L // tn,),
                  in_specs=[pl.BlockSpec((HQ, NB), lambda i: (0, 0)), pl.BlockSpec((NB, tn), lambda i: (0, i))],
                  out_specs=pl.BlockSpec((HQ, tn), lambda i: (0, i)),
                  out_shape=jax.ShapeDtypeStruct((HQ, L), F32), compiler_params=_params(("parallel",)))(rel_bias_t, onehot)


def _bias_reduce(dbias, onehot, name, tk=4096):
    HQ, L = dbias.shape
    NB = onehot.shape[0]

    def body(d_ref, oh_ref, out_ref):
        part = _dot(d_ref[...], oh_ref[...].astype(F32), NT, precision=lax.Precision.HIGHEST)
        _accumulate(out_ref, part, pl.program_id(0) == 0)

    return _pcall(body, name=name, grid=(L // tk,),
                  in_specs=[pl.BlockSpec((HQ, tk), lambda i: (0, i)), pl.BlockSpec((NB, tk), lambda i: (0, i))],
                  out_specs=pl.BlockSpec((HQ, NB), lambda i: (0, 0)),
                  out_shape=jax.ShapeDtypeStruct((HQ, NB), F32), compiler_params=_params(("arbitrary",)))(dbias, onehot)


def _stacked_query_index(n_rows_or_cols_axis, shape):
    idx = lax.broadcasted_iota(jnp.int32, shape, n_rows_or_cols_axis)
    return jnp.where(idx >= WINDOW, idx - WINDOW, idx)


def _swa_fwd(qproj, kk, vv, bias_ab, sink_col, gq2, gk2, name):
    S, HQD = qproj.shape
    KVH = kk.shape[0]
    PP = HQD // LANES
    NP = PP // KVH
    W = WINDOW
    nb = S // W

    def body(q_ref, k_ref, v_ref, bias_ref, sink_ref, gq_ref, gk_ref, o_ref, lse_ref, qs_s, kb_s, vb_s):
        is_a = _lane_is_a()
        qn = _pair_norm(q_ref[...], gq_ref[...], is_a) * 0.125
        qs_s[0] = jnp.where(is_a, qn, 0.0).astype(BF16)
        qs_s[1] = jnp.where(is_a, 0.0, qn).astype(BF16)
        kb_s[...] = _pair_norm(k_ref[...], gk_ref[...], is_a).astype(BF16)
        vb_s[...] = v_ref[...].astype(BF16)
        sink = sink_ref[...]
        qi1 = _stacked_query_index(0, (2 * W, W))
        first_valid = lax.broadcasted_iota(jnp.int32, (2 * W, W), 1) <= qi1
        qi2 = _stacked_query_index(0, (2 * W, 2 * W))
        key2 = lax.broadcasted_iota(jnp.int32, (2 * W, 2 * W), 1)
        band_valid = (key2 > qi2) & (key2 <= qi2 + W)
        for n in range(nb):
            rows = slice(n * W, (n + 1) * W)
            keys = slice(0, W) if n == 0 else slice((n - 1) * W, (n + 1) * W)
            lhs = jnp.concatenate([qs_s[0, rows, :], qs_s[1, rows, :]], axis=0)
            s = _dot(lhs, kb_s[keys, :], NT) + (bias_ref[:, W:2 * W] if n == 0 else bias_ref[...])
            s = jnp.where(first_valid if n == 0 else band_valid, s, NEG)
            m = jnp.maximum(jnp.max(s, axis=-1, keepdims=True), sink)
            e = jnp.exp(s - m)
            l = jnp.sum(e, axis=-1, keepdims=True) + jnp.exp(sink - m)
            o_ab = _dot(e.astype(BF16), vb_s[keys, :], NN) / l
            o_ref[rows, :] = jnp.where(is_a, o_ab[0:W, :], o_ab[W:2 * W, :]).astype(BF16)
            lse_ref[n] = _as_row(m + jnp.log(l))

    qcols = pl.BlockSpec((S, LANES), lambda a, g: (0, a * NP + g))
    kvs = pl.BlockSpec((None, S, LANES), lambda a, g: (a, 0, 0))
    gain = pl.BlockSpec((1, LANES), lambda a, g: (0, 0))
    return _pcall(body, name=name, grid=(KVH, NP),
                  in_specs=[qcols, kvs, kvs, pl.BlockSpec((None, 2 * W, 2 * W), lambda a, g: (a * NP + g, 0, 0)),
                            pl.BlockSpec((None, 2 * W, 1), lambda a, g: (a * NP + g, 0, 0)), gain, gain],
                  out_specs=[qcols, pl.BlockSpec((None, nb, 1, 2 * W), lambda a, g: (a * NP + g, 0, 0, 0))],
                  out_shape=[jax.ShapeDtypeStruct((S, HQD), BF16), jax.ShapeDtypeStruct((PP, nb, 1, 2 * W), F32)],
                  scratch_shapes=[pltpu.VMEM((2, S, LANES), BF16), pltpu.VMEM((S, LANES), BF16), pltpu.VMEM((S, LANES), BF16)],
                  compiler_params=_params(("parallel", "parallel")))(qproj, kk, vv, bias_ab, sink_col, gq2, gk2)


def _swa_bwd(qproj, kk, vv, bias_t_ab, sink_row, gq2, gk2, lse_row, do, name):
    S, HQD = qproj.shape
    KVH = kk.shape[0]
    PP = HQD // LANES
    NP = PP // KVH
    W = WINDOW
    nb = S // W

    def body(q_ref, k_ref, v_ref, bias_ref, sink_ref, gq_ref, gk_ref, lr_ref, do_ref,
             dq_ref, dk_ref, dv_ref, db_ref, dsink_ref, dgq_ref, dgk_ref,
             qs_s, kb_s, kt_s, vb_s, dob_s, dq_s, dk_s, dv_s):
        a, g = pl.program_id(0), pl.program_id(1)
        is_a = _lane_is_a()
        qn = _pair_norm(q_ref[...], gq_ref[...], is_a) * 0.125
        qs_s[0] = jnp.where(is_a, qn, 0.0).astype(BF16)
        qs_s[1] = jnp.where(is_a, 0.0, qn).astype(BF16)
        kn = _pair_norm(k_ref[...], gk_ref[...], is_a)
        kb_s[...] = kn.astype(BF16)
        kt_s[...] = kn.T.astype(BF16)
        vb_s[...] = v_ref[...].astype(BF16)
        dov = do_ref[...]
        dob_s[0] = jnp.where(is_a, dov, 0.0).astype(BF16)
        dob_s[1] = jnp.where(is_a, 0.0, dov).astype(BF16)
        sink = sink_ref[...]

        @pl.when(g == 0)
        def _():
            dk_s[...] = jnp.zeros((S, LANES), F32)
            dv_s[...] = jnp.zeros((S, LANES), F32)

        qi1 = _stacked_query_index(1, (W, 2 * W))
        first_valid = lax.broadcasted_iota(jnp.int32, (W, 2 * W), 0) <= qi1
        qi2 = _stacked_query_index(1, (2 * W, 2 * W))
        key2 = lax.broadcasted_iota(jnp.int32, (2 * W, 2 * W), 0)
        band_valid = (key2 > qi2) & (key2 <= qi2 + W)
        head_rows = lax.broadcasted_iota(jnp.int32, (LANES, W), 0) < HEAD_DIM
        db = jnp.zeros((2 * W, 2 * W), F32)
        dsk = jnp.zeros((1, 2 * W), F32)
        pend_k = pend_v = None
        for n in range(nb):
            rows = slice(n * W, (n + 1) * W)
            keys = slice(0, W) if n == 0 else slice((n - 1) * W, (n + 1) * W)
            lhs_q = jnp.concatenate([qs_s[0, rows, :], qs_s[1, rows, :]], axis=0)
            lhs_do = jnp.concatenate([dob_s[0, rows, :], dob_s[1, rows, :]], axis=0)
            lse = lr_ref[n]
            s_t = _dot(kb_s[keys, :], lhs_q, NT) + (bias_ref[W:2 * W, :] if n == 0 else bias_ref[...])
            p_t = jnp.where(first_valid if n == 0 else band_valid, jnp.exp(s_t - lse), 0.0)
            dp_t = _dot(vb_s[keys, :], lhs_do, NT)
            delta = jnp.sum(p_t * dp_t, axis=0, keepdims=True)
            ds_t = p_t * (dp_t - delta)
            dsb = ds_t.astype(BF16)
            dsk = dsk - jnp.exp(sink - lse) * delta
            dv_band = _dot(p_t.astype(BF16), lhs_do, NN)
            dk_band = _dot(dsb, lhs_q, NN)
            dq_t = _dot(kt_s[:, keys], dsb, NN)
            dq_s[rows, :] = jnp.where(head_rows, dq_t[:, 0:W], dq_t[:, W:2 * W]).T
            if n == 0:
                db = jnp.concatenate([jnp.zeros((W, 2 * W), F32), ds_t], axis=0)
                pend_k, pend_v = dk_band, dv_band
            else:
                db = db + ds_t
                prev = slice((n - 1) * W, n * W)
                dk_s[prev, :] += pend_k + dk_band[0:W, :]
                dv_s[prev, :] += pend_v + dv_band[0:W, :]
                pend_k, pend_v = dk_band[W:2 * W, :], dv_band[W:2 * W, :]
        tail = slice((nb - 1) * W, nb * W)
        dk_s[tail, :] += pend_k
        dv_s[tail, :] += pend_v
        db_ref[...] = db
        dsink_ref[0] = jnp.broadcast_to(jnp.sum(dsk[:, 0:W], axis=1, keepdims=True), (1, LANES))
        dsink_ref[1] = jnp.broadcast_to(jnp.sum(dsk[:, W:2 * W], axis=1, keepdims=True), (1, LANES))
        dq_raw, dgq = _pair_norm_bwd(q_ref[...], gq_ref[...], dq_s[...] * 0.125, is_a)
        dq_ref[...] = dq_raw.astype(BF16)
        _accumulate(dgq_ref, dgq, jnp.logical_and(a == 0, g == 0))

        @pl.when(jnp.logical_and(a == KVH - 1, g == NP - 1))
        def _():
            _fold_row(dgq_ref)

        @pl.when(g == NP - 1)
        def _():
            dk_raw, dgk = _pair_norm_bwd(k_ref[...], gk_ref[...], _fold_heads(dk_s[...]), is_a)
            dk_ref[...] = dk_raw
            _accumulate(dgk_ref, dgk, a == 0)
            dv_ref[...] = _fold_heads(dv_s[...])

    qcols = pl.BlockSpec((S, LANES), lambda a, g: (0, a * NP + g))
    kvs = pl.BlockSpec((None, S, LANES), lambda a, g: (a, 0, 0))
    sq = pl.BlockSpec((None, 2 * W, 2 * W), lambda a, g: (a * NP + g, 0, 0))
    gain = pl.BlockSpec((1, LANES), lambda a, g: (0, 0))
    ks = jax.ShapeDtypeStruct((KVH, S, LANES), F32)
    gs = jax.ShapeDtypeStruct((1, LANES), F32)
    return _pcall(body, name=name, grid=(KVH, NP),
                  in_specs=[qcols, kvs, kvs, sq, pl.BlockSpec((None, 1, 2 * W), lambda a, g: (a * NP + g, 0, 0)), gain, gain,
                            pl.BlockSpec((None, nb, 1, 2 * W), lambda a, g: (a * NP + g, 0, 0, 0)), qcols],
                  out_specs=[qcols, kvs, kvs, sq, pl.BlockSpec((2, 1, LANES), lambda a, g: (a * NP + g, 0, 0)), gain, gain],
                  out_shape=[jax.ShapeDtypeStruct((S, HQD), BF16), ks, ks, jax.ShapeDtypeStruct((PP, 2 * W, 2 * W), F32),
                             jax.ShapeDtypeStruct((2 * PP, 1, LANES), F32), gs, gs],
                  scratch_shapes=[pltpu.VMEM((2, S, LANES), BF16), pltpu.VMEM((S, LANES), BF16), pltpu.VMEM((LANES, S), BF16),
                                  pltpu.VMEM((S, LANES), BF16), pltpu.VMEM((2, S, LANES), BF16)] + [pltpu.VMEM((S, LANES), F32)] * 3,
                  compiler_params=_params(("arbitrary", "arbitrary")))(qproj, kk, vv, bias_t_ab, sink_row, gq2, gk2, lse_row, do)


def _adamw_update(w, g, m, v):
    m2 = ADAM_B1 * m + (1.0 - ADAM_B1) * g
    v2 = ADAM_B2 * v + (1.0 - ADAM_B2) * jnp.square(g)
    m_hat = m2 / (1.0 - ADAM_B1 ** ADAM_STEP)
    v_hat = v2 / (1.0 - ADAM_B2 ** ADAM_STEP)
    return -ADAM_LR * (m_hat / (jnp.sqrt(v_hat) + ADAM_EPS) + ADAM_WD * w), m2, v2


def _adamw(w, g, m, v, name, tr=256, tc=256):
    R, C = w.shape
    tr = min(tr, R)
    if R % tr == 0:
        grid, blk = (R // tr,), pl.BlockSpec((tr, C), lambda i: (i, 0))
    else:
        assert C % tc == 0
        grid, blk = (C // tc,), pl.BlockSpec((R, tc), lambda i: (0, i))

    def body(w_ref, g_ref, m_ref, v_ref, d_ref, m2_ref, v2_ref):
        d_ref[...], m2_ref[...], v2_ref[...] = _adamw_update(w_ref[...], g_ref[...], m_ref[...], v_ref[...])

    return _pcall(body, name=name, grid=grid, in_specs=[blk] * 4, out_specs=[blk] * 3,
                  out_shape=[jax.ShapeDtypeStruct((R, C), F32)] * 3, compiler_params=_params(("parallel",)))(w, g, m, v)


def _adamw_two_layers(w, g0, g1, m, v, name, tr=256):
    R, C = g0.shape
    assert R % tr == 0 and w.shape == (2 * R, C)
    nr = R // tr

    def body(w_ref, g0_ref, g1_ref, m_ref, v_ref, g_ref, d_ref, m2_ref, v2_ref):
        g = jnp.where(pl.program_id(0) == 0, g0_ref[...], g1_ref[...])
        g_ref[...] = g
        d_ref[...], m2_ref[...], v2_ref[...] = _adamw_update(w_ref[...], g, m_ref[...], v_ref[...])

    both = pl.BlockSpec((tr, C), lambda l, i: (l * nr + i, 0))
    first = pl.BlockSpec((tr, C), lambda l, i: (i * (1 - l) + (nr - 1) * l, 0))
    second = pl.BlockSpec((tr, C), lambda l, i: (i * l, 0))
    return _pcall(body, name=name, grid=(2, nr), in_specs=[both, first, second, both, both], out_specs=[both] * 4,
                  out_shape=[jax.ShapeDtypeStruct((2 * R, C), F32)] * 4,
                  compiler_params=_params(("arbitrary", "arbitrary")))(w, g0, g1, m, v)


def _sum_core_pair(arr, got, place, name, tr=512):
    P, hr, C = got.shape
    tr = tr if hr % tr == 0 else hr
    nt = hr // tr

    def body(place_ref, a_ref, g_ref, o_ref):
        o_ref[...] = (a_ref[...].astype(F32) + g_ref[...].astype(F32)).astype(o_ref.dtype)

    spec = pltpu.PrefetchScalarGridSpec(
        num_scalar_prefetch=1, grid=(P, nt),
        in_specs=[pl.BlockSpec((None, tr, C), lambda j, i, pr: (j, pr[1] * nt + i, 0)),
                  pl.BlockSpec((None, tr, C), lambda j, i, pr: (j, i, 0))],
        out_specs=pl.BlockSpec((None, tr, C), lambda j, i, pr: (j, i, 0)))
    return _pcall(body, name=name, grid_spec=spec, out_shape=jax.ShapeDtypeStruct(got.shape, BF16),
                  compiler_params=_params(("parallel", "parallel")))(place, arr, got)


def _sum_chips(pair, landed, place, name, tr=256):
    _, R, C = landed.shape
    tr = tr if R % tr == 0 else R

    def body(place_ref, p_ref, l_ref, o_ref):
        acc = p_ref[...].astype(F32)
        for k in range(3):
            acc = acc + l_ref[k].astype(F32)
        o_ref[...] = acc

    spec = pltpu.PrefetchScalarGridSpec(
        num_scalar_prefetch=1, grid=(R // tr,),
        in_specs=[pl.BlockSpec((None, tr, C), lambda i, pr: (pr[0], i, 0)), pl.BlockSpec((3, tr, C), lambda i, pr: (0, i, 0))],
        out_specs=pl.BlockSpec((None, tr, C), lambda i, pr: (pr[1], i, 0)))
    return _pcall(body, name=name, grid_spec=spec, out_shape=jax.ShapeDtypeStruct((2, R, C), F32),
                  compiler_params=_params(("parallel",)))(place, pair, landed)


def _sum_parts(parts, name, out_dtype, tr=128):
    P, R, C = parts.shape
    tr = min(tr, R)
    assert R % tr == 0, (R, tr)

    def body(p_ref, o_ref):
        acc = p_ref[0].astype(F32)
        for k in range(1, P):
            acc = acc + p_ref[k].astype(F32)
        o_ref[...] = acc.astype(o_ref.dtype)

    return _pcall(body, name=name, grid=(R // tr,), in_specs=[pl.BlockSpec((P, tr, C), lambda i: (0, i, 0))],
                  out_specs=pl.BlockSpec((tr, C), lambda i: (i, 0)),
                  out_shape=jax.ShapeDtypeStruct((R, C), out_dtype), compiler_params=_params(("parallel",)))(parts)


def _place():
    x, y, c = lax.axis_index("x"), lax.axis_index("y"), lax.axis_index("c")
    others = [(1 - x, y), (x, 1 - y), (1 - x, 1 - y)]
    return x, y, c, others


def _half_rows(ref, hh, lead=()):
    hr = ref.shape[-2] // 2
    return ref.at[(*lead, pl.ds(pl.multiple_of(hh * hr, 16), hr), slice(None))]


def _sem_arrays(*counts):
    return [pltpu.SemaphoreType.DMA((k,)) for k in counts]


SEM_SPEC = pl.BlockSpec(memory_space=pltpu.SEMAPHORE)
ANY_SPEC = pl.BlockSpec(memory_space=pl.ANY)
DATAFLOW = pltpu.SideEffectType.DATAFLOW_SIDE_EFFECTING


def _in_hbm(a):
    return pltpu.with_memory_space_constraint(a, pltpu.HBM)


def _gather_copies(srcs, lands, send_sems, recv_sems):
    x, y, c, others = _place()
    me = 2 * x + y

    def copy(w, k, dst_chip, to):
        return pltpu.make_async_remote_copy(src_ref=_half_rows(srcs[w], c), dst_ref=_half_rows(lands[w], c, (dst_chip,)),
                                            send_sem=send_sems.at[3 * w + k], recv_sem=recv_sems.at[3 * w + k],
                                            device_id=to, device_id_type=MESH)

    pairs = [(w, k, cx, cy) for w in range(len(srcs)) for k, (cx, cy) in enumerate(others)]
    return ([copy(w, k, me, (cx, cy, c)) for w, k, cx, cy in pairs],
            [copy(w, k, 2 * cx + cy, (cx, cy, c)) for w, k, cx, cy in pairs])


def _gather_start(shards, after, name):
    n = len(shards)

    def body(*refs):
        srcs, lands, send_sems, recv_sems, token = refs[:n], refs[n:2 * n], refs[2 * n + 1], refs[2 * n + 2], refs[-1]
        for cp in _gather_copies(srcs, lands, send_sems, recv_sems)[0]:
            cp.start()
        token[...] = jnp.zeros_like(token)

    lands = [lax.empty((N_CHIPS,) + s.shape, s.dtype) for s in shards]
    outs = _pcall(
        body, name=name, in_specs=[HBM_SPEC] * (2 * n) + [ANY_SPEC],
        out_specs=[SEM_SPEC, SEM_SPEC] + [HBM_SPEC] * (2 * n) + [VMEM_SPEC],
        out_shape=[pltpu.SemaphoreType.DMA((3 * n,)), pltpu.SemaphoreType.DMA((3 * n,))]
        + [pltpu.HBM(a.shape, a.dtype) for a in list(shards) + lands] + [jax.ShapeDtypeStruct((8, LANES), F32)],
        input_output_aliases={i: 2 + i for i in range(2 * n)},
        compiler_params=pltpu.CompilerParams(has_side_effects=DATAFLOW),
    )(*[_in_hbm(a) for a in list(shards) + lands], after)
    return outs[0], outs[1], outs[2:2 + n], outs[2 + n:2 + 2 * n], outs[-1]


def _gather_wait(started, after, name):
    send_sems, recv_sems, srcs, lands, _ = started
    n = len(srcs)

    def body(*refs):
        src_refs, land_refs, send_ref, recv_ref = refs[:n], refs[n:2 * n], refs[2 * n], refs[2 * n + 1]
        outgoing, incoming = _gather_copies(src_refs, land_refs, send_ref, recv_ref)
        for out_cp, in_cp in zip(outgoing, incoming):
            out_cp.wait_send()
            in_cp.wait_recv()

    outs = _pcall(
        body, name=name, in_specs=[HBM_SPEC] * (2 * n) + [SEM_SPEC, SEM_SPEC, ANY_SPEC], out_specs=[HBM_SPEC] * (2 * n),
        out_shape=[pltpu.HBM(a.shape, a.dtype) for a in list(srcs) + list(lands)],
        input_output_aliases={i: i for i in range(2 * n)},
        compiler_params=pltpu.CompilerParams(has_side_effects=DATAFLOW),
    )(*srcs, *lands, send_sems, recv_sems, after)
    return outs[:n], outs[n:]


def _gather_pass_on(shards, lands, name):
    n = len(shards)
    per = 4

    def body(*refs):
        srcs, bufs = refs[:n], refs[2 * n:3 * n]
        send_sems, recv_sems = refs[3 * n:]
        x, y, c, others = _place()
        me = 2 * x + y
        sibling = (x, y, 1 - c)

        def copy(w, k, src, dst):
            return pltpu.make_async_remote_copy(src_ref=src, dst_ref=dst, send_sem=send_sems.at[per * w + k],
                                                recv_sem=recv_sems.at[per * w + k], device_id=sibling, device_id_type=MESH)

        sends, recvs = [], []
        for w in range(n):
            for k, (cx, cy) in enumerate(others):
                mine, theirs = _half_rows(bufs[w], c, (2 * cx + cy,)), _half_rows(bufs[w], 1 - c, (2 * cx + cy,))
                sends.append(copy(w, k, mine, mine))
                recvs.append(copy(w, k, theirs, theirs))
            sends.append(copy(w, 3, srcs[w], bufs[w].at[me]))
            recvs.append(sends[-1])
        for cp in sends:
            cp.start()
        for snd, rcv in zip(sends, recvs):
            snd.wait_send()
            rcv.wait_recv()

    return _pcall(body, name=name, in_specs=[HBM_SPEC] * (2 * n), out_specs=[HBM_SPEC] * n,
                  out_shape=[jax.ShapeDtypeStruct(l.shape, l.dtype) for l in lands],
                  input_output_aliases={n + w: w for w in range(n)},
                  scratch_shapes=_sem_arrays(per * n, per * n))(*shards, *lands)


def _scatter_copies(srcs, lands, send_sems, recv_sems):
    x, y, c, others = _place()
    return [pltpu.make_async_remote_copy(src_ref=srcs[w].at[2 * cx + cy], dst_ref=lands[w].at[k],
                                         send_sem=send_sems.at[3 * w + k], recv_sem=recv_sems.at[3 * w + k],
                                         device_id=(cx, cy, c), device_id_type=MESH)
            for w in range(len(srcs)) for k, (cx, cy) in enumerate(others)]


def _scatter_start(parts, name):
    n = len(parts)

    def body(*refs):
        srcs, lands, send_sems, recv_sems, token = refs[:n], refs[n:2 * n], refs[2 * n], refs[2 * n + 1], refs[-1]
        for cp in _scatter_copies(srcs, lands, send_sems, recv_sems):
            cp.start()
        token[...] = jnp.zeros_like(token)

    lands = [lax.empty((3,) + p.shape[1:], p.dtype) for p in parts]
    outs = _pcall(
        body, name=name, in_specs=[HBM_SPEC] * (2 * n), out_specs=[SEM_SPEC, SEM_SPEC] + [HBM_SPEC] * (2 * n) + [VMEM_SPEC],
        out_shape=[pltpu.SemaphoreType.DMA((3 * n,)), pltpu.SemaphoreType.DMA((3 * n,))]
        + [pltpu.HBM(a.shape, a.dtype) for a in list(parts) + lands] + [jax.ShapeDtypeStruct((8, LANES), F32)],
        input_output_aliases={i: 2 + i for i in range(2 * n)},
        compiler_params=pltpu.CompilerParams(has_side_effects=DATAFLOW),
    )(*[_in_hbm(a) for a in list(parts) + lands])
    return (outs[0], outs[1], outs[2:2 + n], outs[2 + n:2 + 2 * n]), outs[-1]


def _scatter_wait(started, after, name):
    send_sems, recv_sems, srcs, lands = started
    n = len(srcs)

    def body(*refs):
        for cp in _scatter_copies(refs[:n], refs[n:2 * n], refs[2 * n], refs[2 * n + 1]):
            cp.wait_send()
            cp.wait_recv()

    outs = _pcall(
        body, name=name, in_specs=[HBM_SPEC] * (2 * n) + [SEM_SPEC, SEM_SPEC, ANY_SPEC], out_specs=[HBM_SPEC] * (2 * n),
        out_shape=[pltpu.HBM(a.shape, a.dtype) for a in list(srcs) + list(lands)],
        input_output_aliases={i: i for i in range(2 * n)},
        compiler_params=pltpu.CompilerParams(has_side_effects=DATAFLOW),
    )(*srcs, *lands, send_sems, recv_sems, after)
    return outs[:n], outs[n:]


def _split_start(plan, arrays, n_copies, after, name):
    n = len(arrays)

    def body(*refs):
        for cp in plan(refs[:n], refs[n + 1], refs[n + 2])[0]:
            cp.start()
        refs[-1][...] = jnp.zeros_like(refs[-1])

    outs = _pcall(
        body, name=name, in_specs=[HBM_SPEC] * n + [ANY_SPEC], out_specs=[SEM_SPEC, SEM_SPEC] + [HBM_SPEC] * n + [VMEM_SPEC],
        out_shape=[pltpu.SemaphoreType.DMA((n_copies,)), pltpu.SemaphoreType.DMA((n_copies,))]
        + [pltpu.HBM(a.shape, a.dtype) for a in arrays] + [jax.ShapeDtypeStruct((8, LANES), F32)],
        input_output_aliases={i: 2 + i for i in range(n)},
        compiler_params=pltpu.CompilerParams(has_side_effects=DATAFLOW),
    )(*[_in_hbm(a) for a in arrays], after)
    return (outs[0], outs[1], outs[2:2 + n]), outs[-1]


def _split_wait(plan, started, after, name):
    send_sems, recv_sems, arrays = started
    n = len(arrays)

    def body(*refs):
        outgoing, incoming = plan(refs[:n], refs[n], refs[n + 1])
        for cp in outgoing:
            cp.wait_send()
        for cp in incoming:
            cp.wait_recv()

    return _pcall(
        body, name=name, in_specs=[HBM_SPEC] * n + [SEM_SPEC, SEM_SPEC, ANY_SPEC], out_specs=[HBM_SPEC] * n,
        out_shape=[pltpu.HBM(a.shape, a.dtype) for a in arrays], input_output_aliases={i: i for i in range(n)},
        compiler_params=pltpu.CompilerParams(has_side_effects=DATAFLOW),
    )(*arrays, send_sems, recv_sems, after)


def _to_sibling(src, dst, k, send_sems, recv_sems):
    x, y, c, _ = _place()
    return pltpu.make_async_remote_copy(src_ref=src, dst_ref=dst, send_sem=send_sems.at[k], recv_sem=recv_sems.at[k],
                                        device_id=(x, y, 1 - c), device_id_type=MESH)


def _plan_pass_on(n):
    def plan(refs, send_sems, recv_sems):
        x, y, c, others = _place()
        cps = []
        for w in range(n):
            for k, (cx, cy) in enumerate(others):
                mine = _half_rows(refs[n + w], c, (2 * cx + cy,))
                cps.append(_to_sibling(mine, mine, 4 * w + k, send_sems, recv_sems))
            cps.append(_to_sibling(refs[w], refs[n + w].at[2 * x + y], 4 * w + 3, send_sems, recv_sems))
        return cps, cps
    return plan


def _plan_swap_halves(n):
    def plan(refs, send_sems, recv_sems):
        c = lax.axis_index("c")
        cps = [_to_sibling(_half_rows(refs[w], 1 - c, (slice(None),)), refs[n + w], w, send_sems, recv_sems) for w in range(n)]
        return cps, cps
    return plan


def _plan_share_halves(n):
    def plan(refs, send_sems, recv_sems):
        c = lax.axis_index("c")
        cps = [_to_sibling(refs[w].at[c], refs[w].at[c], w, send_sems, recv_sems) for w in range(n)]
        return cps, cps
    return plan


def _plan_gather_small(refs, send_sems, recv_sems):
    x, y, c, _ = _place()
    flips = [(dx, dy, dc) for dx in (0, 1) for dy in (0, 1) for dc in (0, 1)][1:]
    flip = lambda v, d: 1 - v if d else v
    cps = [pltpu.make_async_remote_copy(src_ref=refs[0], dst_ref=refs[1].at[4 * x + 2 * y + c], send_sem=send_sems.at[k],
                                        recv_sem=recv_sems.at[k], device_id=(flip(x, dx), flip(y, dy), flip(c, dc)),
                                        device_id_type=MESH)
           for k, (dx, dy, dc) in enumerate(flips)]
    return cps, cps


def _sum_gathered_small(gathered, own, place, name):
    _, M, C = gathered.shape

    def body(place_ref, g_ref, own_ref, o_ref):
        me = 2 * place_ref[0] + place_ref[1]
        acc = jnp.zeros((M, C), F32)
        for k in range(8):
            acc = acc + jnp.where(me == k, own_ref[...], g_ref[k])
        o_ref[...] = acc

    spec = pltpu.PrefetchScalarGridSpec(
        num_scalar_prefetch=1, grid=(1,),
        in_specs=[pl.BlockSpec((8, M, C), lambda i, pr: (0, 0, 0)), pl.BlockSpec((M, C), lambda i, pr: (0, 0))],
        out_specs=pl.BlockSpec((M, C), lambda i, pr: (0, 0)))
    return _pcall(body, name=name, grid_spec=spec, out_shape=jax.ShapeDtypeStruct((M, C), F32),
                  compiler_params=_params(("arbitrary",)))(place, gathered, own)


def _allgather_group(shards, name):
    n = len(shards)
    per = 7

    def body(*refs):
        ins, outs = refs[:n], refs[n:2 * n]
        send_sems, recv_sems = refs[2 * n:]
        x, y, c, others = _place()
        me = 2 * x + y
        sibling = (x, y, 1 - c)

        def copy(w, k, src, dst, to):
            return pltpu.make_async_remote_copy(src_ref=src, dst_ref=dst, send_sem=send_sems.at[per * w + k],
                                                recv_sem=recv_sems.at[per * w + k], device_id=to, device_id_type=MESH)

        first = [copy(w, k, _half_rows(ins[w], c), _half_rows(outs[w], c, (me,)), (cx, cy, c))
                 for w in range(n) for k, (cx, cy) in enumerate(others)]
        own = [copy(w, 6, ins[w], outs[w].at[me], sibling) for w in range(n)]
        for cp in first + own:
            cp.start()
        passed = []
        for w in range(n):
            for k, (cx, cy) in enumerate(others):
                landed = _half_rows(outs[w], c, (2 * cx + cy,))
                copy(w, k, landed, landed, sibling).wait_recv()
                passed.append(copy(w, 3 + k, landed, landed, sibling))
                passed[-1].start()
        for w in range(n):
            for k, (cx, cy) in enumerate(others):
                theirs = _half_rows(outs[w], 1 - c, (2 * cx + cy,))
                copy(w, 3 + k, theirs, theirs, sibling).wait_recv()
            own[w].wait_recv()
        for cp in first + passed + own:
            cp.wait_send()

    return _pcall(body, name=name, in_specs=[HBM_SPEC] * n, out_specs=[HBM_SPEC] * n,
                  out_shape=[jax.ShapeDtypeStruct((N_CHIPS,) + s.shape, s.dtype) for s in shards],
                  scratch_shapes=_sem_arrays(per * n, per * n))(*shards)


def _swap_halves_group(arrs, name):
    n = len(arrs)

    def body(*refs):
        ins, gots = refs[:n], refs[n:2 * n]
        send_sems, recv_sems = refs[2 * n:]
        x, y, c, _ = _place()
        swaps = [pltpu.make_async_remote_copy(src_ref=_half_rows(ins[w], 1 - c, (slice(None),)), dst_ref=gots[w],
                                              send_sem=send_sems.at[w], recv_sem=recv_sems.at[w],
                                              device_id=(x, y, 1 - c), device_id_type=MESH) for w in range(n)]
        for cp in swaps:
            cp.start()
        for cp in swaps:
            cp.wait()

    half_shapes = [jax.ShapeDtypeStruct((a.shape[0], a.shape[1] // 2, a.shape[2]), a.dtype) for a in arrs]
    return _pcall(body, name=name, in_specs=[HBM_SPEC] * n, out_specs=[HBM_SPEC] * n, out_shape=half_shapes,
                  scratch_shapes=_sem_arrays(n, n))(*arrs)


def _scatter_group(parts, name):
    n = len(parts)

    def body(*refs):
        ins, outs = refs[:n], refs[n:2 * n]
        send_sems, recv_sems = refs[2 * n:]
        x, y, c, others = _place()

        def copy(w, k, src_chip, to):
            return pltpu.make_async_remote_copy(src_ref=ins[w].at[src_chip], dst_ref=outs[w].at[k],
                                                send_sem=send_sems.at[3 * w + k], recv_sem=recv_sems.at[3 * w + k],
                                                device_id=to, device_id_type=MESH)

        sends = [copy(w, k, 2 * cx + cy, (cx, cy, c)) for w in range(n) for k, (cx, cy) in enumerate(others)]
        for cp in sends:
            cp.start()
        for cp in sends:
            cp.wait()

    return _pcall(body, name=name, in_specs=[HBM_SPEC] * n, out_specs=[HBM_SPEC] * n,
                  out_shape=[jax.ShapeDtypeStruct((3,) + p.shape[1:], p.dtype) for p in parts],
                  scratch_shapes=_sem_arrays(3 * n, 3 * n))(*parts)


def _share_halves_group(halves, name):
    n = len(halves)

    def body(*refs):
        bufs = refs[n:2 * n]
        send_sems, recv_sems = refs[2 * n:]
        x, y, c, _ = _place()
        swaps = [pltpu.make_async_remote_copy(src_ref=bufs[w].at[c], dst_ref=bufs[w].at[c], send_sem=send_sems.at[w],
                                              recv_sem=recv_sems.at[w], device_id=(x, y, 1 - c), device_id_type=MESH)
                 for w in range(n)]
        for cp in swaps:
            cp.start()
        for w in range(n):
            swaps[w].wait_send()
            pltpu.make_async_remote_copy(src_ref=bufs[w].at[c], dst_ref=bufs[w].at[1 - c], send_sem=send_sems.at[w],
                                         recv_sem=recv_sems.at[w], device_id=(x, y, 1 - c), device_id_type=MESH).wait_recv()

    return _pcall(body, name=name, in_specs=[HBM_SPEC] * n, out_specs=[HBM_SPEC] * n,
                  out_shape=[jax.ShapeDtypeStruct(h.shape, h.dtype) for h in halves],
                  input_output_aliases={w: w for w in range(n)},
                  scratch_shapes=_sem_arrays(n, n))(*halves)


def _allgather_small(blk, name):
    M, C = blk.shape

    def body(x_ref, out_ref, send_sems, recv_sems, local_sem):
        x, y, c, others = _place()
        me, sibling = (x, y, c), (x, y, 1 - c)

        def rows(px, py, pc):
            return out_ref.at[4 * px + 2 * py + pc]

        def copy(k, block, to, src=None):
            return pltpu.make_async_remote_copy(src_ref=rows(*block) if src is None else src, dst_ref=rows(*block),
                                                send_sem=send_sems.at[k], recv_sem=recv_sems.at[k], device_id=to, device_id_type=MESH)

        mine = pltpu.make_async_copy(x_ref, rows(*me), local_sem)
        mine.start()
        first = [copy(0, me, sibling, src=x_ref)]
        first += [copy(1 + j, me, (*chip, c), src=x_ref) for j, chip in enumerate(others)]
        for cp in first:
            cp.start()
        passed = [copy(4 + j, (*chip, c), sibling) for j, chip in enumerate(others)]
        for j, chip in enumerate(others):
            copy(1 + j, (*chip, c), me).wait_recv()
            passed[j].start()
        copy(0, sibling, me).wait_recv()
        for j, chip in enumerate(others):
            copy(4 + j, (*chip, 1 - c), me).wait_recv()
        for cp in first + passed:
            cp.wait_send()
        mine.wait()

    return _pcall(body, name=name, in_specs=[VMEM_SPEC], out_specs=VMEM_SPEC,
                  out_shape=jax.ShapeDtypeStruct((8, M, C), blk.dtype),
                  scratch_shapes=[pltpu.SemaphoreType.DMA((7,)), pltpu.SemaphoreType.DMA((7,)), pltpu.SemaphoreType.DMA])(blk)


def _pack_rows(n_elems, width=PACK_W, align=PACK_ROW_ALIGN):
    rows = -(-n_elems // width)
    return -(-rows // align) * align


def _pack(arrays, dtype, width=PACK_W, align=PACK_ROW_ALIGN):
    flat = jnp.concatenate([a.astype(dtype).reshape(-1) for a in arrays])
    rows = _pack_rows(flat.shape[0], width, align)
    flat = jnp.pad(flat, (0, rows * width - flat.shape[0]))
    return flat.reshape(rows, width)


def _pack_small(arrays):
    return _pack(arrays, F32, width=128, align=8)


def _unpack(flat, shapes):
    out, off = [], 0
    for shp in shapes:
        n = int(np.prod(shp))
        out.append(flat[..., off:off + n].reshape(flat.shape[:-1] + tuple(shp)))
        off += n
    return out


def _doubled_heads(x2d, n_heads):
    S = x2d.shape[0]
    h = x2d.reshape(S, n_heads, HEAD_DIM).transpose(1, 0, 2)
    return jnp.concatenate([h, h], axis=-1)


def _rms_bwd_epilogue(dn, x, dres, g):
    r = _rinv(x)
    xh = x * r
    dxh = dn * g
    dx = dres + r * (dxh - xh * jnp.mean(dxh * xh, axis=-1, keepdims=True))
    return dx, dx, jnp.sum(dn * xh, axis=0, keepdims=True)


def _residual_then_norms(acc, res, *gains):
    h = res + acc
    hn = h * _rinv(h)
    return (h,) + tuple(hn * g for g in gains)


def _mlp_fwd(h, n, w_up4, w_down, next_gains, tag):
    u, a = _matmul(n, w_up4, "nn", f"up{tag}", out_dtypes=(F32, BF16), chipwise="b", tm=2048, tn=512,
                   epilogue=lambda acc: (acc, jnp.square(jnp.maximum(acc, 0.0))))
    assert w_down.shape[1] == 1024
    outs = _matmul(a, w_down, "nn", f"down{tag}", out_dtypes=(F32,) + (BF16,) * len(next_gains), extras=(h,),
                   row_extras=tuple(next_gains), epilogue=_residual_then_norms, tm=1024, tn=1024, tk=1024)
    outs = outs if next_gains else (outs,)
    return outs[0], outs[1:], (n, u, a)


def _mlp_bwd(dh_out, dh_out_b, h, g, w_up4, w_down, saved, tag):
    n, u, a = saved
    dw_down = _matmul(a, dh_out_b, "tn", f"dw_down{tag}", out_dtypes=(BF16,), tm=2048, tn=512)
    du = _matmul(dh_out_b, w_down, "nt", f"du{tag}", out_dtypes=(BF16,), extras=(u,), tm=2048, tn=512,
                 epilogue=lambda acc, uu: (acc * (2.0 * jnp.maximum(uu, 0.0)),))
    dw_up = _matmul(n, du, "tn", f"dw_up{tag}", out_dtypes=(BF16,), chipwise="out", tm=1024, tn=512)
    dh, dh_b, dg = _matmul(du, w_up4, "nt", f"dn_mlp{tag}", out_dtypes=(F32, BF16, F32), extras=(h, dh_out), row_extras=(g,),
                           epilogue=_rms_bwd_epilogue, row_accums=1, tm=1024, tn=1024, tk=w_up4.shape[2], chipwise="b")
    return dh, dh_b, dg, dw_up, dw_down


def kernel(x, g_attn, g_mlp, w_in_a, b_f, gq_a, gk_a, w_out_a, g_kv, w_kv, gk_b, w_q_b, gq_b, sinks, rel_bias, w_out_b, w_up, w_down, loss_target, m_g_attn, m_g_mlp, m_w_in_a, m_b_f, m_gq_a, m_gk_a, m_w_out_a, m_g_kv, m_w_kv, m_gk_b, m_w_q_b, m_gq_b, m_sinks, m_rel_bias, m_w_out_b, m_w_up, m_w_down, v_g_attn, v_g_mlp, v_w_in_a, v_b_f, v_gq_a, v_gk_a, v_w_out_a, v_g_kv, v_w_kv, v_gk_b, v_w_q_b, v_gq_b, v_sinks, v_rel_bias, v_w_out_b, v_w_up, v_w_down):
    given = dict(locals())
    S, D = x.shape[1], x.shape[2]
    H = D // HEAD_DIM
    KVH = w_kv.shape[1] // (2 * HEAD_DIM)
    kvw = KVH * HEAD_DIM
    hw = H * HEAD_DIM
    W = WINDOW
    nb = S // W
    c_idx = lax.axis_index("c")
    xs, tgt = x[0], loss_target[0]

    n_in_shard = w_in_a.shape[2]
    rows_in = -(-n_in_shard // 32) * 32
    row_pad = lambda a: jnp.pad(a, [(0, 0)] * (a.ndim - 2) + [(0, rows_in - a.shape[-2]), (0, 0)])
    t_in = lambda a: jnp.swapaxes(a[0], 0, 1)
    shards = {"w_in_a": row_pad(t_in(w_in_a)), "w_out_a": w_out_a[0], "w_up0": w_up[0], "w_down0": w_down[0], "w_kv": w_kv,
              "w_q_b": w_q_b[0], "w_out_b": w_out_b[0], "w_up1": w_up[1], "w_down1": w_down[1]}
    parts = list(shards)
    groups = [("w_in_a", "w_out_a"), ("w_up0", "w_down0"), ("w_kv", "w_q_b", "w_out_b", "w_up1", "w_down1")]
    started = []
    for i, grp in enumerate(groups):
        behind = started[-1][4] if started else g_attn[0]
        started.append(_gather_start([shards[n].astype(BF16) for n in grp], behind, f"gather_start{i}"))
    gathered = {}

    def finish_gather(i, after):
        srcs, lands = _gather_wait(started[i], after, f"gather_wait{i}")
        gathered.update(zip(groups[i], _gather_pass_on(srcs, lands, f"gather_pass_on{i}")))

    def land_gather(i, after):
        srcs, lands = _gather_wait(started[i], after, f"gather_wait{i}")
        n = len(srcs)
        passing, token = _split_start(_plan_pass_on(n), list(srcs) + list(lands), 4 * n, after, f"pass_on_start{i}")
        return passing, token[0:1, 0:1]

    def finish_pass_on(i, passing, after):
        n = len(groups[i])
        gathered.update(zip(groups[i], _split_wait(_plan_pass_on(n), passing, after, f"pass_on_wait{i}")[n:]))

    vec = lambda a: a.reshape(1, -1)
    twice = lambda a: jnp.tile(a.reshape(1, -1), (1, 2))

    g_attn0 = vec(g_attn[0]) + sum(st[4][0, 0] for st in started)
    (n0,) = _rms_fwd(xs, [g_attn0], "rms_attn0")
    finish_gather(0, n0)
    win_t = gathered["w_in_a"][:, :n_in_shard].reshape(-1, D)
    win_t = jnp.pad(win_t, ((0, (-win_t.shape[0]) % 128), (0, 0)))
    wout_a = gathered["w_out_a"].reshape(-1, D)
    n_in = win_t.shape[0]
    tile_in = 640 if n_in % 640 == 0 else 128
    proj = _matmul(n0, win_t, "nt", "proj_in", tm=2048, tn=tile_in)
    zt = proj[:, 3 * hw:3 * hw + H].T
    c_row = _gate_fwd(zt, b_f.reshape(H, 1), "gate_fwd")
    c_row3 = c_row.reshape(H, 1, S)
    o_a, lse_a = _fox_fwd(proj, c_row3, twice(gq_a[0]), twice(gk_a[0]), H, "fox_fwd")
    passing1, tie = land_gather(1, o_a)
    h1, n1 = _matmul(o_a, wout_a, "nn", "out_a", out_dtypes=(F32, BF16), extras=(xs,), row_extras=(vec(g_mlp[0]) + tie,),
                     epilogue=_residual_then_norms, tm=1024, tn=1024)
    finish_pass_on(1, passing1, n1)
    wup = [gathered["w_up0"], None]
    wdown = [gathered["w_down0"].reshape(-1, D), None]
    passing2, tie = land_gather(2, n1)
    h2, (nkv, n2), mlp0 = _mlp_fwd(h1, n1, wup[0], wdown[0], [vec(g_kv) + tie, vec(g_attn[1])], "0")

    finish_pass_on(2, passing2, h2)
    wq_b, wout_b = gathered["w_q_b"].reshape(-1, D), gathered["w_out_b"].reshape(-1, D)
    wkv = gathered["w_kv"].reshape(D, -1)
    wup[1], wdown[1] = gathered["w_up1"], gathered["w_down1"].reshape(-1, D)
    kv = _matmul(nkv, wkv, "nn", "proj_kv", tm=2048)
    kk, vv = _doubled_heads(kv[:, :kvw], KVH), _doubled_heads(kv[:, kvw:], KVH)
    q2 = _matmul(n2, wq_b, "nn", "proj_q", tm=1024, tn=1024)
    onehot = jnp.asarray(_bucket_onehot(), dtype=BF16)
    bias = _bias_expand(rel_bias.T, onehot, "bias_expand").reshape(H, W, 2 * W)
    bias_ab = bias.reshape(H // 2, 2 * W, 2 * W)
    bias_t_ab = bias.reshape(H // 2, 2, W, 2 * W).transpose(0, 3, 1, 2).reshape(H // 2, 2 * W, 2 * W)
    sink_ab = jnp.repeat(sinks[0].reshape(H // 2, 2), W, axis=1)
    o_b, lse_b = _swa_fwd(q2, kk, vv, bias_ab, sink_ab.reshape(H // 2, 2 * W, 1), twice(gq_b[0]), twice(gk_b), "swa_fwd")
    h3, n3 = _matmul(o_b, wout_b, "nn", "out_b", out_dtypes=(F32, BF16), extras=(h2,), row_extras=(vec(g_mlp[1]),),
                     epilogue=_residual_then_norms, tm=1024, tn=1024)
    h4, _, mlp1 = _mlp_fwd(h3, n3, wup[1], wdown[1], [], "1")

    dh4, dh4_b, loss_part = _loss_head(h4, tgt, "loss_head")

    place = jnp.stack([2 * lax.axis_index("x") + lax.axis_index("y"), c_idx]).astype(jnp.int32)
    scattering = []

    def pair_and_scatter(names, mine, got):
        pair_sums = [_sum_core_pair(a, g, place, "sum_core_pair_" + n) for n, a, g in zip(names, mine, got)]
        started_scatter, token = _scatter_start(pair_sums, "scatter_start_" + names[0])
        scattering.append((names, started_scatter))
        return token[0:1, :]

    def start_reduce(named):
        names = list(named)
        mine = [named[n] for n in names]
        return pair_and_scatter(names, mine, _swap_halves_group(mine, "swap_grad_halves_" + names[0]))

    def start_swap(named):
        names = list(named)
        mine = [named[n] for n in names]
        lands = [lax.empty((a.shape[0], a.shape[1] // 2, a.shape[2]), a.dtype) for a in mine]
        swapping, token = _split_start(_plan_swap_halves(len(mine)), mine + lands, len(mine), mine[0], "swap_start_" + names[0])
        return names, swapping, token

    def finish_swap(swap, after):
        names, swapping, _ = swap
        n = len(names)
        arrays = _split_wait(_plan_swap_halves(n), swapping, after, "swap_wait_" + names[0])
        return pair_and_scatter(names, arrays[:n], arrays[n:])

    dh3, dh3_b, dg_mlp1, dw_up1, dw_down1 = _mlp_bwd(dh4, dh4_b, h3, vec(g_mlp[1]), wup[1], wdown[1], mlp1, "1")
    swap1 = start_swap({"w_down1": dw_down1.reshape(N_CHIPS, -1, D), "w_up1": dw_up1})
    dw_out_b = _matmul(o_b, dh3_b, "tn", "dw_out_b", out_dtypes=(BF16,), tm=1024, tn=1024, after=swap1[2])
    do_b = _matmul(dh3_b, wout_b, "nt", "do_b", tm=1024, tn=1024)
    tie1 = finish_swap(swap1, do_b)
    dq2, dk2, dv2, dbias_t_ab, dsink, dgq_b, dgk_b = _swa_bwd(
        q2, kk, vv, bias_t_ab, sink_ab.reshape(H // 2, 1, 2 * W), twice(gq_b[0]) + tie1, twice(gk_b),
        lse_b, do_b, "swa_bwd")
    dbias = dbias_t_ab.reshape(H // 2, 2 * W, 2, W).transpose(0, 2, 3, 1).reshape(H, W * 2 * W)
    d_rel_bias = _bias_reduce(dbias, onehot, "bias_reduce").T
    dw_q_b = _matmul(n2, dq2, "tn", "dw_q_b", out_dtypes=(BF16,), tm=1024, tn=1024)
    dn2 = _matmul(dq2, wq_b, "nt", "dn2", tm=1024, tn=1024)
    dkv = jnp.concatenate([dk2[h, :, :HEAD_DIM] for h in range(KVH)] + [dv2[h, :, :HEAD_DIM] for h in range(KVH)],
                          axis=1).astype(BF16)
    dw_kv = _matmul(nkv, dkv, "tn", "dw_kv", out_dtypes=(BF16,), tm=1024)
    dnkv = _matmul(dkv, wkv, "nt", "dnkv", tm=1024, tn=1024)
    tie2 = start_reduce({"w_out_b": dw_out_b.reshape(N_CHIPS, -1, D), "w_q_b": dw_q_b.reshape(N_CHIPS, -1, D),
                         "w_kv": dw_kv.reshape(N_CHIPS, -1, 2 * kvw)})
    dh2, dh2_b, (dg_kv, dg_attn1) = _rms_bwd(h2, dh3, [vec(g_kv) + tie2[:, :1], vec(g_attn[1])], [dnkv, dn2], "rms_attn1_bwd")

    dh1, dh1_b, dg_mlp0, dw_up0, dw_down0 = _mlp_bwd(dh2, dh2_b, h1, vec(g_mlp[0]), wup[0], wdown[0], mlp0, "0")
    swap3 = start_swap({"w_down0": dw_down0.reshape(N_CHIPS, -1, D), "w_up0": dw_up0})
    dw_out_a = _matmul(o_a, dh1_b, "tn", "dw_out_a", out_dtypes=(BF16,), tm=1024, tn=1024, after=swap3[2])
    do_a = _matmul(dh1_b, wout_a, "nt", "do_a", tm=1024, tn=1024)
    tie3 = finish_swap(swap3, do_a)
    dq_a, dk_a, dv_a, dc_row, dgq_a, dgk_a = _fox_bwd(
        proj, c_row3, twice(gq_a[0]) + tie3, twice(gk_a[0]), lse_a, do_a, H, "fox_bwd")
    dzt, db_f = _gate_bwd(dc_row.reshape(H, S), zt, b_f.reshape(H, 1), "gate_bwd")
    dproj = jnp.concatenate([dq_a, dk_a, dv_a, dzt.T.astype(BF16), jnp.zeros((S, n_in - 3 * hw - H), BF16)], axis=1)
    dw_in_t = _matmul(dproj, n0, "tn", "dw_in", out_dtypes=(BF16,), tm=tile_in, tn=1024)
    dw_in4 = row_pad(dw_in_t[:3 * hw + H].reshape(N_CHIPS, -1, D))
    tie4 = start_reduce({"w_out_a": dw_out_a.reshape(N_CHIPS, -1, D), "w_in_a": dw_in4})
    grad_x, _, dg_attn0 = _matmul(dproj, win_t, "nn", "dn0", out_dtypes=(F32, BF16, F32), extras=(xs, dh1),
                                  row_extras=(vec(g_attn[0]) + tie4[:, :1],), epilogue=_rms_bwd_epilogue, row_accums=1,
                                  tm=1024, tn=1024, tk=tile_in)

    small_grads = {
        "g_attn": jnp.concatenate([dg_attn0, dg_attn1], axis=0), "g_mlp": jnp.concatenate([dg_mlp0, dg_mlp1], axis=0),
        "b_f": db_f.reshape(1, H), "gq_a": dgq_a[:, :HEAD_DIM], "gk_a": dgk_a[:, :HEAD_DIM], "g_kv": dg_kv.reshape(-1),
        "gk_b": dgk_b[0, :HEAD_DIM], "gq_b": dgq_b[:, :HEAD_DIM], "sinks": dsink[:, 0, 0].reshape(1, H), "rel_bias": d_rel_bias,
    }
    small_shapes = [given[n].shape for n in SMALL] + [(1,)]
    spack = _pack_small([small_grads[n] for n in SMALL] + [loss_part])
    gathering_small, token = _split_start(_plan_gather_small, [spack, lax.empty((8,) + spack.shape, F32)], 7, grad_x,
                                          "gather_small_start")
    sharing = []
    for names, started_scatter in scattering:
        pair_sums, landed = _scatter_wait(started_scatter, grad_x, "scatter_wait_" + names[0])
        halves = [_sum_chips(p, l, place, "sum_chips_" + n) for n, p, l in zip(names, pair_sums, landed)]
        started_share, token = _split_start(_plan_share_halves(len(halves)), halves, len(halves), token, "share_start_" + names[0])
        sharing.append((names, started_share))
    own_small, others_small = _split_wait(_plan_gather_small, gathering_small, token, "gather_small_wait")
    small_sum = _sum_gathered_small(others_small, own_small, place, "sum_small")
    small_red = _unpack(small_sum.reshape(-1), small_shapes)
    reduced = {}
    for names, started_share in sharing:
        for n, r in zip(names, _split_wait(_plan_share_halves(len(names)), started_share, token, "share_wait_" + names[0])):
            reduced[n] = r.reshape(-1, r.shape[2])
    reduced["w_in_a"] = reduced["w_in_a"][:n_in_shard]
    loss = small_red[-1][0]

    grads = dict(zip(SMALL, small_red))
    no_loss = [jnp.zeros((1,), F32)]
    sw = _pack_small([given[n] for n in SMALL] + no_loss)
    sm = _pack_small([given["m_" + n] for n in SMALL] + no_loss)
    sv = _pack_small([given["v_" + n] for n in SMALL] + no_loss)
    sd, sm2, sv2 = _adamw(sw, small_sum, sm, sv, "adamw_small", tr=sw.shape[0])
    delta = dict(zip(SMALL, _unpack(sd.reshape(-1), small_shapes)))
    new_m = dict(zip(SMALL, _unpack(sm2.reshape(-1), small_shapes)))
    new_v = dict(zip(SMALL, _unpack(sv2.reshape(-1), small_shapes)))
    for n in ("w_out_a", "w_kv", "w_q_b", "w_out_b"):
        w = given[n]
        two_d = (-1, w.shape[-1])
        d, m2, v2 = _adamw(w.reshape(two_d), reduced[n], given["m_" + n].reshape(two_d), given["v_" + n].reshape(two_d),
                           "adamw_" + n)
        grads[n] = reduced[n].reshape(w.shape)
        delta[n], new_m[n], new_v[n] = d.reshape(w.shape), m2.reshape(w.shape), v2.reshape(w.shape)
    d, m2, v2 = _adamw(t_in(w_in_a), reduced["w_in_a"], t_in(m_w_in_a), t_in(v_w_in_a), "adamw_w_in_a")
    back = lambda a: jnp.swapaxes(a, 0, 1)[None]
    grads["w_in_a"], delta["w_in_a"], new_m["w_in_a"], new_v["w_in_a"] = back(reduced["w_in_a"]), back(d), back(m2), back(v2)
    for n in ("w_up", "w_down"):
        w = given[n]
        two_d = (-1, w.shape[-1])
        g, d, m2, v2 = _adamw_two_layers(w.reshape(two_d), reduced[n + "0"], reduced[n + "1"], given["m_" + n].reshape(two_d),
                                         given["v_" + n].reshape(two_d), "adamw_" + n)
        grads[n], delta[n], new_m[n], new_v[n] = g.reshape(w.shape), d.reshape(w.shape), m2.reshape(w.shape), v2.reshape(w.shape)

    order = ["g_attn", "g_mlp", "w_in_a", "b_f", "gq_a", "gk_a", "w_out_a", "g_kv", "w_kv", "gk_b", "w_q_b", "gq_b",
             "sinks", "rel_bias", "w_out_b", "w_up", "w_down"]
    return (loss, grad_x[None], *[grads[n] for n in order], *[delta[n] for n in order],
            *[new_m[n] for n in order], *[new_v[n] for n in order])
```

```python
import numpy as np
import jax
import jax.numpy as jnp
from jax import lax
from jax.experimental import pallas as pl
from jax.experimental.pallas import tpu as pltpu

F32 = jnp.float32
BF16 = jnp.bfloat16
MESH = pl.DeviceIdType.MESH

HEAD_DIM = 64
LANES = 128
WINDOW = 128
N_BUCKETS = 32
REL_MAX_DIST = 128
NORM_EPS = 1e-6
ADAM_LR = 0.001
ADAM_B1 = 0.9
ADAM_B2 = 0.999
ADAM_EPS = 1e-08
ADAM_WD = 0.01
ADAM_STEP = 10
NEG = -1e30
N_CHIPS = 4
PACK_W = 1024
PACK_ROW_ALIGN = 256
VMEM_LIMIT = 56 * 1024 * 1024
HBM_SPEC = pl.BlockSpec(memory_space=pltpu.HBM)
VMEM_SPEC = pl.BlockSpec(memory_space=pltpu.VMEM)

BIG = (("w_in_a", 2), ("w_out_a", 1), ("w_kv", 0), ("w_q_b", 1), ("w_out_b", 1), ("w_up", 2), ("w_down", 1))
SMALL = ("g_attn", "g_mlp", "b_f", "gq_a", "gk_a", "g_kv", "gk_b", "gq_b", "sinks", "rel_bias")


def _pcall(body, **kw):
    return pl.pallas_call(body, **kw)


def _params(sem=None):
    return pltpu.CompilerParams(dimension_semantics=sem, vmem_limit_bytes=VMEM_LIMIT)


def _rinv(x):
    return lax.rsqrt(jnp.mean(x * x, axis=-1, keepdims=True) + NORM_EPS)


def _dot(a, b, dims, precision=None):
    return lax.dot_general(a, b, (dims, ((), ())), precision=precision, preferred_element_type=F32)


NN = ((1,), (0,))
NT = ((1,), (1,))
TN = ((0,), (0,))


def _accumulate(ref, val, first):
    @pl.when(first)
    def _():
        ref[...] = val

    @pl.when(jnp.logical_not(first))
    def _():
        ref[...] += val


def _matmul(a, b, mode, name, out_dtypes=(F32,), extras=(), row_extras=(), epilogue=None, tm=512, tn=512, tk=None, chipwise=None,
            after=None, row_accums=0):
    if chipwise == "b":
        nc = b.shape[2]
        M, K = a.shape
        (K2, N) = (b.shape[1], N_CHIPS * nc) if mode == "nn" else (N_CHIPS * nc, b.shape[1])
    elif mode == "nn":
        (M, K), (K2, N) = a.shape, b.shape
    elif mode == "nt":
        (M, K), (N, K2) = a.shape, b.shape
    else:
        (K, M), (K2, N) = a.shape, b.shape
    assert K == K2, (a.shape, b.shape, mode)
    tm, tn = min(tm, M), min(tn, N)
    tk = K if tk is None else tk
    assert M % tm == 0 and N % tn == 0 and K % tk == 0, (M, N, K, tm, tn, tk)
    nk = K // tk
    dims = {"nn": NN, "nt": NT, "tn": TN}[mode]
    a_spec = pl.BlockSpec((tk, tm), lambda i, j, k: (k, i)) if mode == "tn" else pl.BlockSpec((tm, tk), lambda i, j, k: (i, k))
    b_spec = pl.BlockSpec((tn, tk), lambda i, j, k: (j, k)) if mode == "nt" else pl.BlockSpec((tk, tn), lambda i, j, k: (k, j))
    o_spec = pl.BlockSpec((tm, tn), lambda i, j, k: (i, j))
    out_shape = (M, N)
    if chipwise == "b" and mode == "nn":
        per = nc // tn
        assert tk == K and nc % tn == 0
        b_spec = pl.BlockSpec((None, tk, tn), lambda i, j, k: (j // per, 0, j % per))
    elif chipwise == "b":
        assert mode == "nt" and tk == nc
        b_spec = pl.BlockSpec((None, tn, tk), lambda i, j, k: (k, j, 0))
    elif chipwise == "out_rows":
        per = (M // N_CHIPS) // tm
        assert (M // N_CHIPS) % tm == 0 and not extras
        o_spec = pl.BlockSpec((None, tm, tn), lambda i, j, k: (i // per, i % per, j))
        out_shape = (N_CHIPS, M // N_CHIPS, N)
    elif chipwise == "out":
        per = (N // N_CHIPS) // tn
        assert (N // N_CHIPS) % tn == 0
        o_spec = pl.BlockSpec((None, tm, tn), lambda i, j, k: (j // per, i, j % per))
        out_shape = (N_CHIPS, M, N // N_CHIPS)
        assert not extras
    n_ex, n_rex, n_out = len(extras), len(row_extras), len(out_dtypes)
    assert not row_accums or tn == N
    tail = () if after is None else (after,)
    n_in = 2 + n_ex + n_rex + len(tail)

    def body(*refs):
        a_ref, b_ref = refs[0], refs[1]
        ex_refs = refs[2:2 + n_ex + n_rex]
        out_refs = refs[n_in:n_in + n_out]
        part = _dot(a_ref[...].astype(BF16), b_ref[...].astype(BF16), dims)

        def finish(acc):
            outs = (acc,) if epilogue is None else epilogue(acc, *[r[...] for r in ex_refs])
            for idx, (r, o) in enumerate(zip(out_refs, outs)):
                if idx >= n_out - row_accums:
                    _accumulate(r, o, pl.program_id(0) == 0)
                else:
                    r[...] = o.astype(r.dtype)

        if nk == 1:
            finish(part)
            return
        acc_ref = refs[n_in + n_out]
        k = pl.program_id(2)

        @pl.when(k == 0)
        def _():
            acc_ref[...] = part

        @pl.when(jnp.logical_and(k > 0, k < nk - 1))
        def _():
            acc_ref[...] += part

        @pl.when(k == nk - 1)
        def _():
            finish(acc_ref[...] + part)

    row_spec = pl.BlockSpec((1, tn), lambda i, j, k: (0, j))
    outs = _pcall(
        body, name=name, grid=(M // tm, N // tn, nk),
        in_specs=[a_spec, b_spec] + [o_spec] * n_ex + [row_spec] * n_rex + [pl.BlockSpec(memory_space=pl.ANY)] * len(tail),
        out_specs=[o_spec] * (n_out - row_accums) + [row_spec] * row_accums,
        out_shape=[jax.ShapeDtypeStruct(out_shape, dt) for dt in out_dtypes[:n_out - row_accums]]
        + [jax.ShapeDtypeStruct((1, N), F32)] * row_accums,
        scratch_shapes=[pltpu.VMEM((tm, tn), F32)] if nk > 1 else [],
        compiler_params=_params(("arbitrary",) * 3 if row_accums else ("parallel", "parallel", "arbitrary")),
    )(a, b, *extras, *row_extras, *tail)
    return outs[0] if n_out == 1 else outs


def _rms_fwd(x, gains, name, ts=256):
    S, D = x.shape
    ts = min(ts, S)
    n = len(gains)

    def body(*refs):
        x_ref, g_refs, o_refs = refs[0], refs[1:1 + n], refs[1 + n:]
        xv = x_ref[...]
        xh = xv * _rinv(xv)
        for g_ref, o_ref in zip(g_refs, o_refs):
            o_ref[...] = (xh * g_ref[...]).astype(BF16)

    row = pl.BlockSpec((ts, D), lambda i: (i, 0))
    vec = pl.BlockSpec((1, D), lambda i: (0, 0))
    return _pcall(body, name=name, grid=(S // ts,), in_specs=[row] + [vec] * n, out_specs=[row] * n,
                  out_shape=[jax.ShapeDtypeStruct((S, D), BF16)] * n, compiler_params=_params(("parallel",)))(x, *gains)


def _rms_bwd(x, dres, gains, dns, name, ts=256):
    S, D = x.shape
    ts = min(ts, S)
    n = len(gains)

    def body(*refs):
        x_ref, dres_ref = refs[0], refs[1]
        g_refs, dn_refs = refs[2:2 + n], refs[2 + n:2 + 2 * n]
        dx_ref, dxb_ref, dg_refs = refs[2 + 2 * n], refs[3 + 2 * n], refs[4 + 2 * n:]
        xv = x_ref[...]
        r = _rinv(xv)
        xh = xv * r
        dx = dres_ref[...]
        first = pl.program_id(0) == 0
        for g_ref, dn_ref, dg_ref in zip(g_refs, dn_refs, dg_refs):
            dn = dn_ref[...].astype(F32)
            _accumulate(dg_ref, jnp.sum(dn * xh, axis=0, keepdims=True), first)
            dxh = dn * g_ref[...]
            dx = dx + r * (dxh - xh * jnp.mean(dxh * xh, axis=-1, keepdims=True))
        dx_ref[...] = dx
        dxb_ref[...] = dx.astype(BF16)

    row = pl.BlockSpec((ts, D), lambda i: (i, 0))
    vec = pl.BlockSpec((1, D), lambda i: (0, 0))
    outs = _pcall(body, name=name, grid=(S // ts,), in_specs=[row, row] + [vec] * n + [row] * n,
                  out_specs=[row, row] + [vec] * n,
                  out_shape=[jax.ShapeDtypeStruct((S, D), F32), jax.ShapeDtypeStruct((S, D), BF16)]
                  + [jax.ShapeDtypeStruct((1, D), F32)] * n,
                  compiler_params=_params(("arbitrary",)))(x, dres, *gains, *dns)
    return outs[0], outs[1], outs[2:]


def _loss_head(h, tgt, name, ts=256):
    S, D = h.shape
    ts = min(ts, S)

    def body(h_ref, t_ref, dh_ref, dhb_ref, loss_ref):
        err = h_ref[...] - t_ref[...]
        dh = err * (1.0 / D)
        dh_ref[...] = dh
        dhb_ref[...] = dh.astype(BF16)
        part = 0.5 * jnp.sum(jnp.mean(err * err, axis=-1, keepdims=True), axis=0, keepdims=True)
        _accumulate(loss_ref, part, pl.program_id(0) == 0)

    row = pl.BlockSpec((ts, D), lambda i: (i, 0))
    return _pcall(body, name=name, grid=(S // ts,), in_specs=[row, row],
                  out_specs=[row, row, pl.BlockSpec((1, 1), lambda i: (0, 0))],
                  out_shape=[jax.ShapeDtypeStruct((S, D), F32), jax.ShapeDtypeStruct((S, D), BF16),
                             jax.ShapeDtypeStruct((1, 1), F32)],
                  compiler_params=_params(("arbitrary",)))(h, tgt)


def _gate_fwd(zt, bf, name):
    H, S = zt.shape
    nb = S // 128

    def body(z_ref, b_ref, c_ref):
        z = z_ref[...] + b_ref[...]
        lf = jnp.minimum(z, 0.0) - jnp.log(1.0 + jnp.exp(-jnp.abs(z)))
        upper = (lax.broadcasted_iota(jnp.int32, (128, 128), 0) <= lax.broadcasted_iota(jnp.int32, (128, 128), 1)).astype(F32)
        carry = jnp.zeros((H, 1), F32)
        for blk in range(nb):
            cs = _dot(lf[:, blk * 128:(blk + 1) * 128], upper, NN, precision=lax.Precision.HIGHEST) + carry
            c_ref[:, blk * 128:(blk + 1) * 128] = cs
            carry = cs[:, 127:128]

    return _pcall(body, name=name, in_specs=[VMEM_SPEC, VMEM_SPEC], out_specs=VMEM_SPEC,
                  out_shape=jax.ShapeDtypeStruct((H, S), F32))(zt, bf)


def _gate_bwd(dct, zt, bf, name):
    H, S = zt.shape
    nb = S // 128

    def body(dc_ref, z_ref, b_ref, dz_ref, db_ref):
        z = z_ref[...] + b_ref[...]
        e = jnp.exp(-jnp.abs(z))
        sig_neg = jnp.where(z >= 0, e, 1.0) / (1.0 + e)
        lower = (lax.broadcasted_iota(jnp.int32, (128, 128), 0) >= lax.broadcasted_iota(jnp.int32, (128, 128), 1)).astype(F32)
        dc = dc_ref[...]
        carry = jnp.zeros((H, 1), F32)
        db = jnp.zeros((H, 1), F32)
        for blk in reversed(range(nb)):
            sl = slice(blk * 128, (blk + 1) * 128)
            dlf = _dot(dc[:, sl], lower, NN, precision=lax.Precision.HIGHEST) + carry
            carry = dlf[:, 0:1]
            dz = dlf * sig_neg[:, sl]
            dz_ref[:, sl] = dz
            db = db + jnp.sum(dz, axis=1, keepdims=True)
        db_ref[...] = db

    return _pcall(body, name=name, in_specs=[VMEM_SPEC] * 3, out_specs=[VMEM_SPEC] * 2,
                  out_shape=[jax.ShapeDtypeStruct((H, S), F32), jax.ShapeDtypeStruct((H, 1), F32)])(dct, zt, bf)


def _lane_is_a():
    return lax.broadcasted_iota(jnp.int32, (1, LANES), 1) < HEAD_DIM


def _per_head_mean(x, is_a):
    sa = jnp.sum(jnp.where(is_a, x, 0.0), axis=-1, keepdims=True)
    sb = jnp.sum(jnp.where(is_a, 0.0, x), axis=-1, keepdims=True)
    return jnp.where(is_a, sa, sb) / HEAD_DIM


def _pair_norm(raw, gain, is_a):
    return raw * lax.rsqrt(_per_head_mean(raw * raw, is_a) + NORM_EPS) * gain


def _pair_norm_bwd(raw, gain, dnormed, is_a):
    r = lax.rsqrt(_per_head_mean(raw * raw, is_a) + NORM_EPS)
    xh = raw * r
    dgain = jnp.sum(dnormed * xh, axis=0, keepdims=True)
    dxh = dnormed * gain
    return r * (dxh - xh * _per_head_mean(dxh * xh, is_a)), dgain


def _fold_heads(x):
    i = lax.broadcasted_iota(jnp.int32, (LANES, LANES), 0)
    j = lax.broadcasted_iota(jnp.int32, (LANES, LANES), 1)
    fold = ((i == j) | (i == j + HEAD_DIM) | (i + HEAD_DIM == j)).astype(F32)
    return _dot(x, fold, NN, precision=lax.Precision.HIGHEST)


def _fold_row(ref):
    ref[...] = _fold_heads(jnp.broadcast_to(ref[...], (8, LANES)))[0:1, :]


def _as_col(row):
    return jnp.broadcast_to(row, (LANES, row.shape[1])).T[:, 0:1]


def _as_row(col):
    return jnp.broadcast_to(col, (col.shape[0], LANES)).T[0:1, :]


def _tri_mask(t, keys_on_rows):
    r = lax.broadcasted_iota(jnp.int32, (t, t), 0)
    c = lax.broadcasted_iota(jnp.int32, (t, t), 1)
    return (r <= c) if keys_on_rows else (r >= c)


def _fox_fwd(proj, c_row, gq2, gk2, n_heads, name, t=256):
    S = proj.shape[0]
    H = n_heads
    P = H // 2
    t = min(t, S)
    nq = S // t

    def body(q_ref, k_ref, v_ref, cr_ref, gq_ref, gk_ref, o_ref, lse_ref, qs_s, kb_s, vb_s):
        is_a = _lane_is_a()
        qn = _pair_norm(q_ref[...], gq_ref[...], is_a) * 0.125
        qs_s[0] = jnp.where(is_a, qn, 0.0).astype(BF16)
        qs_s[1] = jnp.where(is_a, 0.0, qn).astype(BF16)
        kb_s[...] = _pair_norm(k_ref[...], gk_ref[...], is_a).astype(BF16)
        vb_s[...] = v_ref[...].astype(BF16)
        causal = _tri_mask(t, False)
        for i in range(nq):
            t0 = i * t
            rows = slice(t0, t0 + t)
            o_pair = None
            for a in range(2):
                qi = qs_s[a, rows, :]
                ci = _as_col(cr_ref[a, :, rows])
                s_d = jnp.where(causal, _dot(qi, kb_s[rows, :], NT) + ci - cr_ref[a, :, rows], NEG)
                m = jnp.max(s_d, axis=-1, keepdims=True)
                if i > 0:
                    s_l = _dot(qi, kb_s[0:t0, :], NT) + ci - cr_ref[a, :, 0:t0]
                    m = jnp.maximum(m, jnp.max(s_l, axis=-1, keepdims=True))
                p_d = jnp.exp(s_d - m)
                l = jnp.sum(p_d, axis=-1, keepdims=True)
                acc = _dot(p_d.astype(BF16), vb_s[rows, :], NN)
                if i > 0:
                    p_l = jnp.exp(s_l - m)
                    l = l + jnp.sum(p_l, axis=-1, keepdims=True)
                    acc = acc + _dot(p_l.astype(BF16), vb_s[0:t0, :], NN)
                o_a = acc / l
                lse_ref[a, :, rows] = _as_row(m + jnp.log(l))
                o_pair = o_a if a == 0 else jnp.where(is_a, o_pair, o_a)
            o_ref[rows, :] = o_pair.astype(BF16)

    def cols(off):
        return pl.BlockSpec((S, LANES), lambda p: (0, off + p))

    rowv = pl.BlockSpec((2, 1, S), lambda p: (p, 0, 0))
    gain = pl.BlockSpec((1, LANES), lambda p: (0, 0))
    return _pcall(body, name=name, grid=(P,), in_specs=[cols(0), cols(P), cols(2 * P), rowv, gain, gain],
                  out_specs=[cols(0), rowv],
                  out_shape=[jax.ShapeDtypeStruct((S, H * HEAD_DIM), BF16), jax.ShapeDtypeStruct((H, 1, S), F32)],
                  scratch_shapes=[pltpu.VMEM((2, S, LANES), BF16), pltpu.VMEM((S, LANES), BF16), pltpu.VMEM((S, LANES), BF16)],
                  compiler_params=_params(("parallel",)))(proj, proj, proj, c_row, gq2, gk2)


def _fox_bwd(proj, c_row, gq2, gk2, lse_row, do, n_heads, name, t=256):
    S = proj.shape[0]
    H = n_heads
    P = H // 2
    t = min(t, S)
    nq = S // t
    assert t % LANES == 0

    def body(q_ref, k_ref, v_ref, cr_ref, gq_ref, gk_ref, lr_ref, do_ref,
             dq_ref, dk_ref, dv_ref, dc_ref, dgq_ref, dgk_ref,
             qs_s, kb_s, kt_s, vb_s, dob_s, dq_s, dk_s, dv_s, dcs_s, cc_s):
        is_a = _lane_is_a()
        for a in range(2):
            for i in range(nq):
                cc_s[a, i * t:(i + 1) * t, :] = _as_col(cr_ref[a, :, i * t:(i + 1) * t])
        qn = _pair_norm(q_ref[...], gq_ref[...], is_a) * 0.125
        qs_s[0] = jnp.where(is_a, qn, 0.0).astype(BF16)
        qs_s[1] = jnp.where(is_a, 0.0, qn).astype(BF16)
        kn = _pair_norm(k_ref[...], gk_ref[...], is_a)
        kb_s[...] = kn.astype(BF16)
        kt_s[0] = jnp.where(is_a, kn, 0.0).T.astype(BF16)
        kt_s[1] = jnp.where(is_a, 0.0, kn).T.astype(BF16)
        vb_s[...] = v_ref[...].astype(BF16)
        dov = do_ref[...]
        dob_s[0] = jnp.where(is_a, dov, 0.0).astype(BF16)
        dob_s[1] = jnp.where(is_a, 0.0, dov).astype(BF16)
        dk_s[...] = jnp.zeros((S, LANES), F32)
        dv_s[...] = jnp.zeros((S, LANES), F32)
        dcs_s[...] = jnp.zeros((2, S, LANES), F32)
        causal = _tri_mask(t, True)
        for i in range(nq):
            t0 = i * t
            rows = slice(t0, t0 + t)
            dq_t = jnp.zeros((LANES, t), F32)
            for a in range(2):
                qi = qs_s[a, rows, :]
                doi = dob_s[a, rows, :]
                cri = cr_ref[a, :, rows]
                lri = lr_ref[a, :, rows]

                def probs(keys, masked, a=a, qi=qi, doi=doi, cri=cri, lri=lri):
                    p_t = jnp.exp(_dot(kb_s[keys, :], qi, NT) + cri - cc_s[a, keys, :] - lri)
                    if masked:
                        p_t = jnp.where(causal, p_t, 0.0)
                    return p_t, _dot(vb_s[keys, :], doi, NT)

                parts = [(rows,) + probs(rows, True)]
                if i > 0:
                    parts.append((slice(0, t0),) + probs(slice(0, t0), False))
                delta = sum(jnp.sum(p_t * dp_t, axis=0, keepdims=True) for _, p_t, dp_t in parts)
                for keys, p_t, dp_t in parts:
                    ds_t = p_t * (dp_t - delta)
                    dsb = ds_t.astype(BF16)
                    dv_s[keys, :] += _dot(p_t.astype(BF16), doi, NN)
                    dk_s[keys, :] += _dot(dsb, qi, NN)
                    dq_t = dq_t + _dot(kt_s[a, :, keys], dsb, NN)
                    dcs_s[a, keys, :] += sum(ds_t[:, b * LANES:(b + 1) * LANES] for b in range(t // LANES))
            dq_s[rows, :] = dq_t.T
        first = pl.program_id(0) == 0
        last = pl.program_id(0) == P - 1
        dq_raw, dgq = _pair_norm_bwd(q_ref[...], gq_ref[...], dq_s[...] * 0.125, is_a)
        dq_ref[...] = dq_raw.astype(BF16)
        _accumulate(dgq_ref, dgq, first)
        dk_raw, dgk = _pair_norm_bwd(k_ref[...], gk_ref[...], dk_s[...], is_a)
        dk_ref[...] = dk_raw.astype(BF16)
        _accumulate(dgk_ref, dgk, first)
        dv_ref[...] = dv_s[...].astype(BF16)
        for a in range(2):
            for i in range(nq):
                rows = slice(i * t, (i + 1) * t)
                dc_ref[a, :, rows] = _as_row(-jnp.sum(dcs_s[a, rows, :], axis=1, keepdims=True))

        @pl.when(last)
        def _():
            _fold_row(dgq_ref)
            _fold_row(dgk_ref)

    def cols(off):
        return pl.BlockSpec((S, LANES), lambda p: (0, off + p))

    rowv = pl.BlockSpec((2, 1, S), lambda p: (p, 0, 0))
    gain = pl.BlockSpec((1, LANES), lambda p: (0, 0))
    wide = jax.ShapeDtypeStruct((S, H * HEAD_DIM), BF16)
    gs = jax.ShapeDtypeStruct((1, LANES), F32)
    return _pcall(body, name=name, grid=(P,),
                  in_specs=[cols(0), cols(P), cols(2 * P), rowv, gain, gain, rowv, cols(0)],
                  out_specs=[cols(0), cols(0), cols(0), rowv, gain, gain],
                  out_shape=[wide, wide, wide, jax.ShapeDtypeStruct((H, 1, S), F32), gs, gs],
                  scratch_shapes=[pltpu.VMEM((2, S, LANES), BF16), pltpu.VMEM((S, LANES), BF16), pltpu.VMEM((2, LANES, S), BF16),
                                  pltpu.VMEM((S, LANES), BF16), pltpu.VMEM((2, S, LANES), BF16)]
                  + [pltpu.VMEM((S, LANES), F32)] * 3 + [pltpu.VMEM((2, S, LANES), F32), pltpu.VMEM((2, S, 1), F32)],
                  compiler_params=_params(("arbitrary",)))(proj, proj, proj, c_row, gq2, gk2, lse_row, do)


def _bucket_onehot():
    W = WINDOW
    dist = np.arange(W)[:, None] + W - np.arange(2 * W)[None, :]
    n = np.maximum(dist, 0)
    max_exact = N_BUCKETS // 2
    large = max_exact + (np.log(np.maximum(n, 1) / max_exact) / np.log(REL_MAX_DIST / max_exact)
                         * (N_BUCKETS - max_exact)).astype(np.int32)
    large = np.minimum(large, N_BUCKETS - 1)
    bucket = np.where(n < max_exact, n, large).astype(np.int32)
    valid = (dist >= 0) & (dist < W)
    onehot = (bucket[None] == np.arange(N_BUCKETS)[:, None, None]) & valid[None]
    return onehot.reshape(N_BUCKETS, W * 2 * W).astype(np.float32)


def _bias_expand(rel_bias_t, onehot, name, tn=4096):
    HQ, NB = rel_bias_t.shape
    L = onehot.shape[1]

    def body(r_ref, oh_ref, out_ref):
        out_ref[...] = _dot(r_ref[...], oh_ref[...].astype(F32), NN, precision=lax.Precision.HIGHEST)

    return _pcall(body, name=name, grid=(L // tn,),
                  in_specs=[pl.BlockSpec((HQ, NB), lambda i: (0, 0)), pl.BlockSpec((NB, tn), lambda i: (0, i))],
                  out_specs=pl.BlockSpec((HQ, tn), lambda i: (0, i)),
                  out_shape=jax.ShapeDtypeStruct((HQ, L), F32), compiler_params=_params(("parallel",)))(rel_bias_t, onehot)


def _bias_reduce(dbias, onehot, name, tk=4096):
    HQ, L = dbias.shape
    NB = onehot.shape[0]

    def body(d_ref, oh_ref, out_ref):
        part = _dot(d_ref[...], oh_ref[...].astype(F32), NT, precision=lax.Precision.HIGHEST)
        _accumulate(out_ref, part, pl.program_id(0) == 0)

    return _pcall(body, name=name, grid=(L // tk,),
                  in_specs=[pl.BlockSpec((HQ, tk), lambda i: (0, i)), pl.BlockSpec((NB, tk), lambda i: (0, i))],
                  out_specs=pl.BlockSpec((HQ, NB), lambda i: (0, 0)),
                  out_shape=jax.ShapeDtypeStruct((HQ, NB), F32), compiler_params=_params(("arbitrary",)))(dbias, onehot)


def _stacked_query_index(n_rows_or_cols_axis, shape):
    idx = lax.broadcasted_iota(jnp.int32, shape, n_rows_or_cols_axis)
    return jnp.where(idx >= WINDOW, idx - WINDOW, idx)


def _swa_fwd(qproj, kk, vv, bias_ab, sink_col, gq2, gk2, name):
    S, HQD = qproj.shape
    KVH = kk.shape[0]
    PP = HQD // LANES
    NP = PP // KVH
    W = WINDOW
    nb = S // W

    def body(q_ref, k_ref, v_ref, bias_ref, sink_ref, gq_ref, gk_ref, o_ref, lse_ref, qs_s, kb_s, vb_s):
        is_a = _lane_is_a()
        qn = _pair_norm(q_ref[...], gq_ref[...], is_a) * 0.125
        qs_s[0] = jnp.where(is_a, qn, 0.0).astype(BF16)
        qs_s[1] = jnp.where(is_a, 0.0, qn).astype(BF16)
        kb_s[...] = _pair_norm(k_ref[...], gk_ref[...], is_a).astype(BF16)
        vb_s[...] = v_ref[...].astype(BF16)
        sink = sink_ref[...]
        qi1 = _stacked_query_index(0, (2 * W, W))
        first_valid = lax.broadcasted_iota(jnp.int32, (2 * W, W), 1) <= qi1
        qi2 = _stacked_query_index(0, (2 * W, 2 * W))
        key2 = lax.broadcasted_iota(jnp.int32, (2 * W, 2 * W), 1)
        band_valid = (key2 > qi2) & (key2 <= qi2 + W)
        for n in range(nb):
            rows = slice(n * W, (n + 1) * W)
            keys = slice(0, W) if n == 0 else slice((n - 1) * W, (n + 1) * W)
            lhs = jnp.concatenate([qs_s[0, rows, :], qs_s[1, rows, :]], axis=0)
            s = _dot(lhs, kb_s[keys, :], NT) + (bias_ref[:, W:2 * W] if n == 0 else bias_ref[...])
            s = jnp.where(first_valid if n == 0 else band_valid, s, NEG)
            m = jnp.maximum(jnp.max(s, axis=-1, keepdims=True), sink)
            e = jnp.exp(s - m)
            l = jnp.sum(e, axis=-1, keepdims=True) + jnp.exp(sink - m)
            o_ab = _dot(e.astype(BF16), vb_s[keys, :], NN) / l
            o_ref[rows, :] = jnp.where(is_a, o_ab[0:W, :], o_ab[W:2 * W, :]).astype(BF16)
            lse_ref[n] = _as_row(m + jnp.log(l))

    qcols = pl.BlockSpec((S, LANES), lambda a, g: (0, a * NP + g))
    kvs = pl.BlockSpec((None, S, LANES), lambda a, g: (a, 0, 0))
    gain = pl.BlockSpec((1, LANES), lambda a, g: (0, 0))
    return _pcall(body, name=name, grid=(KVH, NP),
                  in_specs=[qcols, kvs, kvs, pl.BlockSpec((None, 2 * W, 2 * W), lambda a, g: (a * NP + g, 0, 0)),
                            pl.BlockSpec((None, 2 * W, 1), lambda a, g: (a * NP + g, 0, 0)), gain, gain],
                  out_specs=[qcols, pl.BlockSpec((None, nb, 1, 2 * W), lambda a, g: (a * NP + g, 0, 0, 0))],
                  out_shape=[jax.ShapeDtypeStruct((S, HQD), BF16), jax.ShapeDtypeStruct((PP, nb, 1, 2 * W), F32)],
                  scratch_shapes=[pltpu.VMEM((2, S, LANES), BF16), pltpu.VMEM((S, LANES), BF16), pltpu.VMEM((S, LANES), BF16)],
                  compiler_params=_params(("parallel", "parallel")))(qproj, kk, vv, bias_ab, sink_col, gq2, gk2)


def _swa_bwd(qproj, kk, vv, bias_t_ab, sink_row, gq2, gk2, lse_row, do, name):
    S, HQD = qproj.shape
    KVH = kk.shape[0]
    PP = HQD // LANES
    NP = PP // KVH
    W = WINDOW
    nb = S // W

    def body(q_ref, k_ref, v_ref, bias_ref, sink_ref, gq_ref, gk_ref, lr_ref, do_ref,
             dq_ref, dk_ref, dv_ref, db_ref, dsink_ref, dgq_ref, dgk_ref,
             qs_s, kb_s, kt_s, vb_s, dob_s, dq_s, dk_s, dv_s):
        a, g = pl.program_id(0), pl.program_id(1)
        is_a = _lane_is_a()
        qn = _pair_norm(q_ref[...], gq_ref[...], is_a) * 0.125
        qs_s[0] = jnp.where(is_a, qn, 0.0).astype(BF16)
        qs_s[1] = jnp.where(is_a, 0.0, qn).astype(BF16)
        kn = _pair_norm(k_ref[...], gk_ref[...], is_a)
        kb_s[...] = kn.astype(BF16)
        kt_s[...] = kn.T.astype(BF16)
        vb_s[...] = v_ref[...].astype(BF16)
        dov = do_ref[...]
        dob_s[0] = jnp.where(is_a, dov, 0.0).astype(BF16)
        dob_s[1] = jnp.where(is_a, 0.0, dov).astype(BF16)
        sink = sink_ref[...]

        @pl.when(g == 0)
        def _():
            dk_s[...] = jnp.zeros((S, LANES), F32)
            dv_s[...] = jnp.zeros((S, LANES), F32)

        qi1 = _stacked_query_index(1, (W, 2 * W))
        first_valid = lax.broadcasted_iota(jnp.int32, (W, 2 * W), 0) <= qi1
        qi2 = _stacked_query_index(1, (2 * W, 2 * W))
        key2 = lax.broadcasted_iota(jnp.int32, (2 * W, 2 * W), 0)
        band_valid = (key2 > qi2) & (key2 <= qi2 + W)
        head_rows = lax.broadcasted_iota(jnp.int32, (LANES, W), 0) < HEAD_DIM
        db = jnp.zeros((2 * W, 2 * W), F32)
        dsk = jnp.zeros((1, 2 * W), F32)
        pend_k = pend_v = None
        for n in range(nb):
            rows = slice(n * W, (n + 1) * W)
            keys = slice(0, W) if n == 0 else slice((n - 1) * W, (n + 1) * W)
            lhs_q = jnp.concatenate([qs_s[0, rows, :], qs_s[1, rows, :]], axis=0)
            lhs_do = jnp.concatenate([dob_s[0, rows, :], dob_s[1, rows, :]], axis=0)
            lse = lr_ref[n]
            s_t = _dot(kb_s[keys, :], lhs_q, NT) + (bias_ref[W:2 * W, :] if n == 0 else bias_ref[...])
            p_t = jnp.where(first_valid if n == 0 else band_valid, jnp.exp(s_t - lse), 0.0)
            dp_t = _dot(vb_s[keys, :], lhs_do, NT)
            delta = jnp.sum(p_t * dp_t, axis=0, keepdims=True)
            ds_t = p_t * (dp_t - delta)
            dsb = ds_t.astype(BF16)
            dsk = dsk - jnp.exp(sink - lse) * delta
            dv_band = _dot(p_t.astype(BF16), lhs_do, NN)
            dk_band = _dot(dsb, lhs_q, NN)
            dq_t = _dot(kt_s[:, keys], dsb, NN)
            dq_s[rows, :] = jnp.where(head_rows, dq_t[:, 0:W], dq_t[:, W:2 * W]).T
            if n == 0:
                db = jnp.concatenate([jnp.zeros((W, 2 * W), F32), ds_t], axis=0)
                pend_k, pend_v = dk_band, dv_band
            else:
                db = db + ds_t
                prev = slice((n - 1) * W, n * W)
                dk_s[prev, :] += pend_k + dk_band[0:W, :]
                dv_s[prev, :] += pend_v + dv_band[0:W, :]
                pend_k, pend_v = dk_band[W:2 * W, :], dv_band[W:2 * W, :]
        tail = slice((nb - 1) * W, nb * W)
        dk_s[tail, :] += pend_k
        dv_s[tail, :] += pend_v
        db_ref[...] = db
        dsink_ref[0] = jnp.broadcast_to(jnp.sum(dsk[:, 0:W], axis=1, keepdims=True), (1, LANES))
        dsink_ref[1] = jnp.broadcast_to(jnp.sum(dsk[:, W:2 * W], axis=1, keepdims=True), (1, LANES))
        dq_raw, dgq = _pair_norm_bwd(q_ref[...], gq_ref[...], dq_s[...] * 0.125, is_a)
        dq_ref[...] = dq_raw.astype(BF16)
        _accumulate(dgq_ref, dgq, jnp.logical_and(a == 0, g == 0))

        @pl.when(jnp.logical_and(a == KVH - 1, g == NP - 1))
        def _():
            _fold_row(dgq_ref)

        @pl.when(g == NP - 1)
        def _():
            dk_raw, dgk = _pair_norm_bwd(k_ref[...], gk_ref[...], _fold_heads(dk_s[...]), is_a)
            dk_ref[...] = dk_raw
            _accumulate(dgk_ref, dgk, a == 0)
            dv_ref[...] = _fold_heads(dv_s[...])

    qcols = pl.BlockSpec((S, LANES), lambda a, g: (0, a * NP + g))
    kvs = pl.BlockSpec((None, S, LANES), lambda a, g: (a, 0, 0))
    sq = pl.BlockSpec((None, 2 * W, 2 * W), lambda a, g: (a * NP + g, 0, 0))
    gain = pl.BlockSpec((1, LANES), lambda a, g: (0, 0))
    ks = jax.ShapeDtypeStruct((KVH, S, LANES), F32)
    gs = jax.ShapeDtypeStruct((1, LANES), F32)
    return _pcall(body, name=name, grid=(KVH, NP),
                  in_specs=[qcols, kvs, kvs, sq, pl.BlockSpec((None, 1, 2 * W), lambda a, g: (a * NP + g, 0, 0)), gain, gain,
                            pl.BlockSpec((None, nb, 1, 2 * W), lambda a, g: (a * NP + g, 0, 0, 0)), qcols],
                  out_specs=[qcols, kvs, kvs, sq, pl.BlockSpec((2, 1, LANES), lambda a, g: (a * NP + g, 0, 0)), gain, gain],
                  out_shape=[jax.ShapeDtypeStruct((S, HQD), BF16), ks, ks, jax.ShapeDtypeStruct((PP, 2 * W, 2 * W), F32),
                             jax.ShapeDtypeStruct((2 * PP, 1, LANES), F32), gs, gs],
                  scratch_shapes=[pltpu.VMEM((2, S, LANES), BF16), pltpu.VMEM((S, LANES), BF16), pltpu.VMEM((LANES, S), BF16),
                                  pltpu.VMEM((S, LANES), BF16), pltpu.VMEM((2, S, LANES), BF16)] + [pltpu.VMEM((S, LANES), F32)] * 3,
                  compiler_params=_params(("arbitrary", "arbitrary")))(qproj, kk, vv, bias_t_ab, sink_row, gq2, gk2, lse_row, do)


def _adamw_update(w, g, m, v):
    m2 = ADAM_B1 * m + (1.0 - ADAM_B1) * g
    v2 = ADAM_B2 * v + (1.0 - ADAM_B2) * jnp.square(g)
    m_hat = m2 / (1.0 - ADAM_B1 ** ADAM_STEP)
    v_hat = v2 / (1.0 - ADAM_B2 ** ADAM_STEP)
    return -ADAM_LR * (m_hat / (jnp.sqrt(v_hat) + ADAM_EPS) + ADAM_WD * w), m2, v2


def _adamw(w, g, m, v, name, tr=256, tc=256):
    R, C = w.shape
    tr = min(tr, R)
    if R % tr == 0:
        grid, blk = (R // tr,), pl.BlockSpec((tr, C), lambda i: (i, 0))
    else:
        assert C % tc == 0
        grid, blk = (C // tc,), pl.BlockSpec((R, tc), lambda i: (0, i))

    def body(w_ref, g_ref, m_ref, v_ref, d_ref, m2_ref, v2_ref):
        d_ref[...], m2_ref[...], v2_ref[...] = _adamw_update(w_ref[...], g_ref[...], m_ref[...], v_ref[...])

    return _pcall(body, name=name, grid=grid, in_specs=[blk] * 4, out_specs=[blk] * 3,
                  out_shape=[jax.ShapeDtypeStruct((R, C), F32)] * 3, compiler_params=_params(("parallel",)))(w, g, m, v)


def _adamw_two_layers(w, g0, g1, m, v, name, tr=256):
    R, C = g0.shape
    assert R % tr == 0 and w.shape == (2 * R, C)
    nr = R // tr

    def body(w_ref, g0_ref, g1_ref, m_ref, v_ref, g_ref, d_ref, m2_ref, v2_ref):
        g = jnp.where(pl.program_id(0) == 0, g0_ref[...], g1_ref[...])
        g_ref[...] = g
        d_ref[...], m2_ref[...], v2_ref[...] = _adamw_update(w_ref[...], g, m_ref[...], v_ref[...])

    both = pl.BlockSpec((tr, C), lambda l, i: (l * nr + i, 0))
    first = pl.BlockSpec((tr, C), lambda l, i: (i * (1 - l) + (nr - 1) * l, 0))
    second = pl.BlockSpec((tr, C), lambda l, i: (i * l, 0))
    return _pcall(body, name=name, grid=(2, nr), in_specs=[both, first, second, both, both], out_specs=[both] * 4,
                  out_shape=[jax.ShapeDtypeStruct((2 * R, C), F32)] * 4,
                  compiler_params=_params(("arbitrary", "arbitrary")))(w, g0, g1, m, v)


def _sum_core_pair(arr, got, place, name, tr=512):
    P, hr, C = got.shape
    tr = tr if hr % tr == 0 else hr
    nt = hr // tr

    def body(place_ref, a_ref, g_ref, o_ref):
        o_ref[...] = (a_ref[...].astype(F32) + g_ref[...].astype(F32)).astype(o_ref.dtype)

    spec = pltpu.PrefetchScalarGridSpec(
        num_scalar_prefetch=1, grid=(P, nt),
        in_specs=[pl.BlockSpec((None, tr, C), lambda j, i, pr: (j, pr[1] * nt + i, 0)),
                  pl.BlockSpec((None, tr, C), lambda j, i, pr: (j, i, 0))],
        out_specs=pl.BlockSpec((None, tr, C), lambda j, i, pr: (j, i, 0)))
    return _pcall(body, name=name, grid_spec=spec, out_shape=jax.ShapeDtypeStruct(got.shape, BF16),
                  compiler_params=_params(("parallel", "parallel")))(place, arr, got)


def _sum_chips(pair, landed, place, name, tr=256):
    _, R, C = landed.shape
    tr = tr if R % tr == 0 else R

    def body(place_ref, p_ref, l_ref, o_ref):
        acc = p_ref[...].astype(F32)
        for k in range(3):
            acc = acc + l_ref[k].astype(F32)
        o_ref[...] = acc

    spec = pltpu.PrefetchScalarGridSpec(
        num_scalar_prefetch=1, grid=(R // tr,),
        in_specs=[pl.BlockSpec((None, tr, C), lambda i, pr: (pr[0], i, 0)), pl.BlockSpec((3, tr, C), lambda i, pr: (0, i, 0))],
        out_specs=pl.BlockSpec((None, tr, C), lambda i, pr: (pr[1], i, 0)))
    return _pcall(body, name=name, grid_spec=spec, out_shape=jax.ShapeDtypeStruct((2, R, C), F32),
                  compiler_params=_params(("parallel",)))(place, pair, landed)


def _sum_parts(parts, name, out_dtype, tr=128):
    P, R, C = parts.shape
    tr = min(tr, R)
    assert R % tr == 0, (R, tr)

    def body(p_ref, o_ref):
        acc = p_ref[0].astype(F32)
        for k in range(1, P):
            acc = acc + p_ref[k].astype(F32)
        o_ref[...] = acc.astype(o_ref.dtype)

    return _pcall(body, name=name, grid=(R // tr,), in_specs=[pl.BlockSpec((P, tr, C), lambda i: (0, i, 0))],
                  out_specs=pl.BlockSpec((tr, C), lambda i: (i, 0)),
                  out_shape=jax.ShapeDtypeStruct((R, C), out_dtype), compiler_params=_params(("parallel",)))(parts)


def _place():
    x, y, c = lax.axis_index("x"), lax.axis_index("y"), lax.axis_index("c")
    others = [(1 - x, y), (x, 1 - y), (1 - x, 1 - y)]
    return x, y, c, others


def _half_rows(ref, hh, lead=()):
    hr = ref.shape[-2] // 2
    return ref.at[(*lead, pl.ds(pl.multiple_of(hh * hr, 16), hr), slice(None))]


def _sem_arrays(*counts):
    return [pltpu.SemaphoreType.DMA((k,)) for k in counts]


SEM_SPEC = pl.BlockSpec(memory_space=pltpu.SEMAPHORE)
ANY_SPEC = pl.BlockSpec(memory_space=pl.ANY)
DATAFLOW = pltpu.SideEffectType.DATAFLOW_SIDE_EFFECTING


def _in_hbm(a):
    return pltpu.with_memory_space_constraint(a, pltpu.HBM)


def _gather_copies(srcs, lands, send_sems, recv_sems):
    x, y, c, others = _place()
    me = 2 * x + y

    def copy(w, k, dst_chip, to):
        return pltpu.make_async_remote_copy(src_ref=_half_rows(srcs[w], c), dst_ref=_half_rows(lands[w], c, (dst_chip,)),
                                            send_sem=send_sems.at[3 * w + k], recv_sem=recv_sems.at[3 * w + k],
                                            device_id=to, device_id_type=MESH)

    pairs = [(w, k, cx, cy) for w in range(len(srcs)) for k, (cx, cy) in enumerate(others)]
    return ([copy(w, k, me, (cx, cy, c)) for w, k, cx, cy in pairs],
            [copy(w, k, 2 * cx + cy, (cx, cy, c)) for w, k, cx, cy in pairs])


def _gather_start(shards, after, name):
    n = len(shards)

    def body(*refs):
        srcs, lands, send_sems, recv_sems, token = refs[:n], refs[n:2 * n], refs[2 * n + 1], refs[2 * n + 2], refs[-1]
        for cp in _gather_copies(srcs, lands, send_sems, recv_sems)[0]:
            cp.start()
        token[...] = jnp.zeros_like(token)

    lands = [lax.empty((N_CHIPS,) + s.shape, s.dtype) for s in shards]
    outs = _pcall(
        body, name=name, in_specs=[HBM_SPEC] * (2 * n) + [ANY_SPEC],
        out_specs=[SEM_SPEC, SEM_SPEC] + [HBM_SPEC] * (2 * n) + [VMEM_SPEC],
        out_shape=[pltpu.SemaphoreType.DMA((3 * n,)), pltpu.SemaphoreType.DMA((3 * n,))]
        + [pltpu.HBM(a.shape, a.dtype) for a in list(shards) + lands] + [jax.ShapeDtypeStruct((8, LANES), F32)],
        input_output_aliases={i: 2 + i for i in range(2 * n)},
        compiler_params=pltpu.CompilerParams(has_side_effects=DATAFLOW),
    )(*[_in_hbm(a) for a in list(shards) + lands], after)
    return outs[0], outs[1], outs[2:2 + n], outs[2 + n:2 + 2 * n], outs[-1]


def _gather_wait(started, after, name):
    send_sems, recv_sems, srcs, lands, _ = started
    n = len(srcs)

    def body(*refs):
        src_refs, land_refs, send_ref, recv_ref = refs[:n], refs[n:2 * n], refs[2 * n], refs[2 * n + 1]
        outgoing, incoming = _gather_copies(src_refs, land_refs, send_ref, recv_ref)
        for out_cp, in_cp in zip(outgoing, incoming):
            out_cp.wait_send()
            in_cp.wait_recv()

    outs = _pcall(
        body, name=name, in_specs=[HBM_SPEC] * (2 * n) + [SEM_SPEC, SEM_SPEC, ANY_SPEC], out_specs=[HBM_SPEC] * (2 * n),
        out_shape=[pltpu.HBM(a.shape, a.dtype) for a in list(srcs) + list(lands)],
        input_output_aliases={i: i for i in range(2 * n)},
        compiler_params=pltpu.CompilerParams(has_side_effects=DATAFLOW),
    )(*srcs, *lands, send_sems, recv_sems, after)
    return outs[:n], outs[n:]


def _gather_pass_on(shards, lands, name):
    n = len(shards)
    per = 4

    def body(*refs):
        srcs, bufs = refs[:n], refs[2 * n:3 * n]
        send_sems, recv_sems = refs[3 * n:]
        x, y, c, others = _place()
        me = 2 * x + y
        sibling = (x, y, 1 - c)

        def copy(w, k, src, dst):
            return pltpu.make_async_remote_copy(src_ref=src, dst_ref=dst, send_sem=send_sems.at[per * w + k],
                                                recv_sem=recv_sems.at[per * w + k], device_id=sibling, device_id_type=MESH)

        sends, recvs = [], []
        for w in range(n):
            for k, (cx, cy) in enumerate(others):
                mine, theirs = _half_rows(bufs[w], c, (2 * cx + cy,)), _half_rows(bufs[w], 1 - c, (2 * cx + cy,))
                sends.append(copy(w, k, mine, mine))
                recvs.append(copy(w, k, theirs, theirs))
            sends.append(copy(w, 3, srcs[w], bufs[w].at[me]))
            recvs.append(sends[-1])
        for cp in sends:
            cp.start()
        for snd, rcv in zip(sends, recvs):
            snd.wait_send()
            rcv.wait_recv()

    return _pcall(body, name=name, in_specs=[HBM_SPEC] * (2 * n), out_specs=[HBM_SPEC] * n,
                  out_shape=[jax.ShapeDtypeStruct(l.shape, l.dtype) for l in lands],
                  input_output_aliases={n + w: w for w in range(n)},
                  scratch_shapes=_sem_arrays(per * n, per * n))(*shards, *lands)


def _scatter_copies(srcs, lands, send_sems, recv_sems):
    x, y, c, others = _place()
    return [pltpu.make_async_remote_copy(src_ref=srcs[w].at[2 * cx + cy], dst_ref=lands[w].at[k],
                                         send_sem=send_sems.at[3 * w + k], recv_sem=recv_sems.at[3 * w + k],
                                         device_id=(cx, cy, c), device_id_type=MESH)
            for w in range(len(srcs)) for k, (cx, cy) in enumerate(others)]


def _scatter_start(parts, name):
    n = len(parts)

    def body(*refs):
        srcs, lands, send_sems, recv_sems, token = refs[:n], refs[n:2 * n], refs[2 * n], refs[2 * n + 1], refs[-1]
        for cp in _scatter_copies(srcs, lands, send_sems, recv_sems):
            cp.start()
        token[...] = jnp.zeros_like(token)

    lands = [lax.empty((3,) + p.shape[1:], p.dtype) for p in parts]
    outs = _pcall(
        body, name=name, in_specs=[HBM_SPEC] * (2 * n), out_specs=[SEM_SPEC, SEM_SPEC] + [HBM_SPEC] * (2 * n) + [VMEM_SPEC],
        out_shape=[pltpu.SemaphoreType.DMA((3 * n,)), pltpu.SemaphoreType.DMA((3 * n,))]
        + [pltpu.HBM(a.shape, a.dtype) for a in list(parts) + lands] + [jax.ShapeDtypeStruct((8, LANES), F32)],
        input_output_aliases={i: 2 + i for i in range(2 * n)},
        compiler_params=pltpu.CompilerParams(has_side_effects=DATAFLOW),
    )(*[_in_hbm(a) for a in list(parts) + lands])
    return (outs[0], outs[1], outs[2:2 + n], outs[2 + n:2 + 2 * n]), outs[-1]


def _scatter_wait(started, after, name):
    send_sems, recv_sems, srcs, lands = started
    n = len(srcs)

    def body(*refs):
        for cp in _scatter_copies(refs[:n], refs[n:2 * n], refs[2 * n], refs[2 * n + 1]):
            cp.wait_send()
            cp.wait_recv()

    outs = _pcall(
        body, name=name, in_specs=[HBM_SPEC] * (2 * n) + [SEM_SPEC, SEM_SPEC, ANY_SPEC], out_specs=[HBM_SPEC] * (2 * n),
        out_shape=[pltpu.HBM(a.shape, a.dtype) for a in list(srcs) + list(lands)],
        input_output_aliases={i: i for i in range(2 * n)},
        compiler_params=pltpu.CompilerParams(has_side_effects=DATAFLOW),
    )(*srcs, *lands, send_sems, recv_sems, after)
    return outs[:n], outs[n:]


def _split_start(plan, arrays, n_copies, after, name):
    n = len(arrays)

    def body(*refs):
        for cp in plan(refs[:n], refs[n + 1], refs[n + 2])[0]:
            cp.start()
        refs[-1][...] = jnp.zeros_like(refs[-1])

    outs = _pcall(
        body, name=name, in_specs=[HBM_SPEC] * n + [ANY_SPEC], out_specs=[SEM_SPEC, SEM_SPEC] + [HBM_SPEC] * n + [VMEM_SPEC],
        out_shape=[pltpu.SemaphoreType.DMA((n_copies,)), pltpu.SemaphoreType.DMA((n_copies,))]
        + [pltpu.HBM(a.shape, a.dtype) for a in arrays] + [jax.ShapeDtypeStruct((8, LANES), F32)],
        input_output_aliases={i: 2 + i for i in range(n)},
        compiler_params=pltpu.CompilerParams(has_side_effects=DATAFLOW),
    )(*[_in_hbm(a) for a in arrays], after)
    return (outs[0], outs[1], outs[2:2 + n]), outs[-1]


def _split_wait(plan, started, after, name):
    send_sems, recv_sems, arrays = started
    n = len(arrays)

    def body(*refs):
        outgoing, incoming = plan(refs[:n], refs[n], refs[n + 1])
        for cp in outgoing:
            cp.wait_send()
        for cp in incoming:
            cp.wait_recv()

    return _pcall(
        body, name=name, in_specs=[HBM_SPEC] * n + [SEM_SPEC, SEM_SPEC, ANY_SPEC], out_specs=[HBM_SPEC] * n,
        out_shape=[pltpu.HBM(a.shape, a.dtype) for a in arrays], input_output_aliases={i: i for i in range(n)},
        compiler_params=pltpu.CompilerParams(has_side_effects=DATAFLOW),
    )(*arrays, send_sems, recv_sems, after)


def _to_sibling(src, dst, k, send_sems, recv_sems):
    x, y, c, _ = _place()
    return pltpu.make_async_remote_copy(src_ref=src, dst_ref=dst, send_sem=send_sems.at[k], recv_sem=recv_sems.at[k],
                                        device_id=(x, y, 1 - c), device_id_type=MESH)


def _plan_pass_on(n):
    def plan(refs, send_sems, recv_sems):
        x, y, c, others = _place()
        cps = []
        for w in range(n):
            for k, (cx, cy) in enumerate(others):
                mine = _half_rows(refs[n + w], c, (2 * cx + cy,))
                cps.append(_to_sibling(mine, mine, 4 * w + k, send_sems, recv_sems))
            cps.append(_to_sibling(refs[w], refs[n + w].at[2 * x + y], 4 * w + 3, send_sems, recv_sems))
        return cps, cps
    return plan


def _plan_swap_halves(n):
    def plan(refs, send_sems, recv_sems):
        c = lax.axis_index("c")
        cps = [_to_sibling(_half_rows(refs[w], 1 - c, (slice(None),)), refs[n + w], w, send_sems, recv_sems) for w in range(n)]
        return cps, cps
    return plan


def _plan_share_halves(n):
    def plan(refs, send_sems, recv_sems):
        c = lax.axis_index("c")
        cps = [_to_sibling(refs[w].at[c], refs[w].at[c], w, send_sems, recv_sems) for w in range(n)]
        return cps, cps
    return plan


def _plan_gather_small(refs, send_sems, recv_sems):
    x, y, c, _ = _place()
    flips = [(dx, dy, dc) for dx in (0, 1) for dy in (0, 1) for dc in (0, 1)][1:]
    flip = lambda v, d: 1 - v if d else v
    cps = [pltpu.make_async_remote_copy(src_ref=refs[0], dst_ref=refs[1].at[4 * x + 2 * y + c], send_sem=send_sems.at[k],
                                        recv_sem=recv_sems.at[k], device_id=(flip(x, dx), flip(y, dy), flip(c, dc)),
                                        device_id_type=MESH)
           for k, (dx, dy, dc) in enumerate(flips)]
    return cps, cps


def _sum_gathered_small(gathered, own, place, name):
    _, M, C = gathered.shape

    def body(place_ref, g_ref, own_ref, o_ref):
        me = 2 * place_ref[0] + place_ref[1]
        acc = jnp.zeros((M, C), F32)
        for k in range(8):
            acc = acc + jnp.where(me == k, own_ref[...], g_ref[k])
        o_ref[...] = acc

    spec = pltpu.PrefetchScalarGridSpec(
        num_scalar_prefetch=1, grid=(1,),
        in_specs=[pl.BlockSpec((8, M, C), lambda i, pr: (0, 0, 0)), pl.BlockSpec((M, C), lambda i, pr: (0, 0))],
        out_specs=pl.BlockSpec((M, C), lambda i, pr: (0, 0)))
    return _pcall(body, name=name, grid_spec=spec, out_shape=jax.ShapeDtypeStruct((M, C), F32),
                  compiler_params=_params(("arbitrary",)))(place, gathered, own)


def _allgather_group(shards, name):
    n = len(shards)
    per = 7

    def body(*refs):
        ins, outs = refs[:n], refs[n:2 * n]
        send_sems, recv_sems = refs[2 * n:]
        x, y, c, others = _place()
        me = 2 * x + y
        sibling = (x, y, 1 - c)

        def copy(w, k, src, dst, to):
            return pltpu.make_async_remote_copy(src_ref=src, dst_ref=dst, send_sem=send_sems.at[per * w + k],
                                                recv_sem=recv_sems.at[per * w + k], device_id=to, device_id_type=MESH)

        first = [copy(w, k, _half_rows(ins[w], c), _half_rows(outs[w], c, (me,)), (cx, cy, c))
                 for w in range(n) for k, (cx, cy) in enumerate(others)]
        own = [copy(w, 6, ins[w], outs[w].at[me], sibling) for w in range(n)]
        for cp in first + own:
            cp.start()
        passed = []
        for w in range(n):
            for k, (cx, cy) in enumerate(others):
                landed = _half_rows(outs[w], c, (2 * cx + cy,))
                copy(w, k, landed, landed, sibling).wait_recv()
                passed.append(copy(w, 3 + k, landed, landed, sibling))
                passed[-1].start()
        for w in range(n):
            for k, (cx, cy) in enumerate(others):
                theirs = _half_rows(outs[w], 1 - c, (2 * cx + cy,))
                copy(w, 3 + k, theirs, theirs, sibling).wait_recv()
            own[w].wait_recv()
        for cp in first + passed + own:
            cp.wait_send()

    return _pcall(body, name=name, in_specs=[HBM_SPEC] * n, out_specs=[HBM_SPEC] * n,
                  out_shape=[jax.ShapeDtypeStruct((N_CHIPS,) + s.shape, s.dtype) for s in shards],
                  scratch_shapes=_sem_arrays(per * n, per * n))(*shards)


def _swap_halves_group(arrs, name):
    n = len(arrs)

    def body(*refs):
        ins, gots = refs[:n], refs[n:2 * n]
        send_sems, recv_sems = refs[2 * n:]
        x, y, c, _ = _place()
        swaps = [pltpu.make_async_remote_copy(src_ref=_half_rows(ins[w], 1 - c, (slice(None),)), dst_ref=gots[w],
                                              send_sem=send_sems.at[w], recv_sem=recv_sems.at[w],
                                              device_id=(x, y, 1 - c), device_id_type=MESH) for w in range(n)]
        for cp in swaps:
            cp.start()
        for cp in swaps:
            cp.wait()

    half_shapes = [jax.ShapeDtypeStruct((a.shape[0], a.shape[1] // 2, a.shape[2]), a.dtype) for a in arrs]
    return _pcall(body, name=name, in_specs=[HBM_SPEC] * n, out_specs=[HBM_SPEC] * n, out_shape=half_shapes,
                  scratch_shapes=_sem_arrays(n, n))(*arrs)


def _scatter_group(parts, name):
    n = len(parts)

    def body(*refs):
        ins, outs = refs[:n], refs[n:2 * n]
        send_sems, recv_sems = refs[2 * n:]
        x, y, c, others = _place()

        def copy(w, k, src_chip, to):
            return pltpu.make_async_remote_copy(src_ref=ins[w].at[src_chip], dst_ref=outs[w].at[k],
                                                send_sem=send_sems.at[3 * w + k], recv_sem=recv_sems.at[3 * w + k],
                                                device_id=to, device_id_type=MESH)

        sends = [copy(w, k, 2 * cx + cy, (cx, cy, c)) for w in range(n) for k, (cx, cy) in enumerate(others)]
        for cp in sends:
            cp.start()
        for cp in sends:
            cp.wait()

    return _pcall(body, name=name, in_specs=[HBM_SPEC] * n, out_specs=[HBM_SPEC] * n,
                  out_shape=[jax.ShapeDtypeStruct((3,) + p.shape[1:], p.dtype) for p in parts],
                  scratch_shapes=_sem_arrays(3 * n, 3 * n))(*parts)


def _share_halves_group(halves, name):
    n = len(halves)

    def body(*refs):
        bufs = refs[n:2 * n]
        send_sems, recv_sems = refs[2 * n:]
        x, y, c, _ = _place()
        swaps = [pltpu.make_async_remote_copy(src_ref=bufs[w].at[c], dst_ref=bufs[w].at[c], send_sem=send_sems.at[w],
                                              recv_sem=recv_sems.at[w], device_id=(x, y, 1 - c), device_id_type=MESH)
                 for w in range(n)]
        for cp in swaps:
            cp.start()
        for w in range(n):
            swaps[w].wait_send()
            pltpu.make_async_remote_copy(src_ref=bufs[w].at[c], dst_ref=bufs[w].at[1 - c], send_sem=send_sems.at[w],
                                         recv_sem=recv_sems.at[w], device_id=(x, y, 1 - c), device_id_type=MESH).wait_recv()

    return _pcall(body, name=name, in_specs=[HBM_SPEC] * n, out_specs=[HBM_SPEC] * n,
                  out_shape=[jax.ShapeDtypeStruct(h.shape, h.dtype) for h in halves],
                  input_output_aliases={w: w for w in range(n)},
                  scratch_shapes=_sem_arrays(n, n))(*halves)


def _allgather_small(blk, name):
    M, C = blk.shape

    def body(x_ref, out_ref, send_sems, recv_sems, local_sem):
        x, y, c, others = _place()
        me, sibling = (x, y, c), (x, y, 1 - c)

        def rows(px, py, pc):
            return out_ref.at[4 * px + 2 * py + pc]

        def copy(k, block, to, src=None):
            return pltpu.make_async_remote_copy(src_ref=rows(*block) if src is None else src, dst_ref=rows(*block),
                                                send_sem=send_sems.at[k], recv_sem=recv_sems.at[k], device_id=to, device_id_type=MESH)

        mine = pltpu.make_async_copy(x_ref, rows(*me), local_sem)
        mine.start()
        first = [copy(0, me, sibling, src=x_ref)]
        first += [copy(1 + j, me, (*chip, c), src=x_ref) for j, chip in enumerate(others)]
        for cp in first:
            cp.start()
        passed = [copy(4 + j, (*chip, c), sibling) for j, chip in enumerate(others)]
        for j, chip in enumerate(others):
            copy(1 + j, (*chip, c), me).wait_recv()
            passed[j].start()
        copy(0, sibling, me).wait_recv()
        for j, chip in enumerate(others):
            copy(4 + j, (*chip, 1 - c), me).wait_recv()
        for cp in first + passed:
            cp.wait_send()
        mine.wait()

    return _pcall(body, name=name, in_specs=[VMEM_SPEC], out_specs=VMEM_SPEC,
                  out_shape=jax.ShapeDtypeStruct((8, M, C), blk.dtype),
                  scratch_shapes=[pltpu.SemaphoreType.DMA((7,)), pltpu.SemaphoreType.DMA((7,)), pltpu.SemaphoreType.DMA])(blk)


def _pack_rows(n_elems, width=PACK_W, align=PACK_ROW_ALIGN):
    rows = -(-n_elems // width)
    return -(-rows // align) * align


def _pack(arrays, dtype, width=PACK_W, align=PACK_ROW_ALIGN):
    flat = jnp.concatenate([a.astype(dtype).reshape(-1) for a in arrays])
    rows = _pack_rows(flat.shape[0], width, align)
    flat = jnp.pad(flat, (0, rows * width - flat.shape[0]))
    return flat.reshape(rows, width)


def _pack_small(arrays):
    return _pack(arrays, F32, width=128, align=8)


def _unpack(flat, shapes):
    out, off = [], 0
    for shp in shapes:
        n = int(np.prod(shp))
        out.append(flat[..., off:off + n].reshape(flat.shape[:-1] + tuple(shp)))
        off += n
    return out


def _doubled_heads(x2d, n_heads):
    S = x2d.shape[0]
    h = x2d.reshape(S, n_heads, HEAD_DIM).transpose(1, 0, 2)
    return jnp.concatenate([h, h], axis=-1)


def _rms_bwd_epilogue(dn, x, dres, g):
    r = _rinv(x)
    xh = x * r
    dxh = dn * g
    dx = dres + r * (dxh - xh * jnp.mean(dxh * xh, axis=-1, keepdims=True))
    return dx, dx, jnp.sum(dn * xh, axis=0, keepdims=True)


def _residual_then_norms(acc, res, *gains):
    h = res + acc
    hn = h * _rinv(h)
    return (h,) + tuple(hn * g for g in gains)


def _mlp_fwd(h, n, w_up4, w_down, next_gains, tag, between=None):
    u, a = _matmul(n, w_up4, "nn", f"up{tag}", out_dtypes=(F32, BF16), chipwise="b", tm=2048, tn=512,
                   epilogue=lambda acc: (acc, jnp.square(jnp.maximum(acc, 0.0))))
    if between is not None:
        next_gains = [next_gains[0] + between(a)] + list(next_gains[1:])
    assert w_down.shape[1] == 1024
    outs = _matmul(a, w_down, "nn", f"down{tag}", out_dtypes=(F32,) + (BF16,) * len(next_gains), extras=(h,),
                   row_extras=tuple(next_gains), epilogue=_residual_then_norms, tm=1024, tn=1024, tk=1024)
    outs = outs if next_gains else (outs,)
    return outs[0], outs[1:], (n, u, a)


def _mlp_bwd(dh_out, dh_out_b, h, g, w_up4, w_down, saved, tag):
    n, u, a = saved
    dw_down = _matmul(a, dh_out_b, "tn", f"dw_down{tag}", out_dtypes=(BF16,), tm=1024, tn=1024, chipwise="out_rows")
    du = _matmul(dh_out_b, w_down, "nt", f"du{tag}", out_dtypes=(BF16,), extras=(u,), tm=2048, tn=512,
                 epilogue=lambda acc, uu: (acc * (2.0 * jnp.maximum(uu, 0.0)),))
    dw_up = _matmul(n, du, "tn", f"dw_up{tag}", out_dtypes=(BF16,), chipwise="out", tm=1024, tn=512)
    dh, dh_b, dg = _matmul(du, w_up4, "nt", f"dn_mlp{tag}", out_dtypes=(F32, BF16, F32), extras=(h, dh_out), row_extras=(g,),
                           epilogue=_rms_bwd_epilogue, row_accums=1, tm=1024, tn=1024, tk=w_up4.shape[2], chipwise="b")
    return dh, dh_b, dg, dw_up, dw_down


def kernel(x, g_attn, g_mlp, w_in_a, b_f, gq_a, gk_a, w_out_a, g_kv, w_kv, gk_b, w_q_b, gq_b, sinks, rel_bias, w_out_b, w_up, w_down, loss_target, m_g_attn, m_g_mlp, m_w_in_a, m_b_f, m_gq_a, m_gk_a, m_w_out_a, m_g_kv, m_w_kv, m_gk_b, m_w_q_b, m_gq_b, m_sinks, m_rel_bias, m_w_out_b, m_w_up, m_w_down, v_g_attn, v_g_mlp, v_w_in_a, v_b_f, v_gq_a, v_gk_a, v_w_out_a, v_g_kv, v_w_kv, v_gk_b, v_w_q_b, v_gq_b, v_sinks, v_rel_bias, v_w_out_b, v_w_up, v_w_down):
    given = dict(locals())
    S, D = x.shape[1], x.shape[2]
    H = D // HEAD_DIM
    KVH = w_kv.shape[1] // (2 * HEAD_DIM)
    kvw = KVH * HEAD_DIM
    hw = H * HEAD_DIM
    W = WINDOW
    nb = S // W
    c_idx = lax.axis_index("c")
    xs, tgt = x[0], loss_target[0]

    n_in_shard = w_in_a.shape[2]
    rows_in = -(-n_in_shard // 32) * 32
    row_pad = lambda a: jnp.pad(a, [(0, 0)] * (a.ndim - 2) + [(0, rows_in - a.shape[-2]), (0, 0)])
    t_in = lambda a: jnp.swapaxes(a[0], 0, 1)
    shards = {"w_in_a": row_pad(t_in(w_in_a)), "w_out_a": w_out_a[0], "w_up0": w_up[0], "w_down0": w_down[0], "w_kv": w_kv,
              "w_q_b": w_q_b[0], "w_out_b": w_out_b[0], "w_up1": w_up[1], "w_down1": w_down[1]}
    parts = list(shards)
    groups = [("w_in_a", "w_out_a"), ("w_up0", "w_down0"), ("w_kv", "w_q_b", "w_out_b", "w_up1", "w_down1")]
    started = []
    for i, grp in enumerate(groups):
        behind = started[-1][4] if started else g_attn[0]
        started.append(_gather_start([shards[n].astype(BF16) for n in grp], behind, f"gather_start{i}"))
    gathered = {}

    def finish_gather(i, after):
        srcs, lands = _gather_wait(started[i], after, f"gather_wait{i}")
        gathered.update(zip(groups[i], _gather_pass_on(srcs, lands, f"gather_pass_on{i}")))

    def land_gather(i, after):
        srcs, lands = _gather_wait(started[i], after, f"gather_wait{i}")
        n = len(srcs)
        passing, token = _split_start(_plan_pass_on(n), list(srcs) + list(lands), 4 * n, after, f"pass_on_start{i}")
        return passing, token[0:1, 0:1]

    def finish_pass_on(i, passing, after):
        n = len(groups[i])
        gathered.update(zip(groups[i], _split_wait(_plan_pass_on(n), passing, after, f"pass_on_wait{i}")[n:]))

    vec = lambda a: a.reshape(1, -1)
    twice = lambda a: jnp.tile(a.reshape(1, -1), (1, 2))

    g_attn0 = vec(g_attn[0]) + sum(st[4][0, 0] for st in started)
    (n0,) = _rms_fwd(xs, [g_attn0], "rms_attn0")
    finish_gather(0, n0)
    win_t = gathered["w_in_a"][:, :n_in_shard].reshape(-1, D)
    win_t = jnp.pad(win_t, ((0, (-win_t.shape[0]) % 128), (0, 0)))
    wout_a = gathered["w_out_a"].reshape(-1, D)
    n_in = win_t.shape[0]
    tile_in = 640 if n_in % 640 == 0 else 128
    proj = _matmul(n0, win_t, "nt", "proj_in", tm=2048, tn=tile_in)
    zt = proj[:, 3 * hw:3 * hw + H].T
    c_row = _gate_fwd(zt, b_f.reshape(H, 1), "gate_fwd")
    c_row3 = c_row.reshape(H, 1, S)
    o_a, lse_a = _fox_fwd(proj, c_row3, twice(gq_a[0]), twice(gk_a[0]), H, "fox_fwd")
    passing1, tie = land_gather(1, o_a)
    h1, n1 = _matmul(o_a, wout_a, "nn", "out_a", out_dtypes=(F32, BF16), extras=(xs,), row_extras=(vec(g_mlp[0]) + tie,),
                     epilogue=_residual_then_norms, tm=1024, tn=1024)
    finish_pass_on(1, passing1, n1)
    wup = [gathered["w_up0"], None]
    wdown = [gathered["w_down0"].reshape(-1, D), None]
    passing = []

    def land_last_group(a):
        started_passing, tie = land_gather(2, a)
        passing.append(started_passing)
        return tie

    h2, (nkv, n2), mlp0 = _mlp_fwd(h1, n1, wup[0], wdown[0], [vec(g_kv), vec(g_attn[1])], "0", between=land_last_group)
    passing2 = passing[0]

    finish_pass_on(2, passing2, h2)
    wq_b, wout_b = gathered["w_q_b"].reshape(-1, D), gathered["w_out_b"].reshape(-1, D)
    wkv = gathered["w_kv"].reshape(D, -1)
    wup[1], wdown[1] = gathered["w_up1"], gathered["w_down1"].reshape(-1, D)
    kv = _matmul(nkv, wkv, "nn", "proj_kv", tm=2048)
    kk, vv = _doubled_heads(kv[:, :kvw], KVH), _doubled_heads(kv[:, kvw:], KVH)
    q2 = _matmul(n2, wq_b, "nn", "proj_q", tm=1024, tn=1024)
    onehot = jnp.asarray(_bucket_onehot(), dtype=BF16)
    bias = _bias_expand(rel_bias.T, onehot, "bias_expand").reshape(H, W, 2 * W)
    bias_ab = bias.reshape(H // 2, 2 * W, 2 * W)
    bias_t_ab = bias.reshape(H // 2, 2, W, 2 * W).transpose(0, 3, 1, 2).reshape(H // 2, 2 * W, 2 * W)
    sink_ab = jnp.repeat(sinks[0].reshape(H // 2, 2), W, axis=1)
    o_b, lse_b = _swa_fwd(q2, kk, vv, bias_ab, sink_ab.reshape(H // 2, 2 * W, 1), twice(gq_b[0]), twice(gk_b), "swa_fwd")
    h3, n3 = _matmul(o_b, wout_b, "nn", "out_b", out_dtypes=(F32, BF16), extras=(h2,), row_extras=(vec(g_mlp[1]),),
                     epilogue=_residual_then_norms, tm=1024, tn=1024)
    h4, _, mlp1 = _mlp_fwd(h3, n3, wup[1], wdown[1], [], "1")

    dh4, dh4_b, loss_part = _loss_head(h4, tgt, "loss_head")

    place = jnp.stack([2 * lax.axis_index("x") + lax.axis_index("y"), c_idx]).astype(jnp.int32)
    scattering = []

    def pair_and_scatter(names, mine, got):
        pair_sums = [_sum_core_pair(a, g, place, "sum_core_pair_" + n) for n, a, g in zip(names, mine, got)]
        started_scatter, token = _scatter_start(pair_sums, "scatter_start_" + names[0])
        scattering.append((names, started_scatter))
        return token[0:1, :]

    def start_reduce(named):
        names = list(named)
        mine = [named[n] for n in names]
        return pair_and_scatter(names, mine, _swap_halves_group(mine, "swap_grad_halves_" + names[0]))

    def start_swap(named):
        names = list(named)
        mine = [named[n] for n in names]
        lands = [lax.empty((a.shape[0], a.shape[1] // 2, a.shape[2]), a.dtype) for a in mine]
        swapping, token = _split_start(_plan_swap_halves(len(mine)), mine + lands, len(mine), mine[0], "swap_start_" + names[0])
        return names, swapping, token

    def finish_swap(swap, after):
        names, swapping, _ = swap
        n = len(names)
        arrays = _split_wait(_plan_swap_halves(n), swapping, after, "swap_wait_" + names[0])
        return pair_and_scatter(names, arrays[:n], arrays[n:])

    dh3, dh3_b, dg_mlp1, dw_up1, dw_down1 = _mlp_bwd(dh4, dh4_b, h3, vec(g_mlp[1]), wup[1], wdown[1], mlp1, "1")
    swap1 = start_swap({"w_down1": dw_down1, "w_up1": dw_up1})
    do_b = _matmul(dh3_b, wout_b, "nt", "do_b", tm=1024, tn=1024, after=swap1[2])
    dw_out_b = _matmul(o_b, dh3_b, "tn", "dw_out_b", out_dtypes=(BF16,), tm=1024, tn=1024, after=do_b)
    tie1 = finish_swap(swap1, dw_out_b)
    dq2, dk2, dv2, dbias_t_ab, dsink, dgq_b, dgk_b = _swa_bwd(
        q2, kk, vv, bias_t_ab, sink_ab.reshape(H // 2, 1, 2 * W), twice(gq_b[0]) + tie1, twice(gk_b),
        lse_b, do_b, "swa_bwd")
    dbias = dbias_t_ab.reshape(H // 2, 2 * W, 2, W).transpose(0, 2, 3, 1).reshape(H, W * 2 * W)
    d_rel_bias = _bias_reduce(dbias, onehot, "bias_reduce").T
    dw_q_b = _matmul(n2, dq2, "tn", "dw_q_b", out_dtypes=(BF16,), tm=1024, tn=1024)
    dn2 = _matmul(dq2, wq_b, "nt", "dn2", tm=1024, tn=1024)
    dkv = jnp.concatenate([dk2[h, :, :HEAD_DIM] for h in range(KVH)] + [dv2[h, :, :HEAD_DIM] for h in range(KVH)],
                          axis=1).astype(BF16)
    dw_kv = _matmul(nkv, dkv, "tn", "dw_kv", out_dtypes=(BF16,), tm=1024)
    dnkv = _matmul(dkv, wkv, "nt", "dnkv", tm=1024, tn=1024)
    tie2 = start_reduce({"w_out_b": dw_out_b.reshape(N_CHIPS, -1, D), "w_q_b": dw_q_b.reshape(N_CHIPS, -1, D),
                         "w_kv": dw_kv.reshape(N_CHIPS, -1, 2 * kvw)})
    dh2, dh2_b, (dg_kv, dg_attn1) = _rms_bwd(h2, dh3, [vec(g_kv) + tie2[:, :1], vec(g_attn[1])], [dnkv, dn2], "rms_attn1_bwd")

    dh1, dh1_b, dg_mlp0, dw_up0, dw_down0 = _mlp_bwd(dh2, dh2_b, h1, vec(g_mlp[0]), wup[0], wdown[0], mlp0, "0")
    swap3 = start_swap({"w_down0": dw_down0, "w_up0": dw_up0})
    do_a = _matmul(dh1_b, wout_a, "nt", "do_a", tm=1024, tn=1024, after=swap3[2])
    dw_out_a = _matmul(o_a, dh1_b, "tn", "dw_out_a", out_dtypes=(BF16,), tm=1024, tn=1024, after=do_a)
    tie3 = finish_swap(swap3, dw_out_a)
    dq_a, dk_a, dv_a, dc_row, dgq_a, dgk_a = _fox_bwd(
        proj, c_row3, twice(gq_a[0]) + tie3, twice(gk_a[0]), lse_a, do_a, H, "fox_bwd")
    dzt, db_f = _gate_bwd(dc_row.reshape(H, S), zt, b_f.reshape(H, 1), "gate_bwd")
    dproj = jnp.concatenate([dq_a, dk_a, dv_a, dzt.T.astype(BF16), jnp.zeros((S, n_in - 3 * hw - H), BF16)], axis=1)
    dw_in_t = _matmul(dproj, n0, "tn", "dw_in", out_dtypes=(BF16,), tm=tile_in, tn=1024)
    dw_in4 = row_pad(dw_in_t[:3 * hw + H].reshape(N_CHIPS, -1, D))
    tie4 = start_reduce({"w_out_a": dw_out_a.reshape(N_CHIPS, -1, D), "w_in_a": dw_in4})
    grad_x, _, dg_attn0 = _matmul(dproj, win_t, "nn", "dn0", out_dtypes=(F32, BF16, F32), extras=(xs, dh1),
                                  row_extras=(vec(g_attn[0]) + tie4[:, :1],), epilogue=_rms_bwd_epilogue, row_accums=1,
                                  tm=1024, tn=1024, tk=tile_in)

    small_grads = {
        "g_attn": jnp.concatenate([dg_attn0, dg_attn1], axis=0), "g_mlp": jnp.concatenate([dg_mlp0, dg_mlp1], axis=0),
        "b_f": db_f.reshape(1, H), "gq_a": dgq_a[:, :HEAD_DIM], "gk_a": dgk_a[:, :HEAD_DIM], "g_kv": dg_kv.reshape(-1),
        "gk_b": dgk_b[0, :HEAD_DIM], "gq_b": dgq_b[:, :HEAD_DIM], "sinks": dsink[:, 0, 0].reshape(1, H), "rel_bias": d_rel_bias,
    }
    small_shapes = [given[n].shape for n in SMALL] + [(1,)]
    spack = _pack_small([small_grads[n] for n in SMALL] + [loss_part])
    gathering_small, token = _split_start(_plan_gather_small, [spack, lax.empty((8,) + spack.shape, F32)], 7, grad_x,
                                          "gather_small_start")
    sharing = []
    for names, started_scatter in scattering:
        pair_sums, landed = _scatter_wait(started_scatter, grad_x, "scatter_wait_" + names[0])
        halves = [_sum_chips(p, l, place, "sum_chips_" + n) for n, p, l in zip(names, pair_sums, landed)]
        started_share, token = _split_start(_plan_share_halves(len(halves)), halves, len(halves), token, "share_start_" + names[0])
        sharing.append((names, started_share))
    own_small, others_small = _split_wait(_plan_gather_small, gathering_small, token, "gather_small_wait")
    small_sum = _sum_gathered_small(others_small, own_small, place, "sum_small")
    small_red = _unpack(small_sum.reshape(-1), small_shapes)
    reduced = {}
    for names, started_share in sharing:
        for n, r in zip(names, _split_wait(_plan_share_halves(len(names)), started_share, token, "share_wait_" + names[0])):
            reduced[n] = r.reshape(-1, r.shape[2])
    reduced["w_in_a"] = reduced["w_in_a"][:n_in_shard]
    loss = small_red[-1][0]

    grads = dict(zip(SMALL, small_red))
    no_loss = [jnp.zeros((1,), F32)]
    sw = _pack_small([given[n] for n in SMALL] + no_loss)
    sm = _pack_small([given["m_" + n] for n in SMALL] + no_loss)
    sv = _pack_small([given["v_" + n] for n in SMALL] + no_loss)
    sd, sm2, sv2 = _adamw(sw, small_sum, sm, sv, "adamw_small", tr=sw.shape[0])
    delta = dict(zip(SMALL, _unpack(sd.reshape(-1), small_shapes)))
    new_m = dict(zip(SMALL, _unpack(sm2.reshape(-1), small_shapes)))
    new_v = dict(zip(SMALL, _unpack(sv2.reshape(-1), small_shapes)))
    for n in ("w_out_a", "w_kv", "w_q_b", "w_out_b"):
        w = given[n]
        two_d = (-1, w.shape[-1])
        d, m2, v2 = _adamw(w.reshape(two_d), reduced[n], given["m_" + n].reshape(two_d), given["v_" + n].reshape(two_d),
                           "adamw_" + n)
        grads[n] = reduced[n].reshape(w.shape)
        delta[n], new_m[n], new_v[n] = d.reshape(w.shape), m2.reshape(w.shape), v2.reshape(w.shape)
    d, m2, v2 = _adamw(t_in(w_in_a), reduced["w_in_a"], t_in(m_w_in_a), t_in(v_w_in_a), "adamw_w_in_a")
    back = lambda a: jnp.swapaxes(a, 0, 1)[None]
    grads["w_in_a"], delta["w_in_a"], new_m["w_in_a"], new_v["w_in_a"] = back(reduced["w_in_a"]), back(d), back(m2), back(v2)
    for n in ("w_up", "w_down"):
        w = given[n]
        two_d = (-1, w.shape[-1])
        g, d, m2, v2 = _adamw_two_layers(w.reshape(two_d), reduced[n + "0"], reduced[n + "1"], given["m_" + n].reshape(two_d),
                                         given["v_" + n].reshape(two_d), "adamw_" + n)
        grads[n], delta[n], new_m[n], new_v[n] = g.reshape(w.shape), d.reshape(w.shape), m2.reshape(w.shape), v2.reshape(w.shape)

    order = ["g_attn", "g_mlp", "w_in_a", "b_f", "gq_a", "gk_a", "w_out_a", "g_kv", "w_kv", "gk_b", "w_q_b", "gq_b",
             "sinks", "rel_bias", "w_out_b", "w_up", "w_down"]
    return (loss, grad_x[None], *[grads[n] for n in order], *[delta[n] for n in order],
            *[new_m[n] for n in order], *[new_v[n] for n in order])
```

```python
import numpy as np
import jax
import jax.numpy as jnp
from jax import lax
from jax.experimental import pallas as pl
from jax.experimental.pallas import tpu as pltpu

F32 = jnp.float32
BF16 = jnp.bfloat16
MESH = pl.DeviceIdType.MESH

HEAD_DIM = 64
LANES = 128
WINDOW = 128
N_BUCKETS = 32
REL_MAX_DIST = 128
NORM_EPS = 1e-6
ADAM_LR = 0.001
ADAM_B1 = 0.9
ADAM_B2 = 0.999
ADAM_EPS = 1e-08
ADAM_WD = 0.01
ADAM_STEP = 10
NEG = -1e30
N_CHIPS = 4
PACK_W = 1024
PACK_ROW_ALIGN = 256
VMEM_LIMIT = 56 * 1024 * 1024
HBM_SPEC = pl.BlockSpec(memory_space=pltpu.HBM)
VMEM_SPEC = pl.BlockSpec(memory_space=pltpu.VMEM)

BIG = (("w_in_a", 2), ("w_out_a", 1), ("w_kv", 0), ("w_q_b", 1), ("w_out_b", 1), ("w_up", 2), ("w_down", 1))
SMALL = ("g_attn", "g_mlp", "b_f", "gq_a", "gk_a", "g_kv", "gk_b", "gq_b", "sinks", "rel_bias")


def _pcall(body, **kw):
    return pl.pallas_call(body, **kw)


def _params(sem=None):
    return pltpu.CompilerParams(dimension_semantics=sem, vmem_limit_bytes=VMEM_LIMIT)


def _rinv(x):
    return lax.rsqrt(jnp.mean(x * x, axis=-1, keepdims=True) + NORM_EPS)


def _dot(a, b, dims, precision=None):
    return lax.dot_general(a, b, (dims, ((), ())), precision=precision, preferred_element_type=F32)


NN = ((1,), (0,))
NT = ((1,), (1,))
TN = ((0,), (0,))


def _accumulate(ref, val, first):
    @pl.when(first)
    def _():
        ref[...] = val

    @pl.when(jnp.logical_not(first))
    def _():
        ref[...] += val


def _matmul(a, b, mode, name, out_dtypes=(F32,), extras=(), row_extras=(), epilogue=None, tm=512, tn=512, tk=None, chipwise=None,
            after=None, row_accums=0):
    if chipwise == "b":
        nc = b.shape[2]
        M, K = a.shape
        (K2, N) = (b.shape[1], N_CHIPS * nc) if mode == "nn" else (N_CHIPS * nc, b.shape[1])
    elif mode == "nn":
        (M, K), (K2, N) = a.shape, b.shape
    elif mode == "nt":
        (M, K), (N, K2) = a.shape, b.shape
    else:
        (K, M), (K2, N) = a.shape, b.shape
    assert K == K2, (a.shape, b.shape, mode)
    tm, tn = min(tm, M), min(tn, N)
    tk = K if tk is None else tk
    assert M % tm == 0 and N % tn == 0 and K % tk == 0, (M, N, K, tm, tn, tk)
    nk = K // tk
    dims = {"nn": NN, "nt": NT, "tn": TN}[mode]
    a_spec = pl.BlockSpec((tk, tm), lambda i, j, k: (k, i)) if mode == "tn" else pl.BlockSpec((tm, tk), lambda i, j, k: (i, k))
    b_spec = pl.BlockSpec((tn, tk), lambda i, j, k: (j, k)) if mode == "nt" else pl.BlockSpec((tk, tn), lambda i, j, k: (k, j))
    o_spec = pl.BlockSpec((tm, tn), lambda i, j, k: (i, j))
    out_shape = (M, N)
    if chipwise == "b" and mode == "nn":
        per = nc // tn
        assert tk == K and nc % tn == 0
        b_spec = pl.BlockSpec((None, tk, tn), lambda i, j, k: (j // per, 0, j % per))
    elif chipwise == "b":
        assert mode == "nt" and tk == nc
        b_spec = pl.BlockSpec((None, tn, tk), lambda i, j, k: (k, j, 0))
    elif chipwise == "out_rows":
        per = (M // N_CHIPS) // tm
        assert (M // N_CHIPS) % tm == 0 and not extras
        o_spec = pl.BlockSpec((None, tm, tn), lambda i, j, k: (i // per, i % per, j))
        out_shape = (N_CHIPS, M // N_CHIPS, N)
    elif chipwise == "out":
        per = (N // N_CHIPS) // tn
        assert (N // N_CHIPS) % tn == 0
        o_spec = pl.BlockSpec((None, tm, tn), lambda i, j, k: (j // per, i, j % per))
        out_shape = (N_CHIPS, M, N // N_CHIPS)
        assert not extras
    n_ex, n_rex, n_out = len(extras), len(row_extras), len(out_dtypes)
    assert not row_accums or tn == N
    tail = () if after is None else (after,)
    n_in = 2 + n_ex + n_rex + len(tail)

    def body(*refs):
        a_ref, b_ref = refs[0], refs[1]
        ex_refs = refs[2:2 + n_ex + n_rex]
        out_refs = refs[n_in:n_in + n_out]
        part = _dot(a_ref[...].astype(BF16), b_ref[...].astype(BF16), dims)

        def finish(acc):
            outs = (acc,) if epilogue is None else epilogue(acc, *[r[...] for r in ex_refs])
            for idx, (r, o) in enumerate(zip(out_refs, outs)):
                if idx >= n_out - row_accums:
                    _accumulate(r, o, pl.program_id(0) == 0)
                else:
                    r[...] = o.astype(r.dtype)

        if nk == 1:
            finish(part)
            return
        acc_ref = refs[n_in + n_out]
        k = pl.program_id(2)

        @pl.when(k == 0)
        def _():
            acc_ref[...] = part

        @pl.when(jnp.logical_and(k > 0, k < nk - 1))
        def _():
            acc_ref[...] += part

        @pl.when(k == nk - 1)
        def _():
            finish(acc_ref[...] + part)

    row_spec = pl.BlockSpec((1, tn), lambda i, j, k: (0, j))
    outs = _pcall(
        body, name=name, grid=(M // tm, N // tn, nk),
        in_specs=[a_spec, b_spec] + [o_spec] * n_ex + [row_spec] * n_rex + [pl.BlockSpec(memory_space=pl.ANY)] * len(tail),
        out_specs=[o_spec] * (n_out - row_accums) + [row_spec] * row_accums,
        out_shape=[jax.ShapeDtypeStruct(out_shape, dt) for dt in out_dtypes[:n_out - row_accums]]
        + [jax.ShapeDtypeStruct((1, N), F32)] * row_accums,
        scratch_shapes=[pltpu.VMEM((tm, tn), F32)] if nk > 1 else [],
        compiler_params=_params(("arbitrary",) * 3 if row_accums else ("parallel", "parallel", "arbitrary")),
    )(a, b, *extras, *row_extras, *tail)
    return outs[0] if n_out == 1 else outs


def _rms_fwd(x, gains, name, ts=256):
    S, D = x.shape
    ts = min(ts, S)
    n = len(gains)

    def body(*refs):
        x_ref, g_refs, o_refs = refs[0], refs[1:1 + n], refs[1 + n:]
        xv = x_ref[...]
        xh = xv * _rinv(xv)
        for g_ref, o_ref in zip(g_refs, o_refs):
            o_ref[...] = (xh * g_ref[...]).astype(BF16)

    row = pl.BlockSpec((ts, D), lambda i: (i, 0))
    vec = pl.BlockSpec((1, D), lambda i: (0, 0))
    return _pcall(body, name=name, grid=(S // ts,), in_specs=[row] + [vec] * n, out_specs=[row] * n,
                  out_shape=[jax.ShapeDtypeStruct((S, D), BF16)] * n, compiler_params=_params(("parallel",)))(x, *gains)


def _rms_bwd(x, dres, gains, dns, name, ts=256):
    S, D = x.shape
    ts = min(ts, S)
    n = len(gains)

    def body(*refs):
        x_ref, dres_ref = refs[0], refs[1]
        g_refs, dn_refs = refs[2:2 + n], refs[2 + n:2 + 2 * n]
        dx_ref, dxb_ref, dg_refs = refs[2 + 2 * n], refs[3 + 2 * n], refs[4 + 2 * n:]
        xv = x_ref[...]
        r = _rinv(xv)
        xh = xv * r
        dx = dres_ref[...]
        first = pl.program_id(0) == 0
        for g_ref, dn_ref, dg_ref in zip(g_refs, dn_refs, dg_refs):
            dn = dn_ref[...].astype(F32)
            _accumulate(dg_ref, jnp.sum(dn * xh, axis=0, keepdims=True), first)
            dxh = dn * g_ref[...]
            dx = dx + r * (dxh - xh * jnp.mean(dxh * xh, axis=-1, keepdims=True))
        dx_ref[...] = dx
        dxb_ref[...] = dx.astype(BF16)

    row = pl.BlockSpec((ts, D), lambda i: (i, 0))
    vec = pl.BlockSpec((1, D), lambda i: (0, 0))
    outs = _pcall(body, name=name, grid=(S // ts,), in_specs=[row, row] + [vec] * n + [row] * n,
                  out_specs=[row, row] + [vec] * n,
                  out_shape=[jax.ShapeDtypeStruct((S, D), F32), jax.ShapeDtypeStruct((S, D), BF16)]
                  + [jax.ShapeDtypeStruct((1, D), F32)] * n,
                  compiler_params=_params(("arbitrary",)))(x, dres, *gains, *dns)
    return outs[0], outs[1], outs[2:]


def _loss_head(h, tgt, name, ts=256):
    S, D = h.shape
    ts = min(ts, S)

    def body(h_ref, t_ref, dh_ref, dhb_ref, loss_ref):
        err = h_ref[...] - t_ref[...]
        dh = err * (1.0 / D)
        dh_ref[...] = dh
        dhb_ref[...] = dh.astype(BF16)
        part = 0.5 * jnp.sum(jnp.mean(err * err, axis=-1, keepdims=True), axis=0, keepdims=True)
        _accumulate(loss_ref, part, pl.program_id(0) == 0)

    row = pl.BlockSpec((ts, D), lambda i: (i, 0))
    return _pcall(body, name=name, grid=(S // ts,), in_specs=[row, row],
                  out_specs=[row, row, pl.BlockSpec((1, 1), lambda i: (0, 0))],
                  out_shape=[jax.ShapeDtypeStruct((S, D), F32), jax.ShapeDtypeStruct((S, D), BF16),
                             jax.ShapeDtypeStruct((1, 1), F32)],
                  compiler_params=_params(("arbitrary",)))(h, tgt)


def _gate_fwd(zt, bf, name):
    H, S = zt.shape
    nb = S // 128

    def body(z_ref, b_ref, c_ref):
        z = z_ref[...] + b_ref[...]
        lf = jnp.minimum(z, 0.0) - jnp.log(1.0 + jnp.exp(-jnp.abs(z)))
        upper = (lax.broadcasted_iota(jnp.int32, (128, 128), 0) <= lax.broadcasted_iota(jnp.int32, (128, 128), 1)).astype(F32)
        carry = jnp.zeros((H, 1), F32)
        for blk in range(nb):
            cs = _dot(lf[:, blk * 128:(blk + 1) * 128], upper, NN, precision=lax.Precision.HIGHEST) + carry
            c_ref[:, blk * 128:(blk + 1) * 128] = cs
            carry = cs[:, 127:128]

    return _pcall(body, name=name, in_specs=[VMEM_SPEC, VMEM_SPEC], out_specs=VMEM_SPEC,
                  out_shape=jax.ShapeDtypeStruct((H, S), F32))(zt, bf)


def _gate_bwd(dct, zt, bf, name):
    H, S = zt.shape
    nb = S // 128

    def body(dc_ref, z_ref, b_ref, dz_ref, db_ref):
        z = z_ref[...] + b_ref[...]
        e = jnp.exp(-jnp.abs(z))
        sig_neg = jnp.where(z >= 0, e, 1.0) / (1.0 + e)
        lower = (lax.broadcasted_iota(jnp.int32, (128, 128), 0) >= lax.broadcasted_iota(jnp.int32, (128, 128), 1)).astype(F32)
        dc = dc_ref[...]
        carry = jnp.zeros((H, 1), F32)
        db = jnp.zeros((H, 1), F32)
        for blk in reversed(range(nb)):
            sl = slice(blk * 128, (blk + 1) * 128)
            dlf = _dot(dc[:, sl], lower, NN, precision=lax.Precision.HIGHEST) + carry
            carry = dlf[:, 0:1]
            dz = dlf * sig_neg[:, sl]
            dz_ref[:, sl] = dz
            db = db + jnp.sum(dz, axis=1, keepdims=True)
        db_ref[...] = db

    return _pcall(body, name=name, in_specs=[VMEM_SPEC] * 3, out_specs=[VMEM_SPEC] * 2,
                  out_shape=[jax.ShapeDtypeStruct((H, S), F32), jax.ShapeDtypeStruct((H, 1), F32)])(dct, zt, bf)


def _lane_is_a():
    return lax.broadcasted_iota(jnp.int32, (1, LANES), 1) < HEAD_DIM


def _per_head_mean(x, is_a):
    sa = jnp.sum(jnp.where(is_a, x, 0.0), axis=-1, keepdims=True)
    sb = jnp.sum(jnp.where(is_a, 0.0, x), axis=-1, keepdims=True)
    return jnp.where(is_a, sa, sb) / HEAD_DIM


def _pair_norm(raw, gain, is_a):
    return raw * lax.rsqrt(_per_head_mean(raw * raw, is_a) + NORM_EPS) * gain


def _pair_norm_bwd(raw, gain, dnormed, is_a):
    r = lax.rsqrt(_per_head_mean(raw * raw, is_a) + NORM_EPS)
    xh = raw * r
    dgain = jnp.sum(dnormed * xh, axis=0, keepdims=True)
    dxh = dnormed * gain
    return r * (dxh - xh * _per_head_mean(dxh * xh, is_a)), dgain


def _fold_heads(x):
    i = lax.broadcasted_iota(jnp.int32, (LANES, LANES), 0)
    j = lax.broadcasted_iota(jnp.int32, (LANES, LANES), 1)
    fold = ((i == j) | (i == j + HEAD_DIM) | (i + HEAD_DIM == j)).astype(F32)
    return _dot(x, fold, NN, precision=lax.Precision.HIGHEST)


def _fold_row(ref):
    ref[...] = _fold_heads(jnp.broadcast_to(ref[...], (8, LANES)))[0:1, :]


def _as_col(row):
    return jnp.broadcast_to(row, (LANES, row.shape[1])).T[:, 0:1]


def _as_row(col):
    return jnp.broadcast_to(col, (col.shape[0], LANES)).T[0:1, :]


def _tri_mask(t, keys_on_rows):
    r = lax.broadcasted_iota(jnp.int32, (t, t), 0)
    c = lax.broadcasted_iota(jnp.int32, (t, t), 1)
    return (r <= c) if keys_on_rows else (r >= c)


def _fox_fwd(proj, c_row, gq2, gk2, n_heads, name, t=256):
    S = proj.shape[0]
    H = n_heads
    P = H // 2
    t = min(t, S)
    nq = S // t

    def body(q_ref, k_ref, v_ref, cr_ref, gq_ref, gk_ref, o_ref, lse_ref, qs_s, kb_s, vb_s):
        is_a = _lane_is_a()
        qn = _pair_norm(q_ref[...], gq_ref[...], is_a) * 0.125
        qs_s[0] = jnp.where(is_a, qn, 0.0).astype(BF16)
        qs_s[1] = jnp.where(is_a, 0.0, qn).astype(BF16)
        kb_s[...] = _pair_norm(k_ref[...], gk_ref[...], is_a).astype(BF16)
        vb_s[...] = v_ref[...].astype(BF16)
        causal = _tri_mask(t, False)
        for i in range(nq):
            t0 = i * t
            rows = slice(t0, t0 + t)
            o_pair = None
            for a in range(2):
                qi = qs_s[a, rows, :]
                ci = _as_col(cr_ref[a, :, rows])
                s_d = jnp.where(causal, _dot(qi, kb_s[rows, :], NT) + ci - cr_ref[a, :, rows], NEG)
                m = jnp.max(s_d, axis=-1, keepdims=True)
                if i > 0:
                    s_l = _dot(qi, kb_s[0:t0, :], NT) + ci - cr_ref[a, :, 0:t0]
                    m = jnp.maximum(m, jnp.max(s_l, axis=-1, keepdims=True))
                p_d = jnp.exp(s_d - m)
                l = jnp.sum(p_d, axis=-1, keepdims=True)
                acc = _dot(p_d.astype(BF16), vb_s[rows, :], NN)
                if i > 0:
                    p_l = jnp.exp(s_l - m)
                    l = l + jnp.sum(p_l, axis=-1, keepdims=True)
                    acc = acc + _dot(p_l.astype(BF16), vb_s[0:t0, :], NN)
                o_a = acc / l
                lse_ref[a, :, rows] = _as_row(m + jnp.log(l))
                o_pair = o_a if a == 0 else jnp.where(is_a, o_pair, o_a)
            o_ref[rows, :] = o_pair.astype(BF16)

    def cols(off):
        return pl.BlockSpec((S, LANES), lambda p: (0, off + p))

    rowv = pl.BlockSpec((2, 1, S), lambda p: (p, 0, 0))
    gain = pl.BlockSpec((1, LANES), lambda p: (0, 0))
    return _pcall(body, name=name, grid=(P,), in_specs=[cols(0), cols(P), cols(2 * P), rowv, gain, gain],
                  out_specs=[cols(0), rowv],
                  out_shape=[jax.ShapeDtypeStruct((S, H * HEAD_DIM), BF16), jax.ShapeDtypeStruct((H, 1, S), F32)],
                  scratch_shapes=[pltpu.VMEM((2, S, LANES), BF16), pltpu.VMEM((S, LANES), BF16), pltpu.VMEM((S, LANES), BF16)],
                  compiler_params=_params(("parallel",)))(proj, proj, proj, c_row, gq2, gk2)


def _fox_bwd(proj, c_row, gq2, gk2, lse_row, do, n_heads, name, t=256):
    S = proj.shape[0]
    H = n_heads
    P = H // 2
    t = min(t, S)
    nq = S // t
    assert t % LANES == 0

    def body(q_ref, k_ref, v_ref, cr_ref, gq_ref, gk_ref, lr_ref, do_ref,
             dq_ref, dk_ref, dv_ref, dc_ref, dgq_ref, dgk_ref,
             qs_s, kb_s, kt_s, vb_s, dob_s, dq_s, dk_s, dv_s, dcs_s, cc_s):
        is_a = _lane_is_a()
        for a in range(2):
            for i in range(nq):
                cc_s[a, i * t:(i + 1) * t, :] = _as_col(cr_ref[a, :, i * t:(i + 1) * t])
        qn = _pair_norm(q_ref[...], gq_ref[...], is_a) * 0.125
        qs_s[0] = jnp.where(is_a, qn, 0.0).astype(BF16)
        qs_s[1] = jnp.where(is_a, 0.0, qn).astype(BF16)
        kn = _pair_norm(k_ref[...], gk_ref[...], is_a)
        kb_s[...] = kn.astype(BF16)
        kt_s[0] = jnp.where(is_a, kn, 0.0).T.astype(BF16)
        kt_s[1] = jnp.where(is_a, 0.0, kn).T.astype(BF16)
        vb_s[...] = v_ref[...].astype(BF16)
        dov = do_ref[...]
        dob_s[0] = jnp.where(is_a, dov, 0.0).astype(BF16)
        dob_s[1] = jnp.where(is_a, 0.0, dov).astype(BF16)
        dk_s[...] = jnp.zeros((S, LANES), F32)
        dv_s[...] = jnp.zeros((S, LANES), F32)
        dcs_s[...] = jnp.zeros((2, S, LANES), F32)
        causal = _tri_mask(t, True)
        for i in range(nq):
            t0 = i * t
            rows = slice(t0, t0 + t)
            dq_t = jnp.zeros((LANES, t), F32)
            for a in range(2):
                qi = qs_s[a, rows, :]
                doi = dob_s[a, rows, :]
                cri = cr_ref[a, :, rows]
                lri = lr_ref[a, :, rows]

                def probs(keys, masked, a=a, qi=qi, doi=doi, cri=cri, lri=lri):
                    p_t = jnp.exp(_dot(kb_s[keys, :], qi, NT) + cri - cc_s[a, keys, :] - lri)
                    if masked:
                        p_t = jnp.where(causal, p_t, 0.0)
                    return p_t, _dot(vb_s[keys, :], doi, NT)

                parts = [(rows,) + probs(rows, True)]
                if i > 0:
                    parts.append((slice(0, t0),) + probs(slice(0, t0), False))
                delta = sum(jnp.sum(p_t * dp_t, axis=0, keepdims=True) for _, p_t, dp_t in parts)
                for keys, p_t, dp_t in parts:
                    ds_t = p_t * (dp_t - delta)
                    dsb = ds_t.astype(BF16)
                    dv_s[keys, :] += _dot(p_t.astype(BF16), doi, NN)
                    dk_s[keys, :] += _dot(dsb, qi, NN)
                    dq_t = dq_t + _dot(kt_s[a, :, keys], dsb, NN)
                    dcs_s[a, keys, :] += sum(ds_t[:, b * LANES:(b + 1) * LANES] for b in range(t // LANES))
            dq_s[rows, :] = dq_t.T
        first = pl.program_id(0) == 0
        last = pl.program_id(0) == P - 1
        dq_raw, dgq = _pair_norm_bwd(q_ref[...], gq_ref[...], dq_s[...] * 0.125, is_a)
        dq_ref[...] = dq_raw.astype(BF16)
        _accumulate(dgq_ref, dgq, first)
        dk_raw, dgk = _pair_norm_bwd(k_ref[...], gk_ref[...], dk_s[...], is_a)
        dk_ref[...] = dk_raw.astype(BF16)
        _accumulate(dgk_ref, dgk, first)
        dv_ref[...] = dv_s[...].astype(BF16)
        for a in range(2):
            for i in range(nq):
                rows = slice(i * t, (i + 1) * t)
                dc_ref[a, :, rows] = _as_row(-jnp.sum(dcs_s[a, rows, :], axis=1, keepdims=True))

        @pl.when(last)
        def _():
            _fold_row(dgq_ref)
            _fold_row(dgk_ref)

    def cols(off):
        return pl.BlockSpec((S, LANES), lambda p: (0, off + p))

    rowv = pl.BlockSpec((2, 1, S), lambda p: (p, 0, 0))
    gain = pl.BlockSpec((1, LANES), lambda p: (0, 0))
    wide = jax.ShapeDtypeStruct((S, H * HEAD_DIM), BF16)
    gs = jax.ShapeDtypeStruct((1, LANES), F32)
    return _pcall(body, name=name, grid=(P,),
                  in_specs=[cols(0), cols(P), cols(2 * P), rowv, gain, gain, rowv, cols(0)],
                  out_specs=[cols(0), cols(0), cols(0), rowv, gain, gain],
                  out_shape=[wide, wide, wide, jax.ShapeDtypeStruct((H, 1, S), F32), gs, gs],
                  scratch_shapes=[pltpu.VMEM((2, S, LANES), BF16), pltpu.VMEM((S, LANES), BF16), pltpu.VMEM((2, LANES, S), BF16),
                                  pltpu.VMEM((S, LANES), BF16), pltpu.VMEM((2, S, LANES), BF16)]
                  + [pltpu.VMEM((S, LANES), F32)] * 3 + [pltpu.VMEM((2, S, LANES), F32), pltpu.VMEM((2, S, 1), F32)],
                  compiler_params=_params(("arbitrary",)))(proj, proj, proj, c_row, gq2, gk2, lse_row, do)


def _bucket_onehot():
    W = WINDOW
    dist = np.arange(W)[:, None] + W - np.arange(2 * W)[None, :]
    n = np.maximum(dist, 0)
    max_exact = N_BUCKETS // 2
    large = max_exact + (np.log(np.maximum(n, 1) / max_exact) / np.log(REL_MAX_DIST / max_exact)
                         * (N_BUCKETS - max_exact)).astype(np.int32)
    large = np.minimum(large, N_BUCKETS - 1)
    bucket = np.where(n < max_exact, n, large).astype(np.int32)
    valid = (dist >= 0) & (dist < W)
    onehot = (bucket[None] == np.arange(N_BUCKETS)[:, None, None]) & valid[None]
    return onehot.reshape(N_BUCKETS, W * 2 * W).astype(np.float32)


def _bias_expand(rel_bias_t, onehot, name, tn=4096):
    HQ, NB = rel_bias_t.shape
    L = onehot.shape[1]

    def body(r_ref, oh_ref, out_ref):
        out_ref[...] = _dot(r_ref[...], oh_ref[...].astype(F32), NN, precision=lax.Precision.HIGHEST)

    return _pcall(body, name=name, grid=(L // tn,),
                  in_specs=[pl.BlockSpec((HQ, NB), lambda i: (0, 0)), pl.BlockSpec((NB, tn), lambda i: (0, i))],
                  out_specs=pl.BlockSpec((HQ, tn), lambda i: (0, i)),
                  out_shape=jax.ShapeDtypeStruct((HQ, L), F32), compiler_params=_params(("parallel",)))(rel_bias_t, onehot)


def _bias_reduce(dbias, onehot, name, tk=4096):
    HQ, L = dbias.shape
    NB = onehot.shape[0]

    def body(d_ref, oh_ref, out_ref):
        part = _dot(d_ref[...], oh_ref[...].astype(F32), NT, precision=lax.Precision.HIGHEST)
        _accumulate(out_ref, part, pl.program_id(0) == 0)

    return _pcall(body, name=name, grid=(L // tk,),
                  in_specs=[pl.BlockSpec((HQ, tk), lambda i: (0, i)), pl.BlockSpec((NB, tk), lambda i: (0, i))],
                  out_specs=pl.BlockSpec((HQ, NB), lambda i: (0, 0)),
                  out_shape=jax.ShapeDtypeStruct((HQ, NB), F32), compiler_params=_params(("arbitrary",)))(dbias, onehot)


def _stacked_query_index(n_rows_or_cols_axis, shape):
    idx = lax.broadcasted_iota(jnp.int32, shape, n_rows_or_cols_axis)
    return jnp.where(idx >= WINDOW, idx - WINDOW, idx)


def _swa_fwd(qproj, kk, vv, bias_ab, sink_col, gq2, gk2, name):
    S, HQD = qproj.shape
    KVH = kk.shape[0]
    PP = HQD // LANES
    NP = PP // KVH
    W = WINDOW
    nb = S // W

    def body(q_ref, k_ref, v_ref, bias_ref, sink_ref, gq_ref, gk_ref, o_ref, lse_ref, qs_s, kb_s, vb_s):
        is_a = _lane_is_a()
        qn = _pair_norm(q_ref[...], gq_ref[...], is_a) * 0.125
        qs_s[0] = jnp.where(is_a, qn, 0.0).astype(BF16)
        qs_s[1] = jnp.where(is_a, 0.0, qn).astype(BF16)
        kb_s[...] = _pair_norm(k_ref[...], gk_ref[...], is_a).astype(BF16)
        vb_s[...] = v_ref[...].astype(BF16)
        sink = sink_ref[...]
        qi1 = _stacked_query_index(0, (2 * W, W))
        first_valid = lax.broadcasted_iota(jnp.int32, (2 * W, W), 1) <= qi1
        qi2 = _stacked_query_index(0, (2 * W, 2 * W))
        key2 = lax.broadcasted_iota(jnp.int32, (2 * W, 2 * W), 1)
        band_valid = (key2 > qi2) & (key2 <= qi2 + W)
        for n in range(nb):
            rows = slice(n * W, (n + 1) * W)
            keys = slice(0, W) if n == 0 else slice((n - 1) * W, (n + 1) * W)
            lhs = jnp.concatenate([qs_s[0, rows, :], qs_s[1, rows, :]], axis=0)
            s = _dot(lhs, kb_s[keys, :], NT) + (bias_ref[:, W:2 * W] if n == 0 else bias_ref[...])
            s = jnp.where(first_valid if n == 0 else band_valid, s, NEG)
            m = jnp.maximum(jnp.max(s, axis=-1, keepdims=True), sink)
            e = jnp.exp(s - m)
            l = jnp.sum(e, axis=-1, keepdims=True) + jnp.exp(sink - m)
            o_ab = _dot(e.astype(BF16), vb_s[keys, :], NN) / l
            o_ref[rows, :] = jnp.where(is_a, o_ab[0:W, :], o_ab[W:2 * W, :]).astype(BF16)
            lse_ref[n] = _as_row(m + jnp.log(l))

    qcols = pl.BlockSpec((S, LANES), lambda a, g: (0, a * NP + g))
    kvs = pl.BlockSpec((None, S, LANES), lambda a, g: (a, 0, 0))
    gain = pl.BlockSpec((1, LANES), lambda a, g: (0, 0))
    return _pcall(body, name=name, grid=(KVH, NP),
                  in_specs=[qcols, kvs, kvs, pl.BlockSpec((None, 2 * W, 2 * W), lambda a, g: (a * NP + g, 0, 0)),
                            pl.BlockSpec((None, 2 * W, 1), lambda a, g: (a * NP + g, 0, 0)), gain, gain],
                  out_specs=[qcols, pl.BlockSpec((None, nb, 1, 2 * W), lambda a, g: (a * NP + g, 0, 0, 0))],
                  out_shape=[jax.ShapeDtypeStruct((S, HQD), BF16), jax.ShapeDtypeStruct((PP, nb, 1, 2 * W), F32)],
                  scratch_shapes=[pltpu.VMEM((2, S, LANES), BF16), pltpu.VMEM((S, LANES), BF16), pltpu.VMEM((S, LANES), BF16)],
                  compiler_params=_params(("parallel", "parallel")))(qproj, kk, vv, bias_ab, sink_col, gq2, gk2)


def _swa_bwd(qproj, kk, vv, bias_t_ab, sink_row, gq2, gk2, lse_row, do, name):
    S, HQD = qproj.shape
    KVH = kk.shape[0]
    PP = HQD // LANES
    NP = PP // KVH
    W = WINDOW
    nb = S // W

    def body(q_ref, k_ref, v_ref, bias_ref, sink_ref, gq_ref, gk_ref, lr_ref, do_ref,
             dq_ref, dk_ref, dv_ref, db_ref, dsink_ref, dgq_ref, dgk_ref,
             qs_s, kb_s, kt_s, vb_s, dob_s, dq_s, dk_s, dv_s):
        a, g = pl.program_id(0), pl.program_id(1)
        is_a = _lane_is_a()
        qn = _pair_norm(q_ref[...], gq_ref[...], is_a) * 0.125
        qs_s[0] = jnp.where(is_a, qn, 0.0).astype(BF16)
        qs_s[1] = jnp.where(is_a, 0.0, qn).astype(BF16)
        kn = _pair_norm(k_ref[...], gk_ref[...], is_a)
        kb_s[...] = kn.astype(BF16)
        kt_s[...] = kn.T.astype(BF16)
        vb_s[...] = v_ref[...].astype(BF16)
        dov = do_ref[...]
        dob_s[0] = jnp.where(is_a, dov, 0.0).astype(BF16)
        dob_s[1] = jnp.where(is_a, 0.0, dov).astype(BF16)
        sink = sink_ref[...]

        @pl.when(g == 0)
        def _():
            dk_s[...] = jnp.zeros((S, LANES), F32)
            dv_s[...] = jnp.zeros((S, LANES), F32)

        qi1 = _stacked_query_index(1, (W, 2 * W))
        first_valid = lax.broadcasted_iota(jnp.int32, (W, 2 * W), 0) <= qi1
        qi2 = _stacked_query_index(1, (2 * W, 2 * W))
        key2 = lax.broadcasted_iota(jnp.int32, (2 * W, 2 * W), 0)
        band_valid = (key2 > qi2) & (key2 <= qi2 + W)
        head_rows = lax.broadcasted_iota(jnp.int32, (LANES, W), 0) < HEAD_DIM
        db = jnp.zeros((2 * W, 2 * W), F32)
        dsk = jnp.zeros((1, 2 * W), F32)
        pend_k = pend_v = None
        for n in range(nb):
            rows = slice(n * W, (n + 1) * W)
            keys = slice(0, W) if n == 0 else slice((n - 1) * W, (n + 1) * W)
            lhs_q = jnp.concatenate([qs_s[0, rows, :], qs_s[1, rows, :]], axis=0)
            lhs_do = jnp.concatenate([dob_s[0, rows, :], dob_s[1, rows, :]], axis=0)
            lse = lr_ref[n]
            s_t = _dot(kb_s[keys, :], lhs_q, NT) + (bias_ref[W:2 * W, :] if n == 0 else bias_ref[...])
            p_t = jnp.where(first_valid if n == 0 else band_valid, jnp.exp(s_t - lse), 0.0)
            dp_t = _dot(vb_s[keys, :], lhs_do, NT)
            delta = jnp.sum(p_t * dp_t, axis=0, keepdims=True)
            ds_t = p_t * (dp_t - delta)
            dsb = ds_t.astype(BF16)
            dsk = dsk - jnp.exp(sink - lse) * delta
            dv_band = _dot(p_t.astype(BF16), lhs_do, NN)
            dk_band = _dot(dsb, lhs_q, NN)
            dq_t = _dot(kt_s[:, keys], dsb, NN)
            dq_s[rows, :] = jnp.where(head_rows, dq_t[:, 0:W], dq_t[:, W:2 * W]).T
            if n == 0:
                db = jnp.concatenate([jnp.zeros((W, 2 * W), F32), ds_t], axis=0)
                pend_k, pend_v = dk_band, dv_band
            else:
                db = db + ds_t
                prev = slice((n - 1) * W, n * W)
                dk_s[prev, :] += pend_k + dk_band[0:W, :]
                dv_s[prev, :] += pend_v + dv_band[0:W, :]
                pend_k, pend_v = dk_band[W:2 * W, :], dv_band[W:2 * W, :]
        tail = slice((nb - 1) * W, nb * W)
        dk_s[tail, :] += pend_k
        dv_s[tail, :] += pend_v
        db_ref[...] = db
        dsink_ref[0] = jnp.broadcast_to(jnp.sum(dsk[:, 0:W], axis=1, keepdims=True), (1, LANES))
        dsink_ref[1] = jnp.broadcast_to(jnp.sum(dsk[:, W:2 * W], axis=1, keepdims=True), (1, LANES))
        dq_raw, dgq = _pair_norm_bwd(q_ref[...], gq_ref[...], dq_s[...] * 0.125, is_a)
        dq_ref[...] = dq_raw.astype(BF16)
        _accumulate(dgq_ref, dgq, jnp.logical_and(a == 0, g == 0))

        @pl.when(jnp.logical_and(a == KVH - 1, g == NP - 1))
        def _():
            _fold_row(dgq_ref)

        @pl.when(g == NP - 1)
        def _():
            dk_raw, dgk = _pair_norm_bwd(k_ref[...], gk_ref[...], _fold_heads(dk_s[...]), is_a)
            dk_ref[...] = dk_raw
            _accumulate(dgk_ref, dgk, a == 0)
            dv_ref[...] = _fold_heads(dv_s[...])

    qcols = pl.BlockSpec((S, LANES), lambda a, g: (0, a * NP + g))
    kvs = pl.BlockSpec((None, S, LANES), lambda a, g: (a, 0, 0))
    sq = pl.BlockSpec((None, 2 * W, 2 * W), lambda a, g: (a * NP + g, 0, 0))
    gain = pl.BlockSpec((1, LANES), lambda a, g: (0, 0))
    ks = jax.ShapeDtypeStruct((KVH, S, LANES), F32)
    gs = jax.ShapeDtypeStruct((1, LANES), F32)
    return _pcall(body, name=name, grid=(KVH, NP),
                  in_specs=[qcols, kvs, kvs, sq, pl.BlockSpec((None, 1, 2 * W), lambda a, g: (a * NP + g, 0, 0)), gain, gain,
                            pl.BlockSpec((None, nb, 1, 2 * W), lambda a, g: (a * NP + g, 0, 0, 0)), qcols],
                  out_specs=[qcols, kvs, kvs, sq, pl.BlockSpec((2, 1, LANES), lambda a, g: (a * NP + g, 0, 0)), gain, gain],
                  out_shape=[jax.ShapeDtypeStruct((S, HQD), BF16), ks, ks, jax.ShapeDtypeStruct((PP, 2 * W, 2 * W), F32),
                             jax.ShapeDtypeStruct((2 * PP, 1, LANES), F32), gs, gs],
                  scratch_shapes=[pltpu.VMEM((2, S, LANES), BF16), pltpu.VMEM((S, LANES), BF16), pltpu.VMEM((LANES, S), BF16),
                                  pltpu.VMEM((S, LANES), BF16), pltpu.VMEM((2, S, LANES), BF16)] + [pltpu.VMEM((S, LANES), F32)] * 3,
                  compiler_params=_params(("arbitrary", "arbitrary")))(qproj, kk, vv, bias_t_ab, sink_row, gq2, gk2, lse_row, do)


def _adamw_update(w, g, m, v):
    m2 = ADAM_B1 * m + (1.0 - ADAM_B1) * g
    v2 = ADAM_B2 * v + (1.0 - ADAM_B2) * jnp.square(g)
    m_hat = m2 / (1.0 - ADAM_B1 ** ADAM_STEP)
    v_hat = v2 / (1.0 - ADAM_B2 ** ADAM_STEP)
    return -ADAM_LR * (m_hat / (jnp.sqrt(v_hat) + ADAM_EPS) + ADAM_WD * w), m2, v2


def _adamw(w, g, m, v, name, tr=256, tc=256):
    R, C = w.shape
    tr = min(tr, R)
    if R % tr == 0:
        grid, blk = (R // tr,), pl.BlockSpec((tr, C), lambda i: (i, 0))
    else:
        assert C % tc == 0
        grid, blk = (C // tc,), pl.BlockSpec((R, tc), lambda i: (0, i))

    def body(w_ref, g_ref, m_ref, v_ref, d_ref, m2_ref, v2_ref):
        d_ref[...], m2_ref[...], v2_ref[...] = _adamw_update(w_ref[...], g_ref[...], m_ref[...], v_ref[...])

    return _pcall(body, name=name, grid=grid, in_specs=[blk] * 4, out_specs=[blk] * 3,
                  out_shape=[jax.ShapeDtypeStruct((R, C), F32)] * 3, compiler_params=_params(("parallel",)))(w, g, m, v)


def _adamw_two_layers(w, g0, g1, m, v, name, tr=256):
    R, C = g0.shape
    assert R % tr == 0 and w.shape == (2 * R, C)
    nr = R // tr

    def body(w_ref, g0_ref, g1_ref, m_ref, v_ref, g_ref, d_ref, m2_ref, v2_ref):
        g = jnp.where(pl.program_id(0) == 0, g0_ref[...], g1_ref[...])
        g_ref[...] = g
        d_ref[...], m2_ref[...], v2_ref[...] = _adamw_update(w_ref[...], g, m_ref[...], v_ref[...])

    both = pl.BlockSpec((tr, C), lambda l, i: (l * nr + i, 0))
    first = pl.BlockSpec((tr, C), lambda l, i: (i * (1 - l) + (nr - 1) * l, 0))
    second = pl.BlockSpec((tr, C), lambda l, i: (i * l, 0))
    return _pcall(body, name=name, grid=(2, nr), in_specs=[both, first, second, both, both], out_specs=[both] * 4,
                  out_shape=[jax.ShapeDtypeStruct((2 * R, C), F32)] * 4,
                  compiler_params=_params(("arbitrary", "arbitrary")))(w, g0, g1, m, v)


def _sum_core_pair(arr, got, place, name, tr=512):
    P, hr, C = got.shape
    tr = tr if hr % tr == 0 else hr
    nt = hr // tr

    def body(place_ref, a_ref, g_ref, o_ref):
        o_ref[...] = (a_ref[...].astype(F32) + g_ref[...].astype(F32)).astype(o_ref.dtype)

    spec = pltpu.PrefetchScalarGridSpec(
        num_scalar_prefetch=1, grid=(P, nt),
        in_specs=[pl.BlockSpec((None, tr, C), lambda j, i, pr: (j, pr[1] * nt + i, 0)),
                  pl.BlockSpec((None, tr, C), lambda j, i, pr: (j, i, 0))],
        out_specs=pl.BlockSpec((None, tr, C), lambda j, i, pr: (j, i, 0)))
    return _pcall(body, name=name, grid_spec=spec, out_shape=jax.ShapeDtypeStruct(got.shape, BF16),
                  compiler_params=_params(("parallel", "parallel")))(place, arr, got)


def _sum_chips(pair, landed, place, name, tr=256):
    _, R, C = landed.shape
    tr = tr if R % tr == 0 else R

    def body(place_ref, p_ref, l_ref, o_ref):
        acc = p_ref[...].astype(F32)
        for k in range(3):
            acc = acc + l_ref[k].astype(F32)
        o_ref[...] = acc

    spec = pltpu.PrefetchScalarGridSpec(
        num_scalar_prefetch=1, grid=(R // tr,),
        in_specs=[pl.BlockSpec((None, tr, C), lambda i, pr: (pr[0], i, 0)), pl.BlockSpec((3, tr, C), lambda i, pr: (0, i, 0))],
        out_specs=pl.BlockSpec((None, tr, C), lambda i, pr: (pr[1], i, 0)))
    return _pcall(body, name=name, grid_spec=spec, out_shape=jax.ShapeDtypeStruct((2, R, C), F32),
                  compiler_params=_params(("parallel",)))(place, pair, landed)


def _sum_parts(parts, name, out_dtype, tr=128):
    P, R, C = parts.shape
    tr = min(tr, R)
    assert R % tr == 0, (R, tr)

    def body(p_ref, o_ref):
        acc = p_ref[0].astype(F32)
        for k in range(1, P):
            acc = acc + p_ref[k].astype(F32)
        o_ref[...] = acc.astype(o_ref.dtype)

    return _pcall(body, name=name, grid=(R // tr,), in_specs=[pl.BlockSpec((P, tr, C), lambda i: (0, i, 0))],
                  out_specs=pl.BlockSpec((tr, C), lambda i: (i, 0)),
                  out_shape=jax.ShapeDtypeStruct((R, C), out_dtype), compiler_params=_params(("parallel",)))(parts)


def _place():
    x, y, c = lax.axis_index("x"), lax.axis_index("y"), lax.axis_index("c")
    others = [(1 - x, y), (x, 1 - y), (1 - x, 1 - y)]
    return x, y, c, others


def _half_rows(ref, hh, lead=()):
    hr = ref.shape[-2] // 2
    return ref.at[(*lead, pl.ds(pl.multiple_of(hh * hr, 16), hr), slice(None))]


def _sem_arrays(*counts):
    return [pltpu.SemaphoreType.DMA((k,)) for k in counts]


SEM_SPEC = pl.BlockSpec(memory_space=pltpu.SEMAPHORE)
ANY_SPEC = pl.BlockSpec(memory_space=pl.ANY)
DATAFLOW = pltpu.SideEffectType.DATAFLOW_SIDE_EFFECTING


def _in_hbm(a):
    return pltpu.with_memory_space_constraint(a, pltpu.HBM)


def _gather_copies(srcs, lands, send_sems, recv_sems):
    x, y, c, others = _place()
    me = 2 * x + y

    def copy(w, k, dst_chip, to):
        return pltpu.make_async_remote_copy(src_ref=_half_rows(srcs[w], c), dst_ref=_half_rows(lands[w], c, (dst_chip,)),
                                            send_sem=send_sems.at[3 * w + k], recv_sem=recv_sems.at[3 * w + k],
                                            device_id=to, device_id_type=MESH)

    pairs = [(w, k, cx, cy) for w in range(len(srcs)) for k, (cx, cy) in enumerate(others)]
    return ([copy(w, k, me, (cx, cy, c)) for w, k, cx, cy in pairs],
            [copy(w, k, 2 * cx + cy, (cx, cy, c)) for w, k, cx, cy in pairs])


def _gather_start(shards, after, name):
    n = len(shards)

    def body(*refs):
        srcs, lands, send_sems, recv_sems, token = refs[:n], refs[n:2 * n], refs[2 * n + 1], refs[2 * n + 2], refs[-1]
        for cp in _gather_copies(srcs, lands, send_sems, recv_sems)[0]:
            cp.start()
        token[...] = jnp.zeros_like(token)

    lands = [lax.empty((N_CHIPS,) + s.shape, s.dtype) for s in shards]
    outs = _pcall(
        body, name=name, in_specs=[HBM_SPEC] * (2 * n) + [ANY_SPEC],
        out_specs=[SEM_SPEC, SEM_SPEC] + [HBM_SPEC] * (2 * n) + [VMEM_SPEC],
        out_shape=[pltpu.SemaphoreType.DMA((3 * n,)), pltpu.SemaphoreType.DMA((3 * n,))]
        + [pltpu.HBM(a.shape, a.dtype) for a in list(shards) + lands] + [jax.ShapeDtypeStruct((8, LANES), F32)],
        input_output_aliases={i: 2 + i for i in range(2 * n)},
        compiler_params=pltpu.CompilerParams(has_side_effects=DATAFLOW),
    )(*[_in_hbm(a) for a in list(shards) + lands], after)
    return outs[0], outs[1], outs[2:2 + n], outs[2 + n:2 + 2 * n], outs[-1]


def _gather_wait(started, after, name):
    send_sems, recv_sems, srcs, lands, _ = started
    n = len(srcs)

    def body(*refs):
        src_refs, land_refs, send_ref, recv_ref = refs[:n], refs[n:2 * n], refs[2 * n], refs[2 * n + 1]
        outgoing, incoming = _gather_copies(src_refs, land_refs, send_ref, recv_ref)
        for out_cp, in_cp in zip(outgoing, incoming):
            out_cp.wait_send()
            in_cp.wait_recv()

    outs = _pcall(
        body, name=name, in_specs=[HBM_SPEC] * (2 * n) + [SEM_SPEC, SEM_SPEC, ANY_SPEC], out_specs=[HBM_SPEC] * (2 * n),
        out_shape=[pltpu.HBM(a.shape, a.dtype) for a in list(srcs) + list(lands)],
        input_output_aliases={i: i for i in range(2 * n)},
        compiler_params=pltpu.CompilerParams(has_side_effects=DATAFLOW),
    )(*srcs, *lands, send_sems, recv_sems, after)
    return outs[:n], outs[n:]


def _gather_pass_on(shards, lands, name):
    n = len(shards)
    per = 4

    def body(*refs):
        srcs, bufs = refs[:n], refs[2 * n:3 * n]
        send_sems, recv_sems = refs[3 * n:]
        x, y, c, others = _place()
        me = 2 * x + y
        sibling = (x, y, 1 - c)

        def copy(w, k, src, dst):
            return pltpu.make_async_remote_copy(src_ref=src, dst_ref=dst, send_sem=send_sems.at[per * w + k],
                                                recv_sem=recv_sems.at[per * w + k], device_id=sibling, device_id_type=MESH)

        sends, recvs = [], []
        for w in range(n):
            for k, (cx, cy) in enumerate(others):
                mine, theirs = _half_rows(bufs[w], c, (2 * cx + cy,)), _half_rows(bufs[w], 1 - c, (2 * cx + cy,))
                sends.append(copy(w, k, mine, mine))
                recvs.append(copy(w, k, theirs, theirs))
            sends.append(copy(w, 3, srcs[w], bufs[w].at[me]))
            recvs.append(sends[-1])
        for cp in sends:
            cp.start()
        for snd, rcv in zip(sends, recvs):
            snd.wait_send()
            rcv.wait_recv()

    return _pcall(body, name=name, in_specs=[HBM_SPEC] * (2 * n), out_specs=[HBM_SPEC] * n,
                  out_shape=[jax.ShapeDtypeStruct(l.shape, l.dtype) for l in lands],
                  input_output_aliases={n + w: w for w in range(n)},
                  scratch_shapes=_sem_arrays(per * n, per * n))(*shards, *lands)


def _scatter_copies(srcs, lands, send_sems, recv_sems):
    x, y, c, others = _place()
    return [pltpu.make_async_remote_copy(src_ref=srcs[w].at[2 * cx + cy], dst_ref=lands[w].at[k],
                                         send_sem=send_sems.at[3 * w + k], recv_sem=recv_sems.at[3 * w + k],
                                         device_id=(cx, cy, c), device_id_type=MESH)
            for w in range(len(srcs)) for k, (cx, cy) in enumerate(others)]


def _scatter_start(parts, name):
    n = len(parts)

    def body(*refs):
        srcs, lands, send_sems, recv_sems, token = refs[:n], refs[n:2 * n], refs[2 * n], refs[2 * n + 1], refs[-1]
        for cp in _scatter_copies(srcs, lands, send_sems, recv_sems):
            cp.start()
        token[...] = jnp.zeros_like(token)

    lands = [lax.empty((3,) + p.shape[1:], p.dtype) for p in parts]
    outs = _pcall(
        body, name=name, in_specs=[HBM_SPEC] * (2 * n), out_specs=[SEM_SPEC, SEM_SPEC] + [HBM_SPEC] * (2 * n) + [VMEM_SPEC],
        out_shape=[pltpu.SemaphoreType.DMA((3 * n,)), pltpu.SemaphoreType.DMA((3 * n,))]
        + [pltpu.HBM(a.shape, a.dtype) for a in list(parts) + lands] + [jax.ShapeDtypeStruct((8, LANES), F32)],
        input_output_aliases={i: 2 + i for i in range(2 * n)},
        compiler_params=pltpu.CompilerParams(has_side_effects=DATAFLOW),
    )(*[_in_hbm(a) for a in list(parts) + lands])
    return (outs[0], outs[1], outs[2:2 + n], outs[2 + n:2 + 2 * n]), outs[-1]


def _scatter_wait(started, after, name):
    send_sems, recv_sems, srcs, lands = started
    n = len(srcs)

    def body(*refs):
        for cp in _scatter_copies(refs[:n], refs[n:2 * n], refs[2 * n], refs[2 * n + 1]):
            cp.wait_send()
            cp.wait_recv()

    outs = _pcall(
        body, name=name, in_specs=[HBM_SPEC] * (2 * n) + [SEM_SPEC, SEM_SPEC, ANY_SPEC], out_specs=[HBM_SPEC] * (2 * n),
        out_shape=[pltpu.HBM(a.shape, a.dtype) for a in list(srcs) + list(lands)],
        input_output_aliases={i: i for i in range(2 * n)},
        compiler_params=pltpu.CompilerParams(has_side_effects=DATAFLOW),
    )(*srcs, *lands, send_sems, recv_sems, after)
    return outs[:n], outs[n:]


def _split_start(plan, arrays, n_copies, after, name):
    n = len(arrays)

    def body(*refs):
        for cp in plan(refs[:n], refs[n + 1], refs[n + 2])[0]:
            cp.start()
        refs[-1][...] = jnp.zeros_like(refs[-1])

    outs = _pcall(
        body, name=name, in_specs=[HBM_SPEC] * n + [ANY_SPEC], out_specs=[SEM_SPEC, SEM_SPEC] + [HBM_SPEC] * n + [VMEM_SPEC],
        out_shape=[pltpu.SemaphoreType.DMA((n_copies,)), pltpu.SemaphoreType.DMA((n_copies,))]
        + [pltpu.HBM(a.shape, a.dtype) for a in arrays] + [jax.ShapeDtypeStruct((8, LANES), F32)],
        input_output_aliases={i: 2 + i for i in range(n)},
        compiler_params=pltpu.CompilerParams(has_side_effects=DATAFLOW),
    )(*[_in_hbm(a) for a in arrays], after)
    return (outs[0], outs[1], outs[2:2 + n]), outs[-1]


def _split_wait(plan, started, after, name):
    send_sems, recv_sems, arrays = started
    n = len(arrays)

    def body(*refs):
        outgoing, incoming = plan(refs[:n], refs[n], refs[n + 1])
        for cp in outgoing:
            cp.wait_send()
        for cp in incoming:
            cp.wait_recv()

    return _pcall(
        body, name=name, in_specs=[HBM_SPEC] * n + [SEM_SPEC, SEM_SPEC, ANY_SPEC], out_specs=[HBM_SPEC] * n,
        out_shape=[pltpu.HBM(a.shape, a.dtype) for a in arrays], input_output_aliases={i: i for i in range(n)},
        compiler_params=pltpu.CompilerParams(has_side_effects=DATAFLOW),
    )(*arrays, send_sems, recv_sems, after)


def _to_sibling(src, dst, k, send_sems, recv_sems):
    x, y, c, _ = _place()
    return pltpu.make_async_remote_copy(src_ref=src, dst_ref=dst, send_sem=send_sems.at[k], recv_sem=recv_sems.at[k],
                                        device_id=(x, y, 1 - c), device_id_type=MESH)


def _plan_pass_on(n):
    def plan(refs, send_sems, recv_sems):
        x, y, c, others = _place()
        cps = []
        for w in range(n):
            for k, (cx, cy) in enumerate(others):
                mine = _half_rows(refs[n + w], c, (2 * cx + cy,))
                cps.append(_to_sibling(mine, mine, 4 * w + k, send_sems, recv_sems))
            cps.append(_to_sibling(refs[w], refs[n + w].at[2 * x + y], 4 * w + 3, send_sems, recv_sems))
        return cps, cps
    return plan


def _plan_swap_halves(n):
    def plan(refs, send_sems, recv_sems):
        c = lax.axis_index("c")
        cps = [_to_sibling(_half_rows(refs[w], 1 - c, (slice(None),)), refs[n + w], w, send_sems, recv_sems) for w in range(n)]
        return cps, cps
    return plan


def _plan_share_halves(n):
    def plan(refs, send_sems, recv_sems):
        c = lax.axis_index("c")
        cps = [_to_sibling(refs[w].at[c], refs[w].at[c], w, send_sems, recv_sems) for w in range(n)]
        return cps, cps
    return plan


def _plan_gather_small(refs, send_sems, recv_sems):
    x, y, c, _ = _place()
    flips = [(dx, dy, dc) for dx in (0, 1) for dy in (0, 1) for dc in (0, 1)][1:]
    flip = lambda v, d: 1 - v if d else v
    cps = [pltpu.make_async_remote_copy(src_ref=refs[0], dst_ref=refs[1].at[4 * x + 2 * y + c], send_sem=send_sems.at[k],
                                        recv_sem=recv_sems.at[k], device_id=(flip(x, dx), flip(y, dy), flip(c, dc)),
                                        device_id_type=MESH)
           for k, (dx, dy, dc) in enumerate(flips)]
    return cps, cps


def _sum_gathered_small(gathered, own, place, name):
    _, M, C = gathered.shape

    def body(place_ref, g_ref, own_ref, o_ref):
        me = 2 * place_ref[0] + place_ref[1]
        acc = jnp.zeros((M, C), F32)
        for k in range(8):
            acc = acc + jnp.where(me == k, own_ref[...], g_ref[k])
        o_ref[...] = acc

    spec = pltpu.PrefetchScalarGridSpec(
        num_scalar_prefetch=1, grid=(1,),
        in_specs=[pl.BlockSpec((8, M, C), lambda i, pr: (0, 0, 0)), pl.BlockSpec((M, C), lambda i, pr: (0, 0))],
        out_specs=pl.BlockSpec((M, C), lambda i, pr: (0, 0)))
    return _pcall(body, name=name, grid_spec=spec, out_shape=jax.ShapeDtypeStruct((M, C), F32),
                  compiler_params=_params(("arbitrary",)))(place, gathered, own)


def _allgather_group(shards, name):
    n = len(shards)
    per = 7

    def body(*refs):
        ins, outs = refs[:n], refs[n:2 * n]
        send_sems, recv_sems = refs[2 * n:]
        x, y, c, others = _place()
        me = 2 * x + y
        sibling = (x, y, 1 - c)

        def copy(w, k, src, dst, to):
            return pltpu.make_async_remote_copy(src_ref=src, dst_ref=dst, send_sem=send_sems.at[per * w + k],
                                                recv_sem=recv_sems.at[per * w + k], device_id=to, device_id_type=MESH)

        first = [copy(w, k, _half_rows(ins[w], c), _half_rows(outs[w], c, (me,)), (cx, cy, c))
                 for w in range(n) for k, (cx, cy) in enumerate(others)]
        own = [copy(w, 6, ins[w], outs[w].at[me], sibling) for w in range(n)]
        for cp in first + own:
            cp.start()
        passed = []
        for w in range(n):
            for k, (cx, cy) in enumerate(others):
                landed = _half_rows(outs[w], c, (2 * cx + cy,))
                copy(w, k, landed, landed, sibling).wait_recv()
                passed.append(copy(w, 3 + k, landed, landed, sibling))
                passed[-1].start()
        for w in range(n):
            for k, (cx, cy) in enumerate(others):
                theirs = _half_rows(outs[w], 1 - c, (2 * cx + cy,))
                copy(w, 3 + k, theirs, theirs, sibling).wait_recv()
            own[w].wait_recv()
        for cp in first + passed + own:
            cp.wait_send()

    return _pcall(body, name=name, in_specs=[HBM_SPEC] * n, out_specs=[HBM_SPEC] * n,
                  out_shape=[jax.ShapeDtypeStruct((N_CHIPS,) + s.shape, s.dtype) for s in shards],
                  scratch_shapes=_sem_arrays(per * n, per * n))(*shards)


def _swap_halves_group(arrs, name):
    n = len(arrs)

    def body(*refs):
        ins, gots = refs[:n], refs[n:2 * n]
        send_sems, recv_sems = refs[2 * n:]
        x, y, c, _ = _place()
        swaps = [pltpu.make_async_remote_copy(src_ref=_half_rows(ins[w], 1 - c, (slice(None),)), dst_ref=gots[w],
                                              send_sem=send_sems.at[w], recv_sem=recv_sems.at[w],
                                              device_id=(x, y, 1 - c), device_id_type=MESH) for w in range(n)]
        for cp in swaps:
            cp.start()
        for cp in swaps:
            cp.wait()

    half_shapes = [jax.ShapeDtypeStruct((a.shape[0], a.shape[1] // 2, a.shape[2]), a.dtype) for a in arrs]
    return _pcall(body, name=name, in_specs=[HBM_SPEC] * n, out_specs=[HBM_SPEC] * n, out_shape=half_shapes,
                  scratch_shapes=_sem_arrays(n, n))(*arrs)


def _scatter_group(parts, name):
    n = len(parts)

    def body(*refs):
        ins, outs = refs[:n], refs[n:2 * n]
        send_sems, recv_sems = refs[2 * n:]
        x, y, c, others = _place()

        def copy(w, k, src_chip, to):
            return pltpu.make_async_remote_copy(src_ref=ins[w].at[src_chip], dst_ref=outs[w].at[k],
                                                send_sem=send_sems.at[3 * w + k], recv_sem=recv_sems.at[3 * w + k],
                                                device_id=to, device_id_type=MESH)

        sends = [copy(w, k, 2 * cx + cy, (cx, cy, c)) for w in range(n) for k, (cx, cy) in enumerate(others)]
        for cp in sends:
            cp.start()
        for cp in sends:
            cp.wait()

    return _pcall(body, name=name, in_specs=[HBM_SPEC] * n, out_specs=[HBM_SPEC] * n,
                  out_shape=[jax.ShapeDtypeStruct((3,) + p.shape[1:], p.dtype) for p in parts],
                  scratch_shapes=_sem_arrays(3 * n, 3 * n))(*parts)


def _share_halves_group(halves, name):
    n = len(halves)

    def body(*refs):
        bufs = refs[n:2 * n]
        send_sems, recv_sems = refs[2 * n:]
        x, y, c, _ = _place()
        swaps = [pltpu.make_async_remote_copy(src_ref=bufs[w].at[c], dst_ref=bufs[w].at[c], send_sem=send_sems.at[w],
                                              recv_sem=recv_sems.at[w], device_id=(x, y, 1 - c), device_id_type=MESH)
                 for w in range(n)]
        for cp in swaps:
            cp.start()
        for w in range(n):
            swaps[w].wait_send()
            pltpu.make_async_remote_copy(src_ref=bufs[w].at[c], dst_ref=bufs[w].at[1 - c], send_sem=send_sems.at[w],
                                         recv_sem=recv_sems.at[w], device_id=(x, y, 1 - c), device_id_type=MESH).wait_recv()

    return _pcall(body, name=name, in_specs=[HBM_SPEC] * n, out_specs=[HBM_SPEC] * n,
                  out_shape=[jax.ShapeDtypeStruct(h.shape, h.dtype) for h in halves],
                  input_output_aliases={w: w for w in range(n)},
                  scratch_shapes=_sem_arrays(n, n))(*halves)


def _allgather_small(blk, name):
    M, C = blk.shape

    def body(x_ref, out_ref, send_sems, recv_sems, local_sem):
        x, y, c, others = _place()
        me, sibling = (x, y, c), (x, y, 1 - c)

        def rows(px, py, pc):
            return out_ref.at[4 * px + 2 * py + pc]

        def copy(k, block, to, src=None):
            return pltpu.make_async_remote_copy(src_ref=rows(*block) if src is None else src, dst_ref=rows(*block),
                                                send_sem=send_sems.at[k], recv_sem=recv_sems.at[k], device_id=to, device_id_type=MESH)

        mine = pltpu.make_async_copy(x_ref, rows(*me), local_sem)
        mine.start()
        first = [copy(0, me, sibling, src=x_ref)]
        first += [copy(1 + j, me, (*chip, c), src=x_ref) for j, chip in enumerate(others)]
        for cp in first:
            cp.start()
        passed = [copy(4 + j, (*chip, c), sibling) for j, chip in enumerate(others)]
        for j, chip in enumerate(others):
            copy(1 + j, (*chip, c), me).wait_recv()
            passed[j].start()
        copy(0, sibling, me).wait_recv()
        for j, chip in enumerate(others):
            copy(4 + j, (*chip, 1 - c), me).wait_recv()
        for cp in first + passed:
            cp.wait_send()
        mine.wait()

    return _pcall(body, name=name, in_specs=[VMEM_SPEC], out_specs=VMEM_SPEC,
                  out_shape=jax.ShapeDtypeStruct((8, M, C), blk.dtype),
                  scratch_shapes=[pltpu.SemaphoreType.DMA((7,)), pltpu.SemaphoreType.DMA((7,)), pltpu.SemaphoreType.DMA])(blk)


def _pack_rows(n_elems, width=PACK_W, align=PACK_ROW_ALIGN):
    rows = -(-n_elems // width)
    return -(-rows // align) * align


def _pack(arrays, dtype, width=PACK_W, align=PACK_ROW_ALIGN):
    flat = jnp.concatenate([a.astype(dtype).reshape(-1) for a in arrays])
    rows = _pack_rows(flat.shape[0], width, align)
    flat = jnp.pad(flat, (0, rows * width - flat.shape[0]))
    return flat.reshape(rows, width)


def _pack_small(arrays):
    return _pack(arrays, F32, width=128, align=8)


def _unpack(flat, shapes):
    out, off = [], 0
    for shp in shapes:
        n = int(np.prod(shp))
        out.append(flat[..., off:off + n].reshape(flat.shape[:-1] + tuple(shp)))
        off += n
    return out


def _doubled_heads(x2d, n_heads):
    S = x2d.shape[0]
    h = x2d.reshape(S, n_heads, HEAD_DIM).transpose(1, 0, 2)
    return jnp.concatenate([h, h], axis=-1)


def _rms_bwd_epilogue(dn, x, dres, g):
    r = _rinv(x)
    xh = x * r
    dxh = dn * g
    dx = dres + r * (dxh - xh * jnp.mean(dxh * xh, axis=-1, keepdims=True))
    return dx, dx, jnp.sum(dn * xh, axis=0, keepdims=True)


def _residual_then_norms(acc, res, *gains):
    h = res + acc
    hn = h * _rinv(h)
    return (h,) + tuple(hn * g for g in gains)


def _mlp_fwd(h, n, w_up4, w_down, next_gains, tag, between=None):
    u, a = _matmul(n, w_up4, "nn", f"up{tag}", out_dtypes=(F32, BF16), chipwise="b", tm=2048, tn=512,
                   epilogue=lambda acc: (acc, jnp.square(jnp.maximum(acc, 0.0))))
    if between is not None:
        next_gains = [next_gains[0] + between(a)] + list(next_gains[1:])
    assert w_down.shape[1] == 1024
    outs = _matmul(a, w_down, "nn", f"down{tag}", out_dtypes=(F32,) + (BF16,) * len(next_gains), extras=(h,),
                   row_extras=tuple(next_gains), epilogue=_residual_then_norms, tm=1024, tn=1024, tk=1024)
    outs = outs if next_gains else (outs,)
    return outs[0], outs[1:], (n, u, a)


def _mlp_bwd(dh_out, dh_out_b, h, g, w_up4, w_down, saved, tag):
    n, u, a = saved
    dw_down = _matmul(a, dh_out_b, "tn", f"dw_down{tag}", out_dtypes=(BF16,), tm=1024, tn=1024, chipwise="out_rows")
    du = _matmul(dh_out_b, w_down, "nt", f"du{tag}", out_dtypes=(BF16,), extras=(u,), tm=2048, tn=512,
                 epilogue=lambda acc, uu: (acc * (2.0 * jnp.maximum(uu, 0.0)),))
    dw_up = _matmul(n, du, "tn", f"dw_up{tag}", out_dtypes=(BF16,), chipwise="out", tm=1024, tn=512)
    dh, dh_b, dg = _matmul(du, w_up4, "nt", f"dn_mlp{tag}", out_dtypes=(F32, BF16, F32), extras=(h, dh_out), row_extras=(g,),
                           epilogue=_rms_bwd_epilogue, row_accums=1, tm=1024, tn=1024, tk=w_up4.shape[2], chipwise="b")
    return dh, dh_b, dg, dw_up, dw_down


def kernel(x, g_attn, g_mlp, w_in_a, b_f, gq_a, gk_a, w_out_a, g_kv, w_kv, gk_b, w_q_b, gq_b, sinks, rel_bias, w_out_b, w_up, w_down, loss_target, m_g_attn, m_g_mlp, m_w_in_a, m_b_f, m_gq_a, m_gk_a, m_w_out_a, m_g_kv, m_w_kv, m_gk_b, m_w_q_b, m_gq_b, m_sinks, m_rel_bias, m_w_out_b, m_w_up, m_w_down, v_g_attn, v_g_mlp, v_w_in_a, v_b_f, v_gq_a, v_gk_a, v_w_out_a, v_g_kv, v_w_kv, v_gk_b, v_w_q_b, v_gq_b, v_sinks, v_rel_bias, v_w_out_b, v_w_up, v_w_down):
    given = dict(locals())
    S, D = x.shape[1], x.shape[2]
    H = D // HEAD_DIM
    KVH = w_kv.shape[1] // (2 * HEAD_DIM)
    kvw = KVH * HEAD_DIM
    hw = H * HEAD_DIM
    W = WINDOW
    nb = S // W
    c_idx = lax.axis_index("c")
    xs, tgt = x[0], loss_target[0]

    n_in_shard = w_in_a.shape[2]
    rows_in = -(-n_in_shard // 32) * 32
    row_pad = lambda a: jnp.pad(a, [(0, 0)] * (a.ndim - 2) + [(0, rows_in - a.shape[-2]), (0, 0)])
    t_in = lambda a: jnp.swapaxes(a[0], 0, 1)
    shards = {"w_in_a": row_pad(t_in(w_in_a)), "w_out_a": w_out_a[0], "w_up0": w_up[0], "w_down0": w_down[0], "w_kv": w_kv,
              "w_q_b": w_q_b[0], "w_out_b": w_out_b[0], "w_up1": w_up[1], "w_down1": w_down[1]}
    parts = list(shards)
    groups = [("w_in_a", "w_out_a"), ("w_up0", "w_down0"), ("w_kv", "w_q_b", "w_out_b", "w_up1", "w_down1")]
    started = []
    for i, grp in enumerate(groups):
        behind = started[-1][4] if started else g_attn[0]
        started.append(_gather_start([shards[n].astype(BF16) for n in grp], behind, f"gather_start{i}"))
    gathered = {}

    def finish_gather(i, after):
        srcs, lands = _gather_wait(started[i], after, f"gather_wait{i}")
        gathered.update(zip(groups[i], _gather_pass_on(srcs, lands, f"gather_pass_on{i}")))

    def land_gather(i, after):
        srcs, lands = _gather_wait(started[i], after, f"gather_wait{i}")
        n = len(srcs)
        passing, token = _split_start(_plan_pass_on(n), list(srcs) + list(lands), 4 * n, after, f"pass_on_start{i}")
        return passing, token[0:1, 0:1]

    def finish_pass_on(i, passing, after):
        n = len(groups[i])
        gathered.update(zip(groups[i], _split_wait(_plan_pass_on(n), passing, after, f"pass_on_wait{i}")[n:]))

    vec = lambda a: a.reshape(1, -1)
    twice = lambda a: jnp.tile(a.reshape(1, -1), (1, 2))

    g_attn0 = vec(g_attn[0]) + sum(st[4][0, 0] for st in started)
    (n0,) = _rms_fwd(xs, [g_attn0], "rms_attn0")
    finish_gather(0, n0)
    win_t = gathered["w_in_a"][:, :n_in_shard].reshape(-1, D)
    win_t = jnp.pad(win_t, ((0, (-win_t.shape[0]) % 128), (0, 0)))
    wout_a = gathered["w_out_a"].reshape(-1, D)
    n_in = win_t.shape[0]
    tile_in = 640 if n_in % 640 == 0 else 128
    proj = _matmul(n0, win_t, "nt", "proj_in", tm=2048, tn=tile_in)
    zt = proj[:, 3 * hw:3 * hw + H].T
    c_row = _gate_fwd(zt, b_f.reshape(H, 1), "gate_fwd")
    c_row3 = c_row.reshape(H, 1, S)
    o_a, lse_a = _fox_fwd(proj, c_row3, twice(gq_a[0]), twice(gk_a[0]), H, "fox_fwd")
    passing1, tie = land_gather(1, o_a)
    h1, n1 = _matmul(o_a, wout_a, "nn", "out_a", out_dtypes=(F32, BF16), extras=(xs,), row_extras=(vec(g_mlp[0]) + tie,),
                     epilogue=_residual_then_norms, tm=1024, tn=1024)
    finish_pass_on(1, passing1, n1)
    wup = [gathered["w_up0"], None]
    wdown = [gathered["w_down0"].reshape(-1, D), None]
    passing = []

    def land_last_group(a):
        started_passing, tie = land_gather(2, a)
        passing.append(started_passing)
        return tie

    h2, (nkv, n2), mlp0 = _mlp_fwd(h1, n1, wup[0], wdown[0], [vec(g_kv), vec(g_attn[1])], "0", between=land_last_group)
    passing2 = passing[0]

    finish_pass_on(2, passing2, h2)
    wq_b, wout_b = gathered["w_q_b"].reshape(-1, D), gathered["w_out_b"].reshape(-1, D)
    wkv = gathered["w_kv"].reshape(D, -1)
    wup[1], wdown[1] = gathered["w_up1"], gathered["w_down1"].reshape(-1, D)
    kv = _matmul(nkv, wkv, "nn", "proj_kv", tm=2048)
    kk, vv = _doubled_heads(kv[:, :kvw], KVH), _doubled_heads(kv[:, kvw:], KVH)
    q2 = _matmul(n2, wq_b, "nn", "proj_q", tm=1024, tn=1024)
    onehot = jnp.asarray(_bucket_onehot(), dtype=BF16)
    bias = _bias_expand(rel_bias.T, onehot, "bias_expand").reshape(H, W, 2 * W)
    bias_ab = bias.reshape(H // 2, 2 * W, 2 * W)
    bias_t_ab = bias.reshape(H // 2, 2, W, 2 * W).transpose(0, 3, 1, 2).reshape(H // 2, 2 * W, 2 * W)
    sink_ab = jnp.repeat(sinks[0].reshape(H // 2, 2), W, axis=1)
    o_b, lse_b = _swa_fwd(q2, kk, vv, bias_ab, sink_ab.reshape(H // 2, 2 * W, 1), twice(gq_b[0]), twice(gk_b), "swa_fwd")
    h3, n3 = _matmul(o_b, wout_b, "nn", "out_b", out_dtypes=(F32, BF16), extras=(h2,), row_extras=(vec(g_mlp[1]),),
                     epilogue=_residual_then_norms, tm=1024, tn=1024)
    h4, _, mlp1 = _mlp_fwd(h3, n3, wup[1], wdown[1], [], "1")

    dh4, dh4_b, loss_part = _loss_head(h4, tgt, "loss_head")

    place = jnp.stack([2 * lax.axis_index("x") + lax.axis_index("y"), c_idx]).astype(jnp.int32)
    scattering = []

    def pair_and_scatter(names, mine, got):
        pair_sums = [_sum_core_pair(a, g, place, "sum_core_pair_" + n) for n, a, g in zip(names, mine, got)]
        started_scatter, token = _scatter_start(pair_sums, "scatter_start_" + names[0])
        scattering.append((names, started_scatter))
        return token[0:1, :]

    def start_reduce(named):
        names = list(named)
        mine = [named[n] for n in names]
        return pair_and_scatter(names, mine, _swap_halves_group(mine, "swap_grad_halves_" + names[0]))

    def start_swap(named):
        names = list(named)
        mine = [named[n] for n in names]
        lands = [lax.empty((a.shape[0], a.shape[1] // 2, a.shape[2]), a.dtype) for a in mine]
        swapping, token = _split_start(_plan_swap_halves(len(mine)), mine + lands, len(mine), mine[0], "swap_start_" + names[0])
        return names, swapping, token

    def finish_swap(swap, after):
        names, swapping, _ = swap
        n = len(names)
        arrays = _split_wait(_plan_swap_halves(n), swapping, after, "swap_wait_" + names[0])
        return pair_and_scatter(names, arrays[:n], arrays[n:])

    dh3, dh3_b, dg_mlp1, dw_up1, dw_down1 = _mlp_bwd(dh4, dh4_b, h3, vec(g_mlp[1]), wup[1], wdown[1], mlp1, "1")
    swap1 = start_swap({"w_down1": dw_down1, "w_up1": dw_up1})
    do_b = _matmul(dh3_b, wout_b, "nt", "do_b", tm=1024, tn=1024, after=swap1[2])
    dw_out_b = _matmul(o_b, dh3_b, "tn", "dw_out_b", out_dtypes=(BF16,), tm=1024, tn=1024, after=do_b)
    tie1 = finish_swap(swap1, dw_out_b)
    dq2, dk2, dv2, dbias_t_ab, dsink, dgq_b, dgk_b = _swa_bwd(
        q2, kk, vv, bias_t_ab, sink_ab.reshape(H // 2, 1, 2 * W), twice(gq_b[0]) + tie1, twice(gk_b),
        lse_b, do_b, "swa_bwd")
    dbias = dbias_t_ab.reshape(H // 2, 2 * W, 2, W).transpose(0, 2, 3, 1).reshape(H, W * 2 * W)
    d_rel_bias = _bias_reduce(dbias, onehot, "bias_reduce").T
    dw_q_b = _matmul(n2, dq2, "tn", "dw_q_b", out_dtypes=(BF16,), tm=1024, tn=1024)
    dn2 = _matmul(dq2, wq_b, "nt", "dn2", tm=1024, tn=1024)
    dkv = jnp.concatenate([dk2[h, :, :HEAD_DIM] for h in range(KVH)] + [dv2[h, :, :HEAD_DIM] for h in range(KVH)],
                          axis=1).astype(BF16)
    dw_kv = _matmul(nkv, dkv, "tn", "dw_kv", out_dtypes=(BF16,), tm=1024)
    dnkv = _matmul(dkv, wkv, "nt", "dnkv", tm=1024, tn=1024)
    tie2 = start_reduce({"w_out_b": dw_out_b.reshape(N_CHIPS, -1, D), "w_q_b": dw_q_b.reshape(N_CHIPS, -1, D),
                         "w_kv": dw_kv.reshape(N_CHIPS, -1, 2 * kvw)})
    dh2, dh2_b, (dg_kv, dg_attn1) = _rms_bwd(h2, dh3, [vec(g_kv) + tie2[:, :1], vec(g_attn[1])], [dnkv, dn2], "rms_attn1_bwd")

    dh1, dh1_b, dg_mlp0, dw_up0, dw_down0 = _mlp_bwd(dh2, dh2_b, h1, vec(g_mlp[0]), wup[0], wdown[0], mlp0, "0")
    swap3 = start_swap({"w_down0": dw_down0, "w_up0": dw_up0})
    do_a = _matmul(dh1_b, wout_a, "nt", "do_a", tm=1024, tn=1024, after=swap3[2])
    dw_out_a = _matmul(o_a, dh1_b, "tn", "dw_out_a", out_dtypes=(BF16,), tm=1024, tn=1024, after=do_a)
    tie3 = finish_swap(swap3, dw_out_a)
    dq_a, dk_a, dv_a, dc_row, dgq_a, dgk_a = _fox_bwd(
        proj, c_row3, twice(gq_a[0]) + tie3, twice(gk_a[0]), lse_a, do_a, H, "fox_bwd")
    dzt, db_f = _gate_bwd(dc_row.reshape(H, S), zt, b_f.reshape(H, 1), "gate_bwd")
    parts_in = [dq_a, dk_a, dv_a, dzt.T.astype(BF16)]

    def true_columns(lo, hi):
        out, off = [], 0
        for part in parts_in:
            a, b = max(lo, off), min(hi, off + part.shape[1])
            if a < b:
                out.append(part[:, a - off:b - off])
            off += part.shape[1]
        return out

    pieces = []
    for j in range(N_CHIPS):
        pieces += true_columns(j * n_in_shard, (j + 1) * n_in_shard) + [jnp.zeros((S, rows_in - n_in_shard), BF16)]
    dproj = jnp.concatenate(pieces, axis=1)
    n_blocked = N_CHIPS * rows_in
    tile_blocked = 640 if n_blocked % 640 == 0 else 128
    dw_in_t = _matmul(dproj, n0, "tn", "dw_in", out_dtypes=(BF16,), tm=tile_blocked, tn=1024)
    tie4 = start_reduce({"w_out_a": dw_out_a.reshape(N_CHIPS, -1, D), "w_in_a": dw_in_t.reshape(N_CHIPS, rows_in, D)})
    grad_x, _, dg_attn0 = _matmul(dproj, gathered["w_in_a"].reshape(n_blocked, D), "nn", "dn0", out_dtypes=(F32, BF16, F32),
                                  extras=(xs, dh1), row_extras=(vec(g_attn[0]) + tie4[:, :1],), epilogue=_rms_bwd_epilogue,
                                  row_accums=1, tm=1024, tn=1024, tk=tile_blocked)

    small_grads = {
        "g_attn": jnp.concatenate([dg_attn0, dg_attn1], axis=0), "g_mlp": jnp.concatenate([dg_mlp0, dg_mlp1], axis=0),
        "b_f": db_f.reshape(1, H), "gq_a": dgq_a[:, :HEAD_DIM], "gk_a": dgk_a[:, :HEAD_DIM], "g_kv": dg_kv.reshape(-1),
        "gk_b": dgk_b[0, :HEAD_DIM], "gq_b": dgq_b[:, :HEAD_DIM], "sinks": dsink[:, 0, 0].reshape(1, H), "rel_bias": d_rel_bias,
    }
    small_shapes = [given[n].shape for n in SMALL] + [(1,)]
    spack = _pack_small([small_grads[n] for n in SMALL] + [loss_part])
    gathering_small, token = _split_start(_plan_gather_small, [spack, lax.empty((8,) + spack.shape, F32)], 7, grad_x,
                                          "gather_small_start")
    sharing = []
    for names, started_scatter in scattering:
        pair_sums, landed = _scatter_wait(started_scatter, grad_x, "scatter_wait_" + names[0])
        halves = [_sum_chips(p, l, place, "sum_chips_" + n) for n, p, l in zip(names, pair_sums, landed)]
        started_share, token = _split_start(_plan_share_halves(len(halves)), halves, len(halves), token, "share_start_" + names[0])
        sharing.append((names, started_share))
    own_small, others_small = _split_wait(_plan_gather_small, gathering_small, token, "gather_small_wait")
    small_sum = _sum_gathered_small(others_small, own_small, place, "sum_small")
    small_red = _unpack(small_sum.reshape(-1), small_shapes)
    reduced = {}
    for names, started_share in sharing:
        for n, r in zip(names, _split_wait(_plan_share_halves(len(names)), started_share, token, "share_wait_" + names[0])):
            reduced[n] = r.reshape(-1, r.shape[2])
    reduced["w_in_a"] = reduced["w_in_a"][:n_in_shard]
    loss = small_red[-1][0]

    grads = dict(zip(SMALL, small_red))
    no_loss = [jnp.zeros((1,), F32)]
    sw = _pack_small([given[n] for n in SMALL] + no_loss)
    sm = _pack_small([given["m_" + n] for n in SMALL] + no_loss)
    sv = _pack_small([given["v_" + n] for n in SMALL] + no_loss)
    sd, sm2, sv2 = _adamw(sw, small_sum, sm, sv, "adamw_small", tr=sw.shape[0])
    delta = dict(zip(SMALL, _unpack(sd.reshape(-1), small_shapes)))
    new_m = dict(zip(SMALL, _unpack(sm2.reshape(-1), small_shapes)))
    new_v = dict(zip(SMALL, _unpack(sv2.reshape(-1), small_shapes)))
    for n in ("w_out_a", "w_kv", "w_q_b", "w_out_b"):
        w = given[n]
        two_d = (-1, w.shape[-1])
        d, m2, v2 = _adamw(w.reshape(two_d), reduced[n], given["m_" + n].reshape(two_d), given["v_" + n].reshape(two_d),
                           "adamw_" + n)
        grads[n] = reduced[n].reshape(w.shape)
        delta[n], new_m[n], new_v[n] = d.reshape(w.shape), m2.reshape(w.shape), v2.reshape(w.shape)
    d, m2, v2 = _adamw(t_in(w_in_a), reduced["w_in_a"], t_in(m_w_in_a), t_in(v_w_in_a), "adamw_w_in_a")
    back = lambda a: jnp.swapaxes(a, 0, 1)[None]
    grads["w_in_a"], delta["w_in_a"], new_m["w_in_a"], new_v["w_in_a"] = back(reduced["w_in_a"]), back(d), back(m2), back(v2)
    for n in ("w_up", "w_down"):
        w = given[n]
        two_d = (-1, w.shape[-1])
        g, d, m2, v2 = _adamw_two_layers(w.reshape(two_d), reduced[n + "0"], reduced[n + "1"], given["m_" + n].reshape(two_d),
                                         given["v_" + n].reshape(two_d), "adamw_" + n)
        grads[n], delta[n], new_m[n], new_v[n] = g.reshape(w.shape), d.reshape(w.shape), m2.reshape(w.shape), v2.reshape(w.shape)

    order = ["g_attn", "g_mlp", "w_in_a", "b_f", "gq_a", "gk_a", "w_out_a", "g_kv", "w_kv", "gk_b", "w_q_b", "gq_b",
             "sinks", "rel_bias", "w_out_b", "w_up", "w_down"]
    return (loss, grad_x[None], *[grads[n] for n in order], *[delta[n] for n in order],
            *[new_m[n] for n in order], *[new_v[n] for n in order])
```

```python
import numpy as np
import jax
import jax.numpy as jnp
from jax import lax
from jax.experimental import pallas as pl
from jax.experimental.pallas import tpu as pltpu

F32 = jnp.float32
BF16 = jnp.bfloat16
MESH = pl.DeviceIdType.MESH

HEAD_DIM = 64
LANES = 128
WINDOW = 128
N_BUCKETS = 32
REL_MAX_DIST = 128
NORM_EPS = 1e-6
ADAM_LR = 0.001
ADAM_B1 = 0.9
ADAM_B2 = 0.999
ADAM_EPS = 1e-08
ADAM_WD = 0.01
ADAM_STEP = 10
NEG = -1e30
N_CHIPS = 4
VMEM_LIMIT = 56 * 1024 * 1024
HBM_SPEC = pl.BlockSpec(memory_space=pltpu.HBM)
VMEM_SPEC = pl.BlockSpec(memory_space=pltpu.VMEM)

SMALL = ("g_attn", "g_mlp", "b_f", "gq_a", "gk_a", "g_kv", "gk_b", "gq_b", "sinks", "rel_bias")


def _pcall(body, **kw):
    return pl.pallas_call(body, **kw)


def _params(sem=None):
    return pltpu.CompilerParams(dimension_semantics=sem, vmem_limit_bytes=VMEM_LIMIT)


def _rinv(x):
    return lax.rsqrt(jnp.mean(x * x, axis=-1, keepdims=True) + NORM_EPS)


def _dot(a, b, dims, precision=None):
    return lax.dot_general(a, b, (dims, ((), ())), precision=precision, preferred_element_type=F32)


NN = ((1,), (0,))
NT = ((1,), (1,))
TN = ((0,), (0,))


def _accumulate(ref, val, first):
    @pl.when(first)
    def _():
        ref[...] = val

    @pl.when(jnp.logical_not(first))
    def _():
        ref[...] += val


def _matmul(a, b, mode, name, out_dtypes=(F32,), extras=(), row_extras=(), epilogue=None, tm=512, tn=512, tk=None, chipwise=None,
            after=None, row_accums=0):
    if chipwise == "b":
        nc = b.shape[2]
        M, K = a.shape
        (K2, N) = (b.shape[1], N_CHIPS * nc) if mode == "nn" else (N_CHIPS * nc, b.shape[1])
    elif mode == "nn":
        (M, K), (K2, N) = a.shape, b.shape
    elif mode == "nt":
        (M, K), (N, K2) = a.shape, b.shape
    else:
        (K, M), (K2, N) = a.shape, b.shape
    assert K == K2, (a.shape, b.shape, mode)
    tm, tn = min(tm, M), min(tn, N)
    tk = K if tk is None else tk
    assert M % tm == 0 and N % tn == 0 and K % tk == 0, (M, N, K, tm, tn, tk)
    nk = K // tk
    dims = {"nn": NN, "nt": NT, "tn": TN}[mode]
    a_spec = pl.BlockSpec((tk, tm), lambda i, j, k: (k, i)) if mode == "tn" else pl.BlockSpec((tm, tk), lambda i, j, k: (i, k))
    b_spec = pl.BlockSpec((tn, tk), lambda i, j, k: (j, k)) if mode == "nt" else pl.BlockSpec((tk, tn), lambda i, j, k: (k, j))
    o_spec = pl.BlockSpec((tm, tn), lambda i, j, k: (i, j))
    out_shape = (M, N)
    if chipwise == "b" and mode == "nn":
        per = nc // tn
        assert tk == K and nc % tn == 0
        b_spec = pl.BlockSpec((None, tk, tn), lambda i, j, k: (j // per, 0, j % per))
    elif chipwise == "b":
        assert mode == "nt" and tk == nc
        b_spec = pl.BlockSpec((None, tn, tk), lambda i, j, k: (k, j, 0))
    elif chipwise == "out_rows":
        per = (M // N_CHIPS) // tm
        assert (M // N_CHIPS) % tm == 0 and not extras
        o_spec = pl.BlockSpec((None, tm, tn), lambda i, j, k: (i // per, i % per, j))
        out_shape = (N_CHIPS, M // N_CHIPS, N)
    elif chipwise == "out":
        per = (N // N_CHIPS) // tn
        assert (N // N_CHIPS) % tn == 0
        o_spec = pl.BlockSpec((None, tm, tn), lambda i, j, k: (j // per, i, j % per))
        out_shape = (N_CHIPS, M, N // N_CHIPS)
        assert not extras
    n_ex, n_rex, n_out = len(extras), len(row_extras), len(out_dtypes)
    assert not row_accums or tn == N
    tail = () if after is None else (after,)
    n_in = 2 + n_ex + n_rex + len(tail)

    def body(*refs):
        a_ref, b_ref = refs[0], refs[1]
        ex_refs = refs[2:2 + n_ex + n_rex]
        out_refs = refs[n_in:n_in + n_out]
        part = _dot(a_ref[...].astype(BF16), b_ref[...].astype(BF16), dims)

        def finish(acc):
            outs = (acc,) if epilogue is None else epilogue(acc, *[r[...] for r in ex_refs])
            for idx, (r, o) in enumerate(zip(out_refs, outs)):
                if idx >= n_out - row_accums:
                    _accumulate(r, o, pl.program_id(0) == 0)
                else:
                    r[...] = o.astype(r.dtype)

        if nk == 1:
            finish(part)
            return
        acc_ref = refs[n_in + n_out]
        k = pl.program_id(2)

        @pl.when(k == 0)
        def _():
            acc_ref[...] = part

        @pl.when(jnp.logical_and(k > 0, k < nk - 1))
        def _():
            acc_ref[...] += part

        @pl.when(k == nk - 1)
        def _():
            finish(acc_ref[...] + part)

    row_spec = pl.BlockSpec((1, tn), lambda i, j, k: (0, j))
    outs = _pcall(
        body, name=name, grid=(M // tm, N // tn, nk),
        in_specs=[a_spec, b_spec] + [o_spec] * n_ex + [row_spec] * n_rex + [pl.BlockSpec(memory_space=pl.ANY)] * len(tail),
        out_specs=[o_spec] * (n_out - row_accums) + [row_spec] * row_accums,
        out_shape=[jax.ShapeDtypeStruct(out_shape, dt) for dt in out_dtypes[:n_out - row_accums]]
        + [jax.ShapeDtypeStruct((1, N), F32)] * row_accums,
        scratch_shapes=[pltpu.VMEM((tm, tn), F32)] if nk > 1 else [],
        compiler_params=_params(("arbitrary",) * 3 if row_accums else ("parallel", "parallel", "arbitrary")),
    )(a, b, *extras, *row_extras, *tail)
    return outs[0] if n_out == 1 else outs


def _rms_fwd(x, gains, name, ts=256):
    S, D = x.shape
    ts = min(ts, S)
    n = len(gains)

    def body(*refs):
        x_ref, g_refs, o_refs = refs[0], refs[1:1 + n], refs[1 + n:]
        xv = x_ref[...]
        xh = xv * _rinv(xv)
        for g_ref, o_ref in zip(g_refs, o_refs):
            o_ref[...] = (xh * g_ref[...]).astype(BF16)

    row = pl.BlockSpec((ts, D), lambda i: (i, 0))
    vec = pl.BlockSpec((1, D), lambda i: (0, 0))
    return _pcall(body, name=name, grid=(S // ts,), in_specs=[row] + [vec] * n, out_specs=[row] * n,
                  out_shape=[jax.ShapeDtypeStruct((S, D), BF16)] * n, compiler_params=_params(("parallel",)))(x, *gains)


def _rms_bwd(x, dres, gains, dns, name, ts=256):
    S, D = x.shape
    ts = min(ts, S)
    n = len(gains)

    def body(*refs):
        x_ref, dres_ref = refs[0], refs[1]
        g_refs, dn_refs = refs[2:2 + n], refs[2 + n:2 + 2 * n]
        dx_ref, dxb_ref, dg_refs = refs[2 + 2 * n], refs[3 + 2 * n], refs[4 + 2 * n:]
        xv = x_ref[...]
        r = _rinv(xv)
        xh = xv * r
        dx = dres_ref[...]
        first = pl.program_id(0) == 0
        for g_ref, dn_ref, dg_ref in zip(g_refs, dn_refs, dg_refs):
            dn = dn_ref[...].astype(F32)
            _accumulate(dg_ref, jnp.sum(dn * xh, axis=0, keepdims=True), first)
            dxh = dn * g_ref[...]
            dx = dx + r * (dxh - xh * jnp.mean(dxh * xh, axis=-1, keepdims=True))
        dx_ref[...] = dx
        dxb_ref[...] = dx.astype(BF16)

    row = pl.BlockSpec((ts, D), lambda i: (i, 0))
    vec = pl.BlockSpec((1, D), lambda i: (0, 0))
    outs = _pcall(body, name=name, grid=(S // ts,), in_specs=[row, row] + [vec] * n + [row] * n,
                  out_specs=[row, row] + [vec] * n,
                  out_shape=[jax.ShapeDtypeStruct((S, D), F32), jax.ShapeDtypeStruct((S, D), BF16)]
                  + [jax.ShapeDtypeStruct((1, D), F32)] * n,
                  compiler_params=_params(("arbitrary",)))(x, dres, *gains, *dns)
    return outs[0], outs[1], outs[2:]


def _loss_head(h, tgt, name, ts=256):
    S, D = h.shape
    ts = min(ts, S)

    def body(h_ref, t_ref, dh_ref, dhb_ref, loss_ref):
        err = h_ref[...] - t_ref[...]
        dh = err * (1.0 / D)
        dh_ref[...] = dh
        dhb_ref[...] = dh.astype(BF16)
        part = 0.5 * jnp.sum(jnp.mean(err * err, axis=-1, keepdims=True), axis=0, keepdims=True)
        _accumulate(loss_ref, part, pl.program_id(0) == 0)

    row = pl.BlockSpec((ts, D), lambda i: (i, 0))
    return _pcall(body, name=name, grid=(S // ts,), in_specs=[row, row],
                  out_specs=[row, row, pl.BlockSpec((1, 1), lambda i: (0, 0))],
                  out_shape=[jax.ShapeDtypeStruct((S, D), F32), jax.ShapeDtypeStruct((S, D), BF16),
                             jax.ShapeDtypeStruct((1, 1), F32)],
                  compiler_params=_params(("arbitrary",)))(h, tgt)


def _gate_fwd(zt, bf, name):
    H, S = zt.shape
    nb = S // 128

    def body(z_ref, b_ref, c_ref):
        z = z_ref[...] + b_ref[...]
        lf = jnp.minimum(z, 0.0) - jnp.log(1.0 + jnp.exp(-jnp.abs(z)))
        upper = (lax.broadcasted_iota(jnp.int32, (128, 128), 0) <= lax.broadcasted_iota(jnp.int32, (128, 128), 1)).astype(F32)
        carry = jnp.zeros((H, 1), F32)
        for blk in range(nb):
            cs = _dot(lf[:, blk * 128:(blk + 1) * 128], upper, NN, precision=lax.Precision.HIGHEST) + carry
            c_ref[:, blk * 128:(blk + 1) * 128] = cs
            carry = cs[:, 127:128]

    return _pcall(body, name=name, in_specs=[VMEM_SPEC, VMEM_SPEC], out_specs=VMEM_SPEC,
                  out_shape=jax.ShapeDtypeStruct((H, S), F32))(zt, bf)


def _gate_bwd(dct, zt, bf, name):
    H, S = zt.shape
    nb = S // 128

    def body(dc_ref, z_ref, b_ref, dz_ref, db_ref):
        z = z_ref[...] + b_ref[...]
        e = jnp.exp(-jnp.abs(z))
        sig_neg = jnp.where(z >= 0, e, 1.0) / (1.0 + e)
        lower = (lax.broadcasted_iota(jnp.int32, (128, 128), 0) >= lax.broadcasted_iota(jnp.int32, (128, 128), 1)).astype(F32)
        dc = dc_ref[...]
        carry = jnp.zeros((H, 1), F32)
        db = jnp.zeros((H, 1), F32)
        for blk in reversed(range(nb)):
            sl = slice(blk * 128, (blk + 1) * 128)
            dlf = _dot(dc[:, sl], lower, NN, precision=lax.Precision.HIGHEST) + carry
            carry = dlf[:, 0:1]
            dz = dlf * sig_neg[:, sl]
            dz_ref[:, sl] = dz
            db = db + jnp.sum(dz, axis=1, keepdims=True)
        db_ref[...] = db

    return _pcall(body, name=name, in_specs=[VMEM_SPEC] * 3, out_specs=[VMEM_SPEC] * 2,
                  out_shape=[jax.ShapeDtypeStruct((H, S), F32), jax.ShapeDtypeStruct((H, 1), F32)])(dct, zt, bf)


def _lane_is_a():
    return lax.broadcasted_iota(jnp.int32, (1, LANES), 1) < HEAD_DIM


def _per_head_mean(x, is_a):
    sa = jnp.sum(jnp.where(is_a, x, 0.0), axis=-1, keepdims=True)
    sb = jnp.sum(jnp.where(is_a, 0.0, x), axis=-1, keepdims=True)
    return jnp.where(is_a, sa, sb) / HEAD_DIM


def _pair_norm(raw, gain, is_a):
    return raw * lax.rsqrt(_per_head_mean(raw * raw, is_a) + NORM_EPS) * gain


def _pair_norm_bwd(raw, gain, dnormed, is_a):
    r = lax.rsqrt(_per_head_mean(raw * raw, is_a) + NORM_EPS)
    xh = raw * r
    dgain = jnp.sum(dnormed * xh, axis=0, keepdims=True)
    dxh = dnormed * gain
    return r * (dxh - xh * _per_head_mean(dxh * xh, is_a)), dgain


def _fold_heads(x):
    i = lax.broadcasted_iota(jnp.int32, (LANES, LANES), 0)
    j = lax.broadcasted_iota(jnp.int32, (LANES, LANES), 1)
    fold = ((i == j) | (i == j + HEAD_DIM) | (i + HEAD_DIM == j)).astype(F32)
    return _dot(x, fold, NN, precision=lax.Precision.HIGHEST)


def _fold_row(ref):
    ref[...] = _fold_heads(jnp.broadcast_to(ref[...], (8, LANES)))[0:1, :]


def _as_col(row):
    return jnp.broadcast_to(row, (LANES, row.shape[1])).T[:, 0:1]


def _as_row(col):
    return jnp.broadcast_to(col, (col.shape[0], LANES)).T[0:1, :]


def _tri_mask(t, keys_on_rows):
    r = lax.broadcasted_iota(jnp.int32, (t, t), 0)
    c = lax.broadcasted_iota(jnp.int32, (t, t), 1)
    return (r <= c) if keys_on_rows else (r >= c)


def _fox_fwd(proj, c_row, gq2, gk2, n_heads, name, t=256):
    S = proj.shape[0]
    H = n_heads
    P = H // 2
    t = min(t, S)
    nq = S // t

    def body(q_ref, k_ref, v_ref, cr_ref, gq_ref, gk_ref, o_ref, lse_ref, qs_s, kb_s, vb_s):
        is_a = _lane_is_a()
        qn = _pair_norm(q_ref[...], gq_ref[...], is_a) * 0.125
        qs_s[0] = jnp.where(is_a, qn, 0.0).astype(BF16)
        qs_s[1] = jnp.where(is_a, 0.0, qn).astype(BF16)
        kb_s[...] = _pair_norm(k_ref[...], gk_ref[...], is_a).astype(BF16)
        vb_s[...] = v_ref[...].astype(BF16)
        causal = _tri_mask(t, False)
        for i in range(nq):
            t0 = i * t
            rows = slice(t0, t0 + t)
            o_pair = None
            for a in range(2):
                qi = qs_s[a, rows, :]
                ci = _as_col(cr_ref[a, :, rows])
                s_d = jnp.where(causal, _dot(qi, kb_s[rows, :], NT) + ci - cr_ref[a, :, rows], NEG)
                m = jnp.max(s_d, axis=-1, keepdims=True)
                if i > 0:
                    s_l = _dot(qi, kb_s[0:t0, :], NT) + ci - cr_ref[a, :, 0:t0]
                    m = jnp.maximum(m, jnp.max(s_l, axis=-1, keepdims=True))
                p_d = jnp.exp(s_d - m)
                l = jnp.sum(p_d, axis=-1, keepdims=True)
                acc = _dot(p_d.astype(BF16), vb_s[rows, :], NN)
                if i > 0:
                    p_l = jnp.exp(s_l - m)
                    l = l + jnp.sum(p_l, axis=-1, keepdims=True)
                    acc = acc + _dot(p_l.astype(BF16), vb_s[0:t0, :], NN)
                o_a = acc / l
                lse_ref[a, :, rows] = _as_row(m + jnp.log(l))
                o_pair = o_a if a == 0 else jnp.where(is_a, o_pair, o_a)
            o_ref[rows, :] = o_pair.astype(BF16)

    def cols(off):
        return pl.BlockSpec((S, LANES), lambda p: (0, off + p))

    rowv = pl.BlockSpec((2, 1, S), lambda p: (p, 0, 0))
    gain = pl.BlockSpec((1, LANES), lambda p: (0, 0))
    return _pcall(body, name=name, grid=(P,), in_specs=[cols(0), cols(P), cols(2 * P), rowv, gain, gain],
                  out_specs=[cols(0), rowv],
                  out_shape=[jax.ShapeDtypeStruct((S, H * HEAD_DIM), BF16), jax.ShapeDtypeStruct((H, 1, S), F32)],
                  scratch_shapes=[pltpu.VMEM((2, S, LANES), BF16), pltpu.VMEM((S, LANES), BF16), pltpu.VMEM((S, LANES), BF16)],
                  compiler_params=_params(("parallel",)))(proj, proj, proj, c_row, gq2, gk2)


def _fox_bwd(proj, c_row, gq2, gk2, lse_row, do, n_heads, name, t=256):
    S = proj.shape[0]
    H = n_heads
    P = H // 2
    t = min(t, S)
    nq = S // t
    assert t % LANES == 0

    def body(q_ref, k_ref, v_ref, cr_ref, gq_ref, gk_ref, lr_ref, do_ref,
             dq_ref, dk_ref, dv_ref, dc_ref, dgq_ref, dgk_ref,
             qs_s, kb_s, kt_s, vb_s, dob_s, dq_s, dk_s, dv_s, dcs_s, cc_s):
        is_a = _lane_is_a()
        for a in range(2):
            for i in range(nq):
                cc_s[a, i * t:(i + 1) * t, :] = _as_col(cr_ref[a, :, i * t:(i + 1) * t])
        qn = _pair_norm(q_ref[...], gq_ref[...], is_a) * 0.125
        qs_s[0] = jnp.where(is_a, qn, 0.0).astype(BF16)
        qs_s[1] = jnp.where(is_a, 0.0, qn).astype(BF16)
        kn = _pair_norm(k_ref[...], gk_ref[...], is_a)
        kb_s[...] = kn.astype(BF16)
        kt_s[0] = jnp.where(is_a, kn, 0.0).T.astype(BF16)
        kt_s[1] = jnp.where(is_a, 0.0, kn).T.astype(BF16)
        vb_s[...] = v_ref[...].astype(BF16)
        dov = do_ref[...]
        dob_s[0] = jnp.where(is_a, dov, 0.0).astype(BF16)
        dob_s[1] = jnp.where(is_a, 0.0, dov).astype(BF16)
        dk_s[...] = jnp.zeros((S, LANES), F32)
        dv_s[...] = jnp.zeros((S, LANES), F32)
        dcs_s[...] = jnp.zeros((2, S, LANES), F32)
        causal = _tri_mask(t, True)
        for i in range(nq):
            t0 = i * t
            rows = slice(t0, t0 + t)
            dq_t = jnp.zeros((LANES, t), F32)
            for a in range(2):
                qi = qs_s[a, rows, :]
                doi = dob_s[a, rows, :]
                cri = cr_ref[a, :, rows]
                lri = lr_ref[a, :, rows]

                def probs(keys, masked, a=a, qi=qi, doi=doi, cri=cri, lri=lri):
                    p_t = jnp.exp(_dot(kb_s[keys, :], qi, NT) + cri - cc_s[a, keys, :] - lri)
                    if masked:
                        p_t = jnp.where(causal, p_t, 0.0)
                    return p_t, _dot(vb_s[keys, :], doi, NT)

                parts = [(rows,) + probs(rows, True)]
                if i > 0:
                    parts.append((slice(0, t0),) + probs(slice(0, t0), False))
                delta = sum(jnp.sum(p_t * dp_t, axis=0, keepdims=True) for _, p_t, dp_t in parts)
                for keys, p_t, dp_t in parts:
                    ds_t = p_t * (dp_t - delta)
                    dsb = ds_t.astype(BF16)
                    dv_s[keys, :] += _dot(p_t.astype(BF16), doi, NN)
                    dk_s[keys, :] += _dot(dsb, qi, NN)
                    dq_t = dq_t + _dot(kt_s[a, :, keys], dsb, NN)
                    dcs_s[a, keys, :] += sum(ds_t[:, b * LANES:(b + 1) * LANES] for b in range(t // LANES))
            dq_s[rows, :] = dq_t.T
        first = pl.program_id(0) == 0
        last = pl.program_id(0) == P - 1
        dq_raw, dgq = _pair_norm_bwd(q_ref[...], gq_ref[...], dq_s[...] * 0.125, is_a)
        dq_ref[...] = dq_raw.astype(BF16)
        _accumulate(dgq_ref, dgq, first)
        dk_raw, dgk = _pair_norm_bwd(k_ref[...], gk_ref[...], dk_s[...], is_a)
        dk_ref[...] = dk_raw.astype(BF16)
        _accumulate(dgk_ref, dgk, first)
        dv_ref[...] = dv_s[...].astype(BF16)
        for a in range(2):
            for i in range(nq):
                rows = slice(i * t, (i + 1) * t)
                dc_ref[a, :, rows] = _as_row(-jnp.sum(dcs_s[a, rows, :], axis=1, keepdims=True))

        @pl.when(last)
        def _():
            _fold_row(dgq_ref)
            _fold_row(dgk_ref)

    def cols(off):
        return pl.BlockSpec((S, LANES), lambda p: (0, off + p))

    rowv = pl.BlockSpec((2, 1, S), lambda p: (p, 0, 0))
    gain = pl.BlockSpec((1, LANES), lambda p: (0, 0))
    wide = jax.ShapeDtypeStruct((S, H * HEAD_DIM), BF16)
    gs = jax.ShapeDtypeStruct((1, LANES), F32)
    return _pcall(body, name=name, grid=(P,),
                  in_specs=[cols(0), cols(P), cols(2 * P), rowv, gain, gain, rowv, cols(0)],
                  out_specs=[cols(0), cols(0), cols(0), rowv, gain, gain],
                  out_shape=[wide, wide, wide, jax.ShapeDtypeStruct((H, 1, S), F32), gs, gs],
                  scratch_shapes=[pltpu.VMEM((2, S, LANES), BF16), pltpu.VMEM((S, LANES), BF16), pltpu.VMEM((2, LANES, S), BF16),
                                  pltpu.VMEM((S, LANES), BF16), pltpu.VMEM((2, S, LANES), BF16)]
                  + [pltpu.VMEM((S, LANES), F32)] * 3 + [pltpu.VMEM((2, S, LANES), F32), pltpu.VMEM((2, S, 1), F32)],
                  compiler_params=_params(("arbitrary",)))(proj, proj, proj, c_row, gq2, gk2, lse_row, do)


def _bucket_onehot():
    W = WINDOW
    dist = np.arange(W)[:, None] + W - np.arange(2 * W)[None, :]
    n = np.maximum(dist, 0)
    max_exact = N_BUCKETS // 2
    large = max_exact + (np.log(np.maximum(n, 1) / max_exact) / np.log(REL_MAX_DIST / max_exact)
                         * (N_BUCKETS - max_exact)).astype(np.int32)
    large = np.minimum(large, N_BUCKETS - 1)
    bucket = np.where(n < max_exact, n, large).astype(np.int32)
    valid = (dist >= 0) & (dist < W)
    onehot = (bucket[None] == np.arange(N_BUCKETS)[:, None, None]) & valid[None]
    return onehot.reshape(N_BUCKETS, W * 2 * W).astype(np.float32)


def _bias_expand(rel_bias_t, onehot, name, tn=4096):
    HQ, NB = rel_bias_t.shape
    L = onehot.shape[1]

    def body(r_ref, oh_ref, out_ref):
        out_ref[...] = _dot(r_ref[...], oh_ref[...].astype(F32), NN, precision=lax.Precision.HIGHEST)

    return _pcall(body, name=name, grid=(L // tn,),
                  in_specs=[pl.BlockSpec((HQ, NB), lambda i: (0, 0)), pl.BlockSpec((NB, tn), lambda i: (0, i))],
                  out_specs=pl.BlockSpec((HQ, tn), lambda i: (0, i)),
                  out_shape=jax.ShapeDtypeStruct((HQ, L), F32), compiler_params=_params(("parallel",)))(rel_bias_t, onehot)


def _bias_reduce(dbias, onehot, name, tk=4096):
    HQ, L = dbias.shape
    NB = onehot.shape[0]

    def body(d_ref, oh_ref, out_ref):
        part = _dot(d_ref[...], oh_ref[...].astype(F32), NT, precision=lax.Precision.HIGHEST)
        _accumulate(out_ref, part, pl.program_id(0) == 0)

    return _pcall(body, name=name, grid=(L // tk,),
                  in_specs=[pl.BlockSpec((HQ, tk), lambda i: (0, i)), pl.BlockSpec((NB, tk), lambda i: (0, i))],
                  out_specs=pl.BlockSpec((HQ, NB), lambda i: (0, 0)),
                  out_shape=jax.ShapeDtypeStruct((HQ, NB), F32), compiler_params=_params(("arbitrary",)))(dbias, onehot)


def _stacked_query_index(n_rows_or_cols_axis, shape):
    idx = lax.broadcasted_iota(jnp.int32, shape, n_rows_or_cols_axis)
    return jnp.where(idx >= WINDOW, idx - WINDOW, idx)


def _swa_fwd(qproj, kk, vv, bias_ab, sink_col, gq2, gk2, name):
    S, HQD = qproj.shape
    KVH = kk.shape[0]
    PP = HQD // LANES
    NP = PP // KVH
    W = WINDOW
    nb = S // W

    def body(q_ref, k_ref, v_ref, bias_ref, sink_ref, gq_ref, gk_ref, o_ref, lse_ref, qs_s, kb_s, vb_s):
        is_a = _lane_is_a()
        qn = _pair_norm(q_ref[...], gq_ref[...], is_a) * 0.125
        qs_s[0] = jnp.where(is_a, qn, 0.0).astype(BF16)
        qs_s[1] = jnp.where(is_a, 0.0, qn).astype(BF16)
        kb_s[...] = _pair_norm(k_ref[...], gk_ref[...], is_a).astype(BF16)
        vb_s[...] = v_ref[...].astype(BF16)
        sink = sink_ref[...]
        qi1 = _stacked_query_index(0, (2 * W, W))
        first_valid = lax.broadcasted_iota(jnp.int32, (2 * W, W), 1) <= qi1
        qi2 = _stacked_query_index(0, (2 * W, 2 * W))
        key2 = lax.broadcasted_iota(jnp.int32, (2 * W, 2 * W), 1)
        band_valid = (key2 > qi2) & (key2 <= qi2 + W)
        for n in range(nb):
            rows = slice(n * W, (n + 1) * W)
            keys = slice(0, W) if n == 0 else slice((n - 1) * W, (n + 1) * W)
            lhs = jnp.concatenate([qs_s[0, rows, :], qs_s[1, rows, :]], axis=0)
            s = _dot(lhs, kb_s[keys, :], NT) + (bias_ref[:, W:2 * W] if n == 0 else bias_ref[...])
            s = jnp.where(first_valid if n == 0 else band_valid, s, NEG)
            m = jnp.maximum(jnp.max(s, axis=-1, keepdims=True), sink)
            e = jnp.exp(s - m)
            l = jnp.sum(e, axis=-1, keepdims=True) + jnp.exp(sink - m)
            o_ab = _dot(e.astype(BF16), vb_s[keys, :], NN) / l
            o_ref[rows, :] = jnp.where(is_a, o_ab[0:W, :], o_ab[W:2 * W, :]).astype(BF16)
            lse_ref[n] = _as_row(m + jnp.log(l))

    qcols = pl.BlockSpec((S, LANES), lambda a, g: (0, a * NP + g))
    kvs = pl.BlockSpec((None, S, LANES), lambda a, g: (a, 0, 0))
    gain = pl.BlockSpec((1, LANES), lambda a, g: (0, 0))
    return _pcall(body, name=name, grid=(KVH, NP),
                  in_specs=[qcols, kvs, kvs, pl.BlockSpec((None, 2 * W, 2 * W), lambda a, g: (a * NP + g, 0, 0)),
                            pl.BlockSpec((None, 2 * W, 1), lambda a, g: (a * NP + g, 0, 0)), gain, gain],
                  out_specs=[qcols, pl.BlockSpec((None, nb, 1, 2 * W), lambda a, g: (a * NP + g, 0, 0, 0))],
                  out_shape=[jax.ShapeDtypeStruct((S, HQD), BF16), jax.ShapeDtypeStruct((PP, nb, 1, 2 * W), F32)],
                  scratch_shapes=[pltpu.VMEM((2, S, LANES), BF16), pltpu.VMEM((S, LANES), BF16), pltpu.VMEM((S, LANES), BF16)],
                  compiler_params=_params(("parallel", "parallel")))(qproj, kk, vv, bias_ab, sink_col, gq2, gk2)


def _swa_bwd(qproj, kk, vv, bias_t_ab, sink_row, gq2, gk2, lse_row, do, name):
    S, HQD = qproj.shape
    KVH = kk.shape[0]
    PP = HQD // LANES
    NP = PP // KVH
    W = WINDOW
    nb = S // W

    def body(q_ref, k_ref, v_ref, bias_ref, sink_ref, gq_ref, gk_ref, lr_ref, do_ref,
             dq_ref, dk_ref, dv_ref, db_ref, dsink_ref, dgq_ref, dgk_ref,
             qs_s, kb_s, kt_s, vb_s, dob_s, dq_s, dk_s, dv_s):
        a, g = pl.program_id(0), pl.program_id(1)
        is_a = _lane_is_a()
        qn = _pair_norm(q_ref[...], gq_ref[...], is_a) * 0.125
        qs_s[0] = jnp.where(is_a, qn, 0.0).astype(BF16)
        qs_s[1] = jnp.where(is_a, 0.0, qn).astype(BF16)
        kn = _pair_norm(k_ref[...], gk_ref[...], is_a)
        kb_s[...] = kn.astype(BF16)
        kt_s[...] = kn.T.astype(BF16)
        vb_s[...] = v_ref[...].astype(BF16)
        dov = do_ref[...]
        dob_s[0] = jnp.where(is_a, dov, 0.0).astype(BF16)
        dob_s[1] = jnp.where(is_a, 0.0, dov).astype(BF16)
        sink = sink_ref[...]

        @pl.when(g == 0)
        def _():
            dk_s[...] = jnp.zeros((S, LANES), F32)
            dv_s[...] = jnp.zeros((S, LANES), F32)

        qi1 = _stacked_query_index(1, (W, 2 * W))
        first_valid = lax.broadcasted_iota(jnp.int32, (W, 2 * W), 0) <= qi1
        qi2 = _stacked_query_index(1, (2 * W, 2 * W))
        key2 = lax.broadcasted_iota(jnp.int32, (2 * W, 2 * W), 0)
        band_valid = (key2 > qi2) & (key2 <= qi2 + W)
        head_rows = lax.broadcasted_iota(jnp.int32, (LANES, W), 0) < HEAD_DIM
        db = jnp.zeros((2 * W, 2 * W), F32)
        dsk = jnp.zeros((1, 2 * W), F32)
        pend_k = pend_v = None
        for n in range(nb):
            rows = slice(n * W, (n + 1) * W)
            keys = slice(0, W) if n == 0 else slice((n - 1) * W, (n + 1) * W)
            lhs_q = jnp.concatenate([qs_s[0, rows, :], qs_s[1, rows, :]], axis=0)
            lhs_do = jnp.concatenate([dob_s[0, rows, :], dob_s[1, rows, :]], axis=0)
            lse = lr_ref[n]
            s_t = _dot(kb_s[keys, :], lhs_q, NT) + (bias_ref[W:2 * W, :] if n == 0 else bias_ref[...])
            p_t = jnp.where(first_valid if n == 0 else band_valid, jnp.exp(s_t - lse), 0.0)
            dp_t = _dot(vb_s[keys, :], lhs_do, NT)
            delta = jnp.sum(p_t * dp_t, axis=0, keepdims=True)
            ds_t = p_t * (dp_t - delta)
            dsb = ds_t.astype(BF16)
            dsk = dsk - jnp.exp(sink - lse) * delta
            dv_band = _dot(p_t.astype(BF16), lhs_do, NN)
            dk_band = _dot(dsb, lhs_q, NN)
            dq_t = _dot(kt_s[:, keys], dsb, NN)
            dq_s[rows, :] = jnp.where(head_rows, dq_t[:, 0:W], dq_t[:, W:2 * W]).T
            if n == 0:
                db = jnp.concatenate([jnp.zeros((W, 2 * W), F32), ds_t], axis=0)
                pend_k, pend_v = dk_band, dv_band
            else:
                db = db + ds_t
                prev = slice((n - 1) * W, n * W)
                dk_s[prev, :] += pend_k + dk_band[0:W, :]
                dv_s[prev, :] += pend_v + dv_band[0:W, :]
                pend_k, pend_v = dk_band[W:2 * W, :], dv_band[W:2 * W, :]
        tail = slice((nb - 1) * W, nb * W)
        dk_s[tail, :] += pend_k
        dv_s[tail, :] += pend_v
        db_ref[...] = db
        dsink_ref[0] = jnp.broadcast_to(jnp.sum(dsk[:, 0:W], axis=1, keepdims=True), (1, LANES))
        dsink_ref[1] = jnp.broadcast_to(jnp.sum(dsk[:, W:2 * W], axis=1, keepdims=True), (1, LANES))
        dq_raw, dgq = _pair_norm_bwd(q_ref[...], gq_ref[...], dq_s[...] * 0.125, is_a)
        dq_ref[...] = dq_raw.astype(BF16)
        _accumulate(dgq_ref, dgq, jnp.logical_and(a == 0, g == 0))

        @pl.when(jnp.logical_and(a == KVH - 1, g == NP - 1))
        def _():
            _fold_row(dgq_ref)

        @pl.when(g == NP - 1)
        def _():
            dk_raw, dgk = _pair_norm_bwd(k_ref[...], gk_ref[...], _fold_heads(dk_s[...]), is_a)
            dk_ref[...] = dk_raw
            _accumulate(dgk_ref, dgk, a == 0)
            dv_ref[...] = _fold_heads(dv_s[...])

    qcols = pl.BlockSpec((S, LANES), lambda a, g: (0, a * NP + g))
    kvs = pl.BlockSpec((None, S, LANES), lambda a, g: (a, 0, 0))
    sq = pl.BlockSpec((None, 2 * W, 2 * W), lambda a, g: (a * NP + g, 0, 0))
    gain = pl.BlockSpec((1, LANES), lambda a, g: (0, 0))
    ks = jax.ShapeDtypeStruct((KVH, S, LANES), F32)
    gs = jax.ShapeDtypeStruct((1, LANES), F32)
    return _pcall(body, name=name, grid=(KVH, NP),
                  in_specs=[qcols, kvs, kvs, sq, pl.BlockSpec((None, 1, 2 * W), lambda a, g: (a * NP + g, 0, 0)), gain, gain,
                            pl.BlockSpec((None, nb, 1, 2 * W), lambda a, g: (a * NP + g, 0, 0, 0)), qcols],
                  out_specs=[qcols, kvs, kvs, sq, pl.BlockSpec((2, 1, LANES), lambda a, g: (a * NP + g, 0, 0)), gain, gain],
                  out_shape=[jax.ShapeDtypeStruct((S, HQD), BF16), ks, ks, jax.ShapeDtypeStruct((PP, 2 * W, 2 * W), F32),
                             jax.ShapeDtypeStruct((2 * PP, 1, LANES), F32), gs, gs],
                  scratch_shapes=[pltpu.VMEM((2, S, LANES), BF16), pltpu.VMEM((S, LANES), BF16), pltpu.VMEM((LANES, S), BF16),
                                  pltpu.VMEM((S, LANES), BF16), pltpu.VMEM((2, S, LANES), BF16)] + [pltpu.VMEM((S, LANES), F32)] * 3,
                  compiler_params=_params(("arbitrary", "arbitrary")))(qproj, kk, vv, bias_t_ab, sink_row, gq2, gk2, lse_row, do)


def _adamw_update(w, g, m, v):
    m2 = ADAM_B1 * m + (1.0 - ADAM_B1) * g
    v2 = ADAM_B2 * v + (1.0 - ADAM_B2) * jnp.square(g)
    m_hat = m2 / (1.0 - ADAM_B1 ** ADAM_STEP)
    v_hat = v2 / (1.0 - ADAM_B2 ** ADAM_STEP)
    return -ADAM_LR * (m_hat / (jnp.sqrt(v_hat) + ADAM_EPS) + ADAM_WD * w), m2, v2


def _adamw(w, g, m, v, name, tr=256, tc=256):
    R, C = w.shape
    tr = min(tr, R)
    if R % tr == 0:
        grid, blk = (R // tr,), pl.BlockSpec((tr, C), lambda i: (i, 0))
    else:
        assert C % tc == 0
        grid, blk = (C // tc,), pl.BlockSpec((R, tc), lambda i: (0, i))

    def body(w_ref, g_ref, m_ref, v_ref, d_ref, m2_ref, v2_ref):
        d_ref[...], m2_ref[...], v2_ref[...] = _adamw_update(w_ref[...], g_ref[...], m_ref[...], v_ref[...])

    return _pcall(body, name=name, grid=grid, in_specs=[blk] * 4, out_specs=[blk] * 3,
                  out_shape=[jax.ShapeDtypeStruct((R, C), F32)] * 3, compiler_params=_params(("parallel",)))(w, g, m, v)


def _adamw_two_layers(w, g0, g1, m, v, name, tr=256):
    R, C = g0.shape
    assert R % tr == 0 and w.shape == (2 * R, C)
    nr = R // tr

    def body(w_ref, g0_ref, g1_ref, m_ref, v_ref, g_ref, d_ref, m2_ref, v2_ref):
        g = jnp.where(pl.program_id(0) == 0, g0_ref[...], g1_ref[...])
        g_ref[...] = g
        d_ref[...], m2_ref[...], v2_ref[...] = _adamw_update(w_ref[...], g, m_ref[...], v_ref[...])

    both = pl.BlockSpec((tr, C), lambda l, i: (l * nr + i, 0))
    first = pl.BlockSpec((tr, C), lambda l, i: (i * (1 - l) + (nr - 1) * l, 0))
    second = pl.BlockSpec((tr, C), lambda l, i: (i * l, 0))
    return _pcall(body, name=name, grid=(2, nr), in_specs=[both, first, second, both, both], out_specs=[both] * 4,
                  out_shape=[jax.ShapeDtypeStruct((2 * R, C), F32)] * 4,
                  compiler_params=_params(("arbitrary", "arbitrary")))(w, g0, g1, m, v)


def _sum_core_pair(arr, got, place, name, tr=512):
    P, hr, C = got.shape
    tr = tr if hr % tr == 0 else hr
    nt = hr // tr

    def body(place_ref, a_ref, g_ref, o_ref):
        o_ref[...] = (a_ref[...].astype(F32) + g_ref[...].astype(F32)).astype(o_ref.dtype)

    spec = pltpu.PrefetchScalarGridSpec(
        num_scalar_prefetch=1, grid=(P, nt),
        in_specs=[pl.BlockSpec((None, tr, C), lambda j, i, pr: (j, pr[1] * nt + i, 0)),
                  pl.BlockSpec((None, tr, C), lambda j, i, pr: (j, i, 0))],
        out_specs=pl.BlockSpec((None, tr, C), lambda j, i, pr: (j, i, 0)))
    return _pcall(body, name=name, grid_spec=spec, out_shape=jax.ShapeDtypeStruct(got.shape, BF16),
                  compiler_params=_params(("parallel", "parallel")))(place, arr, got)


def _sum_chips(pair, landed, place, name, tr=256):
    _, R, C = landed.shape
    tr = tr if R % tr == 0 else R

    def body(place_ref, p_ref, l_ref, o_ref):
        acc = p_ref[...].astype(F32)
        for k in range(3):
            acc = acc + l_ref[k].astype(F32)
        o_ref[...] = acc

    spec = pltpu.PrefetchScalarGridSpec(
        num_scalar_prefetch=1, grid=(R // tr,),
        in_specs=[pl.BlockSpec((None, tr, C), lambda i, pr: (pr[0], i, 0)), pl.BlockSpec((3, tr, C), lambda i, pr: (0, i, 0))],
        out_specs=pl.BlockSpec((None, tr, C), lambda i, pr: (pr[1], i, 0)))
    return _pcall(body, name=name, grid_spec=spec, out_shape=jax.ShapeDtypeStruct((2, R, C), F32),
                  compiler_params=_params(("parallel",)))(place, pair, landed)


def _place():
    x, y, c = lax.axis_index("x"), lax.axis_index("y"), lax.axis_index("c")
    others = [(1 - x, y), (x, 1 - y), (1 - x, 1 - y)]
    return x, y, c, others


def _half_rows(ref, hh, lead=()):
    hr = ref.shape[-2] // 2
    return ref.at[(*lead, pl.ds(pl.multiple_of(hh * hr, 16), hr), slice(None))]


def _sem_arrays(*counts):
    return [pltpu.SemaphoreType.DMA((k,)) for k in counts]


SEM_SPEC = pl.BlockSpec(memory_space=pltpu.SEMAPHORE)
ANY_SPEC = pl.BlockSpec(memory_space=pl.ANY)
DATAFLOW = pltpu.SideEffectType.DATAFLOW_SIDE_EFFECTING


def _in_hbm(a):
    return pltpu.with_memory_space_constraint(a, pltpu.HBM)


def _gather_copies(srcs, lands, send_sems, recv_sems):
    x, y, c, others = _place()
    me = 2 * x + y

    def copy(w, k, dst_chip, to):
        return pltpu.make_async_remote_copy(src_ref=_half_rows(srcs[w], c), dst_ref=_half_rows(lands[w], c, (dst_chip,)),
                                            send_sem=send_sems.at[3 * w + k], recv_sem=recv_sems.at[3 * w + k],
                                            device_id=to, device_id_type=MESH)

    pairs = [(w, k, cx, cy) for w in range(len(srcs)) for k, (cx, cy) in enumerate(others)]
    return ([copy(w, k, me, (cx, cy, c)) for w, k, cx, cy in pairs],
            [copy(w, k, 2 * cx + cy, (cx, cy, c)) for w, k, cx, cy in pairs])


def _gather_start(shards, after, name):
    n = len(shards)

    def body(*refs):
        srcs, lands, send_sems, recv_sems, token = refs[:n], refs[n:2 * n], refs[2 * n + 1], refs[2 * n + 2], refs[-1]
        for cp in _gather_copies(srcs, lands, send_sems, recv_sems)[0]:
            cp.start()
        token[...] = jnp.zeros_like(token)

    lands = [lax.empty((N_CHIPS,) + s.shape, s.dtype) for s in shards]
    outs = _pcall(
        body, name=name, in_specs=[HBM_SPEC] * (2 * n) + [ANY_SPEC],
        out_specs=[SEM_SPEC, SEM_SPEC] + [HBM_SPEC] * (2 * n) + [VMEM_SPEC],
        out_shape=[pltpu.SemaphoreType.DMA((3 * n,)), pltpu.SemaphoreType.DMA((3 * n,))]
        + [pltpu.HBM(a.shape, a.dtype) for a in list(shards) + lands] + [jax.ShapeDtypeStruct((8, LANES), F32)],
        input_output_aliases={i: 2 + i for i in range(2 * n)},
        compiler_params=pltpu.CompilerParams(has_side_effects=DATAFLOW),
    )(*[_in_hbm(a) for a in list(shards) + lands], after)
    return outs[0], outs[1], outs[2:2 + n], outs[2 + n:2 + 2 * n], outs[-1]


def _gather_wait(started, after, name):
    send_sems, recv_sems, srcs, lands, _ = started
    n = len(srcs)

    def body(*refs):
        src_refs, land_refs, send_ref, recv_ref = refs[:n], refs[n:2 * n], refs[2 * n], refs[2 * n + 1]
        outgoing, incoming = _gather_copies(src_refs, land_refs, send_ref, recv_ref)
        for out_cp, in_cp in zip(outgoing, incoming):
            out_cp.wait_send()
            in_cp.wait_recv()

    outs = _pcall(
        body, name=name, in_specs=[HBM_SPEC] * (2 * n) + [SEM_SPEC, SEM_SPEC, ANY_SPEC], out_specs=[HBM_SPEC] * (2 * n),
        out_shape=[pltpu.HBM(a.shape, a.dtype) for a in list(srcs) + list(lands)],
        input_output_aliases={i: i for i in range(2 * n)},
        compiler_params=pltpu.CompilerParams(has_side_effects=DATAFLOW),
    )(*srcs, *lands, send_sems, recv_sems, after)
    return outs[:n], outs[n:]


def _gather_pass_on(shards, lands, name):
    n = len(shards)
    per = 4

    def body(*refs):
        srcs, bufs = refs[:n], refs[2 * n:3 * n]
        send_sems, recv_sems = refs[3 * n:]
        x, y, c, others = _place()
        me = 2 * x + y
        sibling = (x, y, 1 - c)

        def copy(w, k, src, dst):
            return pltpu.make_async_remote_copy(src_ref=src, dst_ref=dst, send_sem=send_sems.at[per * w + k],
                                                recv_sem=recv_sems.at[per * w + k], device_id=sibling, device_id_type=MESH)

        sends, recvs = [], []
        for w in range(n):
            for k, (cx, cy) in enumerate(others):
                mine, theirs = _half_rows(bufs[w], c, (2 * cx + cy,)), _half_rows(bufs[w], 1 - c, (2 * cx + cy,))
                sends.append(copy(w, k, mine, mine))
                recvs.append(copy(w, k, theirs, theirs))
            sends.append(copy(w, 3, srcs[w], bufs[w].at[me]))
            recvs.append(sends[-1])
        for cp in sends:
            cp.start()
        for snd, rcv in zip(sends, recvs):
            snd.wait_send()
            rcv.wait_recv()

    return _pcall(body, name=name, in_specs=[HBM_SPEC] * (2 * n), out_specs=[HBM_SPEC] * n,
                  out_shape=[jax.ShapeDtypeStruct(l.shape, l.dtype) for l in lands],
                  input_output_aliases={n + w: w for w in range(n)},
                  scratch_shapes=_sem_arrays(per * n, per * n))(*shards, *lands)


def _scatter_copies(srcs, lands, send_sems, recv_sems):
    x, y, c, others = _place()
    return [pltpu.make_async_remote_copy(src_ref=srcs[w].at[2 * cx + cy], dst_ref=lands[w].at[k],
                                         send_sem=send_sems.at[3 * w + k], recv_sem=recv_sems.at[3 * w + k],
                                         device_id=(cx, cy, c), device_id_type=MESH)
            for w in range(len(srcs)) for k, (cx, cy) in enumerate(others)]


def _scatter_start(parts, name):
    n = len(parts)

    def body(*refs):
        srcs, lands, send_sems, recv_sems, token = refs[:n], refs[n:2 * n], refs[2 * n], refs[2 * n + 1], refs[-1]
        for cp in _scatter_copies(srcs, lands, send_sems, recv_sems):
            cp.start()
        token[...] = jnp.zeros_like(token)

    lands = [lax.empty((3,) + p.shape[1:], p.dtype) for p in parts]
    outs = _pcall(
        body, name=name, in_specs=[HBM_SPEC] * (2 * n), out_specs=[SEM_SPEC, SEM_SPEC] + [HBM_SPEC] * (2 * n) + [VMEM_SPEC],
        out_shape=[pltpu.SemaphoreType.DMA((3 * n,)), pltpu.SemaphoreType.DMA((3 * n,))]
        + [pltpu.HBM(a.shape, a.dtype) for a in list(parts) + lands] + [jax.ShapeDtypeStruct((8, LANES), F32)],
        input_output_aliases={i: 2 + i for i in range(2 * n)},
        compiler_params=pltpu.CompilerParams(has_side_effects=DATAFLOW),
    )(*[_in_hbm(a) for a in list(parts) + lands])
    return (outs[0], outs[1], outs[2:2 + n], outs[2 + n:2 + 2 * n]), outs[-1]


def _scatter_wait(started, after, name):
    send_sems, recv_sems, srcs, lands = started
    n = len(srcs)

    def body(*refs):
        for cp in _scatter_copies(refs[:n], refs[n:2 * n], refs[2 * n], refs[2 * n + 1]):
            cp.wait_send()
            cp.wait_recv()

    outs = _pcall(
        body, name=name, in_specs=[HBM_SPEC] * (2 * n) + [SEM_SPEC, SEM_SPEC, ANY_SPEC], out_specs=[HBM_SPEC] * (2 * n),
        out_shape=[pltpu.HBM(a.shape, a.dtype) for a in list(srcs) + list(lands)],
        input_output_aliases={i: i for i in range(2 * n)},
        compiler_params=pltpu.CompilerParams(has_side_effects=DATAFLOW),
    )(*srcs, *lands, send_sems, recv_sems, after)
    return outs[:n], outs[n:]


def _split_start(plan, arrays, n_copies, after, name):
    n = len(arrays)

    def body(*refs):
        for cp in plan(refs[:n], refs[n + 1], refs[n + 2])[0]:
            cp.start()
        refs[-1][...] = jnp.zeros_like(refs[-1])

    outs = _pcall(
        body, name=name, in_specs=[HBM_SPEC] * n + [ANY_SPEC], out_specs=[SEM_SPEC, SEM_SPEC] + [HBM_SPEC] * n + [VMEM_SPEC],
        out_shape=[pltpu.SemaphoreType.DMA((n_copies,)), pltpu.SemaphoreType.DMA((n_copies,))]
        + [pltpu.HBM(a.shape, a.dtype) for a in arrays] + [jax.ShapeDtypeStruct((8, LANES), F32)],
        input_output_aliases={i: 2 + i for i in range(n)},
        compiler_params=pltpu.CompilerParams(has_side_effects=DATAFLOW),
    )(*[_in_hbm(a) for a in arrays], after)
    return (outs[0], outs[1], outs[2:2 + n]), outs[-1]


def _split_wait(plan, started, after, name):
    send_sems, recv_sems, arrays = started
    n = len(arrays)

    def body(*refs):
        outgoing, incoming = plan(refs[:n], refs[n], refs[n + 1])
        for cp in outgoing:
            cp.wait_send()
        for cp in incoming:
            cp.wait_recv()

    return _pcall(
        body, name=name, in_specs=[HBM_SPEC] * n + [SEM_SPEC, SEM_SPEC, ANY_SPEC], out_specs=[HBM_SPEC] * n,
        out_shape=[pltpu.HBM(a.shape, a.dtype) for a in arrays], input_output_aliases={i: i for i in range(n)},
        compiler_params=pltpu.CompilerParams(has_side_effects=DATAFLOW),
    )(*arrays, send_sems, recv_sems, after)


def _to_sibling(src, dst, k, send_sems, recv_sems):
    x, y, c, _ = _place()
    return pltpu.make_async_remote_copy(src_ref=src, dst_ref=dst, send_sem=send_sems.at[k], recv_sem=recv_sems.at[k],
                                        device_id=(x, y, 1 - c), device_id_type=MESH)


def _plan_pass_on(n):
    def plan(refs, send_sems, recv_sems):
        x, y, c, others = _place()
        cps = []
        for w in range(n):
            for k, (cx, cy) in enumerate(others):
                mine = _half_rows(refs[n + w], c, (2 * cx + cy,))
                cps.append(_to_sibling(mine, mine, 4 * w + k, send_sems, recv_sems))
            cps.append(_to_sibling(refs[w], refs[n + w].at[2 * x + y], 4 * w + 3, send_sems, recv_sems))
        return cps, cps
    return plan


def _plan_swap_halves(n):
    def plan(refs, send_sems, recv_sems):
        c = lax.axis_index("c")
        cps = [_to_sibling(_half_rows(refs[w], 1 - c, (slice(None),)), refs[n + w], w, send_sems, recv_sems) for w in range(n)]
        return cps, cps
    return plan


def _plan_share_halves(n):
    def plan(refs, send_sems, recv_sems):
        c = lax.axis_index("c")
        cps = [_to_sibling(refs[w].at[c], refs[w].at[c], w, send_sems, recv_sems) for w in range(n)]
        return cps, cps
    return plan


def _plan_gather_small(refs, send_sems, recv_sems):
    x, y, c, _ = _place()
    flips = [(dx, dy, dc) for dx in (0, 1) for dy in (0, 1) for dc in (0, 1)][1:]
    flip = lambda v, d: 1 - v if d else v
    cps = [pltpu.make_async_remote_copy(src_ref=refs[0], dst_ref=refs[1].at[4 * x + 2 * y + c], send_sem=send_sems.at[k],
                                        recv_sem=recv_sems.at[k], device_id=(flip(x, dx), flip(y, dy), flip(c, dc)),
                                        device_id_type=MESH)
           for k, (dx, dy, dc) in enumerate(flips)]
    return cps, cps


def _sum_gathered_small(gathered, own, place, name):
    _, M, C = gathered.shape

    def body(place_ref, g_ref, own_ref, o_ref):
        me = 2 * place_ref[0] + place_ref[1]
        acc = jnp.zeros((M, C), F32)
        for k in range(8):
            acc = acc + jnp.where(me == k, own_ref[...], g_ref[k])
        o_ref[...] = acc

    spec = pltpu.PrefetchScalarGridSpec(
        num_scalar_prefetch=1, grid=(1,),
        in_specs=[pl.BlockSpec((8, M, C), lambda i, pr: (0, 0, 0)), pl.BlockSpec((M, C), lambda i, pr: (0, 0))],
        out_specs=pl.BlockSpec((M, C), lambda i, pr: (0, 0)))
    return _pcall(body, name=name, grid_spec=spec, out_shape=jax.ShapeDtypeStruct((M, C), F32),
                  compiler_params=_params(("arbitrary",)))(place, gathered, own)


def _swap_halves_group(arrs, name):
    n = len(arrs)

    def body(*refs):
        ins, gots = refs[:n], refs[n:2 * n]
        send_sems, recv_sems = refs[2 * n:]
        x, y, c, _ = _place()
        swaps = [pltpu.make_async_remote_copy(src_ref=_half_rows(ins[w], 1 - c, (slice(None),)), dst_ref=gots[w],
                                              send_sem=send_sems.at[w], recv_sem=recv_sems.at[w],
                                              device_id=(x, y, 1 - c), device_id_type=MESH) for w in range(n)]
        for cp in swaps:
            cp.start()
        for cp in swaps:
            cp.wait()

    half_shapes = [jax.ShapeDtypeStruct((a.shape[0], a.shape[1] // 2, a.shape[2]), a.dtype) for a in arrs]
    return _pcall(body, name=name, in_specs=[HBM_SPEC] * n, out_specs=[HBM_SPEC] * n, out_shape=half_shapes,
                  scratch_shapes=_sem_arrays(n, n))(*arrs)


def _pack_rows(n_elems, width, align):
    rows = -(-n_elems // width)
    return -(-rows // align) * align


def _pack(arrays, dtype, width, align):
    flat = jnp.concatenate([a.astype(dtype).reshape(-1) for a in arrays])
    rows = _pack_rows(flat.shape[0], width, align)
    flat = jnp.pad(flat, (0, rows * width - flat.shape[0]))
    return flat.reshape(rows, width)


def _pack_small(arrays):
    return _pack(arrays, F32, width=128, align=8)


def _unpack(flat, shapes):
    out, off = [], 0
    for shp in shapes:
        n = int(np.prod(shp))
        out.append(flat[..., off:off + n].reshape(flat.shape[:-1] + tuple(shp)))
        off += n
    return out


def _doubled_heads(x2d, n_heads):
    S = x2d.shape[0]
    h = x2d.reshape(S, n_heads, HEAD_DIM).transpose(1, 0, 2)
    return jnp.concatenate([h, h], axis=-1)


def _rms_bwd_epilogue(dn, x, dres, g):
    r = _rinv(x)
    xh = x * r
    dxh = dn * g
    dx = dres + r * (dxh - xh * jnp.mean(dxh * xh, axis=-1, keepdims=True))
    return dx, dx, jnp.sum(dn * xh, axis=0, keepdims=True)


def _residual_then_norms(acc, res, *gains):
    h = res + acc
    hn = h * _rinv(h)
    return (h,) + tuple(hn * g for g in gains)


def _mlp_fwd(h, n, w_up4, w_down, next_gains, tag, between=None):
    u, a = _matmul(n, w_up4, "nn", f"up{tag}", out_dtypes=(F32, BF16), chipwise="b", tm=2048, tn=512,
                   epilogue=lambda acc: (acc, jnp.square(jnp.maximum(acc, 0.0))))
    if between is not None:
        next_gains = [next_gains[0] + between(a)] + list(next_gains[1:])
    assert w_down.shape[1] == 1024
    outs = _matmul(a, w_down, "nn", f"down{tag}", out_dtypes=(F32,) + (BF16,) * len(next_gains), extras=(h,),
                   row_extras=tuple(next_gains), epilogue=_residual_then_norms, tm=1024, tn=1024, tk=1024)
    outs = outs if next_gains else (outs,)
    return outs[0], outs[1:], (n, u, a)


def _mlp_bwd(dh_out, dh_out_b, h, g, w_up4, w_down, saved, tag):
    n, u, a = saved
    dw_down = _matmul(a, dh_out_b, "tn", f"dw_down{tag}", out_dtypes=(BF16,), tm=1024, tn=1024, chipwise="out_rows")
    du = _matmul(dh_out_b, w_down, "nt", f"du{tag}", out_dtypes=(BF16,), extras=(u,), tm=2048, tn=512,
                 epilogue=lambda acc, uu: (acc * (2.0 * jnp.maximum(uu, 0.0)),))
    dw_up = _matmul(n, du, "tn", f"dw_up{tag}", out_dtypes=(BF16,), chipwise="out", tm=1024, tn=512)
    dh, dh_b, dg = _matmul(du, w_up4, "nt", f"dn_mlp{tag}", out_dtypes=(F32, BF16, F32), extras=(h, dh_out), row_extras=(g,),
                           epilogue=_rms_bwd_epilogue, row_accums=1, tm=1024, tn=1024, tk=w_up4.shape[2], chipwise="b")
    return dh, dh_b, dg, dw_up, dw_down


def kernel(x, g_attn, g_mlp, w_in_a, b_f, gq_a, gk_a, w_out_a, g_kv, w_kv, gk_b, w_q_b, gq_b, sinks, rel_bias, w_out_b, w_up, w_down, loss_target, m_g_attn, m_g_mlp, m_w_in_a, m_b_f, m_gq_a, m_gk_a, m_w_out_a, m_g_kv, m_w_kv, m_gk_b, m_w_q_b, m_gq_b, m_sinks, m_rel_bias, m_w_out_b, m_w_up, m_w_down, v_g_attn, v_g_mlp, v_w_in_a, v_b_f, v_gq_a, v_gk_a, v_w_out_a, v_g_kv, v_w_kv, v_gk_b, v_w_q_b, v_gq_b, v_sinks, v_rel_bias, v_w_out_b, v_w_up, v_w_down):
    given = dict(locals())
    S, D = x.shape[1], x.shape[2]
    H = D // HEAD_DIM
    KVH = w_kv.shape[1] // (2 * HEAD_DIM)
    kvw = KVH * HEAD_DIM
    hw = H * HEAD_DIM
    W = WINDOW
    nb = S // W
    c_idx = lax.axis_index("c")
    xs, tgt = x[0], loss_target[0]

    n_in_shard = w_in_a.shape[2]
    rows_in = -(-n_in_shard // 32) * 32
    row_pad = lambda a: jnp.pad(a, [(0, 0)] * (a.ndim - 2) + [(0, rows_in - a.shape[-2]), (0, 0)])
    t_in = lambda a: jnp.swapaxes(a[0], 0, 1)
    shards = {"w_in_a": row_pad(t_in(w_in_a)), "w_out_a": w_out_a[0], "w_up0": w_up[0], "w_down0": w_down[0], "w_kv": w_kv,
              "w_q_b": w_q_b[0], "w_out_b": w_out_b[0], "w_up1": w_up[1], "w_down1": w_down[1]}
    parts = list(shards)
    groups = [("w_in_a", "w_out_a"), ("w_up0", "w_down0"), ("w_kv", "w_q_b", "w_out_b", "w_up1", "w_down1")]
    started = []
    for i, grp in enumerate(groups):
        behind = started[-1][4] if started else g_attn[0]
        started.append(_gather_start([shards[n].astype(BF16) for n in grp], behind, f"gather_start{i}"))
    gathered = {}

    def finish_gather(i, after):
        srcs, lands = _gather_wait(started[i], after, f"gather_wait{i}")
        gathered.update(zip(groups[i], _gather_pass_on(srcs, lands, f"gather_pass_on{i}")))

    def land_gather(i, after):
        srcs, lands = _gather_wait(started[i], after, f"gather_wait{i}")
        n = len(srcs)
        passing, token = _split_start(_plan_pass_on(n), list(srcs) + list(lands), 4 * n, after, f"pass_on_start{i}")
        return passing, token[0:1, 0:1]

    def finish_pass_on(i, passing, after):
        n = len(groups[i])
        gathered.update(zip(groups[i], _split_wait(_plan_pass_on(n), passing, after, f"pass_on_wait{i}")[n:]))

    vec = lambda a: a.reshape(1, -1)
    twice = lambda a: jnp.tile(a.reshape(1, -1), (1, 2))

    g_attn0 = vec(g_attn[0]) + sum(st[4][0, 0] for st in started)
    (n0,) = _rms_fwd(xs, [g_attn0], "rms_attn0")
    finish_gather(0, n0)
    win_t = gathered["w_in_a"][:, :n_in_shard].reshape(-1, D)
    win_t = jnp.pad(win_t, ((0, (-win_t.shape[0]) % 128), (0, 0)))
    wout_a = gathered["w_out_a"].reshape(-1, D)
    n_in = win_t.shape[0]
    tile_in = 640 if n_in % 640 == 0 else 128
    proj = _matmul(n0, win_t, "nt", "proj_in", tm=2048, tn=tile_in)
    zt = proj[:, 3 * hw:3 * hw + H].T
    c_row = _gate_fwd(zt, b_f.reshape(H, 1), "gate_fwd")
    c_row3 = c_row.reshape(H, 1, S)
    o_a, lse_a = _fox_fwd(proj, c_row3, twice(gq_a[0]), twice(gk_a[0]), H, "fox_fwd", t=512)
    passing1, tie = land_gather(1, o_a)
    h1, n1 = _matmul(o_a, wout_a, "nn", "out_a", out_dtypes=(F32, BF16), extras=(xs,), row_extras=(vec(g_mlp[0]) + tie,),
                     epilogue=_residual_then_norms, tm=1024, tn=1024)
    finish_pass_on(1, passing1, n1)
    wup = [gathered["w_up0"], None]
    wdown = [gathered["w_down0"].reshape(-1, D), None]
    passing = []

    def land_last_group(a):
        started_passing, tie = land_gather(2, a)
        passing.append(started_passing)
        return tie

    h2, (nkv, n2), mlp0 = _mlp_fwd(h1, n1, wup[0], wdown[0], [vec(g_kv), vec(g_attn[1])], "0", between=land_last_group)
    passing2 = passing[0]

    finish_pass_on(2, passing2, h2)
    wq_b, wout_b = gathered["w_q_b"].reshape(-1, D), gathered["w_out_b"].reshape(-1, D)
    wkv = gathered["w_kv"].reshape(D, -1)
    wup[1], wdown[1] = gathered["w_up1"], gathered["w_down1"].reshape(-1, D)
    kv = _matmul(nkv, wkv, "nn", "proj_kv", tm=2048)
    kk, vv = _doubled_heads(kv[:, :kvw], KVH), _doubled_heads(kv[:, kvw:], KVH)
    q2 = _matmul(n2, wq_b, "nn", "proj_q", tm=1024, tn=1024)
    onehot = jnp.asarray(_bucket_onehot(), dtype=BF16)
    bias = _bias_expand(rel_bias.T, onehot, "bias_expand").reshape(H, W, 2 * W)
    bias_ab = bias.reshape(H // 2, 2 * W, 2 * W)
    bias_t_ab = bias.reshape(H // 2, 2, W, 2 * W).transpose(0, 3, 1, 2).reshape(H // 2, 2 * W, 2 * W)
    sink_ab = jnp.repeat(sinks[0].reshape(H // 2, 2), W, axis=1)
    o_b, lse_b = _swa_fwd(q2, kk, vv, bias_ab, sink_ab.reshape(H // 2, 2 * W, 1), twice(gq_b[0]), twice(gk_b), "swa_fwd")
    h3, n3 = _matmul(o_b, wout_b, "nn", "out_b", out_dtypes=(F32, BF16), extras=(h2,), row_extras=(vec(g_mlp[1]),),
                     epilogue=_residual_then_norms, tm=1024, tn=1024)
    h4, _, mlp1 = _mlp_fwd(h3, n3, wup[1], wdown[1], [], "1")

    dh4, dh4_b, loss_part = _loss_head(h4, tgt, "loss_head")

    place = jnp.stack([2 * lax.axis_index("x") + lax.axis_index("y"), c_idx]).astype(jnp.int32)
    scattering = []

    def pair_and_scatter(names, mine, got):
        pair_sums = [_sum_core_pair(a, g, place, "sum_core_pair_" + n) for n, a, g in zip(names, mine, got)]
        started_scatter, token = _scatter_start(pair_sums, "scatter_start_" + names[0])
        scattering.append((names, started_scatter))
        return token[0:1, :]

    def start_reduce(named):
        names = list(named)
        mine = [named[n] for n in names]
        return pair_and_scatter(names, mine, _swap_halves_group(mine, "swap_grad_halves_" + names[0]))

    def start_swap(named):
        names = list(named)
        mine = [named[n] for n in names]
        lands = [lax.empty((a.shape[0], a.shape[1] // 2, a.shape[2]), a.dtype) for a in mine]
        swapping, token = _split_start(_plan_swap_halves(len(mine)), mine + lands, len(mine), mine[0], "swap_start_" + names[0])
        return names, swapping, token

    def finish_swap(swap, after):
        names, swapping, _ = swap
        n = len(names)
        arrays = _split_wait(_plan_swap_halves(n), swapping, after, "swap_wait_" + names[0])
        return pair_and_scatter(names, arrays[:n], arrays[n:])

    dh3, dh3_b, dg_mlp1, dw_up1, dw_down1 = _mlp_bwd(dh4, dh4_b, h3, vec(g_mlp[1]), wup[1], wdown[1], mlp1, "1")
    swap1 = start_swap({"w_down1": dw_down1, "w_up1": dw_up1})
    do_b = _matmul(dh3_b, wout_b, "nt", "do_b", tm=1024, tn=1024, after=swap1[2])
    dw_out_b = _matmul(o_b, dh3_b, "tn", "dw_out_b", out_dtypes=(BF16,), tm=1024, tn=1024, after=do_b)
    tie1 = finish_swap(swap1, dw_out_b)
    dq2, dk2, dv2, dbias_t_ab, dsink, dgq_b, dgk_b = _swa_bwd(
        q2, kk, vv, bias_t_ab, sink_ab.reshape(H // 2, 1, 2 * W), twice(gq_b[0]) + tie1, twice(gk_b),
        lse_b, do_b, "swa_bwd")
    dbias = dbias_t_ab.reshape(H // 2, 2 * W, 2, W).transpose(0, 2, 3, 1).reshape(H, W * 2 * W)
    d_rel_bias = _bias_reduce(dbias, onehot, "bias_reduce").T
    dw_q_b = _matmul(n2, dq2, "tn", "dw_q_b", out_dtypes=(BF16,), tm=1024, tn=1024)
    dn2 = _matmul(dq2, wq_b, "nt", "dn2", tm=1024, tn=1024)
    dkv = jnp.concatenate([dk2[h, :, :HEAD_DIM] for h in range(KVH)] + [dv2[h, :, :HEAD_DIM] for h in range(KVH)],
                          axis=1).astype(BF16)
    dw_kv = _matmul(nkv, dkv, "tn", "dw_kv", out_dtypes=(BF16,), tm=1024)
    dnkv = _matmul(dkv, wkv, "nt", "dnkv", tm=1024, tn=1024)
    tie2 = start_reduce({"w_out_b": dw_out_b.reshape(N_CHIPS, -1, D), "w_q_b": dw_q_b.reshape(N_CHIPS, -1, D),
                         "w_kv": dw_kv.reshape(N_CHIPS, -1, 2 * kvw)})
    dh2, dh2_b, (dg_kv, dg_attn1) = _rms_bwd(h2, dh3, [vec(g_kv) + tie2[:, :1], vec(g_attn[1])], [dnkv, dn2], "rms_attn1_bwd")

    dh1, dh1_b, dg_mlp0, dw_up0, dw_down0 = _mlp_bwd(dh2, dh2_b, h1, vec(g_mlp[0]), wup[0], wdown[0], mlp0, "0")
    swap3 = start_swap({"w_down0": dw_down0, "w_up0": dw_up0})
    do_a = _matmul(dh1_b, wout_a, "nt", "do_a", tm=1024, tn=1024, after=swap3[2])
    dw_out_a = _matmul(o_a, dh1_b, "tn", "dw_out_a", out_dtypes=(BF16,), tm=1024, tn=1024, after=do_a)
    tie3 = finish_swap(swap3, dw_out_a)
    dq_a, dk_a, dv_a, dc_row, dgq_a, dgk_a = _fox_bwd(
        proj, c_row3, twice(gq_a[0]) + tie3, twice(gk_a[0]), lse_a, do_a, H, "fox_bwd", t=512)
    dzt, db_f = _gate_bwd(dc_row.reshape(H, S), zt, b_f.reshape(H, 1), "gate_bwd")
    dproj = jnp.concatenate([dq_a, dk_a, dv_a, dzt.T.astype(BF16), jnp.zeros((S, n_in - 3 * hw - H), BF16)], axis=1)
    dw_in_t = _matmul(dproj, n0, "tn", "dw_in", out_dtypes=(BF16,), tm=tile_in, tn=1024)
    dw_in4 = row_pad(dw_in_t[:3 * hw + H].reshape(N_CHIPS, -1, D))
    tie4 = start_reduce({"w_out_a": dw_out_a.reshape(N_CHIPS, -1, D), "w_in_a": dw_in4})
    grad_x, _, dg_attn0 = _matmul(dproj, win_t, "nn", "dn0", out_dtypes=(F32, BF16, F32), extras=(xs, dh1),
                                  row_extras=(vec(g_attn[0]) + tie4[:, :1],), epilogue=_rms_bwd_epilogue, row_accums=1,
                                  tm=1024, tn=1024, tk=tile_in)

    small_grads = {
        "g_attn": jnp.concatenate([dg_attn0, dg_attn1], axis=0), "g_mlp": jnp.concatenate([dg_mlp0, dg_mlp1], axis=0),
        "b_f": db_f.reshape(1, H), "gq_a": dgq_a[:, :HEAD_DIM], "gk_a": dgk_a[:, :HEAD_DIM], "g_kv": dg_kv.reshape(-1),
        "gk_b": dgk_b[0, :HEAD_DIM], "gq_b": dgq_b[:, :HEAD_DIM], "sinks": dsink[:, 0, 0].reshape(1, H), "rel_bias": d_rel_bias,
    }
    small_shapes = [given[n].shape for n in SMALL] + [(1,)]
    spack = _pack_small([small_grads[n] for n in SMALL] + [loss_part])
    gathering_small, token = _split_start(_plan_gather_small, [spack, lax.empty((8,) + spack.shape, F32)], 7, grad_x,
                                          "gather_small_start")
    sharing = []
    for names, started_scatter in scattering:
        pair_sums, landed = _scatter_wait(started_scatter, grad_x, "scatter_wait_" + names[0])
        halves = [_sum_chips(p, l, place, "sum_chips_" + n) for n, p, l in zip(names, pair_sums, landed)]
        started_share, token = _split_start(_plan_share_halves(len(halves)), halves, len(halves), token, "share_start_" + names[0])
        sharing.append((names, started_share))
    own_small, others_small = _split_wait(_plan_gather_small, gathering_small, token, "gather_small_wait")
    small_sum = _sum_gathered_small(others_small, own_small, place, "sum_small")
    small_red = _unpack(small_sum.reshape(-1), small_shapes)
    reduced = {}
    for names, started_share in sharing:
        for n, r in zip(names, _split_wait(_plan_share_halves(len(names)), started_share, token, "share_wait_" + names[0])):
            reduced[n] = r.reshape(-1, r.shape[2])
    reduced["w_in_a"] = reduced["w_in_a"][:n_in_shard]
    loss = small_red[-1][0]

    grads = dict(zip(SMALL, small_red))
    no_loss = [jnp.zeros((1,), F32)]
    sw = _pack_small([given[n] for n in SMALL] + no_loss)
    sm = _pack_small([given["m_" + n] for n in SMALL] + no_loss)
    sv = _pack_small([given["v_" + n] for n in SMALL] + no_loss)
    sd, sm2, sv2 = _adamw(sw, small_sum, sm, sv, "adamw_small", tr=sw.shape[0])
    delta = dict(zip(SMALL, _unpack(sd.reshape(-1), small_shapes)))
    new_m = dict(zip(SMALL, _unpack(sm2.reshape(-1), small_shapes)))
    new_v = dict(zip(SMALL, _unpack(sv2.reshape(-1), small_shapes)))
    for n in ("w_out_a", "w_kv", "w_q_b", "w_out_b"):
        w = given[n]
        two_d = (-1, w.shape[-1])
        d, m2, v2 = _adamw(w.reshape(two_d), reduced[n], given["m_" + n].reshape(two_d), given["v_" + n].reshape(two_d),
                           "adamw_" + n)
        grads[n] = reduced[n].reshape(w.shape)
        delta[n], new_m[n], new_v[n] = d.reshape(w.shape), m2.reshape(w.shape), v2.reshape(w.shape)
    d, m2, v2 = _adamw(t_in(w_in_a), reduced["w_in_a"], t_in(m_w_in_a), t_in(v_w_in_a), "adamw_w_in_a")
    back = lambda a: jnp.swapaxes(a, 0, 1)[None]
    grads["w_in_a"], delta["w_in_a"], new_m["w_in_a"], new_v["w_in_a"] = back(reduced["w_in_a"]), back(d), back(m2), back(v2)
    for n in ("w_up", "w_down"):
        w = given[n]
        two_d = (-1, w.shape[-1])
        g, d, m2, v2 = _adamw_two_layers(w.reshape(two_d), reduced[n + "0"], reduced[n + "1"], given["m_" + n].reshape(two_d),
                                         given["v_" + n].reshape(two_d), "adamw_" + n)
        grads[n], delta[n], new_m[n], new_v[n] = g.reshape(w.shape), d.reshape(w.shape), m2.reshape(w.shape), v2.reshape(w.shape)

    order = ["g_attn", "g_mlp", "w_in_a", "b_f", "gq_a", "gk_a", "w_out_a", "g_kv", "w_kv", "gk_b", "w_q_b", "gq_b",
             "sinks", "rel_bias", "w_out_b", "w_up", "w_down"]
    return (loss, grad_x[None], *[grads[n] for n in order], *[delta[n] for n in order],
            *[new_m[n] for n in order], *[new_v[n] for n in order])
```

```python
import numpy as np
import jax
import jax.numpy as jnp
from jax import lax
from jax.experimental import pallas as pl
from jax.experimental.pallas import tpu as pltpu

F32 = jnp.float32
BF16 = jnp.bfloat16
MESH = pl.DeviceIdType.MESH

HEAD_DIM = 64
LANES = 128
WINDOW = 128
N_BUCKETS = 32
REL_MAX_DIST = 128
NORM_EPS = 1e-6
ADAM_LR = 0.001
ADAM_B1 = 0.9
ADAM_B2 = 0.999
ADAM_EPS = 1e-08
ADAM_WD = 0.01
ADAM_STEP = 10
NEG = -1e30
N_CHIPS = 4
VMEM_LIMIT = 56 * 1024 * 1024
HBM_SPEC = pl.BlockSpec(memory_space=pltpu.HBM)
VMEM_SPEC = pl.BlockSpec(memory_space=pltpu.VMEM)

SMALL = ("g_attn", "g_mlp", "b_f", "gq_a", "gk_a", "g_kv", "gk_b", "gq_b", "sinks", "rel_bias")


def _pcall(body, **kw):
    return pl.pallas_call(body, **kw)


def _params(sem=None):
    return pltpu.CompilerParams(dimension_semantics=sem, vmem_limit_bytes=VMEM_LIMIT)


def _rinv(x):
    return lax.rsqrt(jnp.mean(x * x, axis=-1, keepdims=True) + NORM_EPS)


def _dot(a, b, dims, precision=None):
    return lax.dot_general(a, b, (dims, ((), ())), precision=precision, preferred_element_type=F32)


NN = ((1,), (0,))
NT = ((1,), (1,))
TN = ((0,), (0,))


def _accumulate(ref, val, first):
    @pl.when(first)
    def _():
        ref[...] = val

    @pl.when(jnp.logical_not(first))
    def _():
        ref[...] += val


def _matmul(a, b, mode, name, out_dtypes=(F32,), extras=(), row_extras=(), epilogue=None, tm=512, tn=512, tk=None, chipwise=None,
            after=None, row_accums=0):
    if chipwise == "b":
        nc = b.shape[2]
        M, K = a.shape
        (K2, N) = (b.shape[1], N_CHIPS * nc) if mode == "nn" else (N_CHIPS * nc, b.shape[1])
    elif mode == "nn":
        (M, K), (K2, N) = a.shape, b.shape
    elif mode == "nt":
        (M, K), (N, K2) = a.shape, b.shape
    else:
        (K, M), (K2, N) = a.shape, b.shape
    assert K == K2, (a.shape, b.shape, mode)
    tm, tn = min(tm, M), min(tn, N)
    tk = K if tk is None else tk
    assert M % tm == 0 and N % tn == 0 and K % tk == 0, (M, N, K, tm, tn, tk)
    nk = K // tk
    dims = {"nn": NN, "nt": NT, "tn": TN}[mode]
    a_spec = pl.BlockSpec((tk, tm), lambda i, j, k: (k, i)) if mode == "tn" else pl.BlockSpec((tm, tk), lambda i, j, k: (i, k))
    b_spec = pl.BlockSpec((tn, tk), lambda i, j, k: (j, k)) if mode == "nt" else pl.BlockSpec((tk, tn), lambda i, j, k: (k, j))
    o_spec = pl.BlockSpec((tm, tn), lambda i, j, k: (i, j))
    out_shape = (M, N)
    if chipwise == "b" and mode == "nn":
        per = nc // tn
        assert tk == K and nc % tn == 0
        b_spec = pl.BlockSpec((None, tk, tn), lambda i, j, k: (j // per, 0, j % per))
    elif chipwise == "b":
        assert mode == "nt" and tk == nc
        b_spec = pl.BlockSpec((None, tn, tk), lambda i, j, k: (k, j, 0))
    elif chipwise == "out_rows":
        per = (M // N_CHIPS) // tm
        assert (M // N_CHIPS) % tm == 0 and not extras
        o_spec = pl.BlockSpec((None, tm, tn), lambda i, j, k: (i // per, i % per, j))
        out_shape = (N_CHIPS, M // N_CHIPS, N)
    elif chipwise == "out":
        per = (N // N_CHIPS) // tn
        assert (N // N_CHIPS) % tn == 0
        o_spec = pl.BlockSpec((None, tm, tn), lambda i, j, k: (j // per, i, j % per))
        out_shape = (N_CHIPS, M, N // N_CHIPS)
        assert not extras
    n_ex, n_rex, n_out = len(extras), len(row_extras), len(out_dtypes)
    assert not row_accums or tn == N
    tail = () if after is None else (after,)
    n_in = 2 + n_ex + n_rex + len(tail)

    def body(*refs):
        a_ref, b_ref = refs[0], refs[1]
        ex_refs = refs[2:2 + n_ex + n_rex]
        out_refs = refs[n_in:n_in + n_out]
        part = _dot(a_ref[...].astype(BF16), b_ref[...].astype(BF16), dims)

        def finish(acc):
            outs = (acc,) if epilogue is None else epilogue(acc, *[r[...] for r in ex_refs])
            for idx, (r, o) in enumerate(zip(out_refs, outs)):
                if idx >= n_out - row_accums:
                    _accumulate(r, o, pl.program_id(0) == 0)
                else:
                    r[...] = o.astype(r.dtype)

        if nk == 1:
            finish(part)
            return
        acc_ref = refs[n_in + n_out]
        k = pl.program_id(2)

        @pl.when(k == 0)
        def _():
            acc_ref[...] = part

        @pl.when(jnp.logical_and(k > 0, k < nk - 1))
        def _():
            acc_ref[...] += part

        @pl.when(k == nk - 1)
        def _():
            finish(acc_ref[...] + part)

    row_spec = pl.BlockSpec((1, tn), lambda i, j, k: (0, j))
    outs = _pcall(
        body, name=name, grid=(M // tm, N // tn, nk),
        in_specs=[a_spec, b_spec] + [o_spec] * n_ex + [row_spec] * n_rex + [pl.BlockSpec(memory_space=pl.ANY)] * len(tail),
        out_specs=[o_spec] * (n_out - row_accums) + [row_spec] * row_accums,
        out_shape=[jax.ShapeDtypeStruct(out_shape, dt) for dt in out_dtypes[:n_out - row_accums]]
        + [jax.ShapeDtypeStruct((1, N), F32)] * row_accums,
        scratch_shapes=[pltpu.VMEM((tm, tn), F32)] if nk > 1 else [],
        compiler_params=_params(("arbitrary",) * 3 if row_accums else ("parallel", "parallel", "arbitrary")),
    )(a, b, *extras, *row_extras, *tail)
    return outs[0] if n_out == 1 else outs


def _rms_fwd(x, gains, name, ts=512):
    S, D = x.shape
    ts = min(ts, S)
    n = len(gains)

    def body(*refs):
        x_ref, g_refs, o_refs = refs[0], refs[1:1 + n], refs[1 + n:]
        xv = x_ref[...]
        xh = xv * _rinv(xv)
        for g_ref, o_ref in zip(g_refs, o_refs):
            o_ref[...] = (xh * g_ref[...]).astype(BF16)

    row = pl.BlockSpec((ts, D), lambda i: (i, 0))
    vec = pl.BlockSpec((1, D), lambda i: (0, 0))
    return _pcall(body, name=name, grid=(S // ts,), in_specs=[row] + [vec] * n, out_specs=[row] * n,
                  out_shape=[jax.ShapeDtypeStruct((S, D), BF16)] * n, compiler_params=_params(("parallel",)))(x, *gains)


def _rms_bwd(x, dres, gains, dns, name, ts=512):
    S, D = x.shape
    ts = min(ts, S)
    n = len(gains)

    def body(*refs):
        x_ref, dres_ref = refs[0], refs[1]
        g_refs, dn_refs = refs[2:2 + n], refs[2 + n:2 + 2 * n]
        dx_ref, dxb_ref, dg_refs = refs[2 + 2 * n], refs[3 + 2 * n], refs[4 + 2 * n:]
        xv = x_ref[...]
        r = _rinv(xv)
        xh = xv * r
        dx = dres_ref[...]
        first = pl.program_id(0) == 0
        for g_ref, dn_ref, dg_ref in zip(g_refs, dn_refs, dg_refs):
            dn = dn_ref[...].astype(F32)
            _accumulate(dg_ref, jnp.sum(dn * xh, axis=0, keepdims=True), first)
            dxh = dn * g_ref[...]
            dx = dx + r * (dxh - xh * jnp.mean(dxh * xh, axis=-1, keepdims=True))
        dx_ref[...] = dx
        dxb_ref[...] = dx.astype(BF16)

    row = pl.BlockSpec((ts, D), lambda i: (i, 0))
    vec = pl.BlockSpec((1, D), lambda i: (0, 0))
    outs = _pcall(body, name=name, grid=(S // ts,), in_specs=[row, row] + [vec] * n + [row] * n,
                  out_specs=[row, row] + [vec] * n,
                  out_shape=[jax.ShapeDtypeStruct((S, D), F32), jax.ShapeDtypeStruct((S, D), BF16)]
                  + [jax.ShapeDtypeStruct((1, D), F32)] * n,
                  compiler_params=_params(("arbitrary",)))(x, dres, *gains, *dns)
    return outs[0], outs[1], outs[2:]


def _loss_head(h, tgt, name, ts=512):
    S, D = h.shape
    ts = min(ts, S)

    def body(h_ref, t_ref, dh_ref, dhb_ref, loss_ref):
        err = h_ref[...] - t_ref[...]
        dh = err * (1.0 / D)
        dh_ref[...] = dh
        dhb_ref[...] = dh.astype(BF16)
        part = 0.5 * jnp.sum(jnp.mean(err * err, axis=-1, keepdims=True), axis=0, keepdims=True)
        _accumulate(loss_ref, part, pl.program_id(0) == 0)

    row = pl.BlockSpec((ts, D), lambda i: (i, 0))
    return _pcall(body, name=name, grid=(S // ts,), in_specs=[row, row],
                  out_specs=[row, row, pl.BlockSpec((1, 1), lambda i: (0, 0))],
                  out_shape=[jax.ShapeDtypeStruct((S, D), F32), jax.ShapeDtypeStruct((S, D), BF16),
                             jax.ShapeDtypeStruct((1, 1), F32)],
                  compiler_params=_params(("arbitrary",)))(h, tgt)


def _gate_fwd(zt, bf, name):
    H, S = zt.shape
    nb = S // 128

    def body(z_ref, b_ref, c_ref):
        z = z_ref[...] + b_ref[...]
        lf = jnp.minimum(z, 0.0) - jnp.log(1.0 + jnp.exp(-jnp.abs(z)))
        upper = (lax.broadcasted_iota(jnp.int32, (128, 128), 0) <= lax.broadcasted_iota(jnp.int32, (128, 128), 1)).astype(F32)
        carry = jnp.zeros((H, 1), F32)
        for blk in range(nb):
            cs = _dot(lf[:, blk * 128:(blk + 1) * 128], upper, NN, precision=lax.Precision.HIGHEST) + carry
            c_ref[:, blk * 128:(blk + 1) * 128] = cs
            carry = cs[:, 127:128]

    return _pcall(body, name=name, in_specs=[VMEM_SPEC, VMEM_SPEC], out_specs=VMEM_SPEC,
                  out_shape=jax.ShapeDtypeStruct((H, S), F32))(zt, bf)


def _gate_bwd(dct, zt, bf, name):
    H, S = zt.shape
    nb = S // 128

    def body(dc_ref, z_ref, b_ref, dz_ref, db_ref):
        z = z_ref[...] + b_ref[...]
        e = jnp.exp(-jnp.abs(z))
        sig_neg = jnp.where(z >= 0, e, 1.0) / (1.0 + e)
        lower = (lax.broadcasted_iota(jnp.int32, (128, 128), 0) >= lax.broadcasted_iota(jnp.int32, (128, 128), 1)).astype(F32)
        dc = dc_ref[...]
        carry = jnp.zeros((H, 1), F32)
        db = jnp.zeros((H, 1), F32)
        for blk in reversed(range(nb)):
            sl = slice(blk * 128, (blk + 1) * 128)
            dlf = _dot(dc[:, sl], lower, NN, precision=lax.Precision.HIGHEST) + carry
            carry = dlf[:, 0:1]
            dz = dlf * sig_neg[:, sl]
            dz_ref[:, sl] = dz
            db = db + jnp.sum(dz, axis=1, keepdims=True)
        db_ref[...] = db

    return _pcall(body, name=name, in_specs=[VMEM_SPEC] * 3, out_specs=[VMEM_SPEC] * 2,
                  out_shape=[jax.ShapeDtypeStruct((H, S), F32), jax.ShapeDtypeStruct((H, 1), F32)])(dct, zt, bf)


def _lane_is_a():
    return lax.broadcasted_iota(jnp.int32, (1, LANES), 1) < HEAD_DIM


def _per_head_mean(x, is_a):
    sa = jnp.sum(jnp.where(is_a, x, 0.0), axis=-1, keepdims=True)
    sb = jnp.sum(jnp.where(is_a, 0.0, x), axis=-1, keepdims=True)
    return jnp.where(is_a, sa, sb) / HEAD_DIM


def _pair_norm(raw, gain, is_a):
    return raw * lax.rsqrt(_per_head_mean(raw * raw, is_a) + NORM_EPS) * gain


def _pair_norm_bwd(raw, gain, dnormed, is_a):
    r = lax.rsqrt(_per_head_mean(raw * raw, is_a) + NORM_EPS)
    xh = raw * r
    dgain = jnp.sum(dnormed * xh, axis=0, keepdims=True)
    dxh = dnormed * gain
    return r * (dxh - xh * _per_head_mean(dxh * xh, is_a)), dgain


def _fold_heads(x):
    i = lax.broadcasted_iota(jnp.int32, (LANES, LANES), 0)
    j = lax.broadcasted_iota(jnp.int32, (LANES, LANES), 1)
    fold = ((i == j) | (i == j + HEAD_DIM) | (i + HEAD_DIM == j)).astype(F32)
    return _dot(x, fold, NN, precision=lax.Precision.HIGHEST)


def _fold_row(ref):
    ref[...] = _fold_heads(jnp.broadcast_to(ref[...], (8, LANES)))[0:1, :]


def _as_col(row):
    return jnp.broadcast_to(row, (LANES, row.shape[1])).T[:, 0:1]


def _as_row(col):
    return jnp.broadcast_to(col, (col.shape[0], LANES)).T[0:1, :]


def _tri_mask(t, keys_on_rows):
    r = lax.broadcasted_iota(jnp.int32, (t, t), 0)
    c = lax.broadcasted_iota(jnp.int32, (t, t), 1)
    return (r <= c) if keys_on_rows else (r >= c)


def _fox_fwd(proj, c_row, gq2, gk2, n_heads, name, t=256):
    S = proj.shape[0]
    H = n_heads
    P = H // 2
    t = min(t, S)
    nq = S // t

    def body(q_ref, k_ref, v_ref, cr_ref, gq_ref, gk_ref, o_ref, lse_ref, qs_s, kb_s, vb_s):
        is_a = _lane_is_a()
        qn = _pair_norm(q_ref[...], gq_ref[...], is_a) * 0.125
        qs_s[0] = jnp.where(is_a, qn, 0.0).astype(BF16)
        qs_s[1] = jnp.where(is_a, 0.0, qn).astype(BF16)
        kb_s[...] = _pair_norm(k_ref[...], gk_ref[...], is_a).astype(BF16)
        vb_s[...] = v_ref[...].astype(BF16)
        causal = _tri_mask(t, False)
        for i in range(nq):
            t0 = i * t
            rows = slice(t0, t0 + t)
            o_pair = None
            for a in range(2):
                qi = qs_s[a, rows, :]
                ci = _as_col(cr_ref[a, :, rows])
                s_d = jnp.where(causal, _dot(qi, kb_s[rows, :], NT) + ci - cr_ref[a, :, rows], NEG)
                m = jnp.max(s_d, axis=-1, keepdims=True)
                if i > 0:
                    s_l = _dot(qi, kb_s[0:t0, :], NT) + ci - cr_ref[a, :, 0:t0]
                    m = jnp.maximum(m, jnp.max(s_l, axis=-1, keepdims=True))
                p_d = jnp.exp(s_d - m)
                l = jnp.sum(p_d, axis=-1, keepdims=True)
                acc = _dot(p_d.astype(BF16), vb_s[rows, :], NN)
                if i > 0:
                    p_l = jnp.exp(s_l - m)
                    l = l + jnp.sum(p_l, axis=-1, keepdims=True)
                    acc = acc + _dot(p_l.astype(BF16), vb_s[0:t0, :], NN)
                o_a = acc / l
                lse_ref[a, :, rows] = _as_row(m + jnp.log(l))
                o_pair = o_a if a == 0 else jnp.where(is_a, o_pair, o_a)
            o_ref[rows, :] = o_pair.astype(BF16)

    def cols(off):
        return pl.BlockSpec((S, LANES), lambda p: (0, off + p))

    rowv = pl.BlockSpec((2, 1, S), lambda p: (p, 0, 0))
    gain = pl.BlockSpec((1, LANES), lambda p: (0, 0))
    return _pcall(body, name=name, grid=(P,), in_specs=[cols(0), cols(P), cols(2 * P), rowv, gain, gain],
                  out_specs=[cols(0), rowv],
                  out_shape=[jax.ShapeDtypeStruct((S, H * HEAD_DIM), BF16), jax.ShapeDtypeStruct((H, 1, S), F32)],
                  scratch_shapes=[pltpu.VMEM((2, S, LANES), BF16), pltpu.VMEM((S, LANES), BF16), pltpu.VMEM((S, LANES), BF16)],
                  compiler_params=_params(("parallel",)))(proj, proj, proj, c_row, gq2, gk2)


def _fox_bwd(proj, c_row, gq2, gk2, lse_row, do, n_heads, name, t=256):
    S = proj.shape[0]
    H = n_heads
    P = H // 2
    t = min(t, S)
    nq = S // t
    assert t % LANES == 0

    def body(q_ref, k_ref, v_ref, cr_ref, gq_ref, gk_ref, lr_ref, do_ref,
             dq_ref, dk_ref, dv_ref, dc_ref, dgq_ref, dgk_ref,
             qs_s, kb_s, kt_s, vb_s, dob_s, dq_s, dk_s, dv_s, dcs_s, cc_s):
        is_a = _lane_is_a()
        for a in range(2):
            for i in range(nq):
                cc_s[a, i * t:(i + 1) * t, :] = _as_col(cr_ref[a, :, i * t:(i + 1) * t])
        qn = _pair_norm(q_ref[...], gq_ref[...], is_a) * 0.125
        qs_s[0] = jnp.where(is_a, qn, 0.0).astype(BF16)
        qs_s[1] = jnp.where(is_a, 0.0, qn).astype(BF16)
        kn = _pair_norm(k_ref[...], gk_ref[...], is_a)
        kb_s[...] = kn.astype(BF16)
        kt_s[0] = jnp.where(is_a, kn, 0.0).T.astype(BF16)
        kt_s[1] = jnp.where(is_a, 0.0, kn).T.astype(BF16)
        vb_s[...] = v_ref[...].astype(BF16)
        dov = do_ref[...]
        dob_s[0] = jnp.where(is_a, dov, 0.0).astype(BF16)
        dob_s[1] = jnp.where(is_a, 0.0, dov).astype(BF16)
        dk_s[...] = jnp.zeros((S, LANES), F32)
        dv_s[...] = jnp.zeros((S, LANES), F32)
        dcs_s[...] = jnp.zeros((2, S, LANES), F32)
        causal = _tri_mask(t, True)
        for i in range(nq):
            t0 = i * t
            rows = slice(t0, t0 + t)
            dq_t = jnp.zeros((LANES, t), F32)
            for a in range(2):
                qi = qs_s[a, rows, :]
                doi = dob_s[a, rows, :]
                cri = cr_ref[a, :, rows]
                lri = lr_ref[a, :, rows]

                def probs(keys, masked, a=a, qi=qi, doi=doi, cri=cri, lri=lri):
                    p_t = jnp.exp(_dot(kb_s[keys, :], qi, NT) + cri - cc_s[a, keys, :] - lri)
                    if masked:
                        p_t = jnp.where(causal, p_t, 0.0)
                    return p_t, _dot(vb_s[keys, :], doi, NT)

                parts = [(rows,) + probs(rows, True)]
                if i > 0:
                    parts.append((slice(0, t0),) + probs(slice(0, t0), False))
                delta = sum(jnp.sum(p_t * dp_t, axis=0, keepdims=True) for _, p_t, dp_t in parts)
                for keys, p_t, dp_t in parts:
                    ds_t = p_t * (dp_t - delta)
                    dsb = ds_t.astype(BF16)
                    dv_s[keys, :] += _dot(p_t.astype(BF16), doi, NN)
                    dk_s[keys, :] += _dot(dsb, qi, NN)
                    dq_t = dq_t + _dot(kt_s[a, :, keys], dsb, NN)
                    dcs_s[a, keys, :] += sum(ds_t[:, b * LANES:(b + 1) * LANES] for b in range(t // LANES))
            dq_s[rows, :] = dq_t.T
        first = pl.program_id(0) == 0
        last = pl.program_id(0) == P - 1
        dq_raw, dgq = _pair_norm_bwd(q_ref[...], gq_ref[...], dq_s[...] * 0.125, is_a)
        dq_ref[...] = dq_raw.astype(BF16)
        _accumulate(dgq_ref, dgq, first)
        dk_raw, dgk = _pair_norm_bwd(k_ref[...], gk_ref[...], dk_s[...], is_a)
        dk_ref[...] = dk_raw.astype(BF16)
        _accumulate(dgk_ref, dgk, first)
        dv_ref[...] = dv_s[...].astype(BF16)
        for a in range(2):
            for i in range(nq):
                rows = slice(i * t, (i + 1) * t)
                dc_ref[a, :, rows] = _as_row(-jnp.sum(dcs_s[a, rows, :], axis=1, keepdims=True))

        @pl.when(last)
        def _():
            _fold_row(dgq_ref)
            _fold_row(dgk_ref)

    def cols(off):
        return pl.BlockSpec((S, LANES), lambda p: (0, off + p))

    rowv = pl.BlockSpec((2, 1, S), lambda p: (p, 0, 0))
    gain = pl.BlockSpec((1, LANES), lambda p: (0, 0))
    wide = jax.ShapeDtypeStruct((S, H * HEAD_DIM), BF16)
    gs = jax.ShapeDtypeStruct((1, LANES), F32)
    return _pcall(body, name=name, grid=(P,),
                  in_specs=[cols(0), cols(P), cols(2 * P), rowv, gain, gain, rowv, cols(0)],
                  out_specs=[cols(0), cols(0), cols(0), rowv, gain, gain],
                  out_shape=[wide, wide, wide, jax.ShapeDtypeStruct((H, 1, S), F32), gs, gs],
                  scratch_shapes=[pltpu.VMEM((2, S, LANES), BF16), pltpu.VMEM((S, LANES), BF16), pltpu.VMEM((2, LANES, S), BF16),
                                  pltpu.VMEM((S, LANES), BF16), pltpu.VMEM((2, S, LANES), BF16)]
                  + [pltpu.VMEM((S, LANES), F32)] * 3 + [pltpu.VMEM((2, S, LANES), F32), pltpu.VMEM((2, S, 1), F32)],
                  compiler_params=_params(("arbitrary",)))(proj, proj, proj, c_row, gq2, gk2, lse_row, do)


def _bucket_onehot():
    W = WINDOW
    dist = np.arange(W)[:, None] + W - np.arange(2 * W)[None, :]
    n = np.maximum(dist, 0)
    max_exact = N_BUCKETS // 2
    large = max_exact + (np.log(np.maximum(n, 1) / max_exact) / np.log(REL_MAX_DIST / max_exact)
                         * (N_BUCKETS - max_exact)).astype(np.int32)
    large = np.minimum(large, N_BUCKETS - 1)
    bucket = np.where(n < max_exact, n, large).astype(np.int32)
    valid = (dist >= 0) & (dist < W)
    onehot = (bucket[None] == np.arange(N_BUCKETS)[:, None, None]) & valid[None]
    return onehot.reshape(N_BUCKETS, W * 2 * W).astype(np.float32)


def _bias_expand(rel_bias_t, onehot, name, tn=4096):
    HQ, NB = rel_bias_t.shape
    L = onehot.shape[1]

    def body(r_ref, oh_ref, out_ref):
        out_ref[...] = _dot(r_ref[...], oh_ref[...].astype(F32), NN, precision=lax.Precision.HIGHEST)

    return _pcall(body, name=name, grid=(L // tn,),
                  in_specs=[pl.BlockSpec((HQ, NB), lambda i: (0, 0)), pl.BlockSpec((NB, tn), lambda i: (0, i))],
                  out_specs=pl.BlockSpec((HQ, tn), lambda i: (0, i)),
                  out_shape=jax.ShapeDtypeStruct((HQ, L), F32), compiler_params=_params(("parallel",)))(rel_bias_t, onehot)


def _bias_reduce(dbias, onehot, name, tk=4096):
    HQ, L = dbias.shape
    NB = onehot.shape[0]

    def body(d_ref, oh_ref, out_ref):
        part = _dot(d_ref[...], oh_ref[...].astype(F32), NT, precision=lax.Precision.HIGHEST)
        _accumulate(out_ref, part, pl.program_id(0) == 0)

    return _pcall(body, name=name, grid=(L // tk,),
                  in_specs=[pl.BlockSpec((HQ, tk), lambda i: (0, i)), pl.BlockSpec((NB, tk), lambda i: (0, i))],
                  out_specs=pl.BlockSpec((HQ, NB), lambda i: (0, 0)),
                  out_shape=jax.ShapeDtypeStruct((HQ, NB), F32), compiler_params=_params(("arbitrary",)))(dbias, onehot)


def _stacked_query_index(n_rows_or_cols_axis, shape):
    idx = lax.broadcasted_iota(jnp.int32, shape, n_rows_or_cols_axis)
    return jnp.where(idx >= WINDOW, idx - WINDOW, idx)


def _swa_fwd(qproj, kk, vv, bias_ab, sink_col, gq2, gk2, name):
    S, HQD = qproj.shape
    KVH = kk.shape[0]
    PP = HQD // LANES
    NP = PP // KVH
    W = WINDOW
    nb = S // W

    def body(q_ref, k_ref, v_ref, bias_ref, sink_ref, gq_ref, gk_ref, o_ref, lse_ref, qs_s, kb_s, vb_s):
        is_a = _lane_is_a()
        qn = _pair_norm(q_ref[...], gq_ref[...], is_a) * 0.125
        qs_s[0] = jnp.where(is_a, qn, 0.0).astype(BF16)
        qs_s[1] = jnp.where(is_a, 0.0, qn).astype(BF16)
        kb_s[...] = _pair_norm(k_ref[...], gk_ref[...], is_a).astype(BF16)
        vb_s[...] = v_ref[...].astype(BF16)
        sink = sink_ref[...]
        qi1 = _stacked_query_index(0, (2 * W, W))
        first_valid = lax.broadcasted_iota(jnp.int32, (2 * W, W), 1) <= qi1
        qi2 = _stacked_query_index(0, (2 * W, 2 * W))
        key2 = lax.broadcasted_iota(jnp.int32, (2 * W, 2 * W), 1)
        band_valid = (key2 > qi2) & (key2 <= qi2 + W)
        for n in range(nb):
            rows = slice(n * W, (n + 1) * W)
            keys = slice(0, W) if n == 0 else slice((n - 1) * W, (n + 1) * W)
            lhs = jnp.concatenate([qs_s[0, rows, :], qs_s[1, rows, :]], axis=0)
            s = _dot(lhs, kb_s[keys, :], NT) + (bias_ref[:, W:2 * W] if n == 0 else bias_ref[...])
            s = jnp.where(first_valid if n == 0 else band_valid, s, NEG)
            m = jnp.maximum(jnp.max(s, axis=-1, keepdims=True), sink)
            e = jnp.exp(s - m)
            l = jnp.sum(e, axis=-1, keepdims=True) + jnp.exp(sink - m)
            o_ab = _dot(e.astype(BF16), vb_s[keys, :], NN) / l
            o_ref[rows, :] = jnp.where(is_a, o_ab[0:W, :], o_ab[W:2 * W, :]).astype(BF16)
            lse_ref[n] = _as_row(m + jnp.log(l))

    qcols = pl.BlockSpec((S, LANES), lambda a, g: (0, a * NP + g))
    kvs = pl.BlockSpec((None, S, LANES), lambda a, g: (a, 0, 0))
    gain = pl.BlockSpec((1, LANES), lambda a, g: (0, 0))
    return _pcall(body, name=name, grid=(KVH, NP),
                  in_specs=[qcols, kvs, kvs, pl.BlockSpec((None, 2 * W, 2 * W), lambda a, g: (a * NP + g, 0, 0)),
                            pl.BlockSpec((None, 2 * W, 1), lambda a, g: (a * NP + g, 0, 0)), gain, gain],
                  out_specs=[qcols, pl.BlockSpec((None, nb, 1, 2 * W), lambda a, g: (a * NP + g, 0, 0, 0))],
                  out_shape=[jax.ShapeDtypeStruct((S, HQD), BF16), jax.ShapeDtypeStruct((PP, nb, 1, 2 * W), F32)],
                  scratch_shapes=[pltpu.VMEM((2, S, LANES), BF16), pltpu.VMEM((S, LANES), BF16), pltpu.VMEM((S, LANES), BF16)],
                  compiler_params=_params(("parallel", "parallel")))(qproj, kk, vv, bias_ab, sink_col, gq2, gk2)


def _swa_bwd(qproj, kk, vv, bias_t_ab, sink_row, gq2, gk2, lse_row, do, name):
    S, HQD = qproj.shape
    KVH = kk.shape[0]
    PP = HQD // LANES
    NP = PP // KVH
    W = WINDOW
    nb = S // W

    def body(q_ref, k_ref, v_ref, bias_ref, sink_ref, gq_ref, gk_ref, lr_ref, do_ref,
             dq_ref, dk_ref, dv_ref, db_ref, dsink_ref, dgq_ref, dgk_ref,
             qs_s, kb_s, kt_s, vb_s, dob_s, dq_s, dk_s, dv_s):
        a, g = pl.program_id(0), pl.program_id(1)
        is_a = _lane_is_a()
        qn = _pair_norm(q_ref[...], gq_ref[...], is_a) * 0.125
        qs_s[0] = jnp.where(is_a, qn, 0.0).astype(BF16)
        qs_s[1] = jnp.where(is_a, 0.0, qn).astype(BF16)
        kn = _pair_norm(k_ref[...], gk_ref[...], is_a)
        kb_s[...] = kn.astype(BF16)
        kt_s[...] = kn.T.astype(BF16)
        vb_s[...] = v_ref[...].astype(BF16)
        dov = do_ref[...]
        dob_s[0] = jnp.where(is_a, dov, 0.0).astype(BF16)
        dob_s[1] = jnp.where(is_a, 0.0, dov).astype(BF16)
        sink = sink_ref[...]

        @pl.when(g == 0)
        def _():
            dk_s[...] = jnp.zeros((S, LANES), F32)
            dv_s[...] = jnp.zeros((S, LANES), F32)

        qi1 = _stacked_query_index(1, (W, 2 * W))
        first_valid = lax.broadcasted_iota(jnp.int32, (W, 2 * W), 0) <= qi1
        qi2 = _stacked_query_index(1, (2 * W, 2 * W))
        key2 = lax.broadcasted_iota(jnp.int32, (2 * W, 2 * W), 0)
        band_valid = (key2 > qi2) & (key2 <= qi2 + W)
        head_rows = lax.broadcasted_iota(jnp.int32, (LANES, W), 0) < HEAD_DIM
        db = jnp.zeros((2 * W, 2 * W), F32)
        dsk = jnp.zeros((1, 2 * W), F32)
        pend_k = pend_v = None
        for n in range(nb):
            rows = slice(n * W, (n + 1) * W)
            keys = slice(0, W) if n == 0 else slice((n - 1) * W, (n + 1) * W)
            lhs_q = jnp.concatenate([qs_s[0, rows, :], qs_s[1, rows, :]], axis=0)
            lhs_do = jnp.concatenate([dob_s[0, rows, :], dob_s[1, rows, :]], axis=0)
            lse = lr_ref[n]
            s_t = _dot(kb_s[keys, :], lhs_q, NT) + (bias_ref[W:2 * W, :] if n == 0 else bias_ref[...])
            p_t = jnp.where(first_valid if n == 0 else band_valid, jnp.exp(s_t - lse), 0.0)
            dp_t = _dot(vb_s[keys, :], lhs_do, NT)
            delta = jnp.sum(p_t * dp_t, axis=0, keepdims=True)
            ds_t = p_t * (dp_t - delta)
            dsb = ds_t.astype(BF16)
            dsk = dsk - jnp.exp(sink - lse) * delta
            dv_band = _dot(p_t.astype(BF16), lhs_do, NN)
            dk_band = _dot(dsb, lhs_q, NN)
            dq_t = _dot(kt_s[:, keys], dsb, NN)
            dq_s[rows, :] = jnp.where(head_rows, dq_t[:, 0:W], dq_t[:, W:2 * W]).T
            if n == 0:
                db = jnp.concatenate([jnp.zeros((W, 2 * W), F32), ds_t], axis=0)
                pend_k, pend_v = dk_band, dv_band
            else:
                db = db + ds_t
                prev = slice((n - 1) * W, n * W)
                dk_s[prev, :] += pend_k + dk_band[0:W, :]
                dv_s[prev, :] += pend_v + dv_band[0:W, :]
                pend_k, pend_v = dk_band[W:2 * W, :], dv_band[W:2 * W, :]
        tail = slice((nb - 1) * W, nb * W)
        dk_s[tail, :] += pend_k
        dv_s[tail, :] += pend_v
        db_ref[...] = db
        dsink_ref[0] = jnp.broadcast_to(jnp.sum(dsk[:, 0:W], axis=1, keepdims=True), (1, LANES))
        dsink_ref[1] = jnp.broadcast_to(jnp.sum(dsk[:, W:2 * W], axis=1, keepdims=True), (1, LANES))
        dq_raw, dgq = _pair_norm_bwd(q_ref[...], gq_ref[...], dq_s[...] * 0.125, is_a)
        dq_ref[...] = dq_raw.astype(BF16)
        _accumulate(dgq_ref, dgq, jnp.logical_and(a == 0, g == 0))

        @pl.when(jnp.logical_and(a == KVH - 1, g == NP - 1))
        def _():
            _fold_row(dgq_ref)

        @pl.when(g == NP - 1)
        def _():
            dk_raw, dgk = _pair_norm_bwd(k_ref[...], gk_ref[...], _fold_heads(dk_s[...]), is_a)
            dk_ref[...] = dk_raw
            _accumulate(dgk_ref, dgk, a == 0)
            dv_ref[...] = _fold_heads(dv_s[...])

    qcols = pl.BlockSpec((S, LANES), lambda a, g: (0, a * NP + g))
    kvs = pl.BlockSpec((None, S, LANES), lambda a, g: (a, 0, 0))
    sq = pl.BlockSpec((None, 2 * W, 2 * W), lambda a, g: (a * NP + g, 0, 0))
    gain = pl.BlockSpec((1, LANES), lambda a, g: (0, 0))
    ks = jax.ShapeDtypeStruct((KVH, S, LANES), F32)
    gs = jax.ShapeDtypeStruct((1, LANES), F32)
    return _pcall(body, name=name, grid=(KVH, NP),
                  in_specs=[qcols, kvs, kvs, sq, pl.BlockSpec((None, 1, 2 * W), lambda a, g: (a * NP + g, 0, 0)), gain, gain,
                            pl.BlockSpec((None, nb, 1, 2 * W), lambda a, g: (a * NP + g, 0, 0, 0)), qcols],
                  out_specs=[qcols, kvs, kvs, sq, pl.BlockSpec((2, 1, LANES), lambda a, g: (a * NP + g, 0, 0)), gain, gain],
                  out_shape=[jax.ShapeDtypeStruct((S, HQD), BF16), ks, ks, jax.ShapeDtypeStruct((PP, 2 * W, 2 * W), F32),
                             jax.ShapeDtypeStruct((2 * PP, 1, LANES), F32), gs, gs],
                  scratch_shapes=[pltpu.VMEM((2, S, LANES), BF16), pltpu.VMEM((S, LANES), BF16), pltpu.VMEM((LANES, S), BF16),
                                  pltpu.VMEM((S, LANES), BF16), pltpu.VMEM((2, S, LANES), BF16)] + [pltpu.VMEM((S, LANES), F32)] * 3,
                  compiler_params=_params(("arbitrary", "arbitrary")))(qproj, kk, vv, bias_t_ab, sink_row, gq2, gk2, lse_row, do)


def _adamw_update(w, g, m, v):
    m2 = ADAM_B1 * m + (1.0 - ADAM_B1) * g
    v2 = ADAM_B2 * v + (1.0 - ADAM_B2) * jnp.square(g)
    m_hat = m2 / (1.0 - ADAM_B1 ** ADAM_STEP)
    v_hat = v2 / (1.0 - ADAM_B2 ** ADAM_STEP)
    return -ADAM_LR * (m_hat / (jnp.sqrt(v_hat) + ADAM_EPS) + ADAM_WD * w), m2, v2


def _adamw(w, g, m, v, name, tr=512, tc=256):
    R, C = w.shape
    tr = min(tr, R)
    if R % tr == 0:
        grid, blk = (R // tr,), pl.BlockSpec((tr, C), lambda i: (i, 0))
    else:
        assert C % tc == 0
        grid, blk = (C // tc,), pl.BlockSpec((R, tc), lambda i: (0, i))

    def body(w_ref, g_ref, m_ref, v_ref, d_ref, m2_ref, v2_ref):
        d_ref[...], m2_ref[...], v2_ref[...] = _adamw_update(w_ref[...], g_ref[...], m_ref[...], v_ref[...])

    return _pcall(body, name=name, grid=grid, in_specs=[blk] * 4, out_specs=[blk] * 3,
                  out_shape=[jax.ShapeDtypeStruct((R, C), F32)] * 3, compiler_params=_params(("parallel",)))(w, g, m, v)


def _adamw_two_layers(w, g0, g1, m, v, name, tr=512):
    R, C = g0.shape
    assert R % tr == 0 and w.shape == (2 * R, C)
    nr = R // tr

    def body(w_ref, g0_ref, g1_ref, m_ref, v_ref, g_ref, d_ref, m2_ref, v2_ref):
        g = jnp.where(pl.program_id(0) == 0, g0_ref[...], g1_ref[...])
        g_ref[...] = g
        d_ref[...], m2_ref[...], v2_ref[...] = _adamw_update(w_ref[...], g, m_ref[...], v_ref[...])

    both = pl.BlockSpec((tr, C), lambda l, i: (l * nr + i, 0))
    first = pl.BlockSpec((tr, C), lambda l, i: (i * (1 - l) + (nr - 1) * l, 0))
    second = pl.BlockSpec((tr, C), lambda l, i: (i * l, 0))
    return _pcall(body, name=name, grid=(2, nr), in_specs=[both, first, second, both, both], out_specs=[both] * 4,
                  out_shape=[jax.ShapeDtypeStruct((2 * R, C), F32)] * 4,
                  compiler_params=_params(("arbitrary", "arbitrary")))(w, g0, g1, m, v)


def _sum_core_pair(arr, got, place, name, tr=512):
    P, hr, C = got.shape
    tr = tr if hr % tr == 0 else hr
    nt = hr // tr

    def body(place_ref, a_ref, g_ref, o_ref):
        o_ref[...] = (a_ref[...].astype(F32) + g_ref[...].astype(F32)).astype(o_ref.dtype)

    spec = pltpu.PrefetchScalarGridSpec(
        num_scalar_prefetch=1, grid=(P, nt),
        in_specs=[pl.BlockSpec((None, tr, C), lambda j, i, pr: (j, pr[1] * nt + i, 0)),
                  pl.BlockSpec((None, tr, C), lambda j, i, pr: (j, i, 0))],
        out_specs=pl.BlockSpec((None, tr, C), lambda j, i, pr: (j, i, 0)))
    return _pcall(body, name=name, grid_spec=spec, out_shape=jax.ShapeDtypeStruct(got.shape, BF16),
                  compiler_params=_params(("parallel", "parallel")))(place, arr, got)


def _sum_chips(pair, landed, place, name, tr=256):
    _, R, C = landed.shape
    tr = tr if R % tr == 0 else R

    def body(place_ref, p_ref, l_ref, o_ref):
        acc = p_ref[...].astype(F32)
        for k in range(3):
            acc = acc + l_ref[k].astype(F32)
        o_ref[...] = acc

    spec = pltpu.PrefetchScalarGridSpec(
        num_scalar_prefetch=1, grid=(R // tr,),
        in_specs=[pl.BlockSpec((None, tr, C), lambda i, pr: (pr[0], i, 0)), pl.BlockSpec((3, tr, C), lambda i, pr: (0, i, 0))],
        out_specs=pl.BlockSpec((None, tr, C), lambda i, pr: (pr[1], i, 0)))
    return _pcall(body, name=name, grid_spec=spec, out_shape=jax.ShapeDtypeStruct((2, R, C), F32),
                  compiler_params=_params(("parallel",)))(place, pair, landed)


def _place():
    x, y, c = lax.axis_index("x"), lax.axis_index("y"), lax.axis_index("c")
    others = [(1 - x, y), (x, 1 - y), (1 - x, 1 - y)]
    return x, y, c, others


def _half_rows(ref, hh, lead=()):
    hr = ref.shape[-2] // 2
    return ref.at[(*lead, pl.ds(pl.multiple_of(hh * hr, 16), hr), slice(None))]


def _sem_arrays(*counts):
    return [pltpu.SemaphoreType.DMA((k,)) for k in counts]


SEM_SPEC = pl.BlockSpec(memory_space=pltpu.SEMAPHORE)
ANY_SPEC = pl.BlockSpec(memory_space=pl.ANY)
DATAFLOW = pltpu.SideEffectType.DATAFLOW_SIDE_EFFECTING


def _in_hbm(a):
    return pltpu.with_memory_space_constraint(a, pltpu.HBM)


def _gather_copies(srcs, lands, send_sems, recv_sems):
    x, y, c, others = _place()
    me = 2 * x + y

    def copy(w, k, dst_chip, to):
        return pltpu.make_async_remote_copy(src_ref=_half_rows(srcs[w], c), dst_ref=_half_rows(lands[w], c, (dst_chip,)),
                                            send_sem=send_sems.at[3 * w + k], recv_sem=recv_sems.at[3 * w + k],
                                            device_id=to, device_id_type=MESH)

    pairs = [(w, k, cx, cy) for w in range(len(srcs)) for k, (cx, cy) in enumerate(others)]
    return ([copy(w, k, me, (cx, cy, c)) for w, k, cx, cy in pairs],
            [copy(w, k, 2 * cx + cy, (cx, cy, c)) for w, k, cx, cy in pairs])


def _gather_start(shards, after, name):
    n = len(shards)

    def body(*refs):
        srcs, lands, send_sems, recv_sems, token = refs[:n], refs[n:2 * n], refs[2 * n + 1], refs[2 * n + 2], refs[-1]
        for cp in _gather_copies(srcs, lands, send_sems, recv_sems)[0]:
            cp.start()
        token[...] = jnp.zeros_like(token)

    lands = [lax.empty((N_CHIPS,) + s.shape, s.dtype) for s in shards]
    outs = _pcall(
        body, name=name, in_specs=[HBM_SPEC] * (2 * n) + [ANY_SPEC],
        out_specs=[SEM_SPEC, SEM_SPEC] + [HBM_SPEC] * (2 * n) + [VMEM_SPEC],
        out_shape=[pltpu.SemaphoreType.DMA((3 * n,)), pltpu.SemaphoreType.DMA((3 * n,))]
        + [pltpu.HBM(a.shape, a.dtype) for a in list(shards) + lands] + [jax.ShapeDtypeStruct((8, LANES), F32)],
        input_output_aliases={i: 2 + i for i in range(2 * n)},
        compiler_params=pltpu.CompilerParams(has_side_effects=DATAFLOW),
    )(*[_in_hbm(a) for a in list(shards) + lands], after)
    return outs[0], outs[1], outs[2:2 + n], outs[2 + n:2 + 2 * n], outs[-1]


def _gather_wait(started, after, name):
    send_sems, recv_sems, srcs, lands, _ = started
    n = len(srcs)

    def body(*refs):
        src_refs, land_refs, send_ref, recv_ref = refs[:n], refs[n:2 * n], refs[2 * n], refs[2 * n + 1]
        outgoing, incoming = _gather_copies(src_refs, land_refs, send_ref, recv_ref)
        for out_cp, in_cp in zip(outgoing, incoming):
            out_cp.wait_send()
            in_cp.wait_recv()

    outs = _pcall(
        body, name=name, in_specs=[HBM_SPEC] * (2 * n) + [SEM_SPEC, SEM_SPEC, ANY_SPEC], out_specs=[HBM_SPEC] * (2 * n),
        out_shape=[pltpu.HBM(a.shape, a.dtype) for a in list(srcs) + list(lands)],
        input_output_aliases={i: i for i in range(2 * n)},
        compiler_params=pltpu.CompilerParams(has_side_effects=DATAFLOW),
    )(*srcs, *lands, send_sems, recv_sems, after)
    return outs[:n], outs[n:]


def _gather_pass_on(shards, lands, name):
    n = len(shards)
    per = 4

    def body(*refs):
        srcs, bufs = refs[:n], refs[2 * n:3 * n]
        send_sems, recv_sems = refs[3 * n:]
        x, y, c, others = _place()
        me = 2 * x + y
        sibling = (x, y, 1 - c)

        def copy(w, k, src, dst):
            return pltpu.make_async_remote_copy(src_ref=src, dst_ref=dst, send_sem=send_sems.at[per * w + k],
                                                recv_sem=recv_sems.at[per * w + k], device_id=sibling, device_id_type=MESH)

        sends, recvs = [], []
        for w in range(n):
            for k, (cx, cy) in enumerate(others):
                mine, theirs = _half_rows(bufs[w], c, (2 * cx + cy,)), _half_rows(bufs[w], 1 - c, (2 * cx + cy,))
                sends.append(copy(w, k, mine, mine))
                recvs.append(copy(w, k, theirs, theirs))
            sends.append(copy(w, 3, srcs[w], bufs[w].at[me]))
            recvs.append(sends[-1])
        for cp in sends:
            cp.start()
        for snd, rcv in zip(sends, recvs):
            snd.wait_send()
            rcv.wait_recv()

    return _pcall(body, name=name, in_specs=[HBM_SPEC] * (2 * n), out_specs=[HBM_SPEC] * n,
                  out_shape=[jax.ShapeDtypeStruct(l.shape, l.dtype) for l in lands],
                  input_output_aliases={n + w: w for w in range(n)},
                  scratch_shapes=_sem_arrays(per * n, per * n))(*shards, *lands)


def _scatter_copies(srcs, lands, send_sems, recv_sems):
    x, y, c, others = _place()
    return [pltpu.make_async_remote_copy(src_ref=srcs[w].at[2 * cx + cy], dst_ref=lands[w].at[k],
                                         send_sem=send_sems.at[3 * w + k], recv_sem=recv_sems.at[3 * w + k],
                                         device_id=(cx, cy, c), device_id_type=MESH)
            for w in range(len(srcs)) for k, (cx, cy) in enumerate(others)]


def _scatter_start(parts, name):
    n = len(parts)

    def body(*refs):
        srcs, lands, send_sems, recv_sems, token = refs[:n], refs[n:2 * n], refs[2 * n], refs[2 * n + 1], refs[-1]
        for cp in _scatter_copies(srcs, lands, send_sems, recv_sems):
            cp.start()
        token[...] = jnp.zeros_like(token)

    lands = [lax.empty((3,) + p.shape[1:], p.dtype) for p in parts]
    outs = _pcall(
        body, name=name, in_specs=[HBM_SPEC] * (2 * n), out_specs=[SEM_SPEC, SEM_SPEC] + [HBM_SPEC] * (2 * n) + [VMEM_SPEC],
        out_shape=[pltpu.SemaphoreType.DMA((3 * n,)), pltpu.SemaphoreType.DMA((3 * n,))]
        + [pltpu.HBM(a.shape, a.dtype) for a in list(parts) + lands] + [jax.ShapeDtypeStruct((8, LANES), F32)],
        input_output_aliases={i: 2 + i for i in range(2 * n)},
        compiler_params=pltpu.CompilerParams(has_side_effects=DATAFLOW),
    )(*[_in_hbm(a) for a in list(parts) + lands])
    return (outs[0], outs[1], outs[2:2 + n], outs[2 + n:2 + 2 * n]), outs[-1]


def _scatter_wait(started, after, name):
    send_sems, recv_sems, srcs, lands = started
    n = len(srcs)

    def body(*refs):
        for cp in _scatter_copies(refs[:n], refs[n:2 * n], refs[2 * n], refs[2 * n + 1]):
            cp.wait_send()
            cp.wait_recv()

    outs = _pcall(
        body, name=name, in_specs=[HBM_SPEC] * (2 * n) + [SEM_SPEC, SEM_SPEC, ANY_SPEC], out_specs=[HBM_SPEC] * (2 * n),
        out_shape=[pltpu.HBM(a.shape, a.dtype) for a in list(srcs) + list(lands)],
        input_output_aliases={i: i for i in range(2 * n)},
        compiler_params=pltpu.CompilerParams(has_side_effects=DATAFLOW),
    )(*srcs, *lands, send_sems, recv_sems, after)
    return outs[:n], outs[n:]


def _split_start(plan, arrays, n_copies, after, name):
    n = len(arrays)

    def body(*refs):
        for cp in plan(refs[:n], refs[n + 1], refs[n + 2])[0]:
            cp.start()
        refs[-1][...] = jnp.zeros_like(refs[-1])

    outs = _pcall(
        body, name=name, in_specs=[HBM_SPEC] * n + [ANY_SPEC], out_specs=[SEM_SPEC, SEM_SPEC] + [HBM_SPEC] * n + [VMEM_SPEC],
        out_shape=[pltpu.SemaphoreType.DMA((n_copies,)), pltpu.SemaphoreType.DMA((n_copies,))]
        + [pltpu.HBM(a.shape, a.dtype) for a in arrays] + [jax.ShapeDtypeStruct((8, LANES), F32)],
        input_output_aliases={i: 2 + i for i in range(n)},
        compiler_params=pltpu.CompilerParams(has_side_effects=DATAFLOW),
    )(*[_in_hbm(a) for a in arrays], after)
    return (outs[0], outs[1], outs[2:2 + n]), outs[-1]


def _split_wait(plan, started, after, name):
    send_sems, recv_sems, arrays = started
    n = len(arrays)

    def body(*refs):
        outgoing, incoming = plan(refs[:n], refs[n], refs[n + 1])
        for cp in outgoing:
            cp.wait_send()
        for cp in incoming:
            cp.wait_recv()

    return _pcall(
        body, name=name, in_specs=[HBM_SPEC] * n + [SEM_SPEC, SEM_SPEC, ANY_SPEC], out_specs=[HBM_SPEC] * n,
        out_shape=[pltpu.HBM(a.shape, a.dtype) for a in arrays], input_output_aliases={i: i for i in range(n)},
        compiler_params=pltpu.CompilerParams(has_side_effects=DATAFLOW),
    )(*arrays, send_sems, recv_sems, after)


def _to_sibling(src, dst, k, send_sems, recv_sems):
    x, y, c, _ = _place()
    return pltpu.make_async_remote_copy(src_ref=src, dst_ref=dst, send_sem=send_sems.at[k], recv_sem=recv_sems.at[k],
                                        device_id=(x, y, 1 - c), device_id_type=MESH)


def _plan_pass_on(n):
    def plan(refs, send_sems, recv_sems):
        x, y, c, others = _place()
        cps = []
        for w in range(n):
            for k, (cx, cy) in enumerate(others):
                mine = _half_rows(refs[n + w], c, (2 * cx + cy,))
                cps.append(_to_sibling(mine, mine, 4 * w + k, send_sems, recv_sems))
            cps.append(_to_sibling(refs[w], refs[n + w].at[2 * x + y], 4 * w + 3, send_sems, recv_sems))
        return cps, cps
    return plan


def _plan_swap_halves(n):
    def plan(refs, send_sems, recv_sems):
        c = lax.axis_index("c")
        cps = [_to_sibling(_half_rows(refs[w], 1 - c, (slice(None),)), refs[n + w], w, send_sems, recv_sems) for w in range(n)]
        return cps, cps
    return plan


def _plan_share_halves(n):
    def plan(refs, send_sems, recv_sems):
        c = lax.axis_index("c")
        cps = [_to_sibling(refs[w].at[c], refs[w].at[c], w, send_sems, recv_sems) for w in range(n)]
        return cps, cps
    return plan


def _plan_gather_small(refs, send_sems, recv_sems):
    x, y, c, _ = _place()
    flips = [(dx, dy, dc) for dx in (0, 1) for dy in (0, 1) for dc in (0, 1)][1:]
    flip = lambda v, d: 1 - v if d else v
    cps = [pltpu.make_async_remote_copy(src_ref=refs[0], dst_ref=refs[1].at[4 * x + 2 * y + c], send_sem=send_sems.at[k],
                                        recv_sem=recv_sems.at[k], device_id=(flip(x, dx), flip(y, dy), flip(c, dc)),
                                        device_id_type=MESH)
           for k, (dx, dy, dc) in enumerate(flips)]
    return cps, cps


def _sum_gathered_small(gathered, own, place, name):
    _, M, C = gathered.shape

    def body(place_ref, g_ref, own_ref, o_ref):
        me = 2 * place_ref[0] + place_ref[1]
        acc = jnp.zeros((M, C), F32)
        for k in range(8):
            acc = acc + jnp.where(me == k, own_ref[...], g_ref[k])
        o_ref[...] = acc

    spec = pltpu.PrefetchScalarGridSpec(
        num_scalar_prefetch=1, grid=(1,),
        in_specs=[pl.BlockSpec((8, M, C), lambda i, pr: (0, 0, 0)), pl.BlockSpec((M, C), lambda i, pr: (0, 0))],
        out_specs=pl.BlockSpec((M, C), lambda i, pr: (0, 0)))
    return _pcall(body, name=name, grid_spec=spec, out_shape=jax.ShapeDtypeStruct((M, C), F32),
                  compiler_params=_params(("arbitrary",)))(place, gathered, own)


def _swap_halves_group(arrs, name):
    n = len(arrs)

    def body(*refs):
        ins, gots = refs[:n], refs[n:2 * n]
        send_sems, recv_sems = refs[2 * n:]
        x, y, c, _ = _place()
        swaps = [pltpu.make_async_remote_copy(src_ref=_half_rows(ins[w], 1 - c, (slice(None),)), dst_ref=gots[w],
                                              send_sem=send_sems.at[w], recv_sem=recv_sems.at[w],
                                              device_id=(x, y, 1 - c), device_id_type=MESH) for w in range(n)]
        for cp in swaps:
            cp.start()
        for cp in swaps:
            cp.wait()

    half_shapes = [jax.ShapeDtypeStruct((a.shape[0], a.shape[1] // 2, a.shape[2]), a.dtype) for a in arrs]
    return _pcall(body, name=name, in_specs=[HBM_SPEC] * n, out_specs=[HBM_SPEC] * n, out_shape=half_shapes,
                  scratch_shapes=_sem_arrays(n, n))(*arrs)


def _pack_rows(n_elems, width, align):
    rows = -(-n_elems // width)
    return -(-rows // align) * align


def _pack(arrays, dtype, width, align):
    flat = jnp.concatenate([a.astype(dtype).reshape(-1) for a in arrays])
    rows = _pack_rows(flat.shape[0], width, align)
    flat = jnp.pad(flat, (0, rows * width - flat.shape[0]))
    return flat.reshape(rows, width)


def _pack_small(arrays):
    return _pack(arrays, F32, width=128, align=8)


def _unpack(flat, shapes):
    out, off = [], 0
    for shp in shapes:
        n = int(np.prod(shp))
        out.append(flat[..., off:off + n].reshape(flat.shape[:-1] + tuple(shp)))
        off += n
    return out


def _doubled_heads(x2d, n_heads):
    S = x2d.shape[0]
    h = x2d.reshape(S, n_heads, HEAD_DIM).transpose(1, 0, 2)
    return jnp.concatenate([h, h], axis=-1)


def _rms_bwd_epilogue(dn, x, dres, g):
    r = _rinv(x)
    xh = x * r
    dxh = dn * g
    dx = dres + r * (dxh - xh * jnp.mean(dxh * xh, axis=-1, keepdims=True))
    return dx, dx, jnp.sum(dn * xh, axis=0, keepdims=True)


def _residual_then_norms(acc, res, *gains):
    h = res + acc
    hn = h * _rinv(h)
    return (h,) + tuple(hn * g for g in gains)


def _mlp_fwd(h, n, w_up4, w_down, next_gains, tag, between=None):
    u, a = _matmul(n, w_up4, "nn", f"up{tag}", out_dtypes=(F32, BF16), chipwise="b", tm=2048, tn=512,
                   epilogue=lambda acc: (acc, jnp.square(jnp.maximum(acc, 0.0))))
    if between is not None:
        next_gains = [next_gains[0] + between(a)] + list(next_gains[1:])
    assert w_down.shape[1] == 1024
    outs = _matmul(a, w_down, "nn", f"down{tag}", out_dtypes=(F32,) + (BF16,) * len(next_gains), extras=(h,),
                   row_extras=tuple(next_gains), epilogue=_residual_then_norms, tm=1024, tn=1024, tk=1024)
    outs = outs if next_gains else (outs,)
    return outs[0], outs[1:], (n, u, a)


def _mlp_bwd(dh_out, dh_out_b, h, g, w_up4, w_down, saved, tag):
    n, u, a = saved
    dw_down = _matmul(a, dh_out_b, "tn", f"dw_down{tag}", out_dtypes=(BF16,), tm=1024, tn=1024, chipwise="out_rows")
    du = _matmul(dh_out_b, w_down, "nt", f"du{tag}", out_dtypes=(BF16,), extras=(u,), tm=2048, tn=512,
                 epilogue=lambda acc, uu: (acc * (2.0 * jnp.maximum(uu, 0.0)),))
    dw_up = _matmul(n, du, "tn", f"dw_up{tag}", out_dtypes=(BF16,), chipwise="out", tm=1024, tn=512)
    dh, dh_b, dg = _matmul(du, w_up4, "nt", f"dn_mlp{tag}", out_dtypes=(F32, BF16, F32), extras=(h, dh_out), row_extras=(g,),
                           epilogue=_rms_bwd_epilogue, row_accums=1, tm=1024, tn=1024, tk=w_up4.shape[2], chipwise="b")
    return dh, dh_b, dg, dw_up, dw_down


def kernel(x, g_attn, g_mlp, w_in_a, b_f, gq_a, gk_a, w_out_a, g_kv, w_kv, gk_b, w_q_b, gq_b, sinks, rel_bias, w_out_b, w_up, w_down, loss_target, m_g_attn, m_g_mlp, m_w_in_a, m_b_f, m_gq_a, m_gk_a, m_w_out_a, m_g_kv, m_w_kv, m_gk_b, m_w_q_b, m_gq_b, m_sinks, m_rel_bias, m_w_out_b, m_w_up, m_w_down, v_g_attn, v_g_mlp, v_w_in_a, v_b_f, v_gq_a, v_gk_a, v_w_out_a, v_g_kv, v_w_kv, v_gk_b, v_w_q_b, v_gq_b, v_sinks, v_rel_bias, v_w_out_b, v_w_up, v_w_down):
    given = dict(locals())
    S, D = x.shape[1], x.shape[2]
    H = D // HEAD_DIM
    KVH = w_kv.shape[1] // (2 * HEAD_DIM)
    kvw = KVH * HEAD_DIM
    hw = H * HEAD_DIM
    W = WINDOW
    nb = S // W
    c_idx = lax.axis_index("c")
    xs, tgt = x[0], loss_target[0]

    n_in_shard = w_in_a.shape[2]
    rows_in = -(-n_in_shard // 32) * 32
    row_pad = lambda a: jnp.pad(a, [(0, 0)] * (a.ndim - 2) + [(0, rows_in - a.shape[-2]), (0, 0)])
    t_in = lambda a: jnp.swapaxes(a[0], 0, 1)
    shards = {"w_in_a": row_pad(t_in(w_in_a)), "w_out_a": w_out_a[0], "w_up0": w_up[0], "w_down0": w_down[0], "w_kv": w_kv,
              "w_q_b": w_q_b[0], "w_out_b": w_out_b[0], "w_up1": w_up[1], "w_down1": w_down[1]}
    parts = list(shards)
    groups = [("w_in_a", "w_out_a"), ("w_up0", "w_down0"), ("w_kv", "w_q_b", "w_out_b", "w_up1", "w_down1")]
    started = []
    for i, grp in enumerate(groups):
        behind = started[-1][4] if started else g_attn[0]
        started.append(_gather_start([shards[n].astype(BF16) for n in grp], behind, f"gather_start{i}"))
    gathered = {}

    def finish_gather(i, after):
        srcs, lands = _gather_wait(started[i], after, f"gather_wait{i}")
        gathered.update(zip(groups[i], _gather_pass_on(srcs, lands, f"gather_pass_on{i}")))

    def land_gather(i, after):
        srcs, lands = _gather_wait(started[i], after, f"gather_wait{i}")
        n = len(srcs)
        passing, token = _split_start(_plan_pass_on(n), list(srcs) + list(lands), 4 * n, after, f"pass_on_start{i}")
        return passing, token[0:1, 0:1]

    def finish_pass_on(i, passing, after):
        n = len(groups[i])
        gathered.update(zip(groups[i], _split_wait(_plan_pass_on(n), passing, after, f"pass_on_wait{i}")[n:]))

    vec = lambda a: a.reshape(1, -1)
    twice = lambda a: jnp.tile(a.reshape(1, -1), (1, 2))

    g_attn0 = vec(g_attn[0]) + sum(st[4][0, 0] for st in started)
    (n0,) = _rms_fwd(xs, [g_attn0], "rms_attn0")
    finish_gather(0, n0)
    win_t = gathered["w_in_a"][:, :n_in_shard].reshape(-1, D)
    win_t = jnp.pad(win_t, ((0, (-win_t.shape[0]) % 128), (0, 0)))
    wout_a = gathered["w_out_a"].reshape(-1, D)
    n_in = win_t.shape[0]
    tile_in = 640 if n_in % 640 == 0 else 128
    proj = _matmul(n0, win_t, "nt", "proj_in", tm=2048, tn=tile_in)
    zt = proj[:, 3 * hw:3 * hw + H].T
    c_row = _gate_fwd(zt, b_f.reshape(H, 1), "gate_fwd")
    c_row3 = c_row.reshape(H, 1, S)
    o_a, lse_a = _fox_fwd(proj, c_row3, twice(gq_a[0]), twice(gk_a[0]), H, "fox_fwd", t=512)
    passing1, tie = land_gather(1, o_a)
    h1, n1 = _matmul(o_a, wout_a, "nn", "out_a", out_dtypes=(F32, BF16), extras=(xs,), row_extras=(vec(g_mlp[0]) + tie,),
                     epilogue=_residual_then_norms, tm=1024, tn=1024)
    finish_pass_on(1, passing1, n1)
    wup = [gathered["w_up0"], None]
    wdown = [gathered["w_down0"].reshape(-1, D), None]
    passing = []

    def land_last_group(a):
        started_passing, tie = land_gather(2, a)
        passing.append(started_passing)
        return tie

    h2, (nkv, n2), mlp0 = _mlp_fwd(h1, n1, wup[0], wdown[0], [vec(g_kv), vec(g_attn[1])], "0", between=land_last_group)
    passing2 = passing[0]

    finish_pass_on(2, passing2, h2)
    wq_b, wout_b = gathered["w_q_b"].reshape(-1, D), gathered["w_out_b"].reshape(-1, D)
    wkv = gathered["w_kv"].reshape(D, -1)
    wup[1], wdown[1] = gathered["w_up1"], gathered["w_down1"].reshape(-1, D)
    kv = _matmul(nkv, wkv, "nn", "proj_kv", tm=2048)
    kk, vv = _doubled_heads(kv[:, :kvw], KVH), _doubled_heads(kv[:, kvw:], KVH)
    q2 = _matmul(n2, wq_b, "nn", "proj_q", tm=1024, tn=1024)
    onehot = jnp.asarray(_bucket_onehot(), dtype=BF16)
    bias = _bias_expand(rel_bias.T, onehot, "bias_expand").reshape(H, W, 2 * W)
    bias_ab = bias.reshape(H // 2, 2 * W, 2 * W)
    bias_t_ab = bias.reshape(H // 2, 2, W, 2 * W).transpose(0, 3, 1, 2).reshape(H // 2, 2 * W, 2 * W)
    sink_ab = jnp.repeat(sinks[0].reshape(H // 2, 2), W, axis=1)
    o_b, lse_b = _swa_fwd(q2, kk, vv, bias_ab, sink_ab.reshape(H // 2, 2 * W, 1), twice(gq_b[0]), twice(gk_b), "swa_fwd")
    h3, n3 = _matmul(o_b, wout_b, "nn", "out_b", out_dtypes=(F32, BF16), extras=(h2,), row_extras=(vec(g_mlp[1]),),
                     epilogue=_residual_then_norms, tm=1024, tn=1024)
    h4, _, mlp1 = _mlp_fwd(h3, n3, wup[1], wdown[1], [], "1")

    dh4, dh4_b, loss_part = _loss_head(h4, tgt, "loss_head")

    place = jnp.stack([2 * lax.axis_index("x") + lax.axis_index("y"), c_idx]).astype(jnp.int32)
    scattering = []

    def pair_and_scatter(names, mine, got):
        pair_sums = [_sum_core_pair(a, g, place, "sum_core_pair_" + n) for n, a, g in zip(names, mine, got)]
        started_scatter, token = _scatter_start(pair_sums, "scatter_start_" + names[0])
        scattering.append((names, started_scatter))
        return token[0:1, :]

    def start_reduce(named):
        names = list(named)
        mine = [named[n] for n in names]
        return pair_and_scatter(names, mine, _swap_halves_group(mine, "swap_grad_halves_" + names[0]))

    def start_swap(named):
        names = list(named)
        mine = [named[n] for n in names]
        lands = [lax.empty((a.shape[0], a.shape[1] // 2, a.shape[2]), a.dtype) for a in mine]
        swapping, token = _split_start(_plan_swap_halves(len(mine)), mine + lands, len(mine), mine[0], "swap_start_" + names[0])
        return names, swapping, token

    def finish_swap(swap, after):
        names, swapping, _ = swap
        n = len(names)
        arrays = _split_wait(_plan_swap_halves(n), swapping, after, "swap_wait_" + names[0])
        return pair_and_scatter(names, arrays[:n], arrays[n:])

    dh3, dh3_b, dg_mlp1, dw_up1, dw_down1 = _mlp_bwd(dh4, dh4_b, h3, vec(g_mlp[1]), wup[1], wdown[1], mlp1, "1")
    swap1 = start_swap({"w_down1": dw_down1, "w_up1": dw_up1})
    do_b = _matmul(dh3_b, wout_b, "nt", "do_b", tm=1024, tn=1024, after=swap1[2])
    dw_out_b = _matmul(o_b, dh3_b, "tn", "dw_out_b", out_dtypes=(BF16,), tm=1024, tn=1024, after=do_b)
    tie1 = finish_swap(swap1, dw_out_b)
    dq2, dk2, dv2, dbias_t_ab, dsink, dgq_b, dgk_b = _swa_bwd(
        q2, kk, vv, bias_t_ab, sink_ab.reshape(H // 2, 1, 2 * W), twice(gq_b[0]) + tie1, twice(gk_b),
        lse_b, do_b, "swa_bwd")
    dbias = dbias_t_ab.reshape(H // 2, 2 * W, 2, W).transpose(0, 2, 3, 1).reshape(H, W * 2 * W)
    d_rel_bias = _bias_reduce(dbias, onehot, "bias_reduce").T
    dw_q_b = _matmul(n2, dq2, "tn", "dw_q_b", out_dtypes=(BF16,), tm=1024, tn=1024)
    dn2 = _matmul(dq2, wq_b, "nt", "dn2", tm=1024, tn=1024)
    dkv = jnp.concatenate([dk2[h, :, :HEAD_DIM] for h in range(KVH)] + [dv2[h, :, :HEAD_DIM] for h in range(KVH)],
                          axis=1).astype(BF16)
    dw_kv = _matmul(nkv, dkv, "tn", "dw_kv", out_dtypes=(BF16,), tm=1024)
    dnkv = _matmul(dkv, wkv, "nt", "dnkv", tm=1024, tn=1024)
    tie2 = start_reduce({"w_out_b": dw_out_b.reshape(N_CHIPS, -1, D), "w_q_b": dw_q_b.reshape(N_CHIPS, -1, D),
                         "w_kv": dw_kv.reshape(N_CHIPS, -1, 2 * kvw)})
    dh2, dh2_b, (dg_kv, dg_attn1) = _rms_bwd(h2, dh3, [vec(g_kv) + tie2[:, :1], vec(g_attn[1])], [dnkv, dn2], "rms_attn1_bwd")

    dh1, dh1_b, dg_mlp0, dw_up0, dw_down0 = _mlp_bwd(dh2, dh2_b, h1, vec(g_mlp[0]), wup[0], wdown[0], mlp0, "0")
    swap3 = start_swap({"w_down0": dw_down0, "w_up0": dw_up0})
    do_a = _matmul(dh1_b, wout_a, "nt", "do_a", tm=1024, tn=1024, after=swap3[2])
    dw_out_a = _matmul(o_a, dh1_b, "tn", "dw_out_a", out_dtypes=(BF16,), tm=1024, tn=1024, after=do_a)
    tie3 = finish_swap(swap3, dw_out_a)
    dq_a, dk_a, dv_a, dc_row, dgq_a, dgk_a = _fox_bwd(
        proj, c_row3, twice(gq_a[0]) + tie3, twice(gk_a[0]), lse_a, do_a, H, "fox_bwd", t=512)
    dzt, db_f = _gate_bwd(dc_row.reshape(H, S), zt, b_f.reshape(H, 1), "gate_bwd")
    dproj = jnp.concatenate([dq_a, dk_a, dv_a, dzt.T.astype(BF16), jnp.zeros((S, n_in - 3 * hw - H), BF16)], axis=1)
    dw_in_t = _matmul(dproj, n0, "tn", "dw_in", out_dtypes=(BF16,), tm=tile_in, tn=1024)
    dw_in4 = row_pad(dw_in_t[:3 * hw + H].reshape(N_CHIPS, -1, D))
    tie4 = start_reduce({"w_out_a": dw_out_a.reshape(N_CHIPS, -1, D), "w_in_a": dw_in4})
    grad_x, _, dg_attn0 = _matmul(dproj, win_t, "nn", "dn0", out_dtypes=(F32, BF16, F32), extras=(xs, dh1),
                                  row_extras=(vec(g_attn[0]) + tie4[:, :1],), epilogue=_rms_bwd_epilogue, row_accums=1,
                                  tm=1024, tn=1024, tk=tile_in)

    small_grads = {
        "g_attn": jnp.concatenate([dg_attn0, dg_attn1], axis=0), "g_mlp": jnp.concatenate([dg_mlp0, dg_mlp1], axis=0),
        "b_f": db_f.reshape(1, H), "gq_a": dgq_a[:, :HEAD_DIM], "gk_a": dgk_a[:, :HEAD_DIM], "g_kv": dg_kv.reshape(-1),
        "gk_b": dgk_b[0, :HEAD_DIM], "gq_b": dgq_b[:, :HEAD_DIM], "sinks": dsink[:, 0, 0].reshape(1, H), "rel_bias": d_rel_bias,
    }
    small_shapes = [given[n].shape for n in SMALL] + [(1,)]
    spack = _pack_small([small_grads[n] for n in SMALL] + [loss_part])
    gathering_small, token = _split_start(_plan_gather_small, [spack, lax.empty((8,) + spack.shape, F32)], 7, grad_x,
                                          "gather_small_start")
    sharing = []
    for names, started_scatter in scattering:
        pair_sums, landed = _scatter_wait(started_scatter, grad_x, "scatter_wait_" + names[0])
        halves = [_sum_chips(p, l, place, "sum_chips_" + n) for n, p, l in zip(names, pair_sums, landed)]
        started_share, token = _split_start(_plan_share_halves(len(halves)), halves, len(halves), token, "share_start_" + names[0])
        sharing.append((names, started_share))
    own_small, others_small = _split_wait(_plan_gather_small, gathering_small, token, "gather_small_wait")
    small_sum = _sum_gathered_small(others_small, own_small, place, "sum_small")
    small_red = _unpack(small_sum.reshape(-1), small_shapes)
    reduced = {}
    for names, started_share in sharing:
        for n, r in zip(names, _split_wait(_plan_share_halves(len(names)), started_share, token, "share_wait_" + names[0])):
            reduced[n] = r.reshape(-1, r.shape[2])
    reduced["w_in_a"] = reduced["w_in_a"][:n_in_shard]
    loss = small_red[-1][0]

    grads = dict(zip(SMALL, small_red))
    no_loss = [jnp.zeros((1,), F32)]
    sw = _pack_small([given[n] for n in SMALL] + no_loss)
    sm = _pack_small([given["m_" + n] for n in SMALL] + no_loss)
    sv = _pack_small([given["v_" + n] for n in SMALL] + no_loss)
    sd, sm2, sv2 = _adamw(sw, small_sum, sm, sv, "adamw_small", tr=sw.shape[0])
    delta = dict(zip(SMALL, _unpack(sd.reshape(-1), small_shapes)))
    new_m = dict(zip(SMALL, _unpack(sm2.reshape(-1), small_shapes)))
    new_v = dict(zip(SMALL, _unpack(sv2.reshape(-1), small_shapes)))
    for n in ("w_out_a", "w_kv", "w_q_b", "w_out_b"):
        w = given[n]
        two_d = (-1, w.shape[-1])
        d, m2, v2 = _adamw(w.reshape(two_d), reduced[n], given["m_" + n].reshape(two_d), given["v_" + n].reshape(two_d),
                           "adamw_" + n)
        grads[n] = reduced[n].reshape(w.shape)
        delta[n], new_m[n], new_v[n] = d.reshape(w.shape), m2.reshape(w.shape), v2.reshape(w.shape)
    d, m2, v2 = _adamw(t_in(w_in_a), reduced["w_in_a"], t_in(m_w_in_a), t_in(v_w_in_a), "adamw_w_in_a")
    back = lambda a: jnp.swapaxes(a, 0, 1)[None]
    grads["w_in_a"], delta["w_in_a"], new_m["w_in_a"], new_v["w_in_a"] = back(reduced["w_in_a"]), back(d), back(m2), back(v2)
    for n in ("w_up", "w_down"):
        w = given[n]
        two_d = (-1, w.shape[-1])
        g, d, m2, v2 = _adamw_two_layers(w.reshape(two_d), reduced[n + "0"], reduced[n + "1"], given["m_" + n].reshape(two_d),
                                         given["v_" + n].reshape(two_d), "adamw_" + n)
        grads[n], delta[n], new_m[n], new_v[n] = g.reshape(w.shape), d.reshape(w.shape), m2.reshape(w.shape), v2.reshape(w.shape)

    order = ["g_attn", "g_mlp", "w_in_a", "b_f", "gq_a", "gk_a", "w_out_a", "g_kv", "w_kv", "gk_b", "w_q_b", "gq_b",
             "sinks", "rel_bias", "w_out_b", "w_up", "w_down"]
    return (loss, grad_x[None], *[grads[n] for n in order], *[delta[n] for n in order],
            *[new_m[n] for n in order], *[new_v[n] for n in order])
```

```python
import numpy as np
import jax
import jax.numpy as jnp
from jax import lax
from jax.experimental import pallas as pl
from jax.experimental.pallas import tpu as pltpu

F32 = jnp.float32
BF16 = jnp.bfloat16
MESH = pl.DeviceIdType.MESH

HEAD_DIM = 64
LANES = 128
WINDOW = 128
N_BUCKETS = 32
REL_MAX_DIST = 128
NORM_EPS = 1e-6
ADAM_LR = 0.001
ADAM_B1 = 0.9
ADAM_B2 = 0.999
ADAM_EPS = 1e-08
ADAM_WD = 0.01
ADAM_STEP = 10
NEG = -1e30
N_CHIPS = 4
VMEM_LIMIT = 56 * 1024 * 1024
HBM_SPEC = pl.BlockSpec(memory_space=pltpu.HBM)
VMEM_SPEC = pl.BlockSpec(memory_space=pltpu.VMEM)

SMALL = ("g_attn", "g_mlp", "b_f", "gq_a", "gk_a", "g_kv", "gk_b", "gq_b", "sinks", "rel_bias")


def _pcall(body, **kw):
    return pl.pallas_call(body, **kw)


def _params(sem=None):
    return pltpu.CompilerParams(dimension_semantics=sem, vmem_limit_bytes=VMEM_LIMIT)


def _rinv(x):
    return lax.rsqrt(jnp.mean(x * x, axis=-1, keepdims=True) + NORM_EPS)


def _dot(a, b, dims, precision=None):
    return lax.dot_general(a, b, (dims, ((), ())), precision=precision, preferred_element_type=F32)


NN = ((1,), (0,))
NT = ((1,), (1,))
TN = ((0,), (0,))


def _accumulate(ref, val, first):
    @pl.when(first)
    def _():
        ref[...] = val

    @pl.when(jnp.logical_not(first))
    def _():
        ref[...] += val


def _matmul(a, b, mode, name, out_dtypes=(F32,), extras=(), row_extras=(), epilogue=None, tm=512, tn=512, tk=None, chipwise=None,
            after=None, row_accums=0):
    if chipwise == "b":
        nc = b.shape[2]
        M, K = a.shape
        (K2, N) = (b.shape[1], N_CHIPS * nc) if mode == "nn" else (N_CHIPS * nc, b.shape[1])
    elif mode == "nn":
        (M, K), (K2, N) = a.shape, b.shape
    elif mode == "nt":
        (M, K), (N, K2) = a.shape, b.shape
    else:
        (K, M), (K2, N) = a.shape, b.shape
    assert K == K2, (a.shape, b.shape, mode)
    tm, tn = min(tm, M), min(tn, N)
    tk = K if tk is None else tk
    assert M % tm == 0 and N % tn == 0 and K % tk == 0, (M, N, K, tm, tn, tk)
    nk = K // tk
    dims = {"nn": NN, "nt": NT, "tn": TN}[mode]
    a_spec = pl.BlockSpec((tk, tm), lambda i, j, k: (k, i)) if mode == "tn" else pl.BlockSpec((tm, tk), lambda i, j, k: (i, k))
    b_spec = pl.BlockSpec((tn, tk), lambda i, j, k: (j, k)) if mode == "nt" else pl.BlockSpec((tk, tn), lambda i, j, k: (k, j))
    o_spec = pl.BlockSpec((tm, tn), lambda i, j, k: (i, j))
    out_shape = (M, N)
    if chipwise == "b" and mode == "nn":
        per = nc // tn
        assert tk == K and nc % tn == 0
        b_spec = pl.BlockSpec((None, tk, tn), lambda i, j, k: (j // per, 0, j % per))
    elif chipwise == "b":
        assert mode == "nt" and tk == nc
        b_spec = pl.BlockSpec((None, tn, tk), lambda i, j, k: (k, j, 0))
    elif chipwise == "out_rows":
        per = (M // N_CHIPS) // tm
        assert (M // N_CHIPS) % tm == 0 and not extras
        o_spec = pl.BlockSpec((None, tm, tn), lambda i, j, k: (i // per, i % per, j))
        out_shape = (N_CHIPS, M // N_CHIPS, N)
    elif chipwise == "out":
        per = (N // N_CHIPS) // tn
        assert (N // N_CHIPS) % tn == 0
        o_spec = pl.BlockSpec((None, tm, tn), lambda i, j, k: (j // per, i, j % per))
        out_shape = (N_CHIPS, M, N // N_CHIPS)
        assert not extras
    n_ex, n_rex, n_out = len(extras), len(row_extras), len(out_dtypes)
    assert not row_accums or tn == N
    tail = () if after is None else (after,)
    n_in = 2 + n_ex + n_rex + len(tail)

    def body(*refs):
        a_ref, b_ref = refs[0], refs[1]
        ex_refs = refs[2:2 + n_ex + n_rex]
        out_refs = refs[n_in:n_in + n_out]
        part = _dot(a_ref[...].astype(BF16), b_ref[...].astype(BF16), dims)

        def finish(acc):
            outs = (acc,) if epilogue is None else epilogue(acc, *[r[...] for r in ex_refs])
            for idx, (r, o) in enumerate(zip(out_refs, outs)):
                if idx >= n_out - row_accums:
                    _accumulate(r, o, pl.program_id(0) == 0)
                else:
                    r[...] = o.astype(r.dtype)

        if nk == 1:
            finish(part)
            return
        acc_ref = refs[n_in + n_out]
        k = pl.program_id(2)

        @pl.when(k == 0)
        def _():
            acc_ref[...] = part

        @pl.when(jnp.logical_and(k > 0, k < nk - 1))
        def _():
            acc_ref[...] += part

        @pl.when(k == nk - 1)
        def _():
            finish(acc_ref[...] + part)

    row_spec = pl.BlockSpec((1, tn), lambda i, j, k: (0, j))
    outs = _pcall(
        body, name=name, grid=(M // tm, N // tn, nk),
        in_specs=[a_spec, b_spec] + [o_spec] * n_ex + [row_spec] * n_rex + [pl.BlockSpec(memory_space=pl.ANY)] * len(tail),
        out_specs=[o_spec] * (n_out - row_accums) + [row_spec] * row_accums,
        out_shape=[jax.ShapeDtypeStruct(out_shape, dt) for dt in out_dtypes[:n_out - row_accums]]
        + [jax.ShapeDtypeStruct((1, N), F32)] * row_accums,
        scratch_shapes=[pltpu.VMEM((tm, tn), F32)] if nk > 1 else [],
        compiler_params=_params(("arbitrary",) * 3 if row_accums else ("parallel", "parallel", "arbitrary")),
    )(a, b, *extras, *row_extras, *tail)
    return outs[0] if n_out == 1 else outs


def _rms_fwd(x, gains, name, ts=256):
    S, D = x.shape
    ts = min(ts, S)
    n = len(gains)

    def body(*refs):
        x_ref, g_refs, o_refs = refs[0], refs[1:1 + n], refs[1 + n:]
        xv = x_ref[...]
        xh = xv * _rinv(xv)
        for g_ref, o_ref in zip(g_refs, o_refs):
            o_ref[...] = (xh * g_ref[...]).astype(BF16)

    row = pl.BlockSpec((ts, D), lambda i: (i, 0))
    vec = pl.BlockSpec((1, D), lambda i: (0, 0))
    return _pcall(body, name=name, grid=(S // ts,), in_specs=[row] + [vec] * n, out_specs=[row] * n,
                  out_shape=[jax.ShapeDtypeStruct((S, D), BF16)] * n, compiler_params=_params(("parallel",)))(x, *gains)


def _rms_bwd(x, dres, gains, dns, name, ts=256):
    S, D = x.shape
    ts = min(ts, S)
    n = len(gains)

    def body(*refs):
        x_ref, dres_ref = refs[0], refs[1]
        g_refs, dn_refs = refs[2:2 + n], refs[2 + n:2 + 2 * n]
        dx_ref, dxb_ref, dg_refs = refs[2 + 2 * n], refs[3 + 2 * n], refs[4 + 2 * n:]
        xv = x_ref[...]
        r = _rinv(xv)
        xh = xv * r
        dx = dres_ref[...]
        first = pl.program_id(0) == 0
        for g_ref, dn_ref, dg_ref in zip(g_refs, dn_refs, dg_refs):
            dn = dn_ref[...].astype(F32)
            _accumulate(dg_ref, jnp.sum(dn * xh, axis=0, keepdims=True), first)
            dxh = dn * g_ref[...]
            dx = dx + r * (dxh - xh * jnp.mean(dxh * xh, axis=-1, keepdims=True))
        dx_ref[...] = dx
        dxb_ref[...] = dx.astype(BF16)

    row = pl.BlockSpec((ts, D), lambda i: (i, 0))
    vec = pl.BlockSpec((1, D), lambda i: (0, 0))
    outs = _pcall(body, name=name, grid=(S // ts,), in_specs=[row, row] + [vec] * n + [row] * n,
                  out_specs=[row, row] + [vec] * n,
                  out_shape=[jax.ShapeDtypeStruct((S, D), F32), jax.ShapeDtypeStruct((S, D), BF16)]
                  + [jax.ShapeDtypeStruct((1, D), F32)] * n,
                  compiler_params=_params(("arbitrary",)))(x, dres, *gains, *dns)
    return outs[0], outs[1], outs[2:]


def _loss_head(h, tgt, name, ts=256):
    S, D = h.shape
    ts = min(ts, S)

    def body(h_ref, t_ref, dh_ref, dhb_ref, loss_ref):
        err = h_ref[...] - t_ref[...]
        dh = err * (1.0 / D)
        dh_ref[...] = dh
        dhb_ref[...] = dh.astype(BF16)
        part = 0.5 * jnp.sum(jnp.mean(err * err, axis=-1, keepdims=True), axis=0, keepdims=True)
        _accumulate(loss_ref, part, pl.program_id(0) == 0)

    row = pl.BlockSpec((ts, D), lambda i: (i, 0))
    return _pcall(body, name=name, grid=(S // ts,), in_specs=[row, row],
                  out_specs=[row, row, pl.BlockSpec((1, 1), lambda i: (0, 0))],
                  out_shape=[jax.ShapeDtypeStruct((S, D), F32), jax.ShapeDtypeStruct((S, D), BF16),
                             jax.ShapeDtypeStruct((1, 1), F32)],
                  compiler_params=_params(("arbitrary",)))(h, tgt)


def _gate_fwd(zt, bf, name):
    H, S = zt.shape
    nb = S // 128

    def body(z_ref, b_ref, c_ref):
        z = z_ref[...] + b_ref[...]
        lf = jnp.minimum(z, 0.0) - jnp.log(1.0 + jnp.exp(-jnp.abs(z)))
        upper = (lax.broadcasted_iota(jnp.int32, (128, 128), 0) <= lax.broadcasted_iota(jnp.int32, (128, 128), 1)).astype(F32)
        carry = jnp.zeros((H, 1), F32)
        for blk in range(nb):
            cs = _dot(lf[:, blk * 128:(blk + 1) * 128], upper, NN, precision=lax.Precision.HIGHEST) + carry
            c_ref[:, blk * 128:(blk + 1) * 128] = cs
            carry = cs[:, 127:128]

    return _pcall(body, name=name, in_specs=[VMEM_SPEC, VMEM_SPEC], out_specs=VMEM_SPEC,
                  out_shape=jax.ShapeDtypeStruct((H, S), F32))(zt, bf)


def _gate_bwd(dct, zt, bf, name):
    H, S = zt.shape
    nb = S // 128

    def body(dc_ref, z_ref, b_ref, dz_ref, db_ref):
        z = z_ref[...] + b_ref[...]
        e = jnp.exp(-jnp.abs(z))
        sig_neg = jnp.where(z >= 0, e, 1.0) / (1.0 + e)
        lower = (lax.broadcasted_iota(jnp.int32, (128, 128), 0) >= lax.broadcasted_iota(jnp.int32, (128, 128), 1)).astype(F32)
        dc = dc_ref[...]
        carry = jnp.zeros((H, 1), F32)
        db = jnp.zeros((H, 1), F32)
        for blk in reversed(range(nb)):
            sl = slice(blk * 128, (blk + 1) * 128)
            dlf = _dot(dc[:, sl], lower, NN, precision=lax.Precision.HIGHEST) + carry
            carry = dlf[:, 0:1]
            dz = dlf * sig_neg[:, sl]
            dz_ref[:, sl] = dz
            db = db + jnp.sum(dz, axis=1, keepdims=True)
        db_ref[...] = db

    return _pcall(body, name=name, in_specs=[VMEM_SPEC] * 3, out_specs=[VMEM_SPEC] * 2,
                  out_shape=[jax.ShapeDtypeStruct((H, S), F32), jax.ShapeDtypeStruct((H, 1), F32)])(dct, zt, bf)


def _lane_is_a():
    return lax.broadcasted_iota(jnp.int32, (1, LANES), 1) < HEAD_DIM


def _per_head_mean(x, is_a):
    sa = jnp.sum(jnp.where(is_a, x, 0.0), axis=-1, keepdims=True)
    sb = jnp.sum(jnp.where(is_a, 0.0, x), axis=-1, keepdims=True)
    return jnp.where(is_a, sa, sb) / HEAD_DIM


def _pair_norm(raw, gain, is_a):
    return raw * lax.rsqrt(_per_head_mean(raw * raw, is_a) + NORM_EPS) * gain


def _pair_norm_bwd(raw, gain, dnormed, is_a):
    r = lax.rsqrt(_per_head_mean(raw * raw, is_a) + NORM_EPS)
    xh = raw * r
    dgain = jnp.sum(dnormed * xh, axis=0, keepdims=True)
    dxh = dnormed * gain
    return r * (dxh - xh * _per_head_mean(dxh * xh, is_a)), dgain


def _fold_heads(x):
    i = lax.broadcasted_iota(jnp.int32, (LANES, LANES), 0)
    j = lax.broadcasted_iota(jnp.int32, (LANES, LANES), 1)
    fold = ((i == j) | (i == j + HEAD_DIM) | (i + HEAD_DIM == j)).astype(F32)
    return _dot(x, fold, NN, precision=lax.Precision.HIGHEST)


def _fold_row(ref):
    ref[...] = _fold_heads(jnp.broadcast_to(ref[...], (8, LANES)))[0:1, :]


def _as_col(row):
    return jnp.broadcast_to(row, (LANES, row.shape[1])).T[:, 0:1]


def _as_row(col):
    return jnp.broadcast_to(col, (col.shape[0], LANES)).T[0:1, :]


def _tri_mask(t, keys_on_rows):
    r = lax.broadcasted_iota(jnp.int32, (t, t), 0)
    c = lax.broadcasted_iota(jnp.int32, (t, t), 1)
    return (r <= c) if keys_on_rows else (r >= c)


def _fox_fwd(proj, c_row, gq2, gk2, n_heads, name, t=256):
    S = proj.shape[0]
    H = n_heads
    P = H // 2
    t = min(t, S)
    nq = S // t

    def body(q_ref, k_ref, v_ref, cr_ref, gq_ref, gk_ref, o_ref, lse_ref, qs_s, kb_s, vb_s):
        is_a = _lane_is_a()
        qn = _pair_norm(q_ref[...], gq_ref[...], is_a) * 0.125
        qs_s[0] = jnp.where(is_a, qn, 0.0).astype(BF16)
        qs_s[1] = jnp.where(is_a, 0.0, qn).astype(BF16)
        kb_s[...] = _pair_norm(k_ref[...], gk_ref[...], is_a).astype(BF16)
        vb_s[...] = v_ref[...].astype(BF16)
        causal = _tri_mask(t, False)
        for i in range(nq):
            t0 = i * t
            rows = slice(t0, t0 + t)
            o_pair = None
            for a in range(2):
                qi = qs_s[a, rows, :]
                ci = _as_col(cr_ref[a, :, rows])
                s_d = jnp.where(causal, _dot(qi, kb_s[rows, :], NT) + ci - cr_ref[a, :, rows], NEG)
                m = jnp.max(s_d, axis=-1, keepdims=True)
                if i > 0:
                    s_l = _dot(qi, kb_s[0:t0, :], NT) + ci - cr_ref[a, :, 0:t0]
                    m = jnp.maximum(m, jnp.max(s_l, axis=-1, keepdims=True))
                p_d = jnp.exp(s_d - m)
                l = jnp.sum(p_d, axis=-1, keepdims=True)
                acc = _dot(p_d.astype(BF16), vb_s[rows, :], NN)
                if i > 0:
                    p_l = jnp.exp(s_l - m)
                    l = l + jnp.sum(p_l, axis=-1, keepdims=True)
                    acc = acc + _dot(p_l.astype(BF16), vb_s[0:t0, :], NN)
                o_a = acc / l
                lse_ref[a, :, rows] = _as_row(m + jnp.log(l))
                o_pair = o_a if a == 0 else jnp.where(is_a, o_pair, o_a)
            o_ref[rows, :] = o_pair.astype(BF16)

    def cols(off):
        return pl.BlockSpec((S, LANES), lambda p: (0, off + p))

    rowv = pl.BlockSpec((2, 1, S), lambda p: (p, 0, 0))
    gain = pl.BlockSpec((1, LANES), lambda p: (0, 0))
    return _pcall(body, name=name, grid=(P,), in_specs=[cols(0), cols(P), cols(2 * P), rowv, gain, gain],
                  out_specs=[cols(0), rowv],
                  out_shape=[jax.ShapeDtypeStruct((S, H * HEAD_DIM), BF16), jax.ShapeDtypeStruct((H, 1, S), F32)],
                  scratch_shapes=[pltpu.VMEM((2, S, LANES), BF16), pltpu.VMEM((S, LANES), BF16), pltpu.VMEM((S, LANES), BF16)],
                  compiler_params=_params(("parallel",)))(proj, proj, proj, c_row, gq2, gk2)


def _fox_bwd(proj, c_row, gq2, gk2, lse_row, do, n_heads, name, t=256):
    S = proj.shape[0]
    H = n_heads
    P = H // 2
    t = min(t, S)
    nq = S // t
    assert t % LANES == 0

    def body(q_ref, k_ref, v_ref, cr_ref, gq_ref, gk_ref, lr_ref, do_ref,
             dq_ref, dk_ref, dv_ref, dc_ref, dgq_ref, dgk_ref,
             qs_s, kb_s, kt_s, vb_s, dob_s, dq_s, dk_s, dv_s, dcs_s, cc_s):
        is_a = _lane_is_a()
        for a in range(2):
            for i in range(nq):
                cc_s[a, i * t:(i + 1) * t, :] = _as_col(cr_ref[a, :, i * t:(i + 1) * t])
        qn = _pair_norm(q_ref[...], gq_ref[...], is_a) * 0.125
        qs_s[0] = jnp.where(is_a, qn, 0.0).astype(BF16)
        qs_s[1] = jnp.where(is_a, 0.0, qn).astype(BF16)
        kn = _pair_norm(k_ref[...], gk_ref[...], is_a)
        kb_s[...] = kn.astype(BF16)
        kt_s[0] = jnp.where(is_a, kn, 0.0).T.astype(BF16)
        kt_s[1] = jnp.where(is_a, 0.0, kn).T.astype(BF16)
        vb_s[...] = v_ref[...].astype(BF16)
        dov = do_ref[...]
        dob_s[0] = jnp.where(is_a, dov, 0.0).astype(BF16)
        dob_s[1] = jnp.where(is_a, 0.0, dov).astype(BF16)
        dk_s[...] = jnp.zeros((S, LANES), F32)
        dv_s[...] = jnp.zeros((S, LANES), F32)
        dcs_s[...] = jnp.zeros((2, S, LANES), F32)
        causal = _tri_mask(t, True)
        for i in range(nq):
            t0 = i * t
            rows = slice(t0, t0 + t)
            dq_t = jnp.zeros((LANES, t), F32)
            for a in range(2):
                qi = qs_s[a, rows, :]
                doi = dob_s[a, rows, :]
                cri = cr_ref[a, :, rows]
                lri = lr_ref[a, :, rows]

                def probs(keys, masked, a=a, qi=qi, doi=doi, cri=cri, lri=lri):
                    p_t = jnp.exp(_dot(kb_s[keys, :], qi, NT) + cri - cc_s[a, keys, :] - lri)
                    if masked:
                        p_t = jnp.where(causal, p_t, 0.0)
                    return p_t, _dot(vb_s[keys, :], doi, NT)

                parts = [(rows,) + probs(rows, True)]
                if i > 0:
                    parts.append((slice(0, t0),) + probs(slice(0, t0), False))
                delta = sum(jnp.sum(p_t * dp_t, axis=0, keepdims=True) for _, p_t, dp_t in parts)
                for keys, p_t, dp_t in parts:
                    ds_t = p_t * (dp_t - delta)
                    dsb = ds_t.astype(BF16)
                    dv_s[keys, :] += _dot(p_t.astype(BF16), doi, NN)
                    dk_s[keys, :] += _dot(dsb, qi, NN)
                    dq_t = dq_t + _dot(kt_s[a, :, keys], dsb, NN)
                    dcs_s[a, keys, :] += sum(ds_t[:, b * LANES:(b + 1) * LANES] for b in range(t // LANES))
            dq_s[rows, :] = dq_t.T
        first = pl.program_id(0) == 0
        last = pl.program_id(0) == P - 1
        dq_raw, dgq = _pair_norm_bwd(q_ref[...], gq_ref[...], dq_s[...] * 0.125, is_a)
        dq_ref[...] = dq_raw.astype(BF16)
        _accumulate(dgq_ref, dgq, first)
        dk_raw, dgk = _pair_norm_bwd(k_ref[...], gk_ref[...], dk_s[...], is_a)
        dk_ref[...] = dk_raw.astype(BF16)
        _accumulate(dgk_ref, dgk, first)
        dv_ref[...] = dv_s[...].astype(BF16)
        for a in range(2):
            for i in range(nq):
                rows = slice(i * t, (i + 1) * t)
                dc_ref[a, :, rows] = _as_row(-jnp.sum(dcs_s[a, rows, :], axis=1, keepdims=True))

        @pl.when(last)
        def _():
            _fold_row(dgq_ref)
            _fold_row(dgk_ref)

    def cols(off):
        return pl.BlockSpec((S, LANES), lambda p: (0, off + p))

    rowv = pl.BlockSpec((2, 1, S), lambda p: (p, 0, 0))
    gain = pl.BlockSpec((1, LANES), lambda p: (0, 0))
    wide = jax.ShapeDtypeStruct((S, H * HEAD_DIM), BF16)
    gs = jax.ShapeDtypeStruct((1, LANES), F32)
    return _pcall(body, name=name, grid=(P,),
                  in_specs=[cols(0), cols(P), cols(2 * P), rowv, gain, gain, rowv, cols(0)],
                  out_specs=[cols(0), cols(0), cols(0), rowv, gain, gain],
                  out_shape=[wide, wide, wide, jax.ShapeDtypeStruct((H, 1, S), F32), gs, gs],
                  scratch_shapes=[pltpu.VMEM((2, S, LANES), BF16), pltpu.VMEM((S, LANES), BF16), pltpu.VMEM((2, LANES, S), BF16),
                                  pltpu.VMEM((S, LANES), BF16), pltpu.VMEM((2, S, LANES), BF16)]
                  + [pltpu.VMEM((S, LANES), F32)] * 3 + [pltpu.VMEM((2, S, LANES), F32), pltpu.VMEM((2, S, 1), F32)],
                  compiler_params=_params(("arbitrary",)))(proj, proj, proj, c_row, gq2, gk2, lse_row, do)


def _bucket_onehot():
    W = WINDOW
    dist = np.arange(W)[:, None] + W - np.arange(2 * W)[None, :]
    n = np.maximum(dist, 0)
    max_exact = N_BUCKETS // 2
    large = max_exact + (np.log(np.maximum(n, 1) / max_exact) / np.log(REL_MAX_DIST / max_exact)
                         * (N_BUCKETS - max_exact)).astype(np.int32)
    large = np.minimum(large, N_BUCKETS - 1)
    bucket = np.where(n < max_exact, n, large).astype(np.int32)
    valid = (dist >= 0) & (dist < W)
    onehot = (bucket[None] == np.arange(N_BUCKETS)[:, None, None]) & valid[None]
    return onehot.reshape(N_BUCKETS, W * 2 * W).astype(np.float32)


def _bias_expand(rel_bias_t, onehot, name, tn=4096):
    HQ, NB = rel_bias_t.shape
    L = onehot.shape[1]

    def body(r_ref, oh_ref, out_ref):
        out_ref[...] = _dot(r_ref[...], oh_ref[...].astype(F32), NN, precision=lax.Precision.HIGHEST)

    return _pcall(body, name=name, grid=(L // tn,),
                  in_specs=[pl.BlockSpec((HQ, NB), lambda i: (0, 0)), pl.BlockSpec((NB, tn), lambda i: (0, i))],
                  out_specs=pl.BlockSpec((HQ, tn), lambda i: (0, i)),
                  out_shape=jax.ShapeDtypeStruct((HQ, L), F32), compiler_params=_params(("parallel",)))(rel_bias_t, onehot)


def _bias_reduce(dbias, onehot, name, tk=4096):
    HQ, L = dbias.shape
    NB = onehot.shape[0]

    def body(d_ref, oh_ref, out_ref):
        part = _dot(d_ref[...], oh_ref[...].astype(F32), NT, precision=lax.Precision.HIGHEST)
        _accumulate(out_ref, part, pl.program_id(0) == 0)

    return _pcall(body, name=name, grid=(L // tk,),
                  in_specs=[pl.BlockSpec((HQ, tk), lambda i: (0, i)), pl.BlockSpec((NB, tk), lambda i: (0, i))],
                  out_specs=pl.BlockSpec((HQ, NB), lambda i: (0, 0)),
                  out_shape=jax.ShapeDtypeStruct((HQ, NB), F32), compiler_params=_params(("arbitrary",)))(dbias, onehot)


def _stacked_query_index(n_rows_or_cols_axis, shape):
    idx = lax.broadcasted_iota(jnp.int32, shape, n_rows_or_cols_axis)
    return jnp.where(idx >= WINDOW, idx - WINDOW, idx)


def _swa_fwd(qproj, kk, vv, bias_ab, sink_col, gq2, gk2, name):
    S, HQD = qproj.shape
    KVH = kk.shape[0]
    PP = HQD // LANES
    NP = PP // KVH
    W = WINDOW
    nb = S // W

    def body(q_ref, k_ref, v_ref, bias_ref, sink_ref, gq_ref, gk_ref, o_ref, lse_ref, qs_s, kb_s, vb_s):
        is_a = _lane_is_a()
        qn = _pair_norm(q_ref[...], gq_ref[...], is_a) * 0.125
        qs_s[0] = jnp.where(is_a, qn, 0.0).astype(BF16)
        qs_s[1] = jnp.where(is_a, 0.0, qn).astype(BF16)
        kb_s[...] = _pair_norm(k_ref[...], gk_ref[...], is_a).astype(BF16)
        vb_s[...] = v_ref[...].astype(BF16)
        sink = sink_ref[...]
        qi1 = _stacked_query_index(0, (2 * W, W))
        first_valid = lax.broadcasted_iota(jnp.int32, (2 * W, W), 1) <= qi1
        qi2 = _stacked_query_index(0, (2 * W, 2 * W))
        key2 = lax.broadcasted_iota(jnp.int32, (2 * W, 2 * W), 1)
        band_valid = (key2 > qi2) & (key2 <= qi2 + W)
        for n in range(nb):
            rows = slice(n * W, (n + 1) * W)
            keys = slice(0, W) if n == 0 else slice((n - 1) * W, (n + 1) * W)
            lhs = jnp.concatenate([qs_s[0, rows, :], qs_s[1, rows, :]], axis=0)
            s = _dot(lhs, kb_s[keys, :], NT) + (bias_ref[:, W:2 * W] if n == 0 else bias_ref[...])
            s = jnp.where(first_valid if n == 0 else band_valid, s, NEG)
            m = jnp.maximum(jnp.max(s, axis=-1, keepdims=True), sink)
            e = jnp.exp(s - m)
            l = jnp.sum(e, axis=-1, keepdims=True) + jnp.exp(sink - m)
            o_ab = _dot(e.astype(BF16), vb_s[keys, :], NN) / l
            o_ref[rows, :] = jnp.where(is_a, o_ab[0:W, :], o_ab[W:2 * W, :]).astype(BF16)
            lse_ref[n] = _as_row(m + jnp.log(l))

    qcols = pl.BlockSpec((S, LANES), lambda a, g: (0, a * NP + g))
    kvs = pl.BlockSpec((None, S, LANES), lambda a, g: (a, 0, 0))
    gain = pl.BlockSpec((1, LANES), lambda a, g: (0, 0))
    return _pcall(body, name=name, grid=(KVH, NP),
                  in_specs=[qcols, kvs, kvs, pl.BlockSpec((None, 2 * W, 2 * W), lambda a, g: (a * NP + g, 0, 0)),
                            pl.BlockSpec((None, 2 * W, 1), lambda a, g: (a * NP + g, 0, 0)), gain, gain],
                  out_specs=[qcols, pl.BlockSpec((None, nb, 1, 2 * W), lambda a, g: (a * NP + g, 0, 0, 0))],
                  out_shape=[jax.ShapeDtypeStruct((S, HQD), BF16), jax.ShapeDtypeStruct((PP, nb, 1, 2 * W), F32)],
                  scratch_shapes=[pltpu.VMEM((2, S, LANES), BF16), pltpu.VMEM((S, LANES), BF16), pltpu.VMEM((S, LANES), BF16)],
                  compiler_params=_params(("parallel", "parallel")))(qproj, kk, vv, bias_ab, sink_col, gq2, gk2)


def _swa_bwd(qproj, kk, vv, bias_t_ab, sink_row, gq2, gk2, lse_row, do, name):
    S, HQD = qproj.shape
    KVH = kk.shape[0]
    PP = HQD // LANES
    NP = PP // KVH
    W = WINDOW
    nb = S // W

    def body(q_ref, k_ref, v_ref, bias_ref, sink_ref, gq_ref, gk_ref, lr_ref, do_ref,
             dq_ref, dk_ref, dv_ref, db_ref, dsink_ref, dgq_ref, dgk_ref,
             qs_s, kb_s, kt_s, vb_s, dob_s, dq_s, dk_s, dv_s):
        a, g = pl.program_id(0), pl.program_id(1)
        is_a = _lane_is_a()
        qn = _pair_norm(q_ref[...], gq_ref[...], is_a) * 0.125
        qs_s[0] = jnp.where(is_a, qn, 0.0).astype(BF16)
        qs_s[1] = jnp.where(is_a, 0.0, qn).astype(BF16)
        kn = _pair_norm(k_ref[...], gk_ref[...], is_a)
        kb_s[...] = kn.astype(BF16)
        kt_s[...] = kn.T.astype(BF16)
        vb_s[...] = v_ref[...].astype(BF16)
        dov = do_ref[...]
        dob_s[0] = jnp.where(is_a, dov, 0.0).astype(BF16)
        dob_s[1] = jnp.where(is_a, 0.0, dov).astype(BF16)
        sink = sink_ref[...]

        @pl.when(g == 0)
        def _():
            dk_s[...] = jnp.zeros((S, LANES), F32)
            dv_s[...] = jnp.zeros((S, LANES), F32)

        qi1 = _stacked_query_index(1, (W, 2 * W))
        first_valid = lax.broadcasted_iota(jnp.int32, (W, 2 * W), 0) <= qi1
        qi2 = _stacked_query_index(1, (2 * W, 2 * W))
        key2 = lax.broadcasted_iota(jnp.int32, (2 * W, 2 * W), 0)
        band_valid = (key2 > qi2) & (key2 <= qi2 + W)
        head_rows = lax.broadcasted_iota(jnp.int32, (LANES, W), 0) < HEAD_DIM
        db = jnp.zeros((2 * W, 2 * W), F32)
        dsk = jnp.zeros((1, 2 * W), F32)
        pend_k = pend_v = None
        for n in range(nb):
            rows = slice(n * W, (n + 1) * W)
            keys = slice(0, W) if n == 0 else slice((n - 1) * W, (n + 1) * W)
            lhs_q = jnp.concatenate([qs_s[0, rows, :], qs_s[1, rows, :]], axis=0)
            lhs_do = jnp.concatenate([dob_s[0, rows, :], dob_s[1, rows, :]], axis=0)
            lse = lr_ref[n]
            s_t = _dot(kb_s[keys, :], lhs_q, NT) + (bias_ref[W:2 * W, :] if n == 0 else bias_ref[...])
            p_t = jnp.where(first_valid if n == 0 else band_valid, jnp.exp(s_t - lse), 0.0)
            dp_t = _dot(vb_s[keys, :], lhs_do, NT)
            delta = jnp.sum(p_t * dp_t, axis=0, keepdims=True)
            ds_t = p_t * (dp_t - delta)
            dsb = ds_t.astype(BF16)
            dsk = dsk - jnp.exp(sink - lse) * delta
            dv_band = _dot(p_t.astype(BF16), lhs_do, NN)
            dk_band = _dot(dsb, lhs_q, NN)
            dq_t = _dot(kt_s[:, keys], dsb, NN)
            dq_s[rows, :] = jnp.where(head_rows, dq_t[:, 0:W], dq_t[:, W:2 * W]).T
            if n == 0:
                db = jnp.concatenate([jnp.zeros((W, 2 * W), F32), ds_t], axis=0)
                pend_k, pend_v = dk_band, dv_band
            else:
                db = db + ds_t
                prev = slice((n - 1) * W, n * W)
                dk_s[prev, :] += pend_k + dk_band[0:W, :]
                dv_s[prev, :] += pend_v + dv_band[0:W, :]
                pend_k, pend_v = dk_band[W:2 * W, :], dv_band[W:2 * W, :]
        tail = slice((nb - 1) * W, nb * W)
        dk_s[tail, :] += pend_k
        dv_s[tail, :] += pend_v
        db_ref[...] = db
        dsink_ref[0] = jnp.broadcast_to(jnp.sum(dsk[:, 0:W], axis=1, keepdims=True), (1, LANES))
        dsink_ref[1] = jnp.broadcast_to(jnp.sum(dsk[:, W:2 * W], axis=1, keepdims=True), (1, LANES))
        dq_raw, dgq = _pair_norm_bwd(q_ref[...], gq_ref[...], dq_s[...] * 0.125, is_a)
        dq_ref[...] = dq_raw.astype(BF16)
        _accumulate(dgq_ref, dgq, jnp.logical_and(a == 0, g == 0))

        @pl.when(jnp.logical_and(a == KVH - 1, g == NP - 1))
        def _():
            _fold_row(dgq_ref)

        @pl.when(g == NP - 1)
        def _():
            dk_raw, dgk = _pair_norm_bwd(k_ref[...], gk_ref[...], _fold_heads(dk_s[...]), is_a)
            dk_ref[...] = dk_raw
            _accumulate(dgk_ref, dgk, a == 0)
            dv_ref[...] = _fold_heads(dv_s[...])

    qcols = pl.BlockSpec((S, LANES), lambda a, g: (0, a * NP + g))
    kvs = pl.BlockSpec((None, S, LANES), lambda a, g: (a, 0, 0))
    sq = pl.BlockSpec((None, 2 * W, 2 * W), lambda a, g: (a * NP + g, 0, 0))
    gain = pl.BlockSpec((1, LANES), lambda a, g: (0, 0))
    ks = jax.ShapeDtypeStruct((KVH, S, LANES), F32)
    gs = jax.ShapeDtypeStruct((1, LANES), F32)
    return _pcall(body, name=name, grid=(KVH, NP),
                  in_specs=[qcols, kvs, kvs, sq, pl.BlockSpec((None, 1, 2 * W), lambda a, g: (a * NP + g, 0, 0)), gain, gain,
                            pl.BlockSpec((None, nb, 1, 2 * W), lambda a, g: (a * NP + g, 0, 0, 0)), qcols],
                  out_specs=[qcols, kvs, kvs, sq, pl.BlockSpec((2, 1, LANES), lambda a, g: (a * NP + g, 0, 0)), gain, gain],
                  out_shape=[jax.ShapeDtypeStruct((S, HQD), BF16), ks, ks, jax.ShapeDtypeStruct((PP, 2 * W, 2 * W), F32),
                             jax.ShapeDtypeStruct((2 * PP, 1, LANES), F32), gs, gs],
                  scratch_shapes=[pltpu.VMEM((2, S, LANES), BF16), pltpu.VMEM((S, LANES), BF16), pltpu.VMEM((LANES, S), BF16),
                                  pltpu.VMEM((S, LANES), BF16), pltpu.VMEM((2, S, LANES), BF16)] + [pltpu.VMEM((S, LANES), F32)] * 3,
                  compiler_params=_params(("arbitrary", "arbitrary")))(qproj, kk, vv, bias_t_ab, sink_row, gq2, gk2, lse_row, do)


def _adamw_update(w, g, m, v):
    m2 = ADAM_B1 * m + (1.0 - ADAM_B1) * g
    v2 = ADAM_B2 * v + (1.0 - ADAM_B2) * jnp.square(g)
    m_hat = m2 / (1.0 - ADAM_B1 ** ADAM_STEP)
    v_hat = v2 / (1.0 - ADAM_B2 ** ADAM_STEP)
    return -ADAM_LR * (m_hat / (jnp.sqrt(v_hat) + ADAM_EPS) + ADAM_WD * w), m2, v2


def _adamw(w, g, m, v, name, tr=256, tc=256):
    R, C = w.shape
    tr = min(tr, R)
    if R % tr == 0:
        grid, blk = (R // tr,), pl.BlockSpec((tr, C), lambda i: (i, 0))
    else:
        assert C % tc == 0
        grid, blk = (C // tc,), pl.BlockSpec((R, tc), lambda i: (0, i))

    def body(w_ref, g_ref, m_ref, v_ref, d_ref, m2_ref, v2_ref):
        d_ref[...], m2_ref[...], v2_ref[...] = _adamw_update(w_ref[...], g_ref[...], m_ref[...], v_ref[...])

    return _pcall(body, name=name, grid=grid, in_specs=[blk] * 4, out_specs=[blk] * 3,
                  out_shape=[jax.ShapeDtypeStruct((R, C), F32)] * 3, compiler_params=_params(("parallel",)))(w, g, m, v)


def _adamw_two_layers(w, g0, g1, m, v, name, tr=256):
    R, C = g0.shape
    assert R % tr == 0 and w.shape == (2 * R, C)
    nr = R // tr

    def body(w_ref, g0_ref, g1_ref, m_ref, v_ref, g_ref, d_ref, m2_ref, v2_ref):
        g = jnp.where(pl.program_id(0) == 0, g0_ref[...], g1_ref[...])
        g_ref[...] = g
        d_ref[...], m2_ref[...], v2_ref[...] = _adamw_update(w_ref[...], g, m_ref[...], v_ref[...])

    both = pl.BlockSpec((tr, C), lambda l, i: (l * nr + i, 0))
    first = pl.BlockSpec((tr, C), lambda l, i: (i * (1 - l) + (nr - 1) * l, 0))
    second = pl.BlockSpec((tr, C), lambda l, i: (i * l, 0))
    return _pcall(body, name=name, grid=(2, nr), in_specs=[both, first, second, both, both], out_specs=[both] * 4,
                  out_shape=[jax.ShapeDtypeStruct((2 * R, C), F32)] * 4,
                  compiler_params=_params(("arbitrary", "arbitrary")))(w, g0, g1, m, v)


def _sum_core_pair(arr, got, place, name, tr=512):
    P, hr, C = got.shape
    tr = tr if hr % tr == 0 else hr
    nt = hr // tr

    def body(place_ref, a_ref, g_ref, o_ref):
        o_ref[...] = (a_ref[...].astype(F32) + g_ref[...].astype(F32)).astype(o_ref.dtype)

    spec = pltpu.PrefetchScalarGridSpec(
        num_scalar_prefetch=1, grid=(P, nt),
        in_specs=[pl.BlockSpec((None, tr, C), lambda j, i, pr: (j, pr[1] * nt + i, 0)),
                  pl.BlockSpec((None, tr, C), lambda j, i, pr: (j, i, 0))],
        out_specs=pl.BlockSpec((None, tr, C), lambda j, i, pr: (j, i, 0)))
    return _pcall(body, name=name, grid_spec=spec, out_shape=jax.ShapeDtypeStruct(got.shape, BF16),
                  compiler_params=_params(("parallel", "parallel")))(place, arr, got)


def _sum_chips(pair, landed, place, name, tr=256):
    _, R, C = landed.shape
    tr = tr if R % tr == 0 else R

    def body(place_ref, p_ref, l_ref, o_ref):
        acc = p_ref[...].astype(F32)
        for k in range(3):
            acc = acc + l_ref[k].astype(F32)
        o_ref[...] = acc

    spec = pltpu.PrefetchScalarGridSpec(
        num_scalar_prefetch=1, grid=(R // tr,),
        in_specs=[pl.BlockSpec((None, tr, C), lambda i, pr: (pr[0], i, 0)), pl.BlockSpec((3, tr, C), lambda i, pr: (0, i, 0))],
        out_specs=pl.BlockSpec((None, tr, C), lambda i, pr: (pr[1], i, 0)))
    return _pcall(body, name=name, grid_spec=spec, out_shape=jax.ShapeDtypeStruct((2, R, C), F32),
                  compiler_params=_params(("parallel",)))(place, pair, landed)


def _place():
    x, y, c = lax.axis_index("x"), lax.axis_index("y"), lax.axis_index("c")
    others = [(1 - x, y), (x, 1 - y), (1 - x, 1 - y)]
    return x, y, c, others


def _half_rows(ref, hh, lead=()):
    hr = ref.shape[-2] // 2
    return ref.at[(*lead, pl.ds(pl.multiple_of(hh * hr, 16), hr), slice(None))]


def _sem_arrays(*counts):
    return [pltpu.SemaphoreType.DMA((k,)) for k in counts]


SEM_SPEC = pl.BlockSpec(memory_space=pltpu.SEMAPHORE)
ANY_SPEC = pl.BlockSpec(memory_space=pl.ANY)
DATAFLOW = pltpu.SideEffectType.DATAFLOW_SIDE_EFFECTING


def _in_hbm(a):
    return pltpu.with_memory_space_constraint(a, pltpu.HBM)


def _gather_copies(srcs, lands, send_sems, recv_sems):
    x, y, c, others = _place()
    me = 2 * x + y

    def copy(w, k, dst_chip, to):
        return pltpu.make_async_remote_copy(src_ref=_half_rows(srcs[w], c), dst_ref=_half_rows(lands[w], c, (dst_chip,)),
                                            send_sem=send_sems.at[3 * w + k], recv_sem=recv_sems.at[3 * w + k],
                                            device_id=to, device_id_type=MESH)

    pairs = [(w, k, cx, cy) for w in range(len(srcs)) for k, (cx, cy) in enumerate(others)]
    return ([copy(w, k, me, (cx, cy, c)) for w, k, cx, cy in pairs],
            [copy(w, k, 2 * cx + cy, (cx, cy, c)) for w, k, cx, cy in pairs])


def _gather_start(shards, after, name):
    n = len(shards)

    def body(*refs):
        srcs, lands, send_sems, recv_sems, token = refs[:n], refs[n:2 * n], refs[2 * n + 1], refs[2 * n + 2], refs[-1]
        for cp in _gather_copies(srcs, lands, send_sems, recv_sems)[0]:
            cp.start()
        token[...] = jnp.zeros_like(token)

    lands = [lax.empty((N_CHIPS,) + s.shape, s.dtype) for s in shards]
    outs = _pcall(
        body, name=name, in_specs=[HBM_SPEC] * (2 * n) + [ANY_SPEC],
        out_specs=[SEM_SPEC, SEM_SPEC] + [HBM_SPEC] * (2 * n) + [VMEM_SPEC],
        out_shape=[pltpu.SemaphoreType.DMA((3 * n,)), pltpu.SemaphoreType.DMA((3 * n,))]
        + [pltpu.HBM(a.shape, a.dtype) for a in list(shards) + lands] + [jax.ShapeDtypeStruct((8, LANES), F32)],
        input_output_aliases={i: 2 + i for i in range(2 * n)},
        compiler_params=pltpu.CompilerParams(has_side_effects=DATAFLOW),
    )(*[_in_hbm(a) for a in list(shards) + lands], after)
    return outs[0], outs[1], outs[2:2 + n], outs[2 + n:2 + 2 * n], outs[-1]


def _gather_wait(started, after, name):
    send_sems, recv_sems, srcs, lands, _ = started
    n = len(srcs)

    def body(*refs):
        src_refs, land_refs, send_ref, recv_ref = refs[:n], refs[n:2 * n], refs[2 * n], refs[2 * n + 1]
        outgoing, incoming = _gather_copies(src_refs, land_refs, send_ref, recv_ref)
        for out_cp, in_cp in zip(outgoing, incoming):
            out_cp.wait_send()
            in_cp.wait_recv()

    outs = _pcall(
        body, name=name, in_specs=[HBM_SPEC] * (2 * n) + [SEM_SPEC, SEM_SPEC, ANY_SPEC], out_specs=[HBM_SPEC] * (2 * n),
        out_shape=[pltpu.HBM(a.shape, a.dtype) for a in list(srcs) + list(lands)],
        input_output_aliases={i: i for i in range(2 * n)},
        compiler_params=pltpu.CompilerParams(has_side_effects=DATAFLOW),
    )(*srcs, *lands, send_sems, recv_sems, after)
    return outs[:n], outs[n:]


def _gather_pass_on(shards, lands, name):
    n = len(shards)
    per = 4

    def body(*refs):
        srcs, bufs = refs[:n], refs[2 * n:3 * n]
        send_sems, recv_sems = refs[3 * n:]
        x, y, c, others = _place()
        me = 2 * x + y
        sibling = (x, y, 1 - c)

        def copy(w, k, src, dst):
            return pltpu.make_async_remote_copy(src_ref=src, dst_ref=dst, send_sem=send_sems.at[per * w + k],
                                                recv_sem=recv_sems.at[per * w + k], device_id=sibling, device_id_type=MESH)

        sends, recvs = [], []
        for w in range(n):
            for k, (cx, cy) in enumerate(others):
                mine, theirs = _half_rows(bufs[w], c, (2 * cx + cy,)), _half_rows(bufs[w], 1 - c, (2 * cx + cy,))
                sends.append(copy(w, k, mine, mine))
                recvs.append(copy(w, k, theirs, theirs))
            sends.append(copy(w, 3, srcs[w], bufs[w].at[me]))
            recvs.append(sends[-1])
        for cp in sends:
            cp.start()
        for snd, rcv in zip(sends, recvs):
            snd.wait_send()
            rcv.wait_recv()

    return _pcall(body, name=name, in_specs=[HBM_SPEC] * (2 * n), out_specs=[HBM_SPEC] * n,
                  out_shape=[jax.ShapeDtypeStruct(l.shape, l.dtype) for l in lands],
                  input_output_aliases={n + w: w for w in range(n)},
                  scratch_shapes=_sem_arrays(per * n, per * n))(*shards, *lands)


def _scatter_copies(srcs, lands, send_sems, recv_sems):
    x, y, c, others = _place()
    return [pltpu.make_async_remote_copy(src_ref=srcs[w].at[2 * cx + cy], dst_ref=lands[w].at[k],
                                         send_sem=send_sems.at[3 * w + k], recv_sem=recv_sems.at[3 * w + k],
                                         device_id=(cx, cy, c), device_id_type=MESH)
            for w in range(len(srcs)) for k, (cx, cy) in enumerate(others)]


def _scatter_start(parts, name):
    n = len(parts)

    def body(*refs):
        srcs, lands, send_sems, recv_sems, token = refs[:n], refs[n:2 * n], refs[2 * n], refs[2 * n + 1], refs[-1]
        for cp in _scatter_copies(srcs, lands, send_sems, recv_sems):
            cp.start()
        token[...] = jnp.zeros_like(token)

    lands = [lax.empty((3,) + p.shape[1:], p.dtype) for p in parts]
    outs = _pcall(
        body, name=name, in_specs=[HBM_SPEC] * (2 * n), out_specs=[SEM_SPEC, SEM_SPEC] + [HBM_SPEC] * (2 * n) + [VMEM_SPEC],
        out_shape=[pltpu.SemaphoreType.DMA((3 * n,)), pltpu.SemaphoreType.DMA((3 * n,))]
        + [pltpu.HBM(a.shape, a.dtype) for a in list(parts) + lands] + [jax.ShapeDtypeStruct((8, LANES), F32)],
        input_output_aliases={i: 2 + i for i in range(2 * n)},
        compiler_params=pltpu.CompilerParams(has_side_effects=DATAFLOW),
    )(*[_in_hbm(a) for a in list(parts) + lands])
    return (outs[0], outs[1], outs[2:2 + n], outs[2 + n:2 + 2 * n]), outs[-1]


def _scatter_wait(started, after, name):
    send_sems, recv_sems, srcs, lands = started
    n = len(srcs)

    def body(*refs):
        for cp in _scatter_copies(refs[:n], refs[n:2 * n], refs[2 * n], refs[2 * n + 1]):
            cp.wait_send()
            cp.wait_recv()

    outs = _pcall(
        body, name=name, in_specs=[HBM_SPEC] * (2 * n) + [SEM_SPEC, SEM_SPEC, ANY_SPEC], out_specs=[HBM_SPEC] * (2 * n),
        out_shape=[pltpu.HBM(a.shape, a.dtype) for a in list(srcs) + list(lands)],
        input_output_aliases={i: i for i in range(2 * n)},
        compiler_params=pltpu.CompilerParams(has_side_effects=DATAFLOW),
    )(*srcs, *lands, send_sems, recv_sems, after)
    return outs[:n], outs[n:]


def _split_start(plan, arrays, n_copies, after, name):
    n = len(arrays)

    def body(*refs):
        for cp in plan(refs[:n], refs[n + 1], refs[n + 2])[0]:
            cp.start()
        refs[-1][...] = jnp.zeros_like(refs[-1])

    outs = _pcall(
        body, name=name, in_specs=[HBM_SPEC] * n + [ANY_SPEC], out_specs=[SEM_SPEC, SEM_SPEC] + [HBM_SPEC] * n + [VMEM_SPEC],
        out_shape=[pltpu.SemaphoreType.DMA((n_copies,)), pltpu.SemaphoreType.DMA((n_copies,))]
        + [pltpu.HBM(a.shape, a.dtype) for a in arrays] + [jax.ShapeDtypeStruct((8, LANES), F32)],
        input_output_aliases={i: 2 + i for i in range(n)},
        compiler_params=pltpu.CompilerParams(has_side_effects=DATAFLOW),
    )(*[_in_hbm(a) for a in arrays], after)
    return (outs[0], outs[1], outs[2:2 + n]), outs[-1]


def _split_wait(plan, started, after, name):
    send_sems, recv_sems, arrays = started
    n = len(arrays)

    def body(*refs):
        outgoing, incoming = plan(refs[:n], refs[n], refs[n + 1])
        for cp in outgoing:
            cp.wait_send()
        for cp in incoming:
            cp.wait_recv()

    return _pcall(
        body, name=name, in_specs=[HBM_SPEC] * n + [SEM_SPEC, SEM_SPEC, ANY_SPEC], out_specs=[HBM_SPEC] * n,
        out_shape=[pltpu.HBM(a.shape, a.dtype) for a in arrays], input_output_aliases={i: i for i in range(n)},
        compiler_params=pltpu.CompilerParams(has_side_effects=DATAFLOW),
    )(*arrays, send_sems, recv_sems, after)


def _to_sibling(src, dst, k, send_sems, recv_sems):
    x, y, c, _ = _place()
    return pltpu.make_async_remote_copy(src_ref=src, dst_ref=dst, send_sem=send_sems.at[k], recv_sem=recv_sems.at[k],
                                        device_id=(x, y, 1 - c), device_id_type=MESH)


def _plan_pass_on(n):
    def plan(refs, send_sems, recv_sems):
        x, y, c, others = _place()
        cps = []
        for w in range(n):
            for k, (cx, cy) in enumerate(others):
                mine = _half_rows(refs[n + w], c, (2 * cx + cy,))
                cps.append(_to_sibling(mine, mine, 4 * w + k, send_sems, recv_sems))
            cps.append(_to_sibling(refs[w], refs[n + w].at[2 * x + y], 4 * w + 3, send_sems, recv_sems))
        return cps, cps
    return plan


def _plan_swap_halves(n):
    def plan(refs, send_sems, recv_sems):
        c = lax.axis_index("c")
        cps = [_to_sibling(_half_rows(refs[w], 1 - c, (slice(None),)), refs[n + w], w, send_sems, recv_sems) for w in range(n)]
        return cps, cps
    return plan


def _plan_share_halves(n):
    def plan(refs, send_sems, recv_sems):
        c = lax.axis_index("c")
        cps = [_to_sibling(refs[w].at[c], refs[w].at[c], w, send_sems, recv_sems) for w in range(n)]
        return cps, cps
    return plan


def _plan_gather_small(refs, send_sems, recv_sems):
    x, y, c, _ = _place()
    flips = [(dx, dy, dc) for dx in (0, 1) for dy in (0, 1) for dc in (0, 1)][1:]
    flip = lambda v, d: 1 - v if d else v
    cps = [pltpu.make_async_remote_copy(src_ref=refs[0], dst_ref=refs[1].at[4 * x + 2 * y + c], send_sem=send_sems.at[k],
                                        recv_sem=recv_sems.at[k], device_id=(flip(x, dx), flip(y, dy), flip(c, dc)),
                                        device_id_type=MESH)
           for k, (dx, dy, dc) in enumerate(flips)]
    return cps, cps


def _plan_scatter_direct(n):
    def plan(refs, send_sems, recv_sems):
        x, y, c, _ = _place()
        flips = [(dx, dy, dc) for dx in (0, 1) for dy in (0, 1) for dc in (0, 1)][1:]
        flip = lambda v, d: 1 - v if d else v
        cps = []
        for w in range(n):
            for r, (dx, dy, dc) in enumerate(flips):
                tx, ty, tc = flip(x, dx), flip(y, dy), flip(c, dc)
                cps.append(pltpu.make_async_remote_copy(
                    src_ref=_half_rows(refs[w], tc, (2 * tx + ty,)), dst_ref=refs[n + w].at[r],
                    send_sem=send_sems.at[7 * w + r], recv_sem=recv_sems.at[7 * w + r],
                    device_id=(tx, ty, tc), device_id_type=MESH))
        return cps, cps
    return plan


def _sum_direct(mine, landed, place, name, tr=256):
    _, R, C = landed.shape
    tr = tr if R % tr == 0 else R
    nt = R // tr

    def body(place_ref, m_ref, l_ref, o_ref):
        acc = m_ref[...].astype(F32)
        for k in range(7):
            acc = acc + l_ref[k].astype(F32)
        o_ref[...] = acc

    spec = pltpu.PrefetchScalarGridSpec(
        num_scalar_prefetch=1, grid=(nt,),
        in_specs=[pl.BlockSpec((None, tr, C), lambda i, pr: (pr[0], pr[1] * nt + i, 0)),
                  pl.BlockSpec((7, tr, C), lambda i, pr: (0, i, 0))],
        out_specs=pl.BlockSpec((None, tr, C), lambda i, pr: (pr[1], i, 0)))
    return _pcall(body, name=name, grid_spec=spec, out_shape=jax.ShapeDtypeStruct((2, R, C), F32),
                  compiler_params=_params(("parallel",)))(place, mine, landed)


def _sum_gathered_small(gathered, own, place, name):
    _, M, C = gathered.shape

    def body(place_ref, g_ref, own_ref, o_ref):
        me = 2 * place_ref[0] + place_ref[1]
        acc = jnp.zeros((M, C), F32)
        for k in range(8):
            acc = acc + jnp.where(me == k, own_ref[...], g_ref[k])
        o_ref[...] = acc

    spec = pltpu.PrefetchScalarGridSpec(
        num_scalar_prefetch=1, grid=(1,),
        in_specs=[pl.BlockSpec((8, M, C), lambda i, pr: (0, 0, 0)), pl.BlockSpec((M, C), lambda i, pr: (0, 0))],
        out_specs=pl.BlockSpec((M, C), lambda i, pr: (0, 0)))
    return _pcall(body, name=name, grid_spec=spec, out_shape=jax.ShapeDtypeStruct((M, C), F32),
                  compiler_params=_params(("arbitrary",)))(place, gathered, own)


def _swap_halves_group(arrs, name):
    n = len(arrs)

    def body(*refs):
        ins, gots = refs[:n], refs[n:2 * n]
        send_sems, recv_sems = refs[2 * n:]
        x, y, c, _ = _place()
        swaps = [pltpu.make_async_remote_copy(src_ref=_half_rows(ins[w], 1 - c, (slice(None),)), dst_ref=gots[w],
                                              send_sem=send_sems.at[w], recv_sem=recv_sems.at[w],
                                              device_id=(x, y, 1 - c), device_id_type=MESH) for w in range(n)]
        for cp in swaps:
            cp.start()
        for cp in swaps:
            cp.wait()

    half_shapes = [jax.ShapeDtypeStruct((a.shape[0], a.shape[1] // 2, a.shape[2]), a.dtype) for a in arrs]
    return _pcall(body, name=name, in_specs=[HBM_SPEC] * n, out_specs=[HBM_SPEC] * n, out_shape=half_shapes,
                  scratch_shapes=_sem_arrays(n, n))(*arrs)


def _pack_rows(n_elems, width, align):
    rows = -(-n_elems // width)
    return -(-rows // align) * align


def _pack(arrays, dtype, width, align):
    flat = jnp.concatenate([a.astype(dtype).reshape(-1) for a in arrays])
    rows = _pack_rows(flat.shape[0], width, align)
    flat = jnp.pad(flat, (0, rows * width - flat.shape[0]))
    return flat.reshape(rows, width)


def _pack_small(arrays):
    return _pack(arrays, F32, width=128, align=8)


def _unpack(flat, shapes):
    out, off = [], 0
    for shp in shapes:
        n = int(np.prod(shp))
        out.append(flat[..., off:off + n].reshape(flat.shape[:-1] + tuple(shp)))
        off += n
    return out


def _doubled_heads(x2d, n_heads):
    S = x2d.shape[0]
    h = x2d.reshape(S, n_heads, HEAD_DIM).transpose(1, 0, 2)
    return jnp.concatenate([h, h], axis=-1)


def _rms_bwd_epilogue(dn, x, dres, g):
    r = _rinv(x)
    xh = x * r
    dxh = dn * g
    dx = dres + r * (dxh - xh * jnp.mean(dxh * xh, axis=-1, keepdims=True))
    return dx, dx, jnp.sum(dn * xh, axis=0, keepdims=True)


def _residual_then_norms(acc, res, *gains):
    h = res + acc
    hn = h * _rinv(h)
    return (h,) + tuple(hn * g for g in gains)


def _mlp_fwd(h, n, w_up4, w_down, next_gains, tag, between=None):
    u, a = _matmul(n, w_up4, "nn", f"up{tag}", out_dtypes=(F32, BF16), chipwise="b", tm=2048, tn=512,
                   epilogue=lambda acc: (acc, jnp.square(jnp.maximum(acc, 0.0))))
    if between is not None:
        next_gains = [next_gains[0] + between(a)] + list(next_gains[1:])
    assert w_down.shape[1] == 1024
    outs = _matmul(a, w_down, "nn", f"down{tag}", out_dtypes=(F32,) + (BF16,) * len(next_gains), extras=(h,),
                   row_extras=tuple(next_gains), epilogue=_residual_then_norms, tm=1024, tn=1024, tk=1024)
    outs = outs if next_gains else (outs,)
    return outs[0], outs[1:], (n, u, a)


def _mlp_bwd(dh_out, dh_out_b, h, g, w_up4, w_down, saved, tag):
    n, u, a = saved
    dw_down = _matmul(a, dh_out_b, "tn", f"dw_down{tag}", out_dtypes=(BF16,), tm=1024, tn=1024, chipwise="out_rows")
    du = _matmul(dh_out_b, w_down, "nt", f"du{tag}", out_dtypes=(BF16,), extras=(u,), tm=2048, tn=512,
                 epilogue=lambda acc, uu: (acc * (2.0 * jnp.maximum(uu, 0.0)),))
    dw_up = _matmul(n, du, "tn", f"dw_up{tag}", out_dtypes=(BF16,), chipwise="out", tm=1024, tn=512)
    dh, dh_b, dg = _matmul(du, w_up4, "nt", f"dn_mlp{tag}", out_dtypes=(F32, BF16, F32), extras=(h, dh_out), row_extras=(g,),
                           epilogue=_rms_bwd_epilogue, row_accums=1, tm=1024, tn=1024, tk=w_up4.shape[2], chipwise="b")
    return dh, dh_b, dg, dw_up, dw_down


def kernel(x, g_attn, g_mlp, w_in_a, b_f, gq_a, gk_a, w_out_a, g_kv, w_kv, gk_b, w_q_b, gq_b, sinks, rel_bias, w_out_b, w_up, w_down, loss_target, m_g_attn, m_g_mlp, m_w_in_a, m_b_f, m_gq_a, m_gk_a, m_w_out_a, m_g_kv, m_w_kv, m_gk_b, m_w_q_b, m_gq_b, m_sinks, m_rel_bias, m_w_out_b, m_w_up, m_w_down, v_g_attn, v_g_mlp, v_w_in_a, v_b_f, v_gq_a, v_gk_a, v_w_out_a, v_g_kv, v_w_kv, v_gk_b, v_w_q_b, v_gq_b, v_sinks, v_rel_bias, v_w_out_b, v_w_up, v_w_down):
    given = dict(locals())
    S, D = x.shape[1], x.shape[2]
    H = D // HEAD_DIM
    KVH = w_kv.shape[1] // (2 * HEAD_DIM)
    kvw = KVH * HEAD_DIM
    hw = H * HEAD_DIM
    W = WINDOW
    nb = S // W
    c_idx = lax.axis_index("c")
    xs, tgt = x[0], loss_target[0]

    n_in_shard = w_in_a.shape[2]
    rows_in = -(-n_in_shard // 32) * 32
    row_pad = lambda a: jnp.pad(a, [(0, 0)] * (a.ndim - 2) + [(0, rows_in - a.shape[-2]), (0, 0)])
    t_in = lambda a: jnp.swapaxes(a[0], 0, 1)
    shards = {"w_in_a": row_pad(t_in(w_in_a)), "w_out_a": w_out_a[0], "w_up0": w_up[0], "w_down0": w_down[0], "w_kv": w_kv,
              "w_q_b": w_q_b[0], "w_out_b": w_out_b[0], "w_up1": w_up[1], "w_down1": w_down[1]}
    parts = list(shards)
    groups = [("w_in_a", "w_out_a"), ("w_up0", "w_down0"), ("w_kv", "w_q_b", "w_out_b", "w_up1", "w_down1")]
    started = []
    for i, grp in enumerate(groups):
        behind = started[-1][4] if started else g_attn[0]
        started.append(_gather_start([shards[n].astype(BF16) for n in grp], behind, f"gather_start{i}"))
    gathered = {}

    def finish_gather(i, after):
        srcs, lands = _gather_wait(started[i], after, f"gather_wait{i}")
        gathered.update(zip(groups[i], _gather_pass_on(srcs, lands, f"gather_pass_on{i}")))

    def land_gather(i, after):
        srcs, lands = _gather_wait(started[i], after, f"gather_wait{i}")
        n = len(srcs)
        passing, token = _split_start(_plan_pass_on(n), list(srcs) + list(lands), 4 * n, after, f"pass_on_start{i}")
        return passing, token[0:1, 0:1]

    def finish_pass_on(i, passing, after):
        n = len(groups[i])
        gathered.update(zip(groups[i], _split_wait(_plan_pass_on(n), passing, after, f"pass_on_wait{i}")[n:]))

    vec = lambda a: a.reshape(1, -1)
    twice = lambda a: jnp.tile(a.reshape(1, -1), (1, 2))

    g_attn0 = vec(g_attn[0]) + sum(st[4][0, 0] for st in started)
    (n0,) = _rms_fwd(xs, [g_attn0], "rms_attn0")
    finish_gather(0, n0)
    win_t = gathered["w_in_a"][:, :n_in_shard].reshape(-1, D)
    win_t = jnp.pad(win_t, ((0, (-win_t.shape[0]) % 128), (0, 0)))
    wout_a = gathered["w_out_a"].reshape(-1, D)
    n_in = win_t.shape[0]
    tile_in = 640 if n_in % 640 == 0 else 128
    proj = _matmul(n0, win_t, "nt", "proj_in", tm=2048, tn=tile_in)
    zt = proj[:, 3 * hw:3 * hw + H].T
    c_row = _gate_fwd(zt, b_f.reshape(H, 1), "gate_fwd")
    c_row3 = c_row.reshape(H, 1, S)
    o_a, lse_a = _fox_fwd(proj, c_row3, twice(gq_a[0]), twice(gk_a[0]), H, "fox_fwd", t=512)
    passing1, tie = land_gather(1, o_a)
    h1, n1 = _matmul(o_a, wout_a, "nn", "out_a", out_dtypes=(F32, BF16), extras=(xs,), row_extras=(vec(g_mlp[0]) + tie,),
                     epilogue=_residual_then_norms, tm=1024, tn=1024)
    finish_pass_on(1, passing1, n1)
    wup = [gathered["w_up0"], None]
    wdown = [gathered["w_down0"].reshape(-1, D), None]
    passing = []

    def land_last_group(a):
        started_passing, tie = land_gather(2, a)
        passing.append(started_passing)
        return tie

    h2, (nkv, n2), mlp0 = _mlp_fwd(h1, n1, wup[0], wdown[0], [vec(g_kv), vec(g_attn[1])], "0", between=land_last_group)
    passing2 = passing[0]

    finish_pass_on(2, passing2, h2)
    wq_b, wout_b = gathered["w_q_b"].reshape(-1, D), gathered["w_out_b"].reshape(-1, D)
    wkv = gathered["w_kv"].reshape(D, -1)
    wup[1], wdown[1] = gathered["w_up1"], gathered["w_down1"].reshape(-1, D)
    kv = _matmul(nkv, wkv, "nn", "proj_kv", tm=2048)
    kk, vv = _doubled_heads(kv[:, :kvw], KVH), _doubled_heads(kv[:, kvw:], KVH)
    q2 = _matmul(n2, wq_b, "nn", "proj_q", tm=1024, tn=1024)
    onehot = jnp.asarray(_bucket_onehot(), dtype=BF16)
    bias = _bias_expand(rel_bias.T, onehot, "bias_expand").reshape(H, W, 2 * W)
    bias_ab = bias.reshape(H // 2, 2 * W, 2 * W)
    bias_t_ab = bias.reshape(H // 2, 2, W, 2 * W).transpose(0, 3, 1, 2).reshape(H // 2, 2 * W, 2 * W)
    sink_ab = jnp.repeat(sinks[0].reshape(H // 2, 2), W, axis=1)
    o_b, lse_b = _swa_fwd(q2, kk, vv, bias_ab, sink_ab.reshape(H // 2, 2 * W, 1), twice(gq_b[0]), twice(gk_b), "swa_fwd")
    h3, n3 = _matmul(o_b, wout_b, "nn", "out_b", out_dtypes=(F32, BF16), extras=(h2,), row_extras=(vec(g_mlp[1]),),
                     epilogue=_residual_then_norms, tm=1024, tn=1024)
    h4, _, mlp1 = _mlp_fwd(h3, n3, wup[1], wdown[1], [], "1")

    dh4, dh4_b, loss_part = _loss_head(h4, tgt, "loss_head")

    place = jnp.stack([2 * lax.axis_index("x") + lax.axis_index("y"), c_idx]).astype(jnp.int32)
    scattering = []

    def pair_and_scatter(names, mine, got):
        pair_sums = [_sum_core_pair(a, g, place, "sum_core_pair_" + n) for n, a, g in zip(names, mine, got)]
        started_scatter, token = _scatter_start(pair_sums, "scatter_start_" + names[0])
        scattering.append((names, started_scatter))
        return token[0:1, :]

    def start_reduce(named):
        names = list(named)
        mine = [named[n] for n in names]
        lands = [lax.empty((7, a.shape[1] // 2, a.shape[2]), a.dtype) for a in mine]
        started_scatter, token = _split_start(_plan_scatter_direct(len(mine)), mine + lands, 7 * len(mine), mine[0],
                                              "scatter_start_" + names[0])
        scattering.append((names, started_scatter))
        return token[0:1, :]

    def start_swap(named):
        return list(named), None, jnp.broadcast_to(start_reduce(named), (8, LANES))

    def finish_swap(swap, after):
        return swap[2][0:1, :]

    dh3, dh3_b, dg_mlp1, dw_up1, dw_down1 = _mlp_bwd(dh4, dh4_b, h3, vec(g_mlp[1]), wup[1], wdown[1], mlp1, "1")
    swap1 = start_swap({"w_down1": dw_down1, "w_up1": dw_up1})
    do_b = _matmul(dh3_b, wout_b, "nt", "do_b", tm=1024, tn=1024, after=swap1[2])
    dw_out_b = _matmul(o_b, dh3_b, "tn", "dw_out_b", out_dtypes=(BF16,), tm=1024, tn=1024, after=do_b)
    tie1 = finish_swap(swap1, dw_out_b)
    dq2, dk2, dv2, dbias_t_ab, dsink, dgq_b, dgk_b = _swa_bwd(
        q2, kk, vv, bias_t_ab, sink_ab.reshape(H // 2, 1, 2 * W), twice(gq_b[0]) + tie1, twice(gk_b),
        lse_b, do_b, "swa_bwd")
    dbias = dbias_t_ab.reshape(H // 2, 2 * W, 2, W).transpose(0, 2, 3, 1).reshape(H, W * 2 * W)
    d_rel_bias = _bias_reduce(dbias, onehot, "bias_reduce").T
    dw_q_b = _matmul(n2, dq2, "tn", "dw_q_b", out_dtypes=(BF16,), tm=1024, tn=1024)
    dn2 = _matmul(dq2, wq_b, "nt", "dn2", tm=1024, tn=1024)
    dkv = jnp.concatenate([dk2[h, :, :HEAD_DIM] for h in range(KVH)] + [dv2[h, :, :HEAD_DIM] for h in range(KVH)],
                          axis=1).astype(BF16)
    dw_kv = _matmul(nkv, dkv, "tn", "dw_kv", out_dtypes=(BF16,), tm=1024)
    dnkv = _matmul(dkv, wkv, "nt", "dnkv", tm=1024, tn=1024)
    tie2 = start_reduce({"w_out_b": dw_out_b.reshape(N_CHIPS, -1, D), "w_q_b": dw_q_b.reshape(N_CHIPS, -1, D),
                         "w_kv": dw_kv.reshape(N_CHIPS, -1, 2 * kvw)})
    dh2, dh2_b, (dg_kv, dg_attn1) = _rms_bwd(h2, dh3, [vec(g_kv) + tie2[:, :1], vec(g_attn[1])], [dnkv, dn2], "rms_attn1_bwd")

    dh1, dh1_b, dg_mlp0, dw_up0, dw_down0 = _mlp_bwd(dh2, dh2_b, h1, vec(g_mlp[0]), wup[0], wdown[0], mlp0, "0")
    swap3 = start_swap({"w_down0": dw_down0, "w_up0": dw_up0})
    do_a = _matmul(dh1_b, wout_a, "nt", "do_a", tm=1024, tn=1024, after=swap3[2])
    dw_out_a = _matmul(o_a, dh1_b, "tn", "dw_out_a", out_dtypes=(BF16,), tm=1024, tn=1024, after=do_a)
    tie3 = finish_swap(swap3, dw_out_a)
    dq_a, dk_a, dv_a, dc_row, dgq_a, dgk_a = _fox_bwd(
        proj, c_row3, twice(gq_a[0]) + tie3, twice(gk_a[0]), lse_a, do_a, H, "fox_bwd", t=512)
    dzt, db_f = _gate_bwd(dc_row.reshape(H, S), zt, b_f.reshape(H, 1), "gate_bwd")
    dproj = jnp.concatenate([dq_a, dk_a, dv_a, dzt.T.astype(BF16), jnp.zeros((S, n_in - 3 * hw - H), BF16)], axis=1)
    dw_in_t = _matmul(dproj, n0, "tn", "dw_in", out_dtypes=(BF16,), tm=tile_in, tn=1024)
    dw_in4 = row_pad(dw_in_t[:3 * hw + H].reshape(N_CHIPS, -1, D))
    tie4 = start_reduce({"w_out_a": dw_out_a.reshape(N_CHIPS, -1, D), "w_in_a": dw_in4})
    grad_x, _, dg_attn0 = _matmul(dproj, win_t, "nn", "dn0", out_dtypes=(F32, BF16, F32), extras=(xs, dh1),
                                  row_extras=(vec(g_attn[0]) + tie4[:, :1],), epilogue=_rms_bwd_epilogue, row_accums=1,
                                  tm=1024, tn=1024, tk=tile_in)

    small_grads = {
        "g_attn": jnp.concatenate([dg_attn0, dg_attn1], axis=0), "g_mlp": jnp.concatenate([dg_mlp0, dg_mlp1], axis=0),
        "b_f": db_f.reshape(1, H), "gq_a": dgq_a[:, :HEAD_DIM], "gk_a": dgk_a[:, :HEAD_DIM], "g_kv": dg_kv.reshape(-1),
        "gk_b": dgk_b[0, :HEAD_DIM], "gq_b": dgq_b[:, :HEAD_DIM], "sinks": dsink[:, 0, 0].reshape(1, H), "rel_bias": d_rel_bias,
    }
    small_shapes = [given[n].shape for n in SMALL] + [(1,)]
    spack = _pack_small([small_grads[n] for n in SMALL] + [loss_part])
    gathering_small, token = _split_start(_plan_gather_small, [spack, lax.empty((8,) + spack.shape, F32)], 7, grad_x,
                                          "gather_small_start")
    sharing = []
    for names, started_scatter in scattering:
        arrays = _split_wait(_plan_scatter_direct(len(names)), started_scatter, grad_x, "scatter_wait_" + names[0])
        halves = [_sum_direct(p, l, place, "sum_direct_" + n)
                  for n, p, l in zip(names, arrays[:len(names)], arrays[len(names):])]
        started_share, token = _split_start(_plan_share_halves(len(halves)), halves, len(halves), token, "share_start_" + names[0])
        sharing.append((names, started_share))
    own_small, others_small = _split_wait(_plan_gather_small, gathering_small, token, "gather_small_wait")
    small_sum = _sum_gathered_small(others_small, own_small, place, "sum_small")
    small_red = _unpack(small_sum.reshape(-1), small_shapes)
    reduced = {}
    for names, started_share in sharing:
        for n, r in zip(names, _split_wait(_plan_share_halves(len(names)), started_share, token, "share_wait_" + names[0])):
            reduced[n] = r.reshape(-1, r.shape[2])
    reduced["w_in_a"] = reduced["w_in_a"][:n_in_shard]
    loss = small_red[-1][0]

    grads = dict(zip(SMALL, small_red))
    no_loss = [jnp.zeros((1,), F32)]
    sw = _pack_small([given[n] for n in SMALL] + no_loss)
    sm = _pack_small([given["m_" + n] for n in SMALL] + no_loss)
    sv = _pack_small([given["v_" + n] for n in SMALL] + no_loss)
    sd, sm2, sv2 = _adamw(sw, small_sum, sm, sv, "adamw_small", tr=sw.shape[0])
    delta = dict(zip(SMALL, _unpack(sd.reshape(-1), small_shapes)))
    new_m = dict(zip(SMALL, _unpack(sm2.reshape(-1), small_shapes)))
    new_v = dict(zip(SMALL, _unpack(sv2.reshape(-1), small_shapes)))
    for n in ("w_out_a", "w_kv", "w_q_b", "w_out_b"):
        w = given[n]
        two_d = (-1, w.shape[-1])
        d, m2, v2 = _adamw(w.reshape(two_d), reduced[n], given["m_" + n].reshape(two_d), given["v_" + n].reshape(two_d),
                           "adamw_" + n)
        grads[n] = reduced[n].reshape(w.shape)
        delta[n], new_m[n], new_v[n] = d.reshape(w.shape), m2.reshape(w.shape), v2.reshape(w.shape)
    d, m2, v2 = _adamw(t_in(w_in_a), reduced["w_in_a"], t_in(m_w_in_a), t_in(v_w_in_a), "adamw_w_in_a")
    back = lambda a: jnp.swapaxes(a, 0, 1)[None]
    grads["w_in_a"], delta["w_in_a"], new_m["w_in_a"], new_v["w_in_a"] = back(reduced["w_in_a"]), back(d), back(m2), back(v2)
    for n in ("w_up", "w_down"):
        w = given[n]
        two_d = (-1, w.shape[-1])
        g, d, m2, v2 = _adamw_two_layers(w.reshape(two_d), reduced[n + "0"], reduced[n + "1"], given["m_" + n].reshape(two_d),
                                         given["v_" + n].reshape(two_d), "adamw_" + n)
        grads[n], delta[n], new_m[n], new_v[n] = g.reshape(w.shape), d.reshape(w.shape), m2.reshape(w.shape), v2.reshape(w.shape)

    order = ["g_attn", "g_mlp", "w_in_a", "b_f", "gq_a", "gk_a", "w_out_a", "g_kv", "w_kv", "gk_b", "w_q_b", "gq_b",
             "sinks", "rel_bias", "w_out_b", "w_up", "w_down"]
    return (loss, grad_x[None], *[grads[n] for n in order], *[delta[n] for n in order],
            *[new_m[n] for n in order], *[new_v[n] for n in order])
```

```python
import numpy as np
import jax
import jax.numpy as jnp
from jax import lax
from jax.experimental import pallas as pl
from jax.experimental.pallas import tpu as pltpu

F32 = jnp.float32
BF16 = jnp.bfloat16
MESH = pl.DeviceIdType.MESH

HEAD_DIM = 64
LANES = 128
WINDOW = 128
N_BUCKETS = 32
REL_MAX_DIST = 128
NORM_EPS = 1e-6
ADAM_LR = 0.001
ADAM_B1 = 0.9
ADAM_B2 = 0.999
ADAM_EPS = 1e-08
ADAM_WD = 0.01
ADAM_STEP = 10
NEG = -1e30
N_CHIPS = 4
VMEM_LIMIT = 56 * 1024 * 1024
HBM_SPEC = pl.BlockSpec(memory_space=pltpu.HBM)
VMEM_SPEC = pl.BlockSpec(memory_space=pltpu.VMEM)

SMALL = ("g_attn", "g_mlp", "b_f", "gq_a", "gk_a", "g_kv", "gk_b", "gq_b", "sinks", "rel_bias")


def _pcall(body, **kw):
    return pl.pallas_call(body, **kw)


def _params(sem=None):
    return pltpu.CompilerParams(dimension_semantics=sem, vmem_limit_bytes=VMEM_LIMIT)


def _rinv(x):
    return lax.rsqrt(jnp.mean(x * x, axis=-1, keepdims=True) + NORM_EPS)


def _dot(a, b, dims, precision=None):
    return lax.dot_general(a, b, (dims, ((), ())), precision=precision, preferred_element_type=F32)


NN = ((1,), (0,))
NT = ((1,), (1,))
TN = ((0,), (0,))


def _accumulate(ref, val, first):
    @pl.when(first)
    def _():
        ref[...] = val

    @pl.when(jnp.logical_not(first))
    def _():
        ref[...] += val


def _matmul(a, b, mode, name, out_dtypes=(F32,), extras=(), row_extras=(), epilogue=None, tm=512, tn=512, tk=None, chipwise=None,
            after=None, row_accums=0):
    if chipwise == "b":
        nc = b.shape[2]
        M, K = a.shape
        (K2, N) = (b.shape[1], N_CHIPS * nc) if mode == "nn" else (N_CHIPS * nc, b.shape[1])
    elif mode == "nn":
        (M, K), (K2, N) = a.shape, b.shape
    elif mode == "nt":
        (M, K), (N, K2) = a.shape, b.shape
    else:
        (K, M), (K2, N) = a.shape, b.shape
    assert K == K2, (a.shape, b.shape, mode)
    tm, tn = min(tm, M), min(tn, N)
    tk = K if tk is None else tk
    assert M % tm == 0 and N % tn == 0 and K % tk == 0, (M, N, K, tm, tn, tk)
    nk = K // tk
    dims = {"nn": NN, "nt": NT, "tn": TN}[mode]
    a_spec = pl.BlockSpec((tk, tm), lambda i, j, k: (k, i)) if mode == "tn" else pl.BlockSpec((tm, tk), lambda i, j, k: (i, k))
    b_spec = pl.BlockSpec((tn, tk), lambda i, j, k: (j, k)) if mode == "nt" else pl.BlockSpec((tk, tn), lambda i, j, k: (k, j))
    o_spec = pl.BlockSpec((tm, tn), lambda i, j, k: (i, j))
    out_shape = (M, N)
    if chipwise == "b" and mode == "nn":
        per = nc // tn
        assert tk == K and nc % tn == 0
        b_spec = pl.BlockSpec((None, tk, tn), lambda i, j, k: (j // per, 0, j % per))
    elif chipwise == "b":
        assert mode == "nt" and tk == nc
        b_spec = pl.BlockSpec((None, tn, tk), lambda i, j, k: (k, j, 0))
    elif chipwise == "out_rows":
        per = (M // N_CHIPS) // tm
        assert (M // N_CHIPS) % tm == 0 and not extras
        o_spec = pl.BlockSpec((None, tm, tn), lambda i, j, k: (i // per, i % per, j))
        out_shape = (N_CHIPS, M // N_CHIPS, N)
    elif chipwise == "out":
        per = (N // N_CHIPS) // tn
        assert (N // N_CHIPS) % tn == 0
        o_spec = pl.BlockSpec((None, tm, tn), lambda i, j, k: (j // per, i, j % per))
        out_shape = (N_CHIPS, M, N // N_CHIPS)
        assert not extras
    n_ex, n_rex, n_out = len(extras), len(row_extras), len(out_dtypes)
    assert not row_accums or tn == N
    tail = () if after is None else (after,)
    n_in = 2 + n_ex + n_rex + len(tail)

    def body(*refs):
        a_ref, b_ref = refs[0], refs[1]
        ex_refs = refs[2:2 + n_ex + n_rex]
        out_refs = refs[n_in:n_in + n_out]
        part = _dot(a_ref[...].astype(BF16), b_ref[...].astype(BF16), dims)

        def finish(acc):
            outs = (acc,) if epilogue is None else epilogue(acc, *[r[...] for r in ex_refs])
            for idx, (r, o) in enumerate(zip(out_refs, outs)):
                if idx >= n_out - row_accums:
                    _accumulate(r, o, pl.program_id(0) == 0)
                else:
                    r[...] = o.astype(r.dtype)

        if nk == 1:
            finish(part)
            return
        acc_ref = refs[n_in + n_out]
        k = pl.program_id(2)

        @pl.when(k == 0)
        def _():
            acc_ref[...] = part

        @pl.when(jnp.logical_and(k > 0, k < nk - 1))
        def _():
            acc_ref[...] += part

        @pl.when(k == nk - 1)
        def _():
            finish(acc_ref[...] + part)

    row_spec = pl.BlockSpec((1, tn), lambda i, j, k: (0, j))
    outs = _pcall(
        body, name=name, grid=(M // tm, N // tn, nk),
        in_specs=[a_spec, b_spec] + [o_spec] * n_ex + [row_spec] * n_rex + [pl.BlockSpec(memory_space=pl.ANY)] * len(tail),
        out_specs=[o_spec] * (n_out - row_accums) + [row_spec] * row_accums,
        out_shape=[jax.ShapeDtypeStruct(out_shape, dt) for dt in out_dtypes[:n_out - row_accums]]
        + [jax.ShapeDtypeStruct((1, N), F32)] * row_accums,
        scratch_shapes=[pltpu.VMEM((tm, tn), F32)] if nk > 1 else [],
        compiler_params=_params(("arbitrary",) * 3 if row_accums else ("parallel", "parallel", "arbitrary")),
    )(a, b, *extras, *row_extras, *tail)
    return outs[0] if n_out == 1 else outs


def _rms_fwd(x, gains, name, ts=256):
    S, D = x.shape
    ts = min(ts, S)
    n = len(gains)

    def body(*refs):
        x_ref, g_refs, o_refs = refs[0], refs[1:1 + n], refs[1 + n:]
        xv = x_ref[...]
        xh = xv * _rinv(xv)
        for g_ref, o_ref in zip(g_refs, o_refs):
            o_ref[...] = (xh * g_ref[...]).astype(BF16)

    row = pl.BlockSpec((ts, D), lambda i: (i, 0))
    vec = pl.BlockSpec((1, D), lambda i: (0, 0))
    return _pcall(body, name=name, grid=(S // ts,), in_specs=[row] + [vec] * n, out_specs=[row] * n,
                  out_shape=[jax.ShapeDtypeStruct((S, D), BF16)] * n, compiler_params=_params(("parallel",)))(x, *gains)


def _rms_bwd(x, dres, gains, dns, name, ts=256):
    S, D = x.shape
    ts = min(ts, S)
    n = len(gains)

    def body(*refs):
        x_ref, dres_ref = refs[0], refs[1]
        g_refs, dn_refs = refs[2:2 + n], refs[2 + n:2 + 2 * n]
        dx_ref, dxb_ref, dg_refs = refs[2 + 2 * n], refs[3 + 2 * n], refs[4 + 2 * n:]
        xv = x_ref[...]
        r = _rinv(xv)
        xh = xv * r
        dx = dres_ref[...]
        first = pl.program_id(0) == 0
        for g_ref, dn_ref, dg_ref in zip(g_refs, dn_refs, dg_refs):
            dn = dn_ref[...].astype(F32)
            _accumulate(dg_ref, jnp.sum(dn * xh, axis=0, keepdims=True), first)
            dxh = dn * g_ref[...]
            dx = dx + r * (dxh - xh * jnp.mean(dxh * xh, axis=-1, keepdims=True))
        dx_ref[...] = dx
        dxb_ref[...] = dx.astype(BF16)

    row = pl.BlockSpec((ts, D), lambda i: (i, 0))
    vec = pl.BlockSpec((1, D), lambda i: (0, 0))
    outs = _pcall(body, name=name, grid=(S // ts,), in_specs=[row, row] + [vec] * n + [row] * n,
                  out_specs=[row, row] + [vec] * n,
                  out_shape=[jax.ShapeDtypeStruct((S, D), F32), jax.ShapeDtypeStruct((S, D), BF16)]
                  + [jax.ShapeDtypeStruct((1, D), F32)] * n,
                  compiler_params=_params(("arbitrary",)))(x, dres, *gains, *dns)
    return outs[0], outs[1], outs[2:]


def _loss_head(h, tgt, name, ts=256):
    S, D = h.shape
    ts = min(ts, S)

    def body(h_ref, t_ref, dh_ref, dhb_ref, loss_ref):
        err = h_ref[...] - t_ref[...]
        dh = err * (1.0 / D)
        dh_ref[...] = dh
        dhb_ref[...] = dh.astype(BF16)
        part = 0.5 * jnp.sum(jnp.mean(err * err, axis=-1, keepdims=True), axis=0, keepdims=True)
        _accumulate(loss_ref, part, pl.program_id(0) == 0)

    row = pl.BlockSpec((ts, D), lambda i: (i, 0))
    return _pcall(body, name=name, grid=(S // ts,), in_specs=[row, row],
                  out_specs=[row, row, pl.BlockSpec((1, 1), lambda i: (0, 0))],
                  out_shape=[jax.ShapeDtypeStruct((S, D), F32), jax.ShapeDtypeStruct((S, D), BF16),
                             jax.ShapeDtypeStruct((1, 1), F32)],
                  compiler_params=_params(("arbitrary",)))(h, tgt)


def _gate_fwd(zt, bf, name):
    H, S = zt.shape
    nb = S // 128

    def body(z_ref, b_ref, c_ref):
        z = z_ref[...] + b_ref[...]
        lf = jnp.minimum(z, 0.0) - jnp.log(1.0 + jnp.exp(-jnp.abs(z)))
        upper = (lax.broadcasted_iota(jnp.int32, (128, 128), 0) <= lax.broadcasted_iota(jnp.int32, (128, 128), 1)).astype(F32)
        carry = jnp.zeros((H, 1), F32)
        for blk in range(nb):
            cs = _dot(lf[:, blk * 128:(blk + 1) * 128], upper, NN, precision=lax.Precision.HIGHEST) + carry
            c_ref[:, blk * 128:(blk + 1) * 128] = cs
            carry = cs[:, 127:128]

    return _pcall(body, name=name, in_specs=[VMEM_SPEC, VMEM_SPEC], out_specs=VMEM_SPEC,
                  out_shape=jax.ShapeDtypeStruct((H, S), F32))(zt, bf)


def _gate_bwd(dct, zt, bf, name):
    H, S = zt.shape
    nb = S // 128

    def body(dc_ref, z_ref, b_ref, dz_ref, db_ref):
        z = z_ref[...] + b_ref[...]
        e = jnp.exp(-jnp.abs(z))
        sig_neg = jnp.where(z >= 0, e, 1.0) / (1.0 + e)
        lower = (lax.broadcasted_iota(jnp.int32, (128, 128), 0) >= lax.broadcasted_iota(jnp.int32, (128, 128), 1)).astype(F32)
        dc = dc_ref[...]
        carry = jnp.zeros((H, 1), F32)
        db = jnp.zeros((H, 1), F32)
        for blk in reversed(range(nb)):
            sl = slice(blk * 128, (blk + 1) * 128)
            dlf = _dot(dc[:, sl], lower, NN, precision=lax.Precision.HIGHEST) + carry
            carry = dlf[:, 0:1]
            dz = dlf * sig_neg[:, sl]
            dz_ref[:, sl] = dz
            db = db + jnp.sum(dz, axis=1, keepdims=True)
        db_ref[...] = db

    return _pcall(body, name=name, in_specs=[VMEM_SPEC] * 3, out_specs=[VMEM_SPEC] * 2,
                  out_shape=[jax.ShapeDtypeStruct((H, S), F32), jax.ShapeDtypeStruct((H, 1), F32)])(dct, zt, bf)


def _lane_is_a():
    return lax.broadcasted_iota(jnp.int32, (1, LANES), 1) < HEAD_DIM


def _per_head_mean(x, is_a):
    sa = jnp.sum(jnp.where(is_a, x, 0.0), axis=-1, keepdims=True)
    sb = jnp.sum(jnp.where(is_a, 0.0, x), axis=-1, keepdims=True)
    return jnp.where(is_a, sa, sb) / HEAD_DIM


def _pair_norm(raw, gain, is_a):
    return raw * lax.rsqrt(_per_head_mean(raw * raw, is_a) + NORM_EPS) * gain


def _pair_norm_bwd(raw, gain, dnormed, is_a):
    r = lax.rsqrt(_per_head_mean(raw * raw, is_a) + NORM_EPS)
    xh = raw * r
    dgain = jnp.sum(dnormed * xh, axis=0, keepdims=True)
    dxh = dnormed * gain
    return r * (dxh - xh * _per_head_mean(dxh * xh, is_a)), dgain


def _fold_heads(x):
    i = lax.broadcasted_iota(jnp.int32, (LANES, LANES), 0)
    j = lax.broadcasted_iota(jnp.int32, (LANES, LANES), 1)
    fold = ((i == j) | (i == j + HEAD_DIM) | (i + HEAD_DIM == j)).astype(F32)
    return _dot(x, fold, NN, precision=lax.Precision.HIGHEST)


def _fold_row(ref):
    ref[...] = _fold_heads(jnp.broadcast_to(ref[...], (8, LANES)))[0:1, :]


def _as_col(row):
    return jnp.broadcast_to(row, (LANES, row.shape[1])).T[:, 0:1]


def _as_row(col):
    return jnp.broadcast_to(col, (col.shape[0], LANES)).T[0:1, :]


def _tri_mask(t, keys_on_rows):
    r = lax.broadcasted_iota(jnp.int32, (t, t), 0)
    c = lax.broadcasted_iota(jnp.int32, (t, t), 1)
    return (r <= c) if keys_on_rows else (r >= c)


def _fox_fwd(proj, c_row, gq2, gk2, n_heads, name, t=256):
    S = proj.shape[0]
    H = n_heads
    P = H // 2
    t = min(t, S)
    nq = S // t

    def body(q_ref, k_ref, v_ref, cr_ref, gq_ref, gk_ref, o_ref, lse_ref, qs_s, kb_s, vb_s):
        is_a = _lane_is_a()
        qn = _pair_norm(q_ref[...], gq_ref[...], is_a) * 0.125
        qs_s[0] = jnp.where(is_a, qn, 0.0).astype(BF16)
        qs_s[1] = jnp.where(is_a, 0.0, qn).astype(BF16)
        kb_s[...] = _pair_norm(k_ref[...], gk_ref[...], is_a).astype(BF16)
        vb_s[...] = v_ref[...].astype(BF16)
        causal = _tri_mask(t, False)
        for i in range(nq):
            t0 = i * t
            rows = slice(t0, t0 + t)
            o_pair = None
            for a in range(2):
                qi = qs_s[a, rows, :]
                ci = _as_col(cr_ref[a, :, rows])
                s_d = jnp.where(causal, _dot(qi, kb_s[rows, :], NT) + ci - cr_ref[a, :, rows], NEG)
                m = jnp.max(s_d, axis=-1, keepdims=True)
                if i > 0:
                    s_l = _dot(qi, kb_s[0:t0, :], NT) + ci - cr_ref[a, :, 0:t0]
                    m = jnp.maximum(m, jnp.max(s_l, axis=-1, keepdims=True))
                p_d = jnp.exp(s_d - m)
                l = jnp.sum(p_d, axis=-1, keepdims=True)
                acc = _dot(p_d.astype(BF16), vb_s[rows, :], NN)
                if i > 0:
                    p_l = jnp.exp(s_l - m)
                    l = l + jnp.sum(p_l, axis=-1, keepdims=True)
                    acc = acc + _dot(p_l.astype(BF16), vb_s[0:t0, :], NN)
                o_a = acc / l
                lse_ref[a, :, rows] = _as_row(m + jnp.log(l))
                o_pair = o_a if a == 0 else jnp.where(is_a, o_pair, o_a)
            o_ref[rows, :] = o_pair.astype(BF16)

    def cols(off):
        return pl.BlockSpec((S, LANES), lambda p: (0, off + p))

    rowv = pl.BlockSpec((2, 1, S), lambda p: (p, 0, 0))
    gain = pl.BlockSpec((1, LANES), lambda p: (0, 0))
    return _pcall(body, name=name, grid=(P,), in_specs=[cols(0), cols(P), cols(2 * P), rowv, gain, gain],
                  out_specs=[cols(0), rowv],
                  out_shape=[jax.ShapeDtypeStruct((S, H * HEAD_DIM), BF16), jax.ShapeDtypeStruct((H, 1, S), F32)],
                  scratch_shapes=[pltpu.VMEM((2, S, LANES), BF16), pltpu.VMEM((S, LANES), BF16), pltpu.VMEM((S, LANES), BF16)],
                  compiler_params=_params(("parallel",)))(proj, proj, proj, c_row, gq2, gk2)


def _fox_bwd(proj, c_row, gq2, gk2, lse_row, do, n_heads, name, t=256):
    S = proj.shape[0]
    H = n_heads
    P = H // 2
    t = min(t, S)
    nq = S // t
    assert t % LANES == 0

    def body(q_ref, k_ref, v_ref, cr_ref, gq_ref, gk_ref, lr_ref, do_ref,
             dq_ref, dk_ref, dv_ref, dc_ref, dgq_ref, dgk_ref,
             qs_s, kb_s, kt_s, vb_s, dob_s, dq_s, dk_s, dv_s, dcs_s, cc_s):
        is_a = _lane_is_a()
        for a in range(2):
            for i in range(nq):
                cc_s[a, i * t:(i + 1) * t, :] = _as_col(cr_ref[a, :, i * t:(i + 1) * t])
        qn = _pair_norm(q_ref[...], gq_ref[...], is_a) * 0.125
        qs_s[0] = jnp.where(is_a, qn, 0.0).astype(BF16)
        qs_s[1] = jnp.where(is_a, 0.0, qn).astype(BF16)
        kn = _pair_norm(k_ref[...], gk_ref[...], is_a)
        kb_s[...] = kn.astype(BF16)
        kt_s[0] = jnp.where(is_a, kn, 0.0).T.astype(BF16)
        kt_s[1] = jnp.where(is_a, 0.0, kn).T.astype(BF16)
        vb_s[...] = v_ref[...].astype(BF16)
        dov = do_ref[...]
        dob_s[0] = jnp.where(is_a, dov, 0.0).astype(BF16)
        dob_s[1] = jnp.where(is_a, 0.0, dov).astype(BF16)
        dk_s[...] = jnp.zeros((S, LANES), F32)
        dv_s[...] = jnp.zeros((S, LANES), F32)
        dcs_s[...] = jnp.zeros((2, S, LANES), F32)
        causal = _tri_mask(t, True)
        for i in range(nq):
            t0 = i * t
            rows = slice(t0, t0 + t)
            dq_t = jnp.zeros((LANES, t), F32)
            for a in range(2):
                qi = qs_s[a, rows, :]
                doi = dob_s[a, rows, :]
                cri = cr_ref[a, :, rows]
                lri = lr_ref[a, :, rows]

                def probs(keys, masked, a=a, qi=qi, doi=doi, cri=cri, lri=lri):
                    p_t = jnp.exp(_dot(kb_s[keys, :], qi, NT) + cri - cc_s[a, keys, :] - lri)
                    if masked:
                        p_t = jnp.where(causal, p_t, 0.0)
                    return p_t, _dot(vb_s[keys, :], doi, NT)

                parts = [(rows,) + probs(rows, True)]
                if i > 0:
                    parts.append((slice(0, t0),) + probs(slice(0, t0), False))
                delta = sum(jnp.sum(p_t * dp_t, axis=0, keepdims=True) for _, p_t, dp_t in parts)
                for keys, p_t, dp_t in parts:
                    ds_t = p_t * (dp_t - delta)
                    dsb = ds_t.astype(BF16)
                    dv_s[keys, :] += _dot(p_t.astype(BF16), doi, NN)
                    dk_s[keys, :] += _dot(dsb, qi, NN)
                    dq_t = dq_t + _dot(kt_s[a, :, keys], dsb, NN)
                    dcs_s[a, keys, :] += sum(ds_t[:, b * LANES:(b + 1) * LANES] for b in range(t // LANES))
            dq_s[rows, :] = dq_t.T
        first = pl.program_id(0) == 0
        last = pl.program_id(0) == P - 1
        dq_raw, dgq = _pair_norm_bwd(q_ref[...], gq_ref[...], dq_s[...] * 0.125, is_a)
        dq_ref[...] = dq_raw.astype(BF16)
        _accumulate(dgq_ref, dgq, first)
        dk_raw, dgk = _pair_norm_bwd(k_ref[...], gk_ref[...], dk_s[...], is_a)
        dk_ref[...] = dk_raw.astype(BF16)
        _accumulate(dgk_ref, dgk, first)
        dv_ref[...] = dv_s[...].astype(BF16)
        for a in range(2):
            for i in range(nq):
                rows = slice(i * t, (i + 1) * t)
                dc_ref[a, :, rows] = _as_row(-jnp.sum(dcs_s[a, rows, :], axis=1, keepdims=True))

        @pl.when(last)
        def _():
            _fold_row(dgq_ref)
            _fold_row(dgk_ref)

    def cols(off):
        return pl.BlockSpec((S, LANES), lambda p: (0, off + p))

    rowv = pl.BlockSpec((2, 1, S), lambda p: (p, 0, 0))
    gain = pl.BlockSpec((1, LANES), lambda p: (0, 0))
    wide = jax.ShapeDtypeStruct((S, H * HEAD_DIM), BF16)
    gs = jax.ShapeDtypeStruct((1, LANES), F32)
    return _pcall(body, name=name, grid=(P,),
                  in_specs=[cols(0), cols(P), cols(2 * P), rowv, gain, gain, rowv, cols(0)],
                  out_specs=[cols(0), cols(0), cols(0), rowv, gain, gain],
                  out_shape=[wide, wide, wide, jax.ShapeDtypeStruct((H, 1, S), F32), gs, gs],
                  scratch_shapes=[pltpu.VMEM((2, S, LANES), BF16), pltpu.VMEM((S, LANES), BF16), pltpu.VMEM((2, LANES, S), BF16),
                                  pltpu.VMEM((S, LANES), BF16), pltpu.VMEM((2, S, LANES), BF16)]
                  + [pltpu.VMEM((S, LANES), F32)] * 3 + [pltpu.VMEM((2, S, LANES), F32), pltpu.VMEM((2, S, 1), F32)],
                  compiler_params=_params(("arbitrary",)))(proj, proj, proj, c_row, gq2, gk2, lse_row, do)


def _bucket_onehot():
    W = WINDOW
    dist = np.arange(W)[:, None] + W - np.arange(2 * W)[None, :]
    n = np.maximum(dist, 0)
    max_exact = N_BUCKETS // 2
    large = max_exact + (np.log(np.maximum(n, 1) / max_exact) / np.log(REL_MAX_DIST / max_exact)
                         * (N_BUCKETS - max_exact)).astype(np.int32)
    large = np.minimum(large, N_BUCKETS - 1)
    bucket = np.where(n < max_exact, n, large).astype(np.int32)
    valid = (dist >= 0) & (dist < W)
    onehot = (bucket[None] == np.arange(N_BUCKETS)[:, None, None]) & valid[None]
    return onehot.reshape(N_BUCKETS, W * 2 * W).astype(np.float32)


def _bias_expand(rel_bias_t, onehot, name, tn=4096):
    HQ, NB = rel_bias_t.shape
    L = onehot.shape[1]

    def body(r_ref, oh_ref, out_ref):
        out_ref[...] = _dot(r_ref[...], oh_ref[...].astype(F32), NN, precision=lax.Precision.HIGHEST)

    return _pcall(body, name=name, grid=(L // tn,),
                  in_specs=[pl.BlockSpec((HQ, NB), lambda i: (0, 0)), pl.BlockSpec((NB, tn), lambda i: (0, i))],
                  out_specs=pl.BlockSpec((HQ, tn), lambda i: (0, i)),
                  out_shape=jax.ShapeDtypeStruct((HQ, L), F32), compiler_params=_params(("parallel",)))(rel_bias_t, onehot)


def _bias_reduce(dbias, onehot, name, tk=4096):
    HQ, L = dbias.shape
    NB = onehot.shape[0]

    def body(d_ref, oh_ref, out_ref):
        part = _dot(d_ref[...], oh_ref[...].astype(F32), NT, precision=lax.Precision.HIGHEST)
        _accumulate(out_ref, part, pl.program_id(0) == 0)

    return _pcall(body, name=name, grid=(L // tk,),
                  in_specs=[pl.BlockSpec((HQ, tk), lambda i: (0, i)), pl.BlockSpec((NB, tk), lambda i: (0, i))],
                  out_specs=pl.BlockSpec((HQ, NB), lambda i: (0, 0)),
                  out_shape=jax.ShapeDtypeStruct((HQ, NB), F32), compiler_params=_params(("arbitrary",)))(dbias, onehot)


def _stacked_query_index(n_rows_or_cols_axis, shape):
    idx = lax.broadcasted_iota(jnp.int32, shape, n_rows_or_cols_axis)
    return jnp.where(idx >= WINDOW, idx - WINDOW, idx)


def _swa_fwd(qproj, kk, vv, bias_ab, sink_col, gq2, gk2, name):
    S, HQD = qproj.shape
    KVH = kk.shape[0]
    PP = HQD // LANES
    NP = PP // KVH
    W = WINDOW
    nb = S // W

    def body(q_ref, k_ref, v_ref, bias_ref, sink_ref, gq_ref, gk_ref, o_ref, lse_ref, qs_s, kb_s, vb_s):
        is_a = _lane_is_a()
        qn = _pair_norm(q_ref[...], gq_ref[...], is_a) * 0.125
        qs_s[0] = jnp.where(is_a, qn, 0.0).astype(BF16)
        qs_s[1] = jnp.where(is_a, 0.0, qn).astype(BF16)
        kb_s[...] = _pair_norm(k_ref[...], gk_ref[...], is_a).astype(BF16)
        vb_s[...] = v_ref[...].astype(BF16)
        sink = sink_ref[...]
        qi1 = _stacked_query_index(0, (2 * W, W))
        first_valid = lax.broadcasted_iota(jnp.int32, (2 * W, W), 1) <= qi1
        qi2 = _stacked_query_index(0, (2 * W, 2 * W))
        key2 = lax.broadcasted_iota(jnp.int32, (2 * W, 2 * W), 1)
        band_valid = (key2 > qi2) & (key2 <= qi2 + W)
        for n in range(nb):
            rows = slice(n * W, (n + 1) * W)
            keys = slice(0, W) if n == 0 else slice((n - 1) * W, (n + 1) * W)
            lhs = jnp.concatenate([qs_s[0, rows, :], qs_s[1, rows, :]], axis=0)
            s = _dot(lhs, kb_s[keys, :], NT) + (bias_ref[:, W:2 * W] if n == 0 else bias_ref[...])
            s = jnp.where(first_valid if n == 0 else band_valid, s, NEG)
            m = jnp.maximum(jnp.max(s, axis=-1, keepdims=True), sink)
            e = jnp.exp(s - m)
            l = jnp.sum(e, axis=-1, keepdims=True) + jnp.exp(sink - m)
            o_ab = _dot(e.astype(BF16), vb_s[keys, :], NN) / l
            o_ref[rows, :] = jnp.where(is_a, o_ab[0:W, :], o_ab[W:2 * W, :]).astype(BF16)
            lse_ref[n] = _as_row(m + jnp.log(l))

    qcols = pl.BlockSpec((S, LANES), lambda a, g: (0, a * NP + g))
    kvs = pl.BlockSpec((None, S, LANES), lambda a, g: (a, 0, 0))
    gain = pl.BlockSpec((1, LANES), lambda a, g: (0, 0))
    return _pcall(body, name=name, grid=(KVH, NP),
                  in_specs=[qcols, kvs, kvs, pl.BlockSpec((None, 2 * W, 2 * W), lambda a, g: (a * NP + g, 0, 0)),
                            pl.BlockSpec((None, 2 * W, 1), lambda a, g: (a * NP + g, 0, 0)), gain, gain],
                  out_specs=[qcols, pl.BlockSpec((None, nb, 1, 2 * W), lambda a, g: (a * NP + g, 0, 0, 0))],
                  out_shape=[jax.ShapeDtypeStruct((S, HQD), BF16), jax.ShapeDtypeStruct((PP, nb, 1, 2 * W), F32)],
                  scratch_shapes=[pltpu.VMEM((2, S, LANES), BF16), pltpu.VMEM((S, LANES), BF16), pltpu.VMEM((S, LANES), BF16)],
                  compiler_params=_params(("parallel", "parallel")))(qproj, kk, vv, bias_ab, sink_col, gq2, gk2)


def _swa_bwd(qproj, kk, vv, bias_t_ab, sink_row, gq2, gk2, lse_row, do, name):
    S, HQD = qproj.shape
    KVH = kk.shape[0]
    PP = HQD // LANES
    NP = PP // KVH
    W = WINDOW
    nb = S // W

    def body(q_ref, k_ref, v_ref, bias_ref, sink_ref, gq_ref, gk_ref, lr_ref, do_ref,
             dq_ref, dk_ref, dv_ref, db_ref, dsink_ref, dgq_ref, dgk_ref,
             qs_s, kb_s, kt_s, vb_s, dob_s, dq_s, dk_s, dv_s):
        a, g = pl.program_id(0), pl.program_id(1)
        is_a = _lane_is_a()
        qn = _pair_norm(q_ref[...], gq_ref[...], is_a) * 0.125
        qs_s[0] = jnp.where(is_a, qn, 0.0).astype(BF16)
        qs_s[1] = jnp.where(is_a, 0.0, qn).astype(BF16)
        kn = _pair_norm(k_ref[...], gk_ref[...], is_a)
        kb_s[...] = kn.astype(BF16)
        kt_s[...] = kn.T.astype(BF16)
        vb_s[...] = v_ref[...].astype(BF16)
        dov = do_ref[...]
        dob_s[0] = jnp.where(is_a, dov, 0.0).astype(BF16)
        dob_s[1] = jnp.where(is_a, 0.0, dov).astype(BF16)
        sink = sink_ref[...]

        @pl.when(g == 0)
        def _():
            dk_s[...] = jnp.zeros((S, LANES), F32)
            dv_s[...] = jnp.zeros((S, LANES), F32)

        qi1 = _stacked_query_index(1, (W, 2 * W))
        first_valid = lax.broadcasted_iota(jnp.int32, (W, 2 * W), 0) <= qi1
        qi2 = _stacked_query_index(1, (2 * W, 2 * W))
        key2 = lax.broadcasted_iota(jnp.int32, (2 * W, 2 * W), 0)
        band_valid = (key2 > qi2) & (key2 <= qi2 + W)
        head_rows = lax.broadcasted_iota(jnp.int32, (LANES, W), 0) < HEAD_DIM
        db = jnp.zeros((2 * W, 2 * W), F32)
        dsk = jnp.zeros((1, 2 * W), F32)
        pend_k = pend_v = None
        for n in range(nb):
            rows = slice(n * W, (n + 1) * W)
            keys = slice(0, W) if n == 0 else slice((n - 1) * W, (n + 1) * W)
            lhs_q = jnp.concatenate([qs_s[0, rows, :], qs_s[1, rows, :]], axis=0)
            lhs_do = jnp.concatenate([dob_s[0, rows, :], dob_s[1, rows, :]], axis=0)
            lse = lr_ref[n]
            s_t = _dot(kb_s[keys, :], lhs_q, NT) + (bias_ref[W:2 * W, :] if n == 0 else bias_ref[...])
            p_t = jnp.where(first_valid if n == 0 else band_valid, jnp.exp(s_t - lse), 0.0)
            dp_t = _dot(vb_s[keys, :], lhs_do, NT)
            delta = jnp.sum(p_t * dp_t, axis=0, keepdims=True)
            ds_t = p_t * (dp_t - delta)
            dsb = ds_t.astype(BF16)
            dsk = dsk - jnp.exp(sink - lse) * delta
            dv_band = _dot(p_t.astype(BF16), lhs_do, NN)
            dk_band = _dot(dsb, lhs_q, NN)
            dq_t = _dot(kt_s[:, keys], dsb, NN)
            dq_s[rows, :] = jnp.where(head_rows, dq_t[:, 0:W], dq_t[:, W:2 * W]).T
            if n == 0:
                db = jnp.concatenate([jnp.zeros((W, 2 * W), F32), ds_t], axis=0)
                pend_k, pend_v = dk_band, dv_band
            else:
                db = db + ds_t
                prev = slice((n - 1) * W, n * W)
                dk_s[prev, :] += pend_k + dk_band[0:W, :]
                dv_s[prev, :] += pend_v + dv_band[0:W, :]
                pend_k, pend_v = dk_band[W:2 * W, :], dv_band[W:2 * W, :]
        tail = slice((nb - 1) * W, nb * W)
        dk_s[tail, :] += pend_k
        dv_s[tail, :] += pend_v
        db_ref[...] = db
        dsink_ref[0] = jnp.broadcast_to(jnp.sum(dsk[:, 0:W], axis=1, keepdims=True), (1, LANES))
        dsink_ref[1] = jnp.broadcast_to(jnp.sum(dsk[:, W:2 * W], axis=1, keepdims=True), (1, LANES))
        dq_raw, dgq = _pair_norm_bwd(q_ref[...], gq_ref[...], dq_s[...] * 0.125, is_a)
        dq_ref[...] = dq_raw.astype(BF16)
        _accumulate(dgq_ref, dgq, jnp.logical_and(a == 0, g == 0))

        @pl.when(jnp.logical_and(a == KVH - 1, g == NP - 1))
        def _():
            _fold_row(dgq_ref)

        @pl.when(g == NP - 1)
        def _():
            dk_raw, dgk = _pair_norm_bwd(k_ref[...], gk_ref[...], _fold_heads(dk_s[...]), is_a)
            dk_ref[...] = dk_raw
            _accumulate(dgk_ref, dgk, a == 0)
            dv_ref[...] = _fold_heads(dv_s[...])

    qcols = pl.BlockSpec((S, LANES), lambda a, g: (0, a * NP + g))
    kvs = pl.BlockSpec((None, S, LANES), lambda a, g: (a, 0, 0))
    sq = pl.BlockSpec((None, 2 * W, 2 * W), lambda a, g: (a * NP + g, 0, 0))
    gain = pl.BlockSpec((1, LANES), lambda a, g: (0, 0))
    ks = jax.ShapeDtypeStruct((KVH, S, LANES), F32)
    gs = jax.ShapeDtypeStruct((1, LANES), F32)
    return _pcall(body, name=name, grid=(KVH, NP),
                  in_specs=[qcols, kvs, kvs, sq, pl.BlockSpec((None, 1, 2 * W), lambda a, g: (a * NP + g, 0, 0)), gain, gain,
                            pl.BlockSpec((None, nb, 1, 2 * W), lambda a, g: (a * NP + g, 0, 0, 0)), qcols],
                  out_specs=[qcols, kvs, kvs, sq, pl.BlockSpec((2, 1, LANES), lambda a, g: (a * NP + g, 0, 0)), gain, gain],
                  out_shape=[jax.ShapeDtypeStruct((S, HQD), BF16), ks, ks, jax.ShapeDtypeStruct((PP, 2 * W, 2 * W), F32),
                             jax.ShapeDtypeStruct((2 * PP, 1, LANES), F32), gs, gs],
                  scratch_shapes=[pltpu.VMEM((2, S, LANES), BF16), pltpu.VMEM((S, LANES), BF16), pltpu.VMEM((LANES, S), BF16),
                                  pltpu.VMEM((S, LANES), BF16), pltpu.VMEM((2, S, LANES), BF16)] + [pltpu.VMEM((S, LANES), F32)] * 3,
                  compiler_params=_params(("arbitrary", "arbitrary")))(qproj, kk, vv, bias_t_ab, sink_row, gq2, gk2, lse_row, do)


def _adamw_update(w, g, m, v):
    m2 = ADAM_B1 * m + (1.0 - ADAM_B1) * g
    v2 = ADAM_B2 * v + (1.0 - ADAM_B2) * jnp.square(g)
    m_hat = m2 / (1.0 - ADAM_B1 ** ADAM_STEP)
    v_hat = v2 / (1.0 - ADAM_B2 ** ADAM_STEP)
    return -ADAM_LR * (m_hat / (jnp.sqrt(v_hat) + ADAM_EPS) + ADAM_WD * w), m2, v2


def _adamw(w, g, m, v, name, tr=256, tc=256):
    R, C = w.shape
    tr = min(tr, R)
    if R % tr == 0:
        grid, blk = (R // tr,), pl.BlockSpec((tr, C), lambda i: (i, 0))
    else:
        assert C % tc == 0
        grid, blk = (C // tc,), pl.BlockSpec((R, tc), lambda i: (0, i))

    def body(w_ref, g_ref, m_ref, v_ref, d_ref, m2_ref, v2_ref):
        d_ref[...], m2_ref[...], v2_ref[...] = _adamw_update(w_ref[...], g_ref[...], m_ref[...], v_ref[...])

    return _pcall(body, name=name, grid=grid, in_specs=[blk] * 4, out_specs=[blk] * 3,
                  out_shape=[jax.ShapeDtypeStruct((R, C), F32)] * 3, compiler_params=_params(("parallel",)))(w, g, m, v)


def _adamw_two_layers(w, g0, g1, m, v, name, tr=256):
    R, C = g0.shape
    assert R % tr == 0 and w.shape == (2 * R, C)
    nr = R // tr

    def body(w_ref, g0_ref, g1_ref, m_ref, v_ref, g_ref, d_ref, m2_ref, v2_ref):
        g = jnp.where(pl.program_id(0) == 0, g0_ref[...], g1_ref[...])
        g_ref[...] = g
        d_ref[...], m2_ref[...], v2_ref[...] = _adamw_update(w_ref[...], g, m_ref[...], v_ref[...])

    both = pl.BlockSpec((tr, C), lambda l, i: (l * nr + i, 0))
    first = pl.BlockSpec((tr, C), lambda l, i: (i * (1 - l) + (nr - 1) * l, 0))
    second = pl.BlockSpec((tr, C), lambda l, i: (i * l, 0))
    return _pcall(body, name=name, grid=(2, nr), in_specs=[both, first, second, both, both], out_specs=[both] * 4,
                  out_shape=[jax.ShapeDtypeStruct((2 * R, C), F32)] * 4,
                  compiler_params=_params(("arbitrary", "arbitrary")))(w, g0, g1, m, v)


def _sum_core_pair(arr, got, place, name, tr=512):
    P, hr, C = got.shape
    tr = tr if hr % tr == 0 else hr
    nt = hr // tr

    def body(place_ref, a_ref, g_ref, o_ref):
        o_ref[...] = (a_ref[...].astype(F32) + g_ref[...].astype(F32)).astype(o_ref.dtype)

    spec = pltpu.PrefetchScalarGridSpec(
        num_scalar_prefetch=1, grid=(P, nt),
        in_specs=[pl.BlockSpec((None, tr, C), lambda j, i, pr: (j, pr[1] * nt + i, 0)),
                  pl.BlockSpec((None, tr, C), lambda j, i, pr: (j, i, 0))],
        out_specs=pl.BlockSpec((None, tr, C), lambda j, i, pr: (j, i, 0)))
    return _pcall(body, name=name, grid_spec=spec, out_shape=jax.ShapeDtypeStruct(got.shape, BF16),
                  compiler_params=_params(("parallel", "parallel")))(place, arr, got)


def _sum_chips(pair, landed, place, name, tr=256):
    _, R, C = landed.shape
    tr = tr if R % tr == 0 else R

    def body(place_ref, p_ref, l_ref, o_ref):
        acc = p_ref[...].astype(F32)
        for k in range(3):
            acc = acc + l_ref[k].astype(F32)
        o_ref[...] = acc

    spec = pltpu.PrefetchScalarGridSpec(
        num_scalar_prefetch=1, grid=(R // tr,),
        in_specs=[pl.BlockSpec((None, tr, C), lambda i, pr: (pr[0], i, 0)), pl.BlockSpec((3, tr, C), lambda i, pr: (0, i, 0))],
        out_specs=pl.BlockSpec((None, tr, C), lambda i, pr: (pr[1], i, 0)))
    return _pcall(body, name=name, grid_spec=spec, out_shape=jax.ShapeDtypeStruct((2, R, C), F32),
                  compiler_params=_params(("parallel",)))(place, pair, landed)


def _place():
    x, y, c = lax.axis_index("x"), lax.axis_index("y"), lax.axis_index("c")
    others = [(1 - x, y), (x, 1 - y), (1 - x, 1 - y)]
    return x, y, c, others


def _half_rows(ref, hh, lead=()):
    hr = ref.shape[-2] // 2
    return ref.at[(*lead, pl.ds(pl.multiple_of(hh * hr, 16), hr), slice(None))]


def _sem_arrays(*counts):
    return [pltpu.SemaphoreType.DMA((k,)) for k in counts]


SEM_SPEC = pl.BlockSpec(memory_space=pltpu.SEMAPHORE)
ANY_SPEC = pl.BlockSpec(memory_space=pl.ANY)
DATAFLOW = pltpu.SideEffectType.DATAFLOW_SIDE_EFFECTING


def _in_hbm(a):
    return pltpu.with_memory_space_constraint(a, pltpu.HBM)


def _gather_copies(srcs, lands, send_sems, recv_sems):
    x, y, c, others = _place()
    me = 2 * x + y

    def copy(w, k, dst_chip, to):
        return pltpu.make_async_remote_copy(src_ref=_half_rows(srcs[w], c), dst_ref=_half_rows(lands[w], c, (dst_chip,)),
                                            send_sem=send_sems.at[3 * w + k], recv_sem=recv_sems.at[3 * w + k],
                                            device_id=to, device_id_type=MESH)

    pairs = [(w, k, cx, cy) for w in range(len(srcs)) for k, (cx, cy) in enumerate(others)]
    return ([copy(w, k, me, (cx, cy, c)) for w, k, cx, cy in pairs],
            [copy(w, k, 2 * cx + cy, (cx, cy, c)) for w, k, cx, cy in pairs])


def _gather_start(shards, after, name):
    n = len(shards)

    def body(*refs):
        srcs, lands, send_sems, recv_sems, token = refs[:n], refs[n:2 * n], refs[2 * n + 1], refs[2 * n + 2], refs[-1]
        for cp in _gather_copies(srcs, lands, send_sems, recv_sems)[0]:
            cp.start()
        token[...] = jnp.zeros_like(token)

    lands = [lax.empty((N_CHIPS,) + s.shape, s.dtype) for s in shards]
    outs = _pcall(
        body, name=name, in_specs=[HBM_SPEC] * (2 * n) + [ANY_SPEC],
        out_specs=[SEM_SPEC, SEM_SPEC] + [HBM_SPEC] * (2 * n) + [VMEM_SPEC],
        out_shape=[pltpu.SemaphoreType.DMA((3 * n,)), pltpu.SemaphoreType.DMA((3 * n,))]
        + [pltpu.HBM(a.shape, a.dtype) for a in list(shards) + lands] + [jax.ShapeDtypeStruct((8, LANES), F32)],
        input_output_aliases={i: 2 + i for i in range(2 * n)},
        compiler_params=pltpu.CompilerParams(has_side_effects=DATAFLOW),
    )(*[_in_hbm(a) for a in list(shards) + lands], after)
    return outs[0], outs[1], outs[2:2 + n], outs[2 + n:2 + 2 * n], outs[-1]


def _gather_wait(started, after, name):
    send_sems, recv_sems, srcs, lands, _ = started
    n = len(srcs)

    def body(*refs):
        src_refs, land_refs, send_ref, recv_ref = refs[:n], refs[n:2 * n], refs[2 * n], refs[2 * n + 1]
        outgoing, incoming = _gather_copies(src_refs, land_refs, send_ref, recv_ref)
        for out_cp, in_cp in zip(outgoing, incoming):
            out_cp.wait_send()
            in_cp.wait_recv()

    outs = _pcall(
        body, name=name, in_specs=[HBM_SPEC] * (2 * n) + [SEM_SPEC, SEM_SPEC, ANY_SPEC], out_specs=[HBM_SPEC] * (2 * n),
        out_shape=[pltpu.HBM(a.shape, a.dtype) for a in list(srcs) + list(lands)],
        input_output_aliases={i: i for i in range(2 * n)},
        compiler_params=pltpu.CompilerParams(has_side_effects=DATAFLOW),
    )(*srcs, *lands, send_sems, recv_sems, after)
    return outs[:n], outs[n:]


def _gather_pass_on(shards, lands, name):
    n = len(shards)
    per = 4

    def body(*refs):
        srcs, bufs = refs[:n], refs[2 * n:3 * n]
        send_sems, recv_sems = refs[3 * n:]
        x, y, c, others = _place()
        me = 2 * x + y
        sibling = (x, y, 1 - c)

        def copy(w, k, src, dst):
            return pltpu.make_async_remote_copy(src_ref=src, dst_ref=dst, send_sem=send_sems.at[per * w + k],
                                                recv_sem=recv_sems.at[per * w + k], device_id=sibling, device_id_type=MESH)

        sends, recvs = [], []
        for w in range(n):
            for k, (cx, cy) in enumerate(others):
                mine, theirs = _half_rows(bufs[w], c, (2 * cx + cy,)), _half_rows(bufs[w], 1 - c, (2 * cx + cy,))
                sends.append(copy(w, k, mine, mine))
                recvs.append(copy(w, k, theirs, theirs))
            sends.append(copy(w, 3, srcs[w], bufs[w].at[me]))
            recvs.append(sends[-1])
        for cp in sends:
            cp.start()
        for snd, rcv in zip(sends, recvs):
            snd.wait_send()
            rcv.wait_recv()

    return _pcall(body, name=name, in_specs=[HBM_SPEC] * (2 * n), out_specs=[HBM_SPEC] * n,
                  out_shape=[jax.ShapeDtypeStruct(l.shape, l.dtype) for l in lands],
                  input_output_aliases={n + w: w for w in range(n)},
                  scratch_shapes=_sem_arrays(per * n, per * n))(*shards, *lands)


def _scatter_copies(srcs, lands, send_sems, recv_sems):
    x, y, c, others = _place()
    return [pltpu.make_async_remote_copy(src_ref=srcs[w].at[2 * cx + cy], dst_ref=lands[w].at[k],
                                         send_sem=send_sems.at[3 * w + k], recv_sem=recv_sems.at[3 * w + k],
                                         device_id=(cx, cy, c), device_id_type=MESH)
            for w in range(len(srcs)) for k, (cx, cy) in enumerate(others)]


def _scatter_start(parts, name):
    n = len(parts)

    def body(*refs):
        srcs, lands, send_sems, recv_sems, token = refs[:n], refs[n:2 * n], refs[2 * n], refs[2 * n + 1], refs[-1]
        for cp in _scatter_copies(srcs, lands, send_sems, recv_sems):
            cp.start()
        token[...] = jnp.zeros_like(token)

    lands = [lax.empty((3,) + p.shape[1:], p.dtype) for p in parts]
    outs = _pcall(
        body, name=name, in_specs=[HBM_SPEC] * (2 * n), out_specs=[SEM_SPEC, SEM_SPEC] + [HBM_SPEC] * (2 * n) + [VMEM_SPEC],
        out_shape=[pltpu.SemaphoreType.DMA((3 * n,)), pltpu.SemaphoreType.DMA((3 * n,))]
        + [pltpu.HBM(a.shape, a.dtype) for a in list(parts) + lands] + [jax.ShapeDtypeStruct((8, LANES), F32)],
        input_output_aliases={i: 2 + i for i in range(2 * n)},
        compiler_params=pltpu.CompilerParams(has_side_effects=DATAFLOW),
    )(*[_in_hbm(a) for a in list(parts) + lands])
    return (outs[0], outs[1], outs[2:2 + n], outs[2 + n:2 + 2 * n]), outs[-1]


def _scatter_wait(started, after, name):
    send_sems, recv_sems, srcs, lands = started
    n = len(srcs)

    def body(*refs):
        for cp in _scatter_copies(refs[:n], refs[n:2 * n], refs[2 * n], refs[2 * n + 1]):
            cp.wait_send()
            cp.wait_recv()

    outs = _pcall(
        body, name=name, in_specs=[HBM_SPEC] * (2 * n) + [SEM_SPEC, SEM_SPEC, ANY_SPEC], out_specs=[HBM_SPEC] * (2 * n),
        out_shape=[pltpu.HBM(a.shape, a.dtype) for a in list(srcs) + list(lands)],
        input_output_aliases={i: i for i in range(2 * n)},
        compiler_params=pltpu.CompilerParams(has_side_effects=DATAFLOW),
    )(*srcs, *lands, send_sems, recv_sems, after)
    return outs[:n], outs[n:]


def _split_start(plan, arrays, n_copies, after, name):
    n = len(arrays)

    def body(*refs):
        for cp in plan(refs[:n], refs[n + 1], refs[n + 2])[0]:
            cp.start()
        refs[-1][...] = jnp.zeros_like(refs[-1])

    outs = _pcall(
        body, name=name, in_specs=[HBM_SPEC] * n + [ANY_SPEC], out_specs=[SEM_SPEC, SEM_SPEC] + [HBM_SPEC] * n + [VMEM_SPEC],
        out_shape=[pltpu.SemaphoreType.DMA((n_copies,)), pltpu.SemaphoreType.DMA((n_copies,))]
        + [pltpu.HBM(a.shape, a.dtype) for a in arrays] + [jax.ShapeDtypeStruct((8, LANES), F32)],
        input_output_aliases={i: 2 + i for i in range(n)},
        compiler_params=pltpu.CompilerParams(has_side_effects=DATAFLOW),
    )(*[_in_hbm(a) for a in arrays], after)
    return (outs[0], outs[1], outs[2:2 + n]), outs[-1]


def _split_wait(plan, started, after, name):
    send_sems, recv_sems, arrays = started
    n = len(arrays)

    def body(*refs):
        outgoing, incoming = plan(refs[:n], refs[n], refs[n + 1])
        for cp in outgoing:
            cp.wait_send()
        for cp in incoming:
            cp.wait_recv()

    return _pcall(
        body, name=name, in_specs=[HBM_SPEC] * n + [SEM_SPEC, SEM_SPEC, ANY_SPEC], out_specs=[HBM_SPEC] * n,
        out_shape=[pltpu.HBM(a.shape, a.dtype) for a in arrays], input_output_aliases={i: i for i in range(n)},
        compiler_params=pltpu.CompilerParams(has_side_effects=DATAFLOW),
    )(*arrays, send_sems, recv_sems, after)


def _to_sibling(src, dst, k, send_sems, recv_sems):
    x, y, c, _ = _place()
    return pltpu.make_async_remote_copy(src_ref=src, dst_ref=dst, send_sem=send_sems.at[k], recv_sem=recv_sems.at[k],
                                        device_id=(x, y, 1 - c), device_id_type=MESH)


def _plan_pass_on(n):
    def plan(refs, send_sems, recv_sems):
        x, y, c, others = _place()
        cps = []
        for w in range(n):
            for k, (cx, cy) in enumerate(others):
                mine = _half_rows(refs[n + w], c, (2 * cx + cy,))
                cps.append(_to_sibling(mine, mine, 4 * w + k, send_sems, recv_sems))
            cps.append(_to_sibling(refs[w], refs[n + w].at[2 * x + y], 4 * w + 3, send_sems, recv_sems))
        return cps, cps
    return plan


def _plan_swap_halves(n):
    def plan(refs, send_sems, recv_sems):
        c = lax.axis_index("c")
        cps = [_to_sibling(_half_rows(refs[w], 1 - c, (slice(None),)), refs[n + w], w, send_sems, recv_sems) for w in range(n)]
        return cps, cps
    return plan


def _plan_share_halves(n):
    def plan(refs, send_sems, recv_sems):
        c = lax.axis_index("c")
        cps = [_to_sibling(refs[w].at[c], refs[w].at[c], w, send_sems, recv_sems) for w in range(n)]
        return cps, cps
    return plan


def _plan_gather_small(refs, send_sems, recv_sems):
    x, y, c, _ = _place()
    flips = [(dx, dy, dc) for dx in (0, 1) for dy in (0, 1) for dc in (0, 1)][1:]
    flip = lambda v, d: 1 - v if d else v
    cps = [pltpu.make_async_remote_copy(src_ref=refs[0], dst_ref=refs[1].at[4 * x + 2 * y + c], send_sem=send_sems.at[k],
                                        recv_sem=recv_sems.at[k], device_id=(flip(x, dx), flip(y, dy), flip(c, dc)),
                                        device_id_type=MESH)
           for k, (dx, dy, dc) in enumerate(flips)]
    return cps, cps


def _plan_scatter_direct(n):
    def plan(refs, send_sems, recv_sems):
        x, y, c, _ = _place()
        flips = [(dx, dy, dc) for dx in (0, 1) for dy in (0, 1) for dc in (0, 1)][1:]
        flip = lambda v, d: 1 - v if d else v
        cps = []
        for w in range(n):
            for r, (dx, dy, dc) in enumerate(flips):
                tx, ty, tc = flip(x, dx), flip(y, dy), flip(c, dc)
                cps.append(pltpu.make_async_remote_copy(
                    src_ref=_half_rows(refs[w], tc, (2 * tx + ty,)), dst_ref=refs[n + w].at[r],
                    send_sem=send_sems.at[7 * w + r], recv_sem=recv_sems.at[7 * w + r],
                    device_id=(tx, ty, tc), device_id_type=MESH))
        return cps, cps
    return plan


def _sum_direct(mine, landed, place, name, tr=256):
    _, R, C = landed.shape
    tr = tr if R % tr == 0 else R
    nt = R // tr

    def body(place_ref, m_ref, l_ref, o_ref):
        acc = m_ref[...].astype(F32)
        for k in range(7):
            acc = acc + l_ref[k].astype(F32)
        o_ref[...] = acc

    spec = pltpu.PrefetchScalarGridSpec(
        num_scalar_prefetch=1, grid=(nt,),
        in_specs=[pl.BlockSpec((None, tr, C), lambda i, pr: (pr[0], pr[1] * nt + i, 0)),
                  pl.BlockSpec((7, tr, C), lambda i, pr: (0, i, 0))],
        out_specs=pl.BlockSpec((None, tr, C), lambda i, pr: (pr[1], i, 0)))
    return _pcall(body, name=name, grid_spec=spec, out_shape=jax.ShapeDtypeStruct((2, R, C), F32),
                  compiler_params=_params(("parallel",)))(place, mine, landed)


def _sum_gathered_small(gathered, own, place, name):
    _, M, C = gathered.shape

    def body(place_ref, g_ref, own_ref, o_ref):
        me = 2 * place_ref[0] + place_ref[1]
        acc = jnp.zeros((M, C), F32)
        for k in range(8):
            acc = acc + jnp.where(me == k, own_ref[...], g_ref[k])
        o_ref[...] = acc

    spec = pltpu.PrefetchScalarGridSpec(
        num_scalar_prefetch=1, grid=(1,),
        in_specs=[pl.BlockSpec((8, M, C), lambda i, pr: (0, 0, 0)), pl.BlockSpec((M, C), lambda i, pr: (0, 0))],
        out_specs=pl.BlockSpec((M, C), lambda i, pr: (0, 0)))
    return _pcall(body, name=name, grid_spec=spec, out_shape=jax.ShapeDtypeStruct((M, C), F32),
                  compiler_params=_params(("arbitrary",)))(place, gathered, own)


def _swap_halves_group(arrs, name):
    n = len(arrs)

    def body(*refs):
        ins, gots = refs[:n], refs[n:2 * n]
        send_sems, recv_sems = refs[2 * n:]
        x, y, c, _ = _place()
        swaps = [pltpu.make_async_remote_copy(src_ref=_half_rows(ins[w], 1 - c, (slice(None),)), dst_ref=gots[w],
                                              send_sem=send_sems.at[w], recv_sem=recv_sems.at[w],
                                              device_id=(x, y, 1 - c), device_id_type=MESH) for w in range(n)]
        for cp in swaps:
            cp.start()
        for cp in swaps:
            cp.wait()

    half_shapes = [jax.ShapeDtypeStruct((a.shape[0], a.shape[1] // 2, a.shape[2]), a.dtype) for a in arrs]
    return _pcall(body, name=name, in_specs=[HBM_SPEC] * n, out_specs=[HBM_SPEC] * n, out_shape=half_shapes,
                  scratch_shapes=_sem_arrays(n, n))(*arrs)


def _pack_rows(n_elems, width, align):
    rows = -(-n_elems // width)
    return -(-rows // align) * align


def _pack(arrays, dtype, width, align):
    flat = jnp.concatenate([a.astype(dtype).reshape(-1) for a in arrays])
    rows = _pack_rows(flat.shape[0], width, align)
    flat = jnp.pad(flat, (0, rows * width - flat.shape[0]))
    return flat.reshape(rows, width)


def _pack_small(arrays):
    return _pack(arrays, F32, width=128, align=8)


def _unpack(flat, shapes):
    out, off = [], 0
    for shp in shapes:
        n = int(np.prod(shp))
        out.append(flat[..., off:off + n].reshape(flat.shape[:-1] + tuple(shp)))
        off += n
    return out


def _doubled_heads(x2d, n_heads):
    S = x2d.shape[0]
    h = x2d.reshape(S, n_heads, HEAD_DIM).transpose(1, 0, 2)
    return jnp.concatenate([h, h], axis=-1)


def _rms_bwd_epilogue(dn, x, dres, g):
    r = _rinv(x)
    xh = x * r
    dxh = dn * g
    dx = dres + r * (dxh - xh * jnp.mean(dxh * xh, axis=-1, keepdims=True))
    return dx, dx, jnp.sum(dn * xh, axis=0, keepdims=True)


def _residual_then_norms(acc, res, *gains):
    h = res + acc
    hn = h * _rinv(h)
    return (h,) + tuple(hn * g for g in gains)


def _mlp_fwd(h, n, w_up4, w_down, next_gains, tag, between=None):
    u, a = _matmul(n, w_up4, "nn", f"up{tag}", out_dtypes=(F32, BF16), chipwise="b", tm=2048, tn=512,
                   epilogue=lambda acc: (acc, jnp.square(jnp.maximum(acc, 0.0))))
    if between is not None:
        next_gains = [next_gains[0] + between(a)] + list(next_gains[1:])
    assert w_down.shape[1] == 1024
    outs = _matmul(a, w_down, "nn", f"down{tag}", out_dtypes=(F32,) + (BF16,) * len(next_gains), extras=(h,),
                   row_extras=tuple(next_gains), epilogue=_residual_then_norms, tm=1024, tn=1024, tk=1024)
    outs = outs if next_gains else (outs,)
    return outs[0], outs[1:], (n, u, a)


def _mlp_bwd(dh_out, dh_out_b, h, g, w_up4, w_down, saved, tag):
    n, u, a = saved
    dw_down = _matmul(a, dh_out_b, "tn", f"dw_down{tag}", out_dtypes=(BF16,), tm=1024, tn=1024, chipwise="out_rows")
    du = _matmul(dh_out_b, w_down, "nt", f"du{tag}", out_dtypes=(BF16,), extras=(u,), tm=2048, tn=512,
                 epilogue=lambda acc, uu: (acc * (2.0 * jnp.maximum(uu, 0.0)),))
    dw_up = _matmul(n, du, "tn", f"dw_up{tag}", out_dtypes=(BF16,), chipwise="out", tm=1024, tn=512)
    dh, dh_b, dg = _matmul(du, w_up4, "nt", f"dn_mlp{tag}", out_dtypes=(F32, BF16, F32), extras=(h, dh_out), row_extras=(g,),
                           epilogue=_rms_bwd_epilogue, row_accums=1, tm=1024, tn=1024, tk=w_up4.shape[2], chipwise="b")
    return dh, dh_b, dg, dw_up, dw_down


def kernel(x, g_attn, g_mlp, w_in_a, b_f, gq_a, gk_a, w_out_a, g_kv, w_kv, gk_b, w_q_b, gq_b, sinks, rel_bias, w_out_b, w_up, w_down, loss_target, m_g_attn, m_g_mlp, m_w_in_a, m_b_f, m_gq_a, m_gk_a, m_w_out_a, m_g_kv, m_w_kv, m_gk_b, m_w_q_b, m_gq_b, m_sinks, m_rel_bias, m_w_out_b, m_w_up, m_w_down, v_g_attn, v_g_mlp, v_w_in_a, v_b_f, v_gq_a, v_gk_a, v_w_out_a, v_g_kv, v_w_kv, v_gk_b, v_w_q_b, v_gq_b, v_sinks, v_rel_bias, v_w_out_b, v_w_up, v_w_down):
    given = dict(locals())
    S, D = x.shape[1], x.shape[2]
    H = D // HEAD_DIM
    KVH = w_kv.shape[1] // (2 * HEAD_DIM)
    kvw = KVH * HEAD_DIM
    hw = H * HEAD_DIM
    W = WINDOW
    nb = S // W
    c_idx = lax.axis_index("c")
    xs, tgt = x[0], loss_target[0]

    n_in_shard = w_in_a.shape[2]
    rows_in = -(-n_in_shard // 32) * 32
    row_pad = lambda a: jnp.pad(a, [(0, 0)] * (a.ndim - 2) + [(0, rows_in - a.shape[-2]), (0, 0)])
    t_in = lambda a: jnp.swapaxes(a[0], 0, 1)
    shards = {"w_in_a": row_pad(t_in(w_in_a)), "w_out_a": w_out_a[0], "w_up0": w_up[0], "w_down0": w_down[0], "w_kv": w_kv,
              "w_q_b": w_q_b[0], "w_out_b": w_out_b[0], "w_up1": w_up[1], "w_down1": w_down[1]}
    parts = list(shards)
    groups = [("w_in_a", "w_out_a"), ("w_up0", "w_down0"), ("w_kv", "w_q_b", "w_out_b", "w_up1", "w_down1")]
    started = []
    for i, grp in enumerate(groups):
        behind = started[-1][4] if started else g_attn[0]
        started.append(_gather_start([shards[n].astype(BF16) for n in grp], behind, f"gather_start{i}"))
    gathered = {}

    def finish_gather(i, after):
        srcs, lands = _gather_wait(started[i], after, f"gather_wait{i}")
        gathered.update(zip(groups[i], _gather_pass_on(srcs, lands, f"gather_pass_on{i}")))

    def land_gather(i, after):
        srcs, lands = _gather_wait(started[i], after, f"gather_wait{i}")
        n = len(srcs)
        passing, token = _split_start(_plan_pass_on(n), list(srcs) + list(lands), 4 * n, after, f"pass_on_start{i}")
        return passing, token[0:1, 0:1]

    def finish_pass_on(i, passing, after):
        n = len(groups[i])
        gathered.update(zip(groups[i], _split_wait(_plan_pass_on(n), passing, after, f"pass_on_wait{i}")[n:]))

    vec = lambda a: a.reshape(1, -1)
    twice = lambda a: jnp.tile(a.reshape(1, -1), (1, 2))

    g_attn0 = vec(g_attn[0]) + sum(st[4][0, 0] for st in started)
    (n0,) = _rms_fwd(xs, [g_attn0], "rms_attn0")
    finish_gather(0, n0)
    win_t = gathered["w_in_a"][:, :n_in_shard].reshape(-1, D)
    win_t = jnp.pad(win_t, ((0, (-win_t.shape[0]) % 128), (0, 0)))
    wout_a = gathered["w_out_a"].reshape(-1, D)
    n_in = win_t.shape[0]
    tile_in = 640 if n_in % 640 == 0 else 128
    proj = _matmul(n0, win_t, "nt", "proj_in", tm=2048, tn=tile_in)
    zt = proj[:, 3 * hw:3 * hw + H].T
    c_row = _gate_fwd(zt, b_f.reshape(H, 1), "gate_fwd")
    c_row3 = c_row.reshape(H, 1, S)
    o_a, lse_a = _fox_fwd(proj, c_row3, twice(gq_a[0]), twice(gk_a[0]), H, "fox_fwd", t=512)
    passing1, tie = land_gather(1, o_a)
    h1, n1 = _matmul(o_a, wout_a, "nn", "out_a", out_dtypes=(F32, BF16), extras=(xs,), row_extras=(vec(g_mlp[0]) + tie,),
                     epilogue=_residual_then_norms, tm=1024, tn=1024)
    finish_pass_on(1, passing1, n1)
    wup = [gathered["w_up0"], None]
    wdown = [gathered["w_down0"].reshape(-1, D), None]
    passing = []

    def land_last_group(a):
        started_passing, tie = land_gather(2, a)
        passing.append(started_passing)
        return tie

    h2, (nkv, n2), mlp0 = _mlp_fwd(h1, n1, wup[0], wdown[0], [vec(g_kv), vec(g_attn[1])], "0", between=land_last_group)
    passing2 = passing[0]

    finish_pass_on(2, passing2, h2)
    wq_b, wout_b = gathered["w_q_b"].reshape(-1, D), gathered["w_out_b"].reshape(-1, D)
    wkv = gathered["w_kv"].reshape(D, -1)
    wup[1], wdown[1] = gathered["w_up1"], gathered["w_down1"].reshape(-1, D)
    kv = _matmul(nkv, wkv, "nn", "proj_kv", tm=2048)
    kk, vv = _doubled_heads(kv[:, :kvw], KVH), _doubled_heads(kv[:, kvw:], KVH)
    q2 = _matmul(n2, wq_b, "nn", "proj_q", tm=1024, tn=1024)
    onehot = jnp.asarray(_bucket_onehot(), dtype=BF16)
    bias = _bias_expand(rel_bias.T, onehot, "bias_expand").reshape(H, W, 2 * W)
    bias_ab = bias.reshape(H // 2, 2 * W, 2 * W)
    bias_t_ab = bias.reshape(H // 2, 2, W, 2 * W).transpose(0, 3, 1, 2).reshape(H // 2, 2 * W, 2 * W)
    sink_ab = jnp.repeat(sinks[0].reshape(H // 2, 2), W, axis=1)
    o_b, lse_b = _swa_fwd(q2, kk, vv, bias_ab, sink_ab.reshape(H // 2, 2 * W, 1), twice(gq_b[0]), twice(gk_b), "swa_fwd")
    h3, n3 = _matmul(o_b, wout_b, "nn", "out_b", out_dtypes=(F32, BF16), extras=(h2,), row_extras=(vec(g_mlp[1]),),
                     epilogue=_residual_then_norms, tm=1024, tn=1024)
    h4, _, mlp1 = _mlp_fwd(h3, n3, wup[1], wdown[1], [], "1")

    dh4, dh4_b, loss_part = _loss_head(h4, tgt, "loss_head")

    place = jnp.stack([2 * lax.axis_index("x") + lax.axis_index("y"), c_idx]).astype(jnp.int32)
    scattering = []

    def pair_and_scatter(names, mine, got):
        pair_sums = [_sum_core_pair(a, g, place, "sum_core_pair_" + n) for n, a, g in zip(names, mine, got)]
        started_scatter, token = _scatter_start(pair_sums, "scatter_start_" + names[0])
        scattering.append((names, started_scatter))
        return token[0:1, :]

    def start_reduce(named):
        names = list(named)
        mine = [named[n] for n in names]
        lands = [lax.empty((7, a.shape[1] // 2, a.shape[2]), a.dtype) for a in mine]
        started_scatter, token = _split_start(_plan_scatter_direct(len(mine)), mine + lands, 7 * len(mine), mine[0],
                                              "scatter_start_" + names[0])
        scattering.append((names, started_scatter))
        return token[0:1, :]

    def start_swap(named):
        return list(named), None, jnp.broadcast_to(start_reduce(named), (8, LANES))

    def finish_swap(swap, after):
        return swap[2][0:1, :]

    dh3, dh3_b, dg_mlp1, dw_up1, dw_down1 = _mlp_bwd(dh4, dh4_b, h3, vec(g_mlp[1]), wup[1], wdown[1], mlp1, "1")
    swap1 = start_swap({"w_down1": dw_down1, "w_up1": dw_up1})
    do_b = _matmul(dh3_b, wout_b, "nt", "do_b", tm=1024, tn=1024, after=swap1[2])
    dw_out_b = _matmul(o_b, dh3_b, "tn", "dw_out_b", out_dtypes=(BF16,), tm=1024, tn=1024, after=do_b)
    tie1 = finish_swap(swap1, dw_out_b)
    dq2, dk2, dv2, dbias_t_ab, dsink, dgq_b, dgk_b = _swa_bwd(
        q2, kk, vv, bias_t_ab, sink_ab.reshape(H // 2, 1, 2 * W), twice(gq_b[0]) + tie1, twice(gk_b),
        lse_b, do_b, "swa_bwd")
    dbias = dbias_t_ab.reshape(H // 2, 2 * W, 2, W).transpose(0, 2, 3, 1).reshape(H, W * 2 * W)
    d_rel_bias = _bias_reduce(dbias, onehot, "bias_reduce").T
    dw_q_b = _matmul(n2, dq2, "tn", "dw_q_b", out_dtypes=(BF16,), tm=1024, tn=1024)
    dn2 = _matmul(dq2, wq_b, "nt", "dn2", tm=1024, tn=1024)
    dkv = jnp.concatenate([dk2[h, :, :HEAD_DIM] for h in range(KVH)] + [dv2[h, :, :HEAD_DIM] for h in range(KVH)],
                          axis=1).astype(BF16)
    dw_kv = _matmul(nkv, dkv, "tn", "dw_kv", out_dtypes=(BF16,), tm=1024)
    dnkv = _matmul(dkv, wkv, "nt", "dnkv", tm=1024, tn=1024)
    tie2 = start_reduce({"w_out_b": dw_out_b.reshape(N_CHIPS, -1, D), "w_q_b": dw_q_b.reshape(N_CHIPS, -1, D),
                         "w_kv": dw_kv.reshape(N_CHIPS, -1, 2 * kvw)})
    dh2, dh2_b, (dg_kv, dg_attn1) = _rms_bwd(h2, dh3, [vec(g_kv) + tie2[:, :1], vec(g_attn[1])], [dnkv, dn2], "rms_attn1_bwd")

    dh1, dh1_b, dg_mlp0, dw_up0, dw_down0 = _mlp_bwd(dh2, dh2_b, h1, vec(g_mlp[0]), wup[0], wdown[0], mlp0, "0")
    swap3 = start_swap({"w_down0": dw_down0, "w_up0": dw_up0})
    do_a = _matmul(dh1_b, wout_a, "nt", "do_a", tm=1024, tn=1024, after=swap3[2])
    dw_out_a = _matmul(o_a, dh1_b, "tn", "dw_out_a", out_dtypes=(BF16,), tm=1024, tn=1024, after=do_a)
    tie3 = finish_swap(swap3, dw_out_a)
    dq_a, dk_a, dv_a, dc_row, dgq_a, dgk_a = _fox_bwd(
        proj, c_row3, twice(gq_a[0]) + tie3, twice(gk_a[0]), lse_a, do_a, H, "fox_bwd", t=512)
    dzt, db_f = _gate_bwd(dc_row.reshape(H, S), zt, b_f.reshape(H, 1), "gate_bwd")
    dproj = jnp.concatenate([dq_a, dk_a, dv_a, dzt.T.astype(BF16), jnp.zeros((S, n_in - 3 * hw - H), BF16)], axis=1)
    dw_in_t = _matmul(dproj, n0, "tn", "dw_in", out_dtypes=(BF16,), tm=tile_in, tn=1024)
    dw_in4 = row_pad(dw_in_t[:3 * hw + H].reshape(N_CHIPS, -1, D))
    tie4 = start_reduce({"w_out_a": dw_out_a.reshape(N_CHIPS, -1, D), "w_in_a": dw_in4})
    grad_x, _, dg_attn0 = _matmul(dproj, win_t, "nn", "dn0", out_dtypes=(F32, BF16, F32), extras=(xs, dh1),
                                  row_extras=(vec(g_attn[0]) + tie4[:, :1],), epilogue=_rms_bwd_epilogue, row_accums=1,
                                  tm=1024, tn=1024, tk=tile_in)

    small_grads = {
        "g_attn": jnp.concatenate([dg_attn0, dg_attn1], axis=0), "g_mlp": jnp.concatenate([dg_mlp0, dg_mlp1], axis=0),
        "b_f": db_f.reshape(1, H), "gq_a": dgq_a[:, :HEAD_DIM], "gk_a": dgk_a[:, :HEAD_DIM], "g_kv": dg_kv.reshape(-1),
        "gk_b": dgk_b[0, :HEAD_DIM], "gq_b": dgq_b[:, :HEAD_DIM], "sinks": dsink[:, 0, 0].reshape(1, H), "rel_bias": d_rel_bias,
    }
    small_shapes = [given[n].shape for n in SMALL] + [(1,)]
    spack = _pack_small([small_grads[n] for n in SMALL] + [loss_part])
    gathering_small, token = _split_start(_plan_gather_small, [spack, lax.empty((8,) + spack.shape, F32)], 7, grad_x,
                                          "gather_small_start")
    sharing = []

    def land_scatter(names, started_scatter, after, behind):
        arrays = _split_wait(_plan_scatter_direct(len(names)), started_scatter, after, "scatter_wait_" + names[0])
        halves = [_sum_direct(p, l, place, "sum_direct_" + n)
                  for n, p, l in zip(names, arrays[:len(names)], arrays[len(names):])]
        return _split_start(_plan_share_halves(len(halves)), halves, len(halves), behind, "share_start_" + names[0])

    for names, started_scatter in scattering[:-1]:
        started_share, token = land_scatter(names, started_scatter, grad_x, token)
        sharing.append((names, started_share))
    own_small, others_small = _split_wait(_plan_gather_small, gathering_small, token, "gather_small_wait")
    small_sum = _sum_gathered_small(others_small, own_small, place, "sum_small")
    small_red = _unpack(small_sum.reshape(-1), small_shapes)
    reduced = {}
    for names, started_share in sharing:
        for n, r in zip(names, _split_wait(_plan_share_halves(len(names)), started_share, token, "share_wait_" + names[0])):
            reduced[n] = r.reshape(-1, r.shape[2])
    loss = small_red[-1][0]

    grads = dict(zip(SMALL, small_red))
    no_loss = [jnp.zeros((1,), F32)]
    sw = _pack_small([given[n] for n in SMALL] + no_loss)
    sm = _pack_small([given["m_" + n] for n in SMALL] + no_loss)
    sv = _pack_small([given["v_" + n] for n in SMALL] + no_loss)
    sd, sm2, sv2 = _adamw(sw, small_sum, sm, sv, "adamw_small", tr=sw.shape[0])
    delta = dict(zip(SMALL, _unpack(sd.reshape(-1), small_shapes)))
    new_m = dict(zip(SMALL, _unpack(sm2.reshape(-1), small_shapes)))
    new_v = dict(zip(SMALL, _unpack(sv2.reshape(-1), small_shapes)))
    def update_plain(n):
        w = given[n]
        two_d = (-1, w.shape[-1])
        d, m2, v2 = _adamw(w.reshape(two_d), reduced[n], given["m_" + n].reshape(two_d), given["v_" + n].reshape(two_d),
                           "adamw_" + n)
        grads[n] = reduced[n].reshape(w.shape)
        delta[n], new_m[n], new_v[n] = d.reshape(w.shape), m2.reshape(w.shape), v2.reshape(w.shape)

    for n in ("w_kv", "w_q_b", "w_out_b"):
        update_plain(n)
    for n in ("w_up", "w_down"):
        w = given[n]
        two_d = (-1, w.shape[-1])
        g, d, m2, v2 = _adamw_two_layers(w.reshape(two_d), reduced[n + "0"], reduced[n + "1"], given["m_" + n].reshape(two_d),
                                         given["v_" + n].reshape(two_d), "adamw_" + n)
        grads[n], delta[n], new_m[n], new_v[n] = g.reshape(w.shape), d.reshape(w.shape), m2.reshape(w.shape), v2.reshape(w.shape)
    names, started_scatter = scattering[-1]
    started_share, token = land_scatter(names, started_scatter, delta["w_down"], token)
    for n, r in zip(names, _split_wait(_plan_share_halves(len(names)), started_share, token, "share_wait_" + names[0])):
        reduced[n] = r.reshape(-1, r.shape[2])
    reduced["w_in_a"] = reduced["w_in_a"][:n_in_shard]
    update_plain("w_out_a")
    d, m2, v2 = _adamw(t_in(w_in_a), reduced["w_in_a"], t_in(m_w_in_a), t_in(v_w_in_a), "adamw_w_in_a")
    back = lambda a: jnp.swapaxes(a, 0, 1)[None]
    grads["w_in_a"], delta["w_in_a"], new_m["w_in_a"], new_v["w_in_a"] = back(reduced["w_in_a"]), back(d), back(m2), back(v2)

    order = ["g_attn", "g_mlp", "w_in_a", "b_f", "gq_a", "gk_a", "w_out_a", "g_kv", "w_kv", "gk_b", "w_q_b", "gq_b",
             "sinks", "rel_bias", "w_out_b", "w_up", "w_down"]
    return (loss, grad_x[None], *[grads[n] for n in order], *[delta[n] for n in order],
            *[new_m[n] for n in order], *[new_v[n] for n in order])
```
